```python
import math
import jax, jax.numpy as jnp
from jax import lax
import numpy as np

D_MODEL = 2048
BATCH = 8
SEQ = 4096
DEPTH = 2

HEAD_DIM = 128
ATTN_GROUPS = ((128, 1), (512, 4), (2048, 16))
N_ATTN_GROUPS = len(ATTN_GROUPS)
HEADS_PER_GROUP = 4
ATTN_QKV_WIDTH = 3 * N_ATTN_GROUPS * HEADS_PER_GROUP * HEAD_DIM
ATTN_OUT_WIDTH = HEADS_PER_GROUP * HEAD_DIM
SG_CHUNK = 128
SG_GROUPS = 8
SG_GROUP_DIM = 128
SG_WIDTH = SG_GROUPS * SG_GROUP_DIM
N_BRANCHES = 2
GATE_WIDTH = N_BRANCHES * D_MODEL
IN_WIDTH = ATTN_QKV_WIDTH + 2 * SG_WIDTH + GATE_WIDTH
D_FF = -(-8 * D_MODEL // (3 * 256)) * 256
PLE_DIM = 256
ROPE_THETA = 10000.0
NORM_EPS = 1e-6
NEG_INF = -1e30

kernel_name = "hybrid_dilated_attn_gmlp_gated_encoder"


def rms_norm(x, g):
    xf = x.astype(jnp.float32)
    y = xf * lax.rsqrt(jnp.mean(xf * xf, axis=-1, keepdims=True) + NORM_EPS)
    return (y * g.astype(jnp.float32)).astype(x.dtype)


def layer_norm(x, g, b):
    xf = x.astype(jnp.float32)
    mu = jnp.mean(xf, axis=-1, keepdims=True)
    xc = xf - mu
    y = xc * lax.rsqrt(jnp.mean(xc * xc, axis=-1, keepdims=True) + NORM_EPS)
    return (y * g.astype(jnp.float32) + b.astype(jnp.float32)).astype(x.dtype)


def rope_tables(seq):
    pos = jnp.arange(seq, dtype=jnp.float32)
    inv_freq = ROPE_THETA ** (-jnp.arange(0, HEAD_DIM, 2, dtype=jnp.float32) / HEAD_DIM)
    ang = pos[:, None] * inv_freq[None, :]
    return jnp.cos(ang), jnp.sin(ang)


def apply_rope(t, cos, sin):
    half = HEAD_DIM // 2
    tf = t.astype(jnp.float32)
    t1, t2 = tf[..., :half], tf[..., half:]
    c = cos[None, :, None, None, :]
    s = sin[None, :, None, None, :]
    return jnp.concatenate([t1 * c - t2 * s, t2 * c + t1 * s], axis=-1).astype(t.dtype)


def dilated_window_attention(q, k, v, window, dilation):
    B, S, H, hd = q.shape
    d = dilation
    radius = window // (2 * d)
    blk = radius
    L = S // d
    nb = -(-L // blk)
    Lp = nb * blk
    N = B * d

    def strided(t):
        return t.reshape(B, L, d, H, hd).transpose(0, 2, 1, 3, 4).reshape(N, L, H, hd)

    qs, ks, vs = strided(q), strided(k), strided(v)
    qb = jnp.pad(qs, ((0, 0), (0, Lp - L), (0, 0), (0, 0))).reshape(N, nb, blk, H, hd)

    def band(t):
        tp = jnp.pad(t, ((0, 0), (blk, Lp - L + blk), (0, 0), (0, 0))).reshape(N, nb + 2, blk, H, hd)
        return jnp.concatenate([tp[:, :-2], tp[:, 1:-1], tp[:, 2:]], axis=2)

    kb, vb = band(ks), band(vs)
    qi = jnp.arange(nb)[:, None] * blk + jnp.arange(blk)[None, :]
    ki = jnp.arange(nb)[:, None] * blk - blk + jnp.arange(3 * blk)[None, :]
    dist = ki[:, None, :] - qi[:, :, None]
    valid = (jnp.abs(dist) <= radius) & (ki[:, None, :] >= 0) & (ki[:, None, :] < L)

    s = jnp.einsum("nbqhd,nbkhd->nbhqk", qb, kb, preferred_element_type=jnp.float32)
    s = s * (hd ** -0.5)
    s = jnp.where(valid[None, :, None, :, :], s, NEG_INF)
    m = jnp.max(s, axis=-1, keepdims=True)
    e = jnp.exp(s - m)
    den = jnp.sum(e, axis=-1, keepdims=True)
    o = jnp.einsum("nbhqk,nbkhd->nbqhd", e / den, vb.astype(jnp.float32))
    lse = (m + jnp.log(den))[..., 0].transpose(0, 1, 3, 2)

    o = o.reshape(N, Lp, H, hd)[:, :L].reshape(B, d, L, H, hd).transpose(0, 2, 1, 3, 4).reshape(B, S, H, hd)
    lse = lse.reshape(N, Lp, H)[:, :L].reshape(B, d, L, H).transpose(0, 2, 1, 3).reshape(B, S, H)
    return o, lse


def spatial_gating(u, v, sg_w, sg_b, ln_g, ln_b):
    B, S, _ = v.shape
    u = jax.nn.gelu(u)
    v = layer_norm(jax.nn.gelu(v), ln_g, ln_b)
    vc = v.reshape(B, S // SG_CHUNK, SG_CHUNK, SG_GROUPS, SG_GROUP_DIM)
    mixed = jnp.einsum("gij,bcjgd->bcigd", sg_w, vc) + sg_b.T[None, None, :, :, None]
    return u * mixed.reshape(B, S, SG_WIDTH)


def _fwd_setup_inputs(seed: int = 0) -> dict:
    key = jax.random.key(seed)
    ks = jax.random.split(key, 20)
    f32 = jnp.float32

    def nrm(k, shape, fan_in):
        return jax.random.normal(k, shape, f32) * (fan_in ** -0.5)

    def gain(k, shape):
        return 1.0 + 0.02 * jax.random.normal(k, shape, f32)

    return {
        "x": jax.random.normal(ks[0], (BATCH, SEQ, D_MODEL), f32),
        "p": jax.random.normal(ks[1], (DEPTH, BATCH, SEQ, PLE_DIM), f32),
        "w_in": nrm(ks[2], (DEPTH, D_MODEL, IN_WIDTH), D_MODEL),
        "w_br_attn": nrm(ks[3], (DEPTH, ATTN_OUT_WIDTH, D_MODEL), ATTN_OUT_WIDTH),
        "w_br_sg": nrm(ks[4], (DEPTH, SG_WIDTH, D_MODEL), SG_WIDTH),
        "w_out": nrm(ks[5], (DEPTH, D_MODEL, D_MODEL), D_MODEL),
        "sg_w": nrm(ks[6], (DEPTH, SG_GROUPS, SG_CHUNK, SG_CHUNK), SG_CHUNK),
        "sg_b": 0.02 * jax.random.normal(ks[7], (DEPTH, SG_GROUPS, SG_CHUNK), f32),
        "sg_ln_g": gain(ks[8], (DEPTH, SG_WIDTH)),
        "sg_ln_b": 0.02 * jax.random.normal(ks[9], (DEPTH, SG_WIDTH), f32),
        "norm_mix": gain(ks[10], (DEPTH, D_MODEL)),
        "norm_ffn": gain(ks[11], (DEPTH, D_MODEL)),
        "norm_ple": gain(ks[12], (DEPTH, D_MODEL)),
        "norm_final": gain(ks[13], (D_MODEL,)),
        "w_ff_gate": nrm(ks[14], (DEPTH, D_MODEL, D_FF), D_MODEL),
        "w_ff_up": nrm(ks[15], (DEPTH, D_MODEL, D_FF), D_MODEL),
        "w_ff_down": nrm(ks[16], (DEPTH, D_FF, D_MODEL), D_FF),
        "w_ple_gate": nrm(ks[17], (DEPTH, D_MODEL, D_MODEL), D_MODEL),
        "w_ple": nrm(ks[18], (DEPTH, PLE_DIM, D_MODEL), PLE_DIM),
    }


def _fwd_reference(x, p, w_in, w_br_attn, w_br_sg, w_out, sg_w, sg_b, sg_ln_g, sg_ln_b,
              norm_mix, norm_ffn, norm_ple, norm_final, w_ff_gate, w_ff_up, w_ff_down,
              w_ple_gate, w_ple):
    B, S, D = x.shape
    cos, sin = rope_tables(S)
    o_sg0 = ATTN_QKV_WIDTH
    o_g0 = ATTN_QKV_WIDTH + 2 * SG_WIDTH
    for i in range(DEPTH):
        h = rms_norm(x, norm_mix[i])
        z = h @ w_in[i]
        qkv = z[..., :ATTN_QKV_WIDTH].reshape(B, S, 3, N_ATTN_GROUPS, HEADS_PER_GROUP, HEAD_DIM)
        q = apply_rope(qkv[:, :, 0], cos, sin)
        k = apply_rope(qkv[:, :, 1], cos, sin)
        v = qkv[:, :, 2]
        outs, lses = [], []
        for g, (win, dil) in enumerate(ATTN_GROUPS):
            o_g, l_g = dilated_window_attention(q[:, :, g], k[:, :, g], v[:, :, g], win, dil)
            outs.append(o_g)
            lses.append(l_g)
        w_grp = jax.nn.softmax(jnp.stack(lses, axis=0), axis=0)
        attn = jnp.einsum("gbsh,gbshd->bshd", w_grp, jnp.stack(outs, axis=0))
        y_attn = attn.reshape(B, S, ATTN_OUT_WIDTH).astype(x.dtype) @ w_br_attn[i]

        sg = spatial_gating(z[..., o_sg0:o_sg0 + SG_WIDTH], z[..., o_sg0 + SG_WIDTH:o_g0],
                            sg_w[i], sg_b[i], sg_ln_g[i], sg_ln_b[i])
        y_sg = sg @ w_br_sg[i]

        gates = jax.nn.sigmoid(z[..., o_g0:].astype(jnp.float32)).reshape(B, S, N_BRANCHES, D).astype(x.dtype)
        merged = gates[:, :, 0] * y_attn + gates[:, :, 1] * y_sg
        x = x + merged @ w_out[i]

        h2 = rms_norm(x, norm_ffn[i])
        x = x + (jax.nn.silu(h2 @ w_ff_gate[i]) * (h2 @ w_ff_up[i])) @ w_ff_down[i]

        gate_p = jax.nn.sigmoid((rms_norm(x, norm_ple[i]) @ w_ple_gate[i]).astype(jnp.float32)).astype(x.dtype)
        x = x + gate_p * (p[i].astype(x.dtype) @ w_ple[i])
    return rms_norm(x, norm_final)


import jax as _jax
import jax.numpy as _jnp

TWIN_FORMAT = 'train_step'
FWD_PARAMS = ['x', 'p', 'w_in', 'w_br_attn', 'w_br_sg', 'w_out', 'sg_w', 'sg_b', 'sg_ln_g', 'sg_ln_b', 'norm_mix', 'norm_ffn', 'norm_ple', 'norm_final', 'w_ff_gate', 'w_ff_up', 'w_ff_down', 'w_ple_gate', 'w_ple']
TWIN_WEIGHTS = ['w_in', 'w_br_attn', 'w_br_sg', 'w_out', 'sg_w', 'sg_b', 'sg_ln_g', 'sg_ln_b', 'norm_mix', 'norm_ffn', 'norm_ple', 'norm_final', 'w_ff_gate', 'w_ff_up', 'w_ff_down', 'w_ple_gate', 'w_ple']
TWIN_DIFF_INPUT = 'x'
TWIN_INPUTS = ['x', 'p', 'w_in', 'w_br_attn', 'w_br_sg', 'w_out', 'sg_w', 'sg_b', 'sg_ln_g', 'sg_ln_b', 'norm_mix', 'norm_ffn', 'norm_ple', 'norm_final', 'w_ff_gate', 'w_ff_up', 'w_ff_down', 'w_ple_gate', 'w_ple', 'loss_target', 'm_w_in', 'm_w_br_attn', 'm_w_br_sg', 'm_w_out', 'm_sg_w', 'm_sg_b', 'm_sg_ln_g', 'm_sg_ln_b', 'm_norm_mix', 'm_norm_ffn', 'm_norm_ple', 'm_norm_final', 'm_w_ff_gate', 'm_w_ff_up', 'm_w_ff_down', 'm_w_ple_gate', 'm_w_ple', 'v_w_in', 'v_w_br_attn', 'v_w_br_sg', 'v_w_out', 'v_sg_w', 'v_sg_b', 'v_sg_ln_g', 'v_sg_ln_b', 'v_norm_mix', 'v_norm_ffn', 'v_norm_ple', 'v_norm_final', 'v_w_ff_gate', 'v_w_ff_up', 'v_w_ff_down', 'v_w_ple_gate', 'v_w_ple']
TWIN_OUTPUTS = ['loss', 'grad_x', 'grad_w_in', 'grad_w_br_attn', 'grad_w_br_sg', 'grad_w_out', 'grad_sg_w', 'grad_sg_b', 'grad_sg_ln_g', 'grad_sg_ln_b', 'grad_norm_mix', 'grad_norm_ffn', 'grad_norm_ple', 'grad_norm_final', 'grad_w_ff_gate', 'grad_w_ff_up', 'grad_w_ff_down', 'grad_w_ple_gate', 'grad_w_ple', 'delta_w_in', 'delta_w_br_attn', 'delta_w_br_sg', 'delta_w_out', 'delta_sg_w', 'delta_sg_b', 'delta_sg_ln_g', 'delta_sg_ln_b', 'delta_norm_mix', 'delta_norm_ffn', 'delta_norm_ple', 'delta_norm_final', 'delta_w_ff_gate', 'delta_w_ff_up', 'delta_w_ff_down', 'delta_w_ple_gate', 'delta_w_ple', 'new_m_w_in', 'new_m_w_br_attn', 'new_m_w_br_sg', 'new_m_w_out', 'new_m_sg_w', 'new_m_sg_b', 'new_m_sg_ln_g', 'new_m_sg_ln_b', 'new_m_norm_mix', 'new_m_norm_ffn', 'new_m_norm_ple', 'new_m_norm_final', 'new_m_w_ff_gate', 'new_m_w_ff_up', 'new_m_w_ff_down', 'new_m_w_ple_gate', 'new_m_w_ple', 'new_v_w_in', 'new_v_w_br_attn', 'new_v_w_br_sg', 'new_v_w_out', 'new_v_sg_w', 'new_v_sg_b', 'new_v_sg_ln_g', 'new_v_sg_ln_b', 'new_v_norm_mix', 'new_v_norm_ffn', 'new_v_norm_ple', 'new_v_norm_final', 'new_v_w_ff_gate', 'new_v_w_ff_up', 'new_v_w_ff_down', 'new_v_w_ple_gate', 'new_v_w_ple']
TWIN_LEAF_KINDS = {'loss': 'loss', 'grad_x': 'grad_x', 'grad_w_in': 'grad_w', 'grad_w_br_attn': 'grad_w', 'grad_w_br_sg': 'grad_w', 'grad_w_out': 'grad_w', 'grad_sg_w': 'grad_w', 'grad_sg_b': 'grad_w', 'grad_sg_ln_g': 'grad_w', 'grad_sg_ln_b': 'grad_w', 'grad_norm_mix': 'grad_w', 'grad_norm_ffn': 'grad_w', 'grad_norm_ple': 'grad_w', 'grad_norm_final': 'grad_w', 'grad_w_ff_gate': 'grad_w', 'grad_w_ff_up': 'grad_w', 'grad_w_ff_down': 'grad_w', 'grad_w_ple_gate': 'grad_w', 'grad_w_ple': 'grad_w', 'delta_w_in': 'delta_w', 'delta_w_br_attn': 'delta_w', 'delta_w_br_sg': 'delta_w', 'delta_w_out': 'delta_w', 'delta_sg_w': 'delta_w', 'delta_sg_b': 'delta_w', 'delta_sg_ln_g': 'delta_w', 'delta_sg_ln_b': 'delta_w', 'delta_norm_mix': 'delta_w', 'delta_norm_ffn': 'delta_w', 'delta_norm_ple': 'delta_w', 'delta_norm_final': 'delta_w', 'delta_w_ff_gate': 'delta_w', 'delta_w_ff_up': 'delta_w', 'delta_w_ff_down': 'delta_w', 'delta_w_ple_gate': 'delta_w', 'delta_w_ple': 'delta_w', 'new_m_w_in': 'new_m', 'new_m_w_br_attn': 'new_m', 'new_m_w_br_sg': 'new_m', 'new_m_w_out': 'new_m', 'new_m_sg_w': 'new_m', 'new_m_sg_b': 'new_m', 'new_m_sg_ln_g': 'new_m', 'new_m_sg_ln_b': 'new_m', 'new_m_norm_mix': 'new_m', 'new_m_norm_ffn': 'new_m', 'new_m_norm_ple': 'new_m', 'new_m_norm_final': 'new_m', 'new_m_w_ff_gate': 'new_m', 'new_m_w_ff_up': 'new_m', 'new_m_w_ff_down': 'new_m', 'new_m_w_ple_gate': 'new_m', 'new_m_w_ple': 'new_m', 'new_v_w_in': 'new_v', 'new_v_w_br_attn': 'new_v', 'new_v_w_br_sg': 'new_v', 'new_v_w_out': 'new_v', 'new_v_sg_w': 'new_v', 'new_v_sg_b': 'new_v', 'new_v_sg_ln_g': 'new_v', 'new_v_sg_ln_b': 'new_v', 'new_v_norm_mix': 'new_v', 'new_v_norm_ffn': 'new_v', 'new_v_norm_ple': 'new_v', 'new_v_norm_final': 'new_v', 'new_v_w_ff_gate': 'new_v', 'new_v_w_ff_up': 'new_v', 'new_v_w_ff_down': 'new_v', 'new_v_w_ple_gate': 'new_v', 'new_v_w_ple': 'new_v'}


def _forward(args):
    return _fwd_reference(*[args[k] for k in FWD_PARAMS])


def _output_shape():
    out = _jax.eval_shape(lambda: _forward(_fwd_setup_inputs(0)))
    return out.shape, out.dtype

N_MICROBATCH = 1
ADAM_LR = 0.001
ADAM_B1 = 0.9
ADAM_B2 = 0.999
ADAM_EPS = 1e-08
ADAM_WD = 0.01
ADAM_STEP = 10
PER_EXAMPLE_BATCH_AXIS = {'x': 0, 'p': 1, 'loss_target': 0}
SHARED_INPUTS = []
_WEIGHT_DTYPES = {'w_in': _jnp.float32, 'w_br_attn': _jnp.float32, 'w_br_sg': _jnp.float32, 'w_out': _jnp.float32, 'sg_w': _jnp.float32, 'sg_b': _jnp.float32, 'sg_ln_g': _jnp.float32, 'sg_ln_b': _jnp.float32, 'norm_mix': _jnp.float32, 'norm_ffn': _jnp.float32, 'norm_ple': _jnp.float32, 'norm_final': _jnp.float32, 'w_ff_gate': _jnp.float32, 'w_ff_up': _jnp.float32, 'w_ff_down': _jnp.float32, 'w_ple_gate': _jnp.float32, 'w_ple': _jnp.float32}
MOMENT_SCALE = {'w_in': 2.030769e-02, 'w_br_attn': 5.253440e-03, 'w_br_sg': 2.853428e-02, 'w_out': 2.897262e-02, 'sg_w': 4.054432e-02, 'sg_b': 3.991700e-02, 'sg_ln_g': 4.164536e-02, 'sg_ln_b': 4.013875e-02, 'norm_mix': 4.661389e-02, 'norm_ffn': 5.520718e-02, 'norm_ple': 1.351341e-02, 'norm_final': 1.598191e+01, 'w_ff_gate': 2.417167e-02, 'w_ff_up': 2.339249e-02, 'w_ff_down': 3.880151e-02, 'w_ple_gate': 1.362112e-02, 'w_ple': 3.478193e-02}


def _to_microbatches(a, axis):
    t = _jnp.moveaxis(a, axis, 0)
    t = t.reshape((N_MICROBATCH, t.shape[0] // N_MICROBATCH) + t.shape[1:])
    return _jnp.moveaxis(t, 1, axis + 1)


def setup_inputs(seed: int = 0) -> dict:
    inp = _fwd_setup_inputs(seed)
    key = _jax.random.fold_in(_jax.random.key(seed), 7919)
    shape, _ = _output_shape()
    out = dict(inp)
    out["loss_target"] = _jax.random.normal(_jax.random.fold_in(key, 0), shape, _jnp.float32)
    for i, name in enumerate(TWIN_WEIGHTS):
        w = inp[name].astype(_jnp.float32)
        if MOMENT_SCALE is None:
            s = _jnp.sqrt(_jnp.mean(_jnp.square(w)) + 1e-30)
        else:
            s = MOMENT_SCALE[name]
        km, kv = _jax.random.split(_jax.random.fold_in(key, i + 1))
        out[name] = w
        out["m_" + name] = s * _jax.random.normal(km, w.shape, _jnp.float32)
        out["v_" + name] = (s * s) * _jax.random.uniform(kv, w.shape, _jnp.float32, 0.5, 1.5)
    if N_MICROBATCH > 1:
        for name, axis in PER_EXAMPLE_BATCH_AXIS.items():
            out[name] = _to_microbatches(out[name], axis)
    return {'x': out['x'], 'p': out['p'], 'w_in': out['w_in'], 'w_br_attn': out['w_br_attn'], 'w_br_sg': out['w_br_sg'], 'w_out': out['w_out'], 'sg_w': out['sg_w'], 'sg_b': out['sg_b'], 'sg_ln_g': out['sg_ln_g'], 'sg_ln_b': out['sg_ln_b'], 'norm_mix': out['norm_mix'], 'norm_ffn': out['norm_ffn'], 'norm_ple': out['norm_ple'], 'norm_final': out['norm_final'], 'w_ff_gate': out['w_ff_gate'], 'w_ff_up': out['w_ff_up'], 'w_ff_down': out['w_ff_down'], 'w_ple_gate': out['w_ple_gate'], 'w_ple': out['w_ple'], 'loss_target': out['loss_target'], 'm_w_in': out['m_w_in'], 'm_w_br_attn': out['m_w_br_attn'], 'm_w_br_sg': out['m_w_br_sg'], 'm_w_out': out['m_w_out'], 'm_sg_w': out['m_sg_w'], 'm_sg_b': out['m_sg_b'], 'm_sg_ln_g': out['m_sg_ln_g'], 'm_sg_ln_b': out['m_sg_ln_b'], 'm_norm_mix': out['m_norm_mix'], 'm_norm_ffn': out['m_norm_ffn'], 'm_norm_ple': out['m_norm_ple'], 'm_norm_final': out['m_norm_final'], 'm_w_ff_gate': out['m_w_ff_gate'], 'm_w_ff_up': out['m_w_ff_up'], 'm_w_ff_down': out['m_w_ff_down'], 'm_w_ple_gate': out['m_w_ple_gate'], 'm_w_ple': out['m_w_ple'], 'v_w_in': out['v_w_in'], 'v_w_br_attn': out['v_w_br_attn'], 'v_w_br_sg': out['v_w_br_sg'], 'v_w_out': out['v_w_out'], 'v_sg_w': out['v_sg_w'], 'v_sg_b': out['v_sg_b'], 'v_sg_ln_g': out['v_sg_ln_g'], 'v_sg_ln_b': out['v_sg_ln_b'], 'v_norm_mix': out['v_norm_mix'], 'v_norm_ffn': out['v_norm_ffn'], 'v_norm_ple': out['v_norm_ple'], 'v_norm_final': out['v_norm_final'], 'v_w_ff_gate': out['v_w_ff_gate'], 'v_w_ff_up': out['v_w_ff_up'], 'v_w_ff_down': out['v_w_ff_down'], 'v_w_ple_gate': out['v_w_ple_gate'], 'v_w_ple': out['v_w_ple']}


def _loss(weights, diff, rest, loss_target):
    with _jax.named_scope("forward"):
        args = {**rest, TWIN_DIFF_INPUT: diff, **{k: w.astype(_WEIGHT_DTYPES[k]) for k, w in weights.items()}}
        y = _forward(args)
    with _jax.named_scope("loss_head"):
        err = _jnp.square(y.astype(_jnp.float32) - loss_target)
        return 0.5 * _jnp.sum(_jnp.mean(err, axis=-1)) if err.ndim else 0.5 * err


def _adamw(w, g, m, v):
    m = ADAM_B1 * m + (1.0 - ADAM_B1) * g
    v = ADAM_B2 * v + (1.0 - ADAM_B2) * _jnp.square(g)
    m_hat = m / (1.0 - ADAM_B1 ** ADAM_STEP)
    v_hat = v / (1.0 - ADAM_B2 ** ADAM_STEP)
    delta = -ADAM_LR * (m_hat / (_jnp.sqrt(v_hat) + ADAM_EPS) + ADAM_WD * w)
    return delta, m, v


def reference(x, p, w_in, w_br_attn, w_br_sg, w_out, sg_w, sg_b, sg_ln_g, sg_ln_b, norm_mix, norm_ffn, norm_ple, norm_final, w_ff_gate, w_ff_up, w_ff_down, w_ple_gate, w_ple, loss_target, m_w_in, m_w_br_attn, m_w_br_sg, m_w_out, m_sg_w, m_sg_b, m_sg_ln_g, m_sg_ln_b, m_norm_mix, m_norm_ffn, m_norm_ple, m_norm_final, m_w_ff_gate, m_w_ff_up, m_w_ff_down, m_w_ple_gate, m_w_ple, v_w_in, v_w_br_attn, v_w_br_sg, v_w_out, v_sg_w, v_sg_b, v_sg_ln_g, v_sg_ln_b, v_norm_mix, v_norm_ffn, v_norm_ple, v_norm_final, v_w_ff_gate, v_w_ff_up, v_w_ff_down, v_w_ple_gate, v_w_ple):
    given = dict(x=x, p=p, w_in=w_in, w_br_attn=w_br_attn, w_br_sg=w_br_sg, w_out=w_out, sg_w=sg_w, sg_b=sg_b, sg_ln_g=sg_ln_g, sg_ln_b=sg_ln_b, norm_mix=norm_mix, norm_ffn=norm_ffn, norm_ple=norm_ple, norm_final=norm_final, w_ff_gate=w_ff_gate, w_ff_up=w_ff_up, w_ff_down=w_ff_down, w_ple_gate=w_ple_gate, w_ple=w_ple, loss_target=loss_target, m_w_in=m_w_in, m_w_br_attn=m_w_br_attn, m_w_br_sg=m_w_br_sg, m_w_out=m_w_out, m_sg_w=m_sg_w, m_sg_b=m_sg_b, m_sg_ln_g=m_sg_ln_g, m_sg_ln_b=m_sg_ln_b, m_norm_mix=m_norm_mix, m_norm_ffn=m_norm_ffn, m_norm_ple=m_norm_ple, m_norm_final=m_norm_final, m_w_ff_gate=m_w_ff_gate, m_w_ff_up=m_w_ff_up, m_w_ff_down=m_w_ff_down, m_w_ple_gate=m_w_ple_gate, m_w_ple=m_w_ple, v_w_in=v_w_in, v_w_br_attn=v_w_br_attn, v_w_br_sg=v_w_br_sg, v_w_out=v_w_out, v_sg_w=v_sg_w, v_sg_b=v_sg_b, v_sg_ln_g=v_sg_ln_g, v_sg_ln_b=v_sg_ln_b, v_norm_mix=v_norm_mix, v_norm_ffn=v_norm_ffn, v_norm_ple=v_norm_ple, v_norm_final=v_norm_final, v_w_ff_gate=v_w_ff_gate, v_w_ff_up=v_w_ff_up, v_w_ff_down=v_w_ff_down, v_w_ple_gate=v_w_ple_gate, v_w_ple=v_w_ple)
    weights = {n: given[n] for n in TWIN_WEIGHTS}
    shared = {n: given[n] for n in SHARED_INPUTS}
    per_example = {n: given[n] for n in ['x', 'p']}
    grad_fn = _jax.value_and_grad(_loss, argnums=(0, 1))

    def one_microbatch(ex, loss_target):
        ex = dict(ex)
        diff = ex.pop(TWIN_DIFF_INPUT)
        return grad_fn(weights, diff, {**shared, **ex}, loss_target)

    if N_MICROBATCH == 1:
        loss, (grad_w, grad_x) = one_microbatch(per_example, given["loss_target"])
    else:
        def body(carry, xs):
            loss_sum, grad_sum = carry
            l_k, (gw_k, gx_k) = one_microbatch(xs[0], xs[1])
            with _jax.named_scope("update"):
                return (loss_sum + l_k, _jax.tree.map(_jnp.add, grad_sum, gw_k)), gx_k

        init = (_jnp.zeros((), _jnp.float32), _jax.tree.map(_jnp.zeros_like, weights))
        (loss, grad_w), grad_x = _jax.lax.scan(body, init, (per_example, given["loss_target"]))
    with _jax.named_scope("update"):
        delta_w, new_m, new_v = {}, {}, {}
        for n in TWIN_WEIGHTS:
            delta_w[n], new_m[n], new_v[n] = _adamw(weights[n], grad_w[n], given["m_" + n], given["v_" + n])
    return (loss, grad_x, *[grad_w[n] for n in TWIN_WEIGHTS], *[delta_w[n] for n in TWIN_WEIGHTS],
            *[new_m[n] for n in TWIN_WEIGHTS], *[new_v[n] for n in TWIN_WEIGHTS])
```

```python
import functools
import math

import jax
import jax.numpy as jnp
from jax import lax
from jax.experimental import pallas as pl
from jax.experimental.pallas import tpu as pltpu

f32 = jnp.float32
bf16 = jnp.bfloat16

HEAD_DIM = 128
N_GROUPS = 3
HEADS = 4
DILATIONS = (1, 4, 16)
RADIUS = 64
BLK = 128
QKV_W = 3 * N_GROUPS * HEADS * HEAD_DIM
ATTN_W = HEADS * HEAD_DIM
SG_CHUNK = 128
SG_GROUPS = 8
SG_W = SG_GROUPS * 128
ROPE_THETA = 10000.0
EPS = 1e-6
NEG = -1e30
N_DEV = 8
LANES = 128

ADAM_LR = 0.001
ADAM_B1 = 0.9
ADAM_B2 = 0.999
ADAM_EPS = 1e-08
ADAM_WD = 0.01
ADAM_STEP = 10

VMEM_LIMIT = 52 * 1024 * 1024
MESH = pl.DeviceIdType.MESH

NN = (((1,), (0,)), ((), ()))
NT = (((1,), (1,)), ((), ()))
TN = (((0,), (0,)), ((), ()))
_DN = {"nn": NN, "nt": NT, "tn": TN}


def _cp(*sem):
    return pltpu.CompilerParams(dimension_semantics=sem, vmem_limit_bytes=VMEM_LIMIT)


def _tile(n, pref, unit=128):
    if n <= pref:
        return n
    t = (pref // unit) * unit
    while t >= unit:
        if n % t == 0:
            return t
        t -= unit
    return n


def _sigmoid(x):
    return 1.0 / (1.0 + jnp.exp(-x))


_GC = math.sqrt(2.0 / math.pi)
_GA = 0.044715


def _gelu(x):
    return 0.5 * x * (1.0 + jnp.tanh(_GC * (x + _GA * x * x * x)))


def _gelu_grad(x):
    t = jnp.tanh(_GC * (x + _GA * x * x * x))
    return 0.5 * (1.0 + t) + 0.5 * x * (1.0 - t * t) * _GC * (1.0 + 3.0 * _GA * x * x)


def _matmul(name, prods, M, N, tm, tn, nk, outs, epilogue, extras=(), n_acc=1):
    in_specs, operands, metas = [], [], []
    for a, b, mode, acc in prods:
        if mode == "tn":
            tk = a.shape[0] // nk
            in_specs += [pl.BlockSpec((tk, tm), lambda i, j, k: (k, i)), pl.BlockSpec((tk, tn), lambda i, j, k: (k, j))]
        elif mode == "nt":
            tk = a.shape[1] // nk
            in_specs += [pl.BlockSpec((tm, tk), lambda i, j, k: (i, k)), pl.BlockSpec((tn, tk), lambda i, j, k: (j, k))]
        else:
            tk = a.shape[1] // nk
            in_specs += [pl.BlockSpec((tm, tk), lambda i, j, k: (i, k)), pl.BlockSpec((tk, tn), lambda i, j, k: (k, j))]
        operands += [a, b]
        metas.append((mode, acc))
    for arr, bshape, imap in extras:
        in_specs.append(pl.BlockSpec(bshape, functools.partial(lambda i, j, k, f: f(i, j), f=imap)))
        operands.append(arr)
    out_specs = [pl.BlockSpec(bs, functools.partial(lambda i, j, k, f: f(i, j), f=imap)) for _, _, bs, imap in outs]
    out_shape = [jax.ShapeDtypeStruct(s, d) for s, d, _, _ in outs]
    n_prod, n_ext, n_out = len(prods), len(extras), len(outs)

    def body(*refs):
        in_refs = refs[: 2 * n_prod]
        ex_refs = refs[2 * n_prod : 2 * n_prod + n_ext]
        out_refs = refs[2 * n_prod + n_ext : 2 * n_prod + n_ext + n_out]
        acc_refs = refs[2 * n_prod + n_ext + n_out :]

        def partials():
            res = [None] * n_acc
            for idx, (mode, acc) in enumerate(metas):
                a = in_refs[2 * idx][...].astype(bf16)
                b = in_refs[2 * idx + 1][...].astype(bf16)
                d = lax.dot_general(a, b, _DN[mode], preferred_element_type=f32)
                res[acc] = d if res[acc] is None else res[acc] + d
            return res

        def finish(accs):
            vals = epilogue(accs, [r[...] for r in ex_refs])
            for r, v in zip(out_refs, vals):
                r[...] = v.astype(r.dtype)

        if nk == 1:
            finish(partials())
        else:
            k = pl.program_id(2)

            @pl.when(k == 0)
            def _():
                for r in acc_refs:
                    r[...] = jnp.zeros_like(r)

            for r, d in zip(acc_refs, partials()):
                r[...] += d

            @pl.when(k == nk - 1)
            def _():
                finish([r[...] for r in acc_refs])

    scratch = [pltpu.VMEM((tm, tn), f32) for _ in range(n_acc)] if nk > 1 else []
    res = pl.pallas_call(
        body, name=name, out_shape=out_shape, grid=(M // tm, N // tn, nk),
        in_specs=in_specs, out_specs=out_specs, scratch_shapes=scratch,
        compiler_params=_cp("parallel", "parallel", "arbitrary"),
    )(*operands)
    return res


def _ident(accs, ex):
    return [accs[0]]


def _mm_simple(name, a, b, mode, out_dtype, tm_pref=1024, tn_pref=1024, tk_pref=1024):
    if mode == "tn":
        K, M = a.shape
        N = b.shape[1]
    elif mode == "nt":
        M, K = a.shape
        N = b.shape[0]
    else:
        M, K = a.shape
        N = b.shape[1]
    tm, tn, tk = _tile(M, tm_pref), _tile(N, tn_pref), _tile(K, tk_pref)
    return _matmul(name, [(a, b, mode, 0)], M, N, tm, tn, K // tk,
                   [((M, N), out_dtype, (tm, tn), lambda i, j: (i, j))], _ident)[0]


def _norm_fwd(name, x, g):
    S, D = x.shape
    tm = _tile(S, 512, 8)

    def body(x_ref, g_ref, h_ref):
        xv = x_ref[...]
        r = lax.rsqrt(jnp.mean(xv * xv, axis=-1, keepdims=True) + EPS)
        h_ref[...] = (xv * r * g_ref[...]).astype(bf16)

    return pl.pallas_call(
        body, name=name, out_shape=jax.ShapeDtypeStruct((S, D), bf16), grid=(S // tm,),
        in_specs=[pl.BlockSpec((tm, D), lambda i: (i, 0)), pl.BlockSpec((1, D), lambda i: (0, 0))],
        out_specs=pl.BlockSpec((tm, D), lambda i: (i, 0)), compiler_params=_cp("parallel"),
    )(x, g.reshape(1, D))


def _norm_bwd(name, dh, x, g, dx_in):
    S, D = x.shape
    tm = _tile(S, 256, 8)

    def body(dh_ref, x_ref, g_ref, dxi_ref, dx_ref, dg_ref):
        i = pl.program_id(0)
        xv = x_ref[...]
        r = lax.rsqrt(jnp.mean(xv * xv, axis=-1, keepdims=True) + EPS)
        xh = xv * r
        dhv = dh_ref[...].astype(f32)
        dxh = dhv * g_ref[...]
        dx_ref[...] = dxi_ref[...] + r * (dxh - xh * jnp.mean(dxh * xh, axis=-1, keepdims=True))

        @pl.when(i == 0)
        def _():
            dg_ref[...] = jnp.zeros_like(dg_ref)

        dg_ref[...] += jnp.sum(dhv * xh, axis=0, keepdims=True)

    return pl.pallas_call(
        body, name=name, out_shape=[jax.ShapeDtypeStruct((S, D), f32), jax.ShapeDtypeStruct((1, D), f32)], grid=(S // tm,),
        in_specs=[pl.BlockSpec((tm, D), lambda i: (i, 0)), pl.BlockSpec((tm, D), lambda i: (i, 0)),
                  pl.BlockSpec((1, D), lambda i: (0, 0)), pl.BlockSpec((tm, D), lambda i: (i, 0))],
        out_specs=[pl.BlockSpec((tm, D), lambda i: (i, 0)), pl.BlockSpec((1, D), lambda i: (0, 0))],
        compiler_params=_cp("arbitrary"),
    )(dh, x, g.reshape(1, D), dx_in)


def _loss_head(x, g, t):
    S, D = x.shape
    tm = _tile(S, 256, 8)

    def body(x_ref, g_ref, t_ref, dx_ref, dg_ref, loss_ref):
        i = pl.program_id(0)
        xv = x_ref[...]
        r = lax.rsqrt(jnp.mean(xv * xv, axis=-1, keepdims=True) + EPS)
        xh = xv * r
        gv = g_ref[...]
        err = xh * gv - t_ref[...]
        dy = err * (1.0 / D)
        dxh = dy * gv
        dx_ref[...] = r * (dxh - xh * jnp.mean(dxh * xh, axis=-1, keepdims=True))

        @pl.when(i == 0)
        def _():
            dg_ref[...] = jnp.zeros_like(dg_ref)
            loss_ref[...] = jnp.zeros_like(loss_ref)

        dg_ref[...] += jnp.sum(dy * xh, axis=0, keepdims=True)
        row = jnp.sum(err * err, axis=-1, keepdims=True) * (0.5 / D)
        loss_ref[...] += jnp.broadcast_to(jnp.sum(row, axis=0, keepdims=True), loss_ref.shape)

    return pl.pallas_call(
        body, name="loss_head",
        out_shape=[jax.ShapeDtypeStruct((S, D), f32), jax.ShapeDtypeStruct((1, D), f32), jax.ShapeDtypeStruct((8, LANES), f32)],
        grid=(S // tm,),
        in_specs=[pl.BlockSpec((tm, D), lambda i: (i, 0)), pl.BlockSpec((1, D), lambda i: (0, 0)), pl.BlockSpec((tm, D), lambda i: (i, 0))],
        out_specs=[pl.BlockSpec((tm, D), lambda i: (i, 0)), pl.BlockSpec((1, D), lambda i: (0, 0)), pl.BlockSpec((8, LANES), lambda i: (0, 0))],
        compiler_params=_cp("arbitrary"),
    )(x, g.reshape(1, D), t)


def _perm(t, d):
    if d == 1:
        return t
    S, C = t.shape
    return t.reshape(S // d, d, C).transpose(1, 0, 2).reshape(S, C)


def _unperm(t, d):
    if d == 1:
        return t
    S, C = t.shape
    return t.reshape(d, S // d, C).transpose(1, 0, 2).reshape(S, C)


def _rope_tables(S):
    half = HEAD_DIM // 2
    pos = jnp.arange(S, dtype=f32)
    inv_freq = ROPE_THETA ** (-jnp.arange(0, HEAD_DIM, 2, dtype=f32) / HEAD_DIM)
    ang = pos[:, None] * inv_freq[None, :]
    c, s = jnp.cos(ang), jnp.sin(ang)
    cos2 = jnp.concatenate([c, c], axis=-1)
    sin2 = jnp.concatenate([-s, s], axis=-1)
    assert cos2.shape == (S, 2 * half)
    return (jnp.stack([_perm(cos2, d) for d in DILATIONS]), jnp.stack([_perm(sin2, d) for d in DILATIONS]))


def _rope(t, c, s):
    return t * c + pltpu.roll(t, HEAD_DIM // 2, 1) * s


def _rope_bwd(dt, c, s):
    return dt * c - pltpu.roll(dt, HEAD_DIM // 2, 1) * s


def _band_bounds(i, nblk):
    g = pl.program_id(0)
    lb = jnp.right_shift(jnp.int32(nblk), 2 * g)
    pos = lax.rem(i, lb)
    lo = jnp.where(pos == 0, BLK, 0)
    hi = jnp.where(pos == lb - 1, 2 * BLK, 3 * BLK)
    return lo, hi


def _band_specs(width, nblk):
    prev = pl.BlockSpec((None, BLK, width), lambda g, i: (g, jnp.maximum(i - 1, 0), 0))
    cur = pl.BlockSpec((None, BLK, width), lambda g, i: (g, i, 0))
    nxt = pl.BlockSpec((None, BLK, width), lambda g, i: (g, jnp.minimum(i + 1, nblk - 1), 0))
    return [prev, cur, nxt]


_SCALE = HEAD_DIM ** -0.5


def _attn_fwd(q, k, v, cos, sin):
    _, S, W = q.shape
    nblk = S // BLK

    def body(q_ref, kp, kc, kn, vp, vc, vn, cq, sq, ckp, ckc, ckn, skp, skc, skn, o_ref, lse_ref):
        i = pl.program_id(1)
        lo, hi = _band_bounds(i, nblk)
        a = lax.broadcasted_iota(jnp.int32, (BLK, 3 * BLK), 0)
        b = lax.broadcasted_iota(jnp.int32, (BLK, 3 * BLK), 1)
        mask = (jnp.abs(b - BLK - a) <= RADIUS) & (b >= lo) & (b < hi)
        ck = jnp.concatenate([ckp[...], ckc[...], ckn[...]], axis=0)
        sk = jnp.concatenate([skp[...], skc[...], skn[...]], axis=0)
        for hh in range(HEADS):
            sl = slice(hh * HEAD_DIM, (hh + 1) * HEAD_DIM)
            qh = _rope(q_ref[:, sl].astype(f32), cq[...], sq[...]).astype(bf16)
            kh = jnp.concatenate([kp[:, sl], kc[:, sl], kn[:, sl]], axis=0).astype(f32)
            kh = _rope(kh, ck, sk).astype(bf16)
            vh = jnp.concatenate([vp[:, sl], vc[:, sl], vn[:, sl]], axis=0)
            s = lax.dot_general(qh, kh, NT, preferred_element_type=f32) * _SCALE
            s = jnp.where(mask, s, NEG)
            m = jnp.max(s, axis=-1, keepdims=True)
            e = jnp.exp(s - m)
            den = jnp.sum(e, axis=-1, keepdims=True)
            o = lax.dot_general(e.astype(bf16), vh, NN, preferred_element_type=f32) * (1.0 / den)
            o_ref[:, sl] = o.astype(bf16)
            lse_ref[:, sl] = jnp.broadcast_to(m + jnp.log(den), (BLK, HEAD_DIM))

    blk = pl.BlockSpec((None, BLK, W), lambda g, i: (g, i, 0))
    tab = pl.BlockSpec((None, BLK, HEAD_DIM), lambda g, i: (g, i, 0))
    return pl.pallas_call(
        body, name="attn_fwd",
        out_shape=[jax.ShapeDtypeStruct((N_GROUPS, S, W), bf16), jax.ShapeDtypeStruct((N_GROUPS, S, W), f32)],
        grid=(N_GROUPS, nblk),
        in_specs=[blk] + _band_specs(W, nblk) * 2 + [tab, tab] + _band_specs(HEAD_DIM, nblk) * 2,
        out_specs=[blk, blk], compiler_params=_cp("parallel", "parallel"),
    )(q, k, k, k, v, v, v, cos, sin, cos, cos, cos, sin, sin, sin)


def _attn_combine(o3, lse3):
    _, S, W = o3.shape
    tm = _tile(S, 512, 8)

    def body(o_ref, l_ref, attn_ref, lse_ref):
        l0, l1, l2 = l_ref[0], l_ref[1], l_ref[2]
        m = jnp.maximum(jnp.maximum(l0, l1), l2)
        w0, w1, w2 = jnp.exp(l0 - m), jnp.exp(l1 - m), jnp.exp(l2 - m)
        den = w0 + w1 + w2
        acc = w0 * o_ref[0].astype(f32) + w1 * o_ref[1].astype(f32) + w2 * o_ref[2].astype(f32)
        attn_ref[...] = (acc * (1.0 / den)).astype(bf16)
        lse_ref[...] = m + jnp.log(den)

    blk3 = pl.BlockSpec((N_GROUPS, tm, W), lambda i: (0, i, 0))
    blk = pl.BlockSpec((tm, W), lambda i: (i, 0))
    return pl.pallas_call(
        body, name="attn_combine", out_shape=[jax.ShapeDtypeStruct((S, W), bf16), jax.ShapeDtypeStruct((S, W), f32)],
        grid=(S // tm,), in_specs=[blk3, blk3], out_specs=[blk, blk], compiler_params=_cp("parallel"),
    )(o3, lse3)


def _attn_delta(attn, dattn):
    S, W = attn.shape
    tm = _tile(S, 512, 8)

    def body(a_ref, d_ref, o_ref):
        prod = a_ref[...].astype(f32) * d_ref[...].astype(f32)
        for hh in range(HEADS):
            sl = slice(hh * HEAD_DIM, (hh + 1) * HEAD_DIM)
            o_ref[:, sl] = jnp.broadcast_to(jnp.sum(prod[:, sl], axis=-1, keepdims=True), (tm, HEAD_DIM))

    blk = pl.BlockSpec((tm, W), lambda i: (i, 0))
    return pl.pallas_call(
        body, name="attn_delta", out_shape=jax.ShapeDtypeStruct((S, W), f32), grid=(S // tm,),
        in_specs=[blk, blk], out_specs=blk, compiler_params=_cp("parallel"),
    )(attn, dattn)


def _attn_bwd_dq(q, k, v, cos, sin, da, lse, dl):
    _, S, W = q.shape
    nblk = S // BLK

    def body(q_ref, kp, kc, kn, vp, vc, vn, cq, sq, ckp, ckc, ckn, skp, skc, skn, da_ref, l_ref, dl_ref, dq_ref):
        i = pl.program_id(1)
        lo, hi = _band_bounds(i, nblk)
        a = lax.broadcasted_iota(jnp.int32, (BLK, 3 * BLK), 0)
        b = lax.broadcasted_iota(jnp.int32, (BLK, 3 * BLK), 1)
        mask = (jnp.abs(b - BLK - a) <= RADIUS) & (b >= lo) & (b < hi)
        ck = jnp.concatenate([ckp[...], ckc[...], ckn[...]], axis=0)
        sk = jnp.concatenate([skp[...], skc[...], skn[...]], axis=0)
        for hh in range(HEADS):
            sl = slice(hh * HEAD_DIM, (hh + 1) * HEAD_DIM)
            qh = _rope(q_ref[:, sl].astype(f32), cq[...], sq[...]).astype(bf16)
            kh = jnp.concatenate([kp[:, sl], kc[:, sl], kn[:, sl]], axis=0).astype(f32)
            kh = _rope(kh, ck, sk).astype(bf16)
            vh = jnp.concatenate([vp[:, sl], vc[:, sl], vn[:, sl]], axis=0)
            s = lax.dot_general(qh, kh, NT, preferred_element_type=f32) * _SCALE
            lh = l_ref[:, sl]
            l3 = jnp.concatenate([lh, lh, lh], axis=1)
            p = jnp.exp(jnp.where(mask, s - l3, NEG))
            dp = lax.dot_general(da_ref[:, sl], vh, NT, preferred_element_type=f32)
            dh = dl_ref[:, sl]
            ds = p * (dp - jnp.concatenate([dh, dh, dh], axis=1))
            dqh = lax.dot_general(ds.astype(bf16), kh, NN, preferred_element_type=f32) * _SCALE
            dq_ref[:, sl] = _rope_bwd(dqh, cq[...], sq[...]).astype(bf16)

    blk = pl.BlockSpec((None, BLK, W), lambda g, i: (g, i, 0))
    tab = pl.BlockSpec((None, BLK, HEAD_DIM), lambda g, i: (g, i, 0))
    return pl.pallas_call(
        body, name="attn_bwd_dq", out_shape=jax.ShapeDtypeStruct((N_GROUPS, S, W), bf16), grid=(N_GROUPS, nblk),
        in_specs=[blk] + _band_specs(W, nblk) * 2 + [tab, tab] + _band_specs(HEAD_DIM, nblk) * 2 + [blk, blk, blk],
        out_specs=blk, compiler_params=_cp("parallel", "parallel"),
    )(q, k, k, k, v, v, v, cos, sin, cos, cos, cos, sin, sin, sin, da, lse, dl)


def _attn_bwd_dkv(q, k, v, cos, sin, da, lse, dl):
    _, S, W = q.shape
    nblk = S // BLK

    def body(k_ref, v_ref, ck, sk, qp, qc, qn, cqp, cqc, cqn, sqp, sqc, sqn, dap, dac, dan, lp, lc, ln, dlp, dlc, dln,
             dk_ref, dv_ref):
        j = pl.program_id(1)
        lo, hi = _band_bounds(j, nblk)
        a = lax.broadcasted_iota(jnp.int32, (3 * BLK, BLK), 0)
        b = lax.broadcasted_iota(jnp.int32, (3 * BLK, BLK), 1)
        mask = (jnp.abs(b - (a - BLK)) <= RADIUS) & (a >= lo) & (a < hi)
        cq = jnp.concatenate([cqp[...], cqc[...], cqn[...]], axis=0)
        sq = jnp.concatenate([sqp[...], sqc[...], sqn[...]], axis=0)
        for hh in range(HEADS):
            sl = slice(hh * HEAD_DIM, (hh + 1) * HEAD_DIM)
            kh = _rope(k_ref[:, sl].astype(f32), ck[...], sk[...]).astype(bf16)
            vh = v_ref[:, sl]
            qh = jnp.concatenate([qp[:, sl], qc[:, sl], qn[:, sl]], axis=0).astype(f32)
            qh = _rope(qh, cq, sq).astype(bf16)
            dah = jnp.concatenate([dap[:, sl], dac[:, sl], dan[:, sl]], axis=0)
            lh = jnp.concatenate([lp[:, sl], lc[:, sl], ln[:, sl]], axis=0)
            dlh = jnp.concatenate([dlp[:, sl], dlc[:, sl], dln[:, sl]], axis=0)
            s = lax.dot_general(qh, kh, NT, preferred_element_type=f32) * _SCALE
            p = jnp.exp(jnp.where(mask, s - lh, NEG))
            dv_ref[:, sl] = lax.dot_general(p.astype(bf16), dah, TN, preferred_element_type=f32).astype(bf16)
            dp = lax.dot_general(dah, vh, NT, preferred_element_type=f32)
            ds = p * (dp - dlh)
            dkh = lax.dot_general(ds.astype(bf16), qh, TN, preferred_element_type=f32) * _SCALE
            dk_ref[:, sl] = _rope_bwd(dkh, ck[...], sk[...]).astype(bf16)

    blk = pl.BlockSpec((None, BLK, W), lambda g, i: (g, i, 0))
    tab = pl.BlockSpec((None, BLK, HEAD_DIM), lambda g, i: (g, i, 0))
    bw, bt = _band_specs(W, nblk), _band_specs(HEAD_DIM, nblk)
    return pl.pallas_call(
        body, name="attn_bwd_dkv",
        out_shape=[jax.ShapeDtypeStruct((N_GROUPS, S, W), bf16), jax.ShapeDtypeStruct((N_GROUPS, S, W), bf16)],
        grid=(N_GROUPS, nblk),
        in_specs=[blk, blk, tab, tab] + bw + bt + bt + bw + bw + bw,
        out_specs=[blk, blk], compiler_params=_cp("parallel", "parallel"),
    )(k, v, cos, sin, q, q, q, cos, cos, cos, sin, sin, sin, da, da, da, lse, lse, lse, dl, dl, dl)


_SG_ROWS = 512


def _sg_z_specs(tm, half):
    o = QKV_W // half
    return [pl.BlockSpec((tm, half), functools.partial(lambda i, c: (i, c), c=o + n)) for n in range(4)]


def _sg_norm(v, lg, lb):
    gv = _gelu(v)
    mu = jnp.mean(gv, axis=-1, keepdims=True)
    xc = gv - mu
    rstd = lax.rsqrt(jnp.mean(xc * xc, axis=-1, keepdims=True) + EPS)
    xh = xc * rstd
    return xh, rstd, xh * lg + lb


def _sg_fwd(z, w, bb, lg, lb):
    S = z.shape[0]
    tm = _tile(S, _SG_ROWS, SG_CHUNK)
    half = SG_W // 2

    def body(u0, u1, v0, v1, w_ref, bb_ref, lg_ref, lb_ref, o_ref):
        u = jnp.concatenate([u0[...], u1[...]], axis=1).astype(f32)
        v = jnp.concatenate([v0[...], v1[...]], axis=1).astype(f32)
        gu = _gelu(u)
        _, _, vn = _sg_norm(v, lg_ref[...], lb_ref[...])
        vnb = vn.astype(bf16)
        for c in range(tm // SG_CHUNK):
            rs = slice(c * SG_CHUNK, (c + 1) * SG_CHUNK)
            for g in range(SG_GROUPS):
                cs = slice(g * 128, (g + 1) * 128)
                mixed = lax.dot_general(w_ref[g], vnb[rs, cs], NN, preferred_element_type=f32) + bb_ref[g]
                o_ref[rs, cs] = (gu[rs, cs] * mixed).astype(bf16)

    full3 = pl.BlockSpec((SG_GROUPS, 128, 128), lambda i: (0, 0, 0))
    vec = pl.BlockSpec((1, SG_W), lambda i: (0, 0))
    return pl.pallas_call(
        body, name="sg_fwd", out_shape=jax.ShapeDtypeStruct((S, SG_W), bf16), grid=(S // tm,),
        in_specs=_sg_z_specs(tm, half) + [full3, full3, vec, vec],
        out_specs=pl.BlockSpec((tm, SG_W), lambda i: (i, 0)), compiler_params=_cp("parallel"),
    )(z, z, z, z, w, bb, lg.reshape(1, SG_W), lb.reshape(1, SG_W))


def _sg_bwd(dsg, z, w, wt, bb, lg, lb):
    S = z.shape[0]
    tm = _tile(S, _SG_ROWS, SG_CHUNK)
    half = SG_W // 2

    def body(d_ref, u0, u1, v0, v1, w_ref, wt_ref, bb_ref, lg_ref, lb_ref, du_ref, dv_ref, dw_ref, db_ref, dlg_ref, dlb_ref, dvn_scr):
        i = pl.program_id(0)

        @pl.when(i == 0)
        def _():
            dw_ref[...] = jnp.zeros_like(dw_ref)
            db_ref[...] = jnp.zeros_like(db_ref)
            dlg_ref[...] = jnp.zeros_like(dlg_ref)
            dlb_ref[...] = jnp.zeros_like(dlb_ref)

        u = jnp.concatenate([u0[...], u1[...]], axis=1).astype(f32)
        v = jnp.concatenate([v0[...], v1[...]], axis=1).astype(f32)
        gu = _gelu(u)
        dgu = _gelu_grad(u)
        xh, rstd, vn = _sg_norm(v, lg_ref[...], lb_ref[...])
        vnb = vn.astype(bf16)
        dsg_v = d_ref[...].astype(f32)
        for g in range(SG_GROUPS):
            cs = slice(g * 128, (g + 1) * 128)
            dw_g = jnp.zeros((128, 128), f32)
            db_g = jnp.zeros((128, 1), f32)
            for c in range(tm // SG_CHUNK):
                rs = slice(c * SG_CHUNK, (c + 1) * SG_CHUNK)
                ds = dsg_v[rs, cs]
                mixed = lax.dot_general(w_ref[g], vnb[rs, cs], NN, preferred_element_type=f32) + bb_ref[g]
                du_ref[rs, cs] = (ds * mixed * dgu[rs, cs]).astype(bf16)
                dmix = ds * gu[rs, cs]
                dmb = dmix.astype(bf16)
                dw_g = dw_g + lax.dot_general(dmb, vnb[rs, cs], NT, preferred_element_type=f32)
                db_g = db_g + jnp.sum(dmix, axis=-1, keepdims=True)
                dvn_scr[rs, cs] = lax.dot_general(wt_ref[g], dmb, NN, preferred_element_type=f32)
            dw_ref[g] += dw_g
            db_ref[g] += jnp.broadcast_to(db_g, (128, 128))
        dvn = dvn_scr[...]
        dlg_ref[...] += jnp.sum(dvn * xh, axis=0, keepdims=True)
        dlb_ref[...] += jnp.sum(dvn, axis=0, keepdims=True)
        dxh = dvn * lg_ref[...]
        dgv = rstd * (dxh - jnp.mean(dxh, axis=-1, keepdims=True) - xh * jnp.mean(dxh * xh, axis=-1, keepdims=True))
        dv_ref[...] = (dgv * _gelu_grad(v)).astype(bf16)

    full3 = pl.BlockSpec((SG_GROUPS, 128, 128), lambda i: (0, 0, 0))
    vec = pl.BlockSpec((1, SG_W), lambda i: (0, 0))
    row = pl.BlockSpec((tm, SG_W), lambda i: (i, 0))
    return pl.pallas_call(
        body, name="sg_bwd",
        out_shape=[jax.ShapeDtypeStruct((S, SG_W), bf16), jax.ShapeDtypeStruct((S, SG_W), bf16),
                   jax.ShapeDtypeStruct((SG_GROUPS, 128, 128), f32), jax.ShapeDtypeStruct((SG_GROUPS, 128, 128), f32),
                   jax.ShapeDtypeStruct((1, SG_W), f32), jax.ShapeDtypeStruct((1, SG_W), f32)],
        grid=(S // tm,),
        in_specs=[row] + _sg_z_specs(tm, half) + [full3, full3, full3, vec, vec],
        out_specs=[row, row, full3, full3, vec, vec],
        scratch_shapes=[pltpu.VMEM((tm, SG_W), f32)], compiler_params=_cp("arbitrary"),
    )(dsg, z, z, z, z, w, wt, bb, lg.reshape(1, SG_W), lb.reshape(1, SG_W))


def _merge_fwd(z, ya, yb, D):
    S = z.shape[0]
    tm, tc = _tile(S, 512, 8), _tile(D, 512)
    o_a, o_b = (QKV_W + 2 * SG_W) // tc, (QKV_W + 2 * SG_W + D) // tc

    def body(ga_ref, gb_ref, ya_ref, yb_ref, o_ref):
        ga = _sigmoid(ga_ref[...].astype(f32))
        gb = _sigmoid(gb_ref[...].astype(f32))
        o_ref[...] = (ga * ya_ref[...].astype(f32) + gb * yb_ref[...].astype(f32)).astype(bf16)

    blk = pl.BlockSpec((tm, tc), lambda i, j: (i, j))
    return pl.pallas_call(
        body, name="merge_fwd", out_shape=jax.ShapeDtypeStruct((S, D), bf16), grid=(S // tm, D // tc),
        in_specs=[pl.BlockSpec((tm, tc), lambda i, j: (i, o_a + j)), pl.BlockSpec((tm, tc), lambda i, j: (i, o_b + j)), blk, blk],
        out_specs=blk, compiler_params=_cp("parallel", "parallel"),
    )(z, z, ya, yb)


def _ple_bwd_ew(dx, gp, e):
    S, D = dx.shape
    tm, tc = _tile(S, 512, 8), _tile(D, 1024)

    def body(dx_ref, gp_ref, e_ref, dgp_ref, de_ref):
        dxv = dx_ref[...]
        sg = _sigmoid(gp_ref[...].astype(f32))
        dgp_ref[...] = (dxv * e_ref[...].astype(f32) * sg * (1.0 - sg)).astype(bf16)
        de_ref[...] = (dxv * sg).astype(bf16)

    blk = pl.BlockSpec((tm, tc), lambda i, j: (i, j))
    return pl.pallas_call(
        body, name="ple_bwd_ew", out_shape=[jax.ShapeDtypeStruct((S, D), bf16)] * 2, grid=(S // tm, D // tc),
        in_specs=[blk, blk, blk], out_specs=[blk, blk], compiler_params=_cp("parallel", "parallel"),
    )(dx, gp, e)


def _adam_math(w, g, m, v):
    m = ADAM_B1 * m + (1.0 - ADAM_B1) * g
    v = ADAM_B2 * v + (1.0 - ADAM_B2) * (g * g)
    m_hat = m / (1.0 - ADAM_B1 ** ADAM_STEP)
    v_hat = v / (1.0 - ADAM_B2 ** ADAM_STEP)
    delta = -ADAM_LR * (m_hat / (jnp.sqrt(v_hat) + ADAM_EPS) + ADAM_WD * w)
    return delta, m, v


def _adamw(name, w, g, m, v):
    R, C = w.shape
    tr = _tile(R, 256, 8)

    def body(w_ref, g_ref, m_ref, v_ref, d_ref, nm_ref, nv_ref):
        d, nm, nv = _adam_math(w_ref[...], g_ref[...], m_ref[...], v_ref[...])
        d_ref[...] = d
        nm_ref[...] = nm
        nv_ref[...] = nv

    blk = pl.BlockSpec((tr, C), lambda i: (i, 0))
    return pl.pallas_call(
        body, name=name, out_shape=[jax.ShapeDtypeStruct((R, C), f32)] * 3, grid=(R // tr,),
        in_specs=[blk] * 4, out_specs=[blk] * 3, compiler_params=_cp("parallel"),
    )(w, g, m, v)


def _small_sum_adamw(gathered, w, m, v):
    _, R, _ = gathered.shape
    tr = _tile(R, 1024, SMALL_ROWS)

    def body(p_ref, w_ref, m_ref, v_ref, g_ref, d_ref, nm_ref, nv_ref):
        g = p_ref[0]
        for n in range(1, N_DEV):
            g = g + p_ref[n]
        d, nm, nv = _adam_math(w_ref[...], g, m_ref[...], v_ref[...])
        g_ref[...] = g
        d_ref[...] = d
        nm_ref[...] = nm
        nv_ref[...] = nv

    blk = pl.BlockSpec((tr, LANES), lambda i: (i, 0))
    return pl.pallas_call(
        body, name="small_sum_adamw", out_shape=[jax.ShapeDtypeStruct((R, LANES), f32)] * 4, grid=(R // tr,),
        in_specs=[pl.BlockSpec((N_DEV, tr, LANES), lambda i: (0, i, 0)), blk, blk, blk], out_specs=[blk] * 4,
        compiler_params=_cp("parallel"),
    )(gathered, w, m, v)


_ANY = pl.BlockSpec(memory_space=pl.ANY)


def _all_gather(name, shard, in_vmem=False):
    R, C = shard.shape

    def body(x_ref, out_ref, send_sems, recv_sems, local_sem):
        x, y, c = lax.axis_index("x"), lax.axis_index("y"), lax.axis_index("c")
        me, sibling = (x, y, c), (x, y, 1 - c)
        chips = [(1 - x, y), (x, 1 - y), (1 - x, 1 - y)]

        def rows(px, py, pc):
            return out_ref.at[4 * px + 2 * py + pc]

        def copy(k, block, to, src=None):
            return pltpu.make_async_remote_copy(
                src_ref=rows(*block) if src is None else src, dst_ref=rows(*block),
                send_sem=send_sems.at[k], recv_sem=recv_sems.at[k], device_id=to, device_id_type=MESH)

        mine = pltpu.make_async_copy(x_ref, rows(*me), local_sem)
        mine.start()
        first = [copy(0, me, sibling, src=x_ref)]
        first += [copy(1 + j, me, (*chip, c), src=x_ref) for j, chip in enumerate(chips)]
        for cp in first:
            cp.start()
        passed = [copy(4 + j, (*chip, c), sibling) for j, chip in enumerate(chips)]
        for j, chip in enumerate(chips):
            copy(1 + j, (*chip, c), me).wait_recv()
            passed[j].start()
        copy(0, sibling, me).wait_recv()
        for j, chip in enumerate(chips):
            copy(4 + j, (*chip, 1 - c), me).wait_recv()
        for cp in first + passed:
            cp.wait_send()
        mine.wait()

    space = pl.BlockSpec(memory_space=pltpu.VMEM) if in_vmem else _ANY
    return pl.pallas_call(
        body, name=name, out_shape=jax.ShapeDtypeStruct((N_DEV, R, C), shard.dtype),
        in_specs=[space], out_specs=space,
        scratch_shapes=[pltpu.SemaphoreType.DMA((7,)), pltpu.SemaphoreType.DMA((7,)), pltpu.SemaphoreType.DMA],
        compiler_params=pltpu.CompilerParams(has_side_effects=True, vmem_limit_bytes=VMEM_LIMIT),
    )(shard)


def _rs_sibling(g8):
    _, R, C = g8.shape

    def body(g_ref, recv_ref, send_sems, recv_sems):
        x, y, c = lax.axis_index("x"), lax.axis_index("y"), lax.axis_index("c")
        sibling = (x, y, 1 - c)
        copies = [pltpu.make_async_remote_copy(
            src_ref=g_ref.at[2 * q + (1 - c)], dst_ref=recv_ref.at[q], send_sem=send_sems.at[q], recv_sem=recv_sems.at[q],
            device_id=sibling, device_id_type=MESH) for q in range(4)]
        for cp in copies:
            cp.start()
        for cp in copies:
            cp.wait_recv()
        for cp in copies:
            cp.wait_send()

    return pl.pallas_call(
        body, name="rs_sibling", out_shape=jax.ShapeDtypeStruct((4, R, C), g8.dtype),
        in_specs=[_ANY], out_specs=_ANY,
        scratch_shapes=[pltpu.SemaphoreType.DMA((4,)), pltpu.SemaphoreType.DMA((4,))],
        compiler_params=pltpu.CompilerParams(has_side_effects=True, vmem_limit_bytes=VMEM_LIMIT),
    )(g8)


def _rs_chip_sum(g8, recv, c_idx):
    _, R, C = g8.shape
    tr = _tile(R, 2048, 16)
    g42 = g8.reshape(4, 2, R, C)

    def body(c_ref, a_ref, b_ref, o_ref):
        o_ref[...] = (a_ref[...].astype(f32) + b_ref[...].astype(f32)).astype(o_ref.dtype)

    return pl.pallas_call(
        body, name="rs_chip_sum", out_shape=jax.ShapeDtypeStruct((4, R, C), g8.dtype),
        grid_spec=pltpu.PrefetchScalarGridSpec(
            num_scalar_prefetch=1, grid=(4, R // tr),
            in_specs=[pl.BlockSpec((None, None, tr, C), lambda q, r, c_ref: (q, c_ref[0], r, 0)),
                      pl.BlockSpec((None, tr, C), lambda q, r, c_ref: (q, r, 0))],
            out_specs=pl.BlockSpec((None, tr, C), lambda q, r, c_ref: (q, r, 0))),
        compiler_params=_cp("parallel", "parallel"),
    )(c_idx, g42, recv)


def _rs_chips(p4):
    _, R, C = p4.shape

    def body(p_ref, recv_ref, send_sems, recv_sems):
        x, y, c = lax.axis_index("x"), lax.axis_index("y"), lax.axis_index("c")
        chips = [(1 - x, y), (x, 1 - y), (1 - x, 1 - y)]
        copies = [pltpu.make_async_remote_copy(
            src_ref=p_ref.at[2 * cx + cy], dst_ref=recv_ref.at[k], send_sem=send_sems.at[k], recv_sem=recv_sems.at[k],
            device_id=(cx, cy, c), device_id_type=MESH) for k, (cx, cy) in enumerate(chips)]
        for cp in copies:
            cp.start()
        for cp in copies:
            cp.wait_recv()
        for cp in copies:
            cp.wait_send()

    return pl.pallas_call(
        body, name="rs_chips", out_shape=jax.ShapeDtypeStruct((3, R, C), p4.dtype),
        in_specs=[_ANY], out_specs=_ANY,
        scratch_shapes=[pltpu.SemaphoreType.DMA((3,)), pltpu.SemaphoreType.DMA((3,))],
        compiler_params=pltpu.CompilerParams(has_side_effects=True, vmem_limit_bytes=VMEM_LIMIT),
    )(p4)


def _rs_final_sum(p4, recv, q_idx):
    _, R, C = p4.shape
    tr = _tile(R, 2048, 16)

    def body(q_ref, a_ref, b_ref, o_ref):
        o_ref[...] = ((a_ref[...].astype(f32) + b_ref[0].astype(f32)) + b_ref[1].astype(f32)) + b_ref[2].astype(f32)

    return pl.pallas_call(
        body, name="rs_final_sum", out_shape=jax.ShapeDtypeStruct((R, C), f32),
        grid_spec=pltpu.PrefetchScalarGridSpec(
            num_scalar_prefetch=1, grid=(R // tr,),
            in_specs=[pl.BlockSpec((None, tr, C), lambda r, q_ref: (q_ref[0], r, 0)),
                      pl.BlockSpec((3, tr, C), lambda r, q_ref: (0, r, 0))],
            out_specs=pl.BlockSpec((tr, C), lambda r, q_ref: (r, 0))),
        compiler_params=_cp("parallel"),
    )(q_idx, p4, recv)


_BIG = (("w_in", 1), ("w_br_attn", 1), ("w_br_sg", 1), ("w_out", 0), ("w_ff_gate", 1), ("w_ff_up", 1),
        ("w_ff_down", 0), ("w_ple_gate", 0), ("w_ple", 1))


PACK_ROWS = 2048


def _pad_rows(n):
    return -n % PACK_ROWS


def _pack_shards(shards):
    segs = [s.astype(bf16).reshape(-1, LANES) for s in shards]
    n = sum(s.shape[0] for s in segs)
    return jnp.concatenate(segs + [jnp.zeros((_pad_rows(n), LANES), bf16)], axis=0)


def _unpack_full(g8, shapes):
    out, off = [], 0
    for r, c, axis in shapes:
        n = r * c // LANES
        seg = g8[:, off:off + n].reshape(N_DEV, r, c)
        off += n
        out.append(seg.reshape(N_DEV * r, c) if axis == 0 else seg.transpose(1, 0, 2).reshape(r, N_DEV * c))
    return out


def _pack_grads(fulls, shapes):
    segs = []
    for full, (r, c, axis) in zip(fulls, shapes):
        seg = full.reshape(N_DEV, r, c) if axis == 0 else full.reshape(r, N_DEV, c).transpose(1, 0, 2)
        segs.append(seg.reshape(N_DEV, r * c // LANES, LANES))
    n = sum(s.shape[1] for s in segs)
    return jnp.concatenate(segs + [jnp.zeros((N_DEV, _pad_rows(n), LANES), bf16)], axis=1)


def _unpack_shards(flat, shapes):
    out, off = [], 0
    for r, c, _ in shapes:
        n = r * c // LANES
        out.append(flat[off:off + n].reshape(r, c))
        off += n
    return out


def _layer_fwd(x0, p_i, W, sm, tabs):
    S, D = x0.shape
    F = W["w_ff_gate"].shape[1]
    cos, sin = tabs
    tmm = _tile(S, 1024, 8)
    h1 = _norm_fwd("norm_fwd", x0, sm["norm_mix"])
    z = _mm_simple("mm_in", h1, W["w_in"], "nn", bf16, 2048, 512, 2048)
    IN = z.shape[1]

    def grp(base):
        return jnp.stack([_perm(z[:, base + g * ATTN_W: base + (g + 1) * ATTN_W], d) for g, d in enumerate(DILATIONS)])

    qg, kg, vg = grp(0), grp(N_GROUPS * ATTN_W), grp(2 * N_GROUPS * ATTN_W)
    o3, lse3 = _attn_fwd(qg, kg, vg, cos, sin)
    o3 = jnp.stack([_unperm(o3[g], d) for g, d in enumerate(DILATIONS)])
    lse3 = jnp.stack([_unperm(lse3[g], d) for g, d in enumerate(DILATIONS)])
    attn, lse = _attn_combine(o3, lse3)
    ya = _mm_simple("mm_br_attn", attn, W["w_br_attn"], "nn", bf16)
    sgw = sm["sg_w"].astype(bf16)
    bb = jnp.broadcast_to(sm["sg_b"][:, :, None], (SG_GROUPS, SG_CHUNK, 128))
    sg = _sg_fwd(z, sgw, bb, sm["sg_ln_g"], sm["sg_ln_b"])
    yb = _mm_simple("mm_br_sg", sg, W["w_br_sg"], "nn", bf16)
    merged = _merge_fwd(z, ya, yb, D)
    tn = _tile(D, 512)
    x1 = _matmul("mm_out", [(merged, W["w_out"], "nn", 0)], S, D, tmm, tn, 1,
                 [((S, D), f32, (tmm, tn), lambda i, j: (i, j))], lambda accs, ex: [ex[0] + accs[0]],
                 extras=[(x0, (tmm, tn), lambda i, j: (i, j))])[0]
    h2 = _norm_fwd("norm_fwd", x1, sm["norm_ffn"])
    tf = _tile(F, 512)

    def ffn_ep(accs, ex):
        a, b = accs
        return [a, b, a * _sigmoid(a) * b]

    a, b, f = _matmul("mm_ffn_in", [(h2, W["w_ff_gate"], "nn", 0), (h2, W["w_ff_up"], "nn", 1)], S, F, tmm, tf, 1,
                      [((S, F), bf16, (tmm, tf), lambda i, j: (i, j))] * 3, ffn_ep, n_acc=2)
    nk = F // _tile(F, 512)
    x2 = _matmul("mm_ffn_out", [(f, W["w_ff_down"], "nn", 0)], S, D, tmm, tn, nk,
                 [((S, D), f32, (tmm, tn), lambda i, j: (i, j))], lambda accs, ex: [ex[0] + accs[0]],
                 extras=[(x1, (tmm, tn), lambda i, j: (i, j))])[0]
    h3 = _norm_fwd("norm_fwd", x2, sm["norm_ple"])

    def ple_ep(accs, ex):
        gp, e = accs
        return [ex[0] + _sigmoid(gp) * e, gp, e]

    x3, gp, e = _matmul("mm_ple", [(h3, W["w_ple_gate"], "nn", 0), (p_i, W["w_ple"], "nn", 1)], S, D, tmm, tn, 1,
                        [((S, D), f32, (tmm, tn), lambda i, j: (i, j))] + [((S, D), bf16, (tmm, tn), lambda i, j: (i, j))] * 2,
                        ple_ep, extras=[(x2, (tmm, tn), lambda i, j: (i, j))], n_acc=2)
    saved = dict(x0=x0, h1=h1, z=z, qg=qg, kg=kg, vg=vg, attn=attn, lse=lse, ya=ya, yb=yb, sg=sg, merged=merged, x1=x1,
                 h2=h2, a=a, b=b, f=f, x2=x2, h3=h3, gp=gp, e=e, sgw=sgw, bb=bb, IN=IN)
    return x3, saved


def _layer_bwd(dx3, p_i, W, sm, tabs, sv):
    S, D = dx3.shape
    F = W["w_ff_gate"].shape[1]
    cos, sin = tabs
    tmm = _tile(S, 1024, 8)
    tn = _tile(D, 512)
    dgp, de = _ple_bwd_ew(dx3, sv["gp"], sv["e"])
    d_w_ple = _mm_simple("mm_dw_ple", p_i, de, "tn", bf16)
    d_w_pg = _mm_simple("mm_dw_dd", sv["h3"], dgp, "tn", bf16)
    dh3 = _mm_simple("mm_dh_dd", dgp, W["w_ple_gate"], "nt", bf16, 1024, 1024, 2048)
    dx2, dg_ple = _norm_bwd("norm_bwd", dh3, sv["x2"], sm["norm_ple"], dx3)
    tf = _tile(F, 512)

    def ffn_bwd_ep(accs, ex):
        df = accs[0]
        a, b = ex[0].astype(f32), ex[1].astype(f32)
        sg = _sigmoid(a)
        return [df * b * sg * (1.0 + a * (1.0 - sg)), df * a * sg]

    da, db = _matmul("mm_dffn", [(dx2, W["w_ff_down"], "nt", 0)], S, F, tmm, tf, 1,
                     [((S, F), bf16, (tmm, tf), lambda i, j: (i, j))] * 2, ffn_bwd_ep,
                     extras=[(sv["a"], (tmm, tf), lambda i, j: (i, j)), (sv["b"], (tmm, tf), lambda i, j: (i, j))])
    d_w_down = _mm_simple("mm_dw_fd", sv["f"], dx2, "tn", bf16)
    d_w_gate = _mm_simple("mm_dw_df", sv["h2"], da, "tn", bf16)
    d_w_up = _mm_simple("mm_dw_df", sv["h2"], db, "tn", bf16)
    nk = F // _tile(F, 512)
    dh2 = _matmul("mm_dh_ffn", [(da, W["w_ff_gate"], "nt", 0), (db, W["w_ff_up"], "nt", 0)], S, D, tmm, _tile(D, 1024), nk,
                  [((S, D), bf16, (tmm, _tile(D, 1024)), lambda i, j: (i, j))], _ident)[0]
    dx1, dg_ffn = _norm_bwd("norm_bwd", dh2, sv["x1"], sm["norm_ffn"], dx2)
    z = sv["z"]
    o_a, o_b = (QKV_W + 2 * SG_W) // tn, (QKV_W + 2 * SG_W + D) // tn

    def merge_bwd_ep(accs, ex):
        dm = accs[0]
        ga, gb = _sigmoid(ex[0].astype(f32)), _sigmoid(ex[1].astype(f32))
        ya, yb = ex[2].astype(f32), ex[3].astype(f32)
        return [dm * ya * ga * (1.0 - ga), dm * yb * gb * (1.0 - gb), dm * ga, dm * gb]

    dga, dgb, dya, dyb = _matmul(
        "mm_dmerge", [(dx1, W["w_out"], "nt", 0)], S, D, tmm, tn, 1,
        [((S, D), bf16, (tmm, tn), lambda i, j: (i, j))] * 4, merge_bwd_ep,
        extras=[(z, (tmm, tn), lambda i, j: (i, o_a + j)), (z, (tmm, tn), lambda i, j: (i, o_b + j)),
                (sv["ya"], (tmm, tn), lambda i, j: (i, j)), (sv["yb"], (tmm, tn), lambda i, j: (i, j))])
    d_w_out = _mm_simple("mm_dw_dd", sv["merged"], dx1, "tn", bf16)
    dsg = _mm_simple("mm_dsg", dyb, W["w_br_sg"], "nt", bf16, 1024, 1024, 2048)
    d_w_bsg = _mm_simple("mm_dw_bsg", sv["sg"], dyb, "tn", bf16)
    dattn = _mm_simple("mm_dattn", dya, W["w_br_attn"], "nt", bf16, 1024, 1024, 2048)
    d_w_battn = _mm_simple("mm_dw_battn", sv["attn"], dya, "tn", bf16)
    sgwt = jnp.swapaxes(sm["sg_w"], 1, 2).astype(bf16)
    du, dv_sg, d_sgw, d_sgb, d_lg, d_lb = _sg_bwd(dsg, z, sv["sgw"], sgwt, sv["bb"], sm["sg_ln_g"], sm["sg_ln_b"])
    dl = _attn_delta(sv["attn"], dattn)

    def grp(t):
        return jnp.stack([_perm(t, d) for d in DILATIONS])

    dag, lg_, dlg_ = grp(dattn), grp(sv["lse"]), grp(dl)
    dqg = _attn_bwd_dq(sv["qg"], sv["kg"], sv["vg"], cos, sin, dag, lg_, dlg_)
    dkg, dvg = _attn_bwd_dkv(sv["qg"], sv["kg"], sv["vg"], cos, sin, dag, lg_, dlg_)

    def ungrp(t3):
        return [_unperm(t3[g], d) for g, d in enumerate(DILATIONS)]

    dz = jnp.concatenate(ungrp(dqg) + ungrp(dkg) + ungrp(dvg) + [du, dv_sg, dga, dgb], axis=1)
    d_w_in = _mm_simple("mm_dw_in", sv["h1"], dz, "tn", bf16)
    dh1 = _mm_simple("mm_dh_in", dz, W["w_in"], "nt", bf16, 1024, 1024, 512)
    dx0, dg_mix = _norm_bwd("norm_bwd", dh1, sv["x0"], sm["norm_mix"], dx1)
    big = dict(w_in=d_w_in, w_br_attn=d_w_battn, w_br_sg=d_w_bsg, w_out=d_w_out, w_ff_gate=d_w_gate, w_ff_up=d_w_up,
               w_ff_down=d_w_down, w_ple_gate=d_w_pg, w_ple=d_w_ple)
    small = dict(sg_w=d_sgw, sg_b=d_sgb[:, :, 0], sg_ln_g=d_lg[0], sg_ln_b=d_lb[0], norm_mix=dg_mix[0], norm_ffn=dg_ffn[0],
                 norm_ple=dg_ple[0])
    return dx0, [big[n] for n, _ in _BIG], small


_SMALL = ("sg_w", "sg_b", "sg_ln_g", "sg_ln_b", "norm_mix", "norm_ffn", "norm_ple", "norm_final")


SMALL_ROWS = 256


def _pack_small(parts, tail):
    rows = [parts[n].astype(f32).reshape(-1, LANES) for n in _SMALL] + [tail]
    n = sum(r.shape[0] for r in rows)
    return jnp.concatenate(rows + [jnp.zeros((-n % SMALL_ROWS, LANES), f32)], axis=0)


def kernel(x, p, w_in, w_br_attn, w_br_sg, w_out, sg_w, sg_b, sg_ln_g, sg_ln_b, norm_mix, norm_ffn, norm_ple, norm_final, w_ff_gate, w_ff_up, w_ff_down, w_ple_gate, w_ple, loss_target, m_w_in, m_w_br_attn, m_w_br_sg, m_w_out, m_sg_w, m_sg_b, m_sg_ln_g, m_sg_ln_b, m_norm_mix, m_norm_ffn, m_norm_ple, m_norm_final, m_w_ff_gate, m_w_ff_up, m_w_ff_down, m_w_ple_gate, m_w_ple, v_w_in, v_w_br_attn, v_w_br_sg, v_w_out, v_sg_w, v_sg_b, v_sg_ln_g, v_sg_ln_b, v_norm_mix, v_norm_ffn, v_norm_ple, v_norm_final, v_w_ff_gate, v_w_ff_up, v_w_ff_down, v_w_ple_gate, v_w_ple):
    wts = dict(w_in=w_in, w_br_attn=w_br_attn, w_br_sg=w_br_sg, w_out=w_out, w_ff_gate=w_ff_gate, w_ff_up=w_ff_up,
               w_ff_down=w_ff_down, w_ple_gate=w_ple_gate, w_ple=w_ple)
    mom_m = dict(w_in=m_w_in, w_br_attn=m_w_br_attn, w_br_sg=m_w_br_sg, w_out=m_w_out, w_ff_gate=m_w_ff_gate,
                 w_ff_up=m_w_ff_up, w_ff_down=m_w_ff_down, w_ple_gate=m_w_ple_gate, w_ple=m_w_ple)
    mom_v = dict(w_in=v_w_in, w_br_attn=v_w_br_attn, w_br_sg=v_w_br_sg, w_out=v_w_out, w_ff_gate=v_w_ff_gate,
                 w_ff_up=v_w_ff_up, w_ff_down=v_w_ff_down, w_ple_gate=v_w_ple_gate, w_ple=v_w_ple)
    small_w = dict(sg_w=sg_w, sg_b=sg_b, sg_ln_g=sg_ln_g, sg_ln_b=sg_ln_b, norm_mix=norm_mix, norm_ffn=norm_ffn,
                   norm_ple=norm_ple, norm_final=norm_final)
    small_m = dict(sg_w=m_sg_w, sg_b=m_sg_b, sg_ln_g=m_sg_ln_g, sg_ln_b=m_sg_ln_b, norm_mix=m_norm_mix, norm_ffn=m_norm_ffn,
                   norm_ple=m_norm_ple, norm_final=m_norm_final)
    small_v = dict(sg_w=v_sg_w, sg_b=v_sg_b, sg_ln_g=v_sg_ln_g, sg_ln_b=v_sg_ln_b, norm_mix=v_norm_mix, norm_ffn=v_norm_ffn,
                   norm_ple=v_norm_ple, norm_final=v_norm_final)
    depth = w_in.shape[0]
    S = x.shape[1]
    shapes = [(wts[n].shape[1], wts[n].shape[2], axis) for n, axis in _BIG]
    c_idx = lax.axis_index("c").astype(jnp.int32).reshape(1)
    q_idx = (2 * lax.axis_index("x") + lax.axis_index("y")).astype(jnp.int32).reshape(1)
    tabs = _rope_tables(S)

    full = []
    for i in range(depth):
        g8 = _all_gather("ag_weights", _pack_shards([wts[n][i] for n, _ in _BIG]))
        full.append(dict(zip([n for n, _ in _BIG], _unpack_full(g8, shapes))))

    xs = x[0]
    saved = []
    for i in range(depth):
        sm = {n: small_w[n][i] for n in _SMALL if n != "norm_final"}
        xs, sv = _layer_fwd(xs, p[i, 0], full[i], sm, tabs)
        saved.append(sv)
    dx, dg_final, loss_part = _loss_head(xs, norm_final, loss_target[0])

    grads_flat = [None] * depth
    small_parts = [None] * depth
    for i in reversed(range(depth)):
        sm = {n: small_w[n][i] for n in _SMALL if n != "norm_final"}
        dx, bigs, small_parts[i] = _layer_bwd(dx, p[i, 0], full[i], sm, tabs, saved[i])
        g8 = _pack_grads(bigs, shapes)
        recv1 = _rs_sibling(g8)
        p4 = _rs_chip_sum(g8, recv1, c_idx)
        recv2 = _rs_chips(p4)
        grads_flat[i] = _rs_final_sum(p4, recv2, q_idx)
    grad_x = dx[None]

    parts = {n: jnp.stack([small_parts[i][n] for i in range(depth)]) for n in _SMALL if n != "norm_final"}
    parts["norm_final"] = dg_final[0]
    gathered = _all_gather("ag_small", _pack_small(parts, loss_part), in_vmem=True)
    g_s, d_s, nm_s, nv_s = _small_sum_adamw(gathered, _pack_small(small_w, jnp.zeros((8, LANES), f32)),
                                            _pack_small(small_m, jnp.zeros((8, LANES), f32)),
                                            _pack_small(small_v, jnp.ones((8, LANES), f32)))
    loss = g_s[sum(small_w[n].size for n in _SMALL) // LANES, 0]

    def unpack_small(flat):
        out, off = {}, 0
        for n in _SMALL:
            k = small_w[n].size // LANES
            out[n] = flat[off:off + k].reshape(small_w[n].shape)
            off += k
        return out

    sm_g, sm_d, sm_nm, sm_nv = unpack_small(g_s), unpack_small(d_s), unpack_small(nm_s), unpack_small(nv_s)

    big_g, big_d, big_nm, big_nv = {}, {}, {}, {}
    per_layer = [_unpack_shards(grads_flat[i], shapes) for i in range(depth)]
    for k, (n, _) in enumerate(_BIG):
        r, c, _ = shapes[k]
        g = jnp.stack([per_layer[i][k] for i in range(depth)])
        d2, nm2, nv2 = _adamw("adamw_" + n, wts[n].reshape(depth * r, c), g.reshape(depth * r, c),
                              mom_m[n].reshape(depth * r, c), mom_v[n].reshape(depth * r, c))
        big_g[n], big_d[n], big_nm[n], big_nv[n] = g, d2.reshape(depth, r, c), nm2.reshape(depth, r, c), nv2.reshape(depth, r, c)

    order = ["w_in", "w_br_attn", "w_br_sg", "w_out", "sg_w", "sg_b", "sg_ln_g", "sg_ln_b", "norm_mix", "norm_ffn", "norm_ple",
             "norm_final", "w_ff_gate", "w_ff_up", "w_ff_down", "w_ple_gate", "w_ple"]

    def pick(big, small):
        return [big[n] if n in big else small[n] for n in order]

    return (loss, grad_x, *pick(big_g, sm_g), *pick(big_d, sm_d), *pick(big_nm, sm_nm), *pick(big_nv, sm_nv))
```

```python
import functools
import math

import jax
import jax.numpy as jnp
from jax import lax
from jax.experimental import pallas as pl
from jax.experimental.pallas import tpu as pltpu

f32 = jnp.float32
bf16 = jnp.bfloat16

HEAD_DIM = 128
N_GROUPS = 3
HEADS = 4
DILATIONS = (1, 4, 16)
RADIUS = 64
BLK = 128
QKV_W = 3 * N_GROUPS * HEADS * HEAD_DIM
ATTN_W = HEADS * HEAD_DIM
SG_CHUNK = 128
SG_GROUPS = 8
SG_W = SG_GROUPS * 128
ROPE_THETA = 10000.0
EPS = 1e-6
NEG = -1e30
N_DEV = 8
LANES = 128

ADAM_LR = 0.001
ADAM_B1 = 0.9
ADAM_B2 = 0.999
ADAM_EPS = 1e-08
ADAM_WD = 0.01
ADAM_STEP = 10

VMEM_LIMIT = 52 * 1024 * 1024
MESH = pl.DeviceIdType.MESH

NN = (((1,), (0,)), ((), ()))
NT = (((1,), (1,)), ((), ()))
TN = (((0,), (0,)), ((), ()))
_DN = {"nn": NN, "nt": NT, "tn": TN}


def _cp(*sem):
    return pltpu.CompilerParams(dimension_semantics=sem, vmem_limit_bytes=VMEM_LIMIT)


def _tile(n, pref, unit=128):
    if n <= pref:
        return n
    t = (pref // unit) * unit
    while t >= unit:
        if n % t == 0:
            return t
        t -= unit
    return n


def _sigmoid(x):
    return 1.0 / (1.0 + jnp.exp(-x))


_GC = math.sqrt(2.0 / math.pi)
_GA = 0.044715


def _gelu(x):
    return 0.5 * x * (1.0 + jnp.tanh(_GC * (x + _GA * x * x * x)))


def _gelu_grad(x):
    t = jnp.tanh(_GC * (x + _GA * x * x * x))
    return 0.5 * (1.0 + t) + 0.5 * x * (1.0 - t * t) * _GC * (1.0 + 3.0 * _GA * x * x)


def _matmul(name, prods, M, N, tm, tn, nk, outs, epilogue, extras=(), n_acc=1):
    in_specs, operands, metas = [], [], []
    for a, b, mode, acc in prods:
        if mode == "tn":
            tk = a.shape[0] // nk
            in_specs += [pl.BlockSpec((tk, tm), lambda i, j, k: (k, i)), pl.BlockSpec((tk, tn), lambda i, j, k: (k, j))]
        elif mode == "nt":
            tk = a.shape[1] // nk
            in_specs += [pl.BlockSpec((tm, tk), lambda i, j, k: (i, k)), pl.BlockSpec((tn, tk), lambda i, j, k: (j, k))]
        else:
            tk = a.shape[1] // nk
            in_specs += [pl.BlockSpec((tm, tk), lambda i, j, k: (i, k)), pl.BlockSpec((tk, tn), lambda i, j, k: (k, j))]
        operands += [a, b]
        metas.append((mode, acc))
    for arr, bshape, imap in extras:
        in_specs.append(pl.BlockSpec(bshape, functools.partial(lambda i, j, k, f: f(i, j), f=imap)))
        operands.append(arr)
    out_specs = [pl.BlockSpec(bs, functools.partial(lambda i, j, k, f: f(i, j), f=imap)) for _, _, bs, imap in outs]
    out_shape = [jax.ShapeDtypeStruct(s, d) for s, d, _, _ in outs]
    n_prod, n_ext, n_out = len(prods), len(extras), len(outs)

    def body(*refs):
        in_refs = refs[: 2 * n_prod]
        ex_refs = refs[2 * n_prod : 2 * n_prod + n_ext]
        out_refs = refs[2 * n_prod + n_ext : 2 * n_prod + n_ext + n_out]
        acc_refs = refs[2 * n_prod + n_ext + n_out :]

        def partials():
            res = [None] * n_acc
            for idx, (mode, acc) in enumerate(metas):
                a = in_refs[2 * idx][...].astype(bf16)
                b = in_refs[2 * idx + 1][...].astype(bf16)
                d = lax.dot_general(a, b, _DN[mode], preferred_element_type=f32)
                res[acc] = d if res[acc] is None else res[acc] + d
            return res

        def finish(accs):
            vals = epilogue(accs, [r[...] for r in ex_refs])
            for r, v in zip(out_refs, vals):
                r[...] = v.astype(r.dtype)

        if nk == 1:
            finish(partials())
        else:
            k = pl.program_id(2)

            @pl.when(k == 0)
            def _():
                for r in acc_refs:
                    r[...] = jnp.zeros_like(r)

            for r, d in zip(acc_refs, partials()):
                r[...] += d

            @pl.when(k == nk - 1)
            def _():
                finish([r[...] for r in acc_refs])

    scratch = [pltpu.VMEM((tm, tn), f32) for _ in range(n_acc)] if nk > 1 else []
    res = pl.pallas_call(
        body, name=name, out_shape=out_shape, grid=(M // tm, N // tn, nk),
        in_specs=in_specs, out_specs=out_specs, scratch_shapes=scratch,
        compiler_params=_cp("parallel", "parallel", "arbitrary"),
    )(*operands)
    return res


def _ident(accs, ex):
    return [accs[0]]


def _mm_simple(name, a, b, mode, out_dtype, tm_pref=1024, tn_pref=1024, tk_pref=1024):
    if mode == "tn":
        K, M = a.shape
        N = b.shape[1]
    elif mode == "nt":
        M, K = a.shape
        N = b.shape[0]
    else:
        M, K = a.shape
        N = b.shape[1]
    tm, tn, tk = _tile(M, tm_pref), _tile(N, tn_pref), _tile(K, tk_pref)
    return _matmul(name, [(a, b, mode, 0)], M, N, tm, tn, K // tk,
                   [((M, N), out_dtype, (tm, tn), lambda i, j: (i, j))], _ident)[0]


def _group(c, width_pref=1024):
    g = LANES // math.gcd(c, LANES)
    while g < N_DEV and 2 * g * c <= width_pref:
        g *= 2
    return g


def _join(parts):
    return parts[0] if len(parts) == 1 else jnp.concatenate(parts, axis=1)


def _mm_nn_cols(name, a, gs_list, outs_dtypes, epilogue, extras=(), tm_pref=512, width_pref=1024):
    M, K = a.shape
    c = gs_list[0].shape[2]
    g = _group(c, width_pref)
    W = g * c
    tm = _tile(M, tm_pref, 8)
    n_g, n_ex, n_out = len(gs_list), len(extras), len(outs_dtypes)

    def body(*refs):
        a_ref = refs[0]
        g_refs = refs[1:1 + n_g]
        ex_refs = refs[1 + n_g:1 + n_g + n_ex]
        out_refs = refs[1 + n_g + n_ex:]
        av = a_ref[...].astype(bf16)
        accs = [_join([lax.dot_general(av, gr[s], NN, preferred_element_type=f32) for s in range(g)]) for gr in g_refs]
        for r, v in zip(out_refs, epilogue(accs, [r[...] for r in ex_refs])):
            r[...] = v.astype(r.dtype)

    tile = pl.BlockSpec((tm, W), lambda j, i: (i, j))
    return pl.pallas_call(
        body, name=name, out_shape=[jax.ShapeDtypeStruct((M, N_DEV * c), d) for d in outs_dtypes],
        grid=(N_DEV // g, M // tm),
        in_specs=[pl.BlockSpec((tm, K), lambda j, i: (i, 0))]
        + [pl.BlockSpec((None, g, K, c), lambda j, i: (j, 0, 0, 0))] * n_g + [tile] * n_ex,
        out_specs=[tile] * n_out, compiler_params=_cp("parallel", "parallel"),
    )(a, *[gm.reshape(N_DEV // g, g, K, c) for gm in gs_list], *extras)


def _mm_nt_cols(name, pairs, out_dtype, tm_pref=1024, tn_pref=1024, width_pref=1024):
    M = pairs[0][0].shape[0]
    Kw, c = pairs[0][1].shape[1:]
    g = _group(c, width_pref)
    W = g * c
    tm, tn = _tile(M, tm_pref, 8), _tile(Kw, tn_pref)
    nk = N_DEV // g
    n_p = len(pairs)

    def body(*refs):
        o_ref, acc = refs[2 * n_p], refs[2 * n_p + 1]
        k = pl.program_id(2)

        @pl.when(k == 0)
        def _():
            acc[...] = jnp.zeros_like(acc)

        tot = None
        for n in range(n_p):
            d_ref, g_ref = refs[2 * n], refs[2 * n + 1]
            for s in range(g):
                part = lax.dot_general(d_ref[:, s * c:(s + 1) * c], g_ref[s], NT, preferred_element_type=f32)
                tot = part if tot is None else tot + part
        acc[...] += tot

        @pl.when(k == nk - 1)
        def _():
            o_ref[...] = acc[...].astype(o_ref.dtype)

    in_specs, operands = [], []
    for d, gm in pairs:
        in_specs += [pl.BlockSpec((tm, W), lambda i, j, k: (i, k)), pl.BlockSpec((None, g, tn, c), lambda i, j, k: (k, 0, j, 0))]
        operands += [d, gm.reshape(nk, g, Kw, c)]
    return pl.pallas_call(
        body, name=name, out_shape=jax.ShapeDtypeStruct((M, Kw), out_dtype), grid=(M // tm, Kw // tn, nk),
        in_specs=in_specs, out_specs=pl.BlockSpec((tm, tn), lambda i, j, k: (i, j)),
        scratch_shapes=[pltpu.VMEM((tm, tn), f32)], compiler_params=_cp("parallel", "parallel", "arbitrary"),
    )(*operands)


def _mm_tn_cols(name, x, ds, c, tm_pref=1024, tk_pref=1024, width_pref=1024):
    S, Kw = x.shape
    g = _group(c, width_pref)
    W = g * c
    tm, tk = _tile(Kw, tm_pref), _tile(S, tk_pref, 16)
    nk = S // tk
    n_d = len(ds)

    def body(*refs):
        x_ref = refs[0]
        d_refs = refs[1:1 + n_d]
        o_refs = refs[1 + n_d:1 + 2 * n_d]
        accs = refs[1 + 2 * n_d:]
        k = pl.program_id(2)

        @pl.when(k == 0)
        def _():
            for acc in accs:
                acc[...] = jnp.zeros_like(acc)

        xv = x_ref[...].astype(bf16)
        for d_ref, acc in zip(d_refs, accs):
            for s in range(g):
                acc[s] += lax.dot_general(xv, d_ref[:, s * c:(s + 1) * c], TN, preferred_element_type=f32)

        @pl.when(k == nk - 1)
        def _():
            for o_ref, acc in zip(o_refs, accs):
                o_ref[...] = acc[...].astype(o_ref.dtype)

    outs = pl.pallas_call(
        body, name=name, out_shape=[jax.ShapeDtypeStruct((N_DEV // g, g, Kw, c), bf16)] * n_d,
        grid=(Kw // tm, N_DEV // g, nk),
        in_specs=[pl.BlockSpec((tk, tm), lambda i, j, k: (k, i))] + [pl.BlockSpec((tk, W), lambda i, j, k: (k, j))] * n_d,
        out_specs=[pl.BlockSpec((None, g, tm, c), lambda i, j, k: (j, 0, i, 0))] * n_d,
        scratch_shapes=[pltpu.VMEM((g, tm, c), f32)] * n_d, compiler_params=_cp("parallel", "parallel", "arbitrary"),
    )(x, *ds)
    return [o.reshape(N_DEV, Kw, c) for o in outs]


def _norm_fwd(name, x, g):
    S, D = x.shape
    tm = _tile(S, 512, 8)

    def body(x_ref, g_ref, h_ref):
        xv = x_ref[...]
        r = lax.rsqrt(jnp.mean(xv * xv, axis=-1, keepdims=True) + EPS)
        h_ref[...] = (xv * r * g_ref[...]).astype(bf16)

    return pl.pallas_call(
        body, name=name, out_shape=jax.ShapeDtypeStruct((S, D), bf16), grid=(S // tm,),
        in_specs=[pl.BlockSpec((tm, D), lambda i: (i, 0)), pl.BlockSpec((1, D), lambda i: (0, 0))],
        out_specs=pl.BlockSpec((tm, D), lambda i: (i, 0)), compiler_params=_cp("parallel"),
    )(x, g.reshape(1, D))


def _norm_bwd(name, dh, x, g, dx_in):
    S, D = x.shape
    tm = _tile(S, 256, 8)

    def body(dh_ref, x_ref, g_ref, dxi_ref, dx_ref, dg_ref):
        i = pl.program_id(0)
        xv = x_ref[...]
        r = lax.rsqrt(jnp.mean(xv * xv, axis=-1, keepdims=True) + EPS)
        xh = xv * r
        dhv = dh_ref[...].astype(f32)
        dxh = dhv * g_ref[...]
        dx_ref[...] = dxi_ref[...] + r * (dxh - xh * jnp.mean(dxh * xh, axis=-1, keepdims=True))

        @pl.when(i == 0)
        def _():
            dg_ref[...] = jnp.zeros_like(dg_ref)

        dg_ref[...] += jnp.sum(dhv * xh, axis=0, keepdims=True)

    return pl.pallas_call(
        body, name=name, out_shape=[jax.ShapeDtypeStruct((S, D), f32), jax.ShapeDtypeStruct((1, D), f32)], grid=(S // tm,),
        in_specs=[pl.BlockSpec((tm, D), lambda i: (i, 0)), pl.BlockSpec((tm, D), lambda i: (i, 0)),
                  pl.BlockSpec((1, D), lambda i: (0, 0)), pl.BlockSpec((tm, D), lambda i: (i, 0))],
        out_specs=[pl.BlockSpec((tm, D), lambda i: (i, 0)), pl.BlockSpec((1, D), lambda i: (0, 0))],
        compiler_params=_cp("arbitrary"),
    )(dh, x, g.reshape(1, D), dx_in)


def _loss_head(x, g, t):
    S, D = x.shape
    tm = _tile(S, 256, 8)

    def body(x_ref, g_ref, t_ref, dx_ref, dg_ref, loss_ref):
        i = pl.program_id(0)
        xv = x_ref[...]
        r = lax.rsqrt(jnp.mean(xv * xv, axis=-1, keepdims=True) + EPS)
        xh = xv * r
        gv = g_ref[...]
        err = xh * gv - t_ref[...]
        dy = err * (1.0 / D)
        dxh = dy * gv
        dx_ref[...] = r * (dxh - xh * jnp.mean(dxh * xh, axis=-1, keepdims=True))

        @pl.when(i == 0)
        def _():
            dg_ref[...] = jnp.zeros_like(dg_ref)
            loss_ref[...] = jnp.zeros_like(loss_ref)

        dg_ref[...] += jnp.sum(dy * xh, axis=0, keepdims=True)
        row = jnp.sum(err * err, axis=-1, keepdims=True) * (0.5 / D)
        loss_ref[...] += jnp.broadcast_to(jnp.sum(row, axis=0, keepdims=True), loss_ref.shape)

    return pl.pallas_call(
        body, name="loss_head",
        out_shape=[jax.ShapeDtypeStruct((S, D), f32), jax.ShapeDtypeStruct((1, D), f32), jax.ShapeDtypeStruct((8, LANES), f32)],
        grid=(S // tm,),
        in_specs=[pl.BlockSpec((tm, D), lambda i: (i, 0)), pl.BlockSpec((1, D), lambda i: (0, 0)), pl.BlockSpec((tm, D), lambda i: (i, 0))],
        out_specs=[pl.BlockSpec((tm, D), lambda i: (i, 0)), pl.BlockSpec((1, D), lambda i: (0, 0)), pl.BlockSpec((8, LANES), lambda i: (0, 0))],
        compiler_params=_cp("arbitrary"),
    )(x, g.reshape(1, D), t)


def _perm(t, d):
    if d == 1:
        return t
    S, C = t.shape
    return t.reshape(S // d, d, C).transpose(1, 0, 2).reshape(S, C)


def _unperm(t, d):
    if d == 1:
        return t
    S, C = t.shape
    return t.reshape(d, S // d, C).transpose(1, 0, 2).reshape(S, C)


def _rope_tables(S):
    half = HEAD_DIM // 2
    pos = jnp.arange(S, dtype=f32)
    inv_freq = ROPE_THETA ** (-jnp.arange(0, HEAD_DIM, 2, dtype=f32) / HEAD_DIM)
    ang = pos[:, None] * inv_freq[None, :]
    c, s = jnp.cos(ang), jnp.sin(ang)
    cos2 = jnp.concatenate([c, c], axis=-1)
    sin2 = jnp.concatenate([-s, s], axis=-1)
    assert cos2.shape == (S, 2 * half)
    return (jnp.stack([_perm(cos2, d) for d in DILATIONS]), jnp.stack([_perm(sin2, d) for d in DILATIONS]))


def _rope(t, c, s):
    return t * c + pltpu.roll(t, HEAD_DIM // 2, 1) * s


def _rope_bwd(dt, c, s):
    return dt * c - pltpu.roll(dt, HEAD_DIM // 2, 1) * s


def _band_bounds(i, nblk):
    g = pl.program_id(0)
    lb = jnp.right_shift(jnp.int32(nblk), 2 * g)
    pos = lax.rem(i, lb)
    lo = jnp.where(pos == 0, BLK, 0)
    hi = jnp.where(pos == lb - 1, 2 * BLK, 3 * BLK)
    return lo, hi


def _band_specs(width, nblk):
    prev = pl.BlockSpec((None, BLK, width), lambda g, i: (g, jnp.maximum(i - 1, 0), 0))
    cur = pl.BlockSpec((None, BLK, width), lambda g, i: (g, i, 0))
    nxt = pl.BlockSpec((None, BLK, width), lambda g, i: (g, jnp.minimum(i + 1, nblk - 1), 0))
    return [prev, cur, nxt]


_SCALE = HEAD_DIM ** -0.5


def _attn_fwd(q, k, v, cos, sin):
    _, S, W = q.shape
    nblk = S // BLK

    def body(q_ref, kp, kc, kn, vp, vc, vn, cq, sq, ckp, ckc, ckn, skp, skc, skn, o_ref, lse_ref):
        i = pl.program_id(1)
        lo, hi = _band_bounds(i, nblk)
        a = lax.broadcasted_iota(jnp.int32, (BLK, 3 * BLK), 0)
        b = lax.broadcasted_iota(jnp.int32, (BLK, 3 * BLK), 1)
        mask = (jnp.abs(b - BLK - a) <= RADIUS) & (b >= lo) & (b < hi)
        ck = jnp.concatenate([ckp[...], ckc[...], ckn[...]], axis=0)
        sk = jnp.concatenate([skp[...], skc[...], skn[...]], axis=0)
        for hh in range(HEADS):
            sl = slice(hh * HEAD_DIM, (hh + 1) * HEAD_DIM)
            qh = _rope(q_ref[:, sl].astype(f32), cq[...], sq[...]).astype(bf16)
            kh = jnp.concatenate([kp[:, sl], kc[:, sl], kn[:, sl]], axis=0).astype(f32)
            kh = _rope(kh, ck, sk).astype(bf16)
            vh = jnp.concatenate([vp[:, sl], vc[:, sl], vn[:, sl]], axis=0)
            s = lax.dot_general(qh, kh, NT, preferred_element_type=f32) * _SCALE
            s = jnp.where(mask, s, NEG)
            m = jnp.max(s, axis=-1, keepdims=True)
            e = jnp.exp(s - m)
            den = jnp.sum(e, axis=-1, keepdims=True)
            o = lax.dot_general(e.astype(bf16), vh, NN, preferred_element_type=f32) * (1.0 / den)
            o_ref[:, sl] = o.astype(bf16)
            lse_ref[:, sl] = jnp.broadcast_to(m + jnp.log(den), (BLK, HEAD_DIM))

    blk = pl.BlockSpec((None, BLK, W), lambda g, i: (g, i, 0))
    tab = pl.BlockSpec((None, BLK, HEAD_DIM), lambda g, i: (g, i, 0))
    return pl.pallas_call(
        body, name="attn_fwd",
        out_shape=[jax.ShapeDtypeStruct((N_GROUPS, S, W), bf16), jax.ShapeDtypeStruct((N_GROUPS, S, W), f32)],
        grid=(N_GROUPS, nblk),
        in_specs=[blk] + _band_specs(W, nblk) * 2 + [tab, tab] + _band_specs(HEAD_DIM, nblk) * 2,
        out_specs=[blk, blk], compiler_params=_cp("parallel", "parallel"),
    )(q, k, k, k, v, v, v, cos, sin, cos, cos, cos, sin, sin, sin)


def _attn_combine(o3, lse3):
    _, S, W = o3.shape
    tm = _tile(S, 512, 8)

    def body(o_ref, l_ref, attn_ref, lse_ref):
        l0, l1, l2 = l_ref[0], l_ref[1], l_ref[2]
        m = jnp.maximum(jnp.maximum(l0, l1), l2)
        w0, w1, w2 = jnp.exp(l0 - m), jnp.exp(l1 - m), jnp.exp(l2 - m)
        den = w0 + w1 + w2
        acc = w0 * o_ref[0].astype(f32) + w1 * o_ref[1].astype(f32) + w2 * o_ref[2].astype(f32)
        attn_ref[...] = (acc * (1.0 / den)).astype(bf16)
        lse_ref[...] = m + jnp.log(den)

    blk3 = pl.BlockSpec((N_GROUPS, tm, W), lambda i: (0, i, 0))
    blk = pl.BlockSpec((tm, W), lambda i: (i, 0))
    return pl.pallas_call(
        body, name="attn_combine", out_shape=[jax.ShapeDtypeStruct((S, W), bf16), jax.ShapeDtypeStruct((S, W), f32)],
        grid=(S // tm,), in_specs=[blk3, blk3], out_specs=[blk, blk], compiler_params=_cp("parallel"),
    )(o3, lse3)


def _attn_delta(attn, dattn):
    S, W = attn.shape
    tm = _tile(S, 512, 8)

    def body(a_ref, d_ref, o_ref):
        prod = a_ref[...].astype(f32) * d_ref[...].astype(f32)
        for hh in range(HEADS):
            sl = slice(hh * HEAD_DIM, (hh + 1) * HEAD_DIM)
            o_ref[:, sl] = jnp.broadcast_to(jnp.sum(prod[:, sl], axis=-1, keepdims=True), (tm, HEAD_DIM))

    blk = pl.BlockSpec((tm, W), lambda i: (i, 0))
    return pl.pallas_call(
        body, name="attn_delta", out_shape=jax.ShapeDtypeStruct((S, W), f32), grid=(S // tm,),
        in_specs=[blk, blk], out_specs=blk, compiler_params=_cp("parallel"),
    )(attn, dattn)


def _attn_bwd_dq(q, k, v, cos, sin, da, lse, dl):
    _, S, W = q.shape
    nblk = S // BLK

    def body(q_ref, kp, kc, kn, vp, vc, vn, cq, sq, ckp, ckc, ckn, skp, skc, skn, da_ref, l_ref, dl_ref, dq_ref):
        i = pl.program_id(1)
        lo, hi = _band_bounds(i, nblk)
        a = lax.broadcasted_iota(jnp.int32, (BLK, 3 * BLK), 0)
        b = lax.broadcasted_iota(jnp.int32, (BLK, 3 * BLK), 1)
        mask = (jnp.abs(b - BLK - a) <= RADIUS) & (b >= lo) & (b < hi)
        ck = jnp.concatenate([ckp[...], ckc[...], ckn[...]], axis=0)
        sk = jnp.concatenate([skp[...], skc[...], skn[...]], axis=0)
        for hh in range(HEADS):
            sl = slice(hh * HEAD_DIM, (hh + 1) * HEAD_DIM)
            qh = _rope(q_ref[:, sl].astype(f32), cq[...], sq[...]).astype(bf16)
            kh = jnp.concatenate([kp[:, sl], kc[:, sl], kn[:, sl]], axis=0).astype(f32)
            kh = _rope(kh, ck, sk).astype(bf16)
            vh = jnp.concatenate([vp[:, sl], vc[:, sl], vn[:, sl]], axis=0)
            s = lax.dot_general(qh, kh, NT, preferred_element_type=f32) * _SCALE
            lh = l_ref[:, sl]
            l3 = jnp.concatenate([lh, lh, lh], axis=1)
            p = jnp.exp(jnp.where(mask, s - l3, NEG))
            dp = lax.dot_general(da_ref[:, sl], vh, NT, preferred_element_type=f32)
            dh = dl_ref[:, sl]
            ds = p * (dp - jnp.concatenate([dh, dh, dh], axis=1))
            dqh = lax.dot_general(ds.astype(bf16), kh, NN, preferred_element_type=f32) * _SCALE
            dq_ref[:, sl] = _rope_bwd(dqh, cq[...], sq[...]).astype(bf16)

    blk = pl.BlockSpec((None, BLK, W), lambda g, i: (g, i, 0))
    tab = pl.BlockSpec((None, BLK, HEAD_DIM), lambda g, i: (g, i, 0))
    return pl.pallas_call(
        body, name="attn_bwd_dq", out_shape=jax.ShapeDtypeStruct((N_GROUPS, S, W), bf16), grid=(N_GROUPS, nblk),
        in_specs=[blk] + _band_specs(W, nblk) * 2 + [tab, tab] + _band_specs(HEAD_DIM, nblk) * 2 + [blk, blk, blk],
        out_specs=blk, compiler_params=_cp("parallel", "parallel"),
    )(q, k, k, k, v, v, v, cos, sin, cos, cos, cos, sin, sin, sin, da, lse, dl)


def _attn_bwd_dkv(q, k, v, cos, sin, da, lse, dl):
    _, S, W = q.shape
    nblk = S // BLK

    def body(k_ref, v_ref, ck, sk, qp, qc, qn, cqp, cqc, cqn, sqp, sqc, sqn, dap, dac, dan, lp, lc, ln, dlp, dlc, dln,
             dk_ref, dv_ref):
        j = pl.program_id(1)
        lo, hi = _band_bounds(j, nblk)
        a = lax.broadcasted_iota(jnp.int32, (3 * BLK, BLK), 0)
        b = lax.broadcasted_iota(jnp.int32, (3 * BLK, BLK), 1)
        mask = (jnp.abs(b - (a - BLK)) <= RADIUS) & (a >= lo) & (a < hi)
        cq = jnp.concatenate([cqp[...], cqc[...], cqn[...]], axis=0)
        sq = jnp.concatenate([sqp[...], sqc[...], sqn[...]], axis=0)
        for hh in range(HEADS):
            sl = slice(hh * HEAD_DIM, (hh + 1) * HEAD_DIM)
            kh = _rope(k_ref[:, sl].astype(f32), ck[...], sk[...]).astype(bf16)
            vh = v_ref[:, sl]
            qh = jnp.concatenate([qp[:, sl], qc[:, sl], qn[:, sl]], axis=0).astype(f32)
            qh = _rope(qh, cq, sq).astype(bf16)
            dah = jnp.concatenate([dap[:, sl], dac[:, sl], dan[:, sl]], axis=0)
            lh = jnp.concatenate([lp[:, sl], lc[:, sl], ln[:, sl]], axis=0)
            dlh = jnp.concatenate([dlp[:, sl], dlc[:, sl], dln[:, sl]], axis=0)
            s = lax.dot_general(qh, kh, NT, preferred_element_type=f32) * _SCALE
            p = jnp.exp(jnp.where(mask, s - lh, NEG))
            dv_ref[:, sl] = lax.dot_general(p.astype(bf16), dah, TN, preferred_element_type=f32).astype(bf16)
            dp = lax.dot_general(dah, vh, NT, preferred_element_type=f32)
            ds = p * (dp - dlh)
            dkh = lax.dot_general(ds.astype(bf16), qh, TN, preferred_element_type=f32) * _SCALE
            dk_ref[:, sl] = _rope_bwd(dkh, ck[...], sk[...]).astype(bf16)

    blk = pl.BlockSpec((None, BLK, W), lambda g, i: (g, i, 0))
    tab = pl.BlockSpec((None, BLK, HEAD_DIM), lambda g, i: (g, i, 0))
    bw, bt = _band_specs(W, nblk), _band_specs(HEAD_DIM, nblk)
    return pl.pallas_call(
        body, name="attn_bwd_dkv",
        out_shape=[jax.ShapeDtypeStruct((N_GROUPS, S, W), bf16), jax.ShapeDtypeStruct((N_GROUPS, S, W), bf16)],
        grid=(N_GROUPS, nblk),
        in_specs=[blk, blk, tab, tab] + bw + bt + bt + bw + bw + bw,
        out_specs=[blk, blk], compiler_params=_cp("parallel", "parallel"),
    )(k, v, cos, sin, q, q, q, cos, cos, cos, sin, sin, sin, da, da, da, lse, lse, lse, dl, dl, dl)


_SG_ROWS = 512


def _sg_z_specs(tm, half):
    o = QKV_W // half
    return [pl.BlockSpec((tm, half), functools.partial(lambda i, c: (i, c), c=o + n)) for n in range(4)]


def _sg_norm(v, lg, lb):
    gv = _gelu(v)
    mu = jnp.mean(gv, axis=-1, keepdims=True)
    xc = gv - mu
    rstd = lax.rsqrt(jnp.mean(xc * xc, axis=-1, keepdims=True) + EPS)
    xh = xc * rstd
    return xh, rstd, xh * lg + lb


def _sg_fwd(z, w, bb, lg, lb):
    S = z.shape[0]
    tm = _tile(S, _SG_ROWS, SG_CHUNK)
    half = SG_W // 2

    def body(u0, u1, v0, v1, w_ref, bb_ref, lg_ref, lb_ref, o_ref):
        u = jnp.concatenate([u0[...], u1[...]], axis=1).astype(f32)
        v = jnp.concatenate([v0[...], v1[...]], axis=1).astype(f32)
        gu = _gelu(u)
        _, _, vn = _sg_norm(v, lg_ref[...], lb_ref[...])
        vnb = vn.astype(bf16)
        for c in range(tm // SG_CHUNK):
            rs = slice(c * SG_CHUNK, (c + 1) * SG_CHUNK)
            for g in range(SG_GROUPS):
                cs = slice(g * 128, (g + 1) * 128)
                mixed = lax.dot_general(w_ref[g], vnb[rs, cs], NN, preferred_element_type=f32) + bb_ref[g]
                o_ref[rs, cs] = (gu[rs, cs] * mixed).astype(bf16)

    full3 = pl.BlockSpec((SG_GROUPS, 128, 128), lambda i: (0, 0, 0))
    vec = pl.BlockSpec((1, SG_W), lambda i: (0, 0))
    return pl.pallas_call(
        body, name="sg_fwd", out_shape=jax.ShapeDtypeStruct((S, SG_W), bf16), grid=(S // tm,),
        in_specs=_sg_z_specs(tm, half) + [full3, full3, vec, vec],
        out_specs=pl.BlockSpec((tm, SG_W), lambda i: (i, 0)), compiler_params=_cp("parallel"),
    )(z, z, z, z, w, bb, lg.reshape(1, SG_W), lb.reshape(1, SG_W))


def _sg_bwd(dsg, z, w, wt, bb, lg, lb):
    S = z.shape[0]
    tm = _tile(S, _SG_ROWS, SG_CHUNK)
    half = SG_W // 2

    def body(d_ref, u0, u1, v0, v1, w_ref, wt_ref, bb_ref, lg_ref, lb_ref, du_ref, dv_ref, dw_ref, db_ref, dlg_ref, dlb_ref, dvn_scr):
        i = pl.program_id(0)

        @pl.when(i == 0)
        def _():
            dw_ref[...] = jnp.zeros_like(dw_ref)
            db_ref[...] = jnp.zeros_like(db_ref)
            dlg_ref[...] = jnp.zeros_like(dlg_ref)
            dlb_ref[...] = jnp.zeros_like(dlb_ref)

        u = jnp.concatenate([u0[...], u1[...]], axis=1).astype(f32)
        v = jnp.concatenate([v0[...], v1[...]], axis=1).astype(f32)
        gu = _gelu(u)
        dgu = _gelu_grad(u)
        xh, rstd, vn = _sg_norm(v, lg_ref[...], lb_ref[...])
        vnb = vn.astype(bf16)
        dsg_v = d_ref[...].astype(f32)
        for g in range(SG_GROUPS):
            cs = slice(g * 128, (g + 1) * 128)
            dw_g = jnp.zeros((128, 128), f32)
            db_g = jnp.zeros((128, 1), f32)
            for c in range(tm // SG_CHUNK):
                rs = slice(c * SG_CHUNK, (c + 1) * SG_CHUNK)
                ds = dsg_v[rs, cs]
                mixed = lax.dot_general(w_ref[g], vnb[rs, cs], NN, preferred_element_type=f32) + bb_ref[g]
                du_ref[rs, cs] = (ds * mixed * dgu[rs, cs]).astype(bf16)
                dmix = ds * gu[rs, cs]
                dmb = dmix.astype(bf16)
                dw_g = dw_g + lax.dot_general(dmb, vnb[rs, cs], NT, preferred_element_type=f32)
                db_g = db_g + jnp.sum(dmix, axis=-1, keepdims=True)
                dvn_scr[rs, cs] = lax.dot_general(wt_ref[g], dmb, NN, preferred_element_type=f32)
            dw_ref[g] += dw_g
            db_ref[g] += jnp.broadcast_to(db_g, (128, 128))
        dvn = dvn_scr[...]
        dlg_ref[...] += jnp.sum(dvn * xh, axis=0, keepdims=True)
        dlb_ref[...] += jnp.sum(dvn, axis=0, keepdims=True)
        dxh = dvn * lg_ref[...]
        dgv = rstd * (dxh - jnp.mean(dxh, axis=-1, keepdims=True) - xh * jnp.mean(dxh * xh, axis=-1, keepdims=True))
        dv_ref[...] = (dgv * _gelu_grad(v)).astype(bf16)

    full3 = pl.BlockSpec((SG_GROUPS, 128, 128), lambda i: (0, 0, 0))
    vec = pl.BlockSpec((1, SG_W), lambda i: (0, 0))
    row = pl.BlockSpec((tm, SG_W), lambda i: (i, 0))
    return pl.pallas_call(
        body, name="sg_bwd",
        out_shape=[jax.ShapeDtypeStruct((S, SG_W), bf16), jax.ShapeDtypeStruct((S, SG_W), bf16),
                   jax.ShapeDtypeStruct((SG_GROUPS, 128, 128), f32), jax.ShapeDtypeStruct((SG_GROUPS, 128, 128), f32),
                   jax.ShapeDtypeStruct((1, SG_W), f32), jax.ShapeDtypeStruct((1, SG_W), f32)],
        grid=(S // tm,),
        in_specs=[row] + _sg_z_specs(tm, half) + [full3, full3, full3, vec, vec],
        out_specs=[row, row, full3, full3, vec, vec],
        scratch_shapes=[pltpu.VMEM((tm, SG_W), f32)], compiler_params=_cp("arbitrary"),
    )(dsg, z, z, z, z, w, wt, bb, lg.reshape(1, SG_W), lb.reshape(1, SG_W))


def _merge_fwd(z, ya, yb, D):
    S = z.shape[0]
    tm, tc = _tile(S, 512, 8), _tile(D, 512)
    o_a, o_b = (QKV_W + 2 * SG_W) // tc, (QKV_W + 2 * SG_W + D) // tc

    def body(ga_ref, gb_ref, ya_ref, yb_ref, o_ref):
        ga = _sigmoid(ga_ref[...].astype(f32))
        gb = _sigmoid(gb_ref[...].astype(f32))
        o_ref[...] = (ga * ya_ref[...].astype(f32) + gb * yb_ref[...].astype(f32)).astype(bf16)

    blk = pl.BlockSpec((tm, tc), lambda i, j: (i, j))
    return pl.pallas_call(
        body, name="merge_fwd", out_shape=jax.ShapeDtypeStruct((S, D), bf16), grid=(S // tm, D // tc),
        in_specs=[pl.BlockSpec((tm, tc), lambda i, j: (i, o_a + j)), pl.BlockSpec((tm, tc), lambda i, j: (i, o_b + j)), blk, blk],
        out_specs=blk, compiler_params=_cp("parallel", "parallel"),
    )(z, z, ya, yb)


def _ple_bwd_ew(dx, gp, e):
    S, D = dx.shape
    tm, tc = _tile(S, 512, 8), _tile(D, 1024)

    def body(dx_ref, gp_ref, e_ref, dgp_ref, de_ref):
        dxv = dx_ref[...]
        sg = _sigmoid(gp_ref[...].astype(f32))
        dgp_ref[...] = (dxv * e_ref[...].astype(f32) * sg * (1.0 - sg)).astype(bf16)
        de_ref[...] = (dxv * sg).astype(bf16)

    blk = pl.BlockSpec((tm, tc), lambda i, j: (i, j))
    return pl.pallas_call(
        body, name="ple_bwd_ew", out_shape=[jax.ShapeDtypeStruct((S, D), bf16)] * 2, grid=(S // tm, D // tc),
        in_specs=[blk, blk, blk], out_specs=[blk, blk], compiler_params=_cp("parallel", "parallel"),
    )(dx, gp, e)


def _adam_math(w, g, m, v):
    m = ADAM_B1 * m + (1.0 - ADAM_B1) * g
    v = ADAM_B2 * v + (1.0 - ADAM_B2) * (g * g)
    m_hat = m / (1.0 - ADAM_B1 ** ADAM_STEP)
    v_hat = v / (1.0 - ADAM_B2 ** ADAM_STEP)
    delta = -ADAM_LR * (m_hat / (jnp.sqrt(v_hat) + ADAM_EPS) + ADAM_WD * w)
    return delta, m, v


def _small_sum_adamw(gathered, w, m, v):
    _, R, _ = gathered.shape
    tr = _tile(R, 1024, SMALL_ROWS)

    def body(p_ref, w_ref, m_ref, v_ref, g_ref, d_ref, nm_ref, nv_ref):
        g = p_ref[0]
        for n in range(1, N_DEV):
            g = g + p_ref[n]
        d, nm, nv = _adam_math(w_ref[...], g, m_ref[...], v_ref[...])
        g_ref[...] = g
        d_ref[...] = d
        nm_ref[...] = nm
        nv_ref[...] = nv

    blk = pl.BlockSpec((tr, LANES), lambda i: (i, 0))
    return pl.pallas_call(
        body, name="small_sum_adamw", out_shape=[jax.ShapeDtypeStruct((R, LANES), f32)] * 4, grid=(R // tr,),
        in_specs=[pl.BlockSpec((N_DEV, tr, LANES), lambda i: (0, i, 0)), blk, blk, blk], out_specs=[blk] * 4,
        compiler_params=_cp("parallel"),
    )(gathered, w, m, v)


_ANY = pl.BlockSpec(memory_space=pl.ANY)


def _all_gather(name, shard, in_vmem=False):
    R, C = shard.shape

    def body(x_ref, out_ref, send_sems, recv_sems, local_sem):
        x, y, c = lax.axis_index("x"), lax.axis_index("y"), lax.axis_index("c")
        me, sibling = (x, y, c), (x, y, 1 - c)
        chips = [(1 - x, y), (x, 1 - y), (1 - x, 1 - y)]

        def rows(px, py, pc):
            return out_ref.at[4 * px + 2 * py + pc]

        def copy(k, block, to, src=None):
            return pltpu.make_async_remote_copy(
                src_ref=rows(*block) if src is None else src, dst_ref=rows(*block),
                send_sem=send_sems.at[k], recv_sem=recv_sems.at[k], device_id=to, device_id_type=MESH)

        mine = pltpu.make_async_copy(x_ref, rows(*me), local_sem)
        mine.start()
        first = [copy(0, me, sibling, src=x_ref)]
        first += [copy(1 + j, me, (*chip, c), src=x_ref) for j, chip in enumerate(chips)]
        for cp in first:
            cp.start()
        passed = [copy(4 + j, (*chip, c), sibling) for j, chip in enumerate(chips)]
        for j, chip in enumerate(chips):
            copy(1 + j, (*chip, c), me).wait_recv()
            passed[j].start()
        copy(0, sibling, me).wait_recv()
        for j, chip in enumerate(chips):
            copy(4 + j, (*chip, 1 - c), me).wait_recv()
        for cp in first + passed:
            cp.wait_send()
        mine.wait()

    space = pl.BlockSpec(memory_space=pltpu.VMEM) if in_vmem else _ANY
    return pl.pallas_call(
        body, name=name, out_shape=jax.ShapeDtypeStruct((N_DEV, R, C), shard.dtype),
        in_specs=[space], out_specs=space,
        scratch_shapes=[pltpu.SemaphoreType.DMA((7,)), pltpu.SemaphoreType.DMA((7,)), pltpu.SemaphoreType.DMA],
        compiler_params=pltpu.CompilerParams(has_side_effects=True, vmem_limit_bytes=VMEM_LIMIT),
    )(shard)


def _ag_weights(shards):
    n = len(shards)

    def body(*refs):
        x_refs, out_refs = refs[:n], refs[n:2 * n]
        send_sems, recv_sems, local_sems = refs[2 * n:]
        x, y, c = lax.axis_index("x"), lax.axis_index("y"), lax.axis_index("c")
        me, sibling = (x, y, c), (x, y, 1 - c)
        chips = [(1 - x, y), (x, 1 - y), (1 - x, 1 - y)]

        def rows(w, px, py, pc):
            return out_refs[w].at[4 * px + 2 * py + pc]

        def copy(w, k, block, to, src=None):
            return pltpu.make_async_remote_copy(
                src_ref=rows(w, *block) if src is None else src, dst_ref=rows(w, *block),
                send_sem=send_sems.at[w, k], recv_sem=recv_sems.at[w, k], device_id=to, device_id_type=MESH)

        mine = [pltpu.make_async_copy(x_refs[w], rows(w, *me), local_sems.at[w]) for w in range(n)]
        for cp in mine:
            cp.start()
        first = []
        for w in range(n):
            first.append(copy(w, 0, me, sibling, src=x_refs[w]))
            first += [copy(w, 1 + j, me, (*chip, c), src=x_refs[w]) for j, chip in enumerate(chips)]
        for cp in first:
            cp.start()
        passed = []
        for w in range(n):
            for j, chip in enumerate(chips):
                copy(w, 1 + j, (*chip, c), me).wait_recv()
                passed.append(copy(w, 4 + j, (*chip, c), sibling))
                passed[-1].start()
        for w in range(n):
            copy(w, 0, sibling, me).wait_recv()
            for j, chip in enumerate(chips):
                copy(w, 4 + j, (*chip, 1 - c), me).wait_recv()
        for cp in first + passed:
            cp.wait_send()
        for cp in mine:
            cp.wait()

    return pl.pallas_call(
        body, name="ag_weights", out_shape=[jax.ShapeDtypeStruct((N_DEV,) + s.shape, s.dtype) for s in shards],
        in_specs=[_ANY] * n, out_specs=[_ANY] * n,
        scratch_shapes=[pltpu.SemaphoreType.DMA((n, 7)), pltpu.SemaphoreType.DMA((n, 7)), pltpu.SemaphoreType.DMA((n,))],
        compiler_params=pltpu.CompilerParams(has_side_effects=True, vmem_limit_bytes=VMEM_LIMIT),
    )(*shards)


def _rs_sibling(gs):
    n = len(gs)

    def body(*refs):
        g_refs, recv_refs = refs[:n], refs[n:2 * n]
        send_sems, recv_sems = refs[2 * n:]
        x, y, c = lax.axis_index("x"), lax.axis_index("y"), lax.axis_index("c")
        sibling = (x, y, 1 - c)
        copies = [pltpu.make_async_remote_copy(
            src_ref=g_refs[w].at[2 * q + (1 - c)], dst_ref=recv_refs[w].at[q], send_sem=send_sems.at[w, q],
            recv_sem=recv_sems.at[w, q], device_id=sibling, device_id_type=MESH) for w in range(n) for q in range(4)]
        for cp in copies:
            cp.start()
        for cp in copies:
            cp.wait_recv()
        for cp in copies:
            cp.wait_send()

    return pl.pallas_call(
        body, name="rs_sibling", out_shape=[jax.ShapeDtypeStruct((4,) + g.shape[1:], g.dtype) for g in gs],
        in_specs=[_ANY] * n, out_specs=[_ANY] * n,
        scratch_shapes=[pltpu.SemaphoreType.DMA((n, 4)), pltpu.SemaphoreType.DMA((n, 4))],
        compiler_params=pltpu.CompilerParams(has_side_effects=True, vmem_limit_bytes=VMEM_LIMIT),
    )(*gs)


def _rs_chip_sum(name, g8, recv, c_idx):
    _, R, C = g8.shape
    tr = _tile(R, 512, 16)
    g42 = g8.reshape(4, 2, R, C)

    def body(c_ref, a_ref, b_ref, o_ref):
        o_ref[...] = (a_ref[...].astype(f32) + b_ref[...].astype(f32)).astype(o_ref.dtype)

    return pl.pallas_call(
        body, name=name, out_shape=jax.ShapeDtypeStruct((4, R, C), g8.dtype),
        grid_spec=pltpu.PrefetchScalarGridSpec(
            num_scalar_prefetch=1, grid=(4, R // tr),
            in_specs=[pl.BlockSpec((None, None, tr, C), lambda q, r, c_ref: (q, c_ref[0], r, 0)),
                      pl.BlockSpec((None, tr, C), lambda q, r, c_ref: (q, r, 0))],
            out_specs=pl.BlockSpec((None, tr, C), lambda q, r, c_ref: (q, r, 0))),
        compiler_params=_cp("parallel", "parallel"),
    )(c_idx, g42, recv)


def _rs_chips(p4s):
    n = len(p4s)

    def body(*refs):
        p_refs, recv_refs = refs[:n], refs[n:2 * n]
        send_sems, recv_sems = refs[2 * n:]
        x, y, c = lax.axis_index("x"), lax.axis_index("y"), lax.axis_index("c")
        chips = [(1 - x, y), (x, 1 - y), (1 - x, 1 - y)]
        copies = [pltpu.make_async_remote_copy(
            src_ref=p_refs[w].at[2 * cx + cy], dst_ref=recv_refs[w].at[k], send_sem=send_sems.at[w, k],
            recv_sem=recv_sems.at[w, k], device_id=(cx, cy, c), device_id_type=MESH)
            for w in range(n) for k, (cx, cy) in enumerate(chips)]
        for cp in copies:
            cp.start()
        for cp in copies:
            cp.wait_recv()
        for cp in copies:
            cp.wait_send()

    return pl.pallas_call(
        body, name="rs_chips", out_shape=[jax.ShapeDtypeStruct((3,) + p.shape[1:], p.dtype) for p in p4s],
        in_specs=[_ANY] * n, out_specs=[_ANY] * n,
        scratch_shapes=[pltpu.SemaphoreType.DMA((n, 3)), pltpu.SemaphoreType.DMA((n, 3))],
        compiler_params=pltpu.CompilerParams(has_side_effects=True, vmem_limit_bytes=VMEM_LIMIT),
    )(*p4s)


def _adamw_layer(name, layer, w, m, v, p4, recv, q_idx, prev):
    depth, R, C = w.shape
    tr = _tile(R, 256, 8)

    def body(q_ref, w_ref, m_ref, v_ref, a_ref, b_ref, *rest):
        g_ref, d_ref, nm_ref, nv_ref = rest[-4:]
        g = ((a_ref[...].astype(f32) + b_ref[0].astype(f32)) + b_ref[1].astype(f32)) + b_ref[2].astype(f32)
        d, nm, nv = _adam_math(w_ref[...], g, m_ref[...], v_ref[...])
        g_ref[...] = g
        d_ref[...] = d
        nm_ref[...] = nm
        nv_ref[...] = nv

    lay = pl.BlockSpec((None, tr, C), lambda i, q_ref: (layer, i, 0))
    n_prev = 0 if prev is None else 4
    return pl.pallas_call(
        body, name=name, out_shape=[jax.ShapeDtypeStruct((depth, R, C), f32)] * 4,
        grid_spec=pltpu.PrefetchScalarGridSpec(
            num_scalar_prefetch=1, grid=(R // tr,),
            in_specs=[lay, lay, lay, pl.BlockSpec((None, tr, C), lambda i, q_ref: (q_ref[0], i, 0)),
                      pl.BlockSpec((3, tr, C), lambda i, q_ref: (0, i, 0))] + [_ANY] * n_prev,
            out_specs=[lay] * 4),
        input_output_aliases={6 + n: n for n in range(n_prev)},
        compiler_params=_cp("parallel"),
    )(q_idx, w, m, v, p4, recv, *(prev or ()))


_BIG = (("w_in", 1), ("w_br_attn", 1), ("w_br_sg", 1), ("w_out", 0), ("w_ff_gate", 1), ("w_ff_up", 1),
        ("w_ff_down", 0), ("w_ple_gate", 0), ("w_ple", 1))


def _layer_fwd(x0, p_i, W, sm, tabs):
    S, D = x0.shape
    w_out, w_down, w_pg = W["w_out"].reshape(D, D), W["w_ff_down"].reshape(-1, D), W["w_ple_gate"].reshape(D, D)
    F = w_down.shape[0]
    cos, sin = tabs
    tmm = _tile(S, 1024, 8)
    same = lambda accs, ex: accs
    h1 = _norm_fwd("norm_fwd", x0, sm["norm_mix"])
    z = _mm_nn_cols("mm_in", h1, [W["w_in"]], [bf16], same)[0]
    IN = z.shape[1]

    def grp(base):
        return jnp.stack([_perm(z[:, base + g * ATTN_W: base + (g + 1) * ATTN_W], d) for g, d in enumerate(DILATIONS)])

    qg, kg, vg = grp(0), grp(N_GROUPS * ATTN_W), grp(2 * N_GROUPS * ATTN_W)
    o3, lse3 = _attn_fwd(qg, kg, vg, cos, sin)
    o3 = jnp.stack([_unperm(o3[g], d) for g, d in enumerate(DILATIONS)])
    lse3 = jnp.stack([_unperm(lse3[g], d) for g, d in enumerate(DILATIONS)])
    attn, lse = _attn_combine(o3, lse3)
    ya = _mm_nn_cols("mm_br_attn", attn, [W["w_br_attn"]], [bf16], same, tm_pref=1024)[0]
    sgw = sm["sg_w"].astype(bf16)
    bb = jnp.broadcast_to(sm["sg_b"][:, :, None], (SG_GROUPS, SG_CHUNK, 128))
    sg = _sg_fwd(z, sgw, bb, sm["sg_ln_g"], sm["sg_ln_b"])
    yb = _mm_nn_cols("mm_br_sg", sg, [W["w_br_sg"]], [bf16], same, tm_pref=1024)[0]
    merged = _merge_fwd(z, ya, yb, D)
    tn = _tile(D, 512)
    x1 = _matmul("mm_out", [(merged, w_out, "nn", 0)], S, D, tmm, tn, 1,
                 [((S, D), f32, (tmm, tn), lambda i, j: (i, j))], lambda accs, ex: [ex[0] + accs[0]],
                 extras=[(x0, (tmm, tn), lambda i, j: (i, j))])[0]
    h2 = _norm_fwd("norm_fwd", x1, sm["norm_ffn"])

    def ffn_ep(accs, ex):
        a, b = accs
        return [a, b, a * _sigmoid(a) * b]

    a, b, f = _mm_nn_cols("mm_ffn_in", h2, [W["w_ff_gate"], W["w_ff_up"]], [bf16] * 3, ffn_ep)
    nk = F // _tile(F, 512)
    x2 = _matmul("mm_ffn_out", [(f, w_down, "nn", 0)], S, D, tmm, tn, nk,
                 [((S, D), f32, (tmm, tn), lambda i, j: (i, j))], lambda accs, ex: [ex[0] + accs[0]],
                 extras=[(x1, (tmm, tn), lambda i, j: (i, j))])[0]
    h3 = _norm_fwd("norm_fwd", x2, sm["norm_ple"])

    e = _mm_nn_cols("mm_ple_emb", p_i, [W["w_ple"]], [bf16], same, tm_pref=1024)[0]

    def ple_ep(accs, ex):
        gp = accs[0]
        return [ex[0] + _sigmoid(gp) * ex[1].astype(f32), gp]

    x3, gp = _matmul("mm_ple", [(h3, w_pg, "nn", 0)], S, D, tmm, tn, 1,
                     [((S, D), f32, (tmm, tn), lambda i, j: (i, j)), ((S, D), bf16, (tmm, tn), lambda i, j: (i, j))],
                     ple_ep, extras=[(x2, (tmm, tn), lambda i, j: (i, j)), (e, (tmm, tn), lambda i, j: (i, j))])
    saved = dict(x0=x0, h1=h1, z=z, qg=qg, kg=kg, vg=vg, attn=attn, lse=lse, ya=ya, yb=yb, sg=sg, merged=merged, x1=x1,
                 h2=h2, a=a, b=b, f=f, x2=x2, h3=h3, gp=gp, e=e, sgw=sgw, bb=bb, IN=IN)
    return x3, saved


def _layer_bwd(dx3, p_i, W, sm, tabs, sv):
    S, D = dx3.shape
    w_out, w_down, w_pg = W["w_out"].reshape(D, D), W["w_ff_down"].reshape(-1, D), W["w_ple_gate"].reshape(D, D)
    F = w_down.shape[0]
    cos, sin = tabs
    tmm = _tile(S, 1024, 8)
    tn = _tile(D, 512)

    def blocks(full):
        return full.reshape(N_DEV, full.shape[0] // N_DEV, full.shape[1])

    dgp, de = _ple_bwd_ew(dx3, sv["gp"], sv["e"])
    d_w_ple = _mm_tn_cols("mm_dw_ple", p_i, [de], D // N_DEV)[0]
    d_w_pg = blocks(_mm_simple("mm_dw_dd", sv["h3"], dgp, "tn", bf16))
    dh3 = _mm_simple("mm_dh_dd", dgp, w_pg, "nt", bf16, 1024, 1024, 2048)
    dx2, dg_ple = _norm_bwd("norm_bwd", dh3, sv["x2"], sm["norm_ple"], dx3)
    tf = _tile(F, 512)

    def ffn_bwd_ep(accs, ex):
        df = accs[0]
        a, b = ex[0].astype(f32), ex[1].astype(f32)
        sg = _sigmoid(a)
        return [df * b * sg * (1.0 + a * (1.0 - sg)), df * a * sg]

    da, db = _matmul("mm_dffn", [(dx2, w_down, "nt", 0)], S, F, tmm, tf, 1,
                     [((S, F), bf16, (tmm, tf), lambda i, j: (i, j))] * 2, ffn_bwd_ep,
                     extras=[(sv["a"], (tmm, tf), lambda i, j: (i, j)), (sv["b"], (tmm, tf), lambda i, j: (i, j))])
    d_w_down = blocks(_mm_simple("mm_dw_fd", sv["f"], dx2, "tn", bf16))
    d_w_gate, d_w_up = _mm_tn_cols("mm_dw_df", sv["h2"], [da, db], F // N_DEV)
    dh2 = _mm_nt_cols("mm_dh_ffn", [(da, W["w_ff_gate"]), (db, W["w_ff_up"])], bf16)
    dx1, dg_ffn = _norm_bwd("norm_bwd", dh2, sv["x1"], sm["norm_ffn"], dx2)
    z = sv["z"]
    o_a, o_b = (QKV_W + 2 * SG_W) // tn, (QKV_W + 2 * SG_W + D) // tn

    def merge_bwd_ep(accs, ex):
        dm = accs[0]
        ga, gb = _sigmoid(ex[0].astype(f32)), _sigmoid(ex[1].astype(f32))
        ya, yb = ex[2].astype(f32), ex[3].astype(f32)
        return [dm * ya * ga * (1.0 - ga), dm * yb * gb * (1.0 - gb), dm * ga, dm * gb]

    dga, dgb, dya, dyb = _matmul(
        "mm_dmerge", [(dx1, w_out, "nt", 0)], S, D, tmm, tn, 1,
        [((S, D), bf16, (tmm, tn), lambda i, j: (i, j))] * 4, merge_bwd_ep,
        extras=[(z, (tmm, tn), lambda i, j: (i, o_a + j)), (z, (tmm, tn), lambda i, j: (i, o_b + j)),
                (sv["ya"], (tmm, tn), lambda i, j: (i, j)), (sv["yb"], (tmm, tn), lambda i, j: (i, j))])
    d_w_out = blocks(_mm_simple("mm_dw_dd", sv["merged"], dx1, "tn", bf16))
    dsg = _mm_nt_cols("mm_dsg", [(dyb, W["w_br_sg"])], bf16)
    d_w_bsg = _mm_tn_cols("mm_dw_bsg", sv["sg"], [dyb], D // N_DEV)[0]
    dattn = _mm_nt_cols("mm_dattn", [(dya, W["w_br_attn"])], bf16)
    d_w_battn = _mm_tn_cols("mm_dw_battn", sv["attn"], [dya], D // N_DEV)[0]
    sgwt = jnp.swapaxes(sm["sg_w"], 1, 2).astype(bf16)
    du, dv_sg, d_sgw, d_sgb, d_lg, d_lb = _sg_bwd(dsg, z, sv["sgw"], sgwt, sv["bb"], sm["sg_ln_g"], sm["sg_ln_b"])
    dl = _attn_delta(sv["attn"], dattn)

    def grp(t):
        return jnp.stack([_perm(t, d) for d in DILATIONS])

    dag, lg_, dlg_ = grp(dattn), grp(sv["lse"]), grp(dl)
    dqg = _attn_bwd_dq(sv["qg"], sv["kg"], sv["vg"], cos, sin, dag, lg_, dlg_)
    dkg, dvg = _attn_bwd_dkv(sv["qg"], sv["kg"], sv["vg"], cos, sin, dag, lg_, dlg_)

    def ungrp(t3):
        return [_unperm(t3[g], d) for g, d in enumerate(DILATIONS)]

    dz = jnp.concatenate(ungrp(dqg) + ungrp(dkg) + ungrp(dvg) + [du, dv_sg, dga, dgb], axis=1)
    d_w_in = _mm_tn_cols("mm_dw_in", sv["h1"], [dz], sv["IN"] // N_DEV)[0]
    dh1 = _mm_nt_cols("mm_dh_in", [(dz, W["w_in"])], bf16)
    dx0, dg_mix = _norm_bwd("norm_bwd", dh1, sv["x0"], sm["norm_mix"], dx1)
    big = dict(w_in=d_w_in, w_br_attn=d_w_battn, w_br_sg=d_w_bsg, w_out=d_w_out, w_ff_gate=d_w_gate, w_ff_up=d_w_up,
               w_ff_down=d_w_down, w_ple_gate=d_w_pg, w_ple=d_w_ple)
    small = dict(sg_w=d_sgw, sg_b=d_sgb[:, :, 0], sg_ln_g=d_lg[0], sg_ln_b=d_lb[0], norm_mix=dg_mix[0], norm_ffn=dg_ffn[0],
                 norm_ple=dg_ple[0])
    return dx0, [big[n] for n, _ in _BIG], small


_SMALL = ("sg_w", "sg_b", "sg_ln_g", "sg_ln_b", "norm_mix", "norm_ffn", "norm_ple", "norm_final")


SMALL_ROWS = 256


def _pack_small(parts, tail):
    rows = [parts[n].astype(f32).reshape(-1, LANES) for n in _SMALL] + [tail]
    n = sum(r.shape[0] for r in rows)
    return jnp.concatenate(rows + [jnp.zeros((-n % SMALL_ROWS, LANES), f32)], axis=0)


def kernel(x, p, w_in, w_br_attn, w_br_sg, w_out, sg_w, sg_b, sg_ln_g, sg_ln_b, norm_mix, norm_ffn, norm_ple, norm_final, w_ff_gate, w_ff_up, w_ff_down, w_ple_gate, w_ple, loss_target, m_w_in, m_w_br_attn, m_w_br_sg, m_w_out, m_sg_w, m_sg_b, m_sg_ln_g, m_sg_ln_b, m_norm_mix, m_norm_ffn, m_norm_ple, m_norm_final, m_w_ff_gate, m_w_ff_up, m_w_ff_down, m_w_ple_gate, m_w_ple, v_w_in, v_w_br_attn, v_w_br_sg, v_w_out, v_sg_w, v_sg_b, v_sg_ln_g, v_sg_ln_b, v_norm_mix, v_norm_ffn, v_norm_ple, v_norm_final, v_w_ff_gate, v_w_ff_up, v_w_ff_down, v_w_ple_gate, v_w_ple):
    wts = dict(w_in=w_in, w_br_attn=w_br_attn, w_br_sg=w_br_sg, w_out=w_out, w_ff_gate=w_ff_gate, w_ff_up=w_ff_up,
               w_ff_down=w_ff_down, w_ple_gate=w_ple_gate, w_ple=w_ple)
    mom_m = dict(w_in=m_w_in, w_br_attn=m_w_br_attn, w_br_sg=m_w_br_sg, w_out=m_w_out, w_ff_gate=m_w_ff_gate,
                 w_ff_up=m_w_ff_up, w_ff_down=m_w_ff_down, w_ple_gate=m_w_ple_gate, w_ple=m_w_ple)
    mom_v = dict(w_in=v_w_in, w_br_attn=v_w_br_attn, w_br_sg=v_w_br_sg, w_out=v_w_out, w_ff_gate=v_w_ff_gate,
                 w_ff_up=v_w_ff_up, w_ff_down=v_w_ff_down, w_ple_gate=v_w_ple_gate, w_ple=v_w_ple)
    small_w = dict(sg_w=sg_w, sg_b=sg_b, sg_ln_g=sg_ln_g, sg_ln_b=sg_ln_b, norm_mix=norm_mix, norm_ffn=norm_ffn,
                   norm_ple=norm_ple, norm_final=norm_final)
    small_m = dict(sg_w=m_sg_w, sg_b=m_sg_b, sg_ln_g=m_sg_ln_g, sg_ln_b=m_sg_ln_b, norm_mix=m_norm_mix, norm_ffn=m_norm_ffn,
                   norm_ple=m_norm_ple, norm_final=m_norm_final)
    small_v = dict(sg_w=v_sg_w, sg_b=v_sg_b, sg_ln_g=v_sg_ln_g, sg_ln_b=v_sg_ln_b, norm_mix=v_norm_mix, norm_ffn=v_norm_ffn,
                   norm_ple=v_norm_ple, norm_final=v_norm_final)
    depth = w_in.shape[0]
    S = x.shape[1]
    names = [n for n, _ in _BIG]
    c_idx = lax.axis_index("c").astype(jnp.int32).reshape(1)
    q_idx = (2 * lax.axis_index("x") + lax.axis_index("y")).astype(jnp.int32).reshape(1)
    tabs = _rope_tables(S)

    full = [dict(zip(names, _ag_weights([wts[n][i].astype(bf16) for n in names]))) for i in range(depth)]

    xs = x[0]
    saved = []
    for i in range(depth):
        sm = {n: small_w[n][i] for n in _SMALL if n != "norm_final"}
        xs, sv = _layer_fwd(xs, p[i, 0], full[i], sm, tabs)
        saved.append(sv)
    dx, dg_final, loss_part = _loss_head(xs, norm_final, loss_target[0])

    reduced = [None] * depth
    small_parts = [None] * depth
    for i in reversed(range(depth)):
        sm = {n: small_w[n][i] for n in _SMALL if n != "norm_final"}
        dx, bigs, small_parts[i] = _layer_bwd(dx, p[i, 0], full[i], sm, tabs, saved[i])
        recv1 = _rs_sibling(bigs)
        p4 = [_rs_chip_sum("rs_chip_sum_" + n, g, r1, c_idx) for n, g, r1 in zip(names, bigs, recv1)]
        reduced[i] = (p4, _rs_chips(p4))
    grad_x = dx[None]

    parts = {n: jnp.stack([small_parts[i][n] for i in range(depth)]) for n in _SMALL if n != "norm_final"}
    parts["norm_final"] = dg_final[0]
    gathered = _all_gather("ag_small", _pack_small(parts, loss_part), in_vmem=True)
    g_s, d_s, nm_s, nv_s = _small_sum_adamw(gathered, _pack_small(small_w, jnp.zeros((8, LANES), f32)),
                                            _pack_small(small_m, jnp.zeros((8, LANES), f32)),
                                            _pack_small(small_v, jnp.ones((8, LANES), f32)))
    loss = g_s[sum(small_w[n].size for n in _SMALL) // LANES, 0]

    def unpack_small(flat):
        out, off = {}, 0
        for n in _SMALL:
            k = small_w[n].size // LANES
            out[n] = flat[off:off + k].reshape(small_w[n].shape)
            off += k
        return out

    sm_g, sm_d, sm_nm, sm_nv = unpack_small(g_s), unpack_small(d_s), unpack_small(nm_s), unpack_small(nv_s)

    big_g, big_d, big_nm, big_nv = {}, {}, {}, {}
    for k, n in enumerate(names):
        outs = None
        for i in range(depth):
            p4, recv2 = reduced[i]
            outs = _adamw_layer(f"adamw_{n}_{i}", i, wts[n], mom_m[n], mom_v[n], p4[k], recv2[k], q_idx, outs)
        big_g[n], big_d[n], big_nm[n], big_nv[n] = outs

    order = ["w_in", "w_br_attn", "w_br_sg", "w_out", "sg_w", "sg_b", "sg_ln_g", "sg_ln_b", "norm_mix", "norm_ffn", "norm_ple",
             "norm_final", "w_ff_gate", "w_ff_up", "w_ff_down", "w_ple_gate", "w_ple"]

    def pick(big, small):
        return [big[n] if n in big else small[n] for n in order]

    return (loss, grad_x, *pick(big_g, sm_g), *pick(big_d, sm_d), *pick(big_nm, sm_nm), *pick(big_nv, sm_nv))
```

```python
import functools
import math

import jax
import jax.numpy as jnp
from jax import lax
from jax.experimental import pallas as pl
from jax.experimental.pallas import tpu as pltpu

f32 = jnp.float32
bf16 = jnp.bfloat16

HEAD_DIM = 128
N_GROUPS = 3
HEADS = 4
DILATIONS = (1, 4, 16)
RADIUS = 64
BLK = 128
QKV_W = 3 * N_GROUPS * HEADS * HEAD_DIM
ATTN_W = HEADS * HEAD_DIM
SG_CHUNK = 128
SG_GROUPS = 8
SG_W = SG_GROUPS * 128
ROPE_THETA = 10000.0
EPS = 1e-6
NEG = -1e30
N_DEV = 8
LANES = 128

ADAM_LR = 0.001
ADAM_B1 = 0.9
ADAM_B2 = 0.999
ADAM_EPS = 1e-08
ADAM_WD = 0.01
ADAM_STEP = 10

VMEM_LIMIT = 52 * 1024 * 1024
MESH = pl.DeviceIdType.MESH

NN = (((1,), (0,)), ((), ()))
NT = (((1,), (1,)), ((), ()))
TN = (((0,), (0,)), ((), ()))
_DN = {"nn": NN, "nt": NT, "tn": TN}


def _cp(*sem):
    return pltpu.CompilerParams(dimension_semantics=sem, vmem_limit_bytes=VMEM_LIMIT)


def _tile(n, pref, unit=128):
    if n <= pref:
        return n
    t = (pref // unit) * unit
    while t >= unit:
        if n % t == 0:
            return t
        t -= unit
    return n


_ANY = pl.BlockSpec(memory_space=pl.ANY)


def _call(body, *, name, grid, in_specs, out_specs, out_shape, operands, scratch=(), comm=None):
    in_specs, out_specs, out_shape, scratch = list(in_specs), list(out_specs), list(out_shape), list(scratch)
    if comm is None:
        res = pl.pallas_call(
            body, name=name, out_shape=out_shape, grid=grid, in_specs=in_specs, out_specs=out_specs, scratch_shapes=scratch,
            compiler_params=_cp(*(("arbitrary",) * len(grid))))(*operands)
        return res, []
    n_in, n_out, n_scr = len(in_specs), len(out_specs), len(scratch)
    c_in, c_out = len(comm.ins), len(comm.outs)

    def hosted(*refs):
        own_in, refs = refs[:n_in], refs[n_in:]
        ex_in, refs = refs[:c_in], refs[c_in:]
        own_out, refs = refs[:n_out], refs[n_out:]
        ex_out, refs = refs[:c_out], refs[c_out:]
        own_scr, sems = refs[:n_scr], refs[n_scr:]
        ids = [pl.program_id(a) for a in range(len(grid))]
        first = functools.reduce(jnp.logical_and, [i == 0 for i in ids])
        last = functools.reduce(jnp.logical_and, [i == g - 1 for i, g in zip(ids, grid)])

        @pl.when(first)
        def _():
            comm.start(ex_in, ex_out, sems)

        body(*own_in, *own_out, *own_scr)

        @pl.when(last)
        def _():
            comm.finish(ex_in, ex_out, sems)

    res = pl.pallas_call(
        hosted, name=name, out_shape=out_shape + list(comm.outs), grid=grid,
        in_specs=in_specs + [_ANY] * c_in, out_specs=out_specs + [_ANY] * c_out, scratch_shapes=scratch + list(comm.sems),
        compiler_params=pltpu.CompilerParams(dimension_semantics=("arbitrary",) * len(grid), vmem_limit_bytes=VMEM_LIMIT,
                                             has_side_effects=True),
    )(*operands, *comm.ins)
    return res[:n_out], res[n_out:]


def _sigmoid(x):
    return 1.0 / (1.0 + jnp.exp(-x))


_GC = math.sqrt(2.0 / math.pi)
_GA = 0.044715


def _gelu(x):
    return 0.5 * x * (1.0 + jnp.tanh(_GC * (x + _GA * x * x * x)))


def _gelu_grad(x):
    t = jnp.tanh(_GC * (x + _GA * x * x * x))
    return 0.5 * (1.0 + t) + 0.5 * x * (1.0 - t * t) * _GC * (1.0 + 3.0 * _GA * x * x)


def _matmul(name, prods, M, N, tm, tn, nk, outs, epilogue, extras=(), n_acc=1, comm=None):
    in_specs, operands, metas = [], [], []
    for a, b, mode, acc in prods:
        if mode == "tn":
            tk = a.shape[0] // nk
            in_specs += [pl.BlockSpec((tk, tm), lambda i, j, k: (k, i)), pl.BlockSpec((tk, tn), lambda i, j, k: (k, j))]
        elif mode == "nt":
            tk = a.shape[1] // nk
            in_specs += [pl.BlockSpec((tm, tk), lambda i, j, k: (i, k)), pl.BlockSpec((tn, tk), lambda i, j, k: (j, k))]
        else:
            tk = a.shape[1] // nk
            in_specs += [pl.BlockSpec((tm, tk), lambda i, j, k: (i, k)), pl.BlockSpec((tk, tn), lambda i, j, k: (k, j))]
        operands += [a, b]
        metas.append((mode, acc))
    for arr, bshape, imap in extras:
        in_specs.append(pl.BlockSpec(bshape, functools.partial(lambda i, j, k, f: f(i, j), f=imap)))
        operands.append(arr)
    out_specs = [pl.BlockSpec(bs, functools.partial(lambda i, j, k, f: f(i, j), f=imap)) for _, _, bs, imap in outs]
    out_shape = [jax.ShapeDtypeStruct(s, d) for s, d, _, _ in outs]
    n_prod, n_ext, n_out = len(prods), len(extras), len(outs)

    def body(*refs):
        in_refs = refs[: 2 * n_prod]
        ex_refs = refs[2 * n_prod : 2 * n_prod + n_ext]
        out_refs = refs[2 * n_prod + n_ext : 2 * n_prod + n_ext + n_out]
        acc_refs = refs[2 * n_prod + n_ext + n_out :]

        def partials():
            res = [None] * n_acc
            for idx, (mode, acc) in enumerate(metas):
                a = in_refs[2 * idx][...].astype(bf16)
                b = in_refs[2 * idx + 1][...].astype(bf16)
                d = lax.dot_general(a, b, _DN[mode], preferred_element_type=f32)
                res[acc] = d if res[acc] is None else res[acc] + d
            return res

        def finish(accs):
            vals = epilogue(accs, [r[...] for r in ex_refs])
            for r, v in zip(out_refs, vals):
                r[...] = v.astype(r.dtype)

        if nk == 1:
            finish(partials())
        else:
            k = pl.program_id(2)

            @pl.when(k == 0)
            def _():
                for r in acc_refs:
                    r[...] = jnp.zeros_like(r)

            for r, d in zip(acc_refs, partials()):
                r[...] += d

            @pl.when(k == nk - 1)
            def _():
                finish([r[...] for r in acc_refs])

    scratch = [pltpu.VMEM((tm, tn), f32) for _ in range(n_acc)] if nk > 1 else []
    res, ex = _call(body, name=name, grid=(M // tm, N // tn, nk), in_specs=in_specs, out_specs=out_specs,
                    out_shape=out_shape, operands=operands, scratch=scratch, comm=comm)
    return res if comm is None else (res, ex)


def _ident(accs, ex):
    return [accs[0]]


def _mm_simple(name, a, b, mode, out_dtype, tm_pref=1024, tn_pref=1024, tk_pref=1024, comm=None):
    if mode == "tn":
        K, M = a.shape
        N = b.shape[1]
    elif mode == "nt":
        M, K = a.shape
        N = b.shape[0]
    else:
        M, K = a.shape
        N = b.shape[1]
    tm, tn, tk = _tile(M, tm_pref), _tile(N, tn_pref), _tile(K, tk_pref)
    res = _matmul(name, [(a, b, mode, 0)], M, N, tm, tn, K // tk,
                  [((M, N), out_dtype, (tm, tn), lambda i, j: (i, j))], _ident, comm=comm)
    return res[0] if comm is None else (res[0][0], res[1])


def _group(c, width_pref=1024):
    g = LANES // math.gcd(c, LANES)
    while g < N_DEV and 2 * g * c <= width_pref:
        g *= 2
    return g


def _join(parts):
    return parts[0] if len(parts) == 1 else jnp.concatenate(parts, axis=1)


def _mm_nn_cols(name, a, gs_list, outs_dtypes, epilogue, extras=(), tm_pref=512, width_pref=1024, comm=None):
    M, K = a.shape
    c = gs_list[0].shape[2]
    g = _group(c, width_pref)
    W = g * c
    tm = _tile(M, tm_pref, 8)
    n_g, n_ex, n_out = len(gs_list), len(extras), len(outs_dtypes)

    def body(*refs):
        a_ref = refs[0]
        g_refs = refs[1:1 + n_g]
        ex_refs = refs[1 + n_g:1 + n_g + n_ex]
        out_refs = refs[1 + n_g + n_ex:]
        av = a_ref[...].astype(bf16)
        accs = [_join([lax.dot_general(av, gr[s], NN, preferred_element_type=f32) for s in range(g)]) for gr in g_refs]
        for r, v in zip(out_refs, epilogue(accs, [r[...] for r in ex_refs])):
            r[...] = v.astype(r.dtype)

    tile = pl.BlockSpec((tm, W), lambda j, i: (i, j))
    res, ex = _call(
        body, name=name, out_shape=[jax.ShapeDtypeStruct((M, N_DEV * c), d) for d in outs_dtypes],
        grid=(N_DEV // g, M // tm),
        in_specs=[pl.BlockSpec((tm, K), lambda j, i: (i, 0))]
        + [pl.BlockSpec((None, g, K, c), lambda j, i: (j, 0, 0, 0))] * n_g + [tile] * n_ex,
        out_specs=[tile] * n_out, operands=[a, *[gm.reshape(N_DEV // g, g, K, c) for gm in gs_list], *extras], comm=comm)
    return res if comm is None else (res, ex)


def _mm_nt_cols(name, pairs, out_dtype, tm_pref=1024, tn_pref=1024, width_pref=1024, comm=None):
    M = pairs[0][0].shape[0]
    Kw, c = pairs[0][1].shape[1:]
    g = _group(c, width_pref)
    W = g * c
    tm, tn = _tile(M, tm_pref, 8), _tile(Kw, tn_pref)
    nk = N_DEV // g
    n_p = len(pairs)

    def body(*refs):
        o_ref, acc = refs[2 * n_p], refs[2 * n_p + 1]
        k = pl.program_id(2)

        @pl.when(k == 0)
        def _():
            acc[...] = jnp.zeros_like(acc)

        tot = None
        for n in range(n_p):
            d_ref, g_ref = refs[2 * n], refs[2 * n + 1]
            for s in range(g):
                part = lax.dot_general(d_ref[:, s * c:(s + 1) * c], g_ref[s], NT, preferred_element_type=f32)
                tot = part if tot is None else tot + part
        acc[...] += tot

        @pl.when(k == nk - 1)
        def _():
            o_ref[...] = acc[...].astype(o_ref.dtype)

    in_specs, operands = [], []
    for d, gm in pairs:
        in_specs += [pl.BlockSpec((tm, W), lambda i, j, k: (i, k)), pl.BlockSpec((None, g, tn, c), lambda i, j, k: (k, 0, j, 0))]
        operands += [d, gm.reshape(nk, g, Kw, c)]
    res, ex = _call(
        body, name=name, out_shape=[jax.ShapeDtypeStruct((M, Kw), out_dtype)], grid=(M // tm, Kw // tn, nk),
        in_specs=in_specs, out_specs=[pl.BlockSpec((tm, tn), lambda i, j, k: (i, j))],
        scratch=[pltpu.VMEM((tm, tn), f32)], operands=operands, comm=comm)
    return res[0] if comm is None else (res[0], ex)


def _mm_tn_cols(name, x, ds, c, tm_pref=1024, tk_pref=1024, width_pref=1024, comm=None):
    S, Kw = x.shape
    g = _group(c, width_pref)
    W = g * c
    tm, tk = _tile(Kw, tm_pref), _tile(S, tk_pref, 16)
    nk = S // tk
    n_d = len(ds)

    def body(*refs):
        x_ref = refs[0]
        d_refs = refs[1:1 + n_d]
        o_refs = refs[1 + n_d:1 + 2 * n_d]
        accs = refs[1 + 2 * n_d:]
        k = pl.program_id(2)

        @pl.when(k == 0)
        def _():
            for acc in accs:
                acc[...] = jnp.zeros_like(acc)

        xv = x_ref[...].astype(bf16)
        for d_ref, acc in zip(d_refs, accs):
            for s in range(g):
                acc[s] += lax.dot_general(xv, d_ref[:, s * c:(s + 1) * c], TN, preferred_element_type=f32)

        @pl.when(k == nk - 1)
        def _():
            for o_ref, acc in zip(o_refs, accs):
                o_ref[...] = acc[...].astype(o_ref.dtype)

    outs, ex = _call(
        body, name=name, out_shape=[jax.ShapeDtypeStruct((N_DEV // g, g, Kw, c), bf16)] * n_d,
        grid=(Kw // tm, N_DEV // g, nk),
        in_specs=[pl.BlockSpec((tk, tm), lambda i, j, k: (k, i))] + [pl.BlockSpec((tk, W), lambda i, j, k: (k, j))] * n_d,
        out_specs=[pl.BlockSpec((None, g, tm, c), lambda i, j, k: (j, 0, i, 0))] * n_d,
        scratch=[pltpu.VMEM((g, tm, c), f32)] * n_d, operands=[x, *ds], comm=comm)
    outs = [o.reshape(N_DEV, Kw, c) for o in outs]
    return outs if comm is None else (outs, ex)


def _norm_fwd(name, x, g):
    S, D = x.shape
    tm = _tile(S, 512, 8)

    def body(x_ref, g_ref, h_ref):
        xv = x_ref[...]
        r = lax.rsqrt(jnp.mean(xv * xv, axis=-1, keepdims=True) + EPS)
        h_ref[...] = (xv * r * g_ref[...]).astype(bf16)

    return pl.pallas_call(
        body, name=name, out_shape=jax.ShapeDtypeStruct((S, D), bf16), grid=(S // tm,),
        in_specs=[pl.BlockSpec((tm, D), lambda i: (i, 0)), pl.BlockSpec((1, D), lambda i: (0, 0))],
        out_specs=pl.BlockSpec((tm, D), lambda i: (i, 0)), compiler_params=_cp("parallel"),
    )(x, g.reshape(1, D))


def _norm_bwd(name, dh, x, g, dx_in):
    S, D = x.shape
    tm = _tile(S, 256, 8)

    def body(dh_ref, x_ref, g_ref, dxi_ref, dx_ref, dg_ref):
        i = pl.program_id(0)
        xv = x_ref[...]
        r = lax.rsqrt(jnp.mean(xv * xv, axis=-1, keepdims=True) + EPS)
        xh = xv * r
        dhv = dh_ref[...].astype(f32)
        dxh = dhv * g_ref[...]
        dx_ref[...] = dxi_ref[...] + r * (dxh - xh * jnp.mean(dxh * xh, axis=-1, keepdims=True))

        @pl.when(i == 0)
        def _():
            dg_ref[...] = jnp.zeros_like(dg_ref)

        dg_ref[...] += jnp.sum(dhv * xh, axis=0, keepdims=True)

    return pl.pallas_call(
        body, name=name, out_shape=[jax.ShapeDtypeStruct((S, D), f32), jax.ShapeDtypeStruct((1, D), f32)], grid=(S // tm,),
        in_specs=[pl.BlockSpec((tm, D), lambda i: (i, 0)), pl.BlockSpec((tm, D), lambda i: (i, 0)),
                  pl.BlockSpec((1, D), lambda i: (0, 0)), pl.BlockSpec((tm, D), lambda i: (i, 0))],
        out_specs=[pl.BlockSpec((tm, D), lambda i: (i, 0)), pl.BlockSpec((1, D), lambda i: (0, 0))],
        compiler_params=_cp("arbitrary"),
    )(dh, x, g.reshape(1, D), dx_in)


def _loss_head(x, g, t):
    S, D = x.shape
    tm = _tile(S, 256, 8)

    def body(x_ref, g_ref, t_ref, dx_ref, dg_ref, loss_ref):
        i = pl.program_id(0)
        xv = x_ref[...]
        r = lax.rsqrt(jnp.mean(xv * xv, axis=-1, keepdims=True) + EPS)
        xh = xv * r
        gv = g_ref[...]
        err = xh * gv - t_ref[...]
        dy = err * (1.0 / D)
        dxh = dy * gv
        dx_ref[...] = r * (dxh - xh * jnp.mean(dxh * xh, axis=-1, keepdims=True))

        @pl.when(i == 0)
        def _():
            dg_ref[...] = jnp.zeros_like(dg_ref)
            loss_ref[...] = jnp.zeros_like(loss_ref)

        dg_ref[...] += jnp.sum(dy * xh, axis=0, keepdims=True)
        row = jnp.sum(err * err, axis=-1, keepdims=True) * (0.5 / D)
        loss_ref[...] += jnp.broadcast_to(jnp.sum(row, axis=0, keepdims=True), loss_ref.shape)

    return pl.pallas_call(
        body, name="loss_head",
        out_shape=[jax.ShapeDtypeStruct((S, D), f32), jax.ShapeDtypeStruct((1, D), f32), jax.ShapeDtypeStruct((8, LANES), f32)],
        grid=(S // tm,),
        in_specs=[pl.BlockSpec((tm, D), lambda i: (i, 0)), pl.BlockSpec((1, D), lambda i: (0, 0)), pl.BlockSpec((tm, D), lambda i: (i, 0))],
        out_specs=[pl.BlockSpec((tm, D), lambda i: (i, 0)), pl.BlockSpec((1, D), lambda i: (0, 0)), pl.BlockSpec((8, LANES), lambda i: (0, 0))],
        compiler_params=_cp("arbitrary"),
    )(x, g.reshape(1, D), t)


def _perm(t, d):
    if d == 1:
        return t
    S, C = t.shape
    return t.reshape(S // d, d, C).transpose(1, 0, 2).reshape(S, C)


def _unperm(t, d):
    if d == 1:
        return t
    S, C = t.shape
    return t.reshape(d, S // d, C).transpose(1, 0, 2).reshape(S, C)


def _rope_tables(S):
    half = HEAD_DIM // 2
    pos = jnp.arange(S, dtype=f32)
    inv_freq = ROPE_THETA ** (-jnp.arange(0, HEAD_DIM, 2, dtype=f32) / HEAD_DIM)
    ang = pos[:, None] * inv_freq[None, :]
    c, s = jnp.cos(ang), jnp.sin(ang)
    cos2 = jnp.concatenate([c, c], axis=-1)
    sin2 = jnp.concatenate([-s, s], axis=-1)
    assert cos2.shape == (S, 2 * half)
    return (jnp.stack([_perm(cos2, d) for d in DILATIONS]), jnp.stack([_perm(sin2, d) for d in DILATIONS]))


def _rope(t, c, s):
    return t * c + pltpu.roll(t, HEAD_DIM // 2, 1) * s


def _rope_bwd(dt, c, s):
    return dt * c - pltpu.roll(dt, HEAD_DIM // 2, 1) * s


def _band_bounds(i, nblk):
    g = pl.program_id(0)
    lb = jnp.right_shift(jnp.int32(nblk), 2 * g)
    pos = lax.rem(i, lb)
    lo = jnp.where(pos == 0, BLK, 0)
    hi = jnp.where(pos == lb - 1, 2 * BLK, 3 * BLK)
    return lo, hi


def _band_specs(width, nblk):
    prev = pl.BlockSpec((None, BLK, width), lambda g, i: (g, jnp.maximum(i - 1, 0), 0))
    cur = pl.BlockSpec((None, BLK, width), lambda g, i: (g, i, 0))
    nxt = pl.BlockSpec((None, BLK, width), lambda g, i: (g, jnp.minimum(i + 1, nblk - 1), 0))
    return [prev, cur, nxt]


_SCALE = HEAD_DIM ** -0.5


def _attn_fwd(q, k, v, cos, sin):
    _, S, W = q.shape
    nblk = S // BLK

    def body(q_ref, kp, kc, kn, vp, vc, vn, cq, sq, ckp, ckc, ckn, skp, skc, skn, o_ref, lse_ref):
        i = pl.program_id(1)
        lo, hi = _band_bounds(i, nblk)
        a = lax.broadcasted_iota(jnp.int32, (BLK, 3 * BLK), 0)
        b = lax.broadcasted_iota(jnp.int32, (BLK, 3 * BLK), 1)
        mask = (jnp.abs(b - BLK - a) <= RADIUS) & (b >= lo) & (b < hi)
        ck = jnp.concatenate([ckp[...], ckc[...], ckn[...]], axis=0)
        sk = jnp.concatenate([skp[...], skc[...], skn[...]], axis=0)
        for hh in range(HEADS):
            sl = slice(hh * HEAD_DIM, (hh + 1) * HEAD_DIM)
            qh = _rope(q_ref[:, sl].astype(f32), cq[...], sq[...]).astype(bf16)
            kh = jnp.concatenate([kp[:, sl], kc[:, sl], kn[:, sl]], axis=0).astype(f32)
            kh = _rope(kh, ck, sk).astype(bf16)
            vh = jnp.concatenate([vp[:, sl], vc[:, sl], vn[:, sl]], axis=0)
            s = lax.dot_general(qh, kh, NT, preferred_element_type=f32) * _SCALE
            s = jnp.where(mask, s, NEG)
            m = jnp.max(s, axis=-1, keepdims=True)
            e = jnp.exp(s - m)
            den = jnp.sum(e, axis=-1, keepdims=True)
            o = lax.dot_general(e.astype(bf16), vh, NN, preferred_element_type=f32) * (1.0 / den)
            o_ref[:, sl] = o.astype(bf16)
            lse_ref[:, sl] = jnp.broadcast_to(m + jnp.log(den), (BLK, HEAD_DIM))

    blk = pl.BlockSpec((None, BLK, W), lambda g, i: (g, i, 0))
    tab = pl.BlockSpec((None, BLK, HEAD_DIM), lambda g, i: (g, i, 0))
    return pl.pallas_call(
        body, name="attn_fwd",
        out_shape=[jax.ShapeDtypeStruct((N_GROUPS, S, W), bf16), jax.ShapeDtypeStruct((N_GROUPS, S, W), f32)],
        grid=(N_GROUPS, nblk),
        in_specs=[blk] + _band_specs(W, nblk) * 2 + [tab, tab] + _band_specs(HEAD_DIM, nblk) * 2,
        out_specs=[blk, blk], compiler_params=_cp("parallel", "parallel"),
    )(q, k, k, k, v, v, v, cos, sin, cos, cos, cos, sin, sin, sin)


def _attn_combine(o3, lse3):
    _, S, W = o3.shape
    tm = _tile(S, 512, 8)

    def body(o_ref, l_ref, attn_ref, lse_ref):
        l0, l1, l2 = l_ref[0], l_ref[1], l_ref[2]
        m = jnp.maximum(jnp.maximum(l0, l1), l2)
        w0, w1, w2 = jnp.exp(l0 - m), jnp.exp(l1 - m), jnp.exp(l2 - m)
        den = w0 + w1 + w2
        acc = w0 * o_ref[0].astype(f32) + w1 * o_ref[1].astype(f32) + w2 * o_ref[2].astype(f32)
        attn_ref[...] = (acc * (1.0 / den)).astype(bf16)
        lse_ref[...] = m + jnp.log(den)

    blk3 = pl.BlockSpec((N_GROUPS, tm, W), lambda i: (0, i, 0))
    blk = pl.BlockSpec((tm, W), lambda i: (i, 0))
    return pl.pallas_call(
        body, name="attn_combine", out_shape=[jax.ShapeDtypeStruct((S, W), bf16), jax.ShapeDtypeStruct((S, W), f32)],
        grid=(S // tm,), in_specs=[blk3, blk3], out_specs=[blk, blk], compiler_params=_cp("parallel"),
    )(o3, lse3)


def _attn_delta(attn, dattn):
    S, W = attn.shape
    tm = _tile(S, 512, 8)

    def body(a_ref, d_ref, o_ref):
        prod = a_ref[...].astype(f32) * d_ref[...].astype(f32)
        for hh in range(HEADS):
            sl = slice(hh * HEAD_DIM, (hh + 1) * HEAD_DIM)
            o_ref[:, sl] = jnp.broadcast_to(jnp.sum(prod[:, sl], axis=-1, keepdims=True), (tm, HEAD_DIM))

    blk = pl.BlockSpec((tm, W), lambda i: (i, 0))
    return pl.pallas_call(
        body, name="attn_delta", out_shape=jax.ShapeDtypeStruct((S, W), f32), grid=(S // tm,),
        in_specs=[blk, blk], out_specs=blk, compiler_params=_cp("parallel"),
    )(attn, dattn)


def _attn_bwd_dq(q, k, v, cos, sin, da, lse, dl):
    _, S, W = q.shape
    nblk = S // BLK

    def body(q_ref, kp, kc, kn, vp, vc, vn, cq, sq, ckp, ckc, ckn, skp, skc, skn, da_ref, l_ref, dl_ref, dq_ref):
        i = pl.program_id(1)
        lo, hi = _band_bounds(i, nblk)
        a = lax.broadcasted_iota(jnp.int32, (BLK, 3 * BLK), 0)
        b = lax.broadcasted_iota(jnp.int32, (BLK, 3 * BLK), 1)
        mask = (jnp.abs(b - BLK - a) <= RADIUS) & (b >= lo) & (b < hi)
        ck = jnp.concatenate([ckp[...], ckc[...], ckn[...]], axis=0)
        sk = jnp.concatenate([skp[...], skc[...], skn[...]], axis=0)
        for hh in range(HEADS):
            sl = slice(hh * HEAD_DIM, (hh + 1) * HEAD_DIM)
            qh = _rope(q_ref[:, sl].astype(f32), cq[...], sq[...]).astype(bf16)
            kh = jnp.concatenate([kp[:, sl], kc[:, sl], kn[:, sl]], axis=0).astype(f32)
            kh = _rope(kh, ck, sk).astype(bf16)
            vh = jnp.concatenate([vp[:, sl], vc[:, sl], vn[:, sl]], axis=0)
            s = lax.dot_general(qh, kh, NT, preferred_element_type=f32) * _SCALE
            lh = l_ref[:, sl]
            l3 = jnp.concatenate([lh, lh, lh], axis=1)
            p = jnp.exp(jnp.where(mask, s - l3, NEG))
            dp = lax.dot_general(da_ref[:, sl], vh, NT, preferred_element_type=f32)
            dh = dl_ref[:, sl]
            ds = p * (dp - jnp.concatenate([dh, dh, dh], axis=1))
            dqh = lax.dot_general(ds.astype(bf16), kh, NN, preferred_element_type=f32) * _SCALE
            dq_ref[:, sl] = _rope_bwd(dqh, cq[...], sq[...]).astype(bf16)

    blk = pl.BlockSpec((None, BLK, W), lambda g, i: (g, i, 0))
    tab = pl.BlockSpec((None, BLK, HEAD_DIM), lambda g, i: (g, i, 0))
    return pl.pallas_call(
        body, name="attn_bwd_dq", out_shape=jax.ShapeDtypeStruct((N_GROUPS, S, W), bf16), grid=(N_GROUPS, nblk),
        in_specs=[blk] + _band_specs(W, nblk) * 2 + [tab, tab] + _band_specs(HEAD_DIM, nblk) * 2 + [blk, blk, blk],
        out_specs=blk, compiler_params=_cp("parallel", "parallel"),
    )(q, k, k, k, v, v, v, cos, sin, cos, cos, cos, sin, sin, sin, da, lse, dl)


def _attn_bwd_dkv(q, k, v, cos, sin, da, lse, dl):
    _, S, W = q.shape
    nblk = S // BLK

    def body(k_ref, v_ref, ck, sk, qp, qc, qn, cqp, cqc, cqn, sqp, sqc, sqn, dap, dac, dan, lp, lc, ln, dlp, dlc, dln,
             dk_ref, dv_ref):
        j = pl.program_id(1)
        lo, hi = _band_bounds(j, nblk)
        a = lax.broadcasted_iota(jnp.int32, (3 * BLK, BLK), 0)
        b = lax.broadcasted_iota(jnp.int32, (3 * BLK, BLK), 1)
        mask = (jnp.abs(b - (a - BLK)) <= RADIUS) & (a >= lo) & (a < hi)
        cq = jnp.concatenate([cqp[...], cqc[...], cqn[...]], axis=0)
        sq = jnp.concatenate([sqp[...], sqc[...], sqn[...]], axis=0)
        for hh in range(HEADS):
            sl = slice(hh * HEAD_DIM, (hh + 1) * HEAD_DIM)
            kh = _rope(k_ref[:, sl].astype(f32), ck[...], sk[...]).astype(bf16)
            vh = v_ref[:, sl]
            qh = jnp.concatenate([qp[:, sl], qc[:, sl], qn[:, sl]], axis=0).astype(f32)
            qh = _rope(qh, cq, sq).astype(bf16)
            dah = jnp.concatenate([dap[:, sl], dac[:, sl], dan[:, sl]], axis=0)
            lh = jnp.concatenate([lp[:, sl], lc[:, sl], ln[:, sl]], axis=0)
            dlh = jnp.concatenate([dlp[:, sl], dlc[:, sl], dln[:, sl]], axis=0)
            s = lax.dot_general(qh, kh, NT, preferred_element_type=f32) * _SCALE
            p = jnp.exp(jnp.where(mask, s - lh, NEG))
            dv_ref[:, sl] = lax.dot_general(p.astype(bf16), dah, TN, preferred_element_type=f32).astype(bf16)
            dp = lax.dot_general(dah, vh, NT, preferred_element_type=f32)
            ds = p * (dp - dlh)
            dkh = lax.dot_general(ds.astype(bf16), qh, TN, preferred_element_type=f32) * _SCALE
            dk_ref[:, sl] = _rope_bwd(dkh, ck[...], sk[...]).astype(bf16)

    blk = pl.BlockSpec((None, BLK, W), lambda g, i: (g, i, 0))
    tab = pl.BlockSpec((None, BLK, HEAD_DIM), lambda g, i: (g, i, 0))
    bw, bt = _band_specs(W, nblk), _band_specs(HEAD_DIM, nblk)
    return pl.pallas_call(
        body, name="attn_bwd_dkv",
        out_shape=[jax.ShapeDtypeStruct((N_GROUPS, S, W), bf16), jax.ShapeDtypeStruct((N_GROUPS, S, W), bf16)],
        grid=(N_GROUPS, nblk),
        in_specs=[blk, blk, tab, tab] + bw + bt + bt + bw + bw + bw,
        out_specs=[blk, blk], compiler_params=_cp("parallel", "parallel"),
    )(k, v, cos, sin, q, q, q, cos, cos, cos, sin, sin, sin, da, da, da, lse, lse, lse, dl, dl, dl)


_SG_ROWS = 512


def _sg_z_specs(tm, half):
    o = QKV_W // half
    return [pl.BlockSpec((tm, half), functools.partial(lambda i, c: (i, c), c=o + n)) for n in range(4)]


def _sg_norm(v, lg, lb):
    gv = _gelu(v)
    mu = jnp.mean(gv, axis=-1, keepdims=True)
    xc = gv - mu
    rstd = lax.rsqrt(jnp.mean(xc * xc, axis=-1, keepdims=True) + EPS)
    xh = xc * rstd
    return xh, rstd, xh * lg + lb


def _sg_fwd(z, w, bb, lg, lb):
    S = z.shape[0]
    tm = _tile(S, _SG_ROWS, SG_CHUNK)
    half = SG_W // 2

    def body(u0, u1, v0, v1, w_ref, bb_ref, lg_ref, lb_ref, o_ref):
        u = jnp.concatenate([u0[...], u1[...]], axis=1).astype(f32)
        v = jnp.concatenate([v0[...], v1[...]], axis=1).astype(f32)
        gu = _gelu(u)
        _, _, vn = _sg_norm(v, lg_ref[...], lb_ref[...])
        vnb = vn.astype(bf16)
        for c in range(tm // SG_CHUNK):
            rs = slice(c * SG_CHUNK, (c + 1) * SG_CHUNK)
            for g in range(SG_GROUPS):
                cs = slice(g * 128, (g + 1) * 128)
                mixed = lax.dot_general(w_ref[g], vnb[rs, cs], NN, preferred_element_type=f32) + bb_ref[g]
                o_ref[rs, cs] = (gu[rs, cs] * mixed).astype(bf16)

    full3 = pl.BlockSpec((SG_GROUPS, 128, 128), lambda i: (0, 0, 0))
    vec = pl.BlockSpec((1, SG_W), lambda i: (0, 0))
    return pl.pallas_call(
        body, name="sg_fwd", out_shape=jax.ShapeDtypeStruct((S, SG_W), bf16), grid=(S // tm,),
        in_specs=_sg_z_specs(tm, half) + [full3, full3, vec, vec],
        out_specs=pl.BlockSpec((tm, SG_W), lambda i: (i, 0)), compiler_params=_cp("parallel"),
    )(z, z, z, z, w, bb, lg.reshape(1, SG_W), lb.reshape(1, SG_W))


def _sg_bwd(dsg, z, w, wt, bb, lg, lb):
    S = z.shape[0]
    tm = _tile(S, _SG_ROWS, SG_CHUNK)
    half = SG_W // 2

    def body(d_ref, u0, u1, v0, v1, w_ref, wt_ref, bb_ref, lg_ref, lb_ref, du_ref, dv_ref, dw_ref, db_ref, dlg_ref, dlb_ref, dvn_scr):
        i = pl.program_id(0)

        @pl.when(i == 0)
        def _():
            dw_ref[...] = jnp.zeros_like(dw_ref)
            db_ref[...] = jnp.zeros_like(db_ref)
            dlg_ref[...] = jnp.zeros_like(dlg_ref)
            dlb_ref[...] = jnp.zeros_like(dlb_ref)

        u = jnp.concatenate([u0[...], u1[...]], axis=1).astype(f32)
        v = jnp.concatenate([v0[...], v1[...]], axis=1).astype(f32)
        gu = _gelu(u)
        dgu = _gelu_grad(u)
        xh, rstd, vn = _sg_norm(v, lg_ref[...], lb_ref[...])
        vnb = vn.astype(bf16)
        dsg_v = d_ref[...].astype(f32)
        for g in range(SG_GROUPS):
            cs = slice(g * 128, (g + 1) * 128)
            dw_g = jnp.zeros((128, 128), f32)
            db_g = jnp.zeros((128, 1), f32)
            for c in range(tm // SG_CHUNK):
                rs = slice(c * SG_CHUNK, (c + 1) * SG_CHUNK)
                ds = dsg_v[rs, cs]
                mixed = lax.dot_general(w_ref[g], vnb[rs, cs], NN, preferred_element_type=f32) + bb_ref[g]
                du_ref[rs, cs] = (ds * mixed * dgu[rs, cs]).astype(bf16)
                dmix = ds * gu[rs, cs]
                dmb = dmix.astype(bf16)
                dw_g = dw_g + lax.dot_general(dmb, vnb[rs, cs], NT, preferred_element_type=f32)
                db_g = db_g + jnp.sum(dmix, axis=-1, keepdims=True)
                dvn_scr[rs, cs] = lax.dot_general(wt_ref[g], dmb, NN, preferred_element_type=f32)
            dw_ref[g] += dw_g
            db_ref[g] += jnp.broadcast_to(db_g, (128, 128))
        dvn = dvn_scr[...]
        dlg_ref[...] += jnp.sum(dvn * xh, axis=0, keepdims=True)
        dlb_ref[...] += jnp.sum(dvn, axis=0, keepdims=True)
        dxh = dvn * lg_ref[...]
        dgv = rstd * (dxh - jnp.mean(dxh, axis=-1, keepdims=True) - xh * jnp.mean(dxh * xh, axis=-1, keepdims=True))
        dv_ref[...] = (dgv * _gelu_grad(v)).astype(bf16)

    full3 = pl.BlockSpec((SG_GROUPS, 128, 128), lambda i: (0, 0, 0))
    vec = pl.BlockSpec((1, SG_W), lambda i: (0, 0))
    row = pl.BlockSpec((tm, SG_W), lambda i: (i, 0))
    return pl.pallas_call(
        body, name="sg_bwd",
        out_shape=[jax.ShapeDtypeStruct((S, SG_W), bf16), jax.ShapeDtypeStruct((S, SG_W), bf16),
                   jax.ShapeDtypeStruct((SG_GROUPS, 128, 128), f32), jax.ShapeDtypeStruct((SG_GROUPS, 128, 128), f32),
                   jax.ShapeDtypeStruct((1, SG_W), f32), jax.ShapeDtypeStruct((1, SG_W), f32)],
        grid=(S // tm,),
        in_specs=[row] + _sg_z_specs(tm, half) + [full3, full3, full3, vec, vec],
        out_specs=[row, row, full3, full3, vec, vec],
        scratch_shapes=[pltpu.VMEM((tm, SG_W), f32)], compiler_params=_cp("arbitrary"),
    )(dsg, z, z, z, z, w, wt, bb, lg.reshape(1, SG_W), lb.reshape(1, SG_W))


def _merge_fwd(z, ya, yb, D):
    S = z.shape[0]
    tm, tc = _tile(S, 512, 8), _tile(D, 512)
    o_a, o_b = (QKV_W + 2 * SG_W) // tc, (QKV_W + 2 * SG_W + D) // tc

    def body(ga_ref, gb_ref, ya_ref, yb_ref, o_ref):
        ga = _sigmoid(ga_ref[...].astype(f32))
        gb = _sigmoid(gb_ref[...].astype(f32))
        o_ref[...] = (ga * ya_ref[...].astype(f32) + gb * yb_ref[...].astype(f32)).astype(bf16)

    blk = pl.BlockSpec((tm, tc), lambda i, j: (i, j))
    return pl.pallas_call(
        body, name="merge_fwd", out_shape=jax.ShapeDtypeStruct((S, D), bf16), grid=(S // tm, D // tc),
        in_specs=[pl.BlockSpec((tm, tc), lambda i, j: (i, o_a + j)), pl.BlockSpec((tm, tc), lambda i, j: (i, o_b + j)), blk, blk],
        out_specs=blk, compiler_params=_cp("parallel", "parallel"),
    )(z, z, ya, yb)


def _ple_bwd_ew(dx, gp, e):
    S, D = dx.shape
    tm, tc = _tile(S, 512, 8), _tile(D, 1024)

    def body(dx_ref, gp_ref, e_ref, dgp_ref, de_ref):
        dxv = dx_ref[...]
        sg = _sigmoid(gp_ref[...].astype(f32))
        dgp_ref[...] = (dxv * e_ref[...].astype(f32) * sg * (1.0 - sg)).astype(bf16)
        de_ref[...] = (dxv * sg).astype(bf16)

    blk = pl.BlockSpec((tm, tc), lambda i, j: (i, j))
    return pl.pallas_call(
        body, name="ple_bwd_ew", out_shape=[jax.ShapeDtypeStruct((S, D), bf16)] * 2, grid=(S // tm, D // tc),
        in_specs=[blk, blk, blk], out_specs=[blk, blk], compiler_params=_cp("parallel", "parallel"),
    )(dx, gp, e)


def _adam_math(w, g, m, v):
    m = ADAM_B1 * m + (1.0 - ADAM_B1) * g
    v = ADAM_B2 * v + (1.0 - ADAM_B2) * (g * g)
    m_hat = m / (1.0 - ADAM_B1 ** ADAM_STEP)
    v_hat = v / (1.0 - ADAM_B2 ** ADAM_STEP)
    delta = -ADAM_LR * (m_hat / (jnp.sqrt(v_hat) + ADAM_EPS) + ADAM_WD * w)
    return delta, m, v


def _small_sum_adamw(gathered, w, m, v):
    _, R, _ = gathered.shape
    tr = _tile(R, 1024, SMALL_ROWS)

    def body(p_ref, w_ref, m_ref, v_ref, g_ref, d_ref, nm_ref, nv_ref):
        g = p_ref[0]
        for n in range(1, N_DEV):
            g = g + p_ref[n]
        d, nm, nv = _adam_math(w_ref[...], g, m_ref[...], v_ref[...])
        g_ref[...] = g
        d_ref[...] = d
        nm_ref[...] = nm
        nv_ref[...] = nv

    blk = pl.BlockSpec((tr, LANES), lambda i: (i, 0))
    return pl.pallas_call(
        body, name="small_sum_adamw", out_shape=[jax.ShapeDtypeStruct((R, LANES), f32)] * 4, grid=(R // tr,),
        in_specs=[pl.BlockSpec((N_DEV, tr, LANES), lambda i: (0, i, 0)), blk, blk, blk], out_specs=[blk] * 4,
        compiler_params=_cp("parallel"),
    )(gathered, w, m, v)


def _all_gather(name, shard, in_vmem=False):
    R, C = shard.shape

    def body(x_ref, out_ref, send_sems, recv_sems, local_sem):
        x, y, c = lax.axis_index("x"), lax.axis_index("y"), lax.axis_index("c")
        me, sibling = (x, y, c), (x, y, 1 - c)
        chips = [(1 - x, y), (x, 1 - y), (1 - x, 1 - y)]

        def rows(px, py, pc):
            return out_ref.at[4 * px + 2 * py + pc]

        def copy(k, block, to, src=None):
            return pltpu.make_async_remote_copy(
                src_ref=rows(*block) if src is None else src, dst_ref=rows(*block),
                send_sem=send_sems.at[k], recv_sem=recv_sems.at[k], device_id=to, device_id_type=MESH)

        mine = pltpu.make_async_copy(x_ref, rows(*me), local_sem)
        mine.start()
        first = [copy(0, me, sibling, src=x_ref)]
        first += [copy(1 + j, me, (*chip, c), src=x_ref) for j, chip in enumerate(chips)]
        for cp in first:
            cp.start()
        passed = [copy(4 + j, (*chip, c), sibling) for j, chip in enumerate(chips)]
        for j, chip in enumerate(chips):
            copy(1 + j, (*chip, c), me).wait_recv()
            passed[j].start()
        copy(0, sibling, me).wait_recv()
        for j, chip in enumerate(chips):
            copy(4 + j, (*chip, 1 - c), me).wait_recv()
        for cp in first + passed:
            cp.wait_send()
        mine.wait()

    space = pl.BlockSpec(memory_space=pltpu.VMEM) if in_vmem else _ANY
    return pl.pallas_call(
        body, name=name, out_shape=jax.ShapeDtypeStruct((N_DEV, R, C), shard.dtype),
        in_specs=[space], out_specs=space,
        scratch_shapes=[pltpu.SemaphoreType.DMA((7,)), pltpu.SemaphoreType.DMA((7,)), pltpu.SemaphoreType.DMA],
        compiler_params=pltpu.CompilerParams(has_side_effects=True, vmem_limit_bytes=VMEM_LIMIT),
    )(shard)


class _Exchange:
    def __init__(self, ins, outs, sems, start, finish):
        self.ins, self.outs, self.sems, self.start, self.finish = list(ins), list(outs), list(sems), start, finish


def _run_exchange(name, ex):
    c_in, c_out = len(ex.ins), len(ex.outs)

    def body(*refs):
        ins, outs, sems = refs[:c_in], refs[c_in:c_in + c_out], refs[c_in + c_out:]
        ex.start(ins, outs, sems)
        ex.finish(ins, outs, sems)

    return pl.pallas_call(
        body, name=name, out_shape=ex.outs, in_specs=[_ANY] * c_in, out_specs=[_ANY] * c_out, scratch_shapes=ex.sems,
        compiler_params=pltpu.CompilerParams(has_side_effects=True, vmem_limit_bytes=VMEM_LIMIT),
    )(*ex.ins)


def _gather_exchange(shards):
    n = len(shards)

    def plan(ins, outs, sems):
        send_sems, recv_sems, local_sems = sems
        x, y, c = lax.axis_index("x"), lax.axis_index("y"), lax.axis_index("c")
        me, sibling = (x, y, c), (x, y, 1 - c)
        chips = [(1 - x, y), (x, 1 - y), (1 - x, 1 - y)]

        def rows(w, px, py, pc):
            return outs[w].at[4 * px + 2 * py + pc]

        def copy(w, k, block, to, src=None):
            return pltpu.make_async_remote_copy(
                src_ref=rows(w, *block) if src is None else src, dst_ref=rows(w, *block),
                send_sem=send_sems.at[w, k], recv_sem=recv_sems.at[w, k], device_id=to, device_id_type=MESH)

        mine = [pltpu.make_async_copy(ins[w], rows(w, *me), local_sems.at[w]) for w in range(n)]
        first = []
        for w in range(n):
            first.append(copy(w, 0, me, sibling, src=ins[w]))
            first += [copy(w, 1 + j, me, (*chip, c), src=ins[w]) for j, chip in enumerate(chips)]
        return c, me, sibling, chips, copy, mine, first

    def start(ins, outs, sems):
        _, _, _, _, _, mine, first = plan(ins, outs, sems)
        for cp in mine + first:
            cp.start()

    def finish(ins, outs, sems):
        c, me, sibling, chips, copy, mine, first = plan(ins, outs, sems)
        passed = []
        for w in range(n):
            for j, chip in enumerate(chips):
                copy(w, 1 + j, (*chip, c), me).wait_recv()
                passed.append(copy(w, 4 + j, (*chip, c), sibling))
                passed[-1].start()
        for w in range(n):
            copy(w, 0, sibling, me).wait_recv()
            for j, chip in enumerate(chips):
                copy(w, 4 + j, (*chip, 1 - c), me).wait_recv()
        for cp in first + passed:
            cp.wait_send()
        for cp in mine:
            cp.wait()

    return _Exchange(
        shards, [jax.ShapeDtypeStruct((N_DEV,) + s.shape, s.dtype) for s in shards],
        [pltpu.SemaphoreType.DMA((n, 7)), pltpu.SemaphoreType.DMA((n, 7)), pltpu.SemaphoreType.DMA((n,))], start, finish)


def _sibling_exchange(gs):
    n = len(gs)

    def copies(ins, outs, sems):
        send_sems, recv_sems = sems
        x, y, c = lax.axis_index("x"), lax.axis_index("y"), lax.axis_index("c")
        return [pltpu.make_async_remote_copy(
            src_ref=ins[w].at[2 * q + (1 - c)], dst_ref=outs[w].at[q], send_sem=send_sems.at[w, q],
            recv_sem=recv_sems.at[w, q], device_id=(x, y, 1 - c), device_id_type=MESH) for w in range(n) for q in range(4)]

    def start(ins, outs, sems):
        for cp in copies(ins, outs, sems):
            cp.start()

    def finish(ins, outs, sems):
        cps = copies(ins, outs, sems)
        for cp in cps:
            cp.wait_recv()
        for cp in cps:
            cp.wait_send()

    return _Exchange(gs, [jax.ShapeDtypeStruct((4,) + g.shape[1:], g.dtype) for g in gs],
                     [pltpu.SemaphoreType.DMA((n, 4)), pltpu.SemaphoreType.DMA((n, 4))], start, finish)


def _rs_chip_sum(name, g8, recv, c_idx):
    _, R, C = g8.shape
    tr = _tile(R, 512, 16)
    g42 = g8.reshape(4, 2, R, C)

    def body(c_ref, a_ref, b_ref, o_ref):
        o_ref[...] = (a_ref[...].astype(f32) + b_ref[...].astype(f32)).astype(o_ref.dtype)

    return pl.pallas_call(
        body, name=name, out_shape=jax.ShapeDtypeStruct((4, R, C), g8.dtype),
        grid_spec=pltpu.PrefetchScalarGridSpec(
            num_scalar_prefetch=1, grid=(4, R // tr),
            in_specs=[pl.BlockSpec((None, None, tr, C), lambda q, r, c_ref: (q, c_ref[0], r, 0)),
                      pl.BlockSpec((None, tr, C), lambda q, r, c_ref: (q, r, 0))],
            out_specs=pl.BlockSpec((None, tr, C), lambda q, r, c_ref: (q, r, 0))),
        compiler_params=_cp("parallel", "parallel"),
    )(c_idx, g42, recv)


def _chips_exchange(p4s):
    n = len(p4s)

    def copies(ins, outs, sems):
        send_sems, recv_sems = sems
        x, y, c = lax.axis_index("x"), lax.axis_index("y"), lax.axis_index("c")
        chips = [(1 - x, y), (x, 1 - y), (1 - x, 1 - y)]
        return [pltpu.make_async_remote_copy(
            src_ref=ins[w].at[2 * cx + cy], dst_ref=outs[w].at[k], send_sem=send_sems.at[w, k],
            recv_sem=recv_sems.at[w, k], device_id=(cx, cy, c), device_id_type=MESH)
            for w in range(n) for k, (cx, cy) in enumerate(chips)]

    def start(ins, outs, sems):
        for cp in copies(ins, outs, sems):
            cp.start()

    def finish(ins, outs, sems):
        cps = copies(ins, outs, sems)
        for cp in cps:
            cp.wait_recv()
        for cp in cps:
            cp.wait_send()

    return _Exchange(p4s, [jax.ShapeDtypeStruct((3,) + p.shape[1:], p.dtype) for p in p4s],
                     [pltpu.SemaphoreType.DMA((n, 3)), pltpu.SemaphoreType.DMA((n, 3))], start, finish)


def _adamw_layer(name, layer, w, m, v, p4, recv, q_idx, prev):
    depth, R, C = w.shape
    tr = _tile(R, 256, 8)

    def body(q_ref, w_ref, m_ref, v_ref, a_ref, b_ref, *rest):
        g_ref, d_ref, nm_ref, nv_ref = rest[-4:]
        g = ((a_ref[...].astype(f32) + b_ref[0].astype(f32)) + b_ref[1].astype(f32)) + b_ref[2].astype(f32)
        d, nm, nv = _adam_math(w_ref[...], g, m_ref[...], v_ref[...])
        g_ref[...] = g
        d_ref[...] = d
        nm_ref[...] = nm
        nv_ref[...] = nv

    lay = pl.BlockSpec((None, tr, C), lambda i, q_ref: (layer, i, 0))
    n_prev = 0 if prev is None else 4
    return pl.pallas_call(
        body, name=name, out_shape=[jax.ShapeDtypeStruct((depth, R, C), f32)] * 4,
        grid_spec=pltpu.PrefetchScalarGridSpec(
            num_scalar_prefetch=1, grid=(R // tr,),
            in_specs=[lay, lay, lay, pl.BlockSpec((None, tr, C), lambda i, q_ref: (q_ref[0], i, 0)),
                      pl.BlockSpec((3, tr, C), lambda i, q_ref: (0, i, 0))] + [_ANY] * n_prev,
            out_specs=[lay] * 4),
        input_output_aliases={6 + n: n for n in range(n_prev)},
        compiler_params=_cp("parallel"),
    )(q_idx, w, m, v, p4, recv, *(prev or ()))


_BIG = (("w_in", 1), ("w_br_attn", 1), ("w_br_sg", 1), ("w_out", 0), ("w_ff_gate", 1), ("w_ff_up", 1),
        ("w_ff_down", 0), ("w_ple_gate", 0), ("w_ple", 1))
_THIRDS = (("w_in",), ("w_ff_gate", "w_ff_up"), ("w_br_attn", "w_br_sg", "w_out", "w_ff_down", "w_ple_gate", "w_ple"))
_FWD_CARRIERS = ("mm_in", "mm_ffn_in", "mm_ffn_out")
_BWD_CARRIERS = ("mm_dw_df", "mm_dh_ffn", "mm_dw_in")


def _layer_fwd(x0, p_i, W, sm, tabs, comm=None):
    S, D = x0.shape
    w_out, w_down, w_pg = W["w_out"].reshape(D, D), W["w_ff_down"].reshape(-1, D), W["w_ple_gate"].reshape(D, D)
    F = w_down.shape[0]
    cos, sin = tabs
    tmm = _tile(S, 1024, 8)
    same = lambda accs, ex: accs
    comm, carried = comm or {}, {}

    def hosted(key, fn):
        if key not in comm:
            return fn(None)
        res, carried[key] = fn(comm[key])
        return res

    h1 = _norm_fwd("norm_fwd", x0, sm["norm_mix"])
    z = hosted("mm_in", lambda ex: _mm_nn_cols("mm_in", h1, [W["w_in"]], [bf16], same, comm=ex))[0]
    IN = z.shape[1]

    def grp(base):
        return jnp.stack([_perm(z[:, base + g * ATTN_W: base + (g + 1) * ATTN_W], d) for g, d in enumerate(DILATIONS)])

    qg, kg, vg = grp(0), grp(N_GROUPS * ATTN_W), grp(2 * N_GROUPS * ATTN_W)
    o3, lse3 = _attn_fwd(qg, kg, vg, cos, sin)
    o3 = jnp.stack([_unperm(o3[g], d) for g, d in enumerate(DILATIONS)])
    lse3 = jnp.stack([_unperm(lse3[g], d) for g, d in enumerate(DILATIONS)])
    attn, lse = _attn_combine(o3, lse3)
    ya = _mm_nn_cols("mm_br_attn", attn, [W["w_br_attn"]], [bf16], same, tm_pref=1024)[0]
    sgw = sm["sg_w"].astype(bf16)
    bb = jnp.broadcast_to(sm["sg_b"][:, :, None], (SG_GROUPS, SG_CHUNK, 128))
    sg = _sg_fwd(z, sgw, bb, sm["sg_ln_g"], sm["sg_ln_b"])
    yb = _mm_nn_cols("mm_br_sg", sg, [W["w_br_sg"]], [bf16], same, tm_pref=1024)[0]
    merged = _merge_fwd(z, ya, yb, D)
    tn = _tile(D, 512)
    x1 = _matmul("mm_out", [(merged, w_out, "nn", 0)], S, D, tmm, tn, 1,
                 [((S, D), f32, (tmm, tn), lambda i, j: (i, j))], lambda accs, ex: [ex[0] + accs[0]],
                 extras=[(x0, (tmm, tn), lambda i, j: (i, j))])[0]
    h2 = _norm_fwd("norm_fwd", x1, sm["norm_ffn"])

    def ffn_ep(accs, ex):
        a, b = accs
        return [a, b, a * _sigmoid(a) * b]

    a, b, f = hosted("mm_ffn_in", lambda ex: _mm_nn_cols("mm_ffn_in", h2, [W["w_ff_gate"], W["w_ff_up"]], [bf16] * 3, ffn_ep,
                                                         comm=ex))
    nk = F // _tile(F, 512)
    x2 = hosted("mm_ffn_out", lambda ex: _matmul(
        "mm_ffn_out", [(f, w_down, "nn", 0)], S, D, tmm, tn, nk, [((S, D), f32, (tmm, tn), lambda i, j: (i, j))],
        lambda accs, ex_tiles: [ex_tiles[0] + accs[0]], extras=[(x1, (tmm, tn), lambda i, j: (i, j))], comm=ex))[0]
    h3 = _norm_fwd("norm_fwd", x2, sm["norm_ple"])

    e = _mm_nn_cols("mm_ple_emb", p_i, [W["w_ple"]], [bf16], same, tm_pref=1024)[0]

    def ple_ep(accs, ex):
        gp = accs[0]
        return [ex[0] + _sigmoid(gp) * ex[1].astype(f32), gp]

    x3, gp = _matmul("mm_ple", [(h3, w_pg, "nn", 0)], S, D, tmm, tn, 1,
                     [((S, D), f32, (tmm, tn), lambda i, j: (i, j)), ((S, D), bf16, (tmm, tn), lambda i, j: (i, j))],
                     ple_ep, extras=[(x2, (tmm, tn), lambda i, j: (i, j)), (e, (tmm, tn), lambda i, j: (i, j))])
    saved = dict(x0=x0, h1=h1, z=z, qg=qg, kg=kg, vg=vg, attn=attn, lse=lse, ya=ya, yb=yb, sg=sg, merged=merged, x1=x1,
                 h2=h2, a=a, b=b, f=f, x2=x2, h3=h3, gp=gp, e=e, sgw=sgw, bb=bb, IN=IN)
    return x3, saved, carried


def _layer_bwd(dx3, p_i, W, sm, tabs, sv, comm=None):
    S, D = dx3.shape
    w_out, w_down, w_pg = W["w_out"].reshape(D, D), W["w_ff_down"].reshape(-1, D), W["w_ple_gate"].reshape(D, D)
    F = w_down.shape[0]
    cos, sin = tabs
    tmm = _tile(S, 1024, 8)
    tn = _tile(D, 512)

    comm, carried = comm or {}, {}

    def hosted(key, fn):
        if key not in comm:
            return fn(None)
        res, carried[key] = fn(comm[key])
        return res

    def blocks(full):
        return full.reshape(N_DEV, full.shape[0] // N_DEV, full.shape[1])

    dgp, de = _ple_bwd_ew(dx3, sv["gp"], sv["e"])
    d_w_ple = _mm_tn_cols("mm_dw_ple", p_i, [de], D // N_DEV)[0]
    d_w_pg = blocks(_mm_simple("mm_dw_dd", sv["h3"], dgp, "tn", bf16))
    dh3 = _mm_simple("mm_dh_dd", dgp, w_pg, "nt", bf16, 1024, 1024, 2048)
    dx2, dg_ple = _norm_bwd("norm_bwd", dh3, sv["x2"], sm["norm_ple"], dx3)
    tf = _tile(F, 512)

    def ffn_bwd_ep(accs, ex):
        df = accs[0]
        a, b = ex[0].astype(f32), ex[1].astype(f32)
        sg = _sigmoid(a)
        return [df * b * sg * (1.0 + a * (1.0 - sg)), df * a * sg]

    da, db = _matmul("mm_dffn", [(dx2, w_down, "nt", 0)], S, F, tmm, tf, 1,
                     [((S, F), bf16, (tmm, tf), lambda i, j: (i, j))] * 2, ffn_bwd_ep,
                     extras=[(sv["a"], (tmm, tf), lambda i, j: (i, j)), (sv["b"], (tmm, tf), lambda i, j: (i, j))])
    d_w_down = blocks(_mm_simple("mm_dw_fd", sv["f"], dx2, "tn", bf16))
    d_w_gate, d_w_up = hosted("mm_dw_df", lambda ex: _mm_tn_cols("mm_dw_df", sv["h2"], [da, db], F // N_DEV, comm=ex))
    dh2 = hosted("mm_dh_ffn", lambda ex: _mm_nt_cols("mm_dh_ffn", [(da, W["w_ff_gate"]), (db, W["w_ff_up"])], bf16, comm=ex))
    dx1, dg_ffn = _norm_bwd("norm_bwd", dh2, sv["x1"], sm["norm_ffn"], dx2)
    z = sv["z"]
    o_a, o_b = (QKV_W + 2 * SG_W) // tn, (QKV_W + 2 * SG_W + D) // tn

    def merge_bwd_ep(accs, ex):
        dm = accs[0]
        ga, gb = _sigmoid(ex[0].astype(f32)), _sigmoid(ex[1].astype(f32))
        ya, yb = ex[2].astype(f32), ex[3].astype(f32)
        return [dm * ya * ga * (1.0 - ga), dm * yb * gb * (1.0 - gb), dm * ga, dm * gb]

    dga, dgb, dya, dyb = _matmul(
        "mm_dmerge", [(dx1, w_out, "nt", 0)], S, D, tmm, tn, 1,
        [((S, D), bf16, (tmm, tn), lambda i, j: (i, j))] * 4, merge_bwd_ep,
        extras=[(z, (tmm, tn), lambda i, j: (i, o_a + j)), (z, (tmm, tn), lambda i, j: (i, o_b + j)),
                (sv["ya"], (tmm, tn), lambda i, j: (i, j)), (sv["yb"], (tmm, tn), lambda i, j: (i, j))])
    d_w_out = blocks(_mm_simple("mm_dw_dd", sv["merged"], dx1, "tn", bf16))
    dsg = _mm_nt_cols("mm_dsg", [(dyb, W["w_br_sg"])], bf16)
    d_w_bsg = _mm_tn_cols("mm_dw_bsg", sv["sg"], [dyb], D // N_DEV)[0]
    dattn = _mm_nt_cols("mm_dattn", [(dya, W["w_br_attn"])], bf16)
    d_w_battn = _mm_tn_cols("mm_dw_battn", sv["attn"], [dya], D // N_DEV)[0]
    sgwt = jnp.swapaxes(sm["sg_w"], 1, 2).astype(bf16)
    du, dv_sg, d_sgw, d_sgb, d_lg, d_lb = _sg_bwd(dsg, z, sv["sgw"], sgwt, sv["bb"], sm["sg_ln_g"], sm["sg_ln_b"])
    dl = _attn_delta(sv["attn"], dattn)

    def grp(t):
        return jnp.stack([_perm(t, d) for d in DILATIONS])

    dag, lg_, dlg_ = grp(dattn), grp(sv["lse"]), grp(dl)
    dqg = _attn_bwd_dq(sv["qg"], sv["kg"], sv["vg"], cos, sin, dag, lg_, dlg_)
    dkg, dvg = _attn_bwd_dkv(sv["qg"], sv["kg"], sv["vg"], cos, sin, dag, lg_, dlg_)

    def ungrp(t3):
        return [_unperm(t3[g], d) for g, d in enumerate(DILATIONS)]

    dz = jnp.concatenate(ungrp(dqg) + ungrp(dkg) + ungrp(dvg) + [du, dv_sg, dga, dgb], axis=1)
    d_w_in = hosted("mm_dw_in", lambda ex: _mm_tn_cols("mm_dw_in", sv["h1"], [dz], sv["IN"] // N_DEV, comm=ex))[0]
    dh1 = hosted("mm_dh_in", lambda ex: _mm_nt_cols("mm_dh_in", [(dz, W["w_in"])], bf16, comm=ex))
    dx0, dg_mix = _norm_bwd("norm_bwd", dh1, sv["x0"], sm["norm_mix"], dx1)
    big = dict(w_in=d_w_in, w_br_attn=d_w_battn, w_br_sg=d_w_bsg, w_out=d_w_out, w_ff_gate=d_w_gate, w_ff_up=d_w_up,
               w_ff_down=d_w_down, w_ple_gate=d_w_pg, w_ple=d_w_ple)
    small = dict(sg_w=d_sgw, sg_b=d_sgb[:, :, 0], sg_ln_g=d_lg[0], sg_ln_b=d_lb[0], norm_mix=dg_mix[0], norm_ffn=dg_ffn[0],
                 norm_ple=dg_ple[0])
    return dx0, [big[n] for n, _ in _BIG], small, carried


_SMALL = ("sg_w", "sg_b", "sg_ln_g", "sg_ln_b", "norm_mix", "norm_ffn", "norm_ple", "norm_final")


SMALL_ROWS = 256


def _pack_small(parts, tail):
    rows = [parts[n].astype(f32).reshape(-1, LANES) for n in _SMALL] + [tail]
    n = sum(r.shape[0] for r in rows)
    return jnp.concatenate(rows + [jnp.zeros((-n % SMALL_ROWS, LANES), f32)], axis=0)


def kernel(x, p, w_in, w_br_attn, w_br_sg, w_out, sg_w, sg_b, sg_ln_g, sg_ln_b, norm_mix, norm_ffn, norm_ple, norm_final, w_ff_gate, w_ff_up, w_ff_down, w_ple_gate, w_ple, loss_target, m_w_in, m_w_br_attn, m_w_br_sg, m_w_out, m_sg_w, m_sg_b, m_sg_ln_g, m_sg_ln_b, m_norm_mix, m_norm_ffn, m_norm_ple, m_norm_final, m_w_ff_gate, m_w_ff_up, m_w_ff_down, m_w_ple_gate, m_w_ple, v_w_in, v_w_br_attn, v_w_br_sg, v_w_out, v_sg_w, v_sg_b, v_sg_ln_g, v_sg_ln_b, v_norm_mix, v_norm_ffn, v_norm_ple, v_norm_final, v_w_ff_gate, v_w_ff_up, v_w_ff_down, v_w_ple_gate, v_w_ple):
    wts = dict(w_in=w_in, w_br_attn=w_br_attn, w_br_sg=w_br_sg, w_out=w_out, w_ff_gate=w_ff_gate, w_ff_up=w_ff_up,
               w_ff_down=w_ff_down, w_ple_gate=w_ple_gate, w_ple=w_ple)
    mom_m = dict(w_in=m_w_in, w_br_attn=m_w_br_attn, w_br_sg=m_w_br_sg, w_out=m_w_out, w_ff_gate=m_w_ff_gate,
                 w_ff_up=m_w_ff_up, w_ff_down=m_w_ff_down, w_ple_gate=m_w_ple_gate, w_ple=m_w_ple)
    mom_v = dict(w_in=v_w_in, w_br_attn=v_w_br_attn, w_br_sg=v_w_br_sg, w_out=v_w_out, w_ff_gate=v_w_ff_gate,
                 w_ff_up=v_w_ff_up, w_ff_down=v_w_ff_down, w_ple_gate=v_w_ple_gate, w_ple=v_w_ple)
    small_w = dict(sg_w=sg_w, sg_b=sg_b, sg_ln_g=sg_ln_g, sg_ln_b=sg_ln_b, norm_mix=norm_mix, norm_ffn=norm_ffn,
                   norm_ple=norm_ple, norm_final=norm_final)
    small_m = dict(sg_w=m_sg_w, sg_b=m_sg_b, sg_ln_g=m_sg_ln_g, sg_ln_b=m_sg_ln_b, norm_mix=m_norm_mix, norm_ffn=m_norm_ffn,
                   norm_ple=m_norm_ple, norm_final=m_norm_final)
    small_v = dict(sg_w=v_sg_w, sg_b=v_sg_b, sg_ln_g=v_sg_ln_g, sg_ln_b=v_sg_ln_b, norm_mix=v_norm_mix, norm_ffn=v_norm_ffn,
                   norm_ple=v_norm_ple, norm_final=v_norm_final)
    depth = w_in.shape[0]
    S = x.shape[1]
    names = [n for n, _ in _BIG]
    c_idx = lax.axis_index("c").astype(jnp.int32).reshape(1)
    q_idx = (2 * lax.axis_index("x") + lax.axis_index("y")).astype(jnp.int32).reshape(1)
    tabs = _rope_tables(S)

    def shards(i, group):
        return [wts[n][i].astype(bf16) for n in group]

    full = [None] * depth
    full[0] = dict(zip(names, _run_exchange("ag_weights", _gather_exchange(shards(0, names)))))

    xs = x[0]
    saved = []
    for i in range(depth):
        sm = {n: small_w[n][i] for n in _SMALL if n != "norm_final"}
        comm = {h: _gather_exchange(shards(i + 1, grp)) for h, grp in zip(_FWD_CARRIERS, _THIRDS)} if i + 1 < depth else None
        xs, sv, carried = _layer_fwd(xs, p[i, 0], full[i], sm, tabs, comm)
        saved.append(sv)
        if comm:
            full[i + 1] = {n: g for h, grp in zip(_FWD_CARRIERS, _THIRDS) for n, g in zip(grp, carried[h])}
    dx, dg_final, loss_part = _loss_head(xs, norm_final, loss_target[0])

    reduced = [None] * depth
    small_parts = [None] * depth
    pending = None
    for i in reversed(range(depth)):
        sm = {n: small_w[n][i] for n in _SMALL if n != "norm_final"}
        comm = {h: _chips_exchange([pending[n] for n in grp]) for h, grp in zip(_BWD_CARRIERS, _THIRDS)} if pending else None
        dx, bigs, small_parts[i], carried = _layer_bwd(dx, p[i, 0], full[i], sm, tabs, saved[i], comm)
        if comm:
            reduced[i + 1] = (pending, {n: r for h, grp in zip(_BWD_CARRIERS, _THIRDS) for n, r in zip(grp, carried[h])})
        recv1 = _run_exchange("rs_sibling", _sibling_exchange(bigs))
        pending = {n: _rs_chip_sum("rs_chip_sum_" + n, g, r1, c_idx) for n, g, r1 in zip(names, bigs, recv1)}
    reduced[0] = (pending, dict(zip(names, _run_exchange("rs_chips", _chips_exchange([pending[n] for n in names])))))
    grad_x = dx[None]

    parts = {n: jnp.stack([small_parts[i][n] for i in range(depth)]) for n in _SMALL if n != "norm_final"}
    parts["norm_final"] = dg_final[0]
    gathered = _all_gather("ag_small", _pack_small(parts, loss_part), in_vmem=True)
    g_s, d_s, nm_s, nv_s = _small_sum_adamw(gathered, _pack_small(small_w, jnp.zeros((8, LANES), f32)),
                                            _pack_small(small_m, jnp.zeros((8, LANES), f32)),
                                            _pack_small(small_v, jnp.ones((8, LANES), f32)))
    loss = g_s[sum(small_w[n].size for n in _SMALL) // LANES, 0]

    def unpack_small(flat):
        out, off = {}, 0
        for n in _SMALL:
            k = small_w[n].size // LANES
            out[n] = flat[off:off + k].reshape(small_w[n].shape)
            off += k
        return out

    sm_g, sm_d, sm_nm, sm_nv = unpack_small(g_s), unpack_small(d_s), unpack_small(nm_s), unpack_small(nv_s)

    big_g, big_d, big_nm, big_nv = {}, {}, {}, {}
    for k, n in enumerate(names):
        outs = None
        for i in range(depth):
            p4, recv2 = reduced[i]
            outs = _adamw_layer(f"adamw_{n}_{i}", i, wts[n], mom_m[n], mom_v[n], p4[n], recv2[n], q_idx, outs)
        big_g[n], big_d[n], big_nm[n], big_nv[n] = outs

    order = ["w_in", "w_br_attn", "w_br_sg", "w_out", "sg_w", "sg_b", "sg_ln_g", "sg_ln_b", "norm_mix", "norm_ffn", "norm_ple",
             "norm_final", "w_ff_gate", "w_ff_up", "w_ff_down", "w_ple_gate", "w_ple"]

    def pick(big, small):
        return [big[n] if n in big else small[n] for n in order]

    return (loss, grad_x, *pick(big_g, sm_g), *pick(big_d, sm_d), *pick(big_nm, sm_nm), *pick(big_nv, sm_nv))
```

```python
import functools
import math

import jax
import jax.numpy as jnp
from jax import lax
from jax.experimental import pallas as pl
from jax.experimental.pallas import tpu as pltpu

f32 = jnp.float32
bf16 = jnp.bfloat16

HEAD_DIM = 128
N_GROUPS = 3
HEADS = 4
DILATIONS = (1, 4, 16)
RADIUS = 64
BLK = 128
QKV_W = 3 * N_GROUPS * HEADS * HEAD_DIM
ATTN_W = HEADS * HEAD_DIM
SG_CHUNK = 128
SG_GROUPS = 8
SG_W = SG_GROUPS * 128
ROPE_THETA = 10000.0
EPS = 1e-6
NEG = -1e30
N_DEV = 8
LANES = 128

ADAM_LR = 0.001
ADAM_B1 = 0.9
ADAM_B2 = 0.999
ADAM_EPS = 1e-08
ADAM_WD = 0.01
ADAM_STEP = 10

VMEM_LIMIT = 52 * 1024 * 1024
MESH = pl.DeviceIdType.MESH

NN = (((1,), (0,)), ((), ()))
NT = (((1,), (1,)), ((), ()))
TN = (((0,), (0,)), ((), ()))
_DN = {"nn": NN, "nt": NT, "tn": TN}


def _cp(*sem):
    return pltpu.CompilerParams(dimension_semantics=sem, vmem_limit_bytes=VMEM_LIMIT)


def _tile(n, pref, unit=128):
    if n <= pref:
        return n
    t = (pref // unit) * unit
    while t >= unit:
        if n % t == 0:
            return t
        t -= unit
    return n


_ANY = pl.BlockSpec(memory_space=pl.ANY)


def _call(body, *, name, grid, in_specs, out_specs, out_shape, operands, scratch=(), comm=None):
    in_specs, out_specs, out_shape, scratch = list(in_specs), list(out_specs), list(out_shape), list(scratch)
    if comm is None:
        res = pl.pallas_call(
            body, name=name, out_shape=out_shape, grid=grid, in_specs=in_specs, out_specs=out_specs, scratch_shapes=scratch,
            compiler_params=_cp(*(("arbitrary",) * len(grid))))(*operands)
        return res, []
    n_in, n_out, n_scr = len(in_specs), len(out_specs), len(scratch)
    c_in, c_out = len(comm.ins), len(comm.outs)

    def hosted(*refs):
        own_in, refs = refs[:n_in], refs[n_in:]
        ex_in, refs = refs[:c_in], refs[c_in:]
        own_out, refs = refs[:n_out], refs[n_out:]
        ex_out, refs = refs[:c_out], refs[c_out:]
        own_scr, sems = refs[:n_scr], refs[n_scr:]
        ids = [pl.program_id(a) for a in range(len(grid))]
        first = functools.reduce(jnp.logical_and, [i == 0 for i in ids])
        last = functools.reduce(jnp.logical_and, [i == g - 1 for i, g in zip(ids, grid)])

        @pl.when(first)
        def _():
            comm.start(ex_in, ex_out, sems)

        body(*own_in, *own_out, *own_scr)

        @pl.when(last)
        def _():
            comm.finish(ex_in, ex_out, sems)

    res = pl.pallas_call(
        hosted, name=name, out_shape=out_shape + list(comm.outs), grid=grid,
        in_specs=in_specs + [_ANY] * c_in, out_specs=out_specs + [_ANY] * c_out, scratch_shapes=scratch + list(comm.sems),
        compiler_params=pltpu.CompilerParams(dimension_semantics=("arbitrary",) * len(grid), vmem_limit_bytes=VMEM_LIMIT,
                                             has_side_effects=True),
    )(*operands, *comm.ins)
    return res[:n_out], res[n_out:]


def _sigmoid(x):
    return 1.0 / (1.0 + jnp.exp(-x))


_GC = math.sqrt(2.0 / math.pi)
_GA = 0.044715


def _gelu(x):
    return 0.5 * x * (1.0 + jnp.tanh(_GC * (x + _GA * x * x * x)))


def _gelu_grad(x):
    t = jnp.tanh(_GC * (x + _GA * x * x * x))
    return 0.5 * (1.0 + t) + 0.5 * x * (1.0 - t * t) * _GC * (1.0 + 3.0 * _GA * x * x)


def _matmul(name, prods, M, N, tm, tn, nk, outs, epilogue, extras=(), n_acc=1, comm=None):
    in_specs, operands, metas = [], [], []
    for a, b, mode, acc in prods:
        if mode == "tn":
            tk = a.shape[0] // nk
            in_specs += [pl.BlockSpec((tk, tm), lambda i, j, k: (k, i)), pl.BlockSpec((tk, tn), lambda i, j, k: (k, j))]
        elif mode == "nt":
            tk = a.shape[1] // nk
            in_specs += [pl.BlockSpec((tm, tk), lambda i, j, k: (i, k)), pl.BlockSpec((tn, tk), lambda i, j, k: (j, k))]
        else:
            tk = a.shape[1] // nk
            in_specs += [pl.BlockSpec((tm, tk), lambda i, j, k: (i, k)), pl.BlockSpec((tk, tn), lambda i, j, k: (k, j))]
        operands += [a, b]
        metas.append((mode, acc))
    for arr, bshape, imap in extras:
        in_specs.append(pl.BlockSpec(bshape, functools.partial(lambda i, j, k, f: f(i, j), f=imap)))
        operands.append(arr)
    out_specs = [pl.BlockSpec(bs, functools.partial(lambda i, j, k, f: f(i, j), f=imap)) for _, _, bs, imap in outs]
    out_shape = [jax.ShapeDtypeStruct(s, d) for s, d, _, _ in outs]
    n_prod, n_ext, n_out = len(prods), len(extras), len(outs)

    def body(*refs):
        in_refs = refs[: 2 * n_prod]
        ex_refs = refs[2 * n_prod : 2 * n_prod + n_ext]
        out_refs = refs[2 * n_prod + n_ext : 2 * n_prod + n_ext + n_out]
        acc_refs = refs[2 * n_prod + n_ext + n_out :]

        def partials():
            res = [None] * n_acc
            for idx, (mode, acc) in enumerate(metas):
                a = in_refs[2 * idx][...].astype(bf16)
                b = in_refs[2 * idx + 1][...].astype(bf16)
                d = lax.dot_general(a, b, _DN[mode], preferred_element_type=f32)
                res[acc] = d if res[acc] is None else res[acc] + d
            return res

        def finish(accs):
            vals = epilogue(accs, [r[...] for r in ex_refs])
            for r, v in zip(out_refs, vals):
                r[...] = v.astype(r.dtype)

        if nk == 1:
            finish(partials())
        else:
            k = pl.program_id(2)

            @pl.when(k == 0)
            def _():
                for r in acc_refs:
                    r[...] = jnp.zeros_like(r)

            for r, d in zip(acc_refs, partials()):
                r[...] += d

            @pl.when(k == nk - 1)
            def _():
                finish([r[...] for r in acc_refs])

    scratch = [pltpu.VMEM((tm, tn), f32) for _ in range(n_acc)] if nk > 1 else []
    res, ex = _call(body, name=name, grid=(M // tm, N // tn, nk), in_specs=in_specs, out_specs=out_specs,
                    out_shape=out_shape, operands=operands, scratch=scratch, comm=comm)
    return res if comm is None else (res, ex)


def _ident(accs, ex):
    return [accs[0]]


def _mm_simple(name, a, b, mode, out_dtype, tm_pref=1024, tn_pref=1024, tk_pref=1024, comm=None):
    if mode == "tn":
        K, M = a.shape
        N = b.shape[1]
    elif mode == "nt":
        M, K = a.shape
        N = b.shape[0]
    else:
        M, K = a.shape
        N = b.shape[1]
    tm, tn, tk = _tile(M, tm_pref), _tile(N, tn_pref), _tile(K, tk_pref)
    res = _matmul(name, [(a, b, mode, 0)], M, N, tm, tn, K // tk,
                  [((M, N), out_dtype, (tm, tn), lambda i, j: (i, j))], _ident, comm=comm)
    return res[0] if comm is None else (res[0][0], res[1])


def _group(c, width_pref=1024):
    g = LANES // math.gcd(c, LANES)
    while g < N_DEV and 2 * g * c <= width_pref:
        g *= 2
    return g


def _join(parts):
    return parts[0] if len(parts) == 1 else jnp.concatenate(parts, axis=1)


def _mm_nn_cols(name, a, gs_list, outs_dtypes, epilogue, extras=(), tm_pref=512, width_pref=1024, comm=None):
    M, K = a.shape
    c = gs_list[0].shape[2]
    g = _group(c, width_pref)
    W = g * c
    tm = _tile(M, tm_pref, 8)
    n_g, n_ex, n_out = len(gs_list), len(extras), len(outs_dtypes)

    def body(*refs):
        a_ref = refs[0]
        g_refs = refs[1:1 + n_g]
        ex_refs = refs[1 + n_g:1 + n_g + n_ex]
        out_refs = refs[1 + n_g + n_ex:]
        av = a_ref[...].astype(bf16)
        accs = [_join([lax.dot_general(av, gr[s], NN, preferred_element_type=f32) for s in range(g)]) for gr in g_refs]
        for r, v in zip(out_refs, epilogue(accs, [r[...] for r in ex_refs])):
            r[...] = v.astype(r.dtype)

    tile = pl.BlockSpec((tm, W), lambda j, i: (i, j))
    res, ex = _call(
        body, name=name, out_shape=[jax.ShapeDtypeStruct((M, N_DEV * c), d) for d in outs_dtypes],
        grid=(N_DEV // g, M // tm),
        in_specs=[pl.BlockSpec((tm, K), lambda j, i: (i, 0))]
        + [pl.BlockSpec((None, g, K, c), lambda j, i: (j, 0, 0, 0))] * n_g + [tile] * n_ex,
        out_specs=[tile] * n_out, operands=[a, *[gm.reshape(N_DEV // g, g, K, c) for gm in gs_list], *extras], comm=comm)
    return res if comm is None else (res, ex)


def _mm_nt_cols(name, pairs, out_dtype, tm_pref=1024, tn_pref=1024, width_pref=1024, comm=None):
    M = pairs[0][0].shape[0]
    Kw, c = pairs[0][1].shape[1:]
    g = _group(c, width_pref)
    W = g * c
    tm, tn = _tile(M, tm_pref, 8), _tile(Kw, tn_pref)
    nk = N_DEV // g
    n_p = len(pairs)

    def body(*refs):
        o_ref, acc = refs[2 * n_p], refs[2 * n_p + 1]
        k = pl.program_id(2)

        @pl.when(k == 0)
        def _():
            acc[...] = jnp.zeros_like(acc)

        tot = None
        for n in range(n_p):
            d_ref, g_ref = refs[2 * n], refs[2 * n + 1]
            for s in range(g):
                part = lax.dot_general(d_ref[:, s * c:(s + 1) * c], g_ref[s], NT, preferred_element_type=f32)
                tot = part if tot is None else tot + part
        acc[...] += tot

        @pl.when(k == nk - 1)
        def _():
            o_ref[...] = acc[...].astype(o_ref.dtype)

    in_specs, operands = [], []
    for d, gm in pairs:
        in_specs += [pl.BlockSpec((tm, W), lambda i, j, k: (i, k)), pl.BlockSpec((None, g, tn, c), lambda i, j, k: (k, 0, j, 0))]
        operands += [d, gm.reshape(nk, g, Kw, c)]
    res, ex = _call(
        body, name=name, out_shape=[jax.ShapeDtypeStruct((M, Kw), out_dtype)], grid=(M // tm, Kw // tn, nk),
        in_specs=in_specs, out_specs=[pl.BlockSpec((tm, tn), lambda i, j, k: (i, j))],
        scratch=[pltpu.VMEM((tm, tn), f32)], operands=operands, comm=comm)
    return res[0] if comm is None else (res[0], ex)


def _mm_tn_cols(name, x, ds, c, tm_pref=1024, tk_pref=1024, width_pref=1024, comm=None):
    S, Kw = x.shape
    g = _group(c, width_pref)
    W = g * c
    tm, tk = _tile(Kw, tm_pref), _tile(S, tk_pref, 16)
    nk = S // tk
    n_d = len(ds)

    def body(*refs):
        x_ref = refs[0]
        d_refs = refs[1:1 + n_d]
        o_refs = refs[1 + n_d:1 + 2 * n_d]
        accs = refs[1 + 2 * n_d:]
        k = pl.program_id(2)

        @pl.when(k == 0)
        def _():
            for acc in accs:
                acc[...] = jnp.zeros_like(acc)

        xv = x_ref[...].astype(bf16)
        for d_ref, acc in zip(d_refs, accs):
            for s in range(g):
                acc[s] += lax.dot_general(xv, d_ref[:, s * c:(s + 1) * c], TN, preferred_element_type=f32)

        @pl.when(k == nk - 1)
        def _():
            for o_ref, acc in zip(o_refs, accs):
                o_ref[...] = acc[...].astype(o_ref.dtype)

    outs, ex = _call(
        body, name=name, out_shape=[jax.ShapeDtypeStruct((N_DEV // g, g, Kw, c), bf16)] * n_d,
        grid=(Kw // tm, N_DEV // g, nk),
        in_specs=[pl.BlockSpec((tk, tm), lambda i, j, k: (k, i))] + [pl.BlockSpec((tk, W), lambda i, j, k: (k, j))] * n_d,
        out_specs=[pl.BlockSpec((None, g, tm, c), lambda i, j, k: (j, 0, i, 0))] * n_d,
        scratch=[pltpu.VMEM((g, tm, c), f32)] * n_d, operands=[x, *ds], comm=comm)
    outs = [o.reshape(N_DEV, Kw, c) for o in outs]
    return outs if comm is None else (outs, ex)


def _norm_fwd(name, x, g):
    S, D = x.shape
    tm = _tile(S, 512, 8)

    def body(x_ref, g_ref, h_ref):
        xv = x_ref[...]
        r = lax.rsqrt(jnp.mean(xv * xv, axis=-1, keepdims=True) + EPS)
        h_ref[...] = (xv * r * g_ref[...]).astype(bf16)

    return pl.pallas_call(
        body, name=name, out_shape=jax.ShapeDtypeStruct((S, D), bf16), grid=(S // tm,),
        in_specs=[pl.BlockSpec((tm, D), lambda i: (i, 0)), pl.BlockSpec((1, D), lambda i: (0, 0))],
        out_specs=pl.BlockSpec((tm, D), lambda i: (i, 0)), compiler_params=_cp("parallel"),
    )(x, g.reshape(1, D))


def _norm_bwd(name, dh, x, g, dx_in):
    S, D = x.shape
    tm = _tile(S, 256, 8)

    def body(dh_ref, x_ref, g_ref, dxi_ref, dx_ref, dg_ref):
        i = pl.program_id(0)
        xv = x_ref[...]
        r = lax.rsqrt(jnp.mean(xv * xv, axis=-1, keepdims=True) + EPS)
        xh = xv * r
        dhv = dh_ref[...].astype(f32)
        dxh = dhv * g_ref[...]
        dx_ref[...] = dxi_ref[...] + r * (dxh - xh * jnp.mean(dxh * xh, axis=-1, keepdims=True))

        @pl.when(i == 0)
        def _():
            dg_ref[...] = jnp.zeros_like(dg_ref)

        dg_ref[...] += jnp.sum(dhv * xh, axis=0, keepdims=True)

    return pl.pallas_call(
        body, name=name, out_shape=[jax.ShapeDtypeStruct((S, D), f32), jax.ShapeDtypeStruct((1, D), f32)], grid=(S // tm,),
        in_specs=[pl.BlockSpec((tm, D), lambda i: (i, 0)), pl.BlockSpec((tm, D), lambda i: (i, 0)),
                  pl.BlockSpec((1, D), lambda i: (0, 0)), pl.BlockSpec((tm, D), lambda i: (i, 0))],
        out_specs=[pl.BlockSpec((tm, D), lambda i: (i, 0)), pl.BlockSpec((1, D), lambda i: (0, 0))],
        compiler_params=_cp("arbitrary"),
    )(dh, x, g.reshape(1, D), dx_in)


def _loss_head(x, g, t):
    S, D = x.shape
    tm = _tile(S, 256, 8)

    def body(x_ref, g_ref, t_ref, dx_ref, dg_ref, loss_ref):
        i = pl.program_id(0)
        xv = x_ref[...]
        r = lax.rsqrt(jnp.mean(xv * xv, axis=-1, keepdims=True) + EPS)
        xh = xv * r
        gv = g_ref[...]
        err = xh * gv - t_ref[...]
        dy = err * (1.0 / D)
        dxh = dy * gv
        dx_ref[...] = r * (dxh - xh * jnp.mean(dxh * xh, axis=-1, keepdims=True))

        @pl.when(i == 0)
        def _():
            dg_ref[...] = jnp.zeros_like(dg_ref)
            loss_ref[...] = jnp.zeros_like(loss_ref)

        dg_ref[...] += jnp.sum(dy * xh, axis=0, keepdims=True)
        row = jnp.sum(err * err, axis=-1, keepdims=True) * (0.5 / D)
        loss_ref[...] += jnp.broadcast_to(jnp.sum(row, axis=0, keepdims=True), loss_ref.shape)

    return pl.pallas_call(
        body, name="loss_head",
        out_shape=[jax.ShapeDtypeStruct((S, D), f32), jax.ShapeDtypeStruct((1, D), f32), jax.ShapeDtypeStruct((8, LANES), f32)],
        grid=(S // tm,),
        in_specs=[pl.BlockSpec((tm, D), lambda i: (i, 0)), pl.BlockSpec((1, D), lambda i: (0, 0)), pl.BlockSpec((tm, D), lambda i: (i, 0))],
        out_specs=[pl.BlockSpec((tm, D), lambda i: (i, 0)), pl.BlockSpec((1, D), lambda i: (0, 0)), pl.BlockSpec((8, LANES), lambda i: (0, 0))],
        compiler_params=_cp("arbitrary"),
    )(x, g.reshape(1, D), t)


def _perm(t, d):
    if d == 1:
        return t
    S, C = t.shape
    return t.reshape(S // d, d, C).transpose(1, 0, 2).reshape(S, C)


def _unperm(t, d):
    if d == 1:
        return t
    S, C = t.shape
    return t.reshape(d, S // d, C).transpose(1, 0, 2).reshape(S, C)


def _rope_tables(S):
    half = HEAD_DIM // 2
    pos = jnp.arange(S, dtype=f32)
    inv_freq = ROPE_THETA ** (-jnp.arange(0, HEAD_DIM, 2, dtype=f32) / HEAD_DIM)
    ang = pos[:, None] * inv_freq[None, :]
    c, s = jnp.cos(ang), jnp.sin(ang)
    cos2 = jnp.concatenate([c, c], axis=-1)
    sin2 = jnp.concatenate([-s, s], axis=-1)
    assert cos2.shape == (S, 2 * half)
    return (jnp.stack([_perm(cos2, d) for d in DILATIONS]), jnp.stack([_perm(sin2, d) for d in DILATIONS]))


def _rope(t, c, s):
    return t * c + pltpu.roll(t, HEAD_DIM // 2, 1) * s


def _rope_bwd(dt, c, s):
    return dt * c - pltpu.roll(dt, HEAD_DIM // 2, 1) * s


def _band_bounds(i, nblk):
    g = pl.program_id(0)
    lb = jnp.right_shift(jnp.int32(nblk), 2 * g)
    pos = lax.rem(i, lb)
    lo = jnp.where(pos == 0, BLK, 0)
    hi = jnp.where(pos == lb - 1, 2 * BLK, 3 * BLK)
    return lo, hi


def _band_specs(width, nblk):
    prev = pl.BlockSpec((None, BLK, width), lambda g, i: (g, jnp.maximum(i - 1, 0), 0))
    cur = pl.BlockSpec((None, BLK, width), lambda g, i: (g, i, 0))
    nxt = pl.BlockSpec((None, BLK, width), lambda g, i: (g, jnp.minimum(i + 1, nblk - 1), 0))
    return [prev, cur, nxt]


_SCALE = HEAD_DIM ** -0.5


def _attn_fwd(q, k, v, cos, sin, comm=None):
    _, S, W = q.shape
    nblk = S // BLK

    def body(q_ref, kp, kc, kn, vp, vc, vn, cq, sq, ckp, ckc, ckn, skp, skc, skn, o_ref, lse_ref):
        i = pl.program_id(1)
        lo, hi = _band_bounds(i, nblk)
        a = lax.broadcasted_iota(jnp.int32, (BLK, 3 * BLK), 0)
        b = lax.broadcasted_iota(jnp.int32, (BLK, 3 * BLK), 1)
        mask = (jnp.abs(b - BLK - a) <= RADIUS) & (b >= lo) & (b < hi)
        ck = jnp.concatenate([ckp[...], ckc[...], ckn[...]], axis=0)
        sk = jnp.concatenate([skp[...], skc[...], skn[...]], axis=0)
        for hh in range(HEADS):
            sl = slice(hh * HEAD_DIM, (hh + 1) * HEAD_DIM)
            qh = _rope(q_ref[:, sl].astype(f32), cq[...], sq[...]).astype(bf16)
            kh = jnp.concatenate([kp[:, sl], kc[:, sl], kn[:, sl]], axis=0).astype(f32)
            kh = _rope(kh, ck, sk).astype(bf16)
            vh = jnp.concatenate([vp[:, sl], vc[:, sl], vn[:, sl]], axis=0)
            s = lax.dot_general(qh, kh, NT, preferred_element_type=f32) * _SCALE
            s = jnp.where(mask, s, NEG)
            m = jnp.max(s, axis=-1, keepdims=True)
            e = jnp.exp(s - m)
            den = jnp.sum(e, axis=-1, keepdims=True)
            o = lax.dot_general(e.astype(bf16), vh, NN, preferred_element_type=f32) * (1.0 / den)
            o_ref[:, sl] = o.astype(bf16)
            lse_ref[:, sl] = jnp.broadcast_to(m + jnp.log(den), (BLK, HEAD_DIM))

    blk = pl.BlockSpec((None, BLK, W), lambda g, i: (g, i, 0))
    tab = pl.BlockSpec((None, BLK, HEAD_DIM), lambda g, i: (g, i, 0))
    res, ex = _call(
        body, name="attn_fwd",
        out_shape=[jax.ShapeDtypeStruct((N_GROUPS, S, W), bf16), jax.ShapeDtypeStruct((N_GROUPS, S, W), f32)],
        grid=(N_GROUPS, nblk),
        in_specs=[blk] + _band_specs(W, nblk) * 2 + [tab, tab] + _band_specs(HEAD_DIM, nblk) * 2,
        out_specs=[blk, blk], operands=[q, k, k, k, v, v, v, cos, sin, cos, cos, cos, sin, sin, sin], comm=comm)
    return res if comm is None else (res, ex)


def _attn_combine(o3, lse3):
    _, S, W = o3.shape
    tm = _tile(S, 512, 8)

    def body(o_ref, l_ref, attn_ref, lse_ref):
        l0, l1, l2 = l_ref[0], l_ref[1], l_ref[2]
        m = jnp.maximum(jnp.maximum(l0, l1), l2)
        w0, w1, w2 = jnp.exp(l0 - m), jnp.exp(l1 - m), jnp.exp(l2 - m)
        den = w0 + w1 + w2
        acc = w0 * o_ref[0].astype(f32) + w1 * o_ref[1].astype(f32) + w2 * o_ref[2].astype(f32)
        attn_ref[...] = (acc * (1.0 / den)).astype(bf16)
        lse_ref[...] = m + jnp.log(den)

    blk3 = pl.BlockSpec((N_GROUPS, tm, W), lambda i: (0, i, 0))
    blk = pl.BlockSpec((tm, W), lambda i: (i, 0))
    return pl.pallas_call(
        body, name="attn_combine", out_shape=[jax.ShapeDtypeStruct((S, W), bf16), jax.ShapeDtypeStruct((S, W), f32)],
        grid=(S // tm,), in_specs=[blk3, blk3], out_specs=[blk, blk], compiler_params=_cp("parallel"),
    )(o3, lse3)


def _attn_delta(attn, dattn):
    S, W = attn.shape
    tm = _tile(S, 512, 8)

    def body(a_ref, d_ref, o_ref):
        prod = a_ref[...].astype(f32) * d_ref[...].astype(f32)
        for hh in range(HEADS):
            sl = slice(hh * HEAD_DIM, (hh + 1) * HEAD_DIM)
            o_ref[:, sl] = jnp.broadcast_to(jnp.sum(prod[:, sl], axis=-1, keepdims=True), (tm, HEAD_DIM))

    blk = pl.BlockSpec((tm, W), lambda i: (i, 0))
    return pl.pallas_call(
        body, name="attn_delta", out_shape=jax.ShapeDtypeStruct((S, W), f32), grid=(S // tm,),
        in_specs=[blk, blk], out_specs=blk, compiler_params=_cp("parallel"),
    )(attn, dattn)


def _attn_bwd_dq(q, k, v, cos, sin, da, lse, dl):
    _, S, W = q.shape
    nblk = S // BLK

    def body(q_ref, kp, kc, kn, vp, vc, vn, cq, sq, ckp, ckc, ckn, skp, skc, skn, da_ref, l_ref, dl_ref, dq_ref):
        i = pl.program_id(1)
        lo, hi = _band_bounds(i, nblk)
        a = lax.broadcasted_iota(jnp.int32, (BLK, 3 * BLK), 0)
        b = lax.broadcasted_iota(jnp.int32, (BLK, 3 * BLK), 1)
        mask = (jnp.abs(b - BLK - a) <= RADIUS) & (b >= lo) & (b < hi)
        ck = jnp.concatenate([ckp[...], ckc[...], ckn[...]], axis=0)
        sk = jnp.concatenate([skp[...], skc[...], skn[...]], axis=0)
        for hh in range(HEADS):
            sl = slice(hh * HEAD_DIM, (hh + 1) * HEAD_DIM)
            qh = _rope(q_ref[:, sl].astype(f32), cq[...], sq[...]).astype(bf16)
            kh = jnp.concatenate([kp[:, sl], kc[:, sl], kn[:, sl]], axis=0).astype(f32)
            kh = _rope(kh, ck, sk).astype(bf16)
            vh = jnp.concatenate([vp[:, sl], vc[:, sl], vn[:, sl]], axis=0)
            s = lax.dot_general(qh, kh, NT, preferred_element_type=f32) * _SCALE
            lh = l_ref[:, sl]
            l3 = jnp.concatenate([lh, lh, lh], axis=1)
            p = jnp.exp(jnp.where(mask, s - l3, NEG))
            dp = lax.dot_general(da_ref[:, sl], vh, NT, preferred_element_type=f32)
            dh = dl_ref[:, sl]
            ds = p * (dp - jnp.concatenate([dh, dh, dh], axis=1))
            dqh = lax.dot_general(ds.astype(bf16), kh, NN, preferred_element_type=f32) * _SCALE
            dq_ref[:, sl] = _rope_bwd(dqh, cq[...], sq[...]).astype(bf16)

    blk = pl.BlockSpec((None, BLK, W), lambda g, i: (g, i, 0))
    tab = pl.BlockSpec((None, BLK, HEAD_DIM), lambda g, i: (g, i, 0))
    return pl.pallas_call(
        body, name="attn_bwd_dq", out_shape=jax.ShapeDtypeStruct((N_GROUPS, S, W), bf16), grid=(N_GROUPS, nblk),
        in_specs=[blk] + _band_specs(W, nblk) * 2 + [tab, tab] + _band_specs(HEAD_DIM, nblk) * 2 + [blk, blk, blk],
        out_specs=blk, compiler_params=_cp("parallel", "parallel"),
    )(q, k, k, k, v, v, v, cos, sin, cos, cos, cos, sin, sin, sin, da, lse, dl)


def _attn_bwd_dkv(q, k, v, cos, sin, da, lse, dl, comm=None):
    _, S, W = q.shape
    nblk = S // BLK

    def body(k_ref, v_ref, ck, sk, qp, qc, qn, cqp, cqc, cqn, sqp, sqc, sqn, dap, dac, dan, lp, lc, ln, dlp, dlc, dln,
             dk_ref, dv_ref):
        j = pl.program_id(1)
        lo, hi = _band_bounds(j, nblk)
        a = lax.broadcasted_iota(jnp.int32, (3 * BLK, BLK), 0)
        b = lax.broadcasted_iota(jnp.int32, (3 * BLK, BLK), 1)
        mask = (jnp.abs(b - (a - BLK)) <= RADIUS) & (a >= lo) & (a < hi)
        cq = jnp.concatenate([cqp[...], cqc[...], cqn[...]], axis=0)
        sq = jnp.concatenate([sqp[...], sqc[...], sqn[...]], axis=0)
        for hh in range(HEADS):
            sl = slice(hh * HEAD_DIM, (hh + 1) * HEAD_DIM)
            kh = _rope(k_ref[:, sl].astype(f32), ck[...], sk[...]).astype(bf16)
            vh = v_ref[:, sl]
            qh = jnp.concatenate([qp[:, sl], qc[:, sl], qn[:, sl]], axis=0).astype(f32)
            qh = _rope(qh, cq, sq).astype(bf16)
            dah = jnp.concatenate([dap[:, sl], dac[:, sl], dan[:, sl]], axis=0)
            lh = jnp.concatenate([lp[:, sl], lc[:, sl], ln[:, sl]], axis=0)
            dlh = jnp.concatenate([dlp[:, sl], dlc[:, sl], dln[:, sl]], axis=0)
            s = lax.dot_general(qh, kh, NT, preferred_element_type=f32) * _SCALE
            p = jnp.exp(jnp.where(mask, s - lh, NEG))
            dv_ref[:, sl] = lax.dot_general(p.astype(bf16), dah, TN, preferred_element_type=f32).astype(bf16)
            dp = lax.dot_general(dah, vh, NT, preferred_element_type=f32)
            ds = p * (dp - dlh)
            dkh = lax.dot_general(ds.astype(bf16), qh, TN, preferred_element_type=f32) * _SCALE
            dk_ref[:, sl] = _rope_bwd(dkh, ck[...], sk[...]).astype(bf16)

    blk = pl.BlockSpec((None, BLK, W), lambda g, i: (g, i, 0))
    tab = pl.BlockSpec((None, BLK, HEAD_DIM), lambda g, i: (g, i, 0))
    bw, bt = _band_specs(W, nblk), _band_specs(HEAD_DIM, nblk)
    res, ex = _call(
        body, name="attn_bwd_dkv",
        out_shape=[jax.ShapeDtypeStruct((N_GROUPS, S, W), bf16), jax.ShapeDtypeStruct((N_GROUPS, S, W), bf16)],
        grid=(N_GROUPS, nblk),
        in_specs=[blk, blk, tab, tab] + bw + bt + bt + bw + bw + bw, out_specs=[blk, blk],
        operands=[k, v, cos, sin, q, q, q, cos, cos, cos, sin, sin, sin, da, da, da, lse, lse, lse, dl, dl, dl], comm=comm)
    return res if comm is None else (res, ex)


_SG_ROWS = 512


def _sg_z_specs(tm, half):
    o = QKV_W // half
    return [pl.BlockSpec((tm, half), functools.partial(lambda i, c: (i, c), c=o + n)) for n in range(4)]


def _sg_norm(v, lg, lb):
    gv = _gelu(v)
    mu = jnp.mean(gv, axis=-1, keepdims=True)
    xc = gv - mu
    rstd = lax.rsqrt(jnp.mean(xc * xc, axis=-1, keepdims=True) + EPS)
    xh = xc * rstd
    return xh, rstd, xh * lg + lb


def _sg_fwd(z, w, bb, lg, lb):
    S = z.shape[0]
    tm = _tile(S, _SG_ROWS, SG_CHUNK)
    half = SG_W // 2

    def body(u0, u1, v0, v1, w_ref, bb_ref, lg_ref, lb_ref, o_ref):
        u = jnp.concatenate([u0[...], u1[...]], axis=1).astype(f32)
        v = jnp.concatenate([v0[...], v1[...]], axis=1).astype(f32)
        gu = _gelu(u)
        _, _, vn = _sg_norm(v, lg_ref[...], lb_ref[...])
        vnb = vn.astype(bf16)
        for c in range(tm // SG_CHUNK):
            rs = slice(c * SG_CHUNK, (c + 1) * SG_CHUNK)
            for g in range(SG_GROUPS):
                cs = slice(g * 128, (g + 1) * 128)
                mixed = lax.dot_general(w_ref[g], vnb[rs, cs], NN, preferred_element_type=f32) + bb_ref[g]
                o_ref[rs, cs] = (gu[rs, cs] * mixed).astype(bf16)

    full3 = pl.BlockSpec((SG_GROUPS, 128, 128), lambda i: (0, 0, 0))
    vec = pl.BlockSpec((1, SG_W), lambda i: (0, 0))
    return pl.pallas_call(
        body, name="sg_fwd", out_shape=jax.ShapeDtypeStruct((S, SG_W), bf16), grid=(S // tm,),
        in_specs=_sg_z_specs(tm, half) + [full3, full3, vec, vec],
        out_specs=pl.BlockSpec((tm, SG_W), lambda i: (i, 0)), compiler_params=_cp("parallel"),
    )(z, z, z, z, w, bb, lg.reshape(1, SG_W), lb.reshape(1, SG_W))


def _sg_bwd(dsg, z, w, wt, bb, lg, lb):
    S = z.shape[0]
    tm = _tile(S, _SG_ROWS, SG_CHUNK)
    half = SG_W // 2

    def body(d_ref, u0, u1, v0, v1, w_ref, wt_ref, bb_ref, lg_ref, lb_ref, du_ref, dv_ref, dw_ref, db_ref, dlg_ref, dlb_ref, dvn_scr):
        i = pl.program_id(0)

        @pl.when(i == 0)
        def _():
            dw_ref[...] = jnp.zeros_like(dw_ref)
            db_ref[...] = jnp.zeros_like(db_ref)
            dlg_ref[...] = jnp.zeros_like(dlg_ref)
            dlb_ref[...] = jnp.zeros_like(dlb_ref)

        u = jnp.concatenate([u0[...], u1[...]], axis=1).astype(f32)
        v = jnp.concatenate([v0[...], v1[...]], axis=1).astype(f32)
        gu = _gelu(u)
        dgu = _gelu_grad(u)
        xh, rstd, vn = _sg_norm(v, lg_ref[...], lb_ref[...])
        vnb = vn.astype(bf16)
        dsg_v = d_ref[...].astype(f32)
        for g in range(SG_GROUPS):
            cs = slice(g * 128, (g + 1) * 128)
            dw_g = jnp.zeros((128, 128), f32)
            db_g = jnp.zeros((128, 1), f32)
            for c in range(tm // SG_CHUNK):
                rs = slice(c * SG_CHUNK, (c + 1) * SG_CHUNK)
                ds = dsg_v[rs, cs]
                mixed = lax.dot_general(w_ref[g], vnb[rs, cs], NN, preferred_element_type=f32) + bb_ref[g]
                du_ref[rs, cs] = (ds * mixed * dgu[rs, cs]).astype(bf16)
                dmix = ds * gu[rs, cs]
                dmb = dmix.astype(bf16)
                dw_g = dw_g + lax.dot_general(dmb, vnb[rs, cs], NT, preferred_element_type=f32)
                db_g = db_g + jnp.sum(dmix, axis=-1, keepdims=True)
                dvn_scr[rs, cs] = lax.dot_general(wt_ref[g], dmb, NN, preferred_element_type=f32)
            dw_ref[g] += dw_g
            db_ref[g] += jnp.broadcast_to(db_g, (128, 128))
        dvn = dvn_scr[...]
        dlg_ref[...] += jnp.sum(dvn * xh, axis=0, keepdims=True)
        dlb_ref[...] += jnp.sum(dvn, axis=0, keepdims=True)
        dxh = dvn * lg_ref[...]
        dgv = rstd * (dxh - jnp.mean(dxh, axis=-1, keepdims=True) - xh * jnp.mean(dxh * xh, axis=-1, keepdims=True))
        dv_ref[...] = (dgv * _gelu_grad(v)).astype(bf16)

    full3 = pl.BlockSpec((SG_GROUPS, 128, 128), lambda i: (0, 0, 0))
    vec = pl.BlockSpec((1, SG_W), lambda i: (0, 0))
    row = pl.BlockSpec((tm, SG_W), lambda i: (i, 0))
    return pl.pallas_call(
        body, name="sg_bwd",
        out_shape=[jax.ShapeDtypeStruct((S, SG_W), bf16), jax.ShapeDtypeStruct((S, SG_W), bf16),
                   jax.ShapeDtypeStruct((SG_GROUPS, 128, 128), f32), jax.ShapeDtypeStruct((SG_GROUPS, 128, 128), f32),
                   jax.ShapeDtypeStruct((1, SG_W), f32), jax.ShapeDtypeStruct((1, SG_W), f32)],
        grid=(S // tm,),
        in_specs=[row] + _sg_z_specs(tm, half) + [full3, full3, full3, vec, vec],
        out_specs=[row, row, full3, full3, vec, vec],
        scratch_shapes=[pltpu.VMEM((tm, SG_W), f32)], compiler_params=_cp("arbitrary"),
    )(dsg, z, z, z, z, w, wt, bb, lg.reshape(1, SG_W), lb.reshape(1, SG_W))


def _merge_fwd(z, ya, yb, D):
    S = z.shape[0]
    tm, tc = _tile(S, 512, 8), _tile(D, 512)
    o_a, o_b = (QKV_W + 2 * SG_W) // tc, (QKV_W + 2 * SG_W + D) // tc

    def body(ga_ref, gb_ref, ya_ref, yb_ref, o_ref):
        ga = _sigmoid(ga_ref[...].astype(f32))
        gb = _sigmoid(gb_ref[...].astype(f32))
        o_ref[...] = (ga * ya_ref[...].astype(f32) + gb * yb_ref[...].astype(f32)).astype(bf16)

    blk = pl.BlockSpec((tm, tc), lambda i, j: (i, j))
    return pl.pallas_call(
        body, name="merge_fwd", out_shape=jax.ShapeDtypeStruct((S, D), bf16), grid=(S // tm, D // tc),
        in_specs=[pl.BlockSpec((tm, tc), lambda i, j: (i, o_a + j)), pl.BlockSpec((tm, tc), lambda i, j: (i, o_b + j)), blk, blk],
        out_specs=blk, compiler_params=_cp("parallel", "parallel"),
    )(z, z, ya, yb)


def _ple_bwd_ew(dx, gp, e):
    S, D = dx.shape
    tm, tc = _tile(S, 512, 8), _tile(D, 1024)

    def body(dx_ref, gp_ref, e_ref, dgp_ref, de_ref):
        dxv = dx_ref[...]
        sg = _sigmoid(gp_ref[...].astype(f32))
        dgp_ref[...] = (dxv * e_ref[...].astype(f32) * sg * (1.0 - sg)).astype(bf16)
        de_ref[...] = (dxv * sg).astype(bf16)

    blk = pl.BlockSpec((tm, tc), lambda i, j: (i, j))
    return pl.pallas_call(
        body, name="ple_bwd_ew", out_shape=[jax.ShapeDtypeStruct((S, D), bf16)] * 2, grid=(S // tm, D // tc),
        in_specs=[blk, blk, blk], out_specs=[blk, blk], compiler_params=_cp("parallel", "parallel"),
    )(dx, gp, e)


def _adam_math(w, g, m, v):
    m = ADAM_B1 * m + (1.0 - ADAM_B1) * g
    v = ADAM_B2 * v + (1.0 - ADAM_B2) * (g * g)
    m_hat = m / (1.0 - ADAM_B1 ** ADAM_STEP)
    v_hat = v / (1.0 - ADAM_B2 ** ADAM_STEP)
    delta = -ADAM_LR * (m_hat / (jnp.sqrt(v_hat) + ADAM_EPS) + ADAM_WD * w)
    return delta, m, v


def _small_sum_adamw(gathered, w, m, v):
    _, R, _ = gathered.shape
    tr = _tile(R, 1024, SMALL_ROWS)

    def body(p_ref, w_ref, m_ref, v_ref, g_ref, d_ref, nm_ref, nv_ref):
        g = p_ref[0]
        for n in range(1, N_DEV):
            g = g + p_ref[n]
        d, nm, nv = _adam_math(w_ref[...], g, m_ref[...], v_ref[...])
        g_ref[...] = g
        d_ref[...] = d
        nm_ref[...] = nm
        nv_ref[...] = nv

    blk = pl.BlockSpec((tr, LANES), lambda i: (i, 0))
    return pl.pallas_call(
        body, name="small_sum_adamw", out_shape=[jax.ShapeDtypeStruct((R, LANES), f32)] * 4, grid=(R // tr,),
        in_specs=[pl.BlockSpec((N_DEV, tr, LANES), lambda i: (0, i, 0)), blk, blk, blk], out_specs=[blk] * 4,
        compiler_params=_cp("parallel"),
    )(gathered, w, m, v)


def _all_gather(name, shard, in_vmem=False):
    R, C = shard.shape

    def body(x_ref, out_ref, send_sems, recv_sems, local_sem):
        x, y, c = lax.axis_index("x"), lax.axis_index("y"), lax.axis_index("c")
        me, sibling = (x, y, c), (x, y, 1 - c)
        chips = [(1 - x, y), (x, 1 - y), (1 - x, 1 - y)]

        def rows(px, py, pc):
            return out_ref.at[4 * px + 2 * py + pc]

        def copy(k, block, to, src=None):
            return pltpu.make_async_remote_copy(
                src_ref=rows(*block) if src is None else src, dst_ref=rows(*block),
                send_sem=send_sems.at[k], recv_sem=recv_sems.at[k], device_id=to, device_id_type=MESH)

        mine = pltpu.make_async_copy(x_ref, rows(*me), local_sem)
        mine.start()
        first = [copy(0, me, sibling, src=x_ref)]
        first += [copy(1 + j, me, (*chip, c), src=x_ref) for j, chip in enumerate(chips)]
        for cp in first:
            cp.start()
        passed = [copy(4 + j, (*chip, c), sibling) for j, chip in enumerate(chips)]
        for j, chip in enumerate(chips):
            copy(1 + j, (*chip, c), me).wait_recv()
            passed[j].start()
        copy(0, sibling, me).wait_recv()
        for j, chip in enumerate(chips):
            copy(4 + j, (*chip, 1 - c), me).wait_recv()
        for cp in first + passed:
            cp.wait_send()
        mine.wait()

    space = pl.BlockSpec(memory_space=pltpu.VMEM) if in_vmem else _ANY
    return pl.pallas_call(
        body, name=name, out_shape=jax.ShapeDtypeStruct((N_DEV, R, C), shard.dtype),
        in_specs=[space], out_specs=space,
        scratch_shapes=[pltpu.SemaphoreType.DMA((7,)), pltpu.SemaphoreType.DMA((7,)), pltpu.SemaphoreType.DMA],
        compiler_params=pltpu.CompilerParams(has_side_effects=True, vmem_limit_bytes=VMEM_LIMIT),
    )(shard)


class _Exchange:
    def __init__(self, ins, outs, sems, start, finish):
        self.ins, self.outs, self.sems, self.start, self.finish = list(ins), list(outs), list(sems), start, finish


def _run_exchange(name, ex):
    c_in, c_out = len(ex.ins), len(ex.outs)

    def body(*refs):
        ins, outs, sems = refs[:c_in], refs[c_in:c_in + c_out], refs[c_in + c_out:]
        ex.start(ins, outs, sems)
        ex.finish(ins, outs, sems)

    return pl.pallas_call(
        body, name=name, out_shape=ex.outs, in_specs=[_ANY] * c_in, out_specs=[_ANY] * c_out, scratch_shapes=ex.sems,
        compiler_params=pltpu.CompilerParams(has_side_effects=True, vmem_limit_bytes=VMEM_LIMIT),
    )(*ex.ins)


def _gather_exchange(shards):
    n = len(shards)

    def plan(ins, outs, sems):
        send_sems, recv_sems, local_sems = sems
        x, y, c = lax.axis_index("x"), lax.axis_index("y"), lax.axis_index("c")
        me, sibling = (x, y, c), (x, y, 1 - c)
        chips = [(1 - x, y), (x, 1 - y), (1 - x, 1 - y)]

        def rows(w, px, py, pc):
            return outs[w].at[4 * px + 2 * py + pc]

        def copy(w, k, block, to, src=None):
            return pltpu.make_async_remote_copy(
                src_ref=rows(w, *block) if src is None else src, dst_ref=rows(w, *block),
                send_sem=send_sems.at[w, k], recv_sem=recv_sems.at[w, k], device_id=to, device_id_type=MESH)

        mine = [pltpu.make_async_copy(ins[w], rows(w, *me), local_sems.at[w]) for w in range(n)]
        first = []
        for w in range(n):
            first.append(copy(w, 0, me, sibling, src=ins[w]))
            first += [copy(w, 1 + j, me, (*chip, c), src=ins[w]) for j, chip in enumerate(chips)]
        return c, me, sibling, chips, copy, mine, first

    def start(ins, outs, sems):
        _, _, _, _, _, mine, first = plan(ins, outs, sems)
        for cp in mine + first:
            cp.start()

    def finish(ins, outs, sems):
        c, me, sibling, chips, copy, mine, first = plan(ins, outs, sems)
        passed = []
        for w in range(n):
            for j, chip in enumerate(chips):
                copy(w, 1 + j, (*chip, c), me).wait_recv()
                passed.append(copy(w, 4 + j, (*chip, c), sibling))
                passed[-1].start()
        for w in range(n):
            copy(w, 0, sibling, me).wait_recv()
            for j, chip in enumerate(chips):
                copy(w, 4 + j, (*chip, 1 - c), me).wait_recv()
        for cp in first + passed:
            cp.wait_send()
        for cp in mine:
            cp.wait()

    return _Exchange(
        shards, [jax.ShapeDtypeStruct((N_DEV,) + s.shape, s.dtype) for s in shards],
        [pltpu.SemaphoreType.DMA((n, 7)), pltpu.SemaphoreType.DMA((n, 7)), pltpu.SemaphoreType.DMA((n,))], start, finish)


def _sibling_exchange(gs):
    n = len(gs)

    def copies(ins, outs, sems):
        send_sems, recv_sems = sems
        x, y, c = lax.axis_index("x"), lax.axis_index("y"), lax.axis_index("c")
        return [pltpu.make_async_remote_copy(
            src_ref=ins[w].at[2 * q + (1 - c)], dst_ref=outs[w].at[q], send_sem=send_sems.at[w, q],
            recv_sem=recv_sems.at[w, q], device_id=(x, y, 1 - c), device_id_type=MESH) for w in range(n) for q in range(4)]

    def start(ins, outs, sems):
        for cp in copies(ins, outs, sems):
            cp.start()

    def finish(ins, outs, sems):
        cps = copies(ins, outs, sems)
        for cp in cps:
            cp.wait_recv()
        for cp in cps:
            cp.wait_send()

    return _Exchange(gs, [jax.ShapeDtypeStruct((4,) + g.shape[1:], g.dtype) for g in gs],
                     [pltpu.SemaphoreType.DMA((n, 4)), pltpu.SemaphoreType.DMA((n, 4))], start, finish)


def _rs_chip_sum(name, g8, recv, c_idx):
    _, R, C = g8.shape
    tr = _tile(R, 512, 16)
    g42 = g8.reshape(4, 2, R, C)

    def body(c_ref, a_ref, b_ref, o_ref):
        o_ref[...] = (a_ref[...].astype(f32) + b_ref[...].astype(f32)).astype(o_ref.dtype)

    return pl.pallas_call(
        body, name=name, out_shape=jax.ShapeDtypeStruct((4, R, C), g8.dtype),
        grid_spec=pltpu.PrefetchScalarGridSpec(
            num_scalar_prefetch=1, grid=(4, R // tr),
            in_specs=[pl.BlockSpec((None, None, tr, C), lambda q, r, c_ref: (q, c_ref[0], r, 0)),
                      pl.BlockSpec((None, tr, C), lambda q, r, c_ref: (q, r, 0))],
            out_specs=pl.BlockSpec((None, tr, C), lambda q, r, c_ref: (q, r, 0))),
        compiler_params=_cp("parallel", "parallel"),
    )(c_idx, g42, recv)


def _chips_exchange(p4s):
    n = len(p4s)

    def copies(ins, outs, sems):
        send_sems, recv_sems = sems
        x, y, c = lax.axis_index("x"), lax.axis_index("y"), lax.axis_index("c")
        chips = [(1 - x, y), (x, 1 - y), (1 - x, 1 - y)]
        return [pltpu.make_async_remote_copy(
            src_ref=ins[w].at[2 * cx + cy], dst_ref=outs[w].at[k], send_sem=send_sems.at[w, k],
            recv_sem=recv_sems.at[w, k], device_id=(cx, cy, c), device_id_type=MESH)
            for w in range(n) for k, (cx, cy) in enumerate(chips)]

    def start(ins, outs, sems):
        for cp in copies(ins, outs, sems):
            cp.start()

    def finish(ins, outs, sems):
        cps = copies(ins, outs, sems)
        for cp in cps:
            cp.wait_recv()
        for cp in cps:
            cp.wait_send()

    return _Exchange(p4s, [jax.ShapeDtypeStruct((3,) + p.shape[1:], p.dtype) for p in p4s],
                     [pltpu.SemaphoreType.DMA((n, 3)), pltpu.SemaphoreType.DMA((n, 3))], start, finish)


def _adamw_layer(name, layer, w, m, v, p4, recv, q_idx, prev):
    depth, R, C = w.shape
    tr = _tile(R, 256, 8)

    def body(q_ref, w_ref, m_ref, v_ref, a_ref, b_ref, *rest):
        g_ref, d_ref, nm_ref, nv_ref = rest[-4:]
        g = ((a_ref[...].astype(f32) + b_ref[0].astype(f32)) + b_ref[1].astype(f32)) + b_ref[2].astype(f32)
        d, nm, nv = _adam_math(w_ref[...], g, m_ref[...], v_ref[...])
        g_ref[...] = g
        d_ref[...] = d
        nm_ref[...] = nm
        nv_ref[...] = nv

    lay = pl.BlockSpec((None, tr, C), lambda i, q_ref: (layer, i, 0))
    n_prev = 0 if prev is None else 4
    return pl.pallas_call(
        body, name=name, out_shape=[jax.ShapeDtypeStruct((depth, R, C), f32)] * 4,
        grid_spec=pltpu.PrefetchScalarGridSpec(
            num_scalar_prefetch=1, grid=(R // tr,),
            in_specs=[lay, lay, lay, pl.BlockSpec((None, tr, C), lambda i, q_ref: (q_ref[0], i, 0)),
                      pl.BlockSpec((3, tr, C), lambda i, q_ref: (0, i, 0))] + [_ANY] * n_prev,
            out_specs=[lay] * 4),
        input_output_aliases={6 + n: n for n in range(n_prev)},
        compiler_params=_cp("parallel"),
    )(q_idx, w, m, v, p4, recv, *(prev or ()))


_BIG = (("w_in", 1), ("w_br_attn", 1), ("w_br_sg", 1), ("w_out", 0), ("w_ff_gate", 1), ("w_ff_up", 1),
        ("w_ff_down", 0), ("w_ple_gate", 0), ("w_ple", 1))


def _gather_plan(i, depth):
    mixer = ["w_br_attn", "w_br_sg", "w_out"]
    plan = {
        "mm_in": [(i, "w_ff_gate")] + ([(i, n) for n in mixer] if i == 0 else []),
        "attn_fwd": [(i, "w_ff_up")],
        "mm_ffn_in": [(i, "w_ff_down"), (i, "w_ple_gate"), (i, "w_ple")],
    }
    if i + 1 < depth:
        plan["mm_ffn_out"] = [(i + 1, "w_in")]
        plan["mm_ple"] = [(i + 1, n) for n in mixer]
    return plan


def _layer_fwd(x0, p_i, layer, arrived, sm, tabs, comm):
    S, D = x0.shape
    cos, sin = tabs
    tmm = _tile(S, 1024, 8)
    same = lambda accs, ex: accs

    def W(name):
        return arrived[(layer, name)]

    def hosted(key, fn):
        if key not in comm:
            return fn(None)
        ex, keys = comm[key]
        res, outs = fn(ex)
        arrived.update(zip(keys, outs))
        return res

    h1 = _norm_fwd("norm_fwd", x0, sm["norm_mix"])
    z = hosted("mm_in", lambda ex: _mm_nn_cols("mm_in", h1, [W("w_in")], [bf16], same, comm=ex))[0]
    IN = z.shape[1]

    def grp(base):
        return jnp.stack([_perm(z[:, base + g * ATTN_W: base + (g + 1) * ATTN_W], d) for g, d in enumerate(DILATIONS)])

    qg, kg, vg = grp(0), grp(N_GROUPS * ATTN_W), grp(2 * N_GROUPS * ATTN_W)
    o3, lse3 = hosted("attn_fwd", lambda ex: _attn_fwd(qg, kg, vg, cos, sin, comm=ex))
    o3 = jnp.stack([_unperm(o3[g], d) for g, d in enumerate(DILATIONS)])
    lse3 = jnp.stack([_unperm(lse3[g], d) for g, d in enumerate(DILATIONS)])
    attn, lse = _attn_combine(o3, lse3)
    ya = _mm_nn_cols("mm_br_attn", attn, [W("w_br_attn")], [bf16], same, tm_pref=1024)[0]
    sgw = sm["sg_w"].astype(bf16)
    bb = jnp.broadcast_to(sm["sg_b"][:, :, None], (SG_GROUPS, SG_CHUNK, 128))
    sg = _sg_fwd(z, sgw, bb, sm["sg_ln_g"], sm["sg_ln_b"])
    yb = _mm_nn_cols("mm_br_sg", sg, [W("w_br_sg")], [bf16], same, tm_pref=1024)[0]
    merged = _merge_fwd(z, ya, yb, D)
    tn = _tile(D, 512)
    x1 = _matmul("mm_out", [(merged, W("w_out").reshape(D, D), "nn", 0)], S, D, tmm, tn, 1,
                 [((S, D), f32, (tmm, tn), lambda i, j: (i, j))], lambda accs, ex: [ex[0] + accs[0]],
                 extras=[(x0, (tmm, tn), lambda i, j: (i, j))])[0]
    h2 = _norm_fwd("norm_fwd", x1, sm["norm_ffn"])

    def ffn_ep(accs, ex):
        a, b = accs
        return [a, b, a * _sigmoid(a) * b]

    a, b, f = hosted("mm_ffn_in", lambda ex: _mm_nn_cols("mm_ffn_in", h2, [W("w_ff_gate"), W("w_ff_up")], [bf16] * 3, ffn_ep,
                                                         comm=ex))
    w_down = W("w_ff_down").reshape(-1, D)
    F = w_down.shape[0]
    nk = F // _tile(F, 512)
    x2 = hosted("mm_ffn_out", lambda ex: _matmul(
        "mm_ffn_out", [(f, w_down, "nn", 0)], S, D, tmm, tn, nk, [((S, D), f32, (tmm, tn), lambda i, j: (i, j))],
        lambda accs, ex_tiles: [ex_tiles[0] + accs[0]], extras=[(x1, (tmm, tn), lambda i, j: (i, j))], comm=ex))[0]
    h3 = _norm_fwd("norm_fwd", x2, sm["norm_ple"])

    e = _mm_nn_cols("mm_ple_emb", p_i, [W("w_ple")], [bf16], same, tm_pref=1024)[0]

    def ple_ep(accs, ex):
        gp = accs[0]
        return [ex[0] + _sigmoid(gp) * ex[1].astype(f32), gp]

    x3, gp = hosted("mm_ple", lambda ex: _matmul(
        "mm_ple", [(h3, W("w_ple_gate").reshape(D, D), "nn", 0)], S, D, tmm, tn, 1,
        [((S, D), f32, (tmm, tn), lambda i, j: (i, j)), ((S, D), bf16, (tmm, tn), lambda i, j: (i, j))],
        ple_ep, extras=[(x2, (tmm, tn), lambda i, j: (i, j)), (e, (tmm, tn), lambda i, j: (i, j))], comm=ex))
    saved = dict(x0=x0, h1=h1, z=z, qg=qg, kg=kg, vg=vg, attn=attn, lse=lse, ya=ya, yb=yb, sg=sg, merged=merged, x1=x1,
                 h2=h2, a=a, b=b, f=f, x2=x2, h3=h3, gp=gp, e=e, sgw=sgw, bb=bb, IN=IN)
    return x3, saved


def _layer_bwd(dx3, p_i, W, sm, tabs, sv, c_idx):
    S, D = dx3.shape
    w_out, w_down, w_pg = W["w_out"].reshape(D, D), W["w_ff_down"].reshape(-1, D), W["w_ple_gate"].reshape(D, D)
    F = w_down.shape[0]
    cos, sin = tabs
    tmm = _tile(S, 1024, 8)
    tn = _tile(D, 512)
    reduced = {}

    def chip_sums(grads):
        names = list(grads)
        recv = _run_exchange("rs_sibling_" + names[0], _sibling_exchange([grads[n] for n in names]))
        p4 = {n: _rs_chip_sum("rs_chip_sum_" + n, grads[n], r, c_idx) for n, r in zip(names, recv)}
        return p4, _chips_exchange([p4[n] for n in names])

    def carry(p4, outs):
        reduced.update({n: (p4[n], r) for n, r in zip(p4, outs)})

    def blocks(full):
        return full.reshape(N_DEV, full.shape[0] // N_DEV, full.shape[1])

    dgp, de = _ple_bwd_ew(dx3, sv["gp"], sv["e"])
    d_w_ple = _mm_tn_cols("mm_dw_ple", p_i, [de], D // N_DEV)[0]
    d_w_pg = blocks(_mm_simple("mm_dw_dd", sv["h3"], dgp, "tn", bf16))
    dh3 = _mm_simple("mm_dh_dd", dgp, w_pg, "nt", bf16, 1024, 1024, 2048)
    dx2, dg_ple = _norm_bwd("norm_bwd", dh3, sv["x2"], sm["norm_ple"], dx3)
    tf = _tile(F, 512)

    def ffn_bwd_ep(accs, ex):
        df = accs[0]
        a, b = ex[0].astype(f32), ex[1].astype(f32)
        sg = _sigmoid(a)
        return [df * b * sg * (1.0 + a * (1.0 - sg)), df * a * sg]

    da, db = _matmul("mm_dffn", [(dx2, w_down, "nt", 0)], S, F, tmm, tf, 1,
                     [((S, F), bf16, (tmm, tf), lambda i, j: (i, j))] * 2, ffn_bwd_ep,
                     extras=[(sv["a"], (tmm, tf), lambda i, j: (i, j)), (sv["b"], (tmm, tf), lambda i, j: (i, j))])
    d_w_down = blocks(_mm_simple("mm_dw_fd", sv["f"], dx2, "tn", bf16))
    p4, ex = chip_sums(dict(w_ff_down=d_w_down, w_ple_gate=d_w_pg, w_ple=d_w_ple))
    (d_w_gate, d_w_up), outs = _mm_tn_cols("mm_dw_df", sv["h2"], [da, db], F // N_DEV, comm=ex)
    carry(p4, outs)
    p4, ex = chip_sums(dict(w_ff_gate=d_w_gate, w_ff_up=d_w_up))
    dh2, outs = _mm_nt_cols("mm_dh_ffn", [(da, W["w_ff_gate"]), (db, W["w_ff_up"])], bf16, comm=ex)
    carry(p4, outs)
    dx1, dg_ffn = _norm_bwd("norm_bwd", dh2, sv["x1"], sm["norm_ffn"], dx2)
    z = sv["z"]
    o_a, o_b = (QKV_W + 2 * SG_W) // tn, (QKV_W + 2 * SG_W + D) // tn

    def merge_bwd_ep(accs, ex):
        dm = accs[0]
        ga, gb = _sigmoid(ex[0].astype(f32)), _sigmoid(ex[1].astype(f32))
        ya, yb = ex[2].astype(f32), ex[3].astype(f32)
        return [dm * ya * ga * (1.0 - ga), dm * yb * gb * (1.0 - gb), dm * ga, dm * gb]

    dga, dgb, dya, dyb = _matmul(
        "mm_dmerge", [(dx1, w_out, "nt", 0)], S, D, tmm, tn, 1,
        [((S, D), bf16, (tmm, tn), lambda i, j: (i, j))] * 4, merge_bwd_ep,
        extras=[(z, (tmm, tn), lambda i, j: (i, o_a + j)), (z, (tmm, tn), lambda i, j: (i, o_b + j)),
                (sv["ya"], (tmm, tn), lambda i, j: (i, j)), (sv["yb"], (tmm, tn), lambda i, j: (i, j))])
    d_w_out = blocks(_mm_simple("mm_dw_dd", sv["merged"], dx1, "tn", bf16))
    dsg = _mm_nt_cols("mm_dsg", [(dyb, W["w_br_sg"])], bf16)
    d_w_bsg = _mm_tn_cols("mm_dw_bsg", sv["sg"], [dyb], D // N_DEV)[0]
    dattn = _mm_nt_cols("mm_dattn", [(dya, W["w_br_attn"])], bf16)
    d_w_battn = _mm_tn_cols("mm_dw_battn", sv["attn"], [dya], D // N_DEV)[0]
    sgwt = jnp.swapaxes(sm["sg_w"], 1, 2).astype(bf16)
    du, dv_sg, d_sgw, d_sgb, d_lg, d_lb = _sg_bwd(dsg, z, sv["sgw"], sgwt, sv["bb"], sm["sg_ln_g"], sm["sg_ln_b"])
    dl = _attn_delta(sv["attn"], dattn)

    def grp(t):
        return jnp.stack([_perm(t, d) for d in DILATIONS])

    dag, lg_, dlg_ = grp(dattn), grp(sv["lse"]), grp(dl)
    dqg = _attn_bwd_dq(sv["qg"], sv["kg"], sv["vg"], cos, sin, dag, lg_, dlg_)
    p4, ex = chip_sums(dict(w_out=d_w_out, w_br_sg=d_w_bsg, w_br_attn=d_w_battn))
    (dkg, dvg), outs = _attn_bwd_dkv(sv["qg"], sv["kg"], sv["vg"], cos, sin, dag, lg_, dlg_, comm=ex)
    carry(p4, outs)

    def ungrp(t3):
        return [_unperm(t3[g], d) for g, d in enumerate(DILATIONS)]

    dz = jnp.concatenate(ungrp(dqg) + ungrp(dkg) + ungrp(dvg) + [du, dv_sg, dga, dgb], axis=1)
    d_w_in = _mm_tn_cols("mm_dw_in", sv["h1"], [dz], sv["IN"] // N_DEV)[0]
    p4, ex = chip_sums(dict(w_in=d_w_in))
    dh1, outs = _mm_nt_cols("mm_dh_in", [(dz, W["w_in"])], bf16, comm=ex)
    carry(p4, outs)
    dx0, dg_mix = _norm_bwd("norm_bwd", dh1, sv["x0"], sm["norm_mix"], dx1)
    small = dict(sg_w=d_sgw, sg_b=d_sgb[:, :, 0], sg_ln_g=d_lg[0], sg_ln_b=d_lb[0], norm_mix=dg_mix[0], norm_ffn=dg_ffn[0],
                 norm_ple=dg_ple[0])
    return dx0, reduced, small


_SMALL = ("sg_w", "sg_b", "sg_ln_g", "sg_ln_b", "norm_mix", "norm_ffn", "norm_ple", "norm_final")


SMALL_ROWS = 256


def _pack_small(parts, tail):
    rows = [parts[n].astype(f32).reshape(-1, LANES) for n in _SMALL] + [tail]
    n = sum(r.shape[0] for r in rows)
    return jnp.concatenate(rows + [jnp.zeros((-n % SMALL_ROWS, LANES), f32)], axis=0)


def kernel(x, p, w_in, w_br_attn, w_br_sg, w_out, sg_w, sg_b, sg_ln_g, sg_ln_b, norm_mix, norm_ffn, norm_ple, norm_final, w_ff_gate, w_ff_up, w_ff_down, w_ple_gate, w_ple, loss_target, m_w_in, m_w_br_attn, m_w_br_sg, m_w_out, m_sg_w, m_sg_b, m_sg_ln_g, m_sg_ln_b, m_norm_mix, m_norm_ffn, m_norm_ple, m_norm_final, m_w_ff_gate, m_w_ff_up, m_w_ff_down, m_w_ple_gate, m_w_ple, v_w_in, v_w_br_attn, v_w_br_sg, v_w_out, v_sg_w, v_sg_b, v_sg_ln_g, v_sg_ln_b, v_norm_mix, v_norm_ffn, v_norm_ple, v_norm_final, v_w_ff_gate, v_w_ff_up, v_w_ff_down, v_w_ple_gate, v_w_ple):
    wts = dict(w_in=w_in, w_br_attn=w_br_attn, w_br_sg=w_br_sg, w_out=w_out, w_ff_gate=w_ff_gate, w_ff_up=w_ff_up,
               w_ff_down=w_ff_down, w_ple_gate=w_ple_gate, w_ple=w_ple)
    mom_m = dict(w_in=m_w_in, w_br_attn=m_w_br_attn, w_br_sg=m_w_br_sg, w_out=m_w_out, w_ff_gate=m_w_ff_gate,
                 w_ff_up=m_w_ff_up, w_ff_down=m_w_ff_down, w_ple_gate=m_w_ple_gate, w_ple=m_w_ple)
    mom_v = dict(w_in=v_w_in, w_br_attn=v_w_br_attn, w_br_sg=v_w_br_sg, w_out=v_w_out, w_ff_gate=v_w_ff_gate,
                 w_ff_up=v_w_ff_up, w_ff_down=v_w_ff_down, w_ple_gate=v_w_ple_gate, w_ple=v_w_ple)
    small_w = dict(sg_w=sg_w, sg_b=sg_b, sg_ln_g=sg_ln_g, sg_ln_b=sg_ln_b, norm_mix=norm_mix, norm_ffn=norm_ffn,
                   norm_ple=norm_ple, norm_final=norm_final)
    small_m = dict(sg_w=m_sg_w, sg_b=m_sg_b, sg_ln_g=m_sg_ln_g, sg_ln_b=m_sg_ln_b, norm_mix=m_norm_mix, norm_ffn=m_norm_ffn,
                   norm_ple=m_norm_ple, norm_final=m_norm_final)
    small_v = dict(sg_w=v_sg_w, sg_b=v_sg_b, sg_ln_g=v_sg_ln_g, sg_ln_b=v_sg_ln_b, norm_mix=v_norm_mix, norm_ffn=v_norm_ffn,
                   norm_ple=v_norm_ple, norm_final=v_norm_final)
    depth = w_in.shape[0]
    S = x.shape[1]
    names = [n for n, _ in _BIG]
    c_idx = lax.axis_index("c").astype(jnp.int32).reshape(1)
    q_idx = (2 * lax.axis_index("x") + lax.axis_index("y")).astype(jnp.int32).reshape(1)
    tabs = _rope_tables(S)

    def gather(keys):
        return _gather_exchange([wts[n][i].astype(bf16) for i, n in keys]), keys

    ex, keys = gather([(0, "w_in")])
    arrived = dict(zip(keys, _run_exchange("ag_w_in", ex)))

    xs = x[0]
    saved = []
    for i in range(depth):
        sm = {n: small_w[n][i] for n in _SMALL if n != "norm_final"}
        comm = {carrier: gather(keys) for carrier, keys in _gather_plan(i, depth).items()}
        xs, sv = _layer_fwd(xs, p[i, 0], i, arrived, sm, tabs, comm)
        saved.append(sv)
    dx, dg_final, loss_part = _loss_head(xs, norm_final, loss_target[0])

    reduced = [None] * depth
    small_parts = [None] * depth
    for i in reversed(range(depth)):
        sm = {n: small_w[n][i] for n in _SMALL if n != "norm_final"}
        dx, reduced[i], small_parts[i] = _layer_bwd(dx, p[i, 0], {n: arrived[(i, n)] for n in names}, sm, tabs, saved[i], c_idx)
    grad_x = dx[None]

    parts = {n: jnp.stack([small_parts[i][n] for i in range(depth)]) for n in _SMALL if n != "norm_final"}
    parts["norm_final"] = dg_final[0]
    gathered = _all_gather("ag_small", _pack_small(parts, loss_part), in_vmem=True)
    g_s, d_s, nm_s, nv_s = _small_sum_adamw(gathered, _pack_small(small_w, jnp.zeros((8, LANES), f32)),
                                            _pack_small(small_m, jnp.zeros((8, LANES), f32)),
                                            _pack_small(small_v, jnp.ones((8, LANES), f32)))
    loss = g_s[sum(small_w[n].size for n in _SMALL) // LANES, 0]

    def unpack_small(flat):
        out, off = {}, 0
        for n in _SMALL:
            k = small_w[n].size // LANES
            out[n] = flat[off:off + k].reshape(small_w[n].shape)
            off += k
        return out

    sm_g, sm_d, sm_nm, sm_nv = unpack_small(g_s), unpack_small(d_s), unpack_small(nm_s), unpack_small(nv_s)

    big_g, big_d, big_nm, big_nv = {}, {}, {}, {}
    for k, n in enumerate(names):
        outs = None
        for i in range(depth):
            p4, recv2 = reduced[i][n]
            outs = _adamw_layer(f"adamw_{n}_{i}", i, wts[n], mom_m[n], mom_v[n], p4, recv2, q_idx, outs)
        big_g[n], big_d[n], big_nm[n], big_nv[n] = outs

    order = ["w_in", "w_br_attn", "w_br_sg", "w_out", "sg_w", "sg_b", "sg_ln_g", "sg_ln_b", "norm_mix", "norm_ffn", "norm_ple",
             "norm_final", "w_ff_gate", "w_ff_up", "w_ff_down", "w_ple_gate", "w_ple"]

    def pick(big, small):
        return [big[n] if n in big else small[n] for n in order]

    return (loss, grad_x, *pick(big_g, sm_g), *pick(big_d, sm_d), *pick(big_nm, sm_nm), *pick(big_nv, sm_nv))
```

```python
import functools
import math

import jax
import jax.numpy as jnp
from jax import lax
from jax.experimental import pallas as pl
from jax.experimental.pallas import tpu as pltpu

f32 = jnp.float32
bf16 = jnp.bfloat16

HEAD_DIM = 128
N_GROUPS = 3
HEADS = 4
DILATIONS = (1, 4, 16)
RADIUS = 64
BLK = 128
QKV_W = 3 * N_GROUPS * HEADS * HEAD_DIM
ATTN_W = HEADS * HEAD_DIM
SG_CHUNK = 128
SG_GROUPS = 8
SG_W = SG_GROUPS * 128
ROPE_THETA = 10000.0
EPS = 1e-6
NEG = -1e30
N_DEV = 8
LANES = 128

ADAM_LR = 0.001
ADAM_B1 = 0.9
ADAM_B2 = 0.999
ADAM_EPS = 1e-08
ADAM_WD = 0.01
ADAM_STEP = 10

VMEM_LIMIT = 56 * 1024 * 1024
MESH = pl.DeviceIdType.MESH

NN = (((1,), (0,)), ((), ()))
NT = (((1,), (1,)), ((), ()))
TN = (((0,), (0,)), ((), ()))
_DN = {"nn": NN, "nt": NT, "tn": TN}


def _cp(*sem):
    return pltpu.CompilerParams(dimension_semantics=sem, vmem_limit_bytes=VMEM_LIMIT)


def _tile(n, pref, unit=128):
    if n <= pref:
        return n
    t = (pref // unit) * unit
    while t >= unit:
        if n % t == 0:
            return t
        t -= unit
    return n


_ANY = pl.BlockSpec(memory_space=pl.ANY)


def _call(body, *, name, grid, in_specs, out_specs, out_shape, operands, scratch=(), comm=None):
    in_specs, out_specs, out_shape, scratch = list(in_specs), list(out_specs), list(out_shape), list(scratch)
    if comm is None:
        res = pl.pallas_call(
            body, name=name, out_shape=out_shape, grid=grid, in_specs=in_specs, out_specs=out_specs, scratch_shapes=scratch,
            compiler_params=_cp(*(("arbitrary",) * len(grid))))(*operands)
        return res, []
    n_in, n_out, n_scr = len(in_specs), len(out_specs), len(scratch)
    c_in, c_out = len(comm.ins), len(comm.outs)

    def hosted(*refs):
        own_in, refs = refs[:n_in], refs[n_in:]
        ex_in, refs = refs[:c_in], refs[c_in:]
        own_out, refs = refs[:n_out], refs[n_out:]
        ex_out, refs = refs[:c_out], refs[c_out:]
        own_scr, sems = refs[:n_scr], refs[n_scr:]
        ids = [pl.program_id(a) for a in range(len(grid))]
        first = functools.reduce(jnp.logical_and, [i == 0 for i in ids])
        last = functools.reduce(jnp.logical_and, [i == g - 1 for i, g in zip(ids, grid)])

        @pl.when(first)
        def _():
            comm.start(ex_in, ex_out, sems)

        body(*own_in, *own_out, *own_scr)

        @pl.when(last)
        def _():
            comm.finish(ex_in, ex_out, sems)

    res = pl.pallas_call(
        hosted, name=name, out_shape=out_shape + list(comm.outs), grid=grid,
        in_specs=in_specs + [_ANY] * c_in, out_specs=out_specs + [_ANY] * c_out, scratch_shapes=scratch + list(comm.sems),
        compiler_params=pltpu.CompilerParams(dimension_semantics=("arbitrary",) * len(grid), vmem_limit_bytes=VMEM_LIMIT,
                                             has_side_effects=True),
    )(*operands, *comm.ins)
    return res[:n_out], res[n_out:]


def _sigmoid(x):
    return 1.0 / (1.0 + jnp.exp(-x))


_GC = math.sqrt(2.0 / math.pi)
_GA = 0.044715


def _gelu(x):
    return 0.5 * x * (1.0 + jnp.tanh(_GC * (x + _GA * x * x * x)))


def _gelu_grad(x):
    t = jnp.tanh(_GC * (x + _GA * x * x * x))
    return 0.5 * (1.0 + t) + 0.5 * x * (1.0 - t * t) * _GC * (1.0 + 3.0 * _GA * x * x)


def _matmul(name, prods, M, N, tm, tn, nk, outs, epilogue, extras=(), n_acc=1, comm=None):
    in_specs, operands, metas = [], [], []
    for a, b, mode, acc in prods:
        if mode == "tn":
            tk = a.shape[0] // nk
            in_specs += [pl.BlockSpec((tk, tm), lambda i, j, k: (k, i)), pl.BlockSpec((tk, tn), lambda i, j, k: (k, j))]
        elif mode == "nt":
            tk = a.shape[1] // nk
            in_specs += [pl.BlockSpec((tm, tk), lambda i, j, k: (i, k)), pl.BlockSpec((tn, tk), lambda i, j, k: (j, k))]
        else:
            tk = a.shape[1] // nk
            in_specs += [pl.BlockSpec((tm, tk), lambda i, j, k: (i, k)), pl.BlockSpec((tk, tn), lambda i, j, k: (k, j))]
        operands += [a, b]
        metas.append((mode, acc))
    for arr, bshape, imap in extras:
        in_specs.append(pl.BlockSpec(bshape, functools.partial(lambda i, j, k, f: f(i, j), f=imap)))
        operands.append(arr)
    out_specs = [pl.BlockSpec(bs, functools.partial(lambda i, j, k, f: f(i, j), f=imap)) for _, _, bs, imap in outs]
    out_shape = [jax.ShapeDtypeStruct(s, d) for s, d, _, _ in outs]
    n_prod, n_ext, n_out = len(prods), len(extras), len(outs)

    def body(*refs):
        in_refs = refs[: 2 * n_prod]
        ex_refs = refs[2 * n_prod : 2 * n_prod + n_ext]
        out_refs = refs[2 * n_prod + n_ext : 2 * n_prod + n_ext + n_out]
        acc_refs = refs[2 * n_prod + n_ext + n_out :]

        def partials():
            res = [None] * n_acc
            for idx, (mode, acc) in enumerate(metas):
                a = in_refs[2 * idx][...].astype(bf16)
                b = in_refs[2 * idx + 1][...].astype(bf16)
                d = lax.dot_general(a, b, _DN[mode], preferred_element_type=f32)
                res[acc] = d if res[acc] is None else res[acc] + d
            return res

        def finish(accs):
            vals = epilogue(accs, [r[...] for r in ex_refs])
            for r, v in zip(out_refs, vals):
                r[...] = v.astype(r.dtype)

        if nk == 1:
            finish(partials())
        else:
            k = pl.program_id(2)
            parts = partials()

            @pl.when(k == 0)
            def _():
                for r, d in zip(acc_refs, parts):
                    r[...] = d

            @pl.when(k > 0)
            def _():
                for r, d in zip(acc_refs, parts):
                    r[...] += d

            @pl.when(k == nk - 1)
            def _():
                finish([r[...] for r in acc_refs])

    scratch = [pltpu.VMEM((tm, tn), f32) for _ in range(n_acc)] if nk > 1 else []
    res, ex = _call(body, name=name, grid=(M // tm, N // tn, nk), in_specs=in_specs, out_specs=out_specs,
                    out_shape=out_shape, operands=operands, scratch=scratch, comm=comm)
    return res if comm is None else (res, ex)


def _ident(accs, ex):
    return [accs[0]]


def _mm_simple(name, a, b, mode, out_dtype, tm_pref=1024, tn_pref=1024, tk_pref=1024, comm=None):
    if mode == "tn":
        K, M = a.shape
        N = b.shape[1]
    elif mode == "nt":
        M, K = a.shape
        N = b.shape[0]
    else:
        M, K = a.shape
        N = b.shape[1]
    tm, tn, tk = _tile(M, tm_pref), _tile(N, tn_pref), _tile(K, tk_pref)
    res = _matmul(name, [(a, b, mode, 0)], M, N, tm, tn, K // tk,
                  [((M, N), out_dtype, (tm, tn), lambda i, j: (i, j))], _ident, comm=comm)
    return res[0] if comm is None else (res[0][0], res[1])


def _group(c, width_pref=1024):
    g = LANES // math.gcd(c, LANES)
    while g < N_DEV and 2 * g * c <= width_pref:
        g *= 2
    return g


def _join(parts):
    return parts[0] if len(parts) == 1 else jnp.concatenate(parts, axis=1)


def _mm_nn_cols(name, a, gs_list, outs_dtypes, epilogue, extras=(), tm_pref=512, width_pref=1024, comm=None):
    M, K = a.shape
    c = gs_list[0].shape[2]
    g = _group(c, width_pref)
    W = g * c
    tm = _tile(M, tm_pref, 8)
    n_g, n_ex, n_out = len(gs_list), len(extras), len(outs_dtypes)

    def body(*refs):
        a_ref = refs[0]
        g_refs = refs[1:1 + n_g]
        ex_refs = refs[1 + n_g:1 + n_g + n_ex]
        out_refs = refs[1 + n_g + n_ex:]
        av = a_ref[...].astype(bf16)
        accs = [_join([lax.dot_general(av, gr[s], NN, preferred_element_type=f32) for s in range(g)]) for gr in g_refs]
        for r, v in zip(out_refs, epilogue(accs, [r[...] for r in ex_refs])):
            r[...] = v.astype(r.dtype)

    tile = pl.BlockSpec((tm, W), lambda j, i: (i, j))
    res, ex = _call(
        body, name=name, out_shape=[jax.ShapeDtypeStruct((M, N_DEV * c), d) for d in outs_dtypes],
        grid=(N_DEV // g, M // tm),
        in_specs=[pl.BlockSpec((tm, K), lambda j, i: (i, 0))]
        + [pl.BlockSpec((None, g, K, c), lambda j, i: (j, 0, 0, 0))] * n_g + [tile] * n_ex,
        out_specs=[tile] * n_out, operands=[a, *[gm.reshape(N_DEV // g, g, K, c) for gm in gs_list], *extras], comm=comm)
    return res if comm is None else (res, ex)


def _mm_nt_cols(name, pairs, out_dtype, tm_pref=1024, tn_pref=1024, width_pref=1024, comm=None):
    M = pairs[0][0].shape[0]
    Kw, c = pairs[0][1].shape[1:]
    g = _group(c, width_pref)
    W = g * c
    tm, tn = _tile(M, tm_pref, 8), _tile(Kw, tn_pref)
    nk = N_DEV // g
    n_p = len(pairs)

    def body(*refs):
        o_ref, acc = refs[2 * n_p], refs[2 * n_p + 1]
        k = pl.program_id(2)
        tot = None
        for n in range(n_p):
            d_ref, g_ref = refs[2 * n], refs[2 * n + 1]
            for s in range(g):
                part = lax.dot_general(d_ref[:, s * c:(s + 1) * c], g_ref[s], NT, preferred_element_type=f32)
                tot = part if tot is None else tot + part

        @pl.when(k == 0)
        def _():
            acc[...] = tot

        @pl.when(k > 0)
        def _():
            acc[...] += tot

        @pl.when(k == nk - 1)
        def _():
            o_ref[...] = acc[...].astype(o_ref.dtype)

    in_specs, operands = [], []
    for d, gm in pairs:
        in_specs += [pl.BlockSpec((tm, W), lambda i, j, k: (i, k)), pl.BlockSpec((None, g, tn, c), lambda i, j, k: (k, 0, j, 0))]
        operands += [d, gm.reshape(nk, g, Kw, c)]
    res, ex = _call(
        body, name=name, out_shape=[jax.ShapeDtypeStruct((M, Kw), out_dtype)], grid=(M // tm, Kw // tn, nk),
        in_specs=in_specs, out_specs=[pl.BlockSpec((tm, tn), lambda i, j, k: (i, j))],
        scratch=[pltpu.VMEM((tm, tn), f32)], operands=operands, comm=comm)
    return res[0] if comm is None else (res[0], ex)


def _mm_tn_cols(name, x, ds, c, tm_pref=1024, tk_pref=1024, width_pref=1024, comm=None):
    S, Kw = x.shape
    g = _group(c, width_pref)
    W = g * c
    tm, tk = _tile(Kw, tm_pref), _tile(S, tk_pref, 16)
    nk = S // tk
    n_d = len(ds)

    def body(*refs):
        x_ref = refs[0]
        d_refs = refs[1:1 + n_d]
        o_refs = refs[1 + n_d:1 + 2 * n_d]
        accs = refs[1 + 2 * n_d:]
        k = pl.program_id(2)

        @pl.when(k == 0)
        def _():
            for acc in accs:
                acc[...] = jnp.zeros_like(acc)

        xv = x_ref[...].astype(bf16)
        for d_ref, acc in zip(d_refs, accs):
            for s in range(g):
                acc[s] += lax.dot_general(xv, d_ref[:, s * c:(s + 1) * c], TN, preferred_element_type=f32)

        @pl.when(k == nk - 1)
        def _():
            for o_ref, acc in zip(o_refs, accs):
                o_ref[...] = acc[...].astype(o_ref.dtype)

    outs, ex = _call(
        body, name=name, out_shape=[jax.ShapeDtypeStruct((N_DEV // g, g, Kw, c), bf16)] * n_d,
        grid=(Kw // tm, N_DEV // g, nk),
        in_specs=[pl.BlockSpec((tk, tm), lambda i, j, k: (k, i))] + [pl.BlockSpec((tk, W), lambda i, j, k: (k, j))] * n_d,
        out_specs=[pl.BlockSpec((None, g, tm, c), lambda i, j, k: (j, 0, i, 0))] * n_d,
        scratch=[pltpu.VMEM((g, tm, c), f32)] * n_d, operands=[x, *ds], comm=comm)
    outs = [o.reshape(N_DEV, Kw, c) for o in outs]
    return outs if comm is None else (outs, ex)


def _norm_fwd(name, x, g):
    S, D = x.shape
    tm = _tile(S, 512, 8)

    def body(x_ref, g_ref, h_ref):
        xv = x_ref[...]
        r = lax.rsqrt(jnp.mean(xv * xv, axis=-1, keepdims=True) + EPS)
        h_ref[...] = (xv * r * g_ref[...]).astype(bf16)

    return pl.pallas_call(
        body, name=name, out_shape=jax.ShapeDtypeStruct((S, D), bf16), grid=(S // tm,),
        in_specs=[pl.BlockSpec((tm, D), lambda i: (i, 0)), pl.BlockSpec((1, D), lambda i: (0, 0))],
        out_specs=pl.BlockSpec((tm, D), lambda i: (i, 0)), compiler_params=_cp("parallel"),
    )(x, g.reshape(1, D))


def _norm_bwd(name, dh, x, g, dx_in):
    S, D = x.shape
    tm = _tile(S, 256, 8)

    def body(dh_ref, x_ref, g_ref, dxi_ref, dx_ref, dxb_ref, dg_ref):
        i = pl.program_id(0)
        xv = x_ref[...]
        r = lax.rsqrt(jnp.mean(xv * xv, axis=-1, keepdims=True) + EPS)
        xh = xv * r
        dhv = dh_ref[...].astype(f32)
        dxh = dhv * g_ref[...]
        dx = dxi_ref[...] + r * (dxh - xh * jnp.mean(dxh * xh, axis=-1, keepdims=True))
        dx_ref[...] = dx
        dxb_ref[...] = dx.astype(bf16)

        @pl.when(i == 0)
        def _():
            dg_ref[...] = jnp.zeros_like(dg_ref)

        dg_ref[...] += jnp.sum(dhv * xh, axis=0, keepdims=True)

    row = pl.BlockSpec((tm, D), lambda i: (i, 0))
    vec = pl.BlockSpec((1, D), lambda i: (0, 0))
    return pl.pallas_call(
        body, name=name,
        out_shape=[jax.ShapeDtypeStruct((S, D), f32), jax.ShapeDtypeStruct((S, D), bf16), jax.ShapeDtypeStruct((1, D), f32)],
        grid=(S // tm,), in_specs=[row, row, vec, row], out_specs=[row, row, vec], compiler_params=_cp("arbitrary"),
    )(dh, x, g.reshape(1, D), dx_in)


def _loss_head(x, g, t):
    S, D = x.shape
    tm = _tile(S, 256, 8)

    def body(x_ref, g_ref, t_ref, dx_ref, dg_ref, loss_ref):
        i = pl.program_id(0)
        xv = x_ref[...]
        r = lax.rsqrt(jnp.mean(xv * xv, axis=-1, keepdims=True) + EPS)
        xh = xv * r
        gv = g_ref[...]
        err = xh * gv - t_ref[...]
        dy = err * (1.0 / D)
        dxh = dy * gv
        dx_ref[...] = r * (dxh - xh * jnp.mean(dxh * xh, axis=-1, keepdims=True))

        @pl.when(i == 0)
        def _():
            dg_ref[...] = jnp.zeros_like(dg_ref)
            loss_ref[...] = jnp.zeros_like(loss_ref)

        dg_ref[...] += jnp.sum(dy * xh, axis=0, keepdims=True)
        row = jnp.sum(err * err, axis=-1, keepdims=True) * (0.5 / D)
        loss_ref[...] += jnp.broadcast_to(jnp.sum(row, axis=0, keepdims=True), loss_ref.shape)

    return pl.pallas_call(
        body, name="loss_head",
        out_shape=[jax.ShapeDtypeStruct((S, D), f32), jax.ShapeDtypeStruct((1, D), f32), jax.ShapeDtypeStruct((8, LANES), f32)],
        grid=(S // tm,),
        in_specs=[pl.BlockSpec((tm, D), lambda i: (i, 0)), pl.BlockSpec((1, D), lambda i: (0, 0)), pl.BlockSpec((tm, D), lambda i: (i, 0))],
        out_specs=[pl.BlockSpec((tm, D), lambda i: (i, 0)), pl.BlockSpec((1, D), lambda i: (0, 0)), pl.BlockSpec((8, LANES), lambda i: (0, 0))],
        compiler_params=_cp("arbitrary"),
    )(x, g.reshape(1, D), t)


def _perm(t, d):
    if d == 1:
        return t
    S, C = t.shape
    return t.reshape(S // d, d, C).transpose(1, 0, 2).reshape(S, C)


def _unperm(t, d):
    if d == 1:
        return t
    S, C = t.shape
    return t.reshape(d, S // d, C).transpose(1, 0, 2).reshape(S, C)


def _rope_tables(S):
    half = HEAD_DIM // 2
    pos = jnp.arange(S, dtype=f32)
    inv_freq = ROPE_THETA ** (-jnp.arange(0, HEAD_DIM, 2, dtype=f32) / HEAD_DIM)
    ang = pos[:, None] * inv_freq[None, :]
    c, s = jnp.cos(ang), jnp.sin(ang)
    cos2 = jnp.concatenate([c, c], axis=-1)
    sin2 = jnp.concatenate([-s, s], axis=-1)
    assert cos2.shape == (S, 2 * half)
    return (jnp.stack([_perm(cos2, d) for d in DILATIONS]), jnp.stack([_perm(sin2, d) for d in DILATIONS]))


def _rope(t, c, s):
    return t * c + pltpu.roll(t, HEAD_DIM // 2, 1) * s


def _rope_bwd(dt, c, s):
    return dt * c - pltpu.roll(dt, HEAD_DIM // 2, 1) * s


def _band_bounds(i, nblk):
    g = pl.program_id(0)
    lb = jnp.right_shift(jnp.int32(nblk), 2 * g)
    pos = lax.rem(i, lb)
    lo = jnp.where(pos == 0, BLK, 0)
    hi = jnp.where(pos == lb - 1, 2 * BLK, 3 * BLK)
    return lo, hi


def _band_specs(width, nblk):
    prev = pl.BlockSpec((None, BLK, width), lambda g, i: (g, jnp.maximum(i - 1, 0), 0))
    cur = pl.BlockSpec((None, BLK, width), lambda g, i: (g, i, 0))
    nxt = pl.BlockSpec((None, BLK, width), lambda g, i: (g, jnp.minimum(i + 1, nblk - 1), 0))
    return [prev, cur, nxt]


_SCALE = HEAD_DIM ** -0.5


def _attn_fwd(q, k, v, cos, sin, comm=None):
    _, S, W = q.shape
    nblk = S // BLK

    def body(q_ref, kp, kc, kn, vp, vc, vn, cq, sq, ckp, ckc, ckn, skp, skc, skn, o_ref, lse_ref):
        i = pl.program_id(1)
        lo, hi = _band_bounds(i, nblk)
        a = lax.broadcasted_iota(jnp.int32, (BLK, 3 * BLK), 0)
        b = lax.broadcasted_iota(jnp.int32, (BLK, 3 * BLK), 1)
        mask = (jnp.abs(b - BLK - a) <= RADIUS) & (b >= lo) & (b < hi)
        ck = jnp.concatenate([ckp[...], ckc[...], ckn[...]], axis=0)
        sk = jnp.concatenate([skp[...], skc[...], skn[...]], axis=0)
        for hh in range(HEADS):
            sl = slice(hh * HEAD_DIM, (hh + 1) * HEAD_DIM)
            qh = _rope(q_ref[:, sl].astype(f32), cq[...], sq[...]).astype(bf16)
            kh = jnp.concatenate([kp[:, sl], kc[:, sl], kn[:, sl]], axis=0).astype(f32)
            kh = _rope(kh, ck, sk).astype(bf16)
            vh = jnp.concatenate([vp[:, sl], vc[:, sl], vn[:, sl]], axis=0)
            s = lax.dot_general(qh, kh, NT, preferred_element_type=f32) * _SCALE
            s = jnp.where(mask, s, NEG)
            m = jnp.max(s, axis=-1, keepdims=True)
            e = jnp.exp(s - m)
            den = jnp.sum(e, axis=-1, keepdims=True)
            o = lax.dot_general(e.astype(bf16), vh, NN, preferred_element_type=f32) * (1.0 / den)
            o_ref[:, sl] = o.astype(bf16)
            lse_ref[:, sl] = jnp.broadcast_to(m + jnp.log(den), (BLK, HEAD_DIM))

    blk = pl.BlockSpec((None, BLK, W), lambda g, i: (g, i, 0))
    tab = pl.BlockSpec((None, BLK, HEAD_DIM), lambda g, i: (g, i, 0))
    res, ex = _call(
        body, name="attn_fwd",
        out_shape=[jax.ShapeDtypeStruct((N_GROUPS, S, W), bf16), jax.ShapeDtypeStruct((N_GROUPS, S, W), f32)],
        grid=(N_GROUPS, nblk),
        in_specs=[blk] + _band_specs(W, nblk) * 2 + [tab, tab] + _band_specs(HEAD_DIM, nblk) * 2,
        out_specs=[blk, blk], operands=[q, k, k, k, v, v, v, cos, sin, cos, cos, cos, sin, sin, sin], comm=comm)
    return res if comm is None else (res, ex)


def _attn_combine(o3, lse3):
    _, S, W = o3.shape
    tm = _tile(S, 512, 8)

    def body(o_ref, l_ref, attn_ref, lse_ref):
        l0, l1, l2 = l_ref[0], l_ref[1], l_ref[2]
        m = jnp.maximum(jnp.maximum(l0, l1), l2)
        w0, w1, w2 = jnp.exp(l0 - m), jnp.exp(l1 - m), jnp.exp(l2 - m)
        den = w0 + w1 + w2
        acc = w0 * o_ref[0].astype(f32) + w1 * o_ref[1].astype(f32) + w2 * o_ref[2].astype(f32)
        attn_ref[...] = (acc * (1.0 / den)).astype(bf16)
        lse_ref[...] = m + jnp.log(den)

    blk3 = pl.BlockSpec((N_GROUPS, tm, W), lambda i: (0, i, 0))
    blk = pl.BlockSpec((tm, W), lambda i: (i, 0))
    return pl.pallas_call(
        body, name="attn_combine", out_shape=[jax.ShapeDtypeStruct((S, W), bf16), jax.ShapeDtypeStruct((S, W), f32)],
        grid=(S // tm,), in_specs=[blk3, blk3], out_specs=[blk, blk], compiler_params=_cp("parallel"),
    )(o3, lse3)


def _attn_delta(attn, dattn):
    S, W = attn.shape
    tm = _tile(S, 512, 8)

    def body(a_ref, d_ref, o_ref):
        prod = a_ref[...].astype(f32) * d_ref[...].astype(f32)
        for hh in range(HEADS):
            sl = slice(hh * HEAD_DIM, (hh + 1) * HEAD_DIM)
            o_ref[:, sl] = jnp.broadcast_to(jnp.sum(prod[:, sl], axis=-1, keepdims=True), (tm, HEAD_DIM))

    blk = pl.BlockSpec((tm, W), lambda i: (i, 0))
    return pl.pallas_call(
        body, name="attn_delta", out_shape=jax.ShapeDtypeStruct((S, W), f32), grid=(S // tm,),
        in_specs=[blk, blk], out_specs=blk, compiler_params=_cp("parallel"),
    )(attn, dattn)


def _attn_bwd_dq(q, k, v, cos, sin, da, lse, dl):
    _, S, W = q.shape
    nblk = S // BLK

    def body(q_ref, kp, kc, kn, vp, vc, vn, cq, sq, ckp, ckc, ckn, skp, skc, skn, da_ref, l_ref, dl_ref, dq_ref):
        i = pl.program_id(1)
        lo, hi = _band_bounds(i, nblk)
        a = lax.broadcasted_iota(jnp.int32, (BLK, 3 * BLK), 0)
        b = lax.broadcasted_iota(jnp.int32, (BLK, 3 * BLK), 1)
        mask = (jnp.abs(b - BLK - a) <= RADIUS) & (b >= lo) & (b < hi)
        ck = jnp.concatenate([ckp[...], ckc[...], ckn[...]], axis=0)
        sk = jnp.concatenate([skp[...], skc[...], skn[...]], axis=0)
        for hh in range(HEADS):
            sl = slice(hh * HEAD_DIM, (hh + 1) * HEAD_DIM)
            qh = _rope(q_ref[:, sl].astype(f32), cq[...], sq[...]).astype(bf16)
            kh = jnp.concatenate([kp[:, sl], kc[:, sl], kn[:, sl]], axis=0).astype(f32)
            kh = _rope(kh, ck, sk).astype(bf16)
            vh = jnp.concatenate([vp[:, sl], vc[:, sl], vn[:, sl]], axis=0)
            s = lax.dot_general(qh, kh, NT, preferred_element_type=f32) * _SCALE
            lh = l_ref[:, sl]
            l3 = jnp.concatenate([lh, lh, lh], axis=1)
            p = jnp.exp(jnp.where(mask, s - l3, NEG))
            dp = lax.dot_general(da_ref[:, sl], vh, NT, preferred_element_type=f32)
            dh = dl_ref[:, sl]
            ds = p * (dp - jnp.concatenate([dh, dh, dh], axis=1))
            dqh = lax.dot_general(ds.astype(bf16), kh, NN, preferred_element_type=f32) * _SCALE
            dq_ref[:, sl] = _rope_bwd(dqh, cq[...], sq[...]).astype(bf16)

    blk = pl.BlockSpec((None, BLK, W), lambda g, i: (g, i, 0))
    tab = pl.BlockSpec((None, BLK, HEAD_DIM), lambda g, i: (g, i, 0))
    return pl.pallas_call(
        body, name="attn_bwd_dq", out_shape=jax.ShapeDtypeStruct((N_GROUPS, S, W), bf16), grid=(N_GROUPS, nblk),
        in_specs=[blk] + _band_specs(W, nblk) * 2 + [tab, tab] + _band_specs(HEAD_DIM, nblk) * 2 + [blk, blk, blk],
        out_specs=blk, compiler_params=_cp("parallel", "parallel"),
    )(q, k, k, k, v, v, v, cos, sin, cos, cos, cos, sin, sin, sin, da, lse, dl)


def _attn_bwd_dkv(q, k, v, cos, sin, da, lse, dl, comm=None):
    _, S, W = q.shape
    nblk = S // BLK

    def body(k_ref, v_ref, ck, sk, qp, qc, qn, cqp, cqc, cqn, sqp, sqc, sqn, dap, dac, dan, lp, lc, ln, dlp, dlc, dln,
             dk_ref, dv_ref):
        j = pl.program_id(1)
        lo, hi = _band_bounds(j, nblk)
        a = lax.broadcasted_iota(jnp.int32, (3 * BLK, BLK), 0)
        b = lax.broadcasted_iota(jnp.int32, (3 * BLK, BLK), 1)
        mask = (jnp.abs(b - (a - BLK)) <= RADIUS) & (a >= lo) & (a < hi)
        cq = jnp.concatenate([cqp[...], cqc[...], cqn[...]], axis=0)
        sq = jnp.concatenate([sqp[...], sqc[...], sqn[...]], axis=0)
        for hh in range(HEADS):
            sl = slice(hh * HEAD_DIM, (hh + 1) * HEAD_DIM)
            kh = _rope(k_ref[:, sl].astype(f32), ck[...], sk[...]).astype(bf16)
            vh = v_ref[:, sl]
            qh = jnp.concatenate([qp[:, sl], qc[:, sl], qn[:, sl]], axis=0).astype(f32)
            qh = _rope(qh, cq, sq).astype(bf16)
            dah = jnp.concatenate([dap[:, sl], dac[:, sl], dan[:, sl]], axis=0)
            lh = jnp.concatenate([lp[:, sl], lc[:, sl], ln[:, sl]], axis=0)
            dlh = jnp.concatenate([dlp[:, sl], dlc[:, sl], dln[:, sl]], axis=0)
            s = lax.dot_general(qh, kh, NT, preferred_element_type=f32) * _SCALE
            p = jnp.exp(jnp.where(mask, s - lh, NEG))
            dv_ref[:, sl] = lax.dot_general(p.astype(bf16), dah, TN, preferred_element_type=f32).astype(bf16)
            dp = lax.dot_general(dah, vh, NT, preferred_element_type=f32)
            ds = p * (dp - dlh)
            dkh = lax.dot_general(ds.astype(bf16), qh, TN, preferred_element_type=f32) * _SCALE
            dk_ref[:, sl] = _rope_bwd(dkh, ck[...], sk[...]).astype(bf16)

    blk = pl.BlockSpec((None, BLK, W), lambda g, i: (g, i, 0))
    tab = pl.BlockSpec((None, BLK, HEAD_DIM), lambda g, i: (g, i, 0))
    bw, bt = _band_specs(W, nblk), _band_specs(HEAD_DIM, nblk)
    res, ex = _call(
        body, name="attn_bwd_dkv",
        out_shape=[jax.ShapeDtypeStruct((N_GROUPS, S, W), bf16), jax.ShapeDtypeStruct((N_GROUPS, S, W), bf16)],
        grid=(N_GROUPS, nblk),
        in_specs=[blk, blk, tab, tab] + bw + bt + bt + bw + bw + bw, out_specs=[blk, blk],
        operands=[k, v, cos, sin, q, q, q, cos, cos, cos, sin, sin, sin, da, da, da, lse, lse, lse, dl, dl, dl], comm=comm)
    return res if comm is None else (res, ex)


_SG_ROWS = 512


def _sg_z_specs(tm, half):
    o = QKV_W // half
    return [pl.BlockSpec((tm, half), functools.partial(lambda i, c: (i, c), c=o + n)) for n in range(4)]


def _sg_norm(v, lg, lb):
    gv = _gelu(v)
    mu = jnp.mean(gv, axis=-1, keepdims=True)
    xc = gv - mu
    rstd = lax.rsqrt(jnp.mean(xc * xc, axis=-1, keepdims=True) + EPS)
    xh = xc * rstd
    return xh, rstd, xh * lg + lb


def _sg_fwd(z, w, bb, lg, lb):
    S = z.shape[0]
    tm = _tile(S, _SG_ROWS, SG_CHUNK)
    half = SG_W // 2

    def body(u0, u1, v0, v1, w_ref, bb_ref, lg_ref, lb_ref, o_ref):
        u = jnp.concatenate([u0[...], u1[...]], axis=1).astype(f32)
        v = jnp.concatenate([v0[...], v1[...]], axis=1).astype(f32)
        gu = _gelu(u)
        _, _, vn = _sg_norm(v, lg_ref[...], lb_ref[...])
        vnb = vn.astype(bf16)
        for c in range(tm // SG_CHUNK):
            rs = slice(c * SG_CHUNK, (c + 1) * SG_CHUNK)
            for g in range(SG_GROUPS):
                cs = slice(g * 128, (g + 1) * 128)
                mixed = lax.dot_general(w_ref[g], vnb[rs, cs], NN, preferred_element_type=f32) + bb_ref[g]
                o_ref[rs, cs] = (gu[rs, cs] * mixed).astype(bf16)

    full3 = pl.BlockSpec((SG_GROUPS, 128, 128), lambda i: (0, 0, 0))
    vec = pl.BlockSpec((1, SG_W), lambda i: (0, 0))
    return pl.pallas_call(
        body, name="sg_fwd", out_shape=jax.ShapeDtypeStruct((S, SG_W), bf16), grid=(S // tm,),
        in_specs=_sg_z_specs(tm, half) + [full3, full3, vec, vec],
        out_specs=pl.BlockSpec((tm, SG_W), lambda i: (i, 0)), compiler_params=_cp("parallel"),
    )(z, z, z, z, w, bb, lg.reshape(1, SG_W), lb.reshape(1, SG_W))


def _sg_bwd(dsg, z, w, wt, bb, lg, lb):
    S = z.shape[0]
    tm = _tile(S, _SG_ROWS, SG_CHUNK)
    half = SG_W // 2

    def body(d_ref, u0, u1, v0, v1, w_ref, wt_ref, bb_ref, lg_ref, lb_ref, du_ref, dv_ref, dw_ref, db_ref, dlg_ref, dlb_ref, dvn_scr):
        i = pl.program_id(0)

        @pl.when(i == 0)
        def _():
            dw_ref[...] = jnp.zeros_like(dw_ref)
            db_ref[...] = jnp.zeros_like(db_ref)
            dlg_ref[...] = jnp.zeros_like(dlg_ref)
            dlb_ref[...] = jnp.zeros_like(dlb_ref)

        u = jnp.concatenate([u0[...], u1[...]], axis=1).astype(f32)
        v = jnp.concatenate([v0[...], v1[...]], axis=1).astype(f32)
        gu = _gelu(u)
        dgu = _gelu_grad(u)
        xh, rstd, vn = _sg_norm(v, lg_ref[...], lb_ref[...])
        vnb = vn.astype(bf16)
        dsg_v = d_ref[...].astype(f32)
        for g in range(SG_GROUPS):
            cs = slice(g * 128, (g + 1) * 128)
            dw_g = jnp.zeros((128, 128), f32)
            db_g = jnp.zeros((128, 1), f32)
            for c in range(tm // SG_CHUNK):
                rs = slice(c * SG_CHUNK, (c + 1) * SG_CHUNK)
                ds = dsg_v[rs, cs]
                mixed = lax.dot_general(w_ref[g], vnb[rs, cs], NN, preferred_element_type=f32) + bb_ref[g]
                du_ref[rs, cs] = (ds * mixed * dgu[rs, cs]).astype(bf16)
                dmix = ds * gu[rs, cs]
                dmb = dmix.astype(bf16)
                dw_g = dw_g + lax.dot_general(dmb, vnb[rs, cs], NT, preferred_element_type=f32)
                db_g = db_g + jnp.sum(dmix, axis=-1, keepdims=True)
                dvn_scr[rs, cs] = lax.dot_general(wt_ref[g], dmb, NN, preferred_element_type=f32)
            dw_ref[g] += dw_g
            db_ref[g] += jnp.broadcast_to(db_g, (128, 128))
        dvn = dvn_scr[...]
        dlg_ref[...] += jnp.sum(dvn * xh, axis=0, keepdims=True)
        dlb_ref[...] += jnp.sum(dvn, axis=0, keepdims=True)
        dxh = dvn * lg_ref[...]
        dgv = rstd * (dxh - jnp.mean(dxh, axis=-1, keepdims=True) - xh * jnp.mean(dxh * xh, axis=-1, keepdims=True))
        dv_ref[...] = (dgv * _gelu_grad(v)).astype(bf16)

    full3 = pl.BlockSpec((SG_GROUPS, 128, 128), lambda i: (0, 0, 0))
    vec = pl.BlockSpec((1, SG_W), lambda i: (0, 0))
    row = pl.BlockSpec((tm, SG_W), lambda i: (i, 0))
    return pl.pallas_call(
        body, name="sg_bwd",
        out_shape=[jax.ShapeDtypeStruct((S, SG_W), bf16), jax.ShapeDtypeStruct((S, SG_W), bf16),
                   jax.ShapeDtypeStruct((SG_GROUPS, 128, 128), f32), jax.ShapeDtypeStruct((SG_GROUPS, 128, 128), f32),
                   jax.ShapeDtypeStruct((1, SG_W), f32), jax.ShapeDtypeStruct((1, SG_W), f32)],
        grid=(S // tm,),
        in_specs=[row] + _sg_z_specs(tm, half) + [full3, full3, full3, vec, vec],
        out_specs=[row, row, full3, full3, vec, vec],
        scratch_shapes=[pltpu.VMEM((tm, SG_W), f32)], compiler_params=_cp("arbitrary"),
    )(dsg, z, z, z, z, w, wt, bb, lg.reshape(1, SG_W), lb.reshape(1, SG_W))


def _merge_fwd(z, ya, yb, D):
    S = z.shape[0]
    tm, tc = _tile(S, 512, 8), _tile(D, 512)
    o_a, o_b = (QKV_W + 2 * SG_W) // tc, (QKV_W + 2 * SG_W + D) // tc

    def body(ga_ref, gb_ref, ya_ref, yb_ref, o_ref):
        ga = _sigmoid(ga_ref[...].astype(f32))
        gb = _sigmoid(gb_ref[...].astype(f32))
        o_ref[...] = (ga * ya_ref[...].astype(f32) + gb * yb_ref[...].astype(f32)).astype(bf16)

    blk = pl.BlockSpec((tm, tc), lambda i, j: (i, j))
    return pl.pallas_call(
        body, name="merge_fwd", out_shape=jax.ShapeDtypeStruct((S, D), bf16), grid=(S // tm, D // tc),
        in_specs=[pl.BlockSpec((tm, tc), lambda i, j: (i, o_a + j)), pl.BlockSpec((tm, tc), lambda i, j: (i, o_b + j)), blk, blk],
        out_specs=blk, compiler_params=_cp("parallel", "parallel"),
    )(z, z, ya, yb)


def _ple_bwd_ew(dx, gp, e):
    S, D = dx.shape
    tm, tc = _tile(S, 512, 8), _tile(D, 1024)

    def body(dx_ref, gp_ref, e_ref, dgp_ref, de_ref):
        dxv = dx_ref[...]
        sg = _sigmoid(gp_ref[...].astype(f32))
        dgp_ref[...] = (dxv * e_ref[...].astype(f32) * sg * (1.0 - sg)).astype(bf16)
        de_ref[...] = (dxv * sg).astype(bf16)

    blk = pl.BlockSpec((tm, tc), lambda i, j: (i, j))
    return pl.pallas_call(
        body, name="ple_bwd_ew", out_shape=[jax.ShapeDtypeStruct((S, D), bf16)] * 2, grid=(S // tm, D // tc),
        in_specs=[blk, blk, blk], out_specs=[blk, blk], compiler_params=_cp("parallel", "parallel"),
    )(dx, gp, e)


def _adam_math(w, g, m, v):
    m = ADAM_B1 * m + (1.0 - ADAM_B1) * g
    v = ADAM_B2 * v + (1.0 - ADAM_B2) * (g * g)
    m_hat = m / (1.0 - ADAM_B1 ** ADAM_STEP)
    v_hat = v / (1.0 - ADAM_B2 ** ADAM_STEP)
    delta = -ADAM_LR * (m_hat / (jnp.sqrt(v_hat) + ADAM_EPS) + ADAM_WD * w)
    return delta, m, v


def _small_sum_adamw(gathered, w, m, v):
    _, R, _ = gathered.shape
    tr = _tile(R, 1024, SMALL_ROWS)

    def body(p_ref, w_ref, m_ref, v_ref, g_ref, d_ref, nm_ref, nv_ref):
        g = p_ref[0]
        for n in range(1, N_DEV):
            g = g + p_ref[n]
        d, nm, nv = _adam_math(w_ref[...], g, m_ref[...], v_ref[...])
        g_ref[...] = g
        d_ref[...] = d
        nm_ref[...] = nm
        nv_ref[...] = nv

    blk = pl.BlockSpec((tr, LANES), lambda i: (i, 0))
    return pl.pallas_call(
        body, name="small_sum_adamw", out_shape=[jax.ShapeDtypeStruct((R, LANES), f32)] * 4, grid=(R // tr,),
        in_specs=[pl.BlockSpec((N_DEV, tr, LANES), lambda i: (0, i, 0)), blk, blk, blk], out_specs=[blk] * 4,
        compiler_params=_cp("parallel"),
    )(gathered, w, m, v)


def _all_gather(name, shard, in_vmem=False):
    R, C = shard.shape

    def body(x_ref, out_ref, send_sems, recv_sems, local_sem):
        x, y, c = lax.axis_index("x"), lax.axis_index("y"), lax.axis_index("c")
        me, sibling = (x, y, c), (x, y, 1 - c)
        chips = [(1 - x, y), (x, 1 - y), (1 - x, 1 - y)]

        def rows(px, py, pc):
            return out_ref.at[4 * px + 2 * py + pc]

        def copy(k, block, to, src=None):
            return pltpu.make_async_remote_copy(
                src_ref=rows(*block) if src is None else src, dst_ref=rows(*block),
                send_sem=send_sems.at[k], recv_sem=recv_sems.at[k], device_id=to, device_id_type=MESH)

        mine = pltpu.make_async_copy(x_ref, rows(*me), local_sem)
        mine.start()
        first = [copy(0, me, sibling, src=x_ref)]
        first += [copy(1 + j, me, (*chip, c), src=x_ref) for j, chip in enumerate(chips)]
        for cp in first:
            cp.start()
        passed = [copy(4 + j, (*chip, c), sibling) for j, chip in enumerate(chips)]
        for j, chip in enumerate(chips):
            copy(1 + j, (*chip, c), me).wait_recv()
            passed[j].start()
        copy(0, sibling, me).wait_recv()
        for j, chip in enumerate(chips):
            copy(4 + j, (*chip, 1 - c), me).wait_recv()
        for cp in first + passed:
            cp.wait_send()
        mine.wait()

    space = pl.BlockSpec(memory_space=pltpu.VMEM) if in_vmem else _ANY
    return pl.pallas_call(
        body, name=name, out_shape=jax.ShapeDtypeStruct((N_DEV, R, C), shard.dtype),
        in_specs=[space], out_specs=space,
        scratch_shapes=[pltpu.SemaphoreType.DMA((7,)), pltpu.SemaphoreType.DMA((7,)), pltpu.SemaphoreType.DMA],
        compiler_params=pltpu.CompilerParams(has_side_effects=True, vmem_limit_bytes=VMEM_LIMIT),
    )(shard)


class _Exchange:
    def __init__(self, ins, outs, sems, start, finish):
        self.ins, self.outs, self.sems, self.start, self.finish = list(ins), list(outs), list(sems), start, finish


def _run_exchange(name, ex):
    c_in, c_out = len(ex.ins), len(ex.outs)

    def body(*refs):
        ins, outs, sems = refs[:c_in], refs[c_in:c_in + c_out], refs[c_in + c_out:]
        ex.start(ins, outs, sems)
        ex.finish(ins, outs, sems)

    return pl.pallas_call(
        body, name=name, out_shape=ex.outs, in_specs=[_ANY] * c_in, out_specs=[_ANY] * c_out, scratch_shapes=ex.sems,
        compiler_params=pltpu.CompilerParams(has_side_effects=True, vmem_limit_bytes=VMEM_LIMIT),
    )(*ex.ins)


def _gather_exchange(shards):
    n = len(shards)

    def plan(ins, outs, sems):
        send_sems, recv_sems, local_sems = sems
        x, y, c = lax.axis_index("x"), lax.axis_index("y"), lax.axis_index("c")
        me, sibling = (x, y, c), (x, y, 1 - c)
        chips = [(1 - x, y), (x, 1 - y), (1 - x, 1 - y)]

        def rows(w, px, py, pc):
            return outs[w].at[4 * px + 2 * py + pc]

        def copy(w, k, block, to, src=None):
            return pltpu.make_async_remote_copy(
                src_ref=rows(w, *block) if src is None else src, dst_ref=rows(w, *block),
                send_sem=send_sems.at[w, k], recv_sem=recv_sems.at[w, k], device_id=to, device_id_type=MESH)

        mine = [pltpu.make_async_copy(ins[w], rows(w, *me), local_sems.at[w]) for w in range(n)]
        first = []
        for w in range(n):
            first.append(copy(w, 0, me, sibling, src=ins[w]))
            first += [copy(w, 1 + j, me, (*chip, c), src=ins[w]) for j, chip in enumerate(chips)]
        return c, me, sibling, chips, copy, mine, first

    def start(ins, outs, sems):
        _, _, _, _, _, mine, first = plan(ins, outs, sems)
        for cp in mine + first:
            cp.start()

    def finish(ins, outs, sems):
        c, me, sibling, chips, copy, mine, first = plan(ins, outs, sems)
        passed = []
        for w in range(n):
            for j, chip in enumerate(chips):
                copy(w, 1 + j, (*chip, c), me).wait_recv()
                passed.append(copy(w, 4 + j, (*chip, c), sibling))
                passed[-1].start()
        for w in range(n):
            copy(w, 0, sibling, me).wait_recv()
            for j, chip in enumerate(chips):
                copy(w, 4 + j, (*chip, 1 - c), me).wait_recv()
        for cp in first + passed:
            cp.wait_send()
        for cp in mine:
            cp.wait()

    return _Exchange(
        shards, [jax.ShapeDtypeStruct((N_DEV,) + s.shape, s.dtype) for s in shards],
        [pltpu.SemaphoreType.DMA((n, 7)), pltpu.SemaphoreType.DMA((n, 7)), pltpu.SemaphoreType.DMA((n,))], start, finish)


def _sibling_exchange(gs):
    n = len(gs)

    def copies(ins, outs, sems):
        send_sems, recv_sems = sems
        x, y, c = lax.axis_index("x"), lax.axis_index("y"), lax.axis_index("c")
        return [pltpu.make_async_remote_copy(
            src_ref=ins[w].at[2 * q + (1 - c)], dst_ref=outs[w].at[q], send_sem=send_sems.at[w, q],
            recv_sem=recv_sems.at[w, q], device_id=(x, y, 1 - c), device_id_type=MESH) for w in range(n) for q in range(4)]

    def start(ins, outs, sems):
        for cp in copies(ins, outs, sems):
            cp.start()

    def finish(ins, outs, sems):
        cps = copies(ins, outs, sems)
        for cp in cps:
            cp.wait_recv()
        for cp in cps:
            cp.wait_send()

    return _Exchange(gs, [jax.ShapeDtypeStruct((4,) + g.shape[1:], g.dtype) for g in gs],
                     [pltpu.SemaphoreType.DMA((n, 4)), pltpu.SemaphoreType.DMA((n, 4))], start, finish)


def _rs_chip_sum(name, g8, recv, c_idx):
    _, R, C = g8.shape
    tr = _tile(R, 512, 16)
    g42 = g8.reshape(4, 2, R, C)

    def body(c_ref, a_ref, b_ref, o_ref):
        o_ref[...] = (a_ref[...].astype(f32) + b_ref[...].astype(f32)).astype(o_ref.dtype)

    return pl.pallas_call(
        body, name=name, out_shape=jax.ShapeDtypeStruct((4, R, C), g8.dtype),
        grid_spec=pltpu.PrefetchScalarGridSpec(
            num_scalar_prefetch=1, grid=(4, R // tr),
            in_specs=[pl.BlockSpec((None, None, tr, C), lambda q, r, c_ref: (q, c_ref[0], r, 0)),
                      pl.BlockSpec((None, tr, C), lambda q, r, c_ref: (q, r, 0))],
            out_specs=pl.BlockSpec((None, tr, C), lambda q, r, c_ref: (q, r, 0))),
        compiler_params=_cp("parallel", "parallel"),
    )(c_idx, g42, recv)


def _chips_exchange(p4s):
    n = len(p4s)

    def copies(ins, outs, sems):
        send_sems, recv_sems = sems
        x, y, c = lax.axis_index("x"), lax.axis_index("y"), lax.axis_index("c")
        chips = [(1 - x, y), (x, 1 - y), (1 - x, 1 - y)]
        return [pltpu.make_async_remote_copy(
            src_ref=ins[w].at[2 * cx + cy], dst_ref=outs[w].at[k], send_sem=send_sems.at[w, k],
            recv_sem=recv_sems.at[w, k], device_id=(cx, cy, c), device_id_type=MESH)
            for w in range(n) for k, (cx, cy) in enumerate(chips)]

    def start(ins, outs, sems):
        for cp in copies(ins, outs, sems):
            cp.start()

    def finish(ins, outs, sems):
        cps = copies(ins, outs, sems)
        for cp in cps:
            cp.wait_recv()
        for cp in cps:
            cp.wait_send()

    return _Exchange(p4s, [jax.ShapeDtypeStruct((3,) + p.shape[1:], p.dtype) for p in p4s],
                     [pltpu.SemaphoreType.DMA((n, 3)), pltpu.SemaphoreType.DMA((n, 3))], start, finish)


def _adamw_layer(name, layer, w, m, v, p4, recv, q_idx, prev):
    depth, R, C = w.shape
    tr = _tile(R, 256, 8)

    def body(q_ref, w_ref, m_ref, v_ref, a_ref, b_ref, *rest):
        g_ref, d_ref, nm_ref, nv_ref = rest[-4:]
        g = ((a_ref[...].astype(f32) + b_ref[0].astype(f32)) + b_ref[1].astype(f32)) + b_ref[2].astype(f32)
        d, nm, nv = _adam_math(w_ref[...], g, m_ref[...], v_ref[...])
        g_ref[...] = g
        d_ref[...] = d
        nm_ref[...] = nm
        nv_ref[...] = nv

    lay = pl.BlockSpec((None, tr, C), lambda i, q_ref: (layer, i, 0))
    n_prev = 0 if prev is None else 4
    return pl.pallas_call(
        body, name=name, out_shape=[jax.ShapeDtypeStruct((depth, R, C), f32)] * 4,
        grid_spec=pltpu.PrefetchScalarGridSpec(
            num_scalar_prefetch=1, grid=(R // tr,),
            in_specs=[lay, lay, lay, pl.BlockSpec((None, tr, C), lambda i, q_ref: (q_ref[0], i, 0)),
                      pl.BlockSpec((3, tr, C), lambda i, q_ref: (0, i, 0))] + [_ANY] * n_prev,
            out_specs=[lay] * 4),
        input_output_aliases={6 + n: n for n in range(n_prev)},
        compiler_params=_cp("parallel"),
    )(q_idx, w, m, v, p4, recv, *(prev or ()))


_BIG = (("w_in", 1), ("w_br_attn", 1), ("w_br_sg", 1), ("w_out", 0), ("w_ff_gate", 1), ("w_ff_up", 1),
        ("w_ff_down", 0), ("w_ple_gate", 0), ("w_ple", 1))


def _gather_plan(i, depth):
    mixer = ["w_br_attn", "w_br_sg", "w_out"]
    plan = {
        "mm_in": [(i, "w_ff_gate")] + ([(i, n) for n in mixer] if i == 0 else []),
        "attn_fwd": [(i, "w_ff_up")],
        "mm_ffn_in": [(i, "w_ff_down"), (i, "w_ple_gate"), (i, "w_ple")],
    }
    if i + 1 < depth:
        plan["mm_ffn_out"] = [(i + 1, "w_in")]
        plan["mm_ple"] = [(i + 1, n) for n in mixer]
    return plan


def _layer_fwd(x0, p_i, layer, arrived, sm, tabs, comm):
    S, D = x0.shape
    cos, sin = tabs
    tmm = _tile(S, 1024, 8)
    same = lambda accs, ex: accs

    def W(name):
        return arrived[(layer, name)]

    def hosted(key, fn):
        if key not in comm:
            return fn(None)
        ex, keys = comm[key]
        res, outs = fn(ex)
        arrived.update(zip(keys, outs))
        return res

    h1 = _norm_fwd("norm_fwd", x0, sm["norm_mix"])
    z = hosted("mm_in", lambda ex: _mm_nn_cols("mm_in", h1, [W("w_in")], [bf16], same, comm=ex))[0]
    IN = z.shape[1]

    def grp(base):
        return jnp.stack([_perm(z[:, base + g * ATTN_W: base + (g + 1) * ATTN_W], d) for g, d in enumerate(DILATIONS)])

    qg, kg, vg = grp(0), grp(N_GROUPS * ATTN_W), grp(2 * N_GROUPS * ATTN_W)
    o3, lse3 = hosted("attn_fwd", lambda ex: _attn_fwd(qg, kg, vg, cos, sin, comm=ex))
    o3 = jnp.stack([_unperm(o3[g], d) for g, d in enumerate(DILATIONS)])
    lse3 = jnp.stack([_unperm(lse3[g], d) for g, d in enumerate(DILATIONS)])
    attn, lse = _attn_combine(o3, lse3)
    ya = _mm_nn_cols("mm_br_attn", attn, [W("w_br_attn")], [bf16], same, tm_pref=1024)[0]
    sgw = sm["sg_w"].astype(bf16)
    bb = jnp.broadcast_to(sm["sg_b"][:, :, None], (SG_GROUPS, SG_CHUNK, 128))
    sg = _sg_fwd(z, sgw, bb, sm["sg_ln_g"], sm["sg_ln_b"])
    yb = _mm_nn_cols("mm_br_sg", sg, [W("w_br_sg")], [bf16], same, tm_pref=1024)[0]
    merged = _merge_fwd(z, ya, yb, D)
    tn = _tile(D, 1024)
    x1 = _matmul("mm_out", [(merged, W("w_out").reshape(D, D), "nn", 0)], S, D, tmm, tn, 1,
                 [((S, D), f32, (tmm, tn), lambda i, j: (i, j))], lambda accs, ex: [ex[0] + accs[0]],
                 extras=[(x0, (tmm, tn), lambda i, j: (i, j))])[0]
    h2 = _norm_fwd("norm_fwd", x1, sm["norm_ffn"])

    def ffn_ep(accs, ex):
        a, b = accs
        return [a, b, a * _sigmoid(a) * b]

    a, b, f = hosted("mm_ffn_in", lambda ex: _mm_nn_cols("mm_ffn_in", h2, [W("w_ff_gate"), W("w_ff_up")], [bf16] * 3, ffn_ep,
                                                         comm=ex))
    w_down = W("w_ff_down").reshape(-1, D)
    F = w_down.shape[0]
    nk = F // _tile(F, 1408)
    x2 = hosted("mm_ffn_out", lambda ex: _matmul(
        "mm_ffn_out", [(f, w_down, "nn", 0)], S, D, tmm, tn, nk, [((S, D), f32, (tmm, tn), lambda i, j: (i, j))],
        lambda accs, ex_tiles: [ex_tiles[0] + accs[0]], extras=[(x1, (tmm, tn), lambda i, j: (i, j))], comm=ex))[0]
    h3 = _norm_fwd("norm_fwd", x2, sm["norm_ple"])

    e = _mm_nn_cols("mm_ple_emb", p_i, [W("w_ple")], [bf16], same, tm_pref=1024)[0]

    def ple_ep(accs, ex):
        gp = accs[0]
        return [ex[0] + _sigmoid(gp) * ex[1].astype(f32), gp]

    x3, gp = hosted("mm_ple", lambda ex: _matmul(
        "mm_ple", [(h3, W("w_ple_gate").reshape(D, D), "nn", 0)], S, D, tmm, tn, 1,
        [((S, D), f32, (tmm, tn), lambda i, j: (i, j)), ((S, D), bf16, (tmm, tn), lambda i, j: (i, j))],
        ple_ep, extras=[(x2, (tmm, tn), lambda i, j: (i, j)), (e, (tmm, tn), lambda i, j: (i, j))], comm=ex))
    saved = dict(x0=x0, h1=h1, z=z, qg=qg, kg=kg, vg=vg, attn=attn, lse=lse, ya=ya, yb=yb, sg=sg, merged=merged, x1=x1,
                 h2=h2, a=a, b=b, f=f, x2=x2, h3=h3, gp=gp, e=e, sgw=sgw, bb=bb, IN=IN)
    return x3, saved


def _layer_bwd(dx3, p_i, W, sm, tabs, sv, c_idx):
    S, D = dx3.shape
    w_out, w_down, w_pg = W["w_out"].reshape(D, D), W["w_ff_down"].reshape(-1, D), W["w_ple_gate"].reshape(D, D)
    F = w_down.shape[0]
    cos, sin = tabs
    tmm = _tile(S, 1024, 8)
    tn = _tile(D, 512)
    reduced = {}

    def chip_sums(grads):
        names = list(grads)
        recv = _run_exchange("rs_sibling_" + names[0], _sibling_exchange([grads[n] for n in names]))
        p4 = {n: _rs_chip_sum("rs_chip_sum_" + n, grads[n], r, c_idx) for n, r in zip(names, recv)}
        return p4, _chips_exchange([p4[n] for n in names])

    def carry(p4, outs):
        reduced.update({n: (p4[n], r) for n, r in zip(p4, outs)})

    def blocks(full):
        return full.reshape(N_DEV, full.shape[0] // N_DEV, full.shape[1])

    dgp, de = _ple_bwd_ew(dx3, sv["gp"], sv["e"])
    d_w_ple = _mm_tn_cols("mm_dw_ple", p_i, [de], D // N_DEV)[0]
    d_w_pg = blocks(_mm_simple("mm_dw_dd", sv["h3"], dgp, "tn", bf16, 1024, 1024, 2048))
    dh3 = _mm_simple("mm_dh_dd", dgp, w_pg, "nt", bf16, 1024, 1024, 2048)
    dx2, dx2b, dg_ple = _norm_bwd("norm_bwd", dh3, sv["x2"], sm["norm_ple"], dx3)
    tf = _tile(F, 1408)

    def ffn_bwd_ep(accs, ex):
        df = accs[0]
        a, b = ex[0].astype(f32), ex[1].astype(f32)
        sg = _sigmoid(a)
        return [df * b * sg * (1.0 + a * (1.0 - sg)), df * a * sg]

    th = _tile(S, 512, 8)
    da, db = _matmul("mm_dffn", [(dx2b, w_down, "nt", 0)], S, F, th, tf, 1,
                     [((S, F), bf16, (th, tf), lambda i, j: (i, j))] * 2, ffn_bwd_ep,
                     extras=[(sv["a"], (th, tf), lambda i, j: (i, j)), (sv["b"], (th, tf), lambda i, j: (i, j))])
    d_w_down = blocks(_mm_simple("mm_dw_fd", sv["f"], dx2b, "tn", bf16, 1408, 1024, 2048))
    p4, ex = chip_sums(dict(w_ff_down=d_w_down, w_ple_gate=d_w_pg, w_ple=d_w_ple))
    (d_w_gate, d_w_up), outs = _mm_tn_cols("mm_dw_df", sv["h2"], [da, db], F // N_DEV, comm=ex)
    carry(p4, outs)
    p4, ex = chip_sums(dict(w_ff_gate=d_w_gate, w_ff_up=d_w_up))
    dh2, outs = _mm_nt_cols("mm_dh_ffn", [(da, W["w_ff_gate"]), (db, W["w_ff_up"])], bf16, comm=ex)
    carry(p4, outs)
    dx1, dx1b, dg_ffn = _norm_bwd("norm_bwd", dh2, sv["x1"], sm["norm_ffn"], dx2)
    z = sv["z"]
    o_a, o_b = (QKV_W + 2 * SG_W) // tn, (QKV_W + 2 * SG_W + D) // tn

    def merge_bwd_ep(accs, ex):
        dm = accs[0]
        ga, gb = _sigmoid(ex[0].astype(f32)), _sigmoid(ex[1].astype(f32))
        ya, yb = ex[2].astype(f32), ex[3].astype(f32)
        return [dm * ya * ga * (1.0 - ga), dm * yb * gb * (1.0 - gb), dm * ga, dm * gb]

    dga, dgb, dya, dyb = _matmul(
        "mm_dmerge", [(dx1b, w_out, "nt", 0)], S, D, tmm, tn, 1,
        [((S, D), bf16, (tmm, tn), lambda i, j: (i, j))] * 4, merge_bwd_ep,
        extras=[(z, (tmm, tn), lambda i, j: (i, o_a + j)), (z, (tmm, tn), lambda i, j: (i, o_b + j)),
                (sv["ya"], (tmm, tn), lambda i, j: (i, j)), (sv["yb"], (tmm, tn), lambda i, j: (i, j))])
    d_w_out = blocks(_mm_simple("mm_dw_dd", sv["merged"], dx1b, "tn", bf16, 1024, 1024, 2048))
    dsg = _mm_nt_cols("mm_dsg", [(dyb, W["w_br_sg"])], bf16)
    d_w_bsg = _mm_tn_cols("mm_dw_bsg", sv["sg"], [dyb], D // N_DEV)[0]
    dattn = _mm_nt_cols("mm_dattn", [(dya, W["w_br_attn"])], bf16)
    d_w_battn = _mm_tn_cols("mm_dw_battn", sv["attn"], [dya], D // N_DEV)[0]
    sgwt = jnp.swapaxes(sm["sg_w"], 1, 2).astype(bf16)
    du, dv_sg, d_sgw, d_sgb, d_lg, d_lb = _sg_bwd(dsg, z, sv["sgw"], sgwt, sv["bb"], sm["sg_ln_g"], sm["sg_ln_b"])
    dl = _attn_delta(sv["attn"], dattn)

    def grp(t):
        return jnp.stack([_perm(t, d) for d in DILATIONS])

    dag, lg_, dlg_ = grp(dattn), grp(sv["lse"]), grp(dl)
    dqg = _attn_bwd_dq(sv["qg"], sv["kg"], sv["vg"], cos, sin, dag, lg_, dlg_)
    p4, ex = chip_sums(dict(w_out=d_w_out, w_br_sg=d_w_bsg, w_br_attn=d_w_battn))
    (dkg, dvg), outs = _attn_bwd_dkv(sv["qg"], sv["kg"], sv["vg"], cos, sin, dag, lg_, dlg_, comm=ex)
    carry(p4, outs)

    def ungrp(t3):
        return [_unperm(t3[g], d) for g, d in enumerate(DILATIONS)]

    dz = jnp.concatenate(ungrp(dqg) + ungrp(dkg) + ungrp(dvg) + [du, dv_sg, dga, dgb], axis=1)
    d_w_in = _mm_tn_cols("mm_dw_in", sv["h1"], [dz], sv["IN"] // N_DEV)[0]
    p4, ex = chip_sums(dict(w_in=d_w_in))
    dh1, outs = _mm_nt_cols("mm_dh_in", [(dz, W["w_in"])], bf16, comm=ex)
    carry(p4, outs)
    dx0, _, dg_mix = _norm_bwd("norm_bwd", dh1, sv["x0"], sm["norm_mix"], dx1)
    small = dict(sg_w=d_sgw, sg_b=d_sgb[:, :, 0], sg_ln_g=d_lg[0], sg_ln_b=d_lb[0], norm_mix=dg_mix[0], norm_ffn=dg_ffn[0],
                 norm_ple=dg_ple[0])
    return dx0, reduced, small


_SMALL = ("sg_w", "sg_b", "sg_ln_g", "sg_ln_b", "norm_mix", "norm_ffn", "norm_ple", "norm_final")


SMALL_ROWS = 256


def _pack_small(parts, tail):
    rows = [parts[n].astype(f32).reshape(-1, LANES) for n in _SMALL] + [tail]
    n = sum(r.shape[0] for r in rows)
    return jnp.concatenate(rows + [jnp.zeros((-n % SMALL_ROWS, LANES), f32)], axis=0)


def kernel(x, p, w_in, w_br_attn, w_br_sg, w_out, sg_w, sg_b, sg_ln_g, sg_ln_b, norm_mix, norm_ffn, norm_ple, norm_final, w_ff_gate, w_ff_up, w_ff_down, w_ple_gate, w_ple, loss_target, m_w_in, m_w_br_attn, m_w_br_sg, m_w_out, m_sg_w, m_sg_b, m_sg_ln_g, m_sg_ln_b, m_norm_mix, m_norm_ffn, m_norm_ple, m_norm_final, m_w_ff_gate, m_w_ff_up, m_w_ff_down, m_w_ple_gate, m_w_ple, v_w_in, v_w_br_attn, v_w_br_sg, v_w_out, v_sg_w, v_sg_b, v_sg_ln_g, v_sg_ln_b, v_norm_mix, v_norm_ffn, v_norm_ple, v_norm_final, v_w_ff_gate, v_w_ff_up, v_w_ff_down, v_w_ple_gate, v_w_ple):
    wts = dict(w_in=w_in, w_br_attn=w_br_attn, w_br_sg=w_br_sg, w_out=w_out, w_ff_gate=w_ff_gate, w_ff_up=w_ff_up,
               w_ff_down=w_ff_down, w_ple_gate=w_ple_gate, w_ple=w_ple)
    mom_m = dict(w_in=m_w_in, w_br_attn=m_w_br_attn, w_br_sg=m_w_br_sg, w_out=m_w_out, w_ff_gate=m_w_ff_gate,
                 w_ff_up=m_w_ff_up, w_ff_down=m_w_ff_down, w_ple_gate=m_w_ple_gate, w_ple=m_w_ple)
    mom_v = dict(w_in=v_w_in, w_br_attn=v_w_br_attn, w_br_sg=v_w_br_sg, w_out=v_w_out, w_ff_gate=v_w_ff_gate,
                 w_ff_up=v_w_ff_up, w_ff_down=v_w_ff_down, w_ple_gate=v_w_ple_gate, w_ple=v_w_ple)
    small_w = dict(sg_w=sg_w, sg_b=sg_b, sg_ln_g=sg_ln_g, sg_ln_b=sg_ln_b, norm_mix=norm_mix, norm_ffn=norm_ffn,
                   norm_ple=norm_ple, norm_final=norm_final)
    small_m = dict(sg_w=m_sg_w, sg_b=m_sg_b, sg_ln_g=m_sg_ln_g, sg_ln_b=m_sg_ln_b, norm_mix=m_norm_mix, norm_ffn=m_norm_ffn,
                   norm_ple=m_norm_ple, norm_final=m_norm_final)
    small_v = dict(sg_w=v_sg_w, sg_b=v_sg_b, sg_ln_g=v_sg_ln_g, sg_ln_b=v_sg_ln_b, norm_mix=v_norm_mix, norm_ffn=v_norm_ffn,
                   norm_ple=v_norm_ple, norm_final=v_norm_final)
    depth = w_in.shape[0]
    S = x.shape[1]
    names = [n for n, _ in _BIG]
    c_idx = lax.axis_index("c").astype(jnp.int32).reshape(1)
    q_idx = (2 * lax.axis_index("x") + lax.axis_index("y")).astype(jnp.int32).reshape(1)
    tabs = _rope_tables(S)

    def gather(keys):
        return _gather_exchange([wts[n][i].astype(bf16) for i, n in keys]), keys

    ex, keys = gather([(0, "w_in")])
    arrived = dict(zip(keys, _run_exchange("ag_w_in", ex)))

    xs = x[0]
    saved = []
    for i in range(depth):
        sm = {n: small_w[n][i] for n in _SMALL if n != "norm_final"}
        comm = {carrier: gather(keys) for carrier, keys in _gather_plan(i, depth).items()}
        xs, sv = _layer_fwd(xs, p[i, 0], i, arrived, sm, tabs, comm)
        saved.append(sv)
    dx, dg_final, loss_part = _loss_head(xs, norm_final, loss_target[0])

    reduced = [None] * depth
    small_parts = [None] * depth
    for i in reversed(range(depth)):
        sm = {n: small_w[n][i] for n in _SMALL if n != "norm_final"}
        dx, reduced[i], small_parts[i] = _layer_bwd(dx, p[i, 0], {n: arrived[(i, n)] for n in names}, sm, tabs, saved[i], c_idx)
    grad_x = dx[None]

    parts = {n: jnp.stack([small_parts[i][n] for i in range(depth)]) for n in _SMALL if n != "norm_final"}
    parts["norm_final"] = dg_final[0]
    gathered = _all_gather("ag_small", _pack_small(parts, loss_part), in_vmem=True)
    g_s, d_s, nm_s, nv_s = _small_sum_adamw(gathered, _pack_small(small_w, jnp.zeros((8, LANES), f32)),
                                            _pack_small(small_m, jnp.zeros((8, LANES), f32)),
                                            _pack_small(small_v, jnp.ones((8, LANES), f32)))
    loss = g_s[sum(small_w[n].size for n in _SMALL) // LANES, 0]

    def unpack_small(flat):
        out, off = {}, 0
        for n in _SMALL:
            k = small_w[n].size // LANES
            out[n] = flat[off:off + k].reshape(small_w[n].shape)
            off += k
        return out

    sm_g, sm_d, sm_nm, sm_nv = unpack_small(g_s), unpack_small(d_s), unpack_small(nm_s), unpack_small(nv_s)

    big_g, big_d, big_nm, big_nv = {}, {}, {}, {}
    for k, n in enumerate(names):
        outs = None
        for i in range(depth):
            p4, recv2 = reduced[i][n]
            outs = _adamw_layer(f"adamw_{n}_{i}", i, wts[n], mom_m[n], mom_v[n], p4, recv2, q_idx, outs)
        big_g[n], big_d[n], big_nm[n], big_nv[n] = outs

    order = ["w_in", "w_br_attn", "w_br_sg", "w_out", "sg_w", "sg_b", "sg_ln_g", "sg_ln_b", "norm_mix", "norm_ffn", "norm_ple",
             "norm_final", "w_ff_gate", "w_ff_up", "w_ff_down", "w_ple_gate", "w_ple"]

    def pick(big, small):
        return [big[n] if n in big else small[n] for n in order]

    return (loss, grad_x, *pick(big_g, sm_g), *pick(big_d, sm_d), *pick(big_nm, sm_nm), *pick(big_nv, sm_nv))
```

```python
import functools
import math

import jax
import jax.numpy as jnp
from jax import lax
from jax.experimental import pallas as pl
from jax.experimental.pallas import tpu as pltpu

f32 = jnp.float32
bf16 = jnp.bfloat16

HEAD_DIM = 128
N_GROUPS = 3
HEADS = 4
DILATIONS = (1, 4, 16)
RADIUS = 64
BLK = 128
QKV_W = 3 * N_GROUPS * HEADS * HEAD_DIM
ATTN_W = HEADS * HEAD_DIM
SG_CHUNK = 128
SG_GROUPS = 8
SG_W = SG_GROUPS * 128
ROPE_THETA = 10000.0
EPS = 1e-6
NEG = -1e30
N_DEV = 8
LANES = 128

ADAM_LR = 0.001
ADAM_B1 = 0.9
ADAM_B2 = 0.999
ADAM_EPS = 1e-08
ADAM_WD = 0.01
ADAM_STEP = 10

VMEM_LIMIT = 56 * 1024 * 1024
MESH = pl.DeviceIdType.MESH

NN = (((1,), (0,)), ((), ()))
NT = (((1,), (1,)), ((), ()))
TN = (((0,), (0,)), ((), ()))
_DN = {"nn": NN, "nt": NT, "tn": TN}


def _cp(*sem):
    return pltpu.CompilerParams(dimension_semantics=sem, vmem_limit_bytes=VMEM_LIMIT)


def _tile(n, pref, unit=128):
    if n <= pref:
        return n
    t = (pref // unit) * unit
    while t >= unit:
        if n % t == 0:
            return t
        t -= unit
    return n


_ANY = pl.BlockSpec(memory_space=pl.ANY)


def _call(body, *, name, grid, in_specs, out_specs, out_shape, operands, scratch=(), comm=None):
    in_specs, out_specs, out_shape, scratch = list(in_specs), list(out_specs), list(out_shape), list(scratch)
    if comm is None:
        res = pl.pallas_call(
            body, name=name, out_shape=out_shape, grid=grid, in_specs=in_specs, out_specs=out_specs, scratch_shapes=scratch,
            compiler_params=_cp(*(("arbitrary",) * len(grid))))(*operands)
        return res, []
    n_in, n_out, n_scr = len(in_specs), len(out_specs), len(scratch)
    c_in, c_out = len(comm.ins), len(comm.outs)

    def hosted(*refs):
        own_in, refs = refs[:n_in], refs[n_in:]
        ex_in, refs = refs[:c_in], refs[c_in:]
        own_out, refs = refs[:n_out], refs[n_out:]
        ex_out, refs = refs[:c_out], refs[c_out:]
        own_scr, sems = refs[:n_scr], refs[n_scr:]
        ids = [pl.program_id(a) for a in range(len(grid))]
        first = functools.reduce(jnp.logical_and, [i == 0 for i in ids])
        last = functools.reduce(jnp.logical_and, [i == g - 1 for i, g in zip(ids, grid)])

        @pl.when(first)
        def _():
            comm.start(ex_in, ex_out, sems)

        body(*own_in, *own_out, *own_scr)

        @pl.when(last)
        def _():
            comm.finish(ex_in, ex_out, sems)

    res = pl.pallas_call(
        hosted, name=name, out_shape=out_shape + list(comm.outs), grid=grid,
        in_specs=in_specs + [_ANY] * c_in, out_specs=out_specs + [_ANY] * c_out, scratch_shapes=scratch + list(comm.sems),
        compiler_params=pltpu.CompilerParams(dimension_semantics=("arbitrary",) * len(grid), vmem_limit_bytes=VMEM_LIMIT,
                                             has_side_effects=True),
    )(*operands, *comm.ins)
    return res[:n_out], res[n_out:]


def _sigmoid(x):
    return 1.0 / (1.0 + jnp.exp(-x))


_GC = math.sqrt(2.0 / math.pi)
_GA = 0.044715


def _gelu(x):
    return 0.5 * x * (1.0 + jnp.tanh(_GC * (x + _GA * x * x * x)))


def _gelu_grad(x):
    t = jnp.tanh(_GC * (x + _GA * x * x * x))
    return 0.5 * (1.0 + t) + 0.5 * x * (1.0 - t * t) * _GC * (1.0 + 3.0 * _GA * x * x)


def _matmul(name, prods, M, N, tm, tn, nk, outs, epilogue, extras=(), n_acc=1, chunk=None, comm=None):
    in_specs, operands, metas = [], [], []
    for a, b, mode, acc in prods:
        if mode == "tn":
            tk = a.shape[0] // nk
            in_specs += [pl.BlockSpec((tk, tm), lambda i, j, k: (k, i)), pl.BlockSpec((tk, tn), lambda i, j, k: (k, j))]
        elif mode == "nt":
            tk = a.shape[1] // nk
            in_specs += [pl.BlockSpec((tm, tk), lambda i, j, k: (i, k)), pl.BlockSpec((tn, tk), lambda i, j, k: (j, k))]
        else:
            tk = a.shape[1] // nk
            in_specs += [pl.BlockSpec((tm, tk), lambda i, j, k: (i, k)), pl.BlockSpec((tk, tn), lambda i, j, k: (k, j))]
        operands += [a, b]
        metas.append((mode, acc))
    for arr, bshape, imap in extras:
        in_specs.append(pl.BlockSpec(bshape, functools.partial(lambda i, j, k, f: f(i, j), f=imap)))
        operands.append(arr)
    out_specs = [pl.BlockSpec(bs, functools.partial(lambda i, j, k, f: f(i, j), f=imap)) for _, _, bs, imap in outs]
    out_shape = [jax.ShapeDtypeStruct(s, d) for s, d, _, _ in outs]
    n_prod, n_ext, n_out = len(prods), len(extras), len(outs)

    def body(*refs):
        in_refs = refs[: 2 * n_prod]
        ex_refs = refs[2 * n_prod : 2 * n_prod + n_ext]
        out_refs = refs[2 * n_prod + n_ext : 2 * n_prod + n_ext + n_out]
        acc_refs = refs[2 * n_prod + n_ext + n_out :]

        def partials():
            res = [None] * n_acc
            for idx, (mode, acc) in enumerate(metas):
                a = in_refs[2 * idx][...].astype(bf16)
                b = in_refs[2 * idx + 1][...].astype(bf16)
                d = lax.dot_general(a, b, _DN[mode], preferred_element_type=f32)
                res[acc] = d if res[acc] is None else res[acc] + d
            return res

        def finish(accs):
            vals = epilogue(accs, [r[...] for r in ex_refs])
            for r, v in zip(out_refs, vals):
                r[...] = v.astype(r.dtype)

        if nk == 1:
            step = chunk or tn
            for c0 in range(0, tn, step):
                c1 = min(c0 + step, tn)
                res = [None] * n_acc
                for idx, (mode, acc) in enumerate(metas):
                    a = in_refs[2 * idx][...].astype(bf16)
                    b_ref = in_refs[2 * idx + 1]
                    b = (b_ref[c0:c1, :] if mode == "nt" else b_ref[:, c0:c1]).astype(bf16)
                    d = lax.dot_general(a, b, _DN[mode], preferred_element_type=f32)
                    res[acc] = d if res[acc] is None else res[acc] + d
                for r, v in zip(out_refs, epilogue(res, [r[:, c0:c1] for r in ex_refs])):
                    r[:, c0:c1] = v.astype(r.dtype)
        else:
            k = pl.program_id(2)
            parts = partials()

            @pl.when(k == 0)
            def _():
                for r, d in zip(acc_refs, parts):
                    r[...] = d

            @pl.when(k > 0)
            def _():
                for r, d in zip(acc_refs, parts):
                    r[...] += d

            @pl.when(k == nk - 1)
            def _():
                finish([r[...] for r in acc_refs])

    scratch = [pltpu.VMEM((tm, tn), f32) for _ in range(n_acc)] if nk > 1 else []
    res, ex = _call(body, name=name, grid=(M // tm, N // tn, nk), in_specs=in_specs, out_specs=out_specs,
                    out_shape=out_shape, operands=operands, scratch=scratch, comm=comm)
    return res if comm is None else (res, ex)


def _ident(accs, ex):
    return [accs[0]]


def _mm_simple(name, a, b, mode, out_dtype, tm_pref=1024, tn_pref=1024, tk_pref=1024, comm=None):
    if mode == "tn":
        K, M = a.shape
        N = b.shape[1]
    elif mode == "nt":
        M, K = a.shape
        N = b.shape[0]
    else:
        M, K = a.shape
        N = b.shape[1]
    tm, tn, tk = _tile(M, tm_pref), _tile(N, tn_pref), _tile(K, tk_pref)
    res = _matmul(name, [(a, b, mode, 0)], M, N, tm, tn, K // tk,
                  [((M, N), out_dtype, (tm, tn), lambda i, j: (i, j))], _ident, comm=comm)
    return res[0] if comm is None else (res[0][0], res[1])


def _group(c, width_pref=1024):
    g = LANES // math.gcd(c, LANES)
    while g < N_DEV and 2 * g * c <= width_pref:
        g *= 2
    return g


def _join(parts):
    return parts[0] if len(parts) == 1 else jnp.concatenate(parts, axis=1)


def _mm_nn_cols(name, a, gs_list, outs_dtypes, epilogue, tm_pref=512, width_pref=1024, tr=False, comm=None):
    M, K = a.shape
    c = gs_list[0].shape[1 if tr else 2]
    g = _group(c, width_pref)
    W = g * c
    tm = _tile(M, tm_pref, 8)
    n_g, n_out = len(gs_list), len(outs_dtypes)
    blk = (g, c, K) if tr else (g, K, c)

    def body(*refs):
        a_ref = refs[0]
        g_refs = refs[1:1 + n_g]
        out_refs = refs[1 + n_g:]
        av = a_ref[...].astype(bf16)
        cols = [epilogue([lax.dot_general(av, gr[s], NT if tr else NN, preferred_element_type=f32) for gr in g_refs], [])
                for s in range(g)]
        for n, r in enumerate(out_refs):
            r[...] = _join([cols[s][n].astype(r.dtype) for s in range(g)])

    tile = pl.BlockSpec((tm, W), lambda j, i: (i, j))
    res, ex = _call(
        body, name=name, out_shape=[jax.ShapeDtypeStruct((M, N_DEV * c), d) for d in outs_dtypes],
        grid=(N_DEV // g, M // tm),
        in_specs=[pl.BlockSpec((tm, K), lambda j, i: (i, 0))] + [pl.BlockSpec((None,) + blk, lambda j, i: (j, 0, 0, 0))] * n_g,
        out_specs=[tile] * n_out, operands=[a, *[gm.reshape((N_DEV // g,) + blk) for gm in gs_list]], comm=comm)
    return res if comm is None else (res, ex)


def _mm_nt_cols(name, pairs, out_dtype, tm_pref=1024, tn_pref=1024, width_pref=1024, tr=False, comm=None):
    M = pairs[0][0].shape[0]
    c, Kw = pairs[0][1].shape[1:][::1 if tr else -1]
    g = _group(c, width_pref)
    W = g * c
    tm, tn = _tile(M, tm_pref, 8), _tile(Kw, tn_pref)
    nk = N_DEV // g
    n_p = len(pairs)

    def body(*refs):
        o_ref, acc = refs[2 * n_p], refs[2 * n_p + 1]
        k = pl.program_id(2)
        tot = None
        for n in range(n_p):
            d_ref, g_ref = refs[2 * n], refs[2 * n + 1]
            for s in range(g):
                part = lax.dot_general(d_ref[:, s * c:(s + 1) * c], g_ref[s], NN if tr else NT, preferred_element_type=f32)
                tot = part if tot is None else tot + part

        @pl.when(k == 0)
        def _():
            acc[...] = tot

        @pl.when(k > 0)
        def _():
            acc[...] += tot

        @pl.when(k == nk - 1)
        def _():
            o_ref[...] = acc[...].astype(o_ref.dtype)

    in_specs, operands = [], []
    for d, gm in pairs:
        if tr:
            wspec, wview = pl.BlockSpec((None, g, c, tn), lambda i, j, k: (k, 0, 0, j)), gm.reshape(nk, g, c, Kw)
        else:
            wspec, wview = pl.BlockSpec((None, g, tn, c), lambda i, j, k: (k, 0, j, 0)), gm.reshape(nk, g, Kw, c)
        in_specs += [pl.BlockSpec((tm, W), lambda i, j, k: (i, k)), wspec]
        operands += [d, wview]
    res, ex = _call(
        body, name=name, out_shape=[jax.ShapeDtypeStruct((M, Kw), out_dtype)], grid=(M // tm, Kw // tn, nk),
        in_specs=in_specs, out_specs=[pl.BlockSpec((tm, tn), lambda i, j, k: (i, j))],
        scratch=[pltpu.VMEM((tm, tn), f32)], operands=operands, comm=comm)
    return res[0] if comm is None else (res[0], ex)


def _mm_tn_cols(name, x, ds, c, tm_pref=1024, tk_pref=1024, width_pref=1024, tr=False, comm=None):
    S, Kw = x.shape
    g = _group(c, width_pref)
    W = g * c
    tm, tk = _tile(Kw, tm_pref), _tile(S, tk_pref, 16)
    nk = S // tk
    n_d = len(ds)
    blk = (g, c, tm) if tr else (g, tm, c)
    full = (N_DEV // g, g, c, Kw) if tr else (N_DEV // g, g, Kw, c)

    def body(*refs):
        x_ref = refs[0]
        d_refs = refs[1:1 + n_d]
        o_refs = refs[1 + n_d:1 + 2 * n_d]
        accs = refs[1 + 2 * n_d:]
        k = pl.program_id(2)

        @pl.when(k == 0)
        def _():
            for acc in accs:
                acc[...] = jnp.zeros_like(acc)

        xv = x_ref[...].astype(bf16)
        for d_ref, acc in zip(d_refs, accs):
            for s in range(g):
                ds_ = d_ref[:, s * c:(s + 1) * c]
                acc[s] += lax.dot_general(ds_, xv, TN, preferred_element_type=f32) if tr else \
                    lax.dot_general(xv, ds_, TN, preferred_element_type=f32)

        @pl.when(k == nk - 1)
        def _():
            for o_ref, acc in zip(o_refs, accs):
                o_ref[...] = acc[...].astype(o_ref.dtype)

    out_map = (lambda i, j, k: (j, 0, 0, i)) if tr else (lambda i, j, k: (j, 0, i, 0))
    outs, ex = _call(
        body, name=name, out_shape=[jax.ShapeDtypeStruct(full, bf16)] * n_d, grid=(Kw // tm, N_DEV // g, nk),
        in_specs=[pl.BlockSpec((tk, tm), lambda i, j, k: (k, i))] + [pl.BlockSpec((tk, W), lambda i, j, k: (k, j))] * n_d,
        out_specs=[pl.BlockSpec((None,) + blk, out_map)] * n_d,
        scratch=[pltpu.VMEM(blk, f32)] * n_d, operands=[x, *ds], comm=comm)
    outs = [o.reshape((N_DEV,) + full[2:]) for o in outs]
    return outs if comm is None else (outs, ex)


def _norm_fwd(name, x, g):
    S, D = x.shape
    tm = _tile(S, 512, 8)

    def body(x_ref, g_ref, h_ref):
        xv = x_ref[...]
        r = lax.rsqrt(jnp.mean(xv * xv, axis=-1, keepdims=True) + EPS)
        h_ref[...] = (xv * r * g_ref[...]).astype(bf16)

    return pl.pallas_call(
        body, name=name, out_shape=jax.ShapeDtypeStruct((S, D), bf16), grid=(S // tm,),
        in_specs=[pl.BlockSpec((tm, D), lambda i: (i, 0)), pl.BlockSpec((1, D), lambda i: (0, 0))],
        out_specs=pl.BlockSpec((tm, D), lambda i: (i, 0)), compiler_params=_cp("parallel"),
    )(x, g.reshape(1, D))


def _norm_bwd(name, dh, x, g, dx_in):
    S, D = x.shape
    tm = _tile(S, 256, 8)

    def body(dh_ref, x_ref, g_ref, dxi_ref, dx_ref, dxb_ref, dg_ref):
        i = pl.program_id(0)
        xv = x_ref[...]
        r = lax.rsqrt(jnp.mean(xv * xv, axis=-1, keepdims=True) + EPS)
        xh = xv * r
        dhv = dh_ref[...].astype(f32)
        dxh = dhv * g_ref[...]
        dx = dxi_ref[...] + r * (dxh - xh * jnp.mean(dxh * xh, axis=-1, keepdims=True))
        dx_ref[...] = dx
        dxb_ref[...] = dx.astype(bf16)

        @pl.when(i == 0)
        def _():
            dg_ref[...] = jnp.zeros_like(dg_ref)

        dg_ref[...] += jnp.sum(dhv * xh, axis=0, keepdims=True)

    row = pl.BlockSpec((tm, D), lambda i: (i, 0))
    vec = pl.BlockSpec((1, D), lambda i: (0, 0))
    return pl.pallas_call(
        body, name=name,
        out_shape=[jax.ShapeDtypeStruct((S, D), f32), jax.ShapeDtypeStruct((S, D), bf16), jax.ShapeDtypeStruct((1, D), f32)],
        grid=(S // tm,), in_specs=[row, row, vec, row], out_specs=[row, row, vec], compiler_params=_cp("arbitrary"),
    )(dh, x, g.reshape(1, D), dx_in)


def _loss_head(x, g, t):
    S, D = x.shape
    tm = _tile(S, 256, 8)

    def body(x_ref, g_ref, t_ref, dx_ref, dg_ref, loss_ref):
        i = pl.program_id(0)
        xv = x_ref[...]
        r = lax.rsqrt(jnp.mean(xv * xv, axis=-1, keepdims=True) + EPS)
        xh = xv * r
        gv = g_ref[...]
        err = xh * gv - t_ref[...]
        dy = err * (1.0 / D)
        dxh = dy * gv
        dx_ref[...] = r * (dxh - xh * jnp.mean(dxh * xh, axis=-1, keepdims=True))

        @pl.when(i == 0)
        def _():
            dg_ref[...] = jnp.zeros_like(dg_ref)
            loss_ref[...] = jnp.zeros_like(loss_ref)

        dg_ref[...] += jnp.sum(dy * xh, axis=0, keepdims=True)
        row = jnp.sum(err * err, axis=-1, keepdims=True) * (0.5 / D)
        loss_ref[...] += jnp.broadcast_to(jnp.sum(row, axis=0, keepdims=True), loss_ref.shape)

    return pl.pallas_call(
        body, name="loss_head",
        out_shape=[jax.ShapeDtypeStruct((S, D), f32), jax.ShapeDtypeStruct((1, D), f32), jax.ShapeDtypeStruct((8, LANES), f32)],
        grid=(S // tm,),
        in_specs=[pl.BlockSpec((tm, D), lambda i: (i, 0)), pl.BlockSpec((1, D), lambda i: (0, 0)), pl.BlockSpec((tm, D), lambda i: (i, 0))],
        out_specs=[pl.BlockSpec((tm, D), lambda i: (i, 0)), pl.BlockSpec((1, D), lambda i: (0, 0)), pl.BlockSpec((8, LANES), lambda i: (0, 0))],
        compiler_params=_cp("arbitrary"),
    )(x, g.reshape(1, D), t)


def _perm(t, d):
    if d == 1:
        return t
    S, C = t.shape
    return t.reshape(S // d, d, C).transpose(1, 0, 2).reshape(S, C)


def _unperm(t, d):
    if d == 1:
        return t
    S, C = t.shape
    return t.reshape(d, S // d, C).transpose(1, 0, 2).reshape(S, C)


def _rope_tables(S):
    half = HEAD_DIM // 2
    pos = jnp.arange(S, dtype=f32)
    inv_freq = ROPE_THETA ** (-jnp.arange(0, HEAD_DIM, 2, dtype=f32) / HEAD_DIM)
    ang = pos[:, None] * inv_freq[None, :]
    c, s = jnp.cos(ang), jnp.sin(ang)
    cos2 = jnp.concatenate([c, c], axis=-1)
    sin2 = jnp.concatenate([-s, s], axis=-1)
    assert cos2.shape == (S, 2 * half)
    return (jnp.stack([_perm(cos2, d) for d in DILATIONS]), jnp.stack([_perm(sin2, d) for d in DILATIONS]))


def _rope(t, c, s):
    return t * c + pltpu.roll(t, HEAD_DIM // 2, 1) * s


def _rope_bwd(dt, c, s):
    return dt * c - pltpu.roll(dt, HEAD_DIM // 2, 1) * s


def _band_bounds(i, nblk):
    g = pl.program_id(0)
    lb = jnp.right_shift(jnp.int32(nblk), 2 * g)
    pos = lax.rem(i, lb)
    lo = jnp.where(pos == 0, BLK, 0)
    hi = jnp.where(pos == lb - 1, 2 * BLK, 3 * BLK)
    return lo, hi


def _band_specs(width, nblk):
    prev = pl.BlockSpec((None, BLK, width), lambda g, i: (g, jnp.maximum(i - 1, 0), 0))
    cur = pl.BlockSpec((None, BLK, width), lambda g, i: (g, i, 0))
    nxt = pl.BlockSpec((None, BLK, width), lambda g, i: (g, jnp.minimum(i + 1, nblk - 1), 0))
    return [prev, cur, nxt]


_SCALE = HEAD_DIM ** -0.5


def _attn_fwd(q, k, v, cos, sin, comm=None):
    _, S, W = q.shape
    nblk = S // BLK

    def body(q_ref, kp, kc, kn, vp, vc, vn, cq, sq, ckp, ckc, ckn, skp, skc, skn, o_ref, lse_ref):
        i = pl.program_id(1)
        lo, hi = _band_bounds(i, nblk)
        a = lax.broadcasted_iota(jnp.int32, (BLK, 3 * BLK), 0)
        b = lax.broadcasted_iota(jnp.int32, (BLK, 3 * BLK), 1)
        mask = (jnp.abs(b - BLK - a) <= RADIUS) & (b >= lo) & (b < hi)
        ck = jnp.concatenate([ckp[...], ckc[...], ckn[...]], axis=0)
        sk = jnp.concatenate([skp[...], skc[...], skn[...]], axis=0)
        for hh in range(HEADS):
            sl = slice(hh * HEAD_DIM, (hh + 1) * HEAD_DIM)
            qh = _rope(q_ref[:, sl].astype(f32), cq[...], sq[...]).astype(bf16)
            kh = jnp.concatenate([kp[:, sl], kc[:, sl], kn[:, sl]], axis=0).astype(f32)
            kh = _rope(kh, ck, sk).astype(bf16)
            vh = jnp.concatenate([vp[:, sl], vc[:, sl], vn[:, sl]], axis=0)
            s = lax.dot_general(qh, kh, NT, preferred_element_type=f32) * _SCALE
            s = jnp.where(mask, s, NEG)
            m = jnp.max(s, axis=-1, keepdims=True)
            e = jnp.exp(s - m)
            den = jnp.sum(e, axis=-1, keepdims=True)
            o = lax.dot_general(e.astype(bf16), vh, NN, preferred_element_type=f32) * (1.0 / den)
            o_ref[:, sl] = o.astype(bf16)
            lse_ref[:, sl] = jnp.broadcast_to(m + jnp.log(den), (BLK, HEAD_DIM))

    blk = pl.BlockSpec((None, BLK, W), lambda g, i: (g, i, 0))
    tab = pl.BlockSpec((None, BLK, HEAD_DIM), lambda g, i: (g, i, 0))
    res, ex = _call(
        body, name="attn_fwd",
        out_shape=[jax.ShapeDtypeStruct((N_GROUPS, S, W), bf16), jax.ShapeDtypeStruct((N_GROUPS, S, W), f32)],
        grid=(N_GROUPS, nblk),
        in_specs=[blk] + _band_specs(W, nblk) * 2 + [tab, tab] + _band_specs(HEAD_DIM, nblk) * 2,
        out_specs=[blk, blk], operands=[q, k, k, k, v, v, v, cos, sin, cos, cos, cos, sin, sin, sin], comm=comm)
    return res if comm is None else (res, ex)


def _attn_combine(o3, lse3):
    _, S, W = o3.shape
    tm = _tile(S, 512, 8)

    def body(o_ref, l_ref, attn_ref, lse_ref):
        l0, l1, l2 = l_ref[0], l_ref[1], l_ref[2]
        m = jnp.maximum(jnp.maximum(l0, l1), l2)
        w0, w1, w2 = jnp.exp(l0 - m), jnp.exp(l1 - m), jnp.exp(l2 - m)
        den = w0 + w1 + w2
        acc = w0 * o_ref[0].astype(f32) + w1 * o_ref[1].astype(f32) + w2 * o_ref[2].astype(f32)
        attn_ref[...] = (acc * (1.0 / den)).astype(bf16)
        lse_ref[...] = m + jnp.log(den)

    blk3 = pl.BlockSpec((N_GROUPS, tm, W), lambda i: (0, i, 0))
    blk = pl.BlockSpec((tm, W), lambda i: (i, 0))
    return pl.pallas_call(
        body, name="attn_combine", out_shape=[jax.ShapeDtypeStruct((S, W), bf16), jax.ShapeDtypeStruct((S, W), f32)],
        grid=(S // tm,), in_specs=[blk3, blk3], out_specs=[blk, blk], compiler_params=_cp("parallel"),
    )(o3, lse3)


def _attn_delta(attn, dattn):
    S, W = attn.shape
    tm = _tile(S, 512, 8)

    def body(a_ref, d_ref, o_ref):
        prod = a_ref[...].astype(f32) * d_ref[...].astype(f32)
        for hh in range(HEADS):
            sl = slice(hh * HEAD_DIM, (hh + 1) * HEAD_DIM)
            o_ref[:, sl] = jnp.broadcast_to(jnp.sum(prod[:, sl], axis=-1, keepdims=True), (tm, HEAD_DIM))

    blk = pl.BlockSpec((tm, W), lambda i: (i, 0))
    return pl.pallas_call(
        body, name="attn_delta", out_shape=jax.ShapeDtypeStruct((S, W), f32), grid=(S // tm,),
        in_specs=[blk, blk], out_specs=blk, compiler_params=_cp("parallel"),
    )(attn, dattn)


def _attn_bwd_dq(q, k, v, cos, sin, da, lse, dl):
    _, S, W = q.shape
    nblk = S // BLK

    def body(q_ref, kp, kc, kn, vp, vc, vn, cq, sq, ckp, ckc, ckn, skp, skc, skn, da_ref, l_ref, dl_ref, dq_ref):
        i = pl.program_id(1)
        lo, hi = _band_bounds(i, nblk)
        a = lax.broadcasted_iota(jnp.int32, (BLK, 3 * BLK), 0)
        b = lax.broadcasted_iota(jnp.int32, (BLK, 3 * BLK), 1)
        mask = (jnp.abs(b - BLK - a) <= RADIUS) & (b >= lo) & (b < hi)
        ck = jnp.concatenate([ckp[...], ckc[...], ckn[...]], axis=0)
        sk = jnp.concatenate([skp[...], skc[...], skn[...]], axis=0)
        for hh in range(HEADS):
            sl = slice(hh * HEAD_DIM, (hh + 1) * HEAD_DIM)
            qh = _rope(q_ref[:, sl].astype(f32), cq[...], sq[...]).astype(bf16)
            kh = jnp.concatenate([kp[:, sl], kc[:, sl], kn[:, sl]], axis=0).astype(f32)
            kh = _rope(kh, ck, sk).astype(bf16)
            vh = jnp.concatenate([vp[:, sl], vc[:, sl], vn[:, sl]], axis=0)
            s = lax.dot_general(qh, kh, NT, preferred_element_type=f32) * _SCALE
            lh = l_ref[:, sl]
            l3 = jnp.concatenate([lh, lh, lh], axis=1)
            p = jnp.exp(jnp.where(mask, s - l3, NEG))
            dp = lax.dot_general(da_ref[:, sl], vh, NT, preferred_element_type=f32)
            dh = dl_ref[:, sl]
            ds = p * (dp - jnp.concatenate([dh, dh, dh], axis=1))
            dqh = lax.dot_general(ds.astype(bf16), kh, NN, preferred_element_type=f32) * _SCALE
            dq_ref[:, sl] = _rope_bwd(dqh, cq[...], sq[...]).astype(bf16)

    blk = pl.BlockSpec((None, BLK, W), lambda g, i: (g, i, 0))
    tab = pl.BlockSpec((None, BLK, HEAD_DIM), lambda g, i: (g, i, 0))
    return pl.pallas_call(
        body, name="attn_bwd_dq", out_shape=jax.ShapeDtypeStruct((N_GROUPS, S, W), bf16), grid=(N_GROUPS, nblk),
        in_specs=[blk] + _band_specs(W, nblk) * 2 + [tab, tab] + _band_specs(HEAD_DIM, nblk) * 2 + [blk, blk, blk],
        out_specs=blk, compiler_params=_cp("parallel", "parallel"),
    )(q, k, k, k, v, v, v, cos, sin, cos, cos, cos, sin, sin, sin, da, lse, dl)


def _attn_bwd_dkv(q, k, v, cos, sin, da, lse, dl, comm=None):
    _, S, W = q.shape
    nblk = S // BLK

    def body(k_ref, v_ref, ck, sk, qp, qc, qn, cqp, cqc, cqn, sqp, sqc, sqn, dap, dac, dan, lp, lc, ln, dlp, dlc, dln,
             dk_ref, dv_ref):
        j = pl.program_id(1)
        lo, hi = _band_bounds(j, nblk)
        a = lax.broadcasted_iota(jnp.int32, (3 * BLK, BLK), 0)
        b = lax.broadcasted_iota(jnp.int32, (3 * BLK, BLK), 1)
        mask = (jnp.abs(b - (a - BLK)) <= RADIUS) & (a >= lo) & (a < hi)
        cq = jnp.concatenate([cqp[...], cqc[...], cqn[...]], axis=0)
        sq = jnp.concatenate([sqp[...], sqc[...], sqn[...]], axis=0)
        for hh in range(HEADS):
            sl = slice(hh * HEAD_DIM, (hh + 1) * HEAD_DIM)
            kh = _rope(k_ref[:, sl].astype(f32), ck[...], sk[...]).astype(bf16)
            vh = v_ref[:, sl]
            qh = jnp.concatenate([qp[:, sl], qc[:, sl], qn[:, sl]], axis=0).astype(f32)
            qh = _rope(qh, cq, sq).astype(bf16)
            dah = jnp.concatenate([dap[:, sl], dac[:, sl], dan[:, sl]], axis=0)
            lh = jnp.concatenate([lp[:, sl], lc[:, sl], ln[:, sl]], axis=0)
            dlh = jnp.concatenate([dlp[:, sl], dlc[:, sl], dln[:, sl]], axis=0)
            s = lax.dot_general(qh, kh, NT, preferred_element_type=f32) * _SCALE
            p = jnp.exp(jnp.where(mask, s - lh, NEG))
            dv_ref[:, sl] = lax.dot_general(p.astype(bf16), dah, TN, preferred_element_type=f32).astype(bf16)
            dp = lax.dot_general(dah, vh, NT, preferred_element_type=f32)
            ds = p * (dp - dlh)
            dkh = lax.dot_general(ds.astype(bf16), qh, TN, preferred_element_type=f32) * _SCALE
            dk_ref[:, sl] = _rope_bwd(dkh, ck[...], sk[...]).astype(bf16)

    blk = pl.BlockSpec((None, BLK, W), lambda g, i: (g, i, 0))
    tab = pl.BlockSpec((None, BLK, HEAD_DIM), lambda g, i: (g, i, 0))
    bw, bt = _band_specs(W, nblk), _band_specs(HEAD_DIM, nblk)
    res, ex = _call(
        body, name="attn_bwd_dkv",
        out_shape=[jax.ShapeDtypeStruct((N_GROUPS, S, W), bf16), jax.ShapeDtypeStruct((N_GROUPS, S, W), bf16)],
        grid=(N_GROUPS, nblk),
        in_specs=[blk, blk, tab, tab] + bw + bt + bt + bw + bw + bw, out_specs=[blk, blk],
        operands=[k, v, cos, sin, q, q, q, cos, cos, cos, sin, sin, sin, da, da, da, lse, lse, lse, dl, dl, dl], comm=comm)
    return res if comm is None else (res, ex)


_SG_ROWS = 512


def _sg_z_specs(tm, half):
    o = QKV_W // half
    return [pl.BlockSpec((tm, half), functools.partial(lambda i, c: (i, c), c=o + n)) for n in range(4)]


def _sg_norm(v, lg, lb):
    gv = _gelu(v)
    mu = jnp.mean(gv, axis=-1, keepdims=True)
    xc = gv - mu
    rstd = lax.rsqrt(jnp.mean(xc * xc, axis=-1, keepdims=True) + EPS)
    xh = xc * rstd
    return xh, rstd, xh * lg + lb


def _sg_fwd(z, w, bb, lg, lb):
    S = z.shape[0]
    tm = _tile(S, _SG_ROWS, SG_CHUNK)
    half = SG_W // 2

    def body(u0, u1, v0, v1, w_ref, bb_ref, lg_ref, lb_ref, o_ref):
        u = jnp.concatenate([u0[...], u1[...]], axis=1).astype(f32)
        v = jnp.concatenate([v0[...], v1[...]], axis=1).astype(f32)
        gu = _gelu(u)
        _, _, vn = _sg_norm(v, lg_ref[...], lb_ref[...])
        vnb = vn.astype(bf16)
        for c in range(tm // SG_CHUNK):
            rs = slice(c * SG_CHUNK, (c + 1) * SG_CHUNK)
            for g in range(SG_GROUPS):
                cs = slice(g * 128, (g + 1) * 128)
                mixed = lax.dot_general(w_ref[g], vnb[rs, cs], NN, preferred_element_type=f32) + bb_ref[g]
                o_ref[rs, cs] = (gu[rs, cs] * mixed).astype(bf16)

    full3 = pl.BlockSpec((SG_GROUPS, 128, 128), lambda i: (0, 0, 0))
    vec = pl.BlockSpec((1, SG_W), lambda i: (0, 0))
    return pl.pallas_call(
        body, name="sg_fwd", out_shape=jax.ShapeDtypeStruct((S, SG_W), bf16), grid=(S // tm,),
        in_specs=_sg_z_specs(tm, half) + [full3, full3, vec, vec],
        out_specs=pl.BlockSpec((tm, SG_W), lambda i: (i, 0)), compiler_params=_cp("parallel"),
    )(z, z, z, z, w, bb, lg.reshape(1, SG_W), lb.reshape(1, SG_W))


def _sg_bwd(dsg, z, w, wt, bb, lg, lb):
    S = z.shape[0]
    tm = _tile(S, _SG_ROWS, SG_CHUNK)
    half = SG_W // 2

    def body(d_ref, u0, u1, v0, v1, w_ref, wt_ref, bb_ref, lg_ref, lb_ref, du_ref, dv_ref, dw_ref, db_ref, dlg_ref, dlb_ref, dvn_scr):
        i = pl.program_id(0)

        @pl.when(i == 0)
        def _():
            dw_ref[...] = jnp.zeros_like(dw_ref)
            db_ref[...] = jnp.zeros_like(db_ref)
            dlg_ref[...] = jnp.zeros_like(dlg_ref)
            dlb_ref[...] = jnp.zeros_like(dlb_ref)

        u = jnp.concatenate([u0[...], u1[...]], axis=1).astype(f32)
        v = jnp.concatenate([v0[...], v1[...]], axis=1).astype(f32)
        gu = _gelu(u)
        dgu = _gelu_grad(u)
        xh, rstd, vn = _sg_norm(v, lg_ref[...], lb_ref[...])
        vnb = vn.astype(bf16)
        dsg_v = d_ref[...].astype(f32)
        for g in range(SG_GROUPS):
            cs = slice(g * 128, (g + 1) * 128)
            dw_g = jnp.zeros((128, 128), f32)
            db_g = jnp.zeros((128, 1), f32)
            for c in range(tm // SG_CHUNK):
                rs = slice(c * SG_CHUNK, (c + 1) * SG_CHUNK)
                ds = dsg_v[rs, cs]
                mixed = lax.dot_general(w_ref[g], vnb[rs, cs], NN, preferred_element_type=f32) + bb_ref[g]
                du_ref[rs, cs] = (ds * mixed * dgu[rs, cs]).astype(bf16)
                dmix = ds * gu[rs, cs]
                dmb = dmix.astype(bf16)
                dw_g = dw_g + lax.dot_general(dmb, vnb[rs, cs], NT, preferred_element_type=f32)
                db_g = db_g + jnp.sum(dmix, axis=-1, keepdims=True)
                dvn_scr[rs, cs] = lax.dot_general(wt_ref[g], dmb, NN, preferred_element_type=f32)
            dw_ref[g] += dw_g
            db_ref[g] += jnp.broadcast_to(db_g, (128, 128))
        dvn = dvn_scr[...]
        dlg_ref[...] += jnp.sum(dvn * xh, axis=0, keepdims=True)
        dlb_ref[...] += jnp.sum(dvn, axis=0, keepdims=True)
        dxh = dvn * lg_ref[...]
        dgv = rstd * (dxh - jnp.mean(dxh, axis=-1, keepdims=True) - xh * jnp.mean(dxh * xh, axis=-1, keepdims=True))
        dv_ref[...] = (dgv * _gelu_grad(v)).astype(bf16)

    full3 = pl.BlockSpec((SG_GROUPS, 128, 128), lambda i: (0, 0, 0))
    vec = pl.BlockSpec((1, SG_W), lambda i: (0, 0))
    row = pl.BlockSpec((tm, SG_W), lambda i: (i, 0))
    return pl.pallas_call(
        body, name="sg_bwd",
        out_shape=[jax.ShapeDtypeStruct((S, SG_W), bf16), jax.ShapeDtypeStruct((S, SG_W), bf16),
                   jax.ShapeDtypeStruct((SG_GROUPS, 128, 128), f32), jax.ShapeDtypeStruct((SG_GROUPS, 128, 128), f32),
                   jax.ShapeDtypeStruct((1, SG_W), f32), jax.ShapeDtypeStruct((1, SG_W), f32)],
        grid=(S // tm,),
        in_specs=[row] + _sg_z_specs(tm, half) + [full3, full3, full3, vec, vec],
        out_specs=[row, row, full3, full3, vec, vec],
        scratch_shapes=[pltpu.VMEM((tm, SG_W), f32)], compiler_params=_cp("arbitrary"),
    )(dsg, z, z, z, z, w, wt, bb, lg.reshape(1, SG_W), lb.reshape(1, SG_W))


def _merge_fwd(z, ya, yb, D):
    S = z.shape[0]
    tm, tc = _tile(S, 512, 8), _tile(D, 512)
    o_a, o_b = (QKV_W + 2 * SG_W) // tc, (QKV_W + 2 * SG_W + D) // tc

    def body(ga_ref, gb_ref, ya_ref, yb_ref, o_ref):
        ga = _sigmoid(ga_ref[...].astype(f32))
        gb = _sigmoid(gb_ref[...].astype(f32))
        o_ref[...] = (ga * ya_ref[...].astype(f32) + gb * yb_ref[...].astype(f32)).astype(bf16)

    blk = pl.BlockSpec((tm, tc), lambda i, j: (i, j))
    return pl.pallas_call(
        body, name="merge_fwd", out_shape=jax.ShapeDtypeStruct((S, D), bf16), grid=(S // tm, D // tc),
        in_specs=[pl.BlockSpec((tm, tc), lambda i, j: (i, o_a + j)), pl.BlockSpec((tm, tc), lambda i, j: (i, o_b + j)), blk, blk],
        out_specs=blk, compiler_params=_cp("parallel", "parallel"),
    )(z, z, ya, yb)


def _ple_bwd_ew(dx, gp, e):
    S, D = dx.shape
    tm, tc = _tile(S, 512, 8), _tile(D, 1024)

    def body(dx_ref, gp_ref, e_ref, dgp_ref, de_ref):
        dxv = dx_ref[...]
        sg = _sigmoid(gp_ref[...].astype(f32))
        dgp_ref[...] = (dxv * e_ref[...].astype(f32) * sg * (1.0 - sg)).astype(bf16)
        de_ref[...] = (dxv * sg).astype(bf16)

    blk = pl.BlockSpec((tm, tc), lambda i, j: (i, j))
    return pl.pallas_call(
        body, name="ple_bwd_ew", out_shape=[jax.ShapeDtypeStruct((S, D), bf16)] * 2, grid=(S // tm, D // tc),
        in_specs=[blk, blk, blk], out_specs=[blk, blk], compiler_params=_cp("parallel", "parallel"),
    )(dx, gp, e)


def _adam_math(w, g, m, v):
    m = ADAM_B1 * m + (1.0 - ADAM_B1) * g
    v = ADAM_B2 * v + (1.0 - ADAM_B2) * (g * g)
    m_hat = m / (1.0 - ADAM_B1 ** ADAM_STEP)
    v_hat = v / (1.0 - ADAM_B2 ** ADAM_STEP)
    delta = -ADAM_LR * (m_hat / (jnp.sqrt(v_hat) + ADAM_EPS) + ADAM_WD * w)
    return delta, m, v


def _small_sum_adamw(gathered, w, m, v):
    _, R, _ = gathered.shape
    tr = _tile(R, 1024, SMALL_ROWS)

    def body(p_ref, w_ref, m_ref, v_ref, g_ref, d_ref, nm_ref, nv_ref):
        g = p_ref[0]
        for n in range(1, N_DEV):
            g = g + p_ref[n]
        d, nm, nv = _adam_math(w_ref[...], g, m_ref[...], v_ref[...])
        g_ref[...] = g
        d_ref[...] = d
        nm_ref[...] = nm
        nv_ref[...] = nv

    blk = pl.BlockSpec((tr, LANES), lambda i: (i, 0))
    return pl.pallas_call(
        body, name="small_sum_adamw", out_shape=[jax.ShapeDtypeStruct((R, LANES), f32)] * 4, grid=(R // tr,),
        in_specs=[pl.BlockSpec((N_DEV, tr, LANES), lambda i: (0, i, 0)), blk, blk, blk], out_specs=[blk] * 4,
        compiler_params=_cp("parallel"),
    )(gathered, w, m, v)


def _all_gather(name, shard, in_vmem=False):
    R, C = shard.shape

    def body(x_ref, out_ref, send_sems, recv_sems, local_sem):
        x, y, c = lax.axis_index("x"), lax.axis_index("y"), lax.axis_index("c")
        me, sibling = (x, y, c), (x, y, 1 - c)
        chips = [(1 - x, y), (x, 1 - y), (1 - x, 1 - y)]

        def rows(px, py, pc):
            return out_ref.at[4 * px + 2 * py + pc]

        def copy(k, block, to, src=None):
            return pltpu.make_async_remote_copy(
                src_ref=rows(*block) if src is None else src, dst_ref=rows(*block),
                send_sem=send_sems.at[k], recv_sem=recv_sems.at[k], device_id=to, device_id_type=MESH)

        mine = pltpu.make_async_copy(x_ref, rows(*me), local_sem)
        mine.start()
        first = [copy(0, me, sibling, src=x_ref)]
        first += [copy(1 + j, me, (*chip, c), src=x_ref) for j, chip in enumerate(chips)]
        for cp in first:
            cp.start()
        passed = [copy(4 + j, (*chip, c), sibling) for j, chip in enumerate(chips)]
        for j, chip in enumerate(chips):
            copy(1 + j, (*chip, c), me).wait_recv()
            passed[j].start()
        copy(0, sibling, me).wait_recv()
        for j, chip in enumerate(chips):
            copy(4 + j, (*chip, 1 - c), me).wait_recv()
        for cp in first + passed:
            cp.wait_send()
        mine.wait()

    space = pl.BlockSpec(memory_space=pltpu.VMEM) if in_vmem else _ANY
    return pl.pallas_call(
        body, name=name, out_shape=jax.ShapeDtypeStruct((N_DEV, R, C), shard.dtype),
        in_specs=[space], out_specs=space,
        scratch_shapes=[pltpu.SemaphoreType.DMA((7,)), pltpu.SemaphoreType.DMA((7,)), pltpu.SemaphoreType.DMA],
        compiler_params=pltpu.CompilerParams(has_side_effects=True, vmem_limit_bytes=VMEM_LIMIT),
    )(shard)


class _Exchange:
    def __init__(self, ins, outs, sems, start, finish):
        self.ins, self.outs, self.sems, self.start, self.finish = list(ins), list(outs), list(sems), start, finish


def _run_exchange(name, ex):
    c_in, c_out = len(ex.ins), len(ex.outs)

    def body(*refs):
        ins, outs, sems = refs[:c_in], refs[c_in:c_in + c_out], refs[c_in + c_out:]
        ex.start(ins, outs, sems)
        ex.finish(ins, outs, sems)

    return pl.pallas_call(
        body, name=name, out_shape=ex.outs, in_specs=[_ANY] * c_in, out_specs=[_ANY] * c_out, scratch_shapes=ex.sems,
        compiler_params=pltpu.CompilerParams(has_side_effects=True, vmem_limit_bytes=VMEM_LIMIT),
    )(*ex.ins)


def _gather_exchange(shards):
    n = len(shards)

    def plan(ins, outs, sems):
        send_sems, recv_sems, local_sems = sems
        x, y, c = lax.axis_index("x"), lax.axis_index("y"), lax.axis_index("c")
        me, sibling = (x, y, c), (x, y, 1 - c)
        chips = [(1 - x, y), (x, 1 - y), (1 - x, 1 - y)]

        def rows(w, px, py, pc):
            return outs[w].at[4 * px + 2 * py + pc]

        def copy(w, k, block, to, src=None):
            return pltpu.make_async_remote_copy(
                src_ref=rows(w, *block) if src is None else src, dst_ref=rows(w, *block),
                send_sem=send_sems.at[w, k], recv_sem=recv_sems.at[w, k], device_id=to, device_id_type=MESH)

        mine = [pltpu.make_async_copy(ins[w], rows(w, *me), local_sems.at[w]) for w in range(n)]
        first = []
        for w in range(n):
            first.append(copy(w, 0, me, sibling, src=ins[w]))
            first += [copy(w, 1 + j, me, (*chip, c), src=ins[w]) for j, chip in enumerate(chips)]
        return c, me, sibling, chips, copy, mine, first

    def start(ins, outs, sems):
        _, _, _, _, _, mine, first = plan(ins, outs, sems)
        for cp in mine + first:
            cp.start()

    def finish(ins, outs, sems):
        c, me, sibling, chips, copy, mine, first = plan(ins, outs, sems)
        passed = []
        for w in range(n):
            for j, chip in enumerate(chips):
                copy(w, 1 + j, (*chip, c), me).wait_recv()
                passed.append(copy(w, 4 + j, (*chip, c), sibling))
                passed[-1].start()
        for w in range(n):
            copy(w, 0, sibling, me).wait_recv()
            for j, chip in enumerate(chips):
                copy(w, 4 + j, (*chip, 1 - c), me).wait_recv()
        for cp in first + passed:
            cp.wait_send()
        for cp in mine:
            cp.wait()

    return _Exchange(
        shards, [jax.ShapeDtypeStruct((N_DEV,) + s.shape, s.dtype) for s in shards],
        [pltpu.SemaphoreType.DMA((n, 7)), pltpu.SemaphoreType.DMA((n, 7)), pltpu.SemaphoreType.DMA((n,))], start, finish)


def _sibling_exchange(gs):
    n = len(gs)

    def copies(ins, outs, sems):
        send_sems, recv_sems = sems
        x, y, c = lax.axis_index("x"), lax.axis_index("y"), lax.axis_index("c")
        return [pltpu.make_async_remote_copy(
            src_ref=ins[w].at[2 * q + (1 - c)], dst_ref=outs[w].at[q], send_sem=send_sems.at[w, q],
            recv_sem=recv_sems.at[w, q], device_id=(x, y, 1 - c), device_id_type=MESH) for w in range(n) for q in range(4)]

    def start(ins, outs, sems):
        for cp in copies(ins, outs, sems):
            cp.start()

    def finish(ins, outs, sems):
        cps = copies(ins, outs, sems)
        for cp in cps:
            cp.wait_recv()
        for cp in cps:
            cp.wait_send()

    return _Exchange(gs, [jax.ShapeDtypeStruct((4,) + g.shape[1:], g.dtype) for g in gs],
                     [pltpu.SemaphoreType.DMA((n, 4)), pltpu.SemaphoreType.DMA((n, 4))], start, finish)


def _rs_chip_sum(name, g8, recv, c_idx):
    _, R, C = g8.shape
    tr = _tile(R, 512, 16)
    g42 = g8.reshape(4, 2, R, C)

    def body(c_ref, a_ref, b_ref, o_ref):
        o_ref[...] = (a_ref[...].astype(f32) + b_ref[...].astype(f32)).astype(o_ref.dtype)

    return pl.pallas_call(
        body, name=name, out_shape=jax.ShapeDtypeStruct((4, R, C), g8.dtype),
        grid_spec=pltpu.PrefetchScalarGridSpec(
            num_scalar_prefetch=1, grid=(4, R // tr),
            in_specs=[pl.BlockSpec((None, None, tr, C), lambda q, r, c_ref: (q, c_ref[0], r, 0)),
                      pl.BlockSpec((None, tr, C), lambda q, r, c_ref: (q, r, 0))],
            out_specs=pl.BlockSpec((None, tr, C), lambda q, r, c_ref: (q, r, 0))),
        compiler_params=_cp("parallel", "parallel"),
    )(c_idx, g42, recv)


def _chips_exchange(p4s):
    n = len(p4s)

    def copies(ins, outs, sems):
        send_sems, recv_sems = sems
        x, y, c = lax.axis_index("x"), lax.axis_index("y"), lax.axis_index("c")
        chips = [(1 - x, y), (x, 1 - y), (1 - x, 1 - y)]
        return [pltpu.make_async_remote_copy(
            src_ref=ins[w].at[2 * cx + cy], dst_ref=outs[w].at[k], send_sem=send_sems.at[w, k],
            recv_sem=recv_sems.at[w, k], device_id=(cx, cy, c), device_id_type=MESH)
            for w in range(n) for k, (cx, cy) in enumerate(chips)]

    def start(ins, outs, sems):
        for cp in copies(ins, outs, sems):
            cp.start()

    def finish(ins, outs, sems):
        cps = copies(ins, outs, sems)
        for cp in cps:
            cp.wait_recv()
        for cp in cps:
            cp.wait_send()

    return _Exchange(p4s, [jax.ShapeDtypeStruct((3,) + p.shape[1:], p.dtype) for p in p4s],
                     [pltpu.SemaphoreType.DMA((n, 3)), pltpu.SemaphoreType.DMA((n, 3))], start, finish)


def _adamw_layer(name, layer, w, m, v, p4, recv, q_idx, prev):
    depth, R, C = w.shape
    tr = _tile(R, 256, 8)

    def body(q_ref, w_ref, m_ref, v_ref, a_ref, b_ref, *rest):
        g_ref, d_ref, nm_ref, nv_ref = rest[-4:]
        g = ((a_ref[...].astype(f32) + b_ref[0].astype(f32)) + b_ref[1].astype(f32)) + b_ref[2].astype(f32)
        d, nm, nv = _adam_math(w_ref[...], g, m_ref[...], v_ref[...])
        g_ref[...] = g
        d_ref[...] = d
        nm_ref[...] = nm
        nv_ref[...] = nv

    lay = pl.BlockSpec((None, tr, C), lambda i, q_ref: (layer, i, 0))
    n_prev = 0 if prev is None else 4
    return pl.pallas_call(
        body, name=name, out_shape=[jax.ShapeDtypeStruct((depth, R, C), f32)] * 4,
        grid_spec=pltpu.PrefetchScalarGridSpec(
            num_scalar_prefetch=1, grid=(R // tr,),
            in_specs=[lay, lay, lay, pl.BlockSpec((None, tr, C), lambda i, q_ref: (q_ref[0], i, 0)),
                      pl.BlockSpec((3, tr, C), lambda i, q_ref: (0, i, 0))] + [_ANY] * n_prev,
            out_specs=[lay] * 4),
        input_output_aliases={6 + n: n for n in range(n_prev)},
        compiler_params=_cp("parallel"),
    )(q_idx, w, m, v, p4, recv, *(prev or ()))


_BIG = (("w_in", 1), ("w_br_attn", 1), ("w_br_sg", 1), ("w_out", 0), ("w_ff_gate", 1), ("w_ff_up", 1),
        ("w_ff_down", 0), ("w_ple_gate", 0), ("w_ple", 1))


_TURNED = ("w_in", "w_ff_gate", "w_ff_up")


def _gather_plan(i, depth):
    mixer = ["w_br_attn", "w_br_sg", "w_out"]
    plan = {
        "mm_in": [(i, "w_ff_gate")] + ([(i, n) for n in mixer] if i == 0 else []),
        "attn_fwd": [(i, "w_ff_up")],
        "mm_ffn_in": [(i, "w_ff_down"), (i, "w_ple_gate"), (i, "w_ple")],
    }
    if i + 1 < depth:
        plan["mm_ffn_out"] = [(i + 1, "w_in")]
        plan["mm_ple"] = [(i + 1, n) for n in mixer]
    return plan


def _layer_fwd(x0, p_i, layer, arrived, sm, tabs, comm):
    S, D = x0.shape
    cos, sin = tabs
    tmm = _tile(S, 1024, 8)
    same = lambda accs, ex: accs

    def W(name):
        return arrived[(layer, name)]

    def hosted(key, fn):
        if key not in comm:
            return fn(None)
        ex, keys = comm[key]
        res, outs = fn(ex)
        arrived.update(zip(keys, outs))
        return res

    h1 = _norm_fwd("norm_fwd", x0, sm["norm_mix"])
    z = hosted("mm_in", lambda ex: _mm_nn_cols("mm_in", h1, [W("w_in")], [bf16], same, tr=True, comm=ex))[0]
    IN = z.shape[1]

    def grp(base):
        return jnp.stack([_perm(z[:, base + g * ATTN_W: base + (g + 1) * ATTN_W], d) for g, d in enumerate(DILATIONS)])

    qg, kg, vg = grp(0), grp(N_GROUPS * ATTN_W), grp(2 * N_GROUPS * ATTN_W)
    o3, lse3 = hosted("attn_fwd", lambda ex: _attn_fwd(qg, kg, vg, cos, sin, comm=ex))
    o3 = jnp.stack([_unperm(o3[g], d) for g, d in enumerate(DILATIONS)])
    lse3 = jnp.stack([_unperm(lse3[g], d) for g, d in enumerate(DILATIONS)])
    attn, lse = _attn_combine(o3, lse3)
    ya = _mm_nn_cols("mm_br_attn", attn, [W("w_br_attn")], [bf16], same, tm_pref=1024)[0]
    sgw = sm["sg_w"].astype(bf16)
    bb = jnp.broadcast_to(sm["sg_b"][:, :, None], (SG_GROUPS, SG_CHUNK, 128))
    sg = _sg_fwd(z, sgw, bb, sm["sg_ln_g"], sm["sg_ln_b"])
    yb = _mm_nn_cols("mm_br_sg", sg, [W("w_br_sg")], [bf16], same, tm_pref=1024)[0]
    merged = _merge_fwd(z, ya, yb, D)
    tn = _tile(D, 1024)
    x1 = _matmul("mm_out", [(merged, W("w_out").reshape(D, D), "nn", 0)], S, D, tmm, tn, 1,
                 [((S, D), f32, (tmm, tn), lambda i, j: (i, j))], lambda accs, ex: [ex[0] + accs[0]],
                 extras=[(x0, (tmm, tn), lambda i, j: (i, j))], chunk=512)[0]
    h2 = _norm_fwd("norm_fwd", x1, sm["norm_ffn"])

    def ffn_ep(accs, ex):
        a, b = accs
        return [a, b, a * _sigmoid(a) * b]

    a, b, f = hosted("mm_ffn_in", lambda ex: _mm_nn_cols("mm_ffn_in", h2, [W("w_ff_gate"), W("w_ff_up")], [bf16] * 3, ffn_ep,
                                                         tr=True, comm=ex))
    w_down = W("w_ff_down").reshape(-1, D)
    F = w_down.shape[0]
    nk = F // _tile(F, 1408)
    x2 = hosted("mm_ffn_out", lambda ex: _matmul(
        "mm_ffn_out", [(f, w_down, "nn", 0)], S, D, tmm, tn, nk, [((S, D), f32, (tmm, tn), lambda i, j: (i, j))],
        lambda accs, ex_tiles: [ex_tiles[0] + accs[0]], extras=[(x1, (tmm, tn), lambda i, j: (i, j))], comm=ex))[0]
    h3 = _norm_fwd("norm_fwd", x2, sm["norm_ple"])

    e = _mm_nn_cols("mm_ple_emb", p_i, [W("w_ple")], [bf16], same, tm_pref=1024)[0]

    def ple_ep(accs, ex):
        gp = accs[0]
        return [ex[0] + _sigmoid(gp) * ex[1].astype(f32), gp]

    x3, gp = hosted("mm_ple", lambda ex: _matmul(
        "mm_ple", [(h3, W("w_ple_gate").reshape(D, D), "nn", 0)], S, D, tmm, tn, 1,
        [((S, D), f32, (tmm, tn), lambda i, j: (i, j)), ((S, D), bf16, (tmm, tn), lambda i, j: (i, j))],
        ple_ep, extras=[(x2, (tmm, tn), lambda i, j: (i, j)), (e, (tmm, tn), lambda i, j: (i, j))], chunk=512, comm=ex))
    saved = dict(x0=x0, h1=h1, z=z, qg=qg, kg=kg, vg=vg, attn=attn, lse=lse, ya=ya, yb=yb, sg=sg, merged=merged, x1=x1,
                 h2=h2, a=a, b=b, f=f, x2=x2, h3=h3, gp=gp, e=e, sgw=sgw, bb=bb, IN=IN)
    return x3, saved


def _layer_bwd(dx3, p_i, W, sm, tabs, sv, c_idx):
    S, D = dx3.shape
    w_out, w_down, w_pg = W["w_out"].reshape(D, D), W["w_ff_down"].reshape(-1, D), W["w_ple_gate"].reshape(D, D)
    F = w_down.shape[0]
    cos, sin = tabs
    tmm = _tile(S, 1024, 8)
    tn = _tile(D, 512)
    reduced = {}

    def chip_sums(grads):
        names = list(grads)
        recv = _run_exchange("rs_sibling_" + names[0], _sibling_exchange([grads[n] for n in names]))
        p4 = {n: _rs_chip_sum("rs_chip_sum_" + n, grads[n], r, c_idx) for n, r in zip(names, recv)}
        return p4, _chips_exchange([p4[n] for n in names])

    def carry(p4, outs):
        reduced.update({n: (p4[n], r) for n, r in zip(p4, outs)})

    def blocks(full):
        return full.reshape(N_DEV, full.shape[0] // N_DEV, full.shape[1])

    dgp, de = _ple_bwd_ew(dx3, sv["gp"], sv["e"])
    d_w_ple = _mm_tn_cols("mm_dw_ple", p_i, [de], D // N_DEV)[0]
    d_w_pg = blocks(_mm_simple("mm_dw_dd", sv["h3"], dgp, "tn", bf16, 1024, 1024, 2048))
    dh3 = _mm_simple("mm_dh_dd", dgp, w_pg, "nt", bf16, 1024, 1024, 2048)
    dx2, dx2b, dg_ple = _norm_bwd("norm_bwd", dh3, sv["x2"], sm["norm_ple"], dx3)
    tf = _tile(F, 1408)

    def ffn_bwd_ep(accs, ex):
        df = accs[0]
        a, b = ex[0].astype(f32), ex[1].astype(f32)
        sg = _sigmoid(a)
        return [df * b * sg * (1.0 + a * (1.0 - sg)), df * a * sg]

    th = _tile(S, 512, 8)
    da, db = _matmul("mm_dffn", [(dx2b, w_down, "nt", 0)], S, F, th, tf, 1,
                     [((S, F), bf16, (th, tf), lambda i, j: (i, j))] * 2, ffn_bwd_ep,
                     extras=[(sv["a"], (th, tf), lambda i, j: (i, j)), (sv["b"], (th, tf), lambda i, j: (i, j))], chunk=512)
    d_w_down = blocks(_mm_simple("mm_dw_fd", sv["f"], dx2b, "tn", bf16, 1408, 1024, 2048))
    p4, ex = chip_sums(dict(w_ff_down=d_w_down, w_ple_gate=d_w_pg, w_ple=d_w_ple))
    (d_w_gate, d_w_up), outs = _mm_tn_cols("mm_dw_df", sv["h2"], [da, db], F // N_DEV, tr=True, comm=ex)
    carry(p4, outs)
    p4, ex = chip_sums(dict(w_ff_gate=d_w_gate, w_ff_up=d_w_up))
    dh2, outs = _mm_nt_cols("mm_dh_ffn", [(da, W["w_ff_gate"]), (db, W["w_ff_up"])], bf16, tr=True, comm=ex)
    carry(p4, outs)
    dx1, dx1b, dg_ffn = _norm_bwd("norm_bwd", dh2, sv["x1"], sm["norm_ffn"], dx2)
    z = sv["z"]
    o_a, o_b = (QKV_W + 2 * SG_W) // tn, (QKV_W + 2 * SG_W + D) // tn

    def merge_bwd_ep(accs, ex):
        dm = accs[0]
        ga, gb = _sigmoid(ex[0].astype(f32)), _sigmoid(ex[1].astype(f32))
        ya, yb = ex[2].astype(f32), ex[3].astype(f32)
        return [dm * ya * ga * (1.0 - ga), dm * yb * gb * (1.0 - gb), dm * ga, dm * gb]

    dga, dgb, dya, dyb = _matmul(
        "mm_dmerge", [(dx1b, w_out, "nt", 0)], S, D, tmm, tn, 1,
        [((S, D), bf16, (tmm, tn), lambda i, j: (i, j))] * 4, merge_bwd_ep,
        extras=[(z, (tmm, tn), lambda i, j: (i, o_a + j)), (z, (tmm, tn), lambda i, j: (i, o_b + j)),
                (sv["ya"], (tmm, tn), lambda i, j: (i, j)), (sv["yb"], (tmm, tn), lambda i, j: (i, j))], chunk=256)
    d_w_out = blocks(_mm_simple("mm_dw_dd", sv["merged"], dx1b, "tn", bf16, 1024, 1024, 2048))
    dsg = _mm_nt_cols("mm_dsg", [(dyb, W["w_br_sg"])], bf16)
    d_w_bsg = _mm_tn_cols("mm_dw_bsg", sv["sg"], [dyb], D // N_DEV)[0]
    dattn = _mm_nt_cols("mm_dattn", [(dya, W["w_br_attn"])], bf16)
    d_w_battn = _mm_tn_cols("mm_dw_battn", sv["attn"], [dya], D // N_DEV)[0]
    sgwt = jnp.swapaxes(sm["sg_w"], 1, 2).astype(bf16)
    du, dv_sg, d_sgw, d_sgb, d_lg, d_lb = _sg_bwd(dsg, z, sv["sgw"], sgwt, sv["bb"], sm["sg_ln_g"], sm["sg_ln_b"])
    dl = _attn_delta(sv["attn"], dattn)

    def grp(t):
        return jnp.stack([_perm(t, d) for d in DILATIONS])

    dag, lg_, dlg_ = grp(dattn), grp(sv["lse"]), grp(dl)
    dqg = _attn_bwd_dq(sv["qg"], sv["kg"], sv["vg"], cos, sin, dag, lg_, dlg_)
    p4, ex = chip_sums(dict(w_out=d_w_out, w_br_sg=d_w_bsg, w_br_attn=d_w_battn))
    (dkg, dvg), outs = _attn_bwd_dkv(sv["qg"], sv["kg"], sv["vg"], cos, sin, dag, lg_, dlg_, comm=ex)
    carry(p4, outs)

    def ungrp(t3):
        return [_unperm(t3[g], d) for g, d in enumerate(DILATIONS)]

    dz = jnp.concatenate(ungrp(dqg) + ungrp(dkg) + ungrp(dvg) + [du, dv_sg, dga, dgb], axis=1)
    d_w_in = _mm_tn_cols("mm_dw_in", sv["h1"], [dz], sv["IN"] // N_DEV, tr=True)[0]
    p4, ex = chip_sums(dict(w_in=d_w_in))
    dh1, outs = _mm_nt_cols("mm_dh_in", [(dz, W["w_in"])], bf16, tr=True, comm=ex)
    carry(p4, outs)
    dx0, _, dg_mix = _norm_bwd("norm_bwd", dh1, sv["x0"], sm["norm_mix"], dx1)
    small = dict(sg_w=d_sgw, sg_b=d_sgb[:, :, 0], sg_ln_g=d_lg[0], sg_ln_b=d_lb[0], norm_mix=dg_mix[0], norm_ffn=dg_ffn[0],
                 norm_ple=dg_ple[0])
    return dx0, reduced, small


_SMALL = ("sg_w", "sg_b", "sg_ln_g", "sg_ln_b", "norm_mix", "norm_ffn", "norm_ple", "norm_final")


SMALL_ROWS = 256


def _pack_small(parts, tail):
    rows = [parts[n].astype(f32).reshape(-1, LANES) for n in _SMALL] + [tail]
    n = sum(r.shape[0] for r in rows)
    return jnp.concatenate(rows + [jnp.zeros((-n % SMALL_ROWS, LANES), f32)], axis=0)


def kernel(x, p, w_in, w_br_attn, w_br_sg, w_out, sg_w, sg_b, sg_ln_g, sg_ln_b, norm_mix, norm_ffn, norm_ple, norm_final, w_ff_gate, w_ff_up, w_ff_down, w_ple_gate, w_ple, loss_target, m_w_in, m_w_br_attn, m_w_br_sg, m_w_out, m_sg_w, m_sg_b, m_sg_ln_g, m_sg_ln_b, m_norm_mix, m_norm_ffn, m_norm_ple, m_norm_final, m_w_ff_gate, m_w_ff_up, m_w_ff_down, m_w_ple_gate, m_w_ple, v_w_in, v_w_br_attn, v_w_br_sg, v_w_out, v_sg_w, v_sg_b, v_sg_ln_g, v_sg_ln_b, v_norm_mix, v_norm_ffn, v_norm_ple, v_norm_final, v_w_ff_gate, v_w_ff_up, v_w_ff_down, v_w_ple_gate, v_w_ple):
    wts = dict(w_in=w_in, w_br_attn=w_br_attn, w_br_sg=w_br_sg, w_out=w_out, w_ff_gate=w_ff_gate, w_ff_up=w_ff_up,
               w_ff_down=w_ff_down, w_ple_gate=w_ple_gate, w_ple=w_ple)
    mom_m = dict(w_in=m_w_in, w_br_attn=m_w_br_attn, w_br_sg=m_w_br_sg, w_out=m_w_out, w_ff_gate=m_w_ff_gate,
                 w_ff_up=m_w_ff_up, w_ff_down=m_w_ff_down, w_ple_gate=m_w_ple_gate, w_ple=m_w_ple)
    mom_v = dict(w_in=v_w_in, w_br_attn=v_w_br_attn, w_br_sg=v_w_br_sg, w_out=v_w_out, w_ff_gate=v_w_ff_gate,
                 w_ff_up=v_w_ff_up, w_ff_down=v_w_ff_down, w_ple_gate=v_w_ple_gate, w_ple=v_w_ple)
    small_w = dict(sg_w=sg_w, sg_b=sg_b, sg_ln_g=sg_ln_g, sg_ln_b=sg_ln_b, norm_mix=norm_mix, norm_ffn=norm_ffn,
                   norm_ple=norm_ple, norm_final=norm_final)
    small_m = dict(sg_w=m_sg_w, sg_b=m_sg_b, sg_ln_g=m_sg_ln_g, sg_ln_b=m_sg_ln_b, norm_mix=m_norm_mix, norm_ffn=m_norm_ffn,
                   norm_ple=m_norm_ple, norm_final=m_norm_final)
    small_v = dict(sg_w=v_sg_w, sg_b=v_sg_b, sg_ln_g=v_sg_ln_g, sg_ln_b=v_sg_ln_b, norm_mix=v_norm_mix, norm_ffn=v_norm_ffn,
                   norm_ple=v_norm_ple, norm_final=v_norm_final)
    depth = w_in.shape[0]
    S = x.shape[1]
    names = [n for n, _ in _BIG]
    c_idx = lax.axis_index("c").astype(jnp.int32).reshape(1)
    q_idx = (2 * lax.axis_index("x") + lax.axis_index("y")).astype(jnp.int32).reshape(1)
    tabs = _rope_tables(S)

    def turned(n, t):
        return jnp.swapaxes(t, -1, -2) if n in _TURNED else t

    def gather(keys):
        return _gather_exchange([turned(n, wts[n][i]).astype(bf16) for i, n in keys]), keys

    ex, keys = gather([(0, "w_in")])
    arrived = dict(zip(keys, _run_exchange("ag_w_in", ex)))

    xs = x[0]
    saved = []
    for i in range(depth):
        sm = {n: small_w[n][i] for n in _SMALL if n != "norm_final"}
        comm = {carrier: gather(keys) for carrier, keys in _gather_plan(i, depth).items()}
        xs, sv = _layer_fwd(xs, p[i, 0], i, arrived, sm, tabs, comm)
        saved.append(sv)
    dx, dg_final, loss_part = _loss_head(xs, norm_final, loss_target[0])

    reduced = [None] * depth
    small_parts = [None] * depth
    for i in reversed(range(depth)):
        sm = {n: small_w[n][i] for n in _SMALL if n != "norm_final"}
        dx, reduced[i], small_parts[i] = _layer_bwd(dx, p[i, 0], {n: arrived[(i, n)] for n in names}, sm, tabs, saved[i], c_idx)
    grad_x = dx[None]

    parts = {n: jnp.stack([small_parts[i][n] for i in range(depth)]) for n in _SMALL if n != "norm_final"}
    parts["norm_final"] = dg_final[0]
    gathered = _all_gather("ag_small", _pack_small(parts, loss_part), in_vmem=True)
    g_s, d_s, nm_s, nv_s = _small_sum_adamw(gathered, _pack_small(small_w, jnp.zeros((8, LANES), f32)),
                                            _pack_small(small_m, jnp.zeros((8, LANES), f32)),
                                            _pack_small(small_v, jnp.ones((8, LANES), f32)))
    loss = g_s[sum(small_w[n].size for n in _SMALL) // LANES, 0]

    def unpack_small(flat):
        out, off = {}, 0
        for n in _SMALL:
            k = small_w[n].size // LANES
            out[n] = flat[off:off + k].reshape(small_w[n].shape)
            off += k
        return out

    sm_g, sm_d, sm_nm, sm_nv = unpack_small(g_s), unpack_small(d_s), unpack_small(nm_s), unpack_small(nv_s)

    big_g, big_d, big_nm, big_nv = {}, {}, {}, {}
    for k, n in enumerate(names):
        outs = None
        for i in range(depth):
            p4, recv2 = reduced[i][n]
            outs = _adamw_layer(f"adamw_{n}_{i}", i, turned(n, wts[n]), turned(n, mom_m[n]), turned(n, mom_v[n]), p4, recv2,
                                q_idx, outs)
        big_g[n], big_d[n], big_nm[n], big_nv[n] = [turned(n, o) for o in outs]

    order = ["w_in", "w_br_attn", "w_br_sg", "w_out", "sg_w", "sg_b", "sg_ln_g", "sg_ln_b", "norm_mix", "norm_ffn", "norm_ple",
             "norm_final", "w_ff_gate", "w_ff_up", "w_ff_down", "w_ple_gate", "w_ple"]

    def pick(big, small):
        return [big[n] if n in big else small[n] for n in order]

    return (loss, grad_x, *pick(big_g, sm_g), *pick(big_d, sm_d), *pick(big_nm, sm_nm), *pick(big_nv, sm_nv))
```

```python
import functools
import math

import jax
import jax.numpy as jnp
from jax import lax
from jax.experimental import pallas as pl
from jax.experimental.pallas import tpu as pltpu

f32 = jnp.float32
bf16 = jnp.bfloat16

HEAD_DIM = 128
N_GROUPS = 3
HEADS = 4
DILATIONS = (1, 4, 16)
RADIUS = 64
BLK = 128
QKV_W = 3 * N_GROUPS * HEADS * HEAD_DIM
ATTN_W = HEADS * HEAD_DIM
SG_CHUNK = 128
SG_GROUPS = 8
SG_W = SG_GROUPS * 128
ROPE_THETA = 10000.0
EPS = 1e-6
NEG = -1e30
N_DEV = 8
LANES = 128

ADAM_LR = 0.001
ADAM_B1 = 0.9
ADAM_B2 = 0.999
ADAM_EPS = 1e-08
ADAM_WD = 0.01
ADAM_STEP = 10

VMEM_LIMIT = 56 * 1024 * 1024
MESH = pl.DeviceIdType.MESH

NN = (((1,), (0,)), ((), ()))
NT = (((1,), (1,)), ((), ()))
TN = (((0,), (0,)), ((), ()))
_DN = {"nn": NN, "nt": NT, "tn": TN}


def _cp(*sem):
    return pltpu.CompilerParams(dimension_semantics=sem, vmem_limit_bytes=VMEM_LIMIT)


def _tile(n, pref, unit=128):
    if n <= pref:
        return n
    t = (pref // unit) * unit
    while t >= unit:
        if n % t == 0:
            return t
        t -= unit
    return n


_ANY = pl.BlockSpec(memory_space=pl.ANY)


def _call(body, *, name, grid, in_specs, out_specs, out_shape, operands, scratch=(), comm=None):
    in_specs, out_specs, out_shape, scratch = list(in_specs), list(out_specs), list(out_shape), list(scratch)
    if comm is None:
        res = pl.pallas_call(
            body, name=name, out_shape=out_shape, grid=grid, in_specs=in_specs, out_specs=out_specs, scratch_shapes=scratch,
            compiler_params=_cp(*(("arbitrary",) * len(grid))))(*operands)
        return res, []
    n_in, n_out, n_scr = len(in_specs), len(out_specs), len(scratch)
    c_in, c_out = len(comm.ins), len(comm.outs)

    def hosted(*refs):
        own_in, refs = refs[:n_in], refs[n_in:]
        ex_in, refs = refs[:c_in], refs[c_in:]
        own_out, refs = refs[:n_out], refs[n_out:]
        ex_out, refs = refs[:c_out], refs[c_out:]
        own_scr, sems = refs[:n_scr], refs[n_scr:]
        ids = [pl.program_id(a) for a in range(len(grid))]
        first = functools.reduce(jnp.logical_and, [i == 0 for i in ids])
        last = functools.reduce(jnp.logical_and, [i == g - 1 for i, g in zip(ids, grid)])

        @pl.when(first)
        def _():
            comm.start(ex_in, ex_out, sems)

        body(*own_in, *own_out, *own_scr)

        @pl.when(last)
        def _():
            comm.finish(ex_in, ex_out, sems)

    res = pl.pallas_call(
        hosted, name=name, out_shape=out_shape + list(comm.outs), grid=grid,
        in_specs=in_specs + [_ANY] * c_in, out_specs=out_specs + [_ANY] * c_out, scratch_shapes=scratch + list(comm.sems),
        compiler_params=pltpu.CompilerParams(dimension_semantics=("arbitrary",) * len(grid), vmem_limit_bytes=VMEM_LIMIT,
                                             has_side_effects=True),
    )(*operands, *comm.ins)
    return res[:n_out], res[n_out:]


def _sigmoid(x):
    return 1.0 / (1.0 + jnp.exp(-x))


_GC = math.sqrt(2.0 / math.pi)
_GA = 0.044715


def _gelu(x):
    return 0.5 * x * (1.0 + jnp.tanh(_GC * (x + _GA * x * x * x)))


def _gelu_grad(x):
    t = jnp.tanh(_GC * (x + _GA * x * x * x))
    return 0.5 * (1.0 + t) + 0.5 * x * (1.0 - t * t) * _GC * (1.0 + 3.0 * _GA * x * x)


def _matmul(name, prods, M, N, tm, tn, nk, outs, epilogue, extras=(), n_acc=1, chunk=None, comm=None):
    in_specs, operands, metas = [], [], []
    for a, b, mode, acc in prods:
        if mode == "tn":
            tk = a.shape[0] // nk
            in_specs += [pl.BlockSpec((tk, tm), lambda i, j, k: (k, i)), pl.BlockSpec((tk, tn), lambda i, j, k: (k, j))]
        elif mode == "nt":
            tk = a.shape[1] // nk
            in_specs += [pl.BlockSpec((tm, tk), lambda i, j, k: (i, k)), pl.BlockSpec((tn, tk), lambda i, j, k: (j, k))]
        else:
            tk = a.shape[1] // nk
            in_specs += [pl.BlockSpec((tm, tk), lambda i, j, k: (i, k)), pl.BlockSpec((tk, tn), lambda i, j, k: (k, j))]
        operands += [a, b]
        metas.append((mode, acc))
    for arr, bshape, imap in extras:
        in_specs.append(pl.BlockSpec(bshape, functools.partial(lambda i, j, k, f: f(i, j), f=imap)))
        operands.append(arr)
    out_specs = [pl.BlockSpec(bs, functools.partial(lambda i, j, k, f: f(i, j), f=imap)) for _, _, bs, imap in outs]
    out_shape = [jax.ShapeDtypeStruct(s, d) for s, d, _, _ in outs]
    n_prod, n_ext, n_out = len(prods), len(extras), len(outs)

    def body(*refs):
        in_refs = refs[: 2 * n_prod]
        ex_refs = refs[2 * n_prod : 2 * n_prod + n_ext]
        out_refs = refs[2 * n_prod + n_ext : 2 * n_prod + n_ext + n_out]
        acc_refs = refs[2 * n_prod + n_ext + n_out :]

        def partials():
            res = [None] * n_acc
            for idx, (mode, acc) in enumerate(metas):
                a = in_refs[2 * idx][...].astype(bf16)
                b = in_refs[2 * idx + 1][...].astype(bf16)
                d = lax.dot_general(a, b, _DN[mode], preferred_element_type=f32)
                res[acc] = d if res[acc] is None else res[acc] + d
            return res

        def finish(accs):
            vals = epilogue(accs, [r[...] for r in ex_refs])
            for r, v in zip(out_refs, vals):
                r[...] = v.astype(r.dtype)

        if nk == 1:
            step = chunk or tn
            for c0 in range(0, tn, step):
                c1 = min(c0 + step, tn)
                res = [None] * n_acc
                for idx, (mode, acc) in enumerate(metas):
                    a = in_refs[2 * idx][...].astype(bf16)
                    b_ref = in_refs[2 * idx + 1]
                    b = (b_ref[c0:c1, :] if mode == "nt" else b_ref[:, c0:c1]).astype(bf16)
                    d = lax.dot_general(a, b, _DN[mode], preferred_element_type=f32)
                    res[acc] = d if res[acc] is None else res[acc] + d
                for r, v in zip(out_refs, epilogue(res, [r[:, c0:c1] for r in ex_refs])):
                    r[:, c0:c1] = v.astype(r.dtype)
        else:
            k = pl.program_id(2)
            parts = partials()

            @pl.when(k == 0)
            def _():
                for r, d in zip(acc_refs, parts):
                    r[...] = d

            @pl.when(k > 0)
            def _():
                for r, d in zip(acc_refs, parts):
                    r[...] += d

            @pl.when(k == nk - 1)
            def _():
                finish([r[...] for r in acc_refs])

    scratch = [pltpu.VMEM((tm, tn), f32) for _ in range(n_acc)] if nk > 1 else []
    res, ex = _call(body, name=name, grid=(M // tm, N // tn, nk), in_specs=in_specs, out_specs=out_specs,
                    out_shape=out_shape, operands=operands, scratch=scratch, comm=comm)
    return res if comm is None else (res, ex)


def _ident(accs, ex):
    return [accs[0]]


def _mm_simple(name, a, b, mode, out_dtype, tm_pref=1024, tn_pref=1024, tk_pref=1024, comm=None):
    if mode == "tn":
        K, M = a.shape
        N = b.shape[1]
    elif mode == "nt":
        M, K = a.shape
        N = b.shape[0]
    else:
        M, K = a.shape
        N = b.shape[1]
    tm, tn, tk = _tile(M, tm_pref), _tile(N, tn_pref), _tile(K, tk_pref)
    res = _matmul(name, [(a, b, mode, 0)], M, N, tm, tn, K // tk,
                  [((M, N), out_dtype, (tm, tn), lambda i, j: (i, j))], _ident, comm=comm)
    return res[0] if comm is None else (res[0][0], res[1])


def _group(c, width_pref=1024):
    g = LANES // math.gcd(c, LANES)
    while g < N_DEV and 2 * g * c <= width_pref:
        g *= 2
    return g


def _join(parts):
    return parts[0] if len(parts) == 1 else jnp.concatenate(parts, axis=1)


def _mm_nn_cols(name, a, gs_list, outs_dtypes, epilogue, tm_pref=512, width_pref=1024, tr=False, comm=None):
    M, K = a.shape
    c = gs_list[0].shape[1 if tr else 2]
    g = _group(c, width_pref)
    W = g * c
    tm = _tile(M, tm_pref, 8)
    n_g, n_out = len(gs_list), len(outs_dtypes)
    blk = (g, c, K) if tr else (g, K, c)

    def body(*refs):
        a_ref = refs[0]
        g_refs = refs[1:1 + n_g]
        out_refs = refs[1 + n_g:]
        av = a_ref[...].astype(bf16)
        cols = [epilogue([lax.dot_general(av, gr[s], NT if tr else NN, preferred_element_type=f32) for gr in g_refs], [])
                for s in range(g)]
        for n, r in enumerate(out_refs):
            r[...] = _join([cols[s][n].astype(r.dtype) for s in range(g)])

    tile = pl.BlockSpec((tm, W), lambda j, i: (i, j))
    res, ex = _call(
        body, name=name, out_shape=[jax.ShapeDtypeStruct((M, N_DEV * c), d) for d in outs_dtypes],
        grid=(N_DEV // g, M // tm),
        in_specs=[pl.BlockSpec((tm, K), lambda j, i: (i, 0))] + [pl.BlockSpec((None,) + blk, lambda j, i: (j, 0, 0, 0))] * n_g,
        out_specs=[tile] * n_out, operands=[a, *[gm.reshape((N_DEV // g,) + blk) for gm in gs_list]], comm=comm)
    return res if comm is None else (res, ex)


def _mm_nt_cols(name, pairs, out_dtype, tm_pref=1024, tn_pref=1024, width_pref=1024, tr=False, comm=None):
    M = pairs[0][0].shape[0]
    c, Kw = pairs[0][1].shape[1:][::1 if tr else -1]
    g = _group(c, width_pref)
    W = g * c
    tm, tn = _tile(M, tm_pref, 8), _tile(Kw, tn_pref)
    nk = N_DEV // g
    n_p = len(pairs)

    def body(*refs):
        o_ref, acc = refs[2 * n_p], refs[2 * n_p + 1]
        k = pl.program_id(2)
        tot = None
        for n in range(n_p):
            d_ref, g_ref = refs[2 * n], refs[2 * n + 1]
            for s in range(g):
                part = lax.dot_general(d_ref[:, s * c:(s + 1) * c], g_ref[s], NN if tr else NT, preferred_element_type=f32)
                tot = part if tot is None else tot + part

        @pl.when(k == 0)
        def _():
            acc[...] = tot

        @pl.when(k > 0)
        def _():
            acc[...] += tot

        @pl.when(k == nk - 1)
        def _():
            o_ref[...] = acc[...].astype(o_ref.dtype)

    in_specs, operands = [], []
    for d, gm in pairs:
        if tr:
            wspec, wview = pl.BlockSpec((None, g, c, tn), lambda i, j, k: (k, 0, 0, j)), gm.reshape(nk, g, c, Kw)
        else:
            wspec, wview = pl.BlockSpec((None, g, tn, c), lambda i, j, k: (k, 0, j, 0)), gm.reshape(nk, g, Kw, c)
        in_specs += [pl.BlockSpec((tm, W), lambda i, j, k: (i, k)), wspec]
        operands += [d, wview]
    res, ex = _call(
        body, name=name, out_shape=[jax.ShapeDtypeStruct((M, Kw), out_dtype)], grid=(M // tm, Kw // tn, nk),
        in_specs=in_specs, out_specs=[pl.BlockSpec((tm, tn), lambda i, j, k: (i, j))],
        scratch=[pltpu.VMEM((tm, tn), f32)], operands=operands, comm=comm)
    return res[0] if comm is None else (res[0], ex)


def _mm_tn_cols(name, x, ds, c, tm_pref=1024, tk_pref=1024, width_pref=1024, tr=False, comm=None):
    S, Kw = x.shape
    g = _group(c, width_pref)
    W = g * c
    tm, tk = _tile(Kw, tm_pref), _tile(S, tk_pref, 16)
    nk = S // tk
    n_d = len(ds)
    blk = (g, c, tm) if tr else (g, tm, c)
    full = (N_DEV // g, g, c, Kw) if tr else (N_DEV // g, g, Kw, c)

    def body(*refs):
        x_ref = refs[0]
        d_refs = refs[1:1 + n_d]
        o_refs = refs[1 + n_d:1 + 2 * n_d]
        accs = refs[1 + 2 * n_d:]
        k = pl.program_id(2)

        @pl.when(k == 0)
        def _():
            for acc in accs:
                acc[...] = jnp.zeros_like(acc)

        xv = x_ref[...].astype(bf16)
        for d_ref, acc in zip(d_refs, accs):
            for s in range(g):
                ds_ = d_ref[:, s * c:(s + 1) * c]
                acc[s] += lax.dot_general(ds_, xv, TN, preferred_element_type=f32) if tr else \
                    lax.dot_general(xv, ds_, TN, preferred_element_type=f32)

        @pl.when(k == nk - 1)
        def _():
            for o_ref, acc in zip(o_refs, accs):
                o_ref[...] = acc[...].astype(o_ref.dtype)

    out_map = (lambda i, j, k: (j, 0, 0, i)) if tr else (lambda i, j, k: (j, 0, i, 0))
    outs, ex = _call(
        body, name=name, out_shape=[jax.ShapeDtypeStruct(full, bf16)] * n_d, grid=(Kw // tm, N_DEV // g, nk),
        in_specs=[pl.BlockSpec((tk, tm), lambda i, j, k: (k, i))] + [pl.BlockSpec((tk, W), lambda i, j, k: (k, j))] * n_d,
        out_specs=[pl.BlockSpec((None,) + blk, out_map)] * n_d,
        scratch=[pltpu.VMEM(blk, f32)] * n_d, operands=[x, *ds], comm=comm)
    outs = [o.reshape((N_DEV,) + full[2:]) for o in outs]
    return outs if comm is None else (outs, ex)


def _norm_fwd(name, x, g):
    S, D = x.shape
    tm = _tile(S, 512, 8)

    def body(x_ref, g_ref, h_ref):
        xv = x_ref[...]
        r = lax.rsqrt(jnp.mean(xv * xv, axis=-1, keepdims=True) + EPS)
        h_ref[...] = (xv * r * g_ref[...]).astype(bf16)

    return pl.pallas_call(
        body, name=name, out_shape=jax.ShapeDtypeStruct((S, D), bf16), grid=(S // tm,),
        in_specs=[pl.BlockSpec((tm, D), lambda i: (i, 0)), pl.BlockSpec((1, D), lambda i: (0, 0))],
        out_specs=pl.BlockSpec((tm, D), lambda i: (i, 0)), compiler_params=_cp("parallel"),
    )(x, g.reshape(1, D))


def _norm_bwd(name, dh, x, g, dx_in):
    S, D = x.shape
    tm = _tile(S, 256, 8)

    def body(dh_ref, x_ref, g_ref, dxi_ref, dx_ref, dxb_ref, dg_ref):
        i = pl.program_id(0)
        xv = x_ref[...]
        r = lax.rsqrt(jnp.mean(xv * xv, axis=-1, keepdims=True) + EPS)
        xh = xv * r
        dhv = dh_ref[...].astype(f32)
        dxh = dhv * g_ref[...]
        dx = dxi_ref[...] + r * (dxh - xh * jnp.mean(dxh * xh, axis=-1, keepdims=True))
        dx_ref[...] = dx
        dxb_ref[...] = dx.astype(bf16)

        @pl.when(i == 0)
        def _():
            dg_ref[...] = jnp.zeros_like(dg_ref)

        dg_ref[...] += jnp.sum(dhv * xh, axis=0, keepdims=True)

    row = pl.BlockSpec((tm, D), lambda i: (i, 0))
    vec = pl.BlockSpec((1, D), lambda i: (0, 0))
    return pl.pallas_call(
        body, name=name,
        out_shape=[jax.ShapeDtypeStruct((S, D), f32), jax.ShapeDtypeStruct((S, D), bf16), jax.ShapeDtypeStruct((1, D), f32)],
        grid=(S // tm,), in_specs=[row, row, vec, row], out_specs=[row, row, vec], compiler_params=_cp("arbitrary"),
    )(dh, x, g.reshape(1, D), dx_in)


def _loss_head(x, g, t):
    S, D = x.shape
    tm = _tile(S, 256, 8)

    def body(x_ref, g_ref, t_ref, dx_ref, dg_ref, loss_ref):
        i = pl.program_id(0)
        xv = x_ref[...]
        r = lax.rsqrt(jnp.mean(xv * xv, axis=-1, keepdims=True) + EPS)
        xh = xv * r
        gv = g_ref[...]
        err = xh * gv - t_ref[...]
        dy = err * (1.0 / D)
        dxh = dy * gv
        dx_ref[...] = r * (dxh - xh * jnp.mean(dxh * xh, axis=-1, keepdims=True))

        @pl.when(i == 0)
        def _():
            dg_ref[...] = jnp.zeros_like(dg_ref)
            loss_ref[...] = jnp.zeros_like(loss_ref)

        dg_ref[...] += jnp.sum(dy * xh, axis=0, keepdims=True)
        row = jnp.sum(err * err, axis=-1, keepdims=True) * (0.5 / D)
        loss_ref[...] += jnp.broadcast_to(jnp.sum(row, axis=0, keepdims=True), loss_ref.shape)

    return pl.pallas_call(
        body, name="loss_head",
        out_shape=[jax.ShapeDtypeStruct((S, D), f32), jax.ShapeDtypeStruct((1, D), f32), jax.ShapeDtypeStruct((8, LANES), f32)],
        grid=(S // tm,),
        in_specs=[pl.BlockSpec((tm, D), lambda i: (i, 0)), pl.BlockSpec((1, D), lambda i: (0, 0)), pl.BlockSpec((tm, D), lambda i: (i, 0))],
        out_specs=[pl.BlockSpec((tm, D), lambda i: (i, 0)), pl.BlockSpec((1, D), lambda i: (0, 0)), pl.BlockSpec((8, LANES), lambda i: (0, 0))],
        compiler_params=_cp("arbitrary"),
    )(x, g.reshape(1, D), t)


def _perm(t, d):
    if d == 1:
        return t
    S, C = t.shape
    return t.reshape(S // d, d, C).transpose(1, 0, 2).reshape(S, C)


def _rope_tables(S):
    half = HEAD_DIM // 2
    pos = jnp.arange(S, dtype=f32)
    inv_freq = ROPE_THETA ** (-jnp.arange(0, HEAD_DIM, 2, dtype=f32) / HEAD_DIM)
    ang = pos[:, None] * inv_freq[None, :]
    c, s = jnp.cos(ang), jnp.sin(ang)
    cos2 = jnp.concatenate([c, c], axis=-1)
    sin2 = jnp.concatenate([-s, s], axis=-1)
    assert cos2.shape == (S, 2 * half)
    return (jnp.stack([_perm(cos2, d) for d in DILATIONS]), jnp.stack([_perm(sin2, d) for d in DILATIONS]))


def _rope(t, c, s):
    return t * c + pltpu.roll(t, HEAD_DIM // 2, 1) * s


def _rope_bwd(dt, c, s):
    return dt * c - pltpu.roll(dt, HEAD_DIM // 2, 1) * s


def _band_bounds(i, nblk):
    g = pl.program_id(0)
    lb = jnp.right_shift(jnp.int32(nblk), 2 * g)
    pos = lax.rem(i, lb)
    lo = jnp.where(pos == 0, BLK, 0)
    hi = jnp.where(pos == lb - 1, 2 * BLK, 3 * BLK)
    return lo, hi


def _cur_spec(width, t=None):
    if t is None:
        return pl.BlockSpec((None, BLK, width), lambda g, i: (g, i, 0))
    return pl.BlockSpec((None, None, BLK, width), lambda g, i: (t, g, i, 0))


def _band_specs(width, nblk, t=None):
    lo, hi = (lambda i: jnp.maximum(i - 1, 0)), (lambda i: jnp.minimum(i + 1, nblk - 1))
    if t is None:
        return [pl.BlockSpec((None, BLK, width), lambda g, i: (g, lo(i), 0)), _cur_spec(width),
                pl.BlockSpec((None, BLK, width), lambda g, i: (g, hi(i), 0))]
    return [pl.BlockSpec((None, None, BLK, width), lambda g, i: (t, g, lo(i), 0)), _cur_spec(width, t),
            pl.BlockSpec((None, None, BLK, width), lambda g, i: (t, g, hi(i), 0))]


_ROWS = 2048


def _to_scratch(scr, val):
    for h in range(HEADS):
        scr[h] = val[:, h * HEAD_DIM:(h + 1) * HEAD_DIM].astype(f32)


def _qkv_groups(z):
    S = z.shape[0]
    R = min(_ROWS, S)
    nb = S // R

    def body(x_ref, o_ref, scr):
        g, i = pl.program_id(1), pl.program_id(2)
        _to_scratch(scr, x_ref[...])
        for gi, d in enumerate(DILATIONS):
            @pl.when(g == gi)
            def _():
                n, L = R // d, S // d
                for r in range(d):
                    start = pl.multiple_of(r * L + i * n, 16)
                    for h in range(HEADS):
                        o_ref[pl.ds(start, n), h * HEAD_DIM:(h + 1) * HEAD_DIM] = scr[h, pl.ds(r, n, stride=d), :].astype(bf16)

    return pl.pallas_call(
        body, name="qkv_groups", out_shape=jax.ShapeDtypeStruct((3, N_GROUPS, S, ATTN_W), bf16), grid=(3, N_GROUPS, nb),
        in_specs=[pl.BlockSpec((R, ATTN_W), lambda t, g, i: (i, t * N_GROUPS + g))],
        out_specs=pl.BlockSpec((None, None, S, ATTN_W), lambda t, g, i: (t, g, 0, 0)),
        scratch_shapes=[pltpu.VMEM((HEADS, R, HEAD_DIM), f32)], compiler_params=_cp("arbitrary", "arbitrary", "arbitrary"),
    )(z)


def _bwd_groups(attn, dattn, lse):
    S = attn.shape[0]
    R = min(_ROWS, S)
    nb = S // R

    def body(a_ref, d_ref, l_ref, o_ref, scr):
        t, g, i = pl.program_id(0), pl.program_id(1), pl.program_id(2)

        @pl.when(t == 0)
        def _():
            _to_scratch(scr, d_ref[...])

        @pl.when(t == 1)
        def _():
            _to_scratch(scr, l_ref[...])

        @pl.when(t == 2)
        def _():
            prod = a_ref[...].astype(f32) * d_ref[...].astype(f32)
            for h in range(HEADS):
                part = jnp.sum(prod[:, h * HEAD_DIM:(h + 1) * HEAD_DIM], axis=-1, keepdims=True)
                scr[h] = jnp.broadcast_to(part, (R, HEAD_DIM))

        for gi, d in enumerate(DILATIONS):
            @pl.when(g == gi)
            def _():
                n, L = R // d, S // d
                for r in range(d):
                    start = pl.multiple_of(r * L + i * n, 8)
                    for h in range(HEADS):
                        o_ref[pl.ds(start, n), h * HEAD_DIM:(h + 1) * HEAD_DIM] = scr[h, pl.ds(r, n, stride=d), :]

    nat = pl.BlockSpec((R, ATTN_W), lambda t, g, i: (i, 0))
    return pl.pallas_call(
        body, name="bwd_groups", out_shape=jax.ShapeDtypeStruct((3, N_GROUPS, S, ATTN_W), f32), grid=(3, N_GROUPS, nb),
        in_specs=[nat, nat, nat], out_specs=pl.BlockSpec((None, None, S, ATTN_W), lambda t, g, i: (t, g, 0, 0)),
        scratch_shapes=[pltpu.VMEM((HEADS, R, HEAD_DIM), f32)], compiler_params=_cp("arbitrary", "arbitrary", "arbitrary"),
    )(attn, dattn, lse)


def _group_views(t3, R):
    S = t3.shape[1]
    views = [t3.reshape(N_GROUPS, d, S // d, ATTN_W) for d in DILATIONS]
    specs = [pl.BlockSpec((None, d, R // d, ATTN_W), functools.partial(lambda i, g: (g, 0, i, 0), g=g))
             for g, d in enumerate(DILATIONS)]
    return views, specs


def _from_groups(scr, ref, d):
    n = ref.shape[1]
    for r in range(d):
        blk = ref[r]
        for h in range(HEADS):
            scr[h, pl.ds(r, n, stride=d), :] = blk[:, h * HEAD_DIM:(h + 1) * HEAD_DIM].astype(f32)


def _to_natural(t3):
    S = t3.shape[1]
    R = min(_ROWS // 2, S)
    views, specs = _group_views(t3, R)

    def body(v0, v1, v2, o_ref, scr):
        for g, (ref, d) in enumerate(zip((v0, v1, v2), DILATIONS)):
            _from_groups(scr, ref, d)
            for h in range(HEADS):
                o_ref[:, g * ATTN_W + h * HEAD_DIM:g * ATTN_W + (h + 1) * HEAD_DIM] = scr[h].astype(bf16)

    return pl.pallas_call(
        body, name="to_natural", out_shape=jax.ShapeDtypeStruct((S, N_GROUPS * ATTN_W), bf16), grid=(S // R,),
        in_specs=specs, out_specs=pl.BlockSpec((R, N_GROUPS * ATTN_W), lambda i: (i, 0)),
        scratch_shapes=[pltpu.VMEM((HEADS, R, HEAD_DIM), f32)], compiler_params=_cp("arbitrary"),
    )(*views)


_SCALE = HEAD_DIM ** -0.5


def _attn_fwd(qkv, cos, sin, comm=None):
    _, _, S, W = qkv.shape
    nblk = S // BLK

    def body(q_ref, kp, kc, kn, vp, vc, vn, cq, sq, ckp, ckc, ckn, skp, skc, skn, o_ref, lse_ref):
        i = pl.program_id(1)
        lo, hi = _band_bounds(i, nblk)
        a = lax.broadcasted_iota(jnp.int32, (BLK, 3 * BLK), 0)
        b = lax.broadcasted_iota(jnp.int32, (BLK, 3 * BLK), 1)
        mask = (jnp.abs(b - BLK - a) <= RADIUS) & (b >= lo) & (b < hi)
        ck = jnp.concatenate([ckp[...], ckc[...], ckn[...]], axis=0)
        sk = jnp.concatenate([skp[...], skc[...], skn[...]], axis=0)
        for hh in range(HEADS):
            sl = slice(hh * HEAD_DIM, (hh + 1) * HEAD_DIM)
            qh = _rope(q_ref[:, sl].astype(f32), cq[...], sq[...]).astype(bf16)
            kh = jnp.concatenate([kp[:, sl], kc[:, sl], kn[:, sl]], axis=0).astype(f32)
            kh = _rope(kh, ck, sk).astype(bf16)
            vh = jnp.concatenate([vp[:, sl], vc[:, sl], vn[:, sl]], axis=0)
            s = lax.dot_general(qh, kh, NT, preferred_element_type=f32) * _SCALE
            s = jnp.where(mask, s, NEG)
            m = jnp.max(s, axis=-1, keepdims=True)
            e = jnp.exp(s - m)
            den = jnp.sum(e, axis=-1, keepdims=True)
            o = lax.dot_general(e.astype(bf16), vh, NN, preferred_element_type=f32) * (1.0 / den)
            o_ref[:, sl] = o.astype(bf16)
            lse_ref[:, sl] = jnp.broadcast_to(m + jnp.log(den), (BLK, HEAD_DIM))

    blk = _cur_spec(W)
    tab = _cur_spec(HEAD_DIM)
    res, ex = _call(
        body, name="attn_fwd",
        out_shape=[jax.ShapeDtypeStruct((N_GROUPS, S, W), bf16), jax.ShapeDtypeStruct((N_GROUPS, S, W), f32)],
        grid=(N_GROUPS, nblk),
        in_specs=[_cur_spec(W, 0)] + _band_specs(W, nblk, 1) + _band_specs(W, nblk, 2) + [tab, tab]
        + _band_specs(HEAD_DIM, nblk) * 2,
        out_specs=[blk, blk], operands=[qkv] * 7 + [cos, sin, cos, cos, cos, sin, sin, sin], comm=comm)
    return res if comm is None else (res, ex)


def _attn_combine(o3, lse3):
    _, S, W = o3.shape
    R = min(_ROWS // 2, S)
    o_views, specs = _group_views(o3, R)
    l_views, _ = _group_views(lse3, R)

    def body(o0, o1, o2, l0, l1, l2, attn_ref, lse_ref, so0, so1, so2, sl0, sl1, sl2):
        for ref, scr, d in zip((o0, o1, o2, l0, l1, l2), (so0, so1, so2, sl0, sl1, sl2), DILATIONS * 2):
            _from_groups(scr, ref, d)
        for h in range(HEADS):
            a0, a1, a2 = sl0[h], sl1[h], sl2[h]
            m = jnp.maximum(jnp.maximum(a0, a1), a2)
            w0, w1, w2 = jnp.exp(a0 - m), jnp.exp(a1 - m), jnp.exp(a2 - m)
            den = w0 + w1 + w2
            acc = w0 * so0[h] + w1 * so1[h] + w2 * so2[h]
            attn_ref[:, h * HEAD_DIM:(h + 1) * HEAD_DIM] = (acc * (1.0 / den)).astype(bf16)
            lse_ref[:, h * HEAD_DIM:(h + 1) * HEAD_DIM] = m + jnp.log(den)

    nat = pl.BlockSpec((R, W), lambda i: (i, 0))
    return pl.pallas_call(
        body, name="attn_combine", out_shape=[jax.ShapeDtypeStruct((S, W), bf16), jax.ShapeDtypeStruct((S, W), f32)],
        grid=(S // R,), in_specs=specs * 2, out_specs=[nat, nat],
        scratch_shapes=[pltpu.VMEM((HEADS, R, HEAD_DIM), f32)] * 6, compiler_params=_cp("arbitrary"),
    )(*o_views, *l_views)


def _attn_bwd_dq(qkv, cos, sin, b3):
    _, _, S, W = qkv.shape
    nblk = S // BLK

    def body(q_ref, kp, kc, kn, vp, vc, vn, cq, sq, ckp, ckc, ckn, skp, skc, skn, da_ref, l_ref, dl_ref, dq_ref):
        i = pl.program_id(1)
        lo, hi = _band_bounds(i, nblk)
        a = lax.broadcasted_iota(jnp.int32, (BLK, 3 * BLK), 0)
        b = lax.broadcasted_iota(jnp.int32, (BLK, 3 * BLK), 1)
        mask = (jnp.abs(b - BLK - a) <= RADIUS) & (b >= lo) & (b < hi)
        ck = jnp.concatenate([ckp[...], ckc[...], ckn[...]], axis=0)
        sk = jnp.concatenate([skp[...], skc[...], skn[...]], axis=0)
        for hh in range(HEADS):
            sl = slice(hh * HEAD_DIM, (hh + 1) * HEAD_DIM)
            qh = _rope(q_ref[:, sl].astype(f32), cq[...], sq[...]).astype(bf16)
            kh = jnp.concatenate([kp[:, sl], kc[:, sl], kn[:, sl]], axis=0).astype(f32)
            kh = _rope(kh, ck, sk).astype(bf16)
            vh = jnp.concatenate([vp[:, sl], vc[:, sl], vn[:, sl]], axis=0)
            s = lax.dot_general(qh, kh, NT, preferred_element_type=f32) * _SCALE
            lh = l_ref[:, sl]
            l3 = jnp.concatenate([lh, lh, lh], axis=1)
            p = jnp.exp(jnp.where(mask, s - l3, NEG))
            dp = lax.dot_general(da_ref[:, sl].astype(bf16), vh, NT, preferred_element_type=f32)
            dh = dl_ref[:, sl]
            ds = p * (dp - jnp.concatenate([dh, dh, dh], axis=1))
            dqh = lax.dot_general(ds.astype(bf16), kh, NN, preferred_element_type=f32) * _SCALE
            dq_ref[:, sl] = _rope_bwd(dqh, cq[...], sq[...]).astype(bf16)

    tab = _cur_spec(HEAD_DIM)
    return pl.pallas_call(
        body, name="attn_bwd_dq", out_shape=jax.ShapeDtypeStruct((N_GROUPS, S, W), bf16), grid=(N_GROUPS, nblk),
        in_specs=[_cur_spec(W, 0)] + _band_specs(W, nblk, 1) + _band_specs(W, nblk, 2) + [tab, tab]
        + _band_specs(HEAD_DIM, nblk) * 2 + [_cur_spec(W, 0), _cur_spec(W, 1), _cur_spec(W, 2)],
        out_specs=_cur_spec(W), compiler_params=_cp("parallel", "parallel"),
    )(*[qkv] * 7, cos, sin, cos, cos, cos, sin, sin, sin, b3, b3, b3)


def _attn_bwd_dkv(qkv, cos, sin, b3, comm=None):
    _, _, S, W = qkv.shape
    nblk = S // BLK

    def body(k_ref, v_ref, ck, sk, qp, qc, qn, cqp, cqc, cqn, sqp, sqc, sqn, dap, dac, dan, lp, lc, ln, dlp, dlc, dln,
             dk_ref, dv_ref):
        j = pl.program_id(1)
        lo, hi = _band_bounds(j, nblk)
        a = lax.broadcasted_iota(jnp.int32, (3 * BLK, BLK), 0)
        b = lax.broadcasted_iota(jnp.int32, (3 * BLK, BLK), 1)
        mask = (jnp.abs(b - (a - BLK)) <= RADIUS) & (a >= lo) & (a < hi)
        cq = jnp.concatenate([cqp[...], cqc[...], cqn[...]], axis=0)
        sq = jnp.concatenate([sqp[...], sqc[...], sqn[...]], axis=0)
        for hh in range(HEADS):
            sl = slice(hh * HEAD_DIM, (hh + 1) * HEAD_DIM)
            kh = _rope(k_ref[:, sl].astype(f32), ck[...], sk[...]).astype(bf16)
            vh = v_ref[:, sl]
            qh = jnp.concatenate([qp[:, sl], qc[:, sl], qn[:, sl]], axis=0).astype(f32)
            qh = _rope(qh, cq, sq).astype(bf16)
            dah = jnp.concatenate([dap[:, sl], dac[:, sl], dan[:, sl]], axis=0).astype(bf16)
            lh = jnp.concatenate([lp[:, sl], lc[:, sl], ln[:, sl]], axis=0)
            dlh = jnp.concatenate([dlp[:, sl], dlc[:, sl], dln[:, sl]], axis=0)
            s = lax.dot_general(qh, kh, NT, preferred_element_type=f32) * _SCALE
            p = jnp.exp(jnp.where(mask, s - lh, NEG))
            dv_ref[:, sl] = lax.dot_general(p.astype(bf16), dah, TN, preferred_element_type=f32).astype(bf16)
            dp = lax.dot_general(dah, vh, NT, preferred_element_type=f32)
            ds = p * (dp - dlh)
            dkh = lax.dot_general(ds.astype(bf16), qh, TN, preferred_element_type=f32) * _SCALE
            dk_ref[:, sl] = _rope_bwd(dkh, ck[...], sk[...]).astype(bf16)

    blk, tab, bt = _cur_spec(W), _cur_spec(HEAD_DIM), _band_specs(HEAD_DIM, nblk)
    res, ex = _call(
        body, name="attn_bwd_dkv",
        out_shape=[jax.ShapeDtypeStruct((N_GROUPS, S, W), bf16), jax.ShapeDtypeStruct((N_GROUPS, S, W), bf16)],
        grid=(N_GROUPS, nblk),
        in_specs=[_cur_spec(W, 1), _cur_spec(W, 2), tab, tab] + _band_specs(W, nblk, 0) + bt + bt
        + _band_specs(W, nblk, 0) + _band_specs(W, nblk, 1) + _band_specs(W, nblk, 2), out_specs=[blk, blk],
        operands=[qkv, qkv, cos, sin, qkv, qkv, qkv, cos, cos, cos, sin, sin, sin] + [b3] * 9, comm=comm)
    return res if comm is None else (res, ex)


_SG_ROWS = 512


def _sg_z_specs(tm, half):
    o = QKV_W // half
    return [pl.BlockSpec((tm, half), functools.partial(lambda i, c: (i, c), c=o + n)) for n in range(4)]


def _sg_norm(v, lg, lb):
    gv = _gelu(v)
    mu = jnp.mean(gv, axis=-1, keepdims=True)
    xc = gv - mu
    rstd = lax.rsqrt(jnp.mean(xc * xc, axis=-1, keepdims=True) + EPS)
    xh = xc * rstd
    return xh, rstd, xh * lg + lb


def _sg_fwd(z, w, bb, lg, lb):
    S = z.shape[0]
    tm = _tile(S, _SG_ROWS, SG_CHUNK)
    half = SG_W // 2

    def body(u0, u1, v0, v1, w_ref, bb_ref, lg_ref, lb_ref, o_ref):
        u = jnp.concatenate([u0[...], u1[...]], axis=1).astype(f32)
        v = jnp.concatenate([v0[...], v1[...]], axis=1).astype(f32)
        gu = _gelu(u)
        _, _, vn = _sg_norm(v, lg_ref[...], lb_ref[...])
        vnb = vn.astype(bf16)
        for c in range(tm // SG_CHUNK):
            rs = slice(c * SG_CHUNK, (c + 1) * SG_CHUNK)
            for g in range(SG_GROUPS):
                cs = slice(g * 128, (g + 1) * 128)
                mixed = lax.dot_general(w_ref[g], vnb[rs, cs], NN, preferred_element_type=f32) + bb_ref[g]
                o_ref[rs, cs] = (gu[rs, cs] * mixed).astype(bf16)

    full3 = pl.BlockSpec((SG_GROUPS, 128, 128), lambda i: (0, 0, 0))
    vec = pl.BlockSpec((1, SG_W), lambda i: (0, 0))
    return pl.pallas_call(
        body, name="sg_fwd", out_shape=jax.ShapeDtypeStruct((S, SG_W), bf16), grid=(S // tm,),
        in_specs=_sg_z_specs(tm, half) + [full3, full3, vec, vec],
        out_specs=pl.BlockSpec((tm, SG_W), lambda i: (i, 0)), compiler_params=_cp("parallel"),
    )(z, z, z, z, w, bb, lg.reshape(1, SG_W), lb.reshape(1, SG_W))


def _sg_bwd(dsg, z, w, wt, bb, lg, lb):
    S = z.shape[0]
    tm = _tile(S, _SG_ROWS, SG_CHUNK)
    half = SG_W // 2

    def body(d_ref, u0, u1, v0, v1, w_ref, wt_ref, bb_ref, lg_ref, lb_ref, du_ref, dv_ref, dw_ref, db_ref, dlg_ref, dlb_ref, dvn_scr):
        i = pl.program_id(0)

        @pl.when(i == 0)
        def _():
            dw_ref[...] = jnp.zeros_like(dw_ref)
            db_ref[...] = jnp.zeros_like(db_ref)
            dlg_ref[...] = jnp.zeros_like(dlg_ref)
            dlb_ref[...] = jnp.zeros_like(dlb_ref)

        u = jnp.concatenate([u0[...], u1[...]], axis=1).astype(f32)
        v = jnp.concatenate([v0[...], v1[...]], axis=1).astype(f32)
        gu = _gelu(u)
        dgu = _gelu_grad(u)
        xh, rstd, vn = _sg_norm(v, lg_ref[...], lb_ref[...])
        vnb = vn.astype(bf16)
        dsg_v = d_ref[...].astype(f32)
        for g in range(SG_GROUPS):
            cs = slice(g * 128, (g + 1) * 128)
            dw_g = jnp.zeros((128, 128), f32)
            db_g = jnp.zeros((128, 1), f32)
            for c in range(tm // SG_CHUNK):
                rs = slice(c * SG_CHUNK, (c + 1) * SG_CHUNK)
                ds = dsg_v[rs, cs]
                mixed = lax.dot_general(w_ref[g], vnb[rs, cs], NN, preferred_element_type=f32) + bb_ref[g]
                du_ref[rs, cs] = (ds * mixed * dgu[rs, cs]).astype(bf16)
                dmix = ds * gu[rs, cs]
                dmb = dmix.astype(bf16)
                dw_g = dw_g + lax.dot_general(dmb, vnb[rs, cs], NT, preferred_element_type=f32)
                db_g = db_g + jnp.sum(dmix, axis=-1, keepdims=True)
                dvn_scr[rs, cs] = lax.dot_general(wt_ref[g], dmb, NN, preferred_element_type=f32)
            dw_ref[g] += dw_g
            db_ref[g] += jnp.broadcast_to(db_g, (128, 128))
        dvn = dvn_scr[...]
        dlg_ref[...] += jnp.sum(dvn * xh, axis=0, keepdims=True)
        dlb_ref[...] += jnp.sum(dvn, axis=0, keepdims=True)
        dxh = dvn * lg_ref[...]
        dgv = rstd * (dxh - jnp.mean(dxh, axis=-1, keepdims=True) - xh * jnp.mean(dxh * xh, axis=-1, keepdims=True))
        dv_ref[...] = (dgv * _gelu_grad(v)).astype(bf16)

    full3 = pl.BlockSpec((SG_GROUPS, 128, 128), lambda i: (0, 0, 0))
    vec = pl.BlockSpec((1, SG_W), lambda i: (0, 0))
    row = pl.BlockSpec((tm, SG_W), lambda i: (i, 0))
    return pl.pallas_call(
        body, name="sg_bwd",
        out_shape=[jax.ShapeDtypeStruct((S, SG_W), bf16), jax.ShapeDtypeStruct((S, SG_W), bf16),
                   jax.ShapeDtypeStruct((SG_GROUPS, 128, 128), f32), jax.ShapeDtypeStruct((SG_GROUPS, 128, 128), f32),
                   jax.ShapeDtypeStruct((1, SG_W), f32), jax.ShapeDtypeStruct((1, SG_W), f32)],
        grid=(S // tm,),
        in_specs=[row] + _sg_z_specs(tm, half) + [full3, full3, full3, vec, vec],
        out_specs=[row, row, full3, full3, vec, vec],
        scratch_shapes=[pltpu.VMEM((tm, SG_W), f32)], compiler_params=_cp("arbitrary"),
    )(dsg, z, z, z, z, w, wt, bb, lg.reshape(1, SG_W), lb.reshape(1, SG_W))


def _merge_fwd(z, ya, yb, D):
    S = z.shape[0]
    tm, tc = _tile(S, 512, 8), _tile(D, 512)
    o_a, o_b = (QKV_W + 2 * SG_W) // tc, (QKV_W + 2 * SG_W + D) // tc

    def body(ga_ref, gb_ref, ya_ref, yb_ref, o_ref):
        ga = _sigmoid(ga_ref[...].astype(f32))
        gb = _sigmoid(gb_ref[...].astype(f32))
        o_ref[...] = (ga * ya_ref[...].astype(f32) + gb * yb_ref[...].astype(f32)).astype(bf16)

    blk = pl.BlockSpec((tm, tc), lambda i, j: (i, j))
    return pl.pallas_call(
        body, name="merge_fwd", out_shape=jax.ShapeDtypeStruct((S, D), bf16), grid=(S // tm, D // tc),
        in_specs=[pl.BlockSpec((tm, tc), lambda i, j: (i, o_a + j)), pl.BlockSpec((tm, tc), lambda i, j: (i, o_b + j)), blk, blk],
        out_specs=blk, compiler_params=_cp("parallel", "parallel"),
    )(z, z, ya, yb)


def _ple_bwd_ew(dx, gp, e):
    S, D = dx.shape
    tm, tc = _tile(S, 512, 8), _tile(D, 1024)

    def body(dx_ref, gp_ref, e_ref, dgp_ref, de_ref):
        dxv = dx_ref[...]
        sg = _sigmoid(gp_ref[...].astype(f32))
        dgp_ref[...] = (dxv * e_ref[...].astype(f32) * sg * (1.0 - sg)).astype(bf16)
        de_ref[...] = (dxv * sg).astype(bf16)

    blk = pl.BlockSpec((tm, tc), lambda i, j: (i, j))
    return pl.pallas_call(
        body, name="ple_bwd_ew", out_shape=[jax.ShapeDtypeStruct((S, D), bf16)] * 2, grid=(S // tm, D // tc),
        in_specs=[blk, blk, blk], out_specs=[blk, blk], compiler_params=_cp("parallel", "parallel"),
    )(dx, gp, e)


def _adam_math(w, g, m, v):
    m = ADAM_B1 * m + (1.0 - ADAM_B1) * g
    v = ADAM_B2 * v + (1.0 - ADAM_B2) * (g * g)
    m_hat = m / (1.0 - ADAM_B1 ** ADAM_STEP)
    v_hat = v / (1.0 - ADAM_B2 ** ADAM_STEP)
    delta = -ADAM_LR * (m_hat / (jnp.sqrt(v_hat) + ADAM_EPS) + ADAM_WD * w)
    return delta, m, v


def _small_sum_adamw(gathered, w, m, v):
    _, R, _ = gathered.shape
    tr = _tile(R, 1024, SMALL_ROWS)

    def body(p_ref, w_ref, m_ref, v_ref, g_ref, d_ref, nm_ref, nv_ref):
        g = p_ref[0]
        for n in range(1, N_DEV):
            g = g + p_ref[n]
        d, nm, nv = _adam_math(w_ref[...], g, m_ref[...], v_ref[...])
        g_ref[...] = g
        d_ref[...] = d
        nm_ref[...] = nm
        nv_ref[...] = nv

    blk = pl.BlockSpec((tr, LANES), lambda i: (i, 0))
    return pl.pallas_call(
        body, name="small_sum_adamw", out_shape=[jax.ShapeDtypeStruct((R, LANES), f32)] * 4, grid=(R // tr,),
        in_specs=[pl.BlockSpec((N_DEV, tr, LANES), lambda i: (0, i, 0)), blk, blk, blk], out_specs=[blk] * 4,
        compiler_params=_cp("parallel"),
    )(gathered, w, m, v)


def _all_gather(name, shard, in_vmem=False):
    R, C = shard.shape

    def body(x_ref, out_ref, send_sems, recv_sems, local_sem):
        x, y, c = lax.axis_index("x"), lax.axis_index("y"), lax.axis_index("c")
        me, sibling = (x, y, c), (x, y, 1 - c)
        chips = [(1 - x, y), (x, 1 - y), (1 - x, 1 - y)]

        def rows(px, py, pc):
            return out_ref.at[4 * px + 2 * py + pc]

        def copy(k, block, to, src=None):
            return pltpu.make_async_remote_copy(
                src_ref=rows(*block) if src is None else src, dst_ref=rows(*block),
                send_sem=send_sems.at[k], recv_sem=recv_sems.at[k], device_id=to, device_id_type=MESH)

        mine = pltpu.make_async_copy(x_ref, rows(*me), local_sem)
        mine.start()
        first = [copy(0, me, sibling, src=x_ref)]
        first += [copy(1 + j, me, (*chip, c), src=x_ref) for j, chip in enumerate(chips)]
        for cp in first:
            cp.start()
        passed = [copy(4 + j, (*chip, c), sibling) for j, chip in enumerate(chips)]
        for j, chip in enumerate(chips):
            copy(1 + j, (*chip, c), me).wait_recv()
            passed[j].start()
        copy(0, sibling, me).wait_recv()
        for j, chip in enumerate(chips):
            copy(4 + j, (*chip, 1 - c), me).wait_recv()
        for cp in first + passed:
            cp.wait_send()
        mine.wait()

    space = pl.BlockSpec(memory_space=pltpu.VMEM) if in_vmem else _ANY
    return pl.pallas_call(
        body, name=name, out_shape=jax.ShapeDtypeStruct((N_DEV, R, C), shard.dtype),
        in_specs=[space], out_specs=space,
        scratch_shapes=[pltpu.SemaphoreType.DMA((7,)), pltpu.SemaphoreType.DMA((7,)), pltpu.SemaphoreType.DMA],
        compiler_params=pltpu.CompilerParams(has_side_effects=True, vmem_limit_bytes=VMEM_LIMIT),
    )(shard)


class _Exchange:
    def __init__(self, ins, outs, sems, start, finish):
        self.ins, self.outs, self.sems, self.start, self.finish = list(ins), list(outs), list(sems), start, finish


def _run_exchange(name, ex):
    c_in, c_out = len(ex.ins), len(ex.outs)

    def body(*refs):
        ins, outs, sems = refs[:c_in], refs[c_in:c_in + c_out], refs[c_in + c_out:]
        ex.start(ins, outs, sems)
        ex.finish(ins, outs, sems)

    return pl.pallas_call(
        body, name=name, out_shape=ex.outs, in_specs=[_ANY] * c_in, out_specs=[_ANY] * c_out, scratch_shapes=ex.sems,
        compiler_params=pltpu.CompilerParams(has_side_effects=True, vmem_limit_bytes=VMEM_LIMIT),
    )(*ex.ins)


def _gather_exchange(shards):
    n = len(shards)

    def plan(ins, outs, sems):
        send_sems, recv_sems, local_sems = sems
        x, y, c = lax.axis_index("x"), lax.axis_index("y"), lax.axis_index("c")
        me, sibling = (x, y, c), (x, y, 1 - c)
        chips = [(1 - x, y), (x, 1 - y), (1 - x, 1 - y)]

        def rows(w, px, py, pc):
            return outs[w].at[4 * px + 2 * py + pc]

        def copy(w, k, block, to, src=None):
            return pltpu.make_async_remote_copy(
                src_ref=rows(w, *block) if src is None else src, dst_ref=rows(w, *block),
                send_sem=send_sems.at[w, k], recv_sem=recv_sems.at[w, k], device_id=to, device_id_type=MESH)

        mine = [pltpu.make_async_copy(ins[w], rows(w, *me), local_sems.at[w]) for w in range(n)]
        first = []
        for w in range(n):
            first.append(copy(w, 0, me, sibling, src=ins[w]))
            first += [copy(w, 1 + j, me, (*chip, c), src=ins[w]) for j, chip in enumerate(chips)]
        return c, me, sibling, chips, copy, mine, first

    def start(ins, outs, sems):
        _, _, _, _, _, mine, first = plan(ins, outs, sems)
        for cp in mine + first:
            cp.start()

    def finish(ins, outs, sems):
        c, me, sibling, chips, copy, mine, first = plan(ins, outs, sems)
        passed = []
        for w in range(n):
            for j, chip in enumerate(chips):
                copy(w, 1 + j, (*chip, c), me).wait_recv()
                passed.append(copy(w, 4 + j, (*chip, c), sibling))
                passed[-1].start()
        for w in range(n):
            copy(w, 0, sibling, me).wait_recv()
            for j, chip in enumerate(chips):
                copy(w, 4 + j, (*chip, 1 - c), me).wait_recv()
        for cp in first + passed:
            cp.wait_send()
        for cp in mine:
            cp.wait()

    return _Exchange(
        shards, [jax.ShapeDtypeStruct((N_DEV,) + s.shape, s.dtype) for s in shards],
        [pltpu.SemaphoreType.DMA((n, 7)), pltpu.SemaphoreType.DMA((n, 7)), pltpu.SemaphoreType.DMA((n,))], start, finish)


def _sibling_exchange(gs):
    n = len(gs)

    def copies(ins, outs, sems):
        send_sems, recv_sems = sems
        x, y, c = lax.axis_index("x"), lax.axis_index("y"), lax.axis_index("c")
        return [pltpu.make_async_remote_copy(
            src_ref=ins[w].at[2 * q + (1 - c)], dst_ref=outs[w].at[q], send_sem=send_sems.at[w, q],
            recv_sem=recv_sems.at[w, q], device_id=(x, y, 1 - c), device_id_type=MESH) for w in range(n) for q in range(4)]

    def start(ins, outs, sems):
        for cp in copies(ins, outs, sems):
            cp.start()

    def finish(ins, outs, sems):
        cps = copies(ins, outs, sems)
        for cp in cps:
            cp.wait_recv()
        for cp in cps:
            cp.wait_send()

    return _Exchange(gs, [jax.ShapeDtypeStruct((4,) + g.shape[1:], g.dtype) for g in gs],
                     [pltpu.SemaphoreType.DMA((n, 4)), pltpu.SemaphoreType.DMA((n, 4))], start, finish)


def _rs_chip_sum(name, g8, recv, c_idx):
    _, R, C = g8.shape
    tr = _tile(R, 512, 16)
    g42 = g8.reshape(4, 2, R, C)

    def body(c_ref, a_ref, b_ref, o_ref):
        o_ref[...] = (a_ref[...].astype(f32) + b_ref[...].astype(f32)).astype(o_ref.dtype)

    return pl.pallas_call(
        body, name=name, out_shape=jax.ShapeDtypeStruct((4, R, C), g8.dtype),
        grid_spec=pltpu.PrefetchScalarGridSpec(
            num_scalar_prefetch=1, grid=(4, R // tr),
            in_specs=[pl.BlockSpec((None, None, tr, C), lambda q, r, c_ref: (q, c_ref[0], r, 0)),
                      pl.BlockSpec((None, tr, C), lambda q, r, c_ref: (q, r, 0))],
            out_specs=pl.BlockSpec((None, tr, C), lambda q, r, c_ref: (q, r, 0))),
        compiler_params=_cp("parallel", "parallel"),
    )(c_idx, g42, recv)


def _chips_exchange(p4s):
    n = len(p4s)

    def copies(ins, outs, sems):
        send_sems, recv_sems = sems
        x, y, c = lax.axis_index("x"), lax.axis_index("y"), lax.axis_index("c")
        chips = [(1 - x, y), (x, 1 - y), (1 - x, 1 - y)]
        return [pltpu.make_async_remote_copy(
            src_ref=ins[w].at[2 * cx + cy], dst_ref=outs[w].at[k], send_sem=send_sems.at[w, k],
            recv_sem=recv_sems.at[w, k], device_id=(cx, cy, c), device_id_type=MESH)
            for w in range(n) for k, (cx, cy) in enumerate(chips)]

    def start(ins, outs, sems):
        for cp in copies(ins, outs, sems):
            cp.start()

    def finish(ins, outs, sems):
        cps = copies(ins, outs, sems)
        for cp in cps:
            cp.wait_recv()
        for cp in cps:
            cp.wait_send()

    return _Exchange(p4s, [jax.ShapeDtypeStruct((3,) + p.shape[1:], p.dtype) for p in p4s],
                     [pltpu.SemaphoreType.DMA((n, 3)), pltpu.SemaphoreType.DMA((n, 3))], start, finish)


def _adamw_layer(name, layer, w, m, v, p4, recv, q_idx, prev):
    depth, R, C = w.shape
    tr = _tile(R, 256, 8)

    def body(q_ref, w_ref, m_ref, v_ref, a_ref, b_ref, *rest):
        g_ref, d_ref, nm_ref, nv_ref = rest[-4:]
        g = ((a_ref[...].astype(f32) + b_ref[0].astype(f32)) + b_ref[1].astype(f32)) + b_ref[2].astype(f32)
        d, nm, nv = _adam_math(w_ref[...], g, m_ref[...], v_ref[...])
        g_ref[...] = g
        d_ref[...] = d
        nm_ref[...] = nm
        nv_ref[...] = nv

    lay = pl.BlockSpec((None, tr, C), lambda i, q_ref: (layer, i, 0))
    n_prev = 0 if prev is None else 4
    return pl.pallas_call(
        body, name=name, out_shape=[jax.ShapeDtypeStruct((depth, R, C), f32)] * 4,
        grid_spec=pltpu.PrefetchScalarGridSpec(
            num_scalar_prefetch=1, grid=(R // tr,),
            in_specs=[lay, lay, lay, pl.BlockSpec((None, tr, C), lambda i, q_ref: (q_ref[0], i, 0)),
                      pl.BlockSpec((3, tr, C), lambda i, q_ref: (0, i, 0))] + [_ANY] * n_prev,
            out_specs=[lay] * 4),
        input_output_aliases={6 + n: n for n in range(n_prev)},
        compiler_params=_cp("parallel"),
    )(q_idx, w, m, v, p4, recv, *(prev or ()))


_BIG = (("w_in", 1), ("w_br_attn", 1), ("w_br_sg", 1), ("w_out", 0), ("w_ff_gate", 1), ("w_ff_up", 1),
        ("w_ff_down", 0), ("w_ple_gate", 0), ("w_ple", 1))


_TURNED = ("w_in", "w_ff_gate", "w_ff_up")


def _gather_plan(i, depth):
    mixer = ["w_br_attn", "w_br_sg", "w_out"]
    plan = {
        "mm_in": [(i, "w_ff_gate")] + ([(i, n) for n in mixer] if i == 0 else []),
        "attn_fwd": [(i, "w_ff_up")],
        "mm_ffn_in": [(i, "w_ff_down"), (i, "w_ple_gate"), (i, "w_ple")],
    }
    if i + 1 < depth:
        plan["mm_ffn_out"] = [(i + 1, "w_in")]
        plan["mm_ple"] = [(i + 1, n) for n in mixer]
    return plan


def _layer_fwd(x0, p_i, layer, arrived, sm, tabs, comm):
    S, D = x0.shape
    cos, sin = tabs
    tmm = _tile(S, 1024, 8)
    same = lambda accs, ex: accs

    def W(name):
        return arrived[(layer, name)]

    def hosted(key, fn):
        if key not in comm:
            return fn(None)
        ex, keys = comm[key]
        res, outs = fn(ex)
        arrived.update(zip(keys, outs))
        return res

    h1 = _norm_fwd("norm_fwd", x0, sm["norm_mix"])
    z = hosted("mm_in", lambda ex: _mm_nn_cols("mm_in", h1, [W("w_in")], [bf16], same, tr=True, comm=ex))[0]
    IN = z.shape[1]

    qkv = _qkv_groups(z)
    o3, lse3 = hosted("attn_fwd", lambda ex: _attn_fwd(qkv, cos, sin, comm=ex))
    attn, lse = _attn_combine(o3, lse3)
    ya = _mm_nn_cols("mm_br_attn", attn, [W("w_br_attn")], [bf16], same, tm_pref=1024)[0]
    sgw = sm["sg_w"].astype(bf16)
    bb = jnp.broadcast_to(sm["sg_b"][:, :, None], (SG_GROUPS, SG_CHUNK, 128))
    sg = _sg_fwd(z, sgw, bb, sm["sg_ln_g"], sm["sg_ln_b"])
    yb = _mm_nn_cols("mm_br_sg", sg, [W("w_br_sg")], [bf16], same, tm_pref=1024)[0]
    merged = _merge_fwd(z, ya, yb, D)
    tn = _tile(D, 1024)
    x1 = _matmul("mm_out", [(merged, W("w_out").reshape(D, D), "nn", 0)], S, D, tmm, tn, 1,
                 [((S, D), f32, (tmm, tn), lambda i, j: (i, j))], lambda accs, ex: [ex[0] + accs[0]],
                 extras=[(x0, (tmm, tn), lambda i, j: (i, j))], chunk=512)[0]
    h2 = _norm_fwd("norm_fwd", x1, sm["norm_ffn"])

    def ffn_ep(accs, ex):
        a, b = accs
        return [a, b, a * _sigmoid(a) * b]

    a, b, f = hosted("mm_ffn_in", lambda ex: _mm_nn_cols("mm_ffn_in", h2, [W("w_ff_gate"), W("w_ff_up")], [bf16] * 3, ffn_ep,
                                                         tr=True, comm=ex))
    w_down = W("w_ff_down").reshape(-1, D)
    F = w_down.shape[0]
    nk = F // _tile(F, 1408)
    x2 = hosted("mm_ffn_out", lambda ex: _matmul(
        "mm_ffn_out", [(f, w_down, "nn", 0)], S, D, tmm, tn, nk, [((S, D), f32, (tmm, tn), lambda i, j: (i, j))],
        lambda accs, ex_tiles: [ex_tiles[0] + accs[0]], extras=[(x1, (tmm, tn), lambda i, j: (i, j))], comm=ex))[0]
    h3 = _norm_fwd("norm_fwd", x2, sm["norm_ple"])

    e = _mm_nn_cols("mm_ple_emb", p_i, [W("w_ple")], [bf16], same, tm_pref=1024)[0]

    def ple_ep(accs, ex):
        gp = accs[0]
        return [ex[0] + _sigmoid(gp) * ex[1].astype(f32), gp]

    x3, gp = hosted("mm_ple", lambda ex: _matmul(
        "mm_ple", [(h3, W("w_ple_gate").reshape(D, D), "nn", 0)], S, D, tmm, tn, 1,
        [((S, D), f32, (tmm, tn), lambda i, j: (i, j)), ((S, D), bf16, (tmm, tn), lambda i, j: (i, j))],
        ple_ep, extras=[(x2, (tmm, tn), lambda i, j: (i, j)), (e, (tmm, tn), lambda i, j: (i, j))], chunk=512, comm=ex))
    saved = dict(x0=x0, h1=h1, z=z, qkv=qkv, attn=attn, lse=lse, ya=ya, yb=yb, sg=sg, merged=merged, x1=x1,
                 h2=h2, a=a, b=b, f=f, x2=x2, h3=h3, gp=gp, e=e, sgw=sgw, bb=bb, IN=IN)
    return x3, saved


def _layer_bwd(dx3, p_i, W, sm, tabs, sv, c_idx):
    S, D = dx3.shape
    w_out, w_down, w_pg = W["w_out"].reshape(D, D), W["w_ff_down"].reshape(-1, D), W["w_ple_gate"].reshape(D, D)
    F = w_down.shape[0]
    cos, sin = tabs
    tmm = _tile(S, 1024, 8)
    tn = _tile(D, 512)
    reduced = {}

    def chip_sums(grads):
        names = list(grads)
        recv = _run_exchange("rs_sibling_" + names[0], _sibling_exchange([grads[n] for n in names]))
        p4 = {n: _rs_chip_sum("rs_chip_sum_" + n, grads[n], r, c_idx) for n, r in zip(names, recv)}
        return p4, _chips_exchange([p4[n] for n in names])

    def carry(p4, outs):
        reduced.update({n: (p4[n], r) for n, r in zip(p4, outs)})

    def blocks(full):
        return full.reshape(N_DEV, full.shape[0] // N_DEV, full.shape[1])

    dgp, de = _ple_bwd_ew(dx3, sv["gp"], sv["e"])
    d_w_ple = _mm_tn_cols("mm_dw_ple", p_i, [de], D // N_DEV)[0]
    d_w_pg = blocks(_mm_simple("mm_dw_dd", sv["h3"], dgp, "tn", bf16, 1024, 1024, 2048))
    dh3 = _mm_simple("mm_dh_dd", dgp, w_pg, "nt", bf16, 1024, 1024, 2048)
    dx2, dx2b, dg_ple = _norm_bwd("norm_bwd", dh3, sv["x2"], sm["norm_ple"], dx3)
    tf = _tile(F, 1408)

    def ffn_bwd_ep(accs, ex):
        df = accs[0]
        a, b = ex[0].astype(f32), ex[1].astype(f32)
        sg = _sigmoid(a)
        return [df * b * sg * (1.0 + a * (1.0 - sg)), df * a * sg]

    th = _tile(S, 512, 8)
    da, db = _matmul("mm_dffn", [(dx2b, w_down, "nt", 0)], S, F, th, tf, 1,
                     [((S, F), bf16, (th, tf), lambda i, j: (i, j))] * 2, ffn_bwd_ep,
                     extras=[(sv["a"], (th, tf), lambda i, j: (i, j)), (sv["b"], (th, tf), lambda i, j: (i, j))], chunk=512)
    d_w_down = blocks(_mm_simple("mm_dw_fd", sv["f"], dx2b, "tn", bf16, 1408, 1024, 2048))
    p4, ex = chip_sums(dict(w_ff_down=d_w_down, w_ple_gate=d_w_pg, w_ple=d_w_ple))
    (d_w_gate, d_w_up), outs = _mm_tn_cols("mm_dw_df", sv["h2"], [da, db], F // N_DEV, tr=True, comm=ex)
    carry(p4, outs)
    p4, ex = chip_sums(dict(w_ff_gate=d_w_gate, w_ff_up=d_w_up))
    dh2, outs = _mm_nt_cols("mm_dh_ffn", [(da, W["w_ff_gate"]), (db, W["w_ff_up"])], bf16, tr=True, comm=ex)
    carry(p4, outs)
    dx1, dx1b, dg_ffn = _norm_bwd("norm_bwd", dh2, sv["x1"], sm["norm_ffn"], dx2)
    z = sv["z"]
    o_a, o_b = (QKV_W + 2 * SG_W) // tn, (QKV_W + 2 * SG_W + D) // tn

    def merge_bwd_ep(accs, ex):
        dm = accs[0]
        ga, gb = _sigmoid(ex[0].astype(f32)), _sigmoid(ex[1].astype(f32))
        ya, yb = ex[2].astype(f32), ex[3].astype(f32)
        return [dm * ya * ga * (1.0 - ga), dm * yb * gb * (1.0 - gb), dm * ga, dm * gb]

    dga, dgb, dya, dyb = _matmul(
        "mm_dmerge", [(dx1b, w_out, "nt", 0)], S, D, tmm, tn, 1,
        [((S, D), bf16, (tmm, tn), lambda i, j: (i, j))] * 4, merge_bwd_ep,
        extras=[(z, (tmm, tn), lambda i, j: (i, o_a + j)), (z, (tmm, tn), lambda i, j: (i, o_b + j)),
                (sv["ya"], (tmm, tn), lambda i, j: (i, j)), (sv["yb"], (tmm, tn), lambda i, j: (i, j))], chunk=256)
    d_w_out = blocks(_mm_simple("mm_dw_dd", sv["merged"], dx1b, "tn", bf16, 1024, 1024, 2048))
    dsg = _mm_nt_cols("mm_dsg", [(dyb, W["w_br_sg"])], bf16)
    d_w_bsg = _mm_tn_cols("mm_dw_bsg", sv["sg"], [dyb], D // N_DEV)[0]
    dattn = _mm_nt_cols("mm_dattn", [(dya, W["w_br_attn"])], bf16)
    d_w_battn = _mm_tn_cols("mm_dw_battn", sv["attn"], [dya], D // N_DEV)[0]
    sgwt = jnp.swapaxes(sm["sg_w"], 1, 2).astype(bf16)
    du, dv_sg, d_sgw, d_sgb, d_lg, d_lb = _sg_bwd(dsg, z, sv["sgw"], sgwt, sv["bb"], sm["sg_ln_g"], sm["sg_ln_b"])
    b3 = _bwd_groups(sv["attn"], dattn, sv["lse"])
    dqg = _attn_bwd_dq(sv["qkv"], cos, sin, b3)
    p4, ex = chip_sums(dict(w_out=d_w_out, w_br_sg=d_w_bsg, w_br_attn=d_w_battn))
    (dkg, dvg), outs = _attn_bwd_dkv(sv["qkv"], cos, sin, b3, comm=ex)
    carry(p4, outs)
    dz = jnp.concatenate([_to_natural(dqg), _to_natural(dkg), _to_natural(dvg), du, dv_sg, dga, dgb], axis=1)
    d_w_in = _mm_tn_cols("mm_dw_in", sv["h1"], [dz], sv["IN"] // N_DEV, tr=True)[0]
    p4, ex = chip_sums(dict(w_in=d_w_in))
    dh1, outs = _mm_nt_cols("mm_dh_in", [(dz, W["w_in"])], bf16, tr=True, comm=ex)
    carry(p4, outs)
    dx0, _, dg_mix = _norm_bwd("norm_bwd", dh1, sv["x0"], sm["norm_mix"], dx1)
    small = dict(sg_w=d_sgw, sg_b=d_sgb[:, :, 0], sg_ln_g=d_lg[0], sg_ln_b=d_lb[0], norm_mix=dg_mix[0], norm_ffn=dg_ffn[0],
                 norm_ple=dg_ple[0])
    return dx0, reduced, small


_SMALL = ("sg_w", "sg_b", "sg_ln_g", "sg_ln_b", "norm_mix", "norm_ffn", "norm_ple", "norm_final")


SMALL_ROWS = 256


def _pack_small(parts, tail):
    rows = [parts[n].astype(f32).reshape(-1, LANES) for n in _SMALL] + [tail]
    n = sum(r.shape[0] for r in rows)
    return jnp.concatenate(rows + [jnp.zeros((-n % SMALL_ROWS, LANES), f32)], axis=0)


def kernel(x, p, w_in, w_br_attn, w_br_sg, w_out, sg_w, sg_b, sg_ln_g, sg_ln_b, norm_mix, norm_ffn, norm_ple, norm_final, w_ff_gate, w_ff_up, w_ff_down, w_ple_gate, w_ple, loss_target, m_w_in, m_w_br_attn, m_w_br_sg, m_w_out, m_sg_w, m_sg_b, m_sg_ln_g, m_sg_ln_b, m_norm_mix, m_norm_ffn, m_norm_ple, m_norm_final, m_w_ff_gate, m_w_ff_up, m_w_ff_down, m_w_ple_gate, m_w_ple, v_w_in, v_w_br_attn, v_w_br_sg, v_w_out, v_sg_w, v_sg_b, v_sg_ln_g, v_sg_ln_b, v_norm_mix, v_norm_ffn, v_norm_ple, v_norm_final, v_w_ff_gate, v_w_ff_up, v_w_ff_down, v_w_ple_gate, v_w_ple):
    wts = dict(w_in=w_in, w_br_attn=w_br_attn, w_br_sg=w_br_sg, w_out=w_out, w_ff_gate=w_ff_gate, w_ff_up=w_ff_up,
               w_ff_down=w_ff_down, w_ple_gate=w_ple_gate, w_ple=w_ple)
    mom_m = dict(w_in=m_w_in, w_br_attn=m_w_br_attn, w_br_sg=m_w_br_sg, w_out=m_w_out, w_ff_gate=m_w_ff_gate,
                 w_ff_up=m_w_ff_up, w_ff_down=m_w_ff_down, w_ple_gate=m_w_ple_gate, w_ple=m_w_ple)
    mom_v = dict(w_in=v_w_in, w_br_attn=v_w_br_attn, w_br_sg=v_w_br_sg, w_out=v_w_out, w_ff_gate=v_w_ff_gate,
                 w_ff_up=v_w_ff_up, w_ff_down=v_w_ff_down, w_ple_gate=v_w_ple_gate, w_ple=v_w_ple)
    small_w = dict(sg_w=sg_w, sg_b=sg_b, sg_ln_g=sg_ln_g, sg_ln_b=sg_ln_b, norm_mix=norm_mix, norm_ffn=norm_ffn,
                   norm_ple=norm_ple, norm_final=norm_final)
    small_m = dict(sg_w=m_sg_w, sg_b=m_sg_b, sg_ln_g=m_sg_ln_g, sg_ln_b=m_sg_ln_b, norm_mix=m_norm_mix, norm_ffn=m_norm_ffn,
                   norm_ple=m_norm_ple, norm_final=m_norm_final)
    small_v = dict(sg_w=v_sg_w, sg_b=v_sg_b, sg_ln_g=v_sg_ln_g, sg_ln_b=v_sg_ln_b, norm_mix=v_norm_mix, norm_ffn=v_norm_ffn,
                   norm_ple=v_norm_ple, norm_final=v_norm_final)
    depth = w_in.shape[0]
    S = x.shape[1]
    names = [n for n, _ in _BIG]
    c_idx = lax.axis_index("c").astype(jnp.int32).reshape(1)
    q_idx = (2 * lax.axis_index("x") + lax.axis_index("y")).astype(jnp.int32).reshape(1)
    tabs = _rope_tables(S)

    def turned(n, t):
        return jnp.swapaxes(t, -1, -2) if n in _TURNED else t

    def gather(keys):
        return _gather_exchange([turned(n, wts[n][i]).astype(bf16) for i, n in keys]), keys

    ex, keys = gather([(0, "w_in")])
    arrived = dict(zip(keys, _run_exchange("ag_w_in", ex)))

    xs = x[0]
    saved = []
    for i in range(depth):
        sm = {n: small_w[n][i] for n in _SMALL if n != "norm_final"}
        comm = {carrier: gather(keys) for carrier, keys in _gather_plan(i, depth).items()}
        xs, sv = _layer_fwd(xs, p[i, 0], i, arrived, sm, tabs, comm)
        saved.append(sv)
    dx, dg_final, loss_part = _loss_head(xs, norm_final, loss_target[0])

    reduced = [None] * depth
    small_parts = [None] * depth
    for i in reversed(range(depth)):
        sm = {n: small_w[n][i] for n in _SMALL if n != "norm_final"}
        dx, reduced[i], small_parts[i] = _layer_bwd(dx, p[i, 0], {n: arrived[(i, n)] for n in names}, sm, tabs, saved[i], c_idx)
    grad_x = dx[None]

    parts = {n: jnp.stack([small_parts[i][n] for i in range(depth)]) for n in _SMALL if n != "norm_final"}
    parts["norm_final"] = dg_final[0]
    gathered = _all_gather("ag_small", _pack_small(parts, loss_part), in_vmem=True)
    g_s, d_s, nm_s, nv_s = _small_sum_adamw(gathered, _pack_small(small_w, jnp.zeros((8, LANES), f32)),
                                            _pack_small(small_m, jnp.zeros((8, LANES), f32)),
                                            _pack_small(small_v, jnp.ones((8, LANES), f32)))
    loss = g_s[sum(small_w[n].size for n in _SMALL) // LANES, 0]

    def unpack_small(flat):
        out, off = {}, 0
        for n in _SMALL:
            k = small_w[n].size // LANES
            out[n] = flat[off:off + k].reshape(small_w[n].shape)
            off += k
        return out

    sm_g, sm_d, sm_nm, sm_nv = unpack_small(g_s), unpack_small(d_s), unpack_small(nm_s), unpack_small(nv_s)

    big_g, big_d, big_nm, big_nv = {}, {}, {}, {}
    for k, n in enumerate(names):
        outs = None
        for i in range(depth):
            p4, recv2 = reduced[i][n]
            outs = _adamw_layer(f"adamw_{n}_{i}", i, turned(n, wts[n]), turned(n, mom_m[n]), turned(n, mom_v[n]), p4, recv2,
                                q_idx, outs)
        big_g[n], big_d[n], big_nm[n], big_nv[n] = [turned(n, o) for o in outs]

    order = ["w_in", "w_br_attn", "w_br_sg", "w_out", "sg_w", "sg_b", "sg_ln_g", "sg_ln_b", "norm_mix", "norm_ffn", "norm_ple",
             "norm_final", "w_ff_gate", "w_ff_up", "w_ff_down", "w_ple_gate", "w_ple"]

    def pick(big, small):
        return [big[n] if n in big else small[n] for n in order]

    return (loss, grad_x, *pick(big_g, sm_g), *pick(big_d, sm_d), *pick(big_nm, sm_nm), *pick(big_nv, sm_nv))
```

```python
import functools
import math

import jax
import jax.numpy as jnp
from jax import lax
from jax.experimental import pallas as pl
from jax.experimental.pallas import tpu as pltpu

f32 = jnp.float32
bf16 = jnp.bfloat16

HEAD_DIM = 128
N_GROUPS = 3
HEADS = 4
DILATIONS = (1, 4, 16)
RADIUS = 64
BLK = 128
QKV_W = 3 * N_GROUPS * HEADS * HEAD_DIM
ATTN_W = HEADS * HEAD_DIM
SG_CHUNK = 128
SG_GROUPS = 8
SG_W = SG_GROUPS * 128
ROPE_THETA = 10000.0
EPS = 1e-6
NEG = -1e30
N_DEV = 8
LANES = 128

ADAM_LR = 0.001
ADAM_B1 = 0.9
ADAM_B2 = 0.999
ADAM_EPS = 1e-08
ADAM_WD = 0.01
ADAM_STEP = 10

VMEM_LIMIT = 56 * 1024 * 1024
MESH = pl.DeviceIdType.MESH

NN = (((1,), (0,)), ((), ()))
NT = (((1,), (1,)), ((), ()))
TN = (((0,), (0,)), ((), ()))
_DN = {"nn": NN, "nt": NT, "tn": TN}


def _cp(*sem):
    return pltpu.CompilerParams(dimension_semantics=sem, vmem_limit_bytes=VMEM_LIMIT)


def _tile(n, pref, unit=128):
    if n <= pref:
        return n
    t = (pref // unit) * unit
    while t >= unit:
        if n % t == 0:
            return t
        t -= unit
    return n


_ANY = pl.BlockSpec(memory_space=pl.ANY)


def _call(body, *, name, grid, in_specs, out_specs, out_shape, operands, scratch=(), comm=None):
    in_specs, out_specs, out_shape, scratch = list(in_specs), list(out_specs), list(out_shape), list(scratch)
    if comm is None:
        res = pl.pallas_call(
            body, name=name, out_shape=out_shape, grid=grid, in_specs=in_specs, out_specs=out_specs, scratch_shapes=scratch,
            compiler_params=_cp(*(("arbitrary",) * len(grid))))(*operands)
        return res, []
    n_in, n_out, n_scr = len(in_specs), len(out_specs), len(scratch)
    c_in, c_out = len(comm.ins), len(comm.outs)

    def hosted(*refs):
        own_in, refs = refs[:n_in], refs[n_in:]
        ex_in, refs = refs[:c_in], refs[c_in:]
        own_out, refs = refs[:n_out], refs[n_out:]
        ex_out, refs = refs[:c_out], refs[c_out:]
        own_scr, sems = refs[:n_scr], refs[n_scr:]
        ids = [pl.program_id(a) for a in range(len(grid))]
        first = functools.reduce(jnp.logical_and, [i == 0 for i in ids])
        last = functools.reduce(jnp.logical_and, [i == g - 1 for i, g in zip(ids, grid)])

        @pl.when(first)
        def _():
            comm.start(ex_in, ex_out, sems)

        body(*own_in, *own_out, *own_scr)

        @pl.when(last)
        def _():
            comm.finish(ex_in, ex_out, sems)

    res = pl.pallas_call(
        hosted, name=name, out_shape=out_shape + list(comm.outs), grid=grid,
        in_specs=in_specs + [_ANY] * c_in, out_specs=out_specs + [_ANY] * c_out, scratch_shapes=scratch + list(comm.sems),
        compiler_params=pltpu.CompilerParams(dimension_semantics=("arbitrary",) * len(grid), vmem_limit_bytes=VMEM_LIMIT,
                                             has_side_effects=True),
    )(*operands, *comm.ins)
    return res[:n_out], res[n_out:]


def _sigmoid(x):
    return 1.0 / (1.0 + jnp.exp(-x))


_GC = math.sqrt(2.0 / math.pi)
_GA = 0.044715


def _gelu(x):
    return 0.5 * x * (1.0 + jnp.tanh(_GC * (x + _GA * x * x * x)))


def _gelu_grad(x):
    t = jnp.tanh(_GC * (x + _GA * x * x * x))
    return 0.5 * (1.0 + t) + 0.5 * x * (1.0 - t * t) * _GC * (1.0 + 3.0 * _GA * x * x)


def _matmul(name, prods, M, N, tm, tn, nk, outs, epilogue, extras=(), n_acc=1, chunk=None, comm=None):
    in_specs, operands, metas = [], [], []
    for a, b, mode, acc in prods:
        if mode == "tn":
            tk = a.shape[0] // nk
            in_specs += [pl.BlockSpec((tk, tm), lambda i, j, k: (k, i)), pl.BlockSpec((tk, tn), lambda i, j, k: (k, j))]
        elif mode == "nt":
            tk = a.shape[1] // nk
            in_specs += [pl.BlockSpec((tm, tk), lambda i, j, k: (i, k)), pl.BlockSpec((tn, tk), lambda i, j, k: (j, k))]
        else:
            tk = a.shape[1] // nk
            in_specs += [pl.BlockSpec((tm, tk), lambda i, j, k: (i, k)), pl.BlockSpec((tk, tn), lambda i, j, k: (k, j))]
        operands += [a, b]
        metas.append((mode, acc))
    for arr, bshape, imap in extras:
        in_specs.append(pl.BlockSpec(bshape, functools.partial(lambda i, j, k, f: f(i, j), f=imap)))
        operands.append(arr)
    out_specs = [pl.BlockSpec(bs, functools.partial(lambda i, j, k, f: f(i, j), f=imap)) for _, _, bs, imap in outs]
    out_shape = [jax.ShapeDtypeStruct(s, d) for s, d, _, _ in outs]
    n_prod, n_ext, n_out = len(prods), len(extras), len(outs)

    def body(*refs):
        in_refs = refs[: 2 * n_prod]
        ex_refs = refs[2 * n_prod : 2 * n_prod + n_ext]
        out_refs = refs[2 * n_prod + n_ext : 2 * n_prod + n_ext + n_out]
        acc_refs = refs[2 * n_prod + n_ext + n_out :]

        def partials():
            res = [None] * n_acc
            for idx, (mode, acc) in enumerate(metas):
                a = in_refs[2 * idx][...].astype(bf16)
                b = in_refs[2 * idx + 1][...].astype(bf16)
                d = lax.dot_general(a, b, _DN[mode], preferred_element_type=f32)
                res[acc] = d if res[acc] is None else res[acc] + d
            return res

        def finish(accs):
            vals = epilogue(accs, [r[...] for r in ex_refs])
            for r, v in zip(out_refs, vals):
                r[...] = v.astype(r.dtype)

        if nk == 1:
            step = chunk or tn
            for c0 in range(0, tn, step):
                c1 = min(c0 + step, tn)
                res = [None] * n_acc
                for idx, (mode, acc) in enumerate(metas):
                    a = in_refs[2 * idx][...].astype(bf16)
                    b_ref = in_refs[2 * idx + 1]
                    b = (b_ref[c0:c1, :] if mode == "nt" else b_ref[:, c0:c1]).astype(bf16)
                    d = lax.dot_general(a, b, _DN[mode], preferred_element_type=f32)
                    res[acc] = d if res[acc] is None else res[acc] + d
                for r, v in zip(out_refs, epilogue(res, [r[:, c0:c1] for r in ex_refs])):
                    r[:, c0:c1] = v.astype(r.dtype)
        else:
            k = pl.program_id(2)
            parts = partials()

            @pl.when(k == 0)
            def _():
                for r, d in zip(acc_refs, parts):
                    r[...] = d

            @pl.when(k > 0)
            def _():
                for r, d in zip(acc_refs, parts):
                    r[...] += d

            @pl.when(k == nk - 1)
            def _():
                finish([r[...] for r in acc_refs])

    scratch = [pltpu.VMEM((tm, tn), f32) for _ in range(n_acc)] if nk > 1 else []
    res, ex = _call(body, name=name, grid=(M // tm, N // tn, nk), in_specs=in_specs, out_specs=out_specs,
                    out_shape=out_shape, operands=operands, scratch=scratch, comm=comm)
    return res if comm is None else (res, ex)


def _ident(accs, ex):
    return [accs[0]]


def _mm_simple(name, a, b, mode, out_dtype, tm_pref=1024, tn_pref=1024, tk_pref=1024, comm=None):
    if mode == "tn":
        K, M = a.shape
        N = b.shape[1]
    elif mode == "nt":
        M, K = a.shape
        N = b.shape[0]
    else:
        M, K = a.shape
        N = b.shape[1]
    tm, tn, tk = _tile(M, tm_pref), _tile(N, tn_pref), _tile(K, tk_pref)
    res = _matmul(name, [(a, b, mode, 0)], M, N, tm, tn, K // tk,
                  [((M, N), out_dtype, (tm, tn), lambda i, j: (i, j))], _ident, comm=comm)
    return res[0] if comm is None else (res[0][0], res[1])


def _group(c, width_pref=1024):
    g = LANES // math.gcd(c, LANES)
    while g < N_DEV and 2 * g * c <= width_pref:
        g *= 2
    return g


def _join(parts):
    return parts[0] if len(parts) == 1 else jnp.concatenate(parts, axis=1)


def _mm_nn_cols(name, a, gs_list, outs_dtypes, epilogue, tm_pref=512, width_pref=1024, tr=False, comm=None):
    M, K = a.shape
    c = gs_list[0].shape[1 if tr else 2]
    g = _group(c, width_pref)
    W = g * c
    tm = _tile(M, tm_pref, 8)
    n_g, n_out = len(gs_list), len(outs_dtypes)
    blk = (g, c, K) if tr else (g, K, c)

    def body(*refs):
        a_ref = refs[0]
        g_refs = refs[1:1 + n_g]
        out_refs = refs[1 + n_g:]
        av = a_ref[...].astype(bf16)
        cols = [epilogue([lax.dot_general(av, gr[s], NT if tr else NN, preferred_element_type=f32) for gr in g_refs], [])
                for s in range(g)]
        for n, r in enumerate(out_refs):
            r[...] = _join([cols[s][n].astype(r.dtype) for s in range(g)])

    tile = pl.BlockSpec((tm, W), lambda j, i: (i, j))
    res, ex = _call(
        body, name=name, out_shape=[jax.ShapeDtypeStruct((M, N_DEV * c), d) for d in outs_dtypes],
        grid=(N_DEV // g, M // tm),
        in_specs=[pl.BlockSpec((tm, K), lambda j, i: (i, 0))] + [pl.BlockSpec((None,) + blk, lambda j, i: (j, 0, 0, 0))] * n_g,
        out_specs=[tile] * n_out, operands=[a, *[gm.reshape((N_DEV // g,) + blk) for gm in gs_list]], comm=comm)
    return res if comm is None else (res, ex)


def _mm_nt_cols(name, pairs, out_dtype, tm_pref=1024, tn_pref=1024, width_pref=1024, tr=False, comm=None):
    M = pairs[0][0].shape[0]
    c, Kw = pairs[0][1].shape[1:][::1 if tr else -1]
    g = _group(c, width_pref)
    W = g * c
    tm, tn = _tile(M, tm_pref, 8), _tile(Kw, tn_pref)
    nk = N_DEV // g
    n_p = len(pairs)

    def body(*refs):
        o_ref, acc = refs[2 * n_p], refs[2 * n_p + 1]
        k = pl.program_id(2)
        tot = None
        for n in range(n_p):
            d_ref, g_ref = refs[2 * n], refs[2 * n + 1]
            for s in range(g):
                part = lax.dot_general(d_ref[:, s * c:(s + 1) * c], g_ref[s], NN if tr else NT, preferred_element_type=f32)
                tot = part if tot is None else tot + part

        @pl.when(k == 0)
        def _():
            acc[...] = tot

        @pl.when(k > 0)
        def _():
            acc[...] += tot

        @pl.when(k == nk - 1)
        def _():
            o_ref[...] = acc[...].astype(o_ref.dtype)

    in_specs, operands = [], []
    for d, gm in pairs:
        if tr:
            wspec, wview = pl.BlockSpec((None, g, c, tn), lambda i, j, k: (k, 0, 0, j)), gm.reshape(nk, g, c, Kw)
        else:
            wspec, wview = pl.BlockSpec((None, g, tn, c), lambda i, j, k: (k, 0, j, 0)), gm.reshape(nk, g, Kw, c)
        in_specs += [pl.BlockSpec((tm, W), lambda i, j, k: (i, k)), wspec]
        operands += [d, wview]
    res, ex = _call(
        body, name=name, out_shape=[jax.ShapeDtypeStruct((M, Kw), out_dtype)], grid=(M // tm, Kw // tn, nk),
        in_specs=in_specs, out_specs=[pl.BlockSpec((tm, tn), lambda i, j, k: (i, j))],
        scratch=[pltpu.VMEM((tm, tn), f32)], operands=operands, comm=comm)
    return res[0] if comm is None else (res[0], ex)


def _mm_tn_cols(name, x, ds, c, tm_pref=1024, tk_pref=1024, width_pref=1024, tr=False, comm=None):
    S, Kw = x.shape
    g = _group(c, width_pref)
    W = g * c
    tm, tk = _tile(Kw, tm_pref), _tile(S, tk_pref, 16)
    nk = S // tk
    n_d = len(ds)
    blk = (g, c, tm) if tr else (g, tm, c)
    full = (N_DEV // g, g, c, Kw) if tr else (N_DEV // g, g, Kw, c)

    def body(*refs):
        x_ref = refs[0]
        d_refs = refs[1:1 + n_d]
        o_refs = refs[1 + n_d:1 + 2 * n_d]
        accs = refs[1 + 2 * n_d:]
        k = pl.program_id(2)

        @pl.when(k == 0)
        def _():
            for acc in accs:
                acc[...] = jnp.zeros_like(acc)

        xv = x_ref[...].astype(bf16)
        for d_ref, acc in zip(d_refs, accs):
            for s in range(g):
                ds_ = d_ref[:, s * c:(s + 1) * c]
                acc[s] += lax.dot_general(ds_, xv, TN, preferred_element_type=f32) if tr else \
                    lax.dot_general(xv, ds_, TN, preferred_element_type=f32)

        @pl.when(k == nk - 1)
        def _():
            for o_ref, acc in zip(o_refs, accs):
                o_ref[...] = acc[...].astype(o_ref.dtype)

    out_map = (lambda i, j, k: (j, 0, 0, i)) if tr else (lambda i, j, k: (j, 0, i, 0))
    outs, ex = _call(
        body, name=name, out_shape=[jax.ShapeDtypeStruct(full, bf16)] * n_d, grid=(Kw // tm, N_DEV // g, nk),
        in_specs=[pl.BlockSpec((tk, tm), lambda i, j, k: (k, i))] + [pl.BlockSpec((tk, W), lambda i, j, k: (k, j))] * n_d,
        out_specs=[pl.BlockSpec((None,) + blk, out_map)] * n_d,
        scratch=[pltpu.VMEM(blk, f32)] * n_d, operands=[x, *ds], comm=comm)
    outs = [o.reshape((N_DEV,) + full[2:]) for o in outs]
    return outs if comm is None else (outs, ex)


def _norm_fwd(name, x, g):
    S, D = x.shape
    tm = _tile(S, 512, 8)

    def body(x_ref, g_ref, h_ref):
        xv = x_ref[...]
        r = lax.rsqrt(jnp.mean(xv * xv, axis=-1, keepdims=True) + EPS)
        h_ref[...] = (xv * r * g_ref[...]).astype(bf16)

    return pl.pallas_call(
        body, name=name, out_shape=jax.ShapeDtypeStruct((S, D), bf16), grid=(S // tm,),
        in_specs=[pl.BlockSpec((tm, D), lambda i: (i, 0)), pl.BlockSpec((1, D), lambda i: (0, 0))],
        out_specs=pl.BlockSpec((tm, D), lambda i: (i, 0)), compiler_params=_cp("parallel"),
    )(x, g.reshape(1, D))


def _norm_bwd(name, dh, x, g, dx_in):
    S, D = x.shape
    tm = _tile(S, 256, 8)

    def body(dh_ref, x_ref, g_ref, dxi_ref, dx_ref, dxb_ref, dg_ref):
        i = pl.program_id(0)
        xv = x_ref[...]
        r = lax.rsqrt(jnp.mean(xv * xv, axis=-1, keepdims=True) + EPS)
        xh = xv * r
        dhv = dh_ref[...].astype(f32)
        dxh = dhv * g_ref[...]
        dx = dxi_ref[...] + r * (dxh - xh * jnp.mean(dxh * xh, axis=-1, keepdims=True))
        dx_ref[...] = dx
        dxb_ref[...] = dx.astype(bf16)

        @pl.when(i == 0)
        def _():
            dg_ref[...] = jnp.zeros_like(dg_ref)

        dg_ref[...] += jnp.sum(dhv * xh, axis=0, keepdims=True)

    row = pl.BlockSpec((tm, D), lambda i: (i, 0))
    vec = pl.BlockSpec((1, D), lambda i: (0, 0))
    return pl.pallas_call(
        body, name=name,
        out_shape=[jax.ShapeDtypeStruct((S, D), f32), jax.ShapeDtypeStruct((S, D), bf16), jax.ShapeDtypeStruct((1, D), f32)],
        grid=(S // tm,), in_specs=[row, row, vec, row], out_specs=[row, row, vec], compiler_params=_cp("arbitrary"),
    )(dh, x, g.reshape(1, D), dx_in)


def _loss_head(x, g, t):
    S, D = x.shape
    tm = _tile(S, 256, 8)

    def body(x_ref, g_ref, t_ref, dx_ref, dg_ref, loss_ref):
        i = pl.program_id(0)
        xv = x_ref[...]
        r = lax.rsqrt(jnp.mean(xv * xv, axis=-1, keepdims=True) + EPS)
        xh = xv * r
        gv = g_ref[...]
        err = xh * gv - t_ref[...]
        dy = err * (1.0 / D)
        dxh = dy * gv
        dx_ref[...] = r * (dxh - xh * jnp.mean(dxh * xh, axis=-1, keepdims=True))

        @pl.when(i == 0)
        def _():
            dg_ref[...] = jnp.zeros_like(dg_ref)
            loss_ref[...] = jnp.zeros_like(loss_ref)

        dg_ref[...] += jnp.sum(dy * xh, axis=0, keepdims=True)
        row = jnp.sum(err * err, axis=-1, keepdims=True) * (0.5 / D)
        loss_ref[...] += jnp.broadcast_to(jnp.sum(row, axis=0, keepdims=True), loss_ref.shape)

    return pl.pallas_call(
        body, name="loss_head",
        out_shape=[jax.ShapeDtypeStruct((S, D), f32), jax.ShapeDtypeStruct((1, D), f32), jax.ShapeDtypeStruct((8, LANES), f32)],
        grid=(S // tm,),
        in_specs=[pl.BlockSpec((tm, D), lambda i: (i, 0)), pl.BlockSpec((1, D), lambda i: (0, 0)), pl.BlockSpec((tm, D), lambda i: (i, 0))],
        out_specs=[pl.BlockSpec((tm, D), lambda i: (i, 0)), pl.BlockSpec((1, D), lambda i: (0, 0)), pl.BlockSpec((8, LANES), lambda i: (0, 0))],
        compiler_params=_cp("arbitrary"),
    )(x, g.reshape(1, D), t)


def _perm(t, d):
    if d == 1:
        return t
    S, C = t.shape
    return t.reshape(S // d, d, C).transpose(1, 0, 2).reshape(S, C)


def _rope_tables(S):
    half = HEAD_DIM // 2
    pos = jnp.arange(S, dtype=f32)
    inv_freq = ROPE_THETA ** (-jnp.arange(0, HEAD_DIM, 2, dtype=f32) / HEAD_DIM)
    ang = pos[:, None] * inv_freq[None, :]
    c, s = jnp.cos(ang), jnp.sin(ang)
    cos2 = jnp.concatenate([c, c], axis=-1)
    sin2 = jnp.concatenate([-s, s], axis=-1)
    assert cos2.shape == (S, 2 * half)
    return (jnp.stack([_perm(cos2, d) for d in DILATIONS]), jnp.stack([_perm(sin2, d) for d in DILATIONS]))


def _rope(t, c, s):
    return t * c + pltpu.roll(t, HEAD_DIM // 2, 1) * s


def _rope_bwd(dt, c, s):
    return dt * c - pltpu.roll(dt, HEAD_DIM // 2, 1) * s


def _band_bounds(i, nblk):
    g = pl.program_id(0)
    lb = jnp.right_shift(jnp.int32(nblk), 2 * g)
    pos = lax.rem(i, lb)
    lo = jnp.where(pos == 0, BLK, 0)
    hi = jnp.where(pos == lb - 1, 2 * BLK, 3 * BLK)
    return lo, hi


def _cur_spec(width, t=None):
    if t is None:
        return pl.BlockSpec((None, BLK, width), lambda g, i: (g, i, 0))
    return pl.BlockSpec((None, None, BLK, width), lambda g, i: (t, g, i, 0))


def _band_specs(width, nblk, t=None):
    lo, hi = (lambda i: jnp.maximum(i - 1, 0)), (lambda i: jnp.minimum(i + 1, nblk - 1))
    if t is None:
        return [pl.BlockSpec((None, BLK, width), lambda g, i: (g, lo(i), 0)), _cur_spec(width),
                pl.BlockSpec((None, BLK, width), lambda g, i: (g, hi(i), 0))]
    return [pl.BlockSpec((None, None, BLK, width), lambda g, i: (t, g, lo(i), 0)), _cur_spec(width, t),
            pl.BlockSpec((None, None, BLK, width), lambda g, i: (t, g, hi(i), 0))]


_ROWS = 2048


def _to_scratch(scr, val):
    for h in range(HEADS):
        scr[h] = val[:, h * HEAD_DIM:(h + 1) * HEAD_DIM].astype(f32)


def _qkv_groups(z):
    S = z.shape[0]
    R = min(_ROWS, S)
    nb = S // R

    def body(x_ref, o_ref, scr):
        g, i = pl.program_id(1), pl.program_id(2)
        _to_scratch(scr, x_ref[...])
        for gi, d in enumerate(DILATIONS):
            @pl.when(g == gi)
            def _():
                n, L = R // d, S // d
                for r in range(d):
                    start = pl.multiple_of(r * L + i * n, 16)
                    for h in range(HEADS):
                        o_ref[pl.ds(start, n), h * HEAD_DIM:(h + 1) * HEAD_DIM] = scr[h, pl.ds(r, n, stride=d), :].astype(bf16)

    return pl.pallas_call(
        body, name="qkv_groups", out_shape=jax.ShapeDtypeStruct((3, N_GROUPS, S, ATTN_W), bf16), grid=(3, N_GROUPS, nb),
        in_specs=[pl.BlockSpec((R, ATTN_W), lambda t, g, i: (i, t * N_GROUPS + g))],
        out_specs=pl.BlockSpec((None, None, S, ATTN_W), lambda t, g, i: (t, g, 0, 0)),
        scratch_shapes=[pltpu.VMEM((HEADS, R, HEAD_DIM), f32)], compiler_params=_cp("arbitrary", "arbitrary", "arbitrary"),
    )(z)


def _bwd_groups(attn, dattn, lse):
    S = attn.shape[0]
    R = min(_ROWS, S)
    nb = S // R

    def body(a_ref, d_ref, l_ref, o_ref, scr):
        t, g, i = pl.program_id(0), pl.program_id(1), pl.program_id(2)

        @pl.when(t == 0)
        def _():
            _to_scratch(scr, d_ref[...])

        @pl.when(t == 1)
        def _():
            _to_scratch(scr, l_ref[...])

        @pl.when(t == 2)
        def _():
            prod = a_ref[...].astype(f32) * d_ref[...].astype(f32)
            for h in range(HEADS):
                part = jnp.sum(prod[:, h * HEAD_DIM:(h + 1) * HEAD_DIM], axis=-1, keepdims=True)
                scr[h] = jnp.broadcast_to(part, (R, HEAD_DIM))

        for gi, d in enumerate(DILATIONS):
            @pl.when(g == gi)
            def _():
                n, L = R // d, S // d
                for r in range(d):
                    start = pl.multiple_of(r * L + i * n, 8)
                    for h in range(HEADS):
                        o_ref[pl.ds(start, n), h * HEAD_DIM:(h + 1) * HEAD_DIM] = scr[h, pl.ds(r, n, stride=d), :]

    def nat(used):
        return pl.BlockSpec((R, ATTN_W), lambda t, g, i: (jnp.where(used(t), i, 0), 0))

    return pl.pallas_call(
        body, name="bwd_groups", out_shape=jax.ShapeDtypeStruct((3, N_GROUPS, S, ATTN_W), f32), grid=(3, N_GROUPS, nb),
        in_specs=[nat(lambda t: t == 2), nat(lambda t: t != 1), nat(lambda t: t == 1)],
        out_specs=pl.BlockSpec((None, None, S, ATTN_W), lambda t, g, i: (t, g, 0, 0)),
        scratch_shapes=[pltpu.VMEM((HEADS, R, HEAD_DIM), f32)], compiler_params=_cp("arbitrary", "arbitrary", "arbitrary"),
    )(attn, dattn, lse)


def _group_views(t3, R):
    S = t3.shape[1]
    views = [t3.reshape(N_GROUPS, d, S // d, ATTN_W) for d in DILATIONS]
    specs = [pl.BlockSpec((None, d, R // d, ATTN_W), functools.partial(lambda i, g: (g, 0, i, 0), g=g))
             for g, d in enumerate(DILATIONS)]
    return views, specs


def _from_groups(scr, ref, d):
    n = ref.shape[1]
    for r in range(d):
        blk = ref[r]
        for h in range(HEADS):
            scr[h, pl.ds(r, n, stride=d), :] = blk[:, h * HEAD_DIM:(h + 1) * HEAD_DIM].astype(f32)


def _to_natural(t3):
    S = t3.shape[1]
    R = min(_ROWS // 2, S)
    views, specs = _group_views(t3, R)

    def body(v0, v1, v2, o_ref, scr):
        for g, (ref, d) in enumerate(zip((v0, v1, v2), DILATIONS)):
            _from_groups(scr, ref, d)
            for h in range(HEADS):
                o_ref[:, g * ATTN_W + h * HEAD_DIM:g * ATTN_W + (h + 1) * HEAD_DIM] = scr[h].astype(bf16)

    return pl.pallas_call(
        body, name="to_natural", out_shape=jax.ShapeDtypeStruct((S, N_GROUPS * ATTN_W), bf16), grid=(S // R,),
        in_specs=specs, out_specs=pl.BlockSpec((R, N_GROUPS * ATTN_W), lambda i: (i, 0)),
        scratch_shapes=[pltpu.VMEM((HEADS, R, HEAD_DIM), f32)], compiler_params=_cp("arbitrary"),
    )(*views)


_SCALE = HEAD_DIM ** -0.5


def _attn_fwd(qkv, cos, sin, comm=None):
    _, _, S, W = qkv.shape
    nblk = S // BLK

    def body(q_ref, kp, kc, kn, vp, vc, vn, cq, sq, ckp, ckc, ckn, skp, skc, skn, o_ref, lse_ref):
        i = pl.program_id(1)
        lo, hi = _band_bounds(i, nblk)
        a = lax.broadcasted_iota(jnp.int32, (BLK, 3 * BLK), 0)
        b = lax.broadcasted_iota(jnp.int32, (BLK, 3 * BLK), 1)
        mask = (jnp.abs(b - BLK - a) <= RADIUS) & (b >= lo) & (b < hi)
        ck = jnp.concatenate([ckp[...], ckc[...], ckn[...]], axis=0)
        sk = jnp.concatenate([skp[...], skc[...], skn[...]], axis=0)
        for hh in range(HEADS):
            sl = slice(hh * HEAD_DIM, (hh + 1) * HEAD_DIM)
            qh = _rope(q_ref[:, sl].astype(f32), cq[...], sq[...]).astype(bf16)
            kh = jnp.concatenate([kp[:, sl], kc[:, sl], kn[:, sl]], axis=0).astype(f32)
            kh = _rope(kh, ck, sk).astype(bf16)
            vh = jnp.concatenate([vp[:, sl], vc[:, sl], vn[:, sl]], axis=0)
            s = lax.dot_general(qh, kh, NT, preferred_element_type=f32) * _SCALE
            s = jnp.where(mask, s, NEG)
            m = jnp.max(s, axis=-1, keepdims=True)
            e = jnp.exp(s - m)
            den = jnp.sum(e, axis=-1, keepdims=True)
            o = lax.dot_general(e.astype(bf16), vh, NN, preferred_element_type=f32) * (1.0 / den)
            o_ref[:, sl] = o.astype(bf16)
            lse_ref[:, sl] = jnp.broadcast_to(m + jnp.log(den), (BLK, HEAD_DIM))

    blk = _cur_spec(W)
    tab = _cur_spec(HEAD_DIM)
    res, ex = _call(
        body, name="attn_fwd",
        out_shape=[jax.ShapeDtypeStruct((N_GROUPS, S, W), bf16), jax.ShapeDtypeStruct((N_GROUPS, S, W), f32)],
        grid=(N_GROUPS, nblk),
        in_specs=[_cur_spec(W, 0)] + _band_specs(W, nblk, 1) + _band_specs(W, nblk, 2) + [tab, tab]
        + _band_specs(HEAD_DIM, nblk) * 2,
        out_specs=[blk, blk], operands=[qkv] * 7 + [cos, sin, cos, cos, cos, sin, sin, sin], comm=comm)
    return res if comm is None else (res, ex)


def _attn_combine(o3, lse3):
    _, S, W = o3.shape
    R = min(_ROWS // 2, S)
    o_views, specs = _group_views(o3, R)
    l_views, _ = _group_views(lse3, R)

    def body(o0, o1, o2, l0, l1, l2, attn_ref, lse_ref, so0, so1, so2, sl0, sl1, sl2):
        for ref, scr, d in zip((o0, o1, o2, l0, l1, l2), (so0, so1, so2, sl0, sl1, sl2), DILATIONS * 2):
            _from_groups(scr, ref, d)
        for h in range(HEADS):
            a0, a1, a2 = sl0[h], sl1[h], sl2[h]
            m = jnp.maximum(jnp.maximum(a0, a1), a2)
            w0, w1, w2 = jnp.exp(a0 - m), jnp.exp(a1 - m), jnp.exp(a2 - m)
            den = w0 + w1 + w2
            acc = w0 * so0[h] + w1 * so1[h] + w2 * so2[h]
            attn_ref[:, h * HEAD_DIM:(h + 1) * HEAD_DIM] = (acc * (1.0 / den)).astype(bf16)
            lse_ref[:, h * HEAD_DIM:(h + 1) * HEAD_DIM] = m + jnp.log(den)

    nat = pl.BlockSpec((R, W), lambda i: (i, 0))
    return pl.pallas_call(
        body, name="attn_combine", out_shape=[jax.ShapeDtypeStruct((S, W), bf16), jax.ShapeDtypeStruct((S, W), f32)],
        grid=(S // R,), in_specs=specs * 2, out_specs=[nat, nat],
        scratch_shapes=[pltpu.VMEM((HEADS, R, HEAD_DIM), f32)] * 6, compiler_params=_cp("arbitrary"),
    )(*o_views, *l_views)


def _attn_bwd_dq(qkv, cos, sin, b3):
    _, _, S, W = qkv.shape
    nblk = S // BLK

    def body(q_ref, kp, kc, kn, vp, vc, vn, cq, sq, ckp, ckc, ckn, skp, skc, skn, da_ref, l_ref, dl_ref, dq_ref):
        i = pl.program_id(1)
        lo, hi = _band_bounds(i, nblk)
        a = lax.broadcasted_iota(jnp.int32, (BLK, 3 * BLK), 0)
        b = lax.broadcasted_iota(jnp.int32, (BLK, 3 * BLK), 1)
        mask = (jnp.abs(b - BLK - a) <= RADIUS) & (b >= lo) & (b < hi)
        ck = jnp.concatenate([ckp[...], ckc[...], ckn[...]], axis=0)
        sk = jnp.concatenate([skp[...], skc[...], skn[...]], axis=0)
        for hh in range(HEADS):
            sl = slice(hh * HEAD_DIM, (hh + 1) * HEAD_DIM)
            qh = _rope(q_ref[:, sl].astype(f32), cq[...], sq[...]).astype(bf16)
            kh = jnp.concatenate([kp[:, sl], kc[:, sl], kn[:, sl]], axis=0).astype(f32)
            kh = _rope(kh, ck, sk).astype(bf16)
            vh = jnp.concatenate([vp[:, sl], vc[:, sl], vn[:, sl]], axis=0)
            s = lax.dot_general(qh, kh, NT, preferred_element_type=f32) * _SCALE
            lh = l_ref[:, sl]
            l3 = jnp.concatenate([lh, lh, lh], axis=1)
            p = jnp.exp(jnp.where(mask, s - l3, NEG))
            dp = lax.dot_general(da_ref[:, sl].astype(bf16), vh, NT, preferred_element_type=f32)
            dh = dl_ref[:, sl]
            ds = p * (dp - jnp.concatenate([dh, dh, dh], axis=1))
            dqh = lax.dot_general(ds.astype(bf16), kh, NN, preferred_element_type=f32) * _SCALE
            dq_ref[:, sl] = _rope_bwd(dqh, cq[...], sq[...]).astype(bf16)

    tab = _cur_spec(HEAD_DIM)
    return pl.pallas_call(
        body, name="attn_bwd_dq", out_shape=jax.ShapeDtypeStruct((N_GROUPS, S, W), bf16), grid=(N_GROUPS, nblk),
        in_specs=[_cur_spec(W, 0)] + _band_specs(W, nblk, 1) + _band_specs(W, nblk, 2) + [tab, tab]
        + _band_specs(HEAD_DIM, nblk) * 2 + [_cur_spec(W, 0), _cur_spec(W, 1), _cur_spec(W, 2)],
        out_specs=_cur_spec(W), compiler_params=_cp("parallel", "parallel"),
    )(*[qkv] * 7, cos, sin, cos, cos, cos, sin, sin, sin, b3, b3, b3)


def _attn_bwd_dkv(qkv, cos, sin, b3, comm=None):
    _, _, S, W = qkv.shape
    nblk = S // BLK

    def body(k_ref, v_ref, ck, sk, qp, qc, qn, cqp, cqc, cqn, sqp, sqc, sqn, dap, dac, dan, lp, lc, ln, dlp, dlc, dln,
             dk_ref, dv_ref):
        j = pl.program_id(1)
        lo, hi = _band_bounds(j, nblk)
        a = lax.broadcasted_iota(jnp.int32, (3 * BLK, BLK), 0)
        b = lax.broadcasted_iota(jnp.int32, (3 * BLK, BLK), 1)
        mask = (jnp.abs(b - (a - BLK)) <= RADIUS) & (a >= lo) & (a < hi)
        cq = jnp.concatenate([cqp[...], cqc[...], cqn[...]], axis=0)
        sq = jnp.concatenate([sqp[...], sqc[...], sqn[...]], axis=0)
        for hh in range(HEADS):
            sl = slice(hh * HEAD_DIM, (hh + 1) * HEAD_DIM)
            kh = _rope(k_ref[:, sl].astype(f32), ck[...], sk[...]).astype(bf16)
            vh = v_ref[:, sl]
            qh = jnp.concatenate([qp[:, sl], qc[:, sl], qn[:, sl]], axis=0).astype(f32)
            qh = _rope(qh, cq, sq).astype(bf16)
            dah = jnp.concatenate([dap[:, sl], dac[:, sl], dan[:, sl]], axis=0).astype(bf16)
            lh = jnp.concatenate([lp[:, sl], lc[:, sl], ln[:, sl]], axis=0)
            dlh = jnp.concatenate([dlp[:, sl], dlc[:, sl], dln[:, sl]], axis=0)
            s = lax.dot_general(qh, kh, NT, preferred_element_type=f32) * _SCALE
            p = jnp.exp(jnp.where(mask, s - lh, NEG))
            dv_ref[:, sl] = lax.dot_general(p.astype(bf16), dah, TN, preferred_element_type=f32).astype(bf16)
            dp = lax.dot_general(dah, vh, NT, preferred_element_type=f32)
            ds = p * (dp - dlh)
            dkh = lax.dot_general(ds.astype(bf16), qh, TN, preferred_element_type=f32) * _SCALE
            dk_ref[:, sl] = _rope_bwd(dkh, ck[...], sk[...]).astype(bf16)

    blk, tab, bt = _cur_spec(W), _cur_spec(HEAD_DIM), _band_specs(HEAD_DIM, nblk)
    res, ex = _call(
        body, name="attn_bwd_dkv",
        out_shape=[jax.ShapeDtypeStruct((N_GROUPS, S, W), bf16), jax.ShapeDtypeStruct((N_GROUPS, S, W), bf16)],
        grid=(N_GROUPS, nblk),
        in_specs=[_cur_spec(W, 1), _cur_spec(W, 2), tab, tab] + _band_specs(W, nblk, 0) + bt + bt
        + _band_specs(W, nblk, 0) + _band_specs(W, nblk, 1) + _band_specs(W, nblk, 2), out_specs=[blk, blk],
        operands=[qkv, qkv, cos, sin, qkv, qkv, qkv, cos, cos, cos, sin, sin, sin] + [b3] * 9, comm=comm)
    return res if comm is None else (res, ex)


_SG_ROWS = 512


def _sg_z_specs(tm, half):
    o = QKV_W // half
    return [pl.BlockSpec((tm, half), functools.partial(lambda i, c: (i, c), c=o + n)) for n in range(4)]


def _sg_norm(v, lg, lb):
    gv = _gelu(v)
    mu = jnp.mean(gv, axis=-1, keepdims=True)
    xc = gv - mu
    rstd = lax.rsqrt(jnp.mean(xc * xc, axis=-1, keepdims=True) + EPS)
    xh = xc * rstd
    return xh, rstd, xh * lg + lb


def _sg_fwd(z, w, bb, lg, lb):
    S = z.shape[0]
    tm = _tile(S, _SG_ROWS, SG_CHUNK)
    half = SG_W // 2

    def body(u0, u1, v0, v1, w_ref, bb_ref, lg_ref, lb_ref, o_ref):
        u = jnp.concatenate([u0[...], u1[...]], axis=1).astype(f32)
        v = jnp.concatenate([v0[...], v1[...]], axis=1).astype(f32)
        gu = _gelu(u)
        _, _, vn = _sg_norm(v, lg_ref[...], lb_ref[...])
        vnb = vn.astype(bf16)
        for c in range(tm // SG_CHUNK):
            rs = slice(c * SG_CHUNK, (c + 1) * SG_CHUNK)
            for g in range(SG_GROUPS):
                cs = slice(g * 128, (g + 1) * 128)
                mixed = lax.dot_general(w_ref[g], vnb[rs, cs], NN, preferred_element_type=f32) + bb_ref[g]
                o_ref[rs, cs] = (gu[rs, cs] * mixed).astype(bf16)

    full3 = pl.BlockSpec((SG_GROUPS, 128, 128), lambda i: (0, 0, 0))
    vec = pl.BlockSpec((1, SG_W), lambda i: (0, 0))
    return pl.pallas_call(
        body, name="sg_fwd", out_shape=jax.ShapeDtypeStruct((S, SG_W), bf16), grid=(S // tm,),
        in_specs=_sg_z_specs(tm, half) + [full3, full3, vec, vec],
        out_specs=pl.BlockSpec((tm, SG_W), lambda i: (i, 0)), compiler_params=_cp("parallel"),
    )(z, z, z, z, w, bb, lg.reshape(1, SG_W), lb.reshape(1, SG_W))


def _sg_bwd(dsg, z, w, wt, bb, lg, lb):
    S = z.shape[0]
    tm = _tile(S, _SG_ROWS, SG_CHUNK)
    half = SG_W // 2

    def body(d_ref, u0, u1, v0, v1, w_ref, wt_ref, bb_ref, lg_ref, lb_ref, du_ref, dv_ref, dw_ref, db_ref, dlg_ref, dlb_ref, dvn_scr):
        i = pl.program_id(0)

        @pl.when(i == 0)
        def _():
            dw_ref[...] = jnp.zeros_like(dw_ref)
            db_ref[...] = jnp.zeros_like(db_ref)
            dlg_ref[...] = jnp.zeros_like(dlg_ref)
            dlb_ref[...] = jnp.zeros_like(dlb_ref)

        u = jnp.concatenate([u0[...], u1[...]], axis=1).astype(f32)
        v = jnp.concatenate([v0[...], v1[...]], axis=1).astype(f32)
        gu = _gelu(u)
        dgu = _gelu_grad(u)
        xh, rstd, vn = _sg_norm(v, lg_ref[...], lb_ref[...])
        vnb = vn.astype(bf16)
        dsg_v = d_ref[...].astype(f32)
        for g in range(SG_GROUPS):
            cs = slice(g * 128, (g + 1) * 128)
            dw_g = jnp.zeros((128, 128), f32)
            db_g = jnp.zeros((128, 1), f32)
            for c in range(tm // SG_CHUNK):
                rs = slice(c * SG_CHUNK, (c + 1) * SG_CHUNK)
                ds = dsg_v[rs, cs]
                mixed = lax.dot_general(w_ref[g], vnb[rs, cs], NN, preferred_element_type=f32) + bb_ref[g]
                du_ref[rs, cs] = (ds * mixed * dgu[rs, cs]).astype(bf16)
                dmix = ds * gu[rs, cs]
                dmb = dmix.astype(bf16)
                dw_g = dw_g + lax.dot_general(dmb, vnb[rs, cs], NT, preferred_element_type=f32)
                db_g = db_g + jnp.sum(dmix, axis=-1, keepdims=True)
                dvn_scr[rs, cs] = lax.dot_general(wt_ref[g], dmb, NN, preferred_element_type=f32)
            dw_ref[g] += dw_g
            db_ref[g] += jnp.broadcast_to(db_g, (128, 128))
        dvn = dvn_scr[...]
        dlg_ref[...] += jnp.sum(dvn * xh, axis=0, keepdims=True)
        dlb_ref[...] += jnp.sum(dvn, axis=0, keepdims=True)
        dxh = dvn * lg_ref[...]
        dgv = rstd * (dxh - jnp.mean(dxh, axis=-1, keepdims=True) - xh * jnp.mean(dxh * xh, axis=-1, keepdims=True))
        dv_ref[...] = (dgv * _gelu_grad(v)).astype(bf16)

    full3 = pl.BlockSpec((SG_GROUPS, 128, 128), lambda i: (0, 0, 0))
    vec = pl.BlockSpec((1, SG_W), lambda i: (0, 0))
    row = pl.BlockSpec((tm, SG_W), lambda i: (i, 0))
    return pl.pallas_call(
        body, name="sg_bwd",
        out_shape=[jax.ShapeDtypeStruct((S, SG_W), bf16), jax.ShapeDtypeStruct((S, SG_W), bf16),
                   jax.ShapeDtypeStruct((SG_GROUPS, 128, 128), f32), jax.ShapeDtypeStruct((SG_GROUPS, 128, 128), f32),
                   jax.ShapeDtypeStruct((1, SG_W), f32), jax.ShapeDtypeStruct((1, SG_W), f32)],
        grid=(S // tm,),
        in_specs=[row] + _sg_z_specs(tm, half) + [full3, full3, full3, vec, vec],
        out_specs=[row, row, full3, full3, vec, vec],
        scratch_shapes=[pltpu.VMEM((tm, SG_W), f32)], compiler_params=_cp("arbitrary"),
    )(dsg, z, z, z, z, w, wt, bb, lg.reshape(1, SG_W), lb.reshape(1, SG_W))


def _merge_fwd(z, ya, yb, D):
    S = z.shape[0]
    tm, tc = _tile(S, 512, 8), _tile(D, 512)
    o_a, o_b = (QKV_W + 2 * SG_W) // tc, (QKV_W + 2 * SG_W + D) // tc

    def body(ga_ref, gb_ref, ya_ref, yb_ref, o_ref):
        ga = _sigmoid(ga_ref[...].astype(f32))
        gb = _sigmoid(gb_ref[...].astype(f32))
        o_ref[...] = (ga * ya_ref[...].astype(f32) + gb * yb_ref[...].astype(f32)).astype(bf16)

    blk = pl.BlockSpec((tm, tc), lambda i, j: (i, j))
    return pl.pallas_call(
        body, name="merge_fwd", out_shape=jax.ShapeDtypeStruct((S, D), bf16), grid=(S // tm, D // tc),
        in_specs=[pl.BlockSpec((tm, tc), lambda i, j: (i, o_a + j)), pl.BlockSpec((tm, tc), lambda i, j: (i, o_b + j)), blk, blk],
        out_specs=blk, compiler_params=_cp("parallel", "parallel"),
    )(z, z, ya, yb)


def _ple_bwd_ew(dx, gp, e):
    S, D = dx.shape
    tm, tc = _tile(S, 512, 8), _tile(D, 1024)

    def body(dx_ref, gp_ref, e_ref, dgp_ref, de_ref):
        dxv = dx_ref[...]
        sg = _sigmoid(gp_ref[...].astype(f32))
        dgp_ref[...] = (dxv * e_ref[...].astype(f32) * sg * (1.0 - sg)).astype(bf16)
        de_ref[...] = (dxv * sg).astype(bf16)

    blk = pl.BlockSpec((tm, tc), lambda i, j: (i, j))
    return pl.pallas_call(
        body, name="ple_bwd_ew", out_shape=[jax.ShapeDtypeStruct((S, D), bf16)] * 2, grid=(S // tm, D // tc),
        in_specs=[blk, blk, blk], out_specs=[blk, blk], compiler_params=_cp("parallel", "parallel"),
    )(dx, gp, e)


def _adam_math(w, g, m, v):
    m = ADAM_B1 * m + (1.0 - ADAM_B1) * g
    v = ADAM_B2 * v + (1.0 - ADAM_B2) * (g * g)
    m_hat = m / (1.0 - ADAM_B1 ** ADAM_STEP)
    v_hat = v / (1.0 - ADAM_B2 ** ADAM_STEP)
    delta = -ADAM_LR * (m_hat / (jnp.sqrt(v_hat) + ADAM_EPS) + ADAM_WD * w)
    return delta, m, v


def _small_sum_adamw(gathered, w, m, v):
    _, R, _ = gathered.shape
    tr = _tile(R, 1024, SMALL_ROWS)

    def body(p_ref, w_ref, m_ref, v_ref, g_ref, d_ref, nm_ref, nv_ref):
        g = p_ref[0]
        for n in range(1, N_DEV):
            g = g + p_ref[n]
        d, nm, nv = _adam_math(w_ref[...], g, m_ref[...], v_ref[...])
        g_ref[...] = g
        d_ref[...] = d
        nm_ref[...] = nm
        nv_ref[...] = nv

    blk = pl.BlockSpec((tr, LANES), lambda i: (i, 0))
    return pl.pallas_call(
        body, name="small_sum_adamw", out_shape=[jax.ShapeDtypeStruct((R, LANES), f32)] * 4, grid=(R // tr,),
        in_specs=[pl.BlockSpec((N_DEV, tr, LANES), lambda i: (0, i, 0)), blk, blk, blk], out_specs=[blk] * 4,
        compiler_params=_cp("parallel"),
    )(gathered, w, m, v)


def _all_gather(name, shard, in_vmem=False):
    R, C = shard.shape

    def body(x_ref, out_ref, send_sems, recv_sems, local_sem):
        x, y, c = lax.axis_index("x"), lax.axis_index("y"), lax.axis_index("c")
        me, sibling = (x, y, c), (x, y, 1 - c)
        chips = [(1 - x, y), (x, 1 - y), (1 - x, 1 - y)]

        def rows(px, py, pc):
            return out_ref.at[4 * px + 2 * py + pc]

        def copy(k, block, to, src=None):
            return pltpu.make_async_remote_copy(
                src_ref=rows(*block) if src is None else src, dst_ref=rows(*block),
                send_sem=send_sems.at[k], recv_sem=recv_sems.at[k], device_id=to, device_id_type=MESH)

        mine = pltpu.make_async_copy(x_ref, rows(*me), local_sem)
        mine.start()
        first = [copy(0, me, sibling, src=x_ref)]
        first += [copy(1 + j, me, (*chip, c), src=x_ref) for j, chip in enumerate(chips)]
        for cp in first:
            cp.start()
        passed = [copy(4 + j, (*chip, c), sibling) for j, chip in enumerate(chips)]
        for j, chip in enumerate(chips):
            copy(1 + j, (*chip, c), me).wait_recv()
            passed[j].start()
        copy(0, sibling, me).wait_recv()
        for j, chip in enumerate(chips):
            copy(4 + j, (*chip, 1 - c), me).wait_recv()
        for cp in first + passed:
            cp.wait_send()
        mine.wait()

    space = pl.BlockSpec(memory_space=pltpu.VMEM) if in_vmem else _ANY
    return pl.pallas_call(
        body, name=name, out_shape=jax.ShapeDtypeStruct((N_DEV, R, C), shard.dtype),
        in_specs=[space], out_specs=space,
        scratch_shapes=[pltpu.SemaphoreType.DMA((7,)), pltpu.SemaphoreType.DMA((7,)), pltpu.SemaphoreType.DMA],
        compiler_params=pltpu.CompilerParams(has_side_effects=True, vmem_limit_bytes=VMEM_LIMIT),
    )(shard)


class _Exchange:
    def __init__(self, ins, outs, sems, start, finish):
        self.ins, self.outs, self.sems, self.start, self.finish = list(ins), list(outs), list(sems), start, finish


def _run_exchange(name, ex):
    c_in, c_out = len(ex.ins), len(ex.outs)

    def body(*refs):
        ins, outs, sems = refs[:c_in], refs[c_in:c_in + c_out], refs[c_in + c_out:]
        ex.start(ins, outs, sems)
        ex.finish(ins, outs, sems)

    return pl.pallas_call(
        body, name=name, out_shape=ex.outs, in_specs=[_ANY] * c_in, out_specs=[_ANY] * c_out, scratch_shapes=ex.sems,
        compiler_params=pltpu.CompilerParams(has_side_effects=True, vmem_limit_bytes=VMEM_LIMIT),
    )(*ex.ins)


def _gather_exchange(shards):
    n = len(shards)

    def plan(ins, outs, sems):
        send_sems, recv_sems, local_sems = sems
        x, y, c = lax.axis_index("x"), lax.axis_index("y"), lax.axis_index("c")
        me, sibling = (x, y, c), (x, y, 1 - c)
        chips = [(1 - x, y), (x, 1 - y), (1 - x, 1 - y)]

        def rows(w, px, py, pc):
            return outs[w].at[4 * px + 2 * py + pc]

        def copy(w, k, block, to, src=None):
            return pltpu.make_async_remote_copy(
                src_ref=rows(w, *block) if src is None else src, dst_ref=rows(w, *block),
                send_sem=send_sems.at[w, k], recv_sem=recv_sems.at[w, k], device_id=to, device_id_type=MESH)

        mine = [pltpu.make_async_copy(ins[w], rows(w, *me), local_sems.at[w]) for w in range(n)]
        first = []
        for w in range(n):
            first.append(copy(w, 0, me, sibling, src=ins[w]))
            first += [copy(w, 1 + j, me, (*chip, c), src=ins[w]) for j, chip in enumerate(chips)]
        return c, me, sibling, chips, copy, mine, first

    def start(ins, outs, sems):
        _, _, _, _, _, mine, first = plan(ins, outs, sems)
        for cp in mine + first:
            cp.start()

    def finish(ins, outs, sems):
        c, me, sibling, chips, copy, mine, first = plan(ins, outs, sems)
        passed = []
        for w in range(n):
            for j, chip in enumerate(chips):
                copy(w, 1 + j, (*chip, c), me).wait_recv()
                passed.append(copy(w, 4 + j, (*chip, c), sibling))
                passed[-1].start()
        for w in range(n):
            copy(w, 0, sibling, me).wait_recv()
            for j, chip in enumerate(chips):
                copy(w, 4 + j, (*chip, 1 - c), me).wait_recv()
        for cp in first + passed:
            cp.wait_send()
        for cp in mine:
            cp.wait()

    return _Exchange(
        shards, [jax.ShapeDtypeStruct((N_DEV,) + s.shape, s.dtype) for s in shards],
        [pltpu.SemaphoreType.DMA((n, 7)), pltpu.SemaphoreType.DMA((n, 7)), pltpu.SemaphoreType.DMA((n,))], start, finish)


def _sibling_exchange(gs):
    n = len(gs)

    def copies(ins, outs, sems):
        send_sems, recv_sems = sems
        x, y, c = lax.axis_index("x"), lax.axis_index("y"), lax.axis_index("c")
        return [pltpu.make_async_remote_copy(
            src_ref=ins[w].at[2 * q + (1 - c)], dst_ref=outs[w].at[q], send_sem=send_sems.at[w, q],
            recv_sem=recv_sems.at[w, q], device_id=(x, y, 1 - c), device_id_type=MESH) for w in range(n) for q in range(4)]

    def start(ins, outs, sems):
        for cp in copies(ins, outs, sems):
            cp.start()

    def finish(ins, outs, sems):
        cps = copies(ins, outs, sems)
        for cp in cps:
            cp.wait_recv()
        for cp in cps:
            cp.wait_send()

    return _Exchange(gs, [jax.ShapeDtypeStruct((4,) + g.shape[1:], g.dtype) for g in gs],
                     [pltpu.SemaphoreType.DMA((n, 4)), pltpu.SemaphoreType.DMA((n, 4))], start, finish)


def _rs_chip_sum(name, g8, recv, c_idx):
    _, R, C = g8.shape
    tr = _tile(R, 512, 16)
    g42 = g8.reshape(4, 2, R, C)

    def body(c_ref, a_ref, b_ref, o_ref):
        o_ref[...] = (a_ref[...].astype(f32) + b_ref[...].astype(f32)).astype(o_ref.dtype)

    return pl.pallas_call(
        body, name=name, out_shape=jax.ShapeDtypeStruct((4, R, C), g8.dtype),
        grid_spec=pltpu.PrefetchScalarGridSpec(
            num_scalar_prefetch=1, grid=(4, R // tr),
            in_specs=[pl.BlockSpec((None, None, tr, C), lambda q, r, c_ref: (q, c_ref[0], r, 0)),
                      pl.BlockSpec((None, tr, C), lambda q, r, c_ref: (q, r, 0))],
            out_specs=pl.BlockSpec((None, tr, C), lambda q, r, c_ref: (q, r, 0))),
        compiler_params=_cp("parallel", "parallel"),
    )(c_idx, g42, recv)


def _chips_exchange(p4s):
    n = len(p4s)

    def copies(ins, outs, sems):
        send_sems, recv_sems = sems
        x, y, c = lax.axis_index("x"), lax.axis_index("y"), lax.axis_index("c")
        chips = [(1 - x, y), (x, 1 - y), (1 - x, 1 - y)]
        return [pltpu.make_async_remote_copy(
            src_ref=ins[w].at[2 * cx + cy], dst_ref=outs[w].at[k], send_sem=send_sems.at[w, k],
            recv_sem=recv_sems.at[w, k], device_id=(cx, cy, c), device_id_type=MESH)
            for w in range(n) for k, (cx, cy) in enumerate(chips)]

    def start(ins, outs, sems):
        for cp in copies(ins, outs, sems):
            cp.start()

    def finish(ins, outs, sems):
        cps = copies(ins, outs, sems)
        for cp in cps:
            cp.wait_recv()
        for cp in cps:
            cp.wait_send()

    return _Exchange(p4s, [jax.ShapeDtypeStruct((3,) + p.shape[1:], p.dtype) for p in p4s],
                     [pltpu.SemaphoreType.DMA((n, 3)), pltpu.SemaphoreType.DMA((n, 3))], start, finish)


def _adamw_layer(name, layer, w, m, v, p4, recv, q_idx, prev):
    depth, R, C = w.shape
    tr = _tile(R, 256, 8)

    def body(q_ref, w_ref, m_ref, v_ref, a_ref, b_ref, *rest):
        g_ref, d_ref, nm_ref, nv_ref = rest[-4:]
        g = ((a_ref[...].astype(f32) + b_ref[0].astype(f32)) + b_ref[1].astype(f32)) + b_ref[2].astype(f32)
        d, nm, nv = _adam_math(w_ref[...], g, m_ref[...], v_ref[...])
        g_ref[...] = g
        d_ref[...] = d
        nm_ref[...] = nm
        nv_ref[...] = nv

    lay = pl.BlockSpec((None, tr, C), lambda i, q_ref: (layer, i, 0))
    n_prev = 0 if prev is None else 4
    return pl.pallas_call(
        body, name=name, out_shape=[jax.ShapeDtypeStruct((depth, R, C), f32)] * 4,
        grid_spec=pltpu.PrefetchScalarGridSpec(
            num_scalar_prefetch=1, grid=(R // tr,),
            in_specs=[lay, lay, lay, pl.BlockSpec((None, tr, C), lambda i, q_ref: (q_ref[0], i, 0)),
                      pl.BlockSpec((3, tr, C), lambda i, q_ref: (0, i, 0))] + [_ANY] * n_prev,
            out_specs=[lay] * 4),
        input_output_aliases={6 + n: n for n in range(n_prev)},
        compiler_params=_cp("parallel"),
    )(q_idx, w, m, v, p4, recv, *(prev or ()))


_BIG = (("w_in", 1), ("w_br_attn", 1), ("w_br_sg", 1), ("w_out", 0), ("w_ff_gate", 1), ("w_ff_up", 1),
        ("w_ff_down", 0), ("w_ple_gate", 0), ("w_ple", 1))


_TURNED = ("w_in", "w_ff_gate", "w_ff_up")


def _gather_plan(i, depth):
    mixer = ["w_br_attn", "w_br_sg", "w_out"]
    plan = {
        "mm_in": [(i, "w_ff_gate")] + ([(i, n) for n in mixer] if i == 0 else []),
        "attn_fwd": [(i, "w_ff_up")],
        "mm_ffn_in": [(i, "w_ff_down"), (i, "w_ple_gate"), (i, "w_ple")],
    }
    if i + 1 < depth:
        plan["mm_ffn_out"] = [(i + 1, "w_in")]
        plan["mm_ple"] = [(i + 1, n) for n in mixer]
    return plan


def _layer_fwd(x0, p_i, layer, arrived, sm, tabs, comm):
    S, D = x0.shape
    cos, sin = tabs
    tmm = _tile(S, 1024, 8)
    same = lambda accs, ex: accs

    def W(name):
        return arrived[(layer, name)]

    def hosted(key, fn):
        if key not in comm:
            return fn(None)
        ex, keys = comm[key]
        res, outs = fn(ex)
        arrived.update(zip(keys, outs))
        return res

    h1 = _norm_fwd("norm_fwd", x0, sm["norm_mix"])
    z = hosted("mm_in", lambda ex: _mm_nn_cols("mm_in", h1, [W("w_in")], [bf16], same, tr=True, comm=ex))[0]
    IN = z.shape[1]

    qkv = _qkv_groups(z)
    o3, lse3 = hosted("attn_fwd", lambda ex: _attn_fwd(qkv, cos, sin, comm=ex))
    attn, lse = _attn_combine(o3, lse3)
    ya = _mm_nn_cols("mm_br_attn", attn, [W("w_br_attn")], [bf16], same, tm_pref=1024)[0]
    sgw = sm["sg_w"].astype(bf16)
    bb = jnp.broadcast_to(sm["sg_b"][:, :, None], (SG_GROUPS, SG_CHUNK, 128))
    sg = _sg_fwd(z, sgw, bb, sm["sg_ln_g"], sm["sg_ln_b"])
    yb = _mm_nn_cols("mm_br_sg", sg, [W("w_br_sg")], [bf16], same, tm_pref=1024)[0]
    merged = _merge_fwd(z, ya, yb, D)
    tn = _tile(D, 1024)
    x1 = _matmul("mm_out", [(merged, W("w_out").reshape(D, D), "nn", 0)], S, D, tmm, tn, 1,
                 [((S, D), f32, (tmm, tn), lambda i, j: (i, j))], lambda accs, ex: [ex[0] + accs[0]],
                 extras=[(x0, (tmm, tn), lambda i, j: (i, j))], chunk=512)[0]
    h2 = _norm_fwd("norm_fwd", x1, sm["norm_ffn"])

    def ffn_ep(accs, ex):
        a, b = accs
        sg = _sigmoid(a)
        silu = a * sg
        return [b * sg * (1.0 + a * (1.0 - sg)), silu, silu * b]

    dfa, dfb, f = hosted("mm_ffn_in", lambda ex: _mm_nn_cols("mm_ffn_in", h2, [W("w_ff_gate"), W("w_ff_up")], [bf16] * 3, ffn_ep,
                                                         tr=True, comm=ex))
    w_down = W("w_ff_down").reshape(-1, D)
    F = w_down.shape[0]
    thin = _tile(D, 512)
    x2 = hosted("mm_ffn_out", lambda ex: _matmul(
        "mm_ffn_out", [(f, w_down, "nn", 0)], S, D, tmm, thin, 1, [((S, D), f32, (tmm, thin), lambda i, j: (i, j))],
        lambda accs, ex_tiles: [ex_tiles[0] + accs[0]], extras=[(x1, (tmm, thin), lambda i, j: (i, j))], comm=ex))[0]
    h3 = _norm_fwd("norm_fwd", x2, sm["norm_ple"])

    e = _mm_nn_cols("mm_ple_emb", p_i, [W("w_ple")], [bf16], same, tm_pref=1024)[0]

    def ple_ep(accs, ex):
        gp = accs[0]
        return [ex[0] + _sigmoid(gp) * ex[1].astype(f32), gp]

    x3, gp = hosted("mm_ple", lambda ex: _matmul(
        "mm_ple", [(h3, W("w_ple_gate").reshape(D, D), "nn", 0)], S, D, tmm, tn, 1,
        [((S, D), f32, (tmm, tn), lambda i, j: (i, j)), ((S, D), bf16, (tmm, tn), lambda i, j: (i, j))],
        ple_ep, extras=[(x2, (tmm, tn), lambda i, j: (i, j)), (e, (tmm, tn), lambda i, j: (i, j))], chunk=512, comm=ex))
    saved = dict(x0=x0, h1=h1, z=z, qkv=qkv, attn=attn, lse=lse, ya=ya, yb=yb, sg=sg, merged=merged, x1=x1,
                 h2=h2, dfa=dfa, dfb=dfb, f=f, x2=x2, h3=h3, gp=gp, e=e, sgw=sgw, bb=bb, IN=IN)
    return x3, saved


def _layer_bwd(dx3, p_i, W, sm, tabs, sv, c_idx):
    S, D = dx3.shape
    w_out, w_down, w_pg = W["w_out"].reshape(D, D), W["w_ff_down"].reshape(-1, D), W["w_ple_gate"].reshape(D, D)
    F = w_down.shape[0]
    cos, sin = tabs
    tmm = _tile(S, 1024, 8)
    tn = _tile(D, 512)
    reduced = {}

    def chip_sums(grads):
        names = list(grads)
        recv = _run_exchange("rs_sibling_" + names[0], _sibling_exchange([grads[n] for n in names]))
        p4 = {n: _rs_chip_sum("rs_chip_sum_" + n, grads[n], r, c_idx) for n, r in zip(names, recv)}
        return p4, _chips_exchange([p4[n] for n in names])

    def carry(p4, outs):
        reduced.update({n: (p4[n], r) for n, r in zip(p4, outs)})

    def blocks(full):
        return full.reshape(N_DEV, full.shape[0] // N_DEV, full.shape[1])

    dgp, de = _ple_bwd_ew(dx3, sv["gp"], sv["e"])
    d_w_ple = _mm_tn_cols("mm_dw_ple", p_i, [de], D // N_DEV)[0]
    d_w_pg = blocks(_mm_simple("mm_dw_dd", sv["h3"], dgp, "tn", bf16, 1024, 1024, 2048))
    dh3 = _mm_simple("mm_dh_dd", dgp, w_pg, "nt", bf16, 1024, 1024, 2048)
    dx2, dx2b, dg_ple = _norm_bwd("norm_bwd", dh3, sv["x2"], sm["norm_ple"], dx3)
    tf = _tile(F, 1408)

    def ffn_bwd_ep(accs, ex):
        df = accs[0]
        return [df * ex[0].astype(f32), df * ex[1].astype(f32)]

    th = _tile(S, 512, 8)
    da, db = _matmul("mm_dffn", [(dx2b, w_down, "nt", 0)], S, F, th, tf, 1,
                     [((S, F), bf16, (th, tf), lambda i, j: (i, j))] * 2, ffn_bwd_ep,
                     extras=[(sv["dfa"], (th, tf), lambda i, j: (i, j)), (sv["dfb"], (th, tf), lambda i, j: (i, j))], chunk=512)
    d_w_down = blocks(_mm_simple("mm_dw_fd", sv["f"], dx2b, "tn", bf16, 1408, 1024, 2048))
    p4, ex = chip_sums(dict(w_ff_down=d_w_down, w_ple_gate=d_w_pg, w_ple=d_w_ple))
    (d_w_gate, d_w_up), outs = _mm_tn_cols("mm_dw_df", sv["h2"], [da, db], F // N_DEV, tr=True, comm=ex)
    carry(p4, outs)
    p4, ex = chip_sums(dict(w_ff_gate=d_w_gate, w_ff_up=d_w_up))
    dh2, outs = _mm_nt_cols("mm_dh_ffn", [(da, W["w_ff_gate"]), (db, W["w_ff_up"])], bf16, tr=True, comm=ex)
    carry(p4, outs)
    dx1, dx1b, dg_ffn = _norm_bwd("norm_bwd", dh2, sv["x1"], sm["norm_ffn"], dx2)
    z = sv["z"]
    o_a, o_b = (QKV_W + 2 * SG_W) // tn, (QKV_W + 2 * SG_W + D) // tn

    def merge_bwd_ep(accs, ex):
        dm = accs[0]
        ga, gb = _sigmoid(ex[0].astype(f32)), _sigmoid(ex[1].astype(f32))
        ya, yb = ex[2].astype(f32), ex[3].astype(f32)
        return [dm * ya * ga * (1.0 - ga), dm * yb * gb * (1.0 - gb), dm * ga, dm * gb]

    dga, dgb, dya, dyb = _matmul(
        "mm_dmerge", [(dx1b, w_out, "nt", 0)], S, D, tmm, tn, 1,
        [((S, D), bf16, (tmm, tn), lambda i, j: (i, j))] * 4, merge_bwd_ep,
        extras=[(z, (tmm, tn), lambda i, j: (i, o_a + j)), (z, (tmm, tn), lambda i, j: (i, o_b + j)),
                (sv["ya"], (tmm, tn), lambda i, j: (i, j)), (sv["yb"], (tmm, tn), lambda i, j: (i, j))], chunk=256)
    d_w_out = blocks(_mm_simple("mm_dw_dd", sv["merged"], dx1b, "tn", bf16, 1024, 1024, 2048))
    dsg = _mm_nt_cols("mm_dsg", [(dyb, W["w_br_sg"])], bf16)
    d_w_bsg = _mm_tn_cols("mm_dw_bsg", sv["sg"], [dyb], D // N_DEV)[0]
    dattn = _mm_nt_cols("mm_dattn", [(dya, W["w_br_attn"])], bf16)
    d_w_battn = _mm_tn_cols("mm_dw_battn", sv["attn"], [dya], D // N_DEV)[0]
    sgwt = jnp.swapaxes(sm["sg_w"], 1, 2).astype(bf16)
    du, dv_sg, d_sgw, d_sgb, d_lg, d_lb = _sg_bwd(dsg, z, sv["sgw"], sgwt, sv["bb"], sm["sg_ln_g"], sm["sg_ln_b"])
    b3 = _bwd_groups(sv["attn"], dattn, sv["lse"])
    dqg = _attn_bwd_dq(sv["qkv"], cos, sin, b3)
    p4, ex = chip_sums(dict(w_out=d_w_out, w_br_sg=d_w_bsg, w_br_attn=d_w_battn))
    (dkg, dvg), outs = _attn_bwd_dkv(sv["qkv"], cos, sin, b3, comm=ex)
    carry(p4, outs)
    dz = jnp.concatenate([_to_natural(dqg), _to_natural(dkg), _to_natural(dvg), du, dv_sg, dga, dgb], axis=1)
    d_w_in = _mm_tn_cols("mm_dw_in", sv["h1"], [dz], sv["IN"] // N_DEV, tr=True)[0]
    p4, ex = chip_sums(dict(w_in=d_w_in))
    dh1, outs = _mm_nt_cols("mm_dh_in", [(dz, W["w_in"])], bf16, tr=True, comm=ex)
    carry(p4, outs)
    dx0, _, dg_mix = _norm_bwd("norm_bwd", dh1, sv["x0"], sm["norm_mix"], dx1)
    small = dict(sg_w=d_sgw, sg_b=d_sgb[:, :, 0], sg_ln_g=d_lg[0], sg_ln_b=d_lb[0], norm_mix=dg_mix[0], norm_ffn=dg_ffn[0],
                 norm_ple=dg_ple[0])
    return dx0, reduced, small


_SMALL = ("sg_w", "sg_b", "sg_ln_g", "sg_ln_b", "norm_mix", "norm_ffn", "norm_ple", "norm_final")


SMALL_ROWS = 256


def _pack_small(parts, tail):
    rows = [parts[n].astype(f32).reshape(-1, LANES) for n in _SMALL] + [tail]
    n = sum(r.shape[0] for r in rows)
    return jnp.concatenate(rows + [jnp.zeros((-n % SMALL_ROWS, LANES), f32)], axis=0)


def kernel(x, p, w_in, w_br_attn, w_br_sg, w_out, sg_w, sg_b, sg_ln_g, sg_ln_b, norm_mix, norm_ffn, norm_ple, norm_final, w_ff_gate, w_ff_up, w_ff_down, w_ple_gate, w_ple, loss_target, m_w_in, m_w_br_attn, m_w_br_sg, m_w_out, m_sg_w, m_sg_b, m_sg_ln_g, m_sg_ln_b, m_norm_mix, m_norm_ffn, m_norm_ple, m_norm_final, m_w_ff_gate, m_w_ff_up, m_w_ff_down, m_w_ple_gate, m_w_ple, v_w_in, v_w_br_attn, v_w_br_sg, v_w_out, v_sg_w, v_sg_b, v_sg_ln_g, v_sg_ln_b, v_norm_mix, v_norm_ffn, v_norm_ple, v_norm_final, v_w_ff_gate, v_w_ff_up, v_w_ff_down, v_w_ple_gate, v_w_ple):
    wts = dict(w_in=w_in, w_br_attn=w_br_attn, w_br_sg=w_br_sg, w_out=w_out, w_ff_gate=w_ff_gate, w_ff_up=w_ff_up,
               w_ff_down=w_ff_down, w_ple_gate=w_ple_gate, w_ple=w_ple)
    mom_m = dict(w_in=m_w_in, w_br_attn=m_w_br_attn, w_br_sg=m_w_br_sg, w_out=m_w_out, w_ff_gate=m_w_ff_gate,
                 w_ff_up=m_w_ff_up, w_ff_down=m_w_ff_down, w_ple_gate=m_w_ple_gate, w_ple=m_w_ple)
    mom_v = dict(w_in=v_w_in, w_br_attn=v_w_br_attn, w_br_sg=v_w_br_sg, w_out=v_w_out, w_ff_gate=v_w_ff_gate,
                 w_ff_up=v_w_ff_up, w_ff_down=v_w_ff_down, w_ple_gate=v_w_ple_gate, w_ple=v_w_ple)
    small_w = dict(sg_w=sg_w, sg_b=sg_b, sg_ln_g=sg_ln_g, sg_ln_b=sg_ln_b, norm_mix=norm_mix, norm_ffn=norm_ffn,
                   norm_ple=norm_ple, norm_final=norm_final)
    small_m = dict(sg_w=m_sg_w, sg_b=m_sg_b, sg_ln_g=m_sg_ln_g, sg_ln_b=m_sg_ln_b, norm_mix=m_norm_mix, norm_ffn=m_norm_ffn,
                   norm_ple=m_norm_ple, norm_final=m_norm_final)
    small_v = dict(sg_w=v_sg_w, sg_b=v_sg_b, sg_ln_g=v_sg_ln_g, sg_ln_b=v_sg_ln_b, norm_mix=v_norm_mix, norm_ffn=v_norm_ffn,
                   norm_ple=v_norm_ple, norm_final=v_norm_final)
    depth = w_in.shape[0]
    S = x.shape[1]
    names = [n for n, _ in _BIG]
    c_idx = lax.axis_index("c").astype(jnp.int32).reshape(1)
    q_idx = (2 * lax.axis_index("x") + lax.axis_index("y")).astype(jnp.int32).reshape(1)
    tabs = _rope_tables(S)

    def turned(n, t):
        return jnp.swapaxes(t, -1, -2) if n in _TURNED else t

    def gather(keys):
        return _gather_exchange([turned(n, wts[n][i]).astype(bf16) for i, n in keys]), keys

    ex, keys = gather([(0, "w_in")])
    arrived = dict(zip(keys, _run_exchange("ag_w_in", ex)))

    xs = x[0]
    saved = []
    for i in range(depth):
        sm = {n: small_w[n][i] for n in _SMALL if n != "norm_final"}
        comm = {carrier: gather(keys) for carrier, keys in _gather_plan(i, depth).items()}
        xs, sv = _layer_fwd(xs, p[i, 0], i, arrived, sm, tabs, comm)
        saved.append(sv)
    dx, dg_final, loss_part = _loss_head(xs, norm_final, loss_target[0])

    reduced = [None] * depth
    small_parts = [None] * depth
    for i in reversed(range(depth)):
        sm = {n: small_w[n][i] for n in _SMALL if n != "norm_final"}
        dx, reduced[i], small_parts[i] = _layer_bwd(dx, p[i, 0], {n: arrived[(i, n)] for n in names}, sm, tabs, saved[i], c_idx)
    grad_x = dx[None]

    parts = {n: jnp.stack([small_parts[i][n] for i in range(depth)]) for n in _SMALL if n != "norm_final"}
    parts["norm_final"] = dg_final[0]
    gathered = _all_gather("ag_small", _pack_small(parts, loss_part), in_vmem=True)
    g_s, d_s, nm_s, nv_s = _small_sum_adamw(gathered, _pack_small(small_w, jnp.zeros((8, LANES), f32)),
                                            _pack_small(small_m, jnp.zeros((8, LANES), f32)),
                                            _pack_small(small_v, jnp.ones((8, LANES), f32)))
    loss = g_s[sum(small_w[n].size for n in _SMALL) // LANES, 0]

    def unpack_small(flat):
        out, off = {}, 0
        for n in _SMALL:
            k = small_w[n].size // LANES
            out[n] = flat[off:off + k].reshape(small_w[n].shape)
            off += k
        return out

    sm_g, sm_d, sm_nm, sm_nv = unpack_small(g_s), unpack_small(d_s), unpack_small(nm_s), unpack_small(nv_s)

    big_g, big_d, big_nm, big_nv = {}, {}, {}, {}
    for k, n in enumerate(names):
        outs = None
        for i in range(depth):
            p4, recv2 = reduced[i][n]
            outs = _adamw_layer(f"adamw_{n}_{i}", i, turned(n, wts[n]), turned(n, mom_m[n]), turned(n, mom_v[n]), p4, recv2,
                                q_idx, outs)
        big_g[n], big_d[n], big_nm[n], big_nv[n] = [turned(n, o) for o in outs]

    order = ["w_in", "w_br_attn", "w_br_sg", "w_out", "sg_w", "sg_b", "sg_ln_g", "sg_ln_b", "norm_mix", "norm_ffn", "norm_ple",
             "norm_final", "w_ff_gate", "w_ff_up", "w_ff_down", "w_ple_gate", "w_ple"]

    def pick(big, small):
        return [big[n] if n in big else small[n] for n in order]

    return (loss, grad_x, *pick(big_g, sm_g), *pick(big_d, sm_d), *pick(big_nm, sm_nm), *pick(big_nv, sm_nv))
```

```python
import functools
import math

import jax
import jax.numpy as jnp
from jax import lax
from jax.experimental import pallas as pl
from jax.experimental.pallas import tpu as pltpu

f32 = jnp.float32
bf16 = jnp.bfloat16

HEAD_DIM = 128
N_GROUPS = 3
HEADS = 4
DILATIONS = (1, 4, 16)
RADIUS = 64
BLK = 128
QKV_W = 3 * N_GROUPS * HEADS * HEAD_DIM
ATTN_W = HEADS * HEAD_DIM
SG_CHUNK = 128
SG_GROUPS = 8
SG_W = SG_GROUPS * 128
ROPE_THETA = 10000.0
EPS = 1e-6
NEG = -1e30
N_DEV = 8
LANES = 128

ADAM_LR = 0.001
ADAM_B1 = 0.9
ADAM_B2 = 0.999
ADAM_EPS = 1e-08
ADAM_WD = 0.01
ADAM_STEP = 10

VMEM_LIMIT = 56 * 1024 * 1024
MESH = pl.DeviceIdType.MESH

NN = (((1,), (0,)), ((), ()))
NT = (((1,), (1,)), ((), ()))
TN = (((0,), (0,)), ((), ()))
_DN = {"nn": NN, "nt": NT, "tn": TN}


def _cp(*sem):
    return pltpu.CompilerParams(dimension_semantics=sem, vmem_limit_bytes=VMEM_LIMIT)


def _tile(n, pref, unit=128):
    if n <= pref:
        return n
    t = (pref // unit) * unit
    while t >= unit:
        if n % t == 0:
            return t
        t -= unit
    return n


_ANY = pl.BlockSpec(memory_space=pl.ANY)


def _call(body, *, name, grid, in_specs, out_specs, out_shape, operands, scratch=(), comm=None):
    in_specs, out_specs, out_shape, scratch = list(in_specs), list(out_specs), list(out_shape), list(scratch)
    if comm is None:
        res = pl.pallas_call(
            body, name=name, out_shape=out_shape, grid=grid, in_specs=in_specs, out_specs=out_specs, scratch_shapes=scratch,
            compiler_params=_cp(*(("arbitrary",) * len(grid))))(*operands)
        return res, []
    n_in, n_out, n_scr = len(in_specs), len(out_specs), len(scratch)
    c_in, c_out = len(comm.ins), len(comm.outs)

    def hosted(*refs):
        own_in, refs = refs[:n_in], refs[n_in:]
        ex_in, refs = refs[:c_in], refs[c_in:]
        own_out, refs = refs[:n_out], refs[n_out:]
        ex_out, refs = refs[:c_out], refs[c_out:]
        own_scr, sems = refs[:n_scr], refs[n_scr:]
        ids = [pl.program_id(a) for a in range(len(grid))]
        first = functools.reduce(jnp.logical_and, [i == 0 for i in ids])
        last = functools.reduce(jnp.logical_and, [i == g - 1 for i, g in zip(ids, grid)])

        @pl.when(first)
        def _():
            comm.start(ex_in, ex_out, sems)

        body(*own_in, *own_out, *own_scr)

        @pl.when(last)
        def _():
            comm.finish(ex_in, ex_out, sems)

    res = pl.pallas_call(
        hosted, name=name, out_shape=out_shape + list(comm.outs), grid=grid,
        in_specs=in_specs + [_ANY] * c_in, out_specs=out_specs + [_ANY] * c_out, scratch_shapes=scratch + list(comm.sems),
        input_output_aliases={n_in + a: n_out + b for a, b in comm.aliases.items()},
        compiler_params=pltpu.CompilerParams(dimension_semantics=("arbitrary",) * len(grid), vmem_limit_bytes=VMEM_LIMIT,
                                             has_side_effects=True),
    )(*operands, *comm.ins)
    return res[:n_out], res[n_out:]


def _sigmoid(x):
    return 1.0 / (1.0 + jnp.exp(-x))


_GC = math.sqrt(2.0 / math.pi)
_GA = 0.044715


def _gelu(x):
    return 0.5 * x * (1.0 + jnp.tanh(_GC * (x + _GA * x * x * x)))


def _gelu_grad(x):
    t = jnp.tanh(_GC * (x + _GA * x * x * x))
    return 0.5 * (1.0 + t) + 0.5 * x * (1.0 - t * t) * _GC * (1.0 + 3.0 * _GA * x * x)


def _matmul(name, prods, M, N, tm, tn, nk, outs, epilogue, extras=(), n_acc=1, chunk=None, comm=None):
    in_specs, operands, metas = [], [], []
    for a, b, mode, acc in prods:
        if mode == "tn":
            tk = a.shape[0] // nk
            in_specs += [pl.BlockSpec((tk, tm), lambda i, j, k: (k, i)), pl.BlockSpec((tk, tn), lambda i, j, k: (k, j))]
        elif mode == "nt":
            tk = a.shape[1] // nk
            in_specs += [pl.BlockSpec((tm, tk), lambda i, j, k: (i, k)), pl.BlockSpec((tn, tk), lambda i, j, k: (j, k))]
        else:
            tk = a.shape[1] // nk
            in_specs += [pl.BlockSpec((tm, tk), lambda i, j, k: (i, k)), pl.BlockSpec((tk, tn), lambda i, j, k: (k, j))]
        operands += [a, b]
        metas.append((mode, acc))
    for arr, bshape, imap in extras:
        in_specs.append(pl.BlockSpec(bshape, functools.partial(lambda i, j, k, f: f(i, j), f=imap)))
        operands.append(arr)
    out_specs = [pl.BlockSpec(bs, functools.partial(lambda i, j, k, f: f(i, j), f=imap)) for _, _, bs, imap in outs]
    out_shape = [jax.ShapeDtypeStruct(s, d) for s, d, _, _ in outs]
    n_prod, n_ext, n_out = len(prods), len(extras), len(outs)

    def body(*refs):
        in_refs = refs[: 2 * n_prod]
        ex_refs = refs[2 * n_prod : 2 * n_prod + n_ext]
        out_refs = refs[2 * n_prod + n_ext : 2 * n_prod + n_ext + n_out]
        acc_refs = refs[2 * n_prod + n_ext + n_out :]

        def partials():
            res = [None] * n_acc
            for idx, (mode, acc) in enumerate(metas):
                a = in_refs[2 * idx][...].astype(bf16)
                b = in_refs[2 * idx + 1][...].astype(bf16)
                d = lax.dot_general(a, b, _DN[mode], preferred_element_type=f32)
                res[acc] = d if res[acc] is None else res[acc] + d
            return res

        def finish(accs):
            vals = epilogue(accs, [r[...] for r in ex_refs])
            for r, v in zip(out_refs, vals):
                r[...] = v.astype(r.dtype)

        if nk == 1:
            step = chunk or tn
            for c0 in range(0, tn, step):
                c1 = min(c0 + step, tn)
                res = [None] * n_acc
                for idx, (mode, acc) in enumerate(metas):
                    a = in_refs[2 * idx][...].astype(bf16)
                    b_ref = in_refs[2 * idx + 1]
                    b = (b_ref[c0:c1, :] if mode == "nt" else b_ref[:, c0:c1]).astype(bf16)
                    d = lax.dot_general(a, b, _DN[mode], preferred_element_type=f32)
                    res[acc] = d if res[acc] is None else res[acc] + d
                for r, v in zip(out_refs, epilogue(res, [r[:, c0:c1] for r in ex_refs])):
                    r[:, c0:c1] = v.astype(r.dtype)
        else:
            k = pl.program_id(2)
            parts = partials()

            @pl.when(k == 0)
            def _():
                for r, d in zip(acc_refs, parts):
                    r[...] = d

            @pl.when(k > 0)
            def _():
                for r, d in zip(acc_refs, parts):
                    r[...] += d

            @pl.when(k == nk - 1)
            def _():
                finish([r[...] for r in acc_refs])

    scratch = [pltpu.VMEM((tm, tn), f32) for _ in range(n_acc)] if nk > 1 else []
    res, ex = _call(body, name=name, grid=(M // tm, N // tn, nk), in_specs=in_specs, out_specs=out_specs,
                    out_shape=out_shape, operands=operands, scratch=scratch, comm=comm)
    return res if comm is None else (res, ex)


def _ident(accs, ex):
    return [accs[0]]


def _mm_simple(name, a, b, mode, out_dtype, tm_pref=1024, tn_pref=1024, tk_pref=1024, comm=None):
    if mode == "tn":
        K, M = a.shape
        N = b.shape[1]
    elif mode == "nt":
        M, K = a.shape
        N = b.shape[0]
    else:
        M, K = a.shape
        N = b.shape[1]
    tm, tn, tk = _tile(M, tm_pref), _tile(N, tn_pref), _tile(K, tk_pref)
    res = _matmul(name, [(a, b, mode, 0)], M, N, tm, tn, K // tk,
                  [((M, N), out_dtype, (tm, tn), lambda i, j: (i, j))], _ident, comm=comm)
    return res[0] if comm is None else (res[0][0], res[1])


def _group(c, width_pref=1024):
    g = LANES // math.gcd(c, LANES)
    while g < N_DEV and 2 * g * c <= width_pref:
        g *= 2
    return g


def _join(parts):
    return parts[0] if len(parts) == 1 else jnp.concatenate(parts, axis=1)


def _mm_nn_cols(name, a, gs_list, outs_dtypes, epilogue, tm_pref=512, width_pref=1024, tr=False, comm=None):
    M, K = a.shape
    c = gs_list[0].shape[1 if tr else 2]
    g = _group(c, width_pref)
    W = g * c
    tm = _tile(M, tm_pref, 8)
    n_g, n_out = len(gs_list), len(outs_dtypes)
    blk = (g, c, K) if tr else (g, K, c)

    def body(*refs):
        a_ref = refs[0]
        g_refs = refs[1:1 + n_g]
        out_refs = refs[1 + n_g:]
        av = a_ref[...].astype(bf16)
        cols = [epilogue([lax.dot_general(av, gr[s], NT if tr else NN, preferred_element_type=f32) for gr in g_refs], [])
                for s in range(g)]
        for n, r in enumerate(out_refs):
            r[...] = _join([cols[s][n].astype(r.dtype) for s in range(g)])

    tile = pl.BlockSpec((tm, W), lambda j, i: (i, j))
    res, ex = _call(
        body, name=name, out_shape=[jax.ShapeDtypeStruct((M, N_DEV * c), d) for d in outs_dtypes],
        grid=(N_DEV // g, M // tm),
        in_specs=[pl.BlockSpec((tm, K), lambda j, i: (i, 0))] + [pl.BlockSpec((None,) + blk, lambda j, i: (j, 0, 0, 0))] * n_g,
        out_specs=[tile] * n_out, operands=[a, *[gm.reshape((N_DEV // g,) + blk) for gm in gs_list]], comm=comm)
    return res if comm is None else (res, ex)


def _mm_nt_cols(name, pairs, out_dtype, tm_pref=1024, tn_pref=1024, width_pref=1024, tr=False, comm=None):
    M = pairs[0][0].shape[0]
    c, Kw = pairs[0][1].shape[1:][::1 if tr else -1]
    g = _group(c, width_pref)
    W = g * c
    tm, tn = _tile(M, tm_pref, 8), _tile(Kw, tn_pref)
    nk = N_DEV // g
    n_p = len(pairs)

    def body(*refs):
        o_ref, acc = refs[2 * n_p], refs[2 * n_p + 1]
        k = pl.program_id(2)
        tot = None
        for n in range(n_p):
            d_ref, g_ref = refs[2 * n], refs[2 * n + 1]
            for s in range(g):
                part = lax.dot_general(d_ref[:, s * c:(s + 1) * c], g_ref[s], NN if tr else NT, preferred_element_type=f32)
                tot = part if tot is None else tot + part

        @pl.when(k == 0)
        def _():
            acc[...] = tot

        @pl.when(k > 0)
        def _():
            acc[...] += tot

        @pl.when(k == nk - 1)
        def _():
            o_ref[...] = acc[...].astype(o_ref.dtype)

    in_specs, operands = [], []
    for d, gm in pairs:
        if tr:
            wspec, wview = pl.BlockSpec((None, g, c, tn), lambda i, j, k: (k, 0, 0, j)), gm.reshape(nk, g, c, Kw)
        else:
            wspec, wview = pl.BlockSpec((None, g, tn, c), lambda i, j, k: (k, 0, j, 0)), gm.reshape(nk, g, Kw, c)
        in_specs += [pl.BlockSpec((tm, W), lambda i, j, k: (i, k)), wspec]
        operands += [d, wview]
    res, ex = _call(
        body, name=name, out_shape=[jax.ShapeDtypeStruct((M, Kw), out_dtype)], grid=(M // tm, Kw // tn, nk),
        in_specs=in_specs, out_specs=[pl.BlockSpec((tm, tn), lambda i, j, k: (i, j))],
        scratch=[pltpu.VMEM((tm, tn), f32)], operands=operands, comm=comm)
    return res[0] if comm is None else (res[0], ex)


def _mm_tn_cols(name, x, ds, c, tm_pref=1024, tk_pref=1024, width_pref=1024, tr=False, comm=None):
    S, Kw = x.shape
    g = _group(c, width_pref)
    W = g * c
    tm, tk = _tile(Kw, tm_pref), _tile(S, tk_pref, 16)
    nk = S // tk
    n_d = len(ds)
    blk = (g, c, tm) if tr else (g, tm, c)
    full = (N_DEV // g, g, c, Kw) if tr else (N_DEV // g, g, Kw, c)

    def body(*refs):
        x_ref = refs[0]
        d_refs = refs[1:1 + n_d]
        o_refs = refs[1 + n_d:1 + 2 * n_d]
        accs = refs[1 + 2 * n_d:]
        k = pl.program_id(2)

        @pl.when(k == 0)
        def _():
            for acc in accs:
                acc[...] = jnp.zeros_like(acc)

        xv = x_ref[...].astype(bf16)
        for d_ref, acc in zip(d_refs, accs):
            for s in range(g):
                ds_ = d_ref[:, s * c:(s + 1) * c]
                acc[s] += lax.dot_general(ds_, xv, TN, preferred_element_type=f32) if tr else \
                    lax.dot_general(xv, ds_, TN, preferred_element_type=f32)

        @pl.when(k == nk - 1)
        def _():
            for o_ref, acc in zip(o_refs, accs):
                o_ref[...] = acc[...].astype(o_ref.dtype)

    out_map = (lambda i, j, k: (j, 0, 0, i)) if tr else (lambda i, j, k: (j, 0, i, 0))
    outs, ex = _call(
        body, name=name, out_shape=[jax.ShapeDtypeStruct(full, bf16)] * n_d, grid=(Kw // tm, N_DEV // g, nk),
        in_specs=[pl.BlockSpec((tk, tm), lambda i, j, k: (k, i))] + [pl.BlockSpec((tk, W), lambda i, j, k: (k, j))] * n_d,
        out_specs=[pl.BlockSpec((None,) + blk, out_map)] * n_d,
        scratch=[pltpu.VMEM(blk, f32)] * n_d, operands=[x, *ds], comm=comm)
    outs = [o.reshape((N_DEV,) + full[2:]) for o in outs]
    return outs if comm is None else (outs, ex)


def _norm_fwd(name, x, g):
    S, D = x.shape
    tm = _tile(S, 512, 8)

    def body(x_ref, g_ref, h_ref):
        xv = x_ref[...]
        r = lax.rsqrt(jnp.mean(xv * xv, axis=-1, keepdims=True) + EPS)
        h_ref[...] = (xv * r * g_ref[...]).astype(bf16)

    return pl.pallas_call(
        body, name=name, out_shape=jax.ShapeDtypeStruct((S, D), bf16), grid=(S // tm,),
        in_specs=[pl.BlockSpec((tm, D), lambda i: (i, 0)), pl.BlockSpec((1, D), lambda i: (0, 0))],
        out_specs=pl.BlockSpec((tm, D), lambda i: (i, 0)), compiler_params=_cp("parallel"),
    )(x, g.reshape(1, D))


def _norm_bwd(name, dh, x, g, dx_in):
    S, D = x.shape
    tm = _tile(S, 256, 8)

    def body(dh_ref, x_ref, g_ref, dxi_ref, dx_ref, dxb_ref, dg_ref):
        i = pl.program_id(0)
        xv = x_ref[...]
        r = lax.rsqrt(jnp.mean(xv * xv, axis=-1, keepdims=True) + EPS)
        xh = xv * r
        dhv = dh_ref[...].astype(f32)
        dxh = dhv * g_ref[...]
        dx = dxi_ref[...] + r * (dxh - xh * jnp.mean(dxh * xh, axis=-1, keepdims=True))
        dx_ref[...] = dx
        dxb_ref[...] = dx.astype(bf16)

        @pl.when(i == 0)
        def _():
            dg_ref[...] = jnp.zeros_like(dg_ref)

        dg_ref[...] += jnp.sum(dhv * xh, axis=0, keepdims=True)

    row = pl.BlockSpec((tm, D), lambda i: (i, 0))
    vec = pl.BlockSpec((1, D), lambda i: (0, 0))
    return pl.pallas_call(
        body, name=name,
        out_shape=[jax.ShapeDtypeStruct((S, D), f32), jax.ShapeDtypeStruct((S, D), bf16), jax.ShapeDtypeStruct((1, D), f32)],
        grid=(S // tm,), in_specs=[row, row, vec, row], out_specs=[row, row, vec], compiler_params=_cp("arbitrary"),
    )(dh, x, g.reshape(1, D), dx_in)


def _loss_head(x, g, t):
    S, D = x.shape
    tm = _tile(S, 256, 8)

    def body(x_ref, g_ref, t_ref, dx_ref, dg_ref, loss_ref):
        i = pl.program_id(0)
        xv = x_ref[...]
        r = lax.rsqrt(jnp.mean(xv * xv, axis=-1, keepdims=True) + EPS)
        xh = xv * r
        gv = g_ref[...]
        err = xh * gv - t_ref[...]
        dy = err * (1.0 / D)
        dxh = dy * gv
        dx_ref[...] = r * (dxh - xh * jnp.mean(dxh * xh, axis=-1, keepdims=True))

        @pl.when(i == 0)
        def _():
            dg_ref[...] = jnp.zeros_like(dg_ref)
            loss_ref[...] = jnp.zeros_like(loss_ref)

        dg_ref[...] += jnp.sum(dy * xh, axis=0, keepdims=True)
        row = jnp.sum(err * err, axis=-1, keepdims=True) * (0.5 / D)
        loss_ref[...] += jnp.broadcast_to(jnp.sum(row, axis=0, keepdims=True), loss_ref.shape)

    return pl.pallas_call(
        body, name="loss_head",
        out_shape=[jax.ShapeDtypeStruct((S, D), f32), jax.ShapeDtypeStruct((1, D), f32), jax.ShapeDtypeStruct((8, LANES), f32)],
        grid=(S // tm,),
        in_specs=[pl.BlockSpec((tm, D), lambda i: (i, 0)), pl.BlockSpec((1, D), lambda i: (0, 0)), pl.BlockSpec((tm, D), lambda i: (i, 0))],
        out_specs=[pl.BlockSpec((tm, D), lambda i: (i, 0)), pl.BlockSpec((1, D), lambda i: (0, 0)), pl.BlockSpec((8, LANES), lambda i: (0, 0))],
        compiler_params=_cp("arbitrary"),
    )(x, g.reshape(1, D), t)


def _perm(t, d):
    if d == 1:
        return t
    S, C = t.shape
    return t.reshape(S // d, d, C).transpose(1, 0, 2).reshape(S, C)


def _rope_tables(S):
    half = HEAD_DIM // 2
    pos = jnp.arange(S, dtype=f32)
    inv_freq = ROPE_THETA ** (-jnp.arange(0, HEAD_DIM, 2, dtype=f32) / HEAD_DIM)
    ang = pos[:, None] * inv_freq[None, :]
    c, s = jnp.cos(ang), jnp.sin(ang)
    cos2 = jnp.concatenate([c, c], axis=-1)
    sin2 = jnp.concatenate([-s, s], axis=-1)
    assert cos2.shape == (S, 2 * half)
    return (jnp.stack([_perm(cos2, d) for d in DILATIONS]), jnp.stack([_perm(sin2, d) for d in DILATIONS]))


def _rope(t, c, s):
    return t * c + pltpu.roll(t, HEAD_DIM // 2, 1) * s


def _rope_bwd(dt, c, s):
    return dt * c - pltpu.roll(dt, HEAD_DIM // 2, 1) * s


def _band_bounds(i, nblk):
    g = pl.program_id(0)
    lb = jnp.right_shift(jnp.int32(nblk), 2 * g)
    pos = lax.rem(i, lb)
    lo = jnp.where(pos == 0, BLK, 0)
    hi = jnp.where(pos == lb - 1, 2 * BLK, 3 * BLK)
    return lo, hi


def _cur_spec(width, t=None):
    if t is None:
        return pl.BlockSpec((None, BLK, width), lambda g, i: (g, i, 0))
    return pl.BlockSpec((None, None, BLK, width), lambda g, i: (t, g, i, 0))


def _band_specs(width, nblk, t=None):
    lo, hi = (lambda i: jnp.maximum(i - 1, 0)), (lambda i: jnp.minimum(i + 1, nblk - 1))
    if t is None:
        return [pl.BlockSpec((None, BLK, width), lambda g, i: (g, lo(i), 0)), _cur_spec(width),
                pl.BlockSpec((None, BLK, width), lambda g, i: (g, hi(i), 0))]
    return [pl.BlockSpec((None, None, BLK, width), lambda g, i: (t, g, lo(i), 0)), _cur_spec(width, t),
            pl.BlockSpec((None, None, BLK, width), lambda g, i: (t, g, hi(i), 0))]


_ROWS = 2048


def _to_scratch(scr, val):
    for h in range(HEADS):
        scr[h] = val[:, h * HEAD_DIM:(h + 1) * HEAD_DIM].astype(f32)


def _qkv_groups(z):
    S = z.shape[0]
    R = min(_ROWS, S)
    nb = S // R

    def body(x_ref, o_ref, scr):
        g, i = pl.program_id(1), pl.program_id(2)
        _to_scratch(scr, x_ref[...])
        for gi, d in enumerate(DILATIONS):
            @pl.when(g == gi)
            def _():
                n, L = R // d, S // d
                for r in range(d):
                    start = pl.multiple_of(r * L + i * n, 16)
                    for h in range(HEADS):
                        o_ref[pl.ds(start, n), h * HEAD_DIM:(h + 1) * HEAD_DIM] = scr[h, pl.ds(r, n, stride=d), :].astype(bf16)

    return pl.pallas_call(
        body, name="qkv_groups", out_shape=jax.ShapeDtypeStruct((3, N_GROUPS, S, ATTN_W), bf16), grid=(3, N_GROUPS, nb),
        in_specs=[pl.BlockSpec((R, ATTN_W), lambda t, g, i: (i, t * N_GROUPS + g))],
        out_specs=pl.BlockSpec((None, None, S, ATTN_W), lambda t, g, i: (t, g, 0, 0)),
        scratch_shapes=[pltpu.VMEM((HEADS, R, HEAD_DIM), f32)], compiler_params=_cp("arbitrary", "arbitrary", "arbitrary"),
    )(z)


def _bwd_groups(attn, dattn, lse):
    S = attn.shape[0]
    R = min(_ROWS, S)
    nb = S // R

    def body(a_ref, d_ref, l_ref, o_ref, scr):
        t, g, i = pl.program_id(0), pl.program_id(1), pl.program_id(2)

        @pl.when(t == 0)
        def _():
            _to_scratch(scr, d_ref[...])

        @pl.when(t == 1)
        def _():
            _to_scratch(scr, l_ref[...])

        @pl.when(t == 2)
        def _():
            prod = a_ref[...].astype(f32) * d_ref[...].astype(f32)
            for h in range(HEADS):
                part = jnp.sum(prod[:, h * HEAD_DIM:(h + 1) * HEAD_DIM], axis=-1, keepdims=True)
                scr[h] = jnp.broadcast_to(part, (R, HEAD_DIM))

        for gi, d in enumerate(DILATIONS):
            @pl.when(g == gi)
            def _():
                n, L = R // d, S // d
                for r in range(d):
                    start = pl.multiple_of(r * L + i * n, 8)
                    for h in range(HEADS):
                        o_ref[pl.ds(start, n), h * HEAD_DIM:(h + 1) * HEAD_DIM] = scr[h, pl.ds(r, n, stride=d), :]

    def nat(used):
        return pl.BlockSpec((R, ATTN_W), lambda t, g, i: (jnp.where(used(t), i, 0), 0))

    return pl.pallas_call(
        body, name="bwd_groups", out_shape=jax.ShapeDtypeStruct((3, N_GROUPS, S, ATTN_W), f32), grid=(3, N_GROUPS, nb),
        in_specs=[nat(lambda t: t == 2), nat(lambda t: t != 1), nat(lambda t: t == 1)],
        out_specs=pl.BlockSpec((None, None, S, ATTN_W), lambda t, g, i: (t, g, 0, 0)),
        scratch_shapes=[pltpu.VMEM((HEADS, R, HEAD_DIM), f32)], compiler_params=_cp("arbitrary", "arbitrary", "arbitrary"),
    )(attn, dattn, lse)


def _group_views(t3, R):
    S = t3.shape[1]
    views = [t3.reshape(N_GROUPS, d, S // d, ATTN_W) for d in DILATIONS]
    specs = [pl.BlockSpec((None, d, R // d, ATTN_W), functools.partial(lambda i, g: (g, 0, i, 0), g=g))
             for g, d in enumerate(DILATIONS)]
    return views, specs


def _from_groups(scr, ref, d):
    n = ref.shape[1]
    for r in range(d):
        blk = ref[r]
        for h in range(HEADS):
            scr[h, pl.ds(r, n, stride=d), :] = blk[:, h * HEAD_DIM:(h + 1) * HEAD_DIM].astype(f32)


def _to_natural(t3):
    S = t3.shape[1]
    R = min(_ROWS // 2, S)
    views, specs = _group_views(t3, R)

    def body(v0, v1, v2, o_ref, scr):
        for g, (ref, d) in enumerate(zip((v0, v1, v2), DILATIONS)):
            _from_groups(scr, ref, d)
            for h in range(HEADS):
                o_ref[:, g * ATTN_W + h * HEAD_DIM:g * ATTN_W + (h + 1) * HEAD_DIM] = scr[h].astype(bf16)

    return pl.pallas_call(
        body, name="to_natural", out_shape=jax.ShapeDtypeStruct((S, N_GROUPS * ATTN_W), bf16), grid=(S // R,),
        in_specs=specs, out_specs=pl.BlockSpec((R, N_GROUPS * ATTN_W), lambda i: (i, 0)),
        scratch_shapes=[pltpu.VMEM((HEADS, R, HEAD_DIM), f32)], compiler_params=_cp("arbitrary"),
    )(*views)


_SCALE = HEAD_DIM ** -0.5


def _attn_fwd(qkv, cos, sin, comm=None):
    _, _, S, W = qkv.shape
    nblk = S // BLK

    def body(q_ref, kp, kc, kn, vp, vc, vn, cq, sq, ckp, ckc, ckn, skp, skc, skn, o_ref, lse_ref):
        i = pl.program_id(1)
        lo, hi = _band_bounds(i, nblk)
        a = lax.broadcasted_iota(jnp.int32, (BLK, 3 * BLK), 0)
        b = lax.broadcasted_iota(jnp.int32, (BLK, 3 * BLK), 1)
        mask = (jnp.abs(b - BLK - a) <= RADIUS) & (b >= lo) & (b < hi)
        ck = jnp.concatenate([ckp[...], ckc[...], ckn[...]], axis=0)
        sk = jnp.concatenate([skp[...], skc[...], skn[...]], axis=0)
        for hh in range(HEADS):
            sl = slice(hh * HEAD_DIM, (hh + 1) * HEAD_DIM)
            qh = _rope(q_ref[:, sl].astype(f32), cq[...], sq[...]).astype(bf16)
            kh = jnp.concatenate([kp[:, sl], kc[:, sl], kn[:, sl]], axis=0).astype(f32)
            kh = _rope(kh, ck, sk).astype(bf16)
            vh = jnp.concatenate([vp[:, sl], vc[:, sl], vn[:, sl]], axis=0)
            s = lax.dot_general(qh, kh, NT, preferred_element_type=f32) * _SCALE
            s = jnp.where(mask, s, NEG)
            m = jnp.max(s, axis=-1, keepdims=True)
            e = jnp.exp(s - m)
            den = jnp.sum(e, axis=-1, keepdims=True)
            o = lax.dot_general(e.astype(bf16), vh, NN, preferred_element_type=f32) * (1.0 / den)
            o_ref[:, sl] = o.astype(bf16)
            lse_ref[:, sl] = jnp.broadcast_to(m + jnp.log(den), (BLK, HEAD_DIM))

    blk = _cur_spec(W)
    tab = _cur_spec(HEAD_DIM)
    res, ex = _call(
        body, name="attn_fwd",
        out_shape=[jax.ShapeDtypeStruct((N_GROUPS, S, W), bf16), jax.ShapeDtypeStruct((N_GROUPS, S, W), f32)],
        grid=(N_GROUPS, nblk),
        in_specs=[_cur_spec(W, 0)] + _band_specs(W, nblk, 1) + _band_specs(W, nblk, 2) + [tab, tab]
        + _band_specs(HEAD_DIM, nblk) * 2,
        out_specs=[blk, blk], operands=[qkv] * 7 + [cos, sin, cos, cos, cos, sin, sin, sin], comm=comm)
    return res if comm is None else (res, ex)


def _attn_combine(o3, lse3):
    _, S, W = o3.shape
    R = min(_ROWS // 2, S)
    o_views, specs = _group_views(o3, R)
    l_views, _ = _group_views(lse3, R)

    def body(o0, o1, o2, l0, l1, l2, attn_ref, lse_ref, so0, so1, so2, sl0, sl1, sl2):
        for ref, scr, d in zip((o0, o1, o2, l0, l1, l2), (so0, so1, so2, sl0, sl1, sl2), DILATIONS * 2):
            _from_groups(scr, ref, d)
        for h in range(HEADS):
            a0, a1, a2 = sl0[h], sl1[h], sl2[h]
            m = jnp.maximum(jnp.maximum(a0, a1), a2)
            w0, w1, w2 = jnp.exp(a0 - m), jnp.exp(a1 - m), jnp.exp(a2 - m)
            den = w0 + w1 + w2
            acc = w0 * so0[h] + w1 * so1[h] + w2 * so2[h]
            attn_ref[:, h * HEAD_DIM:(h + 1) * HEAD_DIM] = (acc * (1.0 / den)).astype(bf16)
            lse_ref[:, h * HEAD_DIM:(h + 1) * HEAD_DIM] = m + jnp.log(den)

    nat = pl.BlockSpec((R, W), lambda i: (i, 0))
    return pl.pallas_call(
        body, name="attn_combine", out_shape=[jax.ShapeDtypeStruct((S, W), bf16), jax.ShapeDtypeStruct((S, W), f32)],
        grid=(S // R,), in_specs=specs * 2, out_specs=[nat, nat],
        scratch_shapes=[pltpu.VMEM((HEADS, R, HEAD_DIM), f32)] * 6, compiler_params=_cp("arbitrary"),
    )(*o_views, *l_views)


def _attn_bwd_dq(qkv, cos, sin, b3, comm=None):
    _, _, S, W = qkv.shape
    nblk = S // BLK

    def body(q_ref, kp, kc, kn, vp, vc, vn, cq, sq, ckp, ckc, ckn, skp, skc, skn, da_ref, l_ref, dl_ref, dq_ref):
        i = pl.program_id(1)
        lo, hi = _band_bounds(i, nblk)
        a = lax.broadcasted_iota(jnp.int32, (BLK, 3 * BLK), 0)
        b = lax.broadcasted_iota(jnp.int32, (BLK, 3 * BLK), 1)
        mask = (jnp.abs(b - BLK - a) <= RADIUS) & (b >= lo) & (b < hi)
        ck = jnp.concatenate([ckp[...], ckc[...], ckn[...]], axis=0)
        sk = jnp.concatenate([skp[...], skc[...], skn[...]], axis=0)
        for hh in range(HEADS):
            sl = slice(hh * HEAD_DIM, (hh + 1) * HEAD_DIM)
            qh = _rope(q_ref[:, sl].astype(f32), cq[...], sq[...]).astype(bf16)
            kh = jnp.concatenate([kp[:, sl], kc[:, sl], kn[:, sl]], axis=0).astype(f32)
            kh = _rope(kh, ck, sk).astype(bf16)
            vh = jnp.concatenate([vp[:, sl], vc[:, sl], vn[:, sl]], axis=0)
            s = lax.dot_general(qh, kh, NT, preferred_element_type=f32) * _SCALE
            lh = l_ref[:, sl]
            l3 = jnp.concatenate([lh, lh, lh], axis=1)
            p = jnp.exp(jnp.where(mask, s - l3, NEG))
            dp = lax.dot_general(da_ref[:, sl].astype(bf16), vh, NT, preferred_element_type=f32)
            dh = dl_ref[:, sl]
            ds = p * (dp - jnp.concatenate([dh, dh, dh], axis=1))
            dqh = lax.dot_general(ds.astype(bf16), kh, NN, preferred_element_type=f32) * _SCALE
            dq_ref[:, sl] = _rope_bwd(dqh, cq[...], sq[...]).astype(bf16)

    tab = _cur_spec(HEAD_DIM)
    res, ex = _call(
        body, name="attn_bwd_dq", out_shape=[jax.ShapeDtypeStruct((N_GROUPS, S, W), bf16)], grid=(N_GROUPS, nblk),
        in_specs=[_cur_spec(W, 0)] + _band_specs(W, nblk, 1) + _band_specs(W, nblk, 2) + [tab, tab]
        + _band_specs(HEAD_DIM, nblk) * 2 + [_cur_spec(W, 0), _cur_spec(W, 1), _cur_spec(W, 2)],
        out_specs=[_cur_spec(W)], operands=[qkv] * 7 + [cos, sin, cos, cos, cos, sin, sin, sin, b3, b3, b3], comm=comm)
    return res[0] if comm is None else (res[0], ex)


def _attn_bwd_dkv(qkv, cos, sin, b3, comm=None):
    _, _, S, W = qkv.shape
    nblk = S // BLK

    def body(k_ref, v_ref, ck, sk, qp, qc, qn, cqp, cqc, cqn, sqp, sqc, sqn, dap, dac, dan, lp, lc, ln, dlp, dlc, dln,
             dk_ref, dv_ref):
        j = pl.program_id(1)
        lo, hi = _band_bounds(j, nblk)
        a = lax.broadcasted_iota(jnp.int32, (3 * BLK, BLK), 0)
        b = lax.broadcasted_iota(jnp.int32, (3 * BLK, BLK), 1)
        mask = (jnp.abs(b - (a - BLK)) <= RADIUS) & (a >= lo) & (a < hi)
        cq = jnp.concatenate([cqp[...], cqc[...], cqn[...]], axis=0)
        sq = jnp.concatenate([sqp[...], sqc[...], sqn[...]], axis=0)
        for hh in range(HEADS):
            sl = slice(hh * HEAD_DIM, (hh + 1) * HEAD_DIM)
            kh = _rope(k_ref[:, sl].astype(f32), ck[...], sk[...]).astype(bf16)
            vh = v_ref[:, sl]
            qh = jnp.concatenate([qp[:, sl], qc[:, sl], qn[:, sl]], axis=0).astype(f32)
            qh = _rope(qh, cq, sq).astype(bf16)
            dah = jnp.concatenate([dap[:, sl], dac[:, sl], dan[:, sl]], axis=0).astype(bf16)
            lh = jnp.concatenate([lp[:, sl], lc[:, sl], ln[:, sl]], axis=0)
            dlh = jnp.concatenate([dlp[:, sl], dlc[:, sl], dln[:, sl]], axis=0)
            s = lax.dot_general(qh, kh, NT, preferred_element_type=f32) * _SCALE
            p = jnp.exp(jnp.where(mask, s - lh, NEG))
            dv_ref[:, sl] = lax.dot_general(p.astype(bf16), dah, TN, preferred_element_type=f32).astype(bf16)
            dp = lax.dot_general(dah, vh, NT, preferred_element_type=f32)
            ds = p * (dp - dlh)
            dkh = lax.dot_general(ds.astype(bf16), qh, TN, preferred_element_type=f32) * _SCALE
            dk_ref[:, sl] = _rope_bwd(dkh, ck[...], sk[...]).astype(bf16)

    blk, tab, bt = _cur_spec(W), _cur_spec(HEAD_DIM), _band_specs(HEAD_DIM, nblk)
    res, ex = _call(
        body, name="attn_bwd_dkv",
        out_shape=[jax.ShapeDtypeStruct((N_GROUPS, S, W), bf16), jax.ShapeDtypeStruct((N_GROUPS, S, W), bf16)],
        grid=(N_GROUPS, nblk),
        in_specs=[_cur_spec(W, 1), _cur_spec(W, 2), tab, tab] + _band_specs(W, nblk, 0) + bt + bt
        + _band_specs(W, nblk, 0) + _band_specs(W, nblk, 1) + _band_specs(W, nblk, 2), out_specs=[blk, blk],
        operands=[qkv, qkv, cos, sin, qkv, qkv, qkv, cos, cos, cos, sin, sin, sin] + [b3] * 9, comm=comm)
    return res if comm is None else (res, ex)


_SG_ROWS = 512


def _sg_z_specs(tm, half):
    o = QKV_W // half
    return [pl.BlockSpec((tm, half), functools.partial(lambda i, c: (i, c), c=o + n)) for n in range(4)]


def _sg_norm(v, lg, lb):
    gv = _gelu(v)
    mu = jnp.mean(gv, axis=-1, keepdims=True)
    xc = gv - mu
    rstd = lax.rsqrt(jnp.mean(xc * xc, axis=-1, keepdims=True) + EPS)
    xh = xc * rstd
    return xh, rstd, xh * lg + lb


def _sg_fwd(z, w, bb, lg, lb):
    S = z.shape[0]
    tm = _tile(S, _SG_ROWS, SG_CHUNK)
    half = SG_W // 2

    def body(u0, u1, v0, v1, w_ref, bb_ref, lg_ref, lb_ref, o_ref):
        u = jnp.concatenate([u0[...], u1[...]], axis=1).astype(f32)
        v = jnp.concatenate([v0[...], v1[...]], axis=1).astype(f32)
        gu = _gelu(u)
        _, _, vn = _sg_norm(v, lg_ref[...], lb_ref[...])
        vnb = vn.astype(bf16)
        for c in range(tm // SG_CHUNK):
            rs = slice(c * SG_CHUNK, (c + 1) * SG_CHUNK)
            for g in range(SG_GROUPS):
                cs = slice(g * 128, (g + 1) * 128)
                mixed = lax.dot_general(w_ref[g], vnb[rs, cs], NN, preferred_element_type=f32) + bb_ref[g]
                o_ref[rs, cs] = (gu[rs, cs] * mixed).astype(bf16)

    full3 = pl.BlockSpec((SG_GROUPS, 128, 128), lambda i: (0, 0, 0))
    vec = pl.BlockSpec((1, SG_W), lambda i: (0, 0))
    return pl.pallas_call(
        body, name="sg_fwd", out_shape=jax.ShapeDtypeStruct((S, SG_W), bf16), grid=(S // tm,),
        in_specs=_sg_z_specs(tm, half) + [full3, full3, vec, vec],
        out_specs=pl.BlockSpec((tm, SG_W), lambda i: (i, 0)), compiler_params=_cp("parallel"),
    )(z, z, z, z, w, bb, lg.reshape(1, SG_W), lb.reshape(1, SG_W))


def _sg_bwd(dsg, z, w, wt, bb, lg, lb):
    S = z.shape[0]
    tm = _tile(S, _SG_ROWS, SG_CHUNK)
    half = SG_W // 2

    def body(d_ref, u0, u1, v0, v1, w_ref, wt_ref, bb_ref, lg_ref, lb_ref, du_ref, dv_ref, dw_ref, db_ref, dlg_ref, dlb_ref, dvn_scr):
        i = pl.program_id(0)

        @pl.when(i == 0)
        def _():
            dw_ref[...] = jnp.zeros_like(dw_ref)
            db_ref[...] = jnp.zeros_like(db_ref)
            dlg_ref[...] = jnp.zeros_like(dlg_ref)
            dlb_ref[...] = jnp.zeros_like(dlb_ref)

        u = jnp.concatenate([u0[...], u1[...]], axis=1).astype(f32)
        v = jnp.concatenate([v0[...], v1[...]], axis=1).astype(f32)
        gu = _gelu(u)
        dgu = _gelu_grad(u)
        xh, rstd, vn = _sg_norm(v, lg_ref[...], lb_ref[...])
        vnb = vn.astype(bf16)
        dsg_v = d_ref[...].astype(f32)
        for g in range(SG_GROUPS):
            cs = slice(g * 128, (g + 1) * 128)
            dw_g = jnp.zeros((128, 128), f32)
            db_g = jnp.zeros((128, 1), f32)
            for c in range(tm // SG_CHUNK):
                rs = slice(c * SG_CHUNK, (c + 1) * SG_CHUNK)
                ds = dsg_v[rs, cs]
                mixed = lax.dot_general(w_ref[g], vnb[rs, cs], NN, preferred_element_type=f32) + bb_ref[g]
                du_ref[rs, cs] = (ds * mixed * dgu[rs, cs]).astype(bf16)
                dmix = ds * gu[rs, cs]
                dmb = dmix.astype(bf16)
                dw_g = dw_g + lax.dot_general(dmb, vnb[rs, cs], NT, preferred_element_type=f32)
                db_g = db_g + jnp.sum(dmix, axis=-1, keepdims=True)
                dvn_scr[rs, cs] = lax.dot_general(wt_ref[g], dmb, NN, preferred_element_type=f32)
            dw_ref[g] += dw_g
            db_ref[g] += jnp.broadcast_to(db_g, (128, 128))
        dvn = dvn_scr[...]
        dlg_ref[...] += jnp.sum(dvn * xh, axis=0, keepdims=True)
        dlb_ref[...] += jnp.sum(dvn, axis=0, keepdims=True)
        dxh = dvn * lg_ref[...]
        dgv = rstd * (dxh - jnp.mean(dxh, axis=-1, keepdims=True) - xh * jnp.mean(dxh * xh, axis=-1, keepdims=True))
        dv_ref[...] = (dgv * _gelu_grad(v)).astype(bf16)

    full3 = pl.BlockSpec((SG_GROUPS, 128, 128), lambda i: (0, 0, 0))
    vec = pl.BlockSpec((1, SG_W), lambda i: (0, 0))
    row = pl.BlockSpec((tm, SG_W), lambda i: (i, 0))
    return pl.pallas_call(
        body, name="sg_bwd",
        out_shape=[jax.ShapeDtypeStruct((S, SG_W), bf16), jax.ShapeDtypeStruct((S, SG_W), bf16),
                   jax.ShapeDtypeStruct((SG_GROUPS, 128, 128), f32), jax.ShapeDtypeStruct((SG_GROUPS, 128, 128), f32),
                   jax.ShapeDtypeStruct((1, SG_W), f32), jax.ShapeDtypeStruct((1, SG_W), f32)],
        grid=(S // tm,),
        in_specs=[row] + _sg_z_specs(tm, half) + [full3, full3, full3, vec, vec],
        out_specs=[row, row, full3, full3, vec, vec],
        scratch_shapes=[pltpu.VMEM((tm, SG_W), f32)], compiler_params=_cp("arbitrary"),
    )(dsg, z, z, z, z, w, wt, bb, lg.reshape(1, SG_W), lb.reshape(1, SG_W))


def _merge_fwd(z, ya, yb, D):
    S = z.shape[0]
    tm, tc = _tile(S, 512, 8), _tile(D, 512)
    o_a, o_b = (QKV_W + 2 * SG_W) // tc, (QKV_W + 2 * SG_W + D) // tc

    def body(ga_ref, gb_ref, ya_ref, yb_ref, o_ref):
        ga = _sigmoid(ga_ref[...].astype(f32))
        gb = _sigmoid(gb_ref[...].astype(f32))
        o_ref[...] = (ga * ya_ref[...].astype(f32) + gb * yb_ref[...].astype(f32)).astype(bf16)

    blk = pl.BlockSpec((tm, tc), lambda i, j: (i, j))
    return pl.pallas_call(
        body, name="merge_fwd", out_shape=jax.ShapeDtypeStruct((S, D), bf16), grid=(S // tm, D // tc),
        in_specs=[pl.BlockSpec((tm, tc), lambda i, j: (i, o_a + j)), pl.BlockSpec((tm, tc), lambda i, j: (i, o_b + j)), blk, blk],
        out_specs=blk, compiler_params=_cp("parallel", "parallel"),
    )(z, z, ya, yb)


def _ple_bwd_ew(dx, gp, e):
    S, D = dx.shape
    tm, tc = _tile(S, 512, 8), _tile(D, 1024)

    def body(dx_ref, gp_ref, e_ref, dgp_ref, de_ref):
        dxv = dx_ref[...]
        sg = _sigmoid(gp_ref[...].astype(f32))
        dgp_ref[...] = (dxv * e_ref[...].astype(f32) * sg * (1.0 - sg)).astype(bf16)
        de_ref[...] = (dxv * sg).astype(bf16)

    blk = pl.BlockSpec((tm, tc), lambda i, j: (i, j))
    return pl.pallas_call(
        body, name="ple_bwd_ew", out_shape=[jax.ShapeDtypeStruct((S, D), bf16)] * 2, grid=(S // tm, D // tc),
        in_specs=[blk, blk, blk], out_specs=[blk, blk], compiler_params=_cp("parallel", "parallel"),
    )(dx, gp, e)


def _adam_math(w, g, m, v):
    m = ADAM_B1 * m + (1.0 - ADAM_B1) * g
    v = ADAM_B2 * v + (1.0 - ADAM_B2) * (g * g)
    m_hat = m / (1.0 - ADAM_B1 ** ADAM_STEP)
    v_hat = v / (1.0 - ADAM_B2 ** ADAM_STEP)
    delta = -ADAM_LR * (m_hat / (jnp.sqrt(v_hat) + ADAM_EPS) + ADAM_WD * w)
    return delta, m, v


def _small_sum_adamw(gathered, w, m, v):
    _, R, _ = gathered.shape
    tr = _tile(R, 1024, SMALL_ROWS)

    def body(p_ref, w_ref, m_ref, v_ref, g_ref, d_ref, nm_ref, nv_ref):
        g = p_ref[0]
        for n in range(1, N_DEV):
            g = g + p_ref[n]
        d, nm, nv = _adam_math(w_ref[...], g, m_ref[...], v_ref[...])
        g_ref[...] = g
        d_ref[...] = d
        nm_ref[...] = nm
        nv_ref[...] = nv

    blk = pl.BlockSpec((tr, LANES), lambda i: (i, 0))
    return pl.pallas_call(
        body, name="small_sum_adamw", out_shape=[jax.ShapeDtypeStruct((R, LANES), f32)] * 4, grid=(R // tr,),
        in_specs=[pl.BlockSpec((N_DEV, tr, LANES), lambda i: (0, i, 0)), blk, blk, blk], out_specs=[blk] * 4,
        compiler_params=_cp("parallel"),
    )(gathered, w, m, v)


def _all_gather(name, shard, in_vmem=False):
    R, C = shard.shape

    def body(x_ref, out_ref, send_sems, recv_sems, local_sem):
        x, y, c = lax.axis_index("x"), lax.axis_index("y"), lax.axis_index("c")
        me, sibling = (x, y, c), (x, y, 1 - c)
        chips = [(1 - x, y), (x, 1 - y), (1 - x, 1 - y)]

        def rows(px, py, pc):
            return out_ref.at[4 * px + 2 * py + pc]

        def copy(k, block, to, src=None):
            return pltpu.make_async_remote_copy(
                src_ref=rows(*block) if src is None else src, dst_ref=rows(*block),
                send_sem=send_sems.at[k], recv_sem=recv_sems.at[k], device_id=to, device_id_type=MESH)

        mine = pltpu.make_async_copy(x_ref, rows(*me), local_sem)
        mine.start()
        first = [copy(0, me, sibling, src=x_ref)]
        first += [copy(1 + j, me, (*chip, c), src=x_ref) for j, chip in enumerate(chips)]
        for cp in first:
            cp.start()
        passed = [copy(4 + j, (*chip, c), sibling) for j, chip in enumerate(chips)]
        for j, chip in enumerate(chips):
            copy(1 + j, (*chip, c), me).wait_recv()
            passed[j].start()
        copy(0, sibling, me).wait_recv()
        for j, chip in enumerate(chips):
            copy(4 + j, (*chip, 1 - c), me).wait_recv()
        for cp in first + passed:
            cp.wait_send()
        mine.wait()

    space = pl.BlockSpec(memory_space=pltpu.VMEM) if in_vmem else _ANY
    return pl.pallas_call(
        body, name=name, out_shape=jax.ShapeDtypeStruct((N_DEV, R, C), shard.dtype),
        in_specs=[space], out_specs=space,
        scratch_shapes=[pltpu.SemaphoreType.DMA((7,)), pltpu.SemaphoreType.DMA((7,)), pltpu.SemaphoreType.DMA],
        compiler_params=pltpu.CompilerParams(has_side_effects=True, vmem_limit_bytes=VMEM_LIMIT),
    )(shard)


class _Exchange:
    def __init__(self, ins, outs, sems, start, finish, aliases=None):
        self.ins, self.outs, self.sems, self.start, self.finish = list(ins), list(outs), list(sems), start, finish
        self.aliases = dict(aliases or {})


def _run_exchange(name, ex):
    c_in, c_out = len(ex.ins), len(ex.outs)

    def body(*refs):
        ins, outs, sems = refs[:c_in], refs[c_in:c_in + c_out], refs[c_in + c_out:]
        ex.start(ins, outs, sems)
        ex.finish(ins, outs, sems)

    return pl.pallas_call(
        body, name=name, out_shape=ex.outs, in_specs=[_ANY] * c_in, out_specs=[_ANY] * c_out, scratch_shapes=ex.sems,
        input_output_aliases=ex.aliases,
        compiler_params=pltpu.CompilerParams(has_side_effects=True, vmem_limit_bytes=VMEM_LIMIT),
    )(*ex.ins)


def _both(e1, e2):
    i1, o1, s1 = len(e1.ins), len(e1.outs), len(e1.sems)

    def start(ins, outs, sems):
        e1.start(ins[:i1], outs[:o1], sems[:s1])
        e2.start(ins[i1:], outs[o1:], sems[s1:])

    def finish(ins, outs, sems):
        e1.finish(ins[:i1], outs[:o1], sems[:s1])
        e2.finish(ins[i1:], outs[o1:], sems[s1:])

    aliases = dict(e1.aliases)
    aliases.update({i1 + a: o1 + b for a, b in e2.aliases.items()})
    return _Exchange(e1.ins + e2.ins, e1.outs + e2.outs, e1.sems + e2.sems, start, finish, aliases)


def _gather_exchange(shards, parts=None, into=None):
    n = len(shards)
    parts = parts or [None] * n
    into = into or [None] * n
    given = [w for w in range(n) if into[w] is not None]

    def plan(ins, outs, sems):
        send_sems, recv_sems, local_sems = sems
        x, y, c = lax.axis_index("x"), lax.axis_index("y"), lax.axis_index("c")
        me, sibling = (x, y, c), (x, y, 1 - c)
        chips = [(1 - x, y), (x, 1 - y), (1 - x, 1 - y)]

        def cut(ref, w):
            return ref if parts[w] is None else ref.at[pl.ds(parts[w][0], parts[w][1])]

        def rows(w, px, py, pc):
            return cut(outs[w].at[4 * px + 2 * py + pc], w)

        def copy(w, k, block, to, src=None):
            return pltpu.make_async_remote_copy(
                src_ref=rows(w, *block) if src is None else src, dst_ref=rows(w, *block),
                send_sem=send_sems.at[w, k], recv_sem=recv_sems.at[w, k], device_id=to, device_id_type=MESH)

        mine = [pltpu.make_async_copy(cut(ins[w], w), rows(w, *me), local_sems.at[w]) for w in range(n)]
        first = []
        for w in range(n):
            first.append(copy(w, 0, me, sibling, src=cut(ins[w], w)))
            first += [copy(w, 1 + j, me, (*chip, c), src=cut(ins[w], w)) for j, chip in enumerate(chips)]
        return c, me, sibling, chips, copy, mine, first

    def start(ins, outs, sems):
        _, _, _, _, _, mine, first = plan(ins, outs, sems)
        for cp in mine + first:
            cp.start()

    def finish(ins, outs, sems):
        c, me, sibling, chips, copy, mine, first = plan(ins, outs, sems)
        passed = []
        for w in range(n):
            for j, chip in enumerate(chips):
                copy(w, 1 + j, (*chip, c), me).wait_recv()
                passed.append(copy(w, 4 + j, (*chip, c), sibling))
                passed[-1].start()
        for w in range(n):
            copy(w, 0, sibling, me).wait_recv()
            for j, chip in enumerate(chips):
                copy(w, 4 + j, (*chip, 1 - c), me).wait_recv()
        for cp in first + passed:
            cp.wait_send()
        for cp in mine:
            cp.wait()

    return _Exchange(
        list(shards) + [into[w] for w in given], [jax.ShapeDtypeStruct((N_DEV,) + s.shape, s.dtype) for s in shards],
        [pltpu.SemaphoreType.DMA((n, 7)), pltpu.SemaphoreType.DMA((n, 7)), pltpu.SemaphoreType.DMA((n,))], start, finish,
        {n + k: w for k, w in enumerate(given)})


def _sibling_exchange(gs):
    n = len(gs)

    def copies(ins, outs, sems):
        send_sems, recv_sems = sems
        x, y, c = lax.axis_index("x"), lax.axis_index("y"), lax.axis_index("c")
        return [pltpu.make_async_remote_copy(
            src_ref=ins[w].at[2 * q + (1 - c)], dst_ref=outs[w].at[q], send_sem=send_sems.at[w, q],
            recv_sem=recv_sems.at[w, q], device_id=(x, y, 1 - c), device_id_type=MESH) for w in range(n) for q in range(4)]

    def start(ins, outs, sems):
        for cp in copies(ins, outs, sems):
            cp.start()

    def finish(ins, outs, sems):
        cps = copies(ins, outs, sems)
        for cp in cps:
            cp.wait_recv()
        for cp in cps:
            cp.wait_send()

    return _Exchange(gs, [jax.ShapeDtypeStruct((4,) + g.shape[1:], g.dtype) for g in gs],
                     [pltpu.SemaphoreType.DMA((n, 4)), pltpu.SemaphoreType.DMA((n, 4))], start, finish)


def _rs_chip_sum(name, g8, recv, c_idx):
    _, R, C = g8.shape
    tr = _tile(R, 512, 16)
    g42 = g8.reshape(4, 2, R, C)

    def body(c_ref, a_ref, b_ref, o_ref):
        o_ref[...] = (a_ref[...].astype(f32) + b_ref[...].astype(f32)).astype(o_ref.dtype)

    return pl.pallas_call(
        body, name=name, out_shape=jax.ShapeDtypeStruct((4, R, C), g8.dtype),
        grid_spec=pltpu.PrefetchScalarGridSpec(
            num_scalar_prefetch=1, grid=(4, R // tr),
            in_specs=[pl.BlockSpec((None, None, tr, C), lambda q, r, c_ref: (q, c_ref[0], r, 0)),
                      pl.BlockSpec((None, tr, C), lambda q, r, c_ref: (q, r, 0))],
            out_specs=pl.BlockSpec((None, tr, C), lambda q, r, c_ref: (q, r, 0))),
        compiler_params=_cp("parallel", "parallel"),
    )(c_idx, g42, recv)


def _chips_exchange(p4s):
    n = len(p4s)

    def copies(ins, outs, sems):
        send_sems, recv_sems = sems
        x, y, c = lax.axis_index("x"), lax.axis_index("y"), lax.axis_index("c")
        chips = [(1 - x, y), (x, 1 - y), (1 - x, 1 - y)]
        return [pltpu.make_async_remote_copy(
            src_ref=ins[w].at[2 * cx + cy], dst_ref=outs[w].at[k], send_sem=send_sems.at[w, k],
            recv_sem=recv_sems.at[w, k], device_id=(cx, cy, c), device_id_type=MESH)
            for w in range(n) for k, (cx, cy) in enumerate(chips)]

    def start(ins, outs, sems):
        for cp in copies(ins, outs, sems):
            cp.start()

    def finish(ins, outs, sems):
        cps = copies(ins, outs, sems)
        for cp in cps:
            cp.wait_recv()
        for cp in cps:
            cp.wait_send()

    return _Exchange(p4s, [jax.ShapeDtypeStruct((3,) + p.shape[1:], p.dtype) for p in p4s],
                     [pltpu.SemaphoreType.DMA((n, 3)), pltpu.SemaphoreType.DMA((n, 3))], start, finish)


def _adamw_layer(name, layer, w, m, v, p4, recv, q_idx, prev):
    depth, R, C = w.shape
    tr = _tile(R, 256, 8)

    def body(q_ref, w_ref, m_ref, v_ref, a_ref, b_ref, *rest):
        g_ref, d_ref, nm_ref, nv_ref = rest[-4:]
        g = ((a_ref[...].astype(f32) + b_ref[0].astype(f32)) + b_ref[1].astype(f32)) + b_ref[2].astype(f32)
        d, nm, nv = _adam_math(w_ref[...], g, m_ref[...], v_ref[...])
        g_ref[...] = g
        d_ref[...] = d
        nm_ref[...] = nm
        nv_ref[...] = nv

    lay = pl.BlockSpec((None, tr, C), lambda i, q_ref: (layer, i, 0))
    n_prev = 0 if prev is None else 4
    return pl.pallas_call(
        body, name=name, out_shape=[jax.ShapeDtypeStruct((depth, R, C), f32)] * 4,
        grid_spec=pltpu.PrefetchScalarGridSpec(
            num_scalar_prefetch=1, grid=(R // tr,),
            in_specs=[lay, lay, lay, pl.BlockSpec((None, tr, C), lambda i, q_ref: (q_ref[0], i, 0)),
                      pl.BlockSpec((3, tr, C), lambda i, q_ref: (0, i, 0))] + [_ANY] * n_prev,
            out_specs=[lay] * 4),
        input_output_aliases={6 + n: n for n in range(n_prev)},
        compiler_params=_cp("parallel"),
    )(q_idx, w, m, v, p4, recv, *(prev or ()))


_BIG = (("w_in", 1), ("w_br_attn", 1), ("w_br_sg", 1), ("w_out", 0), ("w_ff_gate", 1), ("w_ff_up", 1),
        ("w_ff_down", 0), ("w_ple_gate", 0), ("w_ple", 1))


_TURNED = ("w_in", "w_ff_gate", "w_ff_up")


def _gather_plan(i, depth):
    mixer = ["w_br_attn", "w_br_sg", "w_out"]
    plan = {
        "mm_in": [(i, "w_ff_gate")] + ([(i, n) for n in mixer] if i == 0 else []),
        "attn_fwd": [(i, "w_ff_up")],
        "mm_out": [(i, "w_ple_gate"), (i, "w_ple")],
        "mm_ffn_in": [(i, "w_ff_down")],
    }
    if i + 1 < depth:
        plan["mm_ffn_in"] = plan["mm_ffn_in"] + [(i + 1, "w_in", 0)]
        plan["mm_ffn_out"] = [(i + 1, "w_in", 1)]
        plan["mm_ple"] = [(i + 1, n) for n in mixer]
    return plan


def _layer_fwd(x0, p_i, layer, arrived, sm, tabs, comm):
    S, D = x0.shape
    cos, sin = tabs
    tmm = _tile(S, 1024, 8)
    same = lambda accs, ex: accs

    def W(name):
        return arrived[(layer, name)]

    def hosted(key, fn):
        if key not in comm:
            return fn(None)
        ex, keys = comm[key]()
        res, outs = fn(ex)
        arrived.update(zip(keys, outs))
        return res

    h1 = _norm_fwd("norm_fwd", x0, sm["norm_mix"])
    z = hosted("mm_in", lambda ex: _mm_nn_cols("mm_in", h1, [W("w_in")], [bf16], same, tr=True, comm=ex))[0]
    IN = z.shape[1]

    qkv = _qkv_groups(z)
    o3, lse3 = hosted("attn_fwd", lambda ex: _attn_fwd(qkv, cos, sin, comm=ex))
    attn, lse = _attn_combine(o3, lse3)
    ya = _mm_nn_cols("mm_br_attn", attn, [W("w_br_attn")], [bf16], same, tm_pref=1024)[0]
    sgw = sm["sg_w"].astype(bf16)
    bb = jnp.broadcast_to(sm["sg_b"][:, :, None], (SG_GROUPS, SG_CHUNK, 128))
    sg = _sg_fwd(z, sgw, bb, sm["sg_ln_g"], sm["sg_ln_b"])
    yb = _mm_nn_cols("mm_br_sg", sg, [W("w_br_sg")], [bf16], same, tm_pref=1024)[0]
    merged = _merge_fwd(z, ya, yb, D)
    tn = _tile(D, 1024)
    x1 = hosted("mm_out", lambda ex: _matmul(
        "mm_out", [(merged, W("w_out").reshape(D, D), "nn", 0)], S, D, tmm, tn, 1,
        [((S, D), f32, (tmm, tn), lambda i, j: (i, j))], lambda accs, ex_tiles: [ex_tiles[0] + accs[0]],
        extras=[(x0, (tmm, tn), lambda i, j: (i, j))], chunk=512, comm=ex))[0]
    h2 = _norm_fwd("norm_fwd", x1, sm["norm_ffn"])

    def ffn_ep(accs, ex):
        a, b = accs
        sg = _sigmoid(a)
        silu = a * sg
        return [b * sg * (1.0 + a * (1.0 - sg)), silu, silu * b]

    dfa, dfb, f = hosted("mm_ffn_in", lambda ex: _mm_nn_cols("mm_ffn_in", h2, [W("w_ff_gate"), W("w_ff_up")], [bf16] * 3, ffn_ep,
                                                         tr=True, comm=ex))
    w_down = W("w_ff_down").reshape(-1, D)
    F = w_down.shape[0]
    thin = _tile(D, 512)
    x2 = hosted("mm_ffn_out", lambda ex: _matmul(
        "mm_ffn_out", [(f, w_down, "nn", 0)], S, D, tmm, thin, 1, [((S, D), f32, (tmm, thin), lambda i, j: (i, j))],
        lambda accs, ex_tiles: [ex_tiles[0] + accs[0]], extras=[(x1, (tmm, thin), lambda i, j: (i, j))], comm=ex))[0]
    h3 = _norm_fwd("norm_fwd", x2, sm["norm_ple"])

    e = _mm_nn_cols("mm_ple_emb", p_i, [W("w_ple")], [bf16], same, tm_pref=1024)[0]

    def ple_ep(accs, ex):
        gp = accs[0]
        return [ex[0] + _sigmoid(gp) * ex[1].astype(f32), gp]

    x3, gp = hosted("mm_ple", lambda ex: _matmul(
        "mm_ple", [(h3, W("w_ple_gate").reshape(D, D), "nn", 0)], S, D, tmm, tn, 1,
        [((S, D), f32, (tmm, tn), lambda i, j: (i, j)), ((S, D), bf16, (tmm, tn), lambda i, j: (i, j))],
        ple_ep, extras=[(x2, (tmm, tn), lambda i, j: (i, j)), (e, (tmm, tn), lambda i, j: (i, j))], chunk=512, comm=ex))
    saved = dict(x0=x0, h1=h1, z=z, qkv=qkv, attn=attn, lse=lse, ya=ya, yb=yb, sg=sg, merged=merged, x1=x1,
                 h2=h2, dfa=dfa, dfb=dfb, f=f, x2=x2, h3=h3, gp=gp, e=e, sgw=sgw, bb=bb, IN=IN)
    return x3, saved


def _layer_bwd(dx3, p_i, W, sm, tabs, sv, c_idx):
    S, D = dx3.shape
    w_out, w_down, w_pg = W["w_out"].reshape(D, D), W["w_ff_down"].reshape(-1, D), W["w_ple_gate"].reshape(D, D)
    F = w_down.shape[0]
    cos, sin = tabs
    tmm = _tile(S, 1024, 8)
    tn = _tile(D, 512)
    reduced = {}

    def to_sibling(grads):
        return _sibling_exchange(list(grads.values()))

    def chip_sums(grads, recv):
        return {n: _rs_chip_sum("rs_chip_sum_" + n, grads[n], r, c_idx) for n, r in zip(grads, recv)}

    def to_owners(p4, names=None):
        return _chips_exchange([p4[n] for n in (names or p4)])

    def carry(p4, outs, names=None):
        reduced.update({n: (p4[n], r) for n, r in zip(names or p4, outs)})

    def blocks(full):
        return full.reshape(N_DEV, full.shape[0] // N_DEV, full.shape[1])

    dgp, de = _ple_bwd_ew(dx3, sv["gp"], sv["e"])
    d_w_ple = _mm_tn_cols("mm_dw_ple", p_i, [de], D // N_DEV)[0]
    d_w_pg = blocks(_mm_simple("mm_dw_dd", sv["h3"], dgp, "tn", bf16, 1024, 1024, 2048))
    dh3 = _mm_simple("mm_dh_dd", dgp, w_pg, "nt", bf16, 1024, 1024, 2048)
    dx2, dx2b, dg_ple = _norm_bwd("norm_bwd", dh3, sv["x2"], sm["norm_ple"], dx3)
    tf = _tile(F, 1408)

    def ffn_bwd_ep(accs, ex):
        df = accs[0]
        return [df * ex[0].astype(f32), df * ex[1].astype(f32)]

    th = _tile(S, 512, 8)
    da, db = _matmul("mm_dffn", [(dx2b, w_down, "nt", 0)], S, F, th, tf, 1,
                     [((S, F), bf16, (th, tf), lambda i, j: (i, j))] * 2, ffn_bwd_ep,
                     extras=[(sv["dfa"], (th, tf), lambda i, j: (i, j)), (sv["dfb"], (th, tf), lambda i, j: (i, j))], chunk=512)
    d_w_down = blocks(_mm_simple("mm_dw_fd", sv["f"], dx2b, "tn", bf16, 1408, 1024, 2048))
    g_a = dict(w_ff_down=d_w_down, w_ple_gate=d_w_pg, w_ple=d_w_ple)
    (d_w_gate, d_w_up), recv = _mm_tn_cols("mm_dw_df", sv["h2"], [da, db], F // N_DEV, tr=True, comm=to_sibling(g_a))
    p4_a = chip_sums(g_a, recv)
    g_b = dict(w_ff_gate=d_w_gate, w_ff_up=d_w_up)
    dh2, outs = _mm_nt_cols("mm_dh_ffn", [(da, W["w_ff_gate"]), (db, W["w_ff_up"])], bf16, tr=True,
                            comm=_both(to_owners(p4_a), to_sibling(g_b)))
    carry(p4_a, outs[:len(p4_a)])
    p4_b = chip_sums(g_b, outs[len(p4_a):])
    dx1, dx1b, dg_ffn = _norm_bwd("norm_bwd", dh2, sv["x1"], sm["norm_ffn"], dx2)
    z = sv["z"]
    o_a, o_b = (QKV_W + 2 * SG_W) // tn, (QKV_W + 2 * SG_W + D) // tn

    def merge_bwd_ep(accs, ex):
        dm = accs[0]
        ga, gb = _sigmoid(ex[0].astype(f32)), _sigmoid(ex[1].astype(f32))
        ya, yb = ex[2].astype(f32), ex[3].astype(f32)
        return [dm * ya * ga * (1.0 - ga), dm * yb * gb * (1.0 - gb), dm * ga, dm * gb]

    dga, dgb, dya, dyb = _matmul(
        "mm_dmerge", [(dx1b, w_out, "nt", 0)], S, D, tmm, tn, 1,
        [((S, D), bf16, (tmm, tn), lambda i, j: (i, j))] * 4, merge_bwd_ep,
        extras=[(z, (tmm, tn), lambda i, j: (i, o_a + j)), (z, (tmm, tn), lambda i, j: (i, o_b + j)),
                (sv["ya"], (tmm, tn), lambda i, j: (i, j)), (sv["yb"], (tmm, tn), lambda i, j: (i, j))], chunk=256)
    d_w_out = blocks(_mm_simple("mm_dw_dd", sv["merged"], dx1b, "tn", bf16, 1024, 1024, 2048))
    dsg = _mm_nt_cols("mm_dsg", [(dyb, W["w_br_sg"])], bf16)
    d_w_bsg = _mm_tn_cols("mm_dw_bsg", sv["sg"], [dyb], D // N_DEV)[0]
    dattn = _mm_nt_cols("mm_dattn", [(dya, W["w_br_attn"])], bf16)
    d_w_battn = _mm_tn_cols("mm_dw_battn", sv["attn"], [dya], D // N_DEV)[0]
    sgwt = jnp.swapaxes(sm["sg_w"], 1, 2).astype(bf16)
    du, dv_sg, d_sgw, d_sgb, d_lg, d_lb = _sg_bwd(dsg, z, sv["sgw"], sgwt, sv["bb"], sm["sg_ln_g"], sm["sg_ln_b"])
    b3 = _bwd_groups(sv["attn"], dattn, sv["lse"])
    g_c = dict(w_out=d_w_out, w_br_sg=d_w_bsg, w_br_attn=d_w_battn)
    dqg, outs = _attn_bwd_dq(sv["qkv"], cos, sin, b3, comm=_both(to_owners(p4_b, ["w_ff_gate"]), to_sibling(g_c)))
    carry(p4_b, outs[:1], ["w_ff_gate"])
    p4_c = chip_sums(g_c, outs[1:])
    (dkg, dvg), outs = _attn_bwd_dkv(sv["qkv"], cos, sin, b3, comm=to_owners(p4_b, ["w_ff_up"]))
    carry(p4_b, outs, ["w_ff_up"])
    dz = jnp.concatenate([_to_natural(dqg), _to_natural(dkg), _to_natural(dvg), du, dv_sg, dga, dgb], axis=1)
    (d_w_in,), outs = _mm_tn_cols("mm_dw_in", sv["h1"], [dz], sv["IN"] // N_DEV, tr=True, comm=to_owners(p4_c))
    carry(p4_c, outs)
    g_d = dict(w_in=d_w_in)
    p4_d = chip_sums(g_d, _run_exchange("rs_sibling_w_in", to_sibling(g_d)))
    dh1, outs = _mm_nt_cols("mm_dh_in", [(dz, W["w_in"])], bf16, tr=True, comm=to_owners(p4_d))
    carry(p4_d, outs)
    dx0, _, dg_mix = _norm_bwd("norm_bwd", dh1, sv["x0"], sm["norm_mix"], dx1)
    small = dict(sg_w=d_sgw, sg_b=d_sgb[:, :, 0], sg_ln_g=d_lg[0], sg_ln_b=d_lb[0], norm_mix=dg_mix[0], norm_ffn=dg_ffn[0],
                 norm_ple=dg_ple[0])
    return dx0, reduced, small


_SMALL = ("sg_w", "sg_b", "sg_ln_g", "sg_ln_b", "norm_mix", "norm_ffn", "norm_ple", "norm_final")


SMALL_ROWS = 256


def _pack_small(parts, tail):
    rows = [parts[n].astype(f32).reshape(-1, LANES) for n in _SMALL] + [tail]
    n = sum(r.shape[0] for r in rows)
    return jnp.concatenate(rows + [jnp.zeros((-n % SMALL_ROWS, LANES), f32)], axis=0)


def kernel(x, p, w_in, w_br_attn, w_br_sg, w_out, sg_w, sg_b, sg_ln_g, sg_ln_b, norm_mix, norm_ffn, norm_ple, norm_final, w_ff_gate, w_ff_up, w_ff_down, w_ple_gate, w_ple, loss_target, m_w_in, m_w_br_attn, m_w_br_sg, m_w_out, m_sg_w, m_sg_b, m_sg_ln_g, m_sg_ln_b, m_norm_mix, m_norm_ffn, m_norm_ple, m_norm_final, m_w_ff_gate, m_w_ff_up, m_w_ff_down, m_w_ple_gate, m_w_ple, v_w_in, v_w_br_attn, v_w_br_sg, v_w_out, v_sg_w, v_sg_b, v_sg_ln_g, v_sg_ln_b, v_norm_mix, v_norm_ffn, v_norm_ple, v_norm_final, v_w_ff_gate, v_w_ff_up, v_w_ff_down, v_w_ple_gate, v_w_ple):
    wts = dict(w_in=w_in, w_br_attn=w_br_attn, w_br_sg=w_br_sg, w_out=w_out, w_ff_gate=w_ff_gate, w_ff_up=w_ff_up,
               w_ff_down=w_ff_down, w_ple_gate=w_ple_gate, w_ple=w_ple)
    mom_m = dict(w_in=m_w_in, w_br_attn=m_w_br_attn, w_br_sg=m_w_br_sg, w_out=m_w_out, w_ff_gate=m_w_ff_gate,
                 w_ff_up=m_w_ff_up, w_ff_down=m_w_ff_down, w_ple_gate=m_w_ple_gate, w_ple=m_w_ple)
    mom_v = dict(w_in=v_w_in, w_br_attn=v_w_br_attn, w_br_sg=v_w_br_sg, w_out=v_w_out, w_ff_gate=v_w_ff_gate,
                 w_ff_up=v_w_ff_up, w_ff_down=v_w_ff_down, w_ple_gate=v_w_ple_gate, w_ple=v_w_ple)
    small_w = dict(sg_w=sg_w, sg_b=sg_b, sg_ln_g=sg_ln_g, sg_ln_b=sg_ln_b, norm_mix=norm_mix, norm_ffn=norm_ffn,
                   norm_ple=norm_ple, norm_final=norm_final)
    small_m = dict(sg_w=m_sg_w, sg_b=m_sg_b, sg_ln_g=m_sg_ln_g, sg_ln_b=m_sg_ln_b, norm_mix=m_norm_mix, norm_ffn=m_norm_ffn,
                   norm_ple=m_norm_ple, norm_final=m_norm_final)
    small_v = dict(sg_w=v_sg_w, sg_b=v_sg_b, sg_ln_g=v_sg_ln_g, sg_ln_b=v_sg_ln_b, norm_mix=v_norm_mix, norm_ffn=v_norm_ffn,
                   norm_ple=v_norm_ple, norm_final=v_norm_final)
    depth = w_in.shape[0]
    S = x.shape[1]
    names = [n for n, _ in _BIG]
    c_idx = lax.axis_index("c").astype(jnp.int32).reshape(1)
    q_idx = (2 * lax.axis_index("x") + lax.axis_index("y")).astype(jnp.int32).reshape(1)
    tabs = _rope_tables(S)

    def turned(n, t):
        return jnp.swapaxes(t, -1, -2) if n in _TURNED else t

    arrived = {}

    def gather(keys):
        shards, parts, into = [], [], []
        for key in keys:
            s = turned(key[1], wts[key[1]][key[0]]).astype(bf16)
            shards.append(s)
            half = s.shape[0] // 2
            parts.append((key[2] * half, half) if len(key) == 3 else None)
            into.append(arrived.get(key[:2]) if len(key) == 3 else None)
        return _gather_exchange(shards, parts, into), [key[:2] for key in keys]

    ex, keys = gather([(0, "w_in")])
    arrived.update(zip(keys, _run_exchange("ag_w_in", ex)))

    xs = x[0]
    saved = []
    for i in range(depth):
        sm = {n: small_w[n][i] for n in _SMALL if n != "norm_final"}
        comm = {carrier: functools.partial(gather, keys) for carrier, keys in _gather_plan(i, depth).items()}
        xs, sv = _layer_fwd(xs, p[i, 0], i, arrived, sm, tabs, comm)
        saved.append(sv)
    dx, dg_final, loss_part = _loss_head(xs, norm_final, loss_target[0])

    reduced = [None] * depth
    small_parts = [None] * depth
    for i in reversed(range(depth)):
        sm = {n: small_w[n][i] for n in _SMALL if n != "norm_final"}
        dx, reduced[i], small_parts[i] = _layer_bwd(dx, p[i, 0], {n: arrived[(i, n)] for n in names}, sm, tabs, saved[i], c_idx)
    grad_x = dx[None]

    parts = {n: jnp.stack([small_parts[i][n] for i in range(depth)]) for n in _SMALL if n != "norm_final"}
    parts["norm_final"] = dg_final[0]
    gathered = _all_gather("ag_small", _pack_small(parts, loss_part), in_vmem=True)
    g_s, d_s, nm_s, nv_s = _small_sum_adamw(gathered, _pack_small(small_w, jnp.zeros((8, LANES), f32)),
                                            _pack_small(small_m, jnp.zeros((8, LANES), f32)),
                                            _pack_small(small_v, jnp.ones((8, LANES), f32)))
    loss = g_s[sum(small_w[n].size for n in _SMALL) // LANES, 0]

    def unpack_small(flat):
        out, off = {}, 0
        for n in _SMALL:
            k = small_w[n].size // LANES
            out[n] = flat[off:off + k].reshape(small_w[n].shape)
            off += k
        return out

    sm_g, sm_d, sm_nm, sm_nv = unpack_small(g_s), unpack_small(d_s), unpack_small(nm_s), unpack_small(nv_s)

    big_g, big_d, big_nm, big_nv = {}, {}, {}, {}
    for k, n in enumerate(names):
        outs = None
        for i in range(depth):
            p4, recv2 = reduced[i][n]
            outs = _adamw_layer(f"adamw_{n}_{i}", i, turned(n, wts[n]), turned(n, mom_m[n]), turned(n, mom_v[n]), p4, recv2,
                                q_idx, outs)
        big_g[n], big_d[n], big_nm[n], big_nv[n] = [turned(n, o) for o in outs]

    order = ["w_in", "w_br_attn", "w_br_sg", "w_out", "sg_w", "sg_b", "sg_ln_g", "sg_ln_b", "norm_mix", "norm_ffn", "norm_ple",
             "norm_final", "w_ff_gate", "w_ff_up", "w_ff_down", "w_ple_gate", "w_ple"]

    def pick(big, small):
        return [big[n] if n in big else small[n] for n in order]

    return (loss, grad_x, *pick(big_g, sm_g), *pick(big_d, sm_d), *pick(big_nm, sm_nm), *pick(big_nv, sm_nv))
```

```python
import functools
import math

import jax
import jax.numpy as jnp
from jax import lax
from jax.experimental import pallas as pl
from jax.experimental.pallas import tpu as pltpu

f32 = jnp.float32
bf16 = jnp.bfloat16

HEAD_DIM = 128
N_GROUPS = 3
HEADS = 4
DILATIONS = (1, 4, 16)
RADIUS = 64
BLK = 128
QKV_W = 3 * N_GROUPS * HEADS * HEAD_DIM
ATTN_W = HEADS * HEAD_DIM
SG_CHUNK = 128
SG_GROUPS = 8
SG_W = SG_GROUPS * 128
ROPE_THETA = 10000.0
EPS = 1e-6
NEG = -1e30
N_DEV = 8
LANES = 128

ADAM_LR = 0.001
ADAM_B1 = 0.9
ADAM_B2 = 0.999
ADAM_EPS = 1e-08
ADAM_WD = 0.01
ADAM_STEP = 10

VMEM_LIMIT = 56 * 1024 * 1024
MESH = pl.DeviceIdType.MESH

NN = (((1,), (0,)), ((), ()))
NT = (((1,), (1,)), ((), ()))
TN = (((0,), (0,)), ((), ()))
_DN = {"nn": NN, "nt": NT, "tn": TN}


def _cp(*sem):
    return pltpu.CompilerParams(dimension_semantics=sem, vmem_limit_bytes=VMEM_LIMIT)


def _tile(n, pref, unit=128):
    if n <= pref:
        return n
    t = (pref // unit) * unit
    while t >= unit:
        if n % t == 0:
            return t
        t -= unit
    return n


_ANY = pl.BlockSpec(memory_space=pl.ANY)


def _call(body, *, name, grid, in_specs, out_specs, out_shape, operands, scratch=(), comm=None):
    in_specs, out_specs, out_shape, scratch = list(in_specs), list(out_specs), list(out_shape), list(scratch)
    if comm is None:
        res = pl.pallas_call(
            body, name=name, out_shape=out_shape, grid=grid, in_specs=in_specs, out_specs=out_specs, scratch_shapes=scratch,
            compiler_params=_cp(*(("arbitrary",) * len(grid))))(*operands)
        return res, []
    n_in, n_out, n_scr = len(in_specs), len(out_specs), len(scratch)
    c_in, c_out = len(comm.ins), len(comm.outs)

    def hosted(*refs):
        own_in, refs = refs[:n_in], refs[n_in:]
        ex_in, refs = refs[:c_in], refs[c_in:]
        own_out, refs = refs[:n_out], refs[n_out:]
        ex_out, refs = refs[:c_out], refs[c_out:]
        own_scr, sems = refs[:n_scr], refs[n_scr:]
        ids = [pl.program_id(a) for a in range(len(grid))]
        first = functools.reduce(jnp.logical_and, [i == 0 for i in ids])
        last = functools.reduce(jnp.logical_and, [i == g - 1 for i, g in zip(ids, grid)])

        @pl.when(first)
        def _():
            comm.start(ex_in, ex_out, sems)

        body(*own_in, *own_out, *own_scr)

        @pl.when(last)
        def _():
            comm.finish(ex_in, ex_out, sems)

    res = pl.pallas_call(
        hosted, name=name, out_shape=out_shape + list(comm.outs), grid=grid,
        in_specs=in_specs + [_ANY] * c_in, out_specs=out_specs + [_ANY] * c_out, scratch_shapes=scratch + list(comm.sems),
        input_output_aliases={n_in + a: n_out + b for a, b in comm.aliases.items()},
        compiler_params=pltpu.CompilerParams(dimension_semantics=("arbitrary",) * len(grid), vmem_limit_bytes=VMEM_LIMIT,
                                             has_side_effects=True),
    )(*operands, *comm.ins)
    return res[:n_out], res[n_out:]


def _sigmoid(x):
    return 1.0 / (1.0 + jnp.exp(-x))


_GC = math.sqrt(2.0 / math.pi)
_GA = 0.044715


def _gelu(x):
    return 0.5 * x * (1.0 + jnp.tanh(_GC * (x + _GA * x * x * x)))


def _gelu_grad(x):
    t = jnp.tanh(_GC * (x + _GA * x * x * x))
    return 0.5 * (1.0 + t) + 0.5 * x * (1.0 - t * t) * _GC * (1.0 + 3.0 * _GA * x * x)


def _matmul(name, prods, M, N, tm, tn, nk, outs, epilogue, extras=(), n_acc=1, chunk=None, comm=None):
    in_specs, operands, metas = [], [], []
    for a, b, mode, acc in prods:
        if mode == "tn":
            tk = a.shape[0] // nk
            in_specs += [pl.BlockSpec((tk, tm), lambda i, j, k: (k, i)), pl.BlockSpec((tk, tn), lambda i, j, k: (k, j))]
        elif mode == "nt":
            tk = a.shape[1] // nk
            in_specs += [pl.BlockSpec((tm, tk), lambda i, j, k: (i, k)), pl.BlockSpec((tn, tk), lambda i, j, k: (j, k))]
        else:
            tk = a.shape[1] // nk
            in_specs += [pl.BlockSpec((tm, tk), lambda i, j, k: (i, k)), pl.BlockSpec((tk, tn), lambda i, j, k: (k, j))]
        operands += [a, b]
        metas.append((mode, acc))
    for arr, bshape, imap in extras:
        in_specs.append(pl.BlockSpec(bshape, functools.partial(lambda i, j, k, f: f(i, j), f=imap)))
        operands.append(arr)
    out_specs = [pl.BlockSpec(bs, functools.partial(lambda i, j, k, f: f(i, j), f=imap)) for _, _, bs, imap in outs]
    out_shape = [jax.ShapeDtypeStruct(s, d) for s, d, _, _ in outs]
    n_prod, n_ext, n_out = len(prods), len(extras), len(outs)

    def body(*refs):
        in_refs = refs[: 2 * n_prod]
        ex_refs = refs[2 * n_prod : 2 * n_prod + n_ext]
        out_refs = refs[2 * n_prod + n_ext : 2 * n_prod + n_ext + n_out]
        acc_refs = refs[2 * n_prod + n_ext + n_out :]

        def partials():
            res = [None] * n_acc
            for idx, (mode, acc) in enumerate(metas):
                a = in_refs[2 * idx][...].astype(bf16)
                b = in_refs[2 * idx + 1][...].astype(bf16)
                d = lax.dot_general(a, b, _DN[mode], preferred_element_type=f32)
                res[acc] = d if res[acc] is None else res[acc] + d
            return res

        def finish(accs):
            vals = epilogue(accs, [r[...] for r in ex_refs])
            for r, v in zip(out_refs, vals):
                r[...] = v.astype(r.dtype)

        if nk == 1:
            step = chunk or tn
            for c0 in range(0, tn, step):
                c1 = min(c0 + step, tn)
                res = [None] * n_acc
                for idx, (mode, acc) in enumerate(metas):
                    a = in_refs[2 * idx][...].astype(bf16)
                    b_ref = in_refs[2 * idx + 1]
                    b = (b_ref[c0:c1, :] if mode == "nt" else b_ref[:, c0:c1]).astype(bf16)
                    d = lax.dot_general(a, b, _DN[mode], preferred_element_type=f32)
                    res[acc] = d if res[acc] is None else res[acc] + d
                for r, v in zip(out_refs, epilogue(res, [r[:, c0:c1] for r in ex_refs])):
                    r[:, c0:c1] = v.astype(r.dtype)
        else:
            k = pl.program_id(2)
            parts = partials()

            @pl.when(k == 0)
            def _():
                for r, d in zip(acc_refs, parts):
                    r[...] = d

            @pl.when(k > 0)
            def _():
                for r, d in zip(acc_refs, parts):
                    r[...] += d

            @pl.when(k == nk - 1)
            def _():
                finish([r[...] for r in acc_refs])

    scratch = [pltpu.VMEM((tm, tn), f32) for _ in range(n_acc)] if nk > 1 else []
    res, ex = _call(body, name=name, grid=(M // tm, N // tn, nk), in_specs=in_specs, out_specs=out_specs,
                    out_shape=out_shape, operands=operands, scratch=scratch, comm=comm)
    return res if comm is None else (res, ex)


def _ident(accs, ex):
    return [accs[0]]


def _mm_simple(name, a, b, mode, out_dtype, tm_pref=1024, tn_pref=1024, tk_pref=1024, comm=None):
    if mode == "tn":
        K, M = a.shape
        N = b.shape[1]
    elif mode == "nt":
        M, K = a.shape
        N = b.shape[0]
    else:
        M, K = a.shape
        N = b.shape[1]
    tm, tn, tk = _tile(M, tm_pref), _tile(N, tn_pref), _tile(K, tk_pref)
    res = _matmul(name, [(a, b, mode, 0)], M, N, tm, tn, K // tk,
                  [((M, N), out_dtype, (tm, tn), lambda i, j: (i, j))], _ident, comm=comm)
    return res[0] if comm is None else (res[0][0], res[1])


def _group(c, width_pref=1024):
    g = LANES // math.gcd(c, LANES)
    while g < N_DEV and 2 * g * c <= width_pref:
        g *= 2
    return g


def _join(parts):
    return parts[0] if len(parts) == 1 else jnp.concatenate(parts, axis=1)


def _mm_nn_cols(name, a, gs_list, outs_dtypes, epilogue, tm_pref=512, width_pref=1024, tr=False, comm=None):
    M, K = a.shape
    c = gs_list[0].shape[1 if tr else 2]
    g = _group(c, width_pref)
    W = g * c
    tm = _tile(M, tm_pref, 8)
    n_g, n_out = len(gs_list), len(outs_dtypes)
    blk = (g, c, K) if tr else (g, K, c)

    def body(*refs):
        a_ref = refs[0]
        g_refs = refs[1:1 + n_g]
        out_refs = refs[1 + n_g:]
        av = a_ref[...].astype(bf16)
        cols = [epilogue([lax.dot_general(av, gr[s], NT if tr else NN, preferred_element_type=f32) for gr in g_refs], [])
                for s in range(g)]
        for n, r in enumerate(out_refs):
            r[...] = _join([cols[s][n].astype(r.dtype) for s in range(g)])

    tile = pl.BlockSpec((tm, W), lambda j, i: (i, j))
    res, ex = _call(
        body, name=name, out_shape=[jax.ShapeDtypeStruct((M, N_DEV * c), d) for d in outs_dtypes],
        grid=(N_DEV // g, M // tm),
        in_specs=[pl.BlockSpec((tm, K), lambda j, i: (i, 0))] + [pl.BlockSpec((None,) + blk, lambda j, i: (j, 0, 0, 0))] * n_g,
        out_specs=[tile] * n_out, operands=[a, *[gm.reshape((N_DEV // g,) + blk) for gm in gs_list]], comm=comm)
    return res if comm is None else (res, ex)


def _mm_nt_cols(name, pairs, out_dtype, tm_pref=1024, tn_pref=1024, width_pref=1024, tr=False, comm=None):
    M = pairs[0][0].shape[0]
    c, Kw = pairs[0][1].shape[1:][::1 if tr else -1]
    g = _group(c, width_pref)
    W = g * c
    tm, tn = _tile(M, tm_pref, 8), _tile(Kw, tn_pref)
    nk = N_DEV // g
    n_p = len(pairs)

    def body(*refs):
        o_ref, acc = refs[2 * n_p], refs[2 * n_p + 1]
        k = pl.program_id(2)
        tot = None
        for n in range(n_p):
            d_ref, g_ref = refs[2 * n], refs[2 * n + 1]
            for s in range(g):
                part = lax.dot_general(d_ref[:, s * c:(s + 1) * c], g_ref[s], NN if tr else NT, preferred_element_type=f32)
                tot = part if tot is None else tot + part

        @pl.when(k == 0)
        def _():
            acc[...] = tot

        @pl.when(k > 0)
        def _():
            acc[...] += tot

        @pl.when(k == nk - 1)
        def _():
            o_ref[...] = acc[...].astype(o_ref.dtype)

    in_specs, operands = [], []
    for d, gm in pairs:
        if tr:
            wspec, wview = pl.BlockSpec((None, g, c, tn), lambda i, j, k: (k, 0, 0, j)), gm.reshape(nk, g, c, Kw)
        else:
            wspec, wview = pl.BlockSpec((None, g, tn, c), lambda i, j, k: (k, 0, j, 0)), gm.reshape(nk, g, Kw, c)
        in_specs += [pl.BlockSpec((tm, W), lambda i, j, k: (i, k)), wspec]
        operands += [d, wview]
    res, ex = _call(
        body, name=name, out_shape=[jax.ShapeDtypeStruct((M, Kw), out_dtype)], grid=(M // tm, Kw // tn, nk),
        in_specs=in_specs, out_specs=[pl.BlockSpec((tm, tn), lambda i, j, k: (i, j))],
        scratch=[pltpu.VMEM((tm, tn), f32)], operands=operands, comm=comm)
    return res[0] if comm is None else (res[0], ex)


def _mm_tn_cols(name, x, ds, c, tm_pref=1024, tk_pref=1024, width_pref=1024, tr=False, comm=None):
    S, Kw = x.shape
    g = _group(c, width_pref)
    W = g * c
    tm, tk = _tile(Kw, tm_pref), _tile(S, tk_pref, 16)
    nk = S // tk
    n_d = len(ds)
    blk = (g, c, tm) if tr else (g, tm, c)
    full = (N_DEV // g, g, c, Kw) if tr else (N_DEV // g, g, Kw, c)

    def body(*refs):
        x_ref = refs[0]
        d_refs = refs[1:1 + n_d]
        o_refs = refs[1 + n_d:1 + 2 * n_d]
        accs = refs[1 + 2 * n_d:]
        k = pl.program_id(2)

        @pl.when(k == 0)
        def _():
            for acc in accs:
                acc[...] = jnp.zeros_like(acc)

        xv = x_ref[...].astype(bf16)
        for d_ref, acc in zip(d_refs, accs):
            for s in range(g):
                ds_ = d_ref[:, s * c:(s + 1) * c]
                acc[s] += lax.dot_general(ds_, xv, TN, preferred_element_type=f32) if tr else \
                    lax.dot_general(xv, ds_, TN, preferred_element_type=f32)

        @pl.when(k == nk - 1)
        def _():
            for o_ref, acc in zip(o_refs, accs):
                o_ref[...] = acc[...].astype(o_ref.dtype)

    out_map = (lambda i, j, k: (j, 0, 0, i)) if tr else (lambda i, j, k: (j, 0, i, 0))
    outs, ex = _call(
        body, name=name, out_shape=[jax.ShapeDtypeStruct(full, bf16)] * n_d, grid=(Kw // tm, N_DEV // g, nk),
        in_specs=[pl.BlockSpec((tk, tm), lambda i, j, k: (k, i))] + [pl.BlockSpec((tk, W), lambda i, j, k: (k, j))] * n_d,
        out_specs=[pl.BlockSpec((None,) + blk, out_map)] * n_d,
        scratch=[pltpu.VMEM(blk, f32)] * n_d, operands=[x, *ds], comm=comm)
    outs = [o.reshape((N_DEV,) + full[2:]) for o in outs]
    return outs if comm is None else (outs, ex)


def _norm_fwd(name, x, g):
    S, D = x.shape
    tm = _tile(S, 512, 8)

    def body(x_ref, g_ref, h_ref):
        xv = x_ref[...]
        r = lax.rsqrt(jnp.mean(xv * xv, axis=-1, keepdims=True) + EPS)
        h_ref[...] = (xv * r * g_ref[...]).astype(bf16)

    return pl.pallas_call(
        body, name=name, out_shape=jax.ShapeDtypeStruct((S, D), bf16), grid=(S // tm,),
        in_specs=[pl.BlockSpec((tm, D), lambda i: (i, 0)), pl.BlockSpec((1, D), lambda i: (0, 0))],
        out_specs=pl.BlockSpec((tm, D), lambda i: (i, 0)), compiler_params=_cp("parallel"),
    )(x, g.reshape(1, D))


def _norm_bwd(name, dh, x, g, dx_in):
    S, D = x.shape
    tm = _tile(S, 256, 8)

    def body(dh_ref, x_ref, g_ref, dxi_ref, dx_ref, dxb_ref, dg_ref):
        i = pl.program_id(0)
        xv = x_ref[...]
        r = lax.rsqrt(jnp.mean(xv * xv, axis=-1, keepdims=True) + EPS)
        xh = xv * r
        dhv = dh_ref[...].astype(f32)
        dxh = dhv * g_ref[...]
        dx = dxi_ref[...] + r * (dxh - xh * jnp.mean(dxh * xh, axis=-1, keepdims=True))
        dx_ref[...] = dx
        dxb_ref[...] = dx.astype(bf16)

        @pl.when(i == 0)
        def _():
            dg_ref[...] = jnp.zeros_like(dg_ref)

        dg_ref[...] += jnp.sum(dhv * xh, axis=0, keepdims=True)

    row = pl.BlockSpec((tm, D), lambda i: (i, 0))
    vec = pl.BlockSpec((1, D), lambda i: (0, 0))
    return pl.pallas_call(
        body, name=name,
        out_shape=[jax.ShapeDtypeStruct((S, D), f32), jax.ShapeDtypeStruct((S, D), bf16), jax.ShapeDtypeStruct((1, D), f32)],
        grid=(S // tm,), in_specs=[row, row, vec, row], out_specs=[row, row, vec], compiler_params=_cp("arbitrary"),
    )(dh, x, g.reshape(1, D), dx_in)


def _loss_head(x, g, t):
    S, D = x.shape
    tm = _tile(S, 256, 8)

    def body(x_ref, g_ref, t_ref, dx_ref, dg_ref, loss_ref):
        i = pl.program_id(0)
        xv = x_ref[...]
        r = lax.rsqrt(jnp.mean(xv * xv, axis=-1, keepdims=True) + EPS)
        xh = xv * r
        gv = g_ref[...]
        err = xh * gv - t_ref[...]
        dy = err * (1.0 / D)
        dxh = dy * gv
        dx_ref[...] = r * (dxh - xh * jnp.mean(dxh * xh, axis=-1, keepdims=True))

        @pl.when(i == 0)
        def _():
            dg_ref[...] = jnp.zeros_like(dg_ref)
            loss_ref[...] = jnp.zeros_like(loss_ref)

        dg_ref[...] += jnp.sum(dy * xh, axis=0, keepdims=True)
        row = jnp.sum(err * err, axis=-1, keepdims=True) * (0.5 / D)
        loss_ref[...] += jnp.broadcast_to(jnp.sum(row, axis=0, keepdims=True), loss_ref.shape)

    return pl.pallas_call(
        body, name="loss_head",
        out_shape=[jax.ShapeDtypeStruct((S, D), f32), jax.ShapeDtypeStruct((1, D), f32), jax.ShapeDtypeStruct((8, LANES), f32)],
        grid=(S // tm,),
        in_specs=[pl.BlockSpec((tm, D), lambda i: (i, 0)), pl.BlockSpec((1, D), lambda i: (0, 0)), pl.BlockSpec((tm, D), lambda i: (i, 0))],
        out_specs=[pl.BlockSpec((tm, D), lambda i: (i, 0)), pl.BlockSpec((1, D), lambda i: (0, 0)), pl.BlockSpec((8, LANES), lambda i: (0, 0))],
        compiler_params=_cp("arbitrary"),
    )(x, g.reshape(1, D), t)


def _perm(t, d):
    if d == 1:
        return t
    S, C = t.shape
    return t.reshape(S // d, d, C).transpose(1, 0, 2).reshape(S, C)


def _rope_tables(S):
    half = HEAD_DIM // 2
    pos = jnp.arange(S, dtype=f32)
    inv_freq = ROPE_THETA ** (-jnp.arange(0, HEAD_DIM, 2, dtype=f32) / HEAD_DIM)
    ang = pos[:, None] * inv_freq[None, :]
    c, s = jnp.cos(ang), jnp.sin(ang)
    cos2 = jnp.concatenate([c, c], axis=-1)
    sin2 = jnp.concatenate([-s, s], axis=-1)
    assert cos2.shape == (S, 2 * half)
    return (jnp.stack([_perm(cos2, d) for d in DILATIONS]), jnp.stack([_perm(sin2, d) for d in DILATIONS]))


def _rope(t, c, s):
    return t * c + pltpu.roll(t, HEAD_DIM // 2, 1) * s


def _rope_bwd(dt, c, s):
    return dt * c - pltpu.roll(dt, HEAD_DIM // 2, 1) * s


def _band_bounds(i, nblk):
    g = pl.program_id(0)
    lb = jnp.right_shift(jnp.int32(nblk), 2 * g)
    pos = lax.rem(i, lb)
    lo = jnp.where(pos == 0, BLK, 0)
    hi = jnp.where(pos == lb - 1, 2 * BLK, 3 * BLK)
    return lo, hi


def _cur_spec(width, t=None):
    if t is None:
        return pl.BlockSpec((None, 2 * BLK, width), lambda g, i: (g, i, 0))
    return pl.BlockSpec((None, None, 2 * BLK, width), lambda g, i: (t, g, i, 0))


def _band_specs(width, nblk, t=None):
    lo, hi = (lambda i: jnp.maximum(2 * i - 1, 0)), (lambda i: jnp.minimum(2 * i + 2, nblk - 1))
    if t is None:
        return [pl.BlockSpec((None, BLK, width), lambda g, i: (g, lo(i), 0)), _cur_spec(width),
                pl.BlockSpec((None, BLK, width), lambda g, i: (g, hi(i), 0))]
    return [pl.BlockSpec((None, None, BLK, width), lambda g, i: (t, g, lo(i), 0)), _cur_spec(width, t),
            pl.BlockSpec((None, None, BLK, width), lambda g, i: (t, g, hi(i), 0))]


def _band(sub, prev, pair, nxt, cols=slice(None)):
    pieces = [(prev, slice(None)), (pair, slice(0, BLK)), (pair, slice(BLK, 2 * BLK)), (nxt, slice(None))][sub:sub + 3]
    return jnp.concatenate([ref[rows, cols] for ref, rows in pieces], axis=0)


_ROWS = 2048


def _to_scratch(scr, val):
    for h in range(HEADS):
        scr[h] = val[:, h * HEAD_DIM:(h + 1) * HEAD_DIM].astype(f32)


def _qkv_groups(z):
    S = z.shape[0]
    R = min(_ROWS, S)
    nb = S // R

    def body(x_ref, o_ref, scr):
        g, i = pl.program_id(1), pl.program_id(2)
        _to_scratch(scr, x_ref[...])
        for gi, d in enumerate(DILATIONS):
            @pl.when(g == gi)
            def _():
                n, L = R // d, S // d
                for r in range(d):
                    start = pl.multiple_of(r * L + i * n, 16)
                    for h in range(HEADS):
                        o_ref[pl.ds(start, n), h * HEAD_DIM:(h + 1) * HEAD_DIM] = scr[h, pl.ds(r, n, stride=d), :].astype(bf16)

    return pl.pallas_call(
        body, name="qkv_groups", out_shape=jax.ShapeDtypeStruct((3, N_GROUPS, S, ATTN_W), bf16), grid=(3, N_GROUPS, nb),
        in_specs=[pl.BlockSpec((R, ATTN_W), lambda t, g, i: (i, t * N_GROUPS + g))],
        out_specs=pl.BlockSpec((None, None, S, ATTN_W), lambda t, g, i: (t, g, 0, 0)),
        scratch_shapes=[pltpu.VMEM((HEADS, R, HEAD_DIM), f32)], compiler_params=_cp("arbitrary", "arbitrary", "arbitrary"),
    )(z)


def _bwd_groups(attn, dattn, lse):
    S = attn.shape[0]
    R = min(_ROWS, S)
    nb = S // R

    def body(a_ref, d_ref, l_ref, o_ref, scr):
        t, g, i = pl.program_id(0), pl.program_id(1), pl.program_id(2)

        @pl.when(t == 0)
        def _():
            _to_scratch(scr, d_ref[...])

        @pl.when(t == 1)
        def _():
            _to_scratch(scr, l_ref[...])

        @pl.when(t == 2)
        def _():
            prod = a_ref[...].astype(f32) * d_ref[...].astype(f32)
            for h in range(HEADS):
                part = jnp.sum(prod[:, h * HEAD_DIM:(h + 1) * HEAD_DIM], axis=-1, keepdims=True)
                scr[h] = jnp.broadcast_to(part, (R, HEAD_DIM))

        for gi, d in enumerate(DILATIONS):
            @pl.when(g == gi)
            def _():
                n, L = R // d, S // d
                for r in range(d):
                    start = pl.multiple_of(r * L + i * n, 8)
                    for h in range(HEADS):
                        o_ref[pl.ds(start, n), h * HEAD_DIM:(h + 1) * HEAD_DIM] = scr[h, pl.ds(r, n, stride=d), :]

    def nat(used):
        return pl.BlockSpec((R, ATTN_W), lambda t, g, i: (jnp.where(used(t), i, 0), 0))

    return pl.pallas_call(
        body, name="bwd_groups", out_shape=jax.ShapeDtypeStruct((3, N_GROUPS, S, ATTN_W), f32), grid=(3, N_GROUPS, nb),
        in_specs=[nat(lambda t: t == 2), nat(lambda t: t != 1), nat(lambda t: t == 1)],
        out_specs=pl.BlockSpec((None, None, S, ATTN_W), lambda t, g, i: (t, g, 0, 0)),
        scratch_shapes=[pltpu.VMEM((HEADS, R, HEAD_DIM), f32)], compiler_params=_cp("arbitrary", "arbitrary", "arbitrary"),
    )(attn, dattn, lse)


def _group_views(t3, R):
    S = t3.shape[1]
    views = [t3.reshape(N_GROUPS, d, S // d, ATTN_W) for d in DILATIONS]
    specs = [pl.BlockSpec((None, d, R // d, ATTN_W), functools.partial(lambda i, g: (g, 0, i, 0), g=g))
             for g, d in enumerate(DILATIONS)]
    return views, specs


def _from_groups(scr, ref, d):
    n = ref.shape[1]
    for r in range(d):
        blk = ref[r]
        for h in range(HEADS):
            scr[h, pl.ds(r, n, stride=d), :] = blk[:, h * HEAD_DIM:(h + 1) * HEAD_DIM].astype(f32)


def _to_natural(t3):
    S = t3.shape[1]
    R = min(_ROWS // 2, S)
    views, specs = _group_views(t3, R)

    def body(v0, v1, v2, o_ref, scr):
        for g, (ref, d) in enumerate(zip((v0, v1, v2), DILATIONS)):
            _from_groups(scr, ref, d)
            for h in range(HEADS):
                o_ref[:, g * ATTN_W + h * HEAD_DIM:g * ATTN_W + (h + 1) * HEAD_DIM] = scr[h].astype(bf16)

    return pl.pallas_call(
        body, name="to_natural", out_shape=jax.ShapeDtypeStruct((S, N_GROUPS * ATTN_W), bf16), grid=(S // R,),
        in_specs=specs, out_specs=pl.BlockSpec((R, N_GROUPS * ATTN_W), lambda i: (i, 0)),
        scratch_shapes=[pltpu.VMEM((HEADS, R, HEAD_DIM), f32)], compiler_params=_cp("arbitrary"),
    )(*views)


_SCALE = HEAD_DIM ** -0.5


def _attn_fwd(qkv, cos, sin, comm=None):
    _, _, S, W = qkv.shape
    nblk = S // BLK

    def body(q_ref, kp, kc, kn, vp, vc, vn, cq, sq, ckp, ckc, ckn, skp, skc, skn, o_ref, lse_ref):
        i = pl.program_id(1)
        a = lax.broadcasted_iota(jnp.int32, (BLK, 3 * BLK), 0)
        b = lax.broadcasted_iota(jnp.int32, (BLK, 3 * BLK), 1)
        for sub in range(2):
            rows = slice(sub * BLK, (sub + 1) * BLK)
            lo, hi = _band_bounds(2 * i + sub, nblk)
            mask = (jnp.abs(b - BLK - a) <= RADIUS) & (b >= lo) & (b < hi)
            ck, sk = _band(sub, ckp, ckc, ckn), _band(sub, skp, skc, skn)
            for hh in range(HEADS):
                sl = slice(hh * HEAD_DIM, (hh + 1) * HEAD_DIM)
                qh = _rope(q_ref[rows, sl].astype(f32), cq[rows, :], sq[rows, :]).astype(bf16)
                kh = _rope(_band(sub, kp, kc, kn, sl).astype(f32), ck, sk).astype(bf16)
                vh = _band(sub, vp, vc, vn, sl)
                s = lax.dot_general(qh, kh, NT, preferred_element_type=f32) * _SCALE
                s = jnp.where(mask, s, NEG)
                m = jnp.max(s, axis=-1, keepdims=True)
                e = jnp.exp(s - m)
                den = jnp.sum(e, axis=-1, keepdims=True)
                o = lax.dot_general(e.astype(bf16), vh, NN, preferred_element_type=f32) * (1.0 / den)
                o_ref[rows, sl] = o.astype(bf16)
                lse_ref[rows, sl] = jnp.broadcast_to(m + jnp.log(den), (BLK, HEAD_DIM))

    blk = _cur_spec(W)
    tab = _cur_spec(HEAD_DIM)
    res, ex = _call(
        body, name="attn_fwd",
        out_shape=[jax.ShapeDtypeStruct((N_GROUPS, S, W), bf16), jax.ShapeDtypeStruct((N_GROUPS, S, W), f32)],
        grid=(N_GROUPS, nblk // 2),
        in_specs=[_cur_spec(W, 0)] + _band_specs(W, nblk, 1) + _band_specs(W, nblk, 2) + [tab, tab]
        + _band_specs(HEAD_DIM, nblk) * 2,
        out_specs=[blk, blk], operands=[qkv] * 7 + [cos, sin, cos, cos, cos, sin, sin, sin], comm=comm)
    return res if comm is None else (res, ex)


def _attn_combine(o3, lse3):
    _, S, W = o3.shape
    R = min(_ROWS // 2, S)
    o_views, specs = _group_views(o3, R)
    l_views, _ = _group_views(lse3, R)

    def body(o0, o1, o2, l0, l1, l2, attn_ref, lse_ref, so0, so1, so2, sl0, sl1, sl2):
        for ref, scr, d in zip((o0, o1, o2, l0, l1, l2), (so0, so1, so2, sl0, sl1, sl2), DILATIONS * 2):
            _from_groups(scr, ref, d)
        for h in range(HEADS):
            a0, a1, a2 = sl0[h], sl1[h], sl2[h]
            m = jnp.maximum(jnp.maximum(a0, a1), a2)
            w0, w1, w2 = jnp.exp(a0 - m), jnp.exp(a1 - m), jnp.exp(a2 - m)
            den = w0 + w1 + w2
            acc = w0 * so0[h] + w1 * so1[h] + w2 * so2[h]
            attn_ref[:, h * HEAD_DIM:(h + 1) * HEAD_DIM] = (acc * (1.0 / den)).astype(bf16)
            lse_ref[:, h * HEAD_DIM:(h + 1) * HEAD_DIM] = m + jnp.log(den)

    nat = pl.BlockSpec((R, W), lambda i: (i, 0))
    return pl.pallas_call(
        body, name="attn_combine", out_shape=[jax.ShapeDtypeStruct((S, W), bf16), jax.ShapeDtypeStruct((S, W), f32)],
        grid=(S // R,), in_specs=specs * 2, out_specs=[nat, nat],
        scratch_shapes=[pltpu.VMEM((HEADS, R, HEAD_DIM), f32)] * 6, compiler_params=_cp("arbitrary"),
    )(*o_views, *l_views)


def _attn_bwd_dq(qkv, cos, sin, b3, comm=None):
    _, _, S, W = qkv.shape
    nblk = S // BLK

    def body(q_ref, kp, kc, kn, vp, vc, vn, cq, sq, ckp, ckc, ckn, skp, skc, skn, da_ref, l_ref, dl_ref, dq_ref):
        i = pl.program_id(1)
        a = lax.broadcasted_iota(jnp.int32, (BLK, 3 * BLK), 0)
        b = lax.broadcasted_iota(jnp.int32, (BLK, 3 * BLK), 1)
        for sub in range(2):
            rows = slice(sub * BLK, (sub + 1) * BLK)
            lo, hi = _band_bounds(2 * i + sub, nblk)
            mask = (jnp.abs(b - BLK - a) <= RADIUS) & (b >= lo) & (b < hi)
            ck, sk = _band(sub, ckp, ckc, ckn), _band(sub, skp, skc, skn)
            for hh in range(HEADS):
                sl = slice(hh * HEAD_DIM, (hh + 1) * HEAD_DIM)
                qh = _rope(q_ref[rows, sl].astype(f32), cq[rows, :], sq[rows, :]).astype(bf16)
                kh = _rope(_band(sub, kp, kc, kn, sl).astype(f32), ck, sk).astype(bf16)
                vh = _band(sub, vp, vc, vn, sl)
                s = lax.dot_general(qh, kh, NT, preferred_element_type=f32) * _SCALE
                lh = l_ref[rows, sl]
                l3 = jnp.concatenate([lh, lh, lh], axis=1)
                p = jnp.exp(jnp.where(mask, s - l3, NEG))
                dp = lax.dot_general(da_ref[rows, sl].astype(bf16), vh, NT, preferred_element_type=f32)
                dh = dl_ref[rows, sl]
                ds = p * (dp - jnp.concatenate([dh, dh, dh], axis=1))
                dqh = lax.dot_general(ds.astype(bf16), kh, NN, preferred_element_type=f32) * _SCALE
                dq_ref[rows, sl] = _rope_bwd(dqh, cq[rows, :], sq[rows, :]).astype(bf16)

    tab = _cur_spec(HEAD_DIM)
    res, ex = _call(
        body, name="attn_bwd_dq", out_shape=[jax.ShapeDtypeStruct((N_GROUPS, S, W), bf16)], grid=(N_GROUPS, nblk // 2),
        in_specs=[_cur_spec(W, 0)] + _band_specs(W, nblk, 1) + _band_specs(W, nblk, 2) + [tab, tab]
        + _band_specs(HEAD_DIM, nblk) * 2 + [_cur_spec(W, 0), _cur_spec(W, 1), _cur_spec(W, 2)],
        out_specs=[_cur_spec(W)], operands=[qkv] * 7 + [cos, sin, cos, cos, cos, sin, sin, sin, b3, b3, b3], comm=comm)
    return res[0] if comm is None else (res[0], ex)


def _attn_bwd_dkv(qkv, cos, sin, b3, comm=None):
    _, _, S, W = qkv.shape
    nblk = S // BLK

    def body(k_ref, v_ref, ck, sk, qp, qc, qn, cqp, cqc, cqn, sqp, sqc, sqn, dap, dac, dan, lp, lc, ln, dlp, dlc, dln,
             dk_ref, dv_ref):
        j = pl.program_id(1)
        a = lax.broadcasted_iota(jnp.int32, (3 * BLK, BLK), 0)
        b = lax.broadcasted_iota(jnp.int32, (3 * BLK, BLK), 1)
        for sub in range(2):
            rows = slice(sub * BLK, (sub + 1) * BLK)
            lo, hi = _band_bounds(2 * j + sub, nblk)
            mask = (jnp.abs(b - (a - BLK)) <= RADIUS) & (a >= lo) & (a < hi)
            cq, sq = _band(sub, cqp, cqc, cqn), _band(sub, sqp, sqc, sqn)
            for hh in range(HEADS):
                sl = slice(hh * HEAD_DIM, (hh + 1) * HEAD_DIM)
                kh = _rope(k_ref[rows, sl].astype(f32), ck[rows, :], sk[rows, :]).astype(bf16)
                vh = v_ref[rows, sl]
                qh = _rope(_band(sub, qp, qc, qn, sl).astype(f32), cq, sq).astype(bf16)
                dah = _band(sub, dap, dac, dan, sl).astype(bf16)
                lh = _band(sub, lp, lc, ln, sl)
                dlh = _band(sub, dlp, dlc, dln, sl)
                s = lax.dot_general(qh, kh, NT, preferred_element_type=f32) * _SCALE
                p = jnp.exp(jnp.where(mask, s - lh, NEG))
                dv_ref[rows, sl] = lax.dot_general(p.astype(bf16), dah, TN, preferred_element_type=f32).astype(bf16)
                dp = lax.dot_general(dah, vh, NT, preferred_element_type=f32)
                ds = p * (dp - dlh)
                dkh = lax.dot_general(ds.astype(bf16), qh, TN, preferred_element_type=f32) * _SCALE
                dk_ref[rows, sl] = _rope_bwd(dkh, ck[rows, :], sk[rows, :]).astype(bf16)

    blk, tab, bt = _cur_spec(W), _cur_spec(HEAD_DIM), _band_specs(HEAD_DIM, nblk)
    res, ex = _call(
        body, name="attn_bwd_dkv",
        out_shape=[jax.ShapeDtypeStruct((N_GROUPS, S, W), bf16), jax.ShapeDtypeStruct((N_GROUPS, S, W), bf16)],
        grid=(N_GROUPS, nblk // 2),
        in_specs=[_cur_spec(W, 1), _cur_spec(W, 2), tab, tab] + _band_specs(W, nblk, 0) + bt + bt
        + _band_specs(W, nblk, 0) + _band_specs(W, nblk, 1) + _band_specs(W, nblk, 2), out_specs=[blk, blk],
        operands=[qkv, qkv, cos, sin, qkv, qkv, qkv, cos, cos, cos, sin, sin, sin] + [b3] * 9, comm=comm)
    return res if comm is None else (res, ex)


_SG_ROWS = 512


def _sg_z_specs(tm, half):
    o = QKV_W // half
    return [pl.BlockSpec((tm, half), functools.partial(lambda i, c: (i, c), c=o + n)) for n in range(4)]


def _sg_norm(v, lg, lb):
    gv = _gelu(v)
    mu = jnp.mean(gv, axis=-1, keepdims=True)
    xc = gv - mu
    rstd = lax.rsqrt(jnp.mean(xc * xc, axis=-1, keepdims=True) + EPS)
    xh = xc * rstd
    return xh, rstd, xh * lg + lb


def _sg_fwd(z, w, bb, lg, lb):
    S = z.shape[0]
    tm = _tile(S, _SG_ROWS, SG_CHUNK)
    half = SG_W // 2

    def body(u0, u1, v0, v1, w_ref, bb_ref, lg_ref, lb_ref, o_ref):
        u = jnp.concatenate([u0[...], u1[...]], axis=1).astype(f32)
        v = jnp.concatenate([v0[...], v1[...]], axis=1).astype(f32)
        gu = _gelu(u)
        _, _, vn = _sg_norm(v, lg_ref[...], lb_ref[...])
        vnb = vn.astype(bf16)
        for c in range(tm // SG_CHUNK):
            rs = slice(c * SG_CHUNK, (c + 1) * SG_CHUNK)
            for g in range(SG_GROUPS):
                cs = slice(g * 128, (g + 1) * 128)
                mixed = lax.dot_general(w_ref[g], vnb[rs, cs], NN, preferred_element_type=f32) + bb_ref[g]
                o_ref[rs, cs] = (gu[rs, cs] * mixed).astype(bf16)

    full3 = pl.BlockSpec((SG_GROUPS, 128, 128), lambda i: (0, 0, 0))
    vec = pl.BlockSpec((1, SG_W), lambda i: (0, 0))
    return pl.pallas_call(
        body, name="sg_fwd", out_shape=jax.ShapeDtypeStruct((S, SG_W), bf16), grid=(S // tm,),
        in_specs=_sg_z_specs(tm, half) + [full3, full3, vec, vec],
        out_specs=pl.BlockSpec((tm, SG_W), lambda i: (i, 0)), compiler_params=_cp("parallel"),
    )(z, z, z, z, w, bb, lg.reshape(1, SG_W), lb.reshape(1, SG_W))


def _sg_bwd(dsg, z, w, wt, bb, lg, lb):
    S = z.shape[0]
    tm = _tile(S, _SG_ROWS, SG_CHUNK)
    half = SG_W // 2

    def body(d_ref, u0, u1, v0, v1, w_ref, wt_ref, bb_ref, lg_ref, lb_ref, du_ref, dv_ref, dw_ref, db_ref, dlg_ref, dlb_ref, dvn_scr):
        i = pl.program_id(0)

        @pl.when(i == 0)
        def _():
            dw_ref[...] = jnp.zeros_like(dw_ref)
            db_ref[...] = jnp.zeros_like(db_ref)
            dlg_ref[...] = jnp.zeros_like(dlg_ref)
            dlb_ref[...] = jnp.zeros_like(dlb_ref)

        u = jnp.concatenate([u0[...], u1[...]], axis=1).astype(f32)
        v = jnp.concatenate([v0[...], v1[...]], axis=1).astype(f32)
        gu = _gelu(u)
        dgu = _gelu_grad(u)
        xh, rstd, vn = _sg_norm(v, lg_ref[...], lb_ref[...])
        vnb = vn.astype(bf16)
        dsg_v = d_ref[...].astype(f32)
        for g in range(SG_GROUPS):
            cs = slice(g * 128, (g + 1) * 128)
            dw_g = jnp.zeros((128, 128), f32)
            db_g = jnp.zeros((128, 1), f32)
            for c in range(tm // SG_CHUNK):
                rs = slice(c * SG_CHUNK, (c + 1) * SG_CHUNK)
                ds = dsg_v[rs, cs]
                mixed = lax.dot_general(w_ref[g], vnb[rs, cs], NN, preferred_element_type=f32) + bb_ref[g]
                du_ref[rs, cs] = (ds * mixed * dgu[rs, cs]).astype(bf16)
                dmix = ds * gu[rs, cs]
                dmb = dmix.astype(bf16)
                dw_g = dw_g + lax.dot_general(dmb, vnb[rs, cs], NT, preferred_element_type=f32)
                db_g = db_g + jnp.sum(dmix, axis=-1, keepdims=True)
                dvn_scr[rs, cs] = lax.dot_general(wt_ref[g], dmb, NN, preferred_element_type=f32)
            dw_ref[g] += dw_g
            db_ref[g] += jnp.broadcast_to(db_g, (128, 128))
        dvn = dvn_scr[...]
        dlg_ref[...] += jnp.sum(dvn * xh, axis=0, keepdims=True)
        dlb_ref[...] += jnp.sum(dvn, axis=0, keepdims=True)
        dxh = dvn * lg_ref[...]
        dgv = rstd * (dxh - jnp.mean(dxh, axis=-1, keepdims=True) - xh * jnp.mean(dxh * xh, axis=-1, keepdims=True))
        dv_ref[...] = (dgv * _gelu_grad(v)).astype(bf16)

    full3 = pl.BlockSpec((SG_GROUPS, 128, 128), lambda i: (0, 0, 0))
    vec = pl.BlockSpec((1, SG_W), lambda i: (0, 0))
    row = pl.BlockSpec((tm, SG_W), lambda i: (i, 0))
    return pl.pallas_call(
        body, name="sg_bwd",
        out_shape=[jax.ShapeDtypeStruct((S, SG_W), bf16), jax.ShapeDtypeStruct((S, SG_W), bf16),
                   jax.ShapeDtypeStruct((SG_GROUPS, 128, 128), f32), jax.ShapeDtypeStruct((SG_GROUPS, 128, 128), f32),
                   jax.ShapeDtypeStruct((1, SG_W), f32), jax.ShapeDtypeStruct((1, SG_W), f32)],
        grid=(S // tm,),
        in_specs=[row] + _sg_z_specs(tm, half) + [full3, full3, full3, vec, vec],
        out_specs=[row, row, full3, full3, vec, vec],
        scratch_shapes=[pltpu.VMEM((tm, SG_W), f32)], compiler_params=_cp("arbitrary"),
    )(dsg, z, z, z, z, w, wt, bb, lg.reshape(1, SG_W), lb.reshape(1, SG_W))


def _merge_fwd(z, ya, yb, D):
    S = z.shape[0]
    tm, tc = _tile(S, 512, 8), _tile(D, 512)
    o_a, o_b = (QKV_W + 2 * SG_W) // tc, (QKV_W + 2 * SG_W + D) // tc

    def body(ga_ref, gb_ref, ya_ref, yb_ref, o_ref):
        ga = _sigmoid(ga_ref[...].astype(f32))
        gb = _sigmoid(gb_ref[...].astype(f32))
        o_ref[...] = (ga * ya_ref[...].astype(f32) + gb * yb_ref[...].astype(f32)).astype(bf16)

    blk = pl.BlockSpec((tm, tc), lambda i, j: (i, j))
    return pl.pallas_call(
        body, name="merge_fwd", out_shape=jax.ShapeDtypeStruct((S, D), bf16), grid=(S // tm, D // tc),
        in_specs=[pl.BlockSpec((tm, tc), lambda i, j: (i, o_a + j)), pl.BlockSpec((tm, tc), lambda i, j: (i, o_b + j)), blk, blk],
        out_specs=blk, compiler_params=_cp("parallel", "parallel"),
    )(z, z, ya, yb)


def _ple_bwd_ew(dx, gp, e):
    S, D = dx.shape
    tm, tc = _tile(S, 512, 8), _tile(D, 1024)

    def body(dx_ref, gp_ref, e_ref, dgp_ref, de_ref):
        dxv = dx_ref[...]
        sg = _sigmoid(gp_ref[...].astype(f32))
        dgp_ref[...] = (dxv * e_ref[...].astype(f32) * sg * (1.0 - sg)).astype(bf16)
        de_ref[...] = (dxv * sg).astype(bf16)

    blk = pl.BlockSpec((tm, tc), lambda i, j: (i, j))
    return pl.pallas_call(
        body, name="ple_bwd_ew", out_shape=[jax.ShapeDtypeStruct((S, D), bf16)] * 2, grid=(S // tm, D // tc),
        in_specs=[blk, blk, blk], out_specs=[blk, blk], compiler_params=_cp("parallel", "parallel"),
    )(dx, gp, e)


def _adam_math(w, g, m, v):
    m = ADAM_B1 * m + (1.0 - ADAM_B1) * g
    v = ADAM_B2 * v + (1.0 - ADAM_B2) * (g * g)
    m_hat = m / (1.0 - ADAM_B1 ** ADAM_STEP)
    v_hat = v / (1.0 - ADAM_B2 ** ADAM_STEP)
    delta = -ADAM_LR * (m_hat / (jnp.sqrt(v_hat) + ADAM_EPS) + ADAM_WD * w)
    return delta, m, v


def _small_sum_adamw(gathered, w, m, v):
    _, R, _ = gathered.shape
    tr = _tile(R, 1024, SMALL_ROWS)

    def body(p_ref, w_ref, m_ref, v_ref, g_ref, d_ref, nm_ref, nv_ref):
        g = p_ref[0]
        for n in range(1, N_DEV):
            g = g + p_ref[n]
        d, nm, nv = _adam_math(w_ref[...], g, m_ref[...], v_ref[...])
        g_ref[...] = g
        d_ref[...] = d
        nm_ref[...] = nm
        nv_ref[...] = nv

    blk = pl.BlockSpec((tr, LANES), lambda i: (i, 0))
    return pl.pallas_call(
        body, name="small_sum_adamw", out_shape=[jax.ShapeDtypeStruct((R, LANES), f32)] * 4, grid=(R // tr,),
        in_specs=[pl.BlockSpec((N_DEV, tr, LANES), lambda i: (0, i, 0)), blk, blk, blk], out_specs=[blk] * 4,
        compiler_params=_cp("parallel"),
    )(gathered, w, m, v)


def _all_gather(name, shard, in_vmem=False):
    R, C = shard.shape

    def body(x_ref, out_ref, send_sems, recv_sems, local_sem):
        x, y, c = lax.axis_index("x"), lax.axis_index("y"), lax.axis_index("c")
        me, sibling = (x, y, c), (x, y, 1 - c)
        chips = [(1 - x, y), (x, 1 - y), (1 - x, 1 - y)]

        def rows(px, py, pc):
            return out_ref.at[4 * px + 2 * py + pc]

        def copy(k, block, to, src=None):
            return pltpu.make_async_remote_copy(
                src_ref=rows(*block) if src is None else src, dst_ref=rows(*block),
                send_sem=send_sems.at[k], recv_sem=recv_sems.at[k], device_id=to, device_id_type=MESH)

        mine = pltpu.make_async_copy(x_ref, rows(*me), local_sem)
        mine.start()
        first = [copy(0, me, sibling, src=x_ref)]
        first += [copy(1 + j, me, (*chip, c), src=x_ref) for j, chip in enumerate(chips)]
        for cp in first:
            cp.start()
        passed = [copy(4 + j, (*chip, c), sibling) for j, chip in enumerate(chips)]
        for j, chip in enumerate(chips):
            copy(1 + j, (*chip, c), me).wait_recv()
            passed[j].start()
        copy(0, sibling, me).wait_recv()
        for j, chip in enumerate(chips):
            copy(4 + j, (*chip, 1 - c), me).wait_recv()
        for cp in first + passed:
            cp.wait_send()
        mine.wait()

    space = pl.BlockSpec(memory_space=pltpu.VMEM) if in_vmem else _ANY
    return pl.pallas_call(
        body, name=name, out_shape=jax.ShapeDtypeStruct((N_DEV, R, C), shard.dtype),
        in_specs=[space], out_specs=space,
        scratch_shapes=[pltpu.SemaphoreType.DMA((7,)), pltpu.SemaphoreType.DMA((7,)), pltpu.SemaphoreType.DMA],
        compiler_params=pltpu.CompilerParams(has_side_effects=True, vmem_limit_bytes=VMEM_LIMIT),
    )(shard)


class _Exchange:
    def __init__(self, ins, outs, sems, start, finish, aliases=None):
        self.ins, self.outs, self.sems, self.start, self.finish = list(ins), list(outs), list(sems), start, finish
        self.aliases = dict(aliases or {})


def _run_exchange(name, ex):
    c_in, c_out = len(ex.ins), len(ex.outs)

    def body(*refs):
        ins, outs, sems = refs[:c_in], refs[c_in:c_in + c_out], refs[c_in + c_out:]
        ex.start(ins, outs, sems)
        ex.finish(ins, outs, sems)

    return pl.pallas_call(
        body, name=name, out_shape=ex.outs, in_specs=[_ANY] * c_in, out_specs=[_ANY] * c_out, scratch_shapes=ex.sems,
        input_output_aliases=ex.aliases,
        compiler_params=pltpu.CompilerParams(has_side_effects=True, vmem_limit_bytes=VMEM_LIMIT),
    )(*ex.ins)


def _both(e1, e2):
    i1, o1, s1 = len(e1.ins), len(e1.outs), len(e1.sems)

    def start(ins, outs, sems):
        e1.start(ins[:i1], outs[:o1], sems[:s1])
        e2.start(ins[i1:], outs[o1:], sems[s1:])

    def finish(ins, outs, sems):
        e1.finish(ins[:i1], outs[:o1], sems[:s1])
        e2.finish(ins[i1:], outs[o1:], sems[s1:])

    aliases = dict(e1.aliases)
    aliases.update({i1 + a: o1 + b for a, b in e2.aliases.items()})
    return _Exchange(e1.ins + e2.ins, e1.outs + e2.outs, e1.sems + e2.sems, start, finish, aliases)


def _gather_exchange(shards, parts=None, into=None):
    n = len(shards)
    parts = parts or [None] * n
    into = into or [None] * n
    given = [w for w in range(n) if into[w] is not None]

    def plan(ins, outs, sems):
        send_sems, recv_sems, local_sems = sems
        x, y, c = lax.axis_index("x"), lax.axis_index("y"), lax.axis_index("c")
        me, sibling = (x, y, c), (x, y, 1 - c)
        chips = [(1 - x, y), (x, 1 - y), (1 - x, 1 - y)]

        def cut(ref, w):
            return ref if parts[w] is None else ref.at[pl.ds(parts[w][0], parts[w][1])]

        def rows(w, px, py, pc):
            return cut(outs[w].at[4 * px + 2 * py + pc], w)

        def copy(w, k, block, to, src=None):
            return pltpu.make_async_remote_copy(
                src_ref=rows(w, *block) if src is None else src, dst_ref=rows(w, *block),
                send_sem=send_sems.at[w, k], recv_sem=recv_sems.at[w, k], device_id=to, device_id_type=MESH)

        mine = [pltpu.make_async_copy(cut(ins[w], w), rows(w, *me), local_sems.at[w]) for w in range(n)]
        first = []
        for w in range(n):
            first.append(copy(w, 0, me, sibling, src=cut(ins[w], w)))
            first += [copy(w, 1 + j, me, (*chip, c), src=cut(ins[w], w)) for j, chip in enumerate(chips)]
        return c, me, sibling, chips, copy, mine, first

    def start(ins, outs, sems):
        _, _, _, _, _, mine, first = plan(ins, outs, sems)
        for cp in mine + first:
            cp.start()

    def finish(ins, outs, sems):
        c, me, sibling, chips, copy, mine, first = plan(ins, outs, sems)
        passed = []
        for w in range(n):
            for j, chip in enumerate(chips):
                copy(w, 1 + j, (*chip, c), me).wait_recv()
                passed.append(copy(w, 4 + j, (*chip, c), sibling))
                passed[-1].start()
        for w in range(n):
            copy(w, 0, sibling, me).wait_recv()
            for j, chip in enumerate(chips):
                copy(w, 4 + j, (*chip, 1 - c), me).wait_recv()
        for cp in first + passed:
            cp.wait_send()
        for cp in mine:
            cp.wait()

    return _Exchange(
        list(shards) + [into[w] for w in given], [jax.ShapeDtypeStruct((N_DEV,) + s.shape, s.dtype) for s in shards],
        [pltpu.SemaphoreType.DMA((n, 7)), pltpu.SemaphoreType.DMA((n, 7)), pltpu.SemaphoreType.DMA((n,))], start, finish,
        {n + k: w for k, w in enumerate(given)})


def _sibling_exchange(gs):
    n = len(gs)

    def copies(ins, outs, sems):
        send_sems, recv_sems = sems
        x, y, c = lax.axis_index("x"), lax.axis_index("y"), lax.axis_index("c")
        return [pltpu.make_async_remote_copy(
            src_ref=ins[w].at[2 * q + (1 - c)], dst_ref=outs[w].at[q], send_sem=send_sems.at[w, q],
            recv_sem=recv_sems.at[w, q], device_id=(x, y, 1 - c), device_id_type=MESH) for w in range(n) for q in range(4)]

    def start(ins, outs, sems):
        for cp in copies(ins, outs, sems):
            cp.start()

    def finish(ins, outs, sems):
        cps = copies(ins, outs, sems)
        for cp in cps:
            cp.wait_recv()
        for cp in cps:
            cp.wait_send()

    return _Exchange(gs, [jax.ShapeDtypeStruct((4,) + g.shape[1:], g.dtype) for g in gs],
                     [pltpu.SemaphoreType.DMA((n, 4)), pltpu.SemaphoreType.DMA((n, 4))], start, finish)


def _rs_chip_sum(name, g8, recv, c_idx):
    _, R, C = g8.shape
    tr = _tile(R, 512, 16)
    g42 = g8.reshape(4, 2, R, C)

    def body(c_ref, a_ref, b_ref, o_ref):
        o_ref[...] = (a_ref[...].astype(f32) + b_ref[...].astype(f32)).astype(o_ref.dtype)

    return pl.pallas_call(
        body, name=name, out_shape=jax.ShapeDtypeStruct((4, R, C), g8.dtype),
        grid_spec=pltpu.PrefetchScalarGridSpec(
            num_scalar_prefetch=1, grid=(4, R // tr),
            in_specs=[pl.BlockSpec((None, None, tr, C), lambda q, r, c_ref: (q, c_ref[0], r, 0)),
                      pl.BlockSpec((None, tr, C), lambda q, r, c_ref: (q, r, 0))],
            out_specs=pl.BlockSpec((None, tr, C), lambda q, r, c_ref: (q, r, 0))),
        compiler_params=_cp("parallel", "parallel"),
    )(c_idx, g42, recv)


def _chips_exchange(p4s):
    n = len(p4s)

    def copies(ins, outs, sems):
        send_sems, recv_sems = sems
        x, y, c = lax.axis_index("x"), lax.axis_index("y"), lax.axis_index("c")
        chips = [(1 - x, y), (x, 1 - y), (1 - x, 1 - y)]
        return [pltpu.make_async_remote_copy(
            src_ref=ins[w].at[2 * cx + cy], dst_ref=outs[w].at[k], send_sem=send_sems.at[w, k],
            recv_sem=recv_sems.at[w, k], device_id=(cx, cy, c), device_id_type=MESH)
            for w in range(n) for k, (cx, cy) in enumerate(chips)]

    def start(ins, outs, sems):
        for cp in copies(ins, outs, sems):
            cp.start()

    def finish(ins, outs, sems):
        cps = copies(ins, outs, sems)
        for cp in cps:
            cp.wait_recv()
        for cp in cps:
            cp.wait_send()

    return _Exchange(p4s, [jax.ShapeDtypeStruct((3,) + p.shape[1:], p.dtype) for p in p4s],
                     [pltpu.SemaphoreType.DMA((n, 3)), pltpu.SemaphoreType.DMA((n, 3))], start, finish)


def _adamw_layer(name, layer, w, m, v, p4, recv, q_idx, prev):
    depth, R, C = w.shape
    tr = _tile(R, 256, 8)

    def body(q_ref, w_ref, m_ref, v_ref, a_ref, b_ref, *rest):
        g_ref, d_ref, nm_ref, nv_ref = rest[-4:]
        g = ((a_ref[...].astype(f32) + b_ref[0].astype(f32)) + b_ref[1].astype(f32)) + b_ref[2].astype(f32)
        d, nm, nv = _adam_math(w_ref[...], g, m_ref[...], v_ref[...])
        g_ref[...] = g
        d_ref[...] = d
        nm_ref[...] = nm
        nv_ref[...] = nv

    lay = pl.BlockSpec((None, tr, C), lambda i, q_ref: (layer, i, 0))
    n_prev = 0 if prev is None else 4
    return pl.pallas_call(
        body, name=name, out_shape=[jax.ShapeDtypeStruct((depth, R, C), f32)] * 4,
        grid_spec=pltpu.PrefetchScalarGridSpec(
            num_scalar_prefetch=1, grid=(R // tr,),
            in_specs=[lay, lay, lay, pl.BlockSpec((None, tr, C), lambda i, q_ref: (q_ref[0], i, 0)),
                      pl.BlockSpec((3, tr, C), lambda i, q_ref: (0, i, 0))] + [_ANY] * n_prev,
            out_specs=[lay] * 4),
        input_output_aliases={6 + n: n for n in range(n_prev)},
        compiler_params=_cp("parallel"),
    )(q_idx, w, m, v, p4, recv, *(prev or ()))


_BIG = (("w_in", 1), ("w_br_attn", 1), ("w_br_sg", 1), ("w_out", 0), ("w_ff_gate", 1), ("w_ff_up", 1),
        ("w_ff_down", 0), ("w_ple_gate", 0), ("w_ple", 1))


_TURNED = ("w_in", "w_ff_gate", "w_ff_up")


def _gather_plan(i, depth):
    mixer = ["w_br_attn", "w_br_sg", "w_out"]
    plan = {
        "mm_in": [(i, "w_ff_gate")] + [(i, n) for n in mixer],
        "attn_fwd": [(i, "w_ff_up")],
        "mm_out": [(i, "w_ple_gate"), (i, "w_ple")],
        "mm_ffn_in": [(i, "w_ff_down")],
    }
    if i + 1 < depth:
        plan["mm_ffn_in"] = plan["mm_ffn_in"] + [(i + 1, "w_in", 0)]
        plan["mm_ffn_out"] = [(i + 1, "w_in", 1)]
    return plan


def _layer_fwd(x0, p_i, layer, arrived, sm, tabs, comm):
    S, D = x0.shape
    cos, sin = tabs
    tmm = _tile(S, 1024, 8)
    same = lambda accs, ex: accs

    def W(name):
        return arrived[(layer, name)]

    def hosted(key, fn):
        if key not in comm:
            return fn(None)
        ex, keys = comm[key]()
        res, outs = fn(ex)
        arrived.update(zip(keys, outs))
        return res

    h1 = _norm_fwd("norm_fwd", x0, sm["norm_mix"])
    z = hosted("mm_in", lambda ex: _mm_nn_cols("mm_in", h1, [W("w_in")], [bf16], same, tr=True, comm=ex))[0]
    IN = z.shape[1]

    qkv = _qkv_groups(z)
    o3, lse3 = hosted("attn_fwd", lambda ex: _attn_fwd(qkv, cos, sin, comm=ex))
    attn, lse = _attn_combine(o3, lse3)
    ya = _mm_nn_cols("mm_br_attn", attn, [W("w_br_attn")], [bf16], same, tm_pref=1024)[0]
    sgw = sm["sg_w"].astype(bf16)
    bb = jnp.broadcast_to(sm["sg_b"][:, :, None], (SG_GROUPS, SG_CHUNK, 128))
    sg = _sg_fwd(z, sgw, bb, sm["sg_ln_g"], sm["sg_ln_b"])
    yb = _mm_nn_cols("mm_br_sg", sg, [W("w_br_sg")], [bf16], same, tm_pref=1024)[0]
    merged = _merge_fwd(z, ya, yb, D)
    tn = _tile(D, 1024)
    x1 = hosted("mm_out", lambda ex: _matmul(
        "mm_out", [(merged, W("w_out").reshape(D, D), "nn", 0)], S, D, tmm, tn, 1,
        [((S, D), f32, (tmm, tn), lambda i, j: (i, j))], lambda accs, ex_tiles: [ex_tiles[0] + accs[0]],
        extras=[(x0, (tmm, tn), lambda i, j: (i, j))], chunk=512, comm=ex))[0]
    h2 = _norm_fwd("norm_fwd", x1, sm["norm_ffn"])

    def ffn_ep(accs, ex):
        a, b = accs
        sg = _sigmoid(a)
        silu = a * sg
        return [b * sg * (1.0 + a * (1.0 - sg)), silu, silu * b]

    dfa, dfb, f = hosted("mm_ffn_in", lambda ex: _mm_nn_cols("mm_ffn_in", h2, [W("w_ff_gate"), W("w_ff_up")], [bf16] * 3, ffn_ep,
                                                         tr=True, comm=ex))
    w_down = W("w_ff_down").reshape(-1, D)
    F = w_down.shape[0]
    thin = _tile(D, 512)
    x2 = hosted("mm_ffn_out", lambda ex: _matmul(
        "mm_ffn_out", [(f, w_down, "nn", 0)], S, D, tmm, thin, 1, [((S, D), f32, (tmm, thin), lambda i, j: (i, j))],
        lambda accs, ex_tiles: [ex_tiles[0] + accs[0]], extras=[(x1, (tmm, thin), lambda i, j: (i, j))], comm=ex))[0]
    h3 = _norm_fwd("norm_fwd", x2, sm["norm_ple"])

    e = _mm_nn_cols("mm_ple_emb", p_i, [W("w_ple")], [bf16], same, tm_pref=1024)[0]

    def ple_ep(accs, ex):
        gp = accs[0]
        return [ex[0] + _sigmoid(gp) * ex[1].astype(f32), gp]

    x3, gp = hosted("mm_ple", lambda ex: _matmul(
        "mm_ple", [(h3, W("w_ple_gate").reshape(D, D), "nn", 0)], S, D, tmm, tn, 1,
        [((S, D), f32, (tmm, tn), lambda i, j: (i, j)), ((S, D), bf16, (tmm, tn), lambda i, j: (i, j))],
        ple_ep, extras=[(x2, (tmm, tn), lambda i, j: (i, j)), (e, (tmm, tn), lambda i, j: (i, j))], chunk=512, comm=ex))
    saved = dict(x0=x0, h1=h1, z=z, qkv=qkv, attn=attn, lse=lse, ya=ya, yb=yb, sg=sg, merged=merged, x1=x1,
                 h2=h2, dfa=dfa, dfb=dfb, f=f, x2=x2, h3=h3, gp=gp, e=e, sgw=sgw, bb=bb, IN=IN)
    return x3, saved


def _layer_bwd(dx3, p_i, W, sm, tabs, sv, c_idx):
    S, D = dx3.shape
    w_out, w_down, w_pg = W["w_out"].reshape(D, D), W["w_ff_down"].reshape(-1, D), W["w_ple_gate"].reshape(D, D)
    F = w_down.shape[0]
    cos, sin = tabs
    tmm = _tile(S, 1024, 8)
    tn = _tile(D, 512)
    reduced = {}

    def to_sibling(grads):
        return _sibling_exchange(list(grads.values()))

    def chip_sums(grads, recv):
        return {n: _rs_chip_sum("rs_chip_sum_" + n, grads[n], r, c_idx) for n, r in zip(grads, recv)}

    def to_owners(p4, names=None):
        return _chips_exchange([p4[n] for n in (names or p4)])

    def carry(p4, outs, names=None):
        reduced.update({n: (p4[n], r) for n, r in zip(names or p4, outs)})

    def blocks(full):
        return full.reshape(N_DEV, full.shape[0] // N_DEV, full.shape[1])

    dgp, de = _ple_bwd_ew(dx3, sv["gp"], sv["e"])
    d_w_ple = _mm_tn_cols("mm_dw_ple", p_i, [de], D // N_DEV)[0]
    d_w_pg = blocks(_mm_simple("mm_dw_dd", sv["h3"], dgp, "tn", bf16, 1024, 1024, 2048))
    dh3 = _mm_simple("mm_dh_dd", dgp, w_pg, "nt", bf16, 1024, 1024, 2048)
    dx2, dx2b, dg_ple = _norm_bwd("norm_bwd", dh3, sv["x2"], sm["norm_ple"], dx3)
    tf = _tile(F, 1408)

    def ffn_bwd_ep(accs, ex):
        df = accs[0]
        return [df * ex[0].astype(f32), df * ex[1].astype(f32)]

    th = _tile(S, 512, 8)
    da, db = _matmul("mm_dffn", [(dx2b, w_down, "nt", 0)], S, F, th, tf, 1,
                     [((S, F), bf16, (th, tf), lambda i, j: (i, j))] * 2, ffn_bwd_ep,
                     extras=[(sv["dfa"], (th, tf), lambda i, j: (i, j)), (sv["dfb"], (th, tf), lambda i, j: (i, j))], chunk=512)
    d_w_down = blocks(_mm_simple("mm_dw_fd", sv["f"], dx2b, "tn", bf16, 1408, 1024, 2048))
    g_a = dict(w_ff_down=d_w_down, w_ple_gate=d_w_pg, w_ple=d_w_ple)
    (d_w_gate, d_w_up), recv = _mm_tn_cols("mm_dw_df", sv["h2"], [da, db], F // N_DEV, tr=True, comm=to_sibling(g_a))
    p4_a = chip_sums(g_a, recv)
    g_b = dict(w_ff_gate=d_w_gate, w_ff_up=d_w_up)
    dh2, outs = _mm_nt_cols("mm_dh_ffn", [(da, W["w_ff_gate"]), (db, W["w_ff_up"])], bf16, tr=True,
                            comm=_both(to_owners(p4_a), to_sibling(g_b)))
    carry(p4_a, outs[:len(p4_a)])
    p4_b = chip_sums(g_b, outs[len(p4_a):])
    dx1, dx1b, dg_ffn = _norm_bwd("norm_bwd", dh2, sv["x1"], sm["norm_ffn"], dx2)
    z = sv["z"]
    o_a, o_b = (QKV_W + 2 * SG_W) // tn, (QKV_W + 2 * SG_W + D) // tn

    def merge_bwd_ep(accs, ex):
        dm = accs[0]
        ga, gb = _sigmoid(ex[0].astype(f32)), _sigmoid(ex[1].astype(f32))
        ya, yb = ex[2].astype(f32), ex[3].astype(f32)
        return [dm * ya * ga * (1.0 - ga), dm * yb * gb * (1.0 - gb), dm * ga, dm * gb]

    dga, dgb, dya, dyb = _matmul(
        "mm_dmerge", [(dx1b, w_out, "nt", 0)], S, D, tmm, tn, 1,
        [((S, D), bf16, (tmm, tn), lambda i, j: (i, j))] * 4, merge_bwd_ep,
        extras=[(z, (tmm, tn), lambda i, j: (i, o_a + j)), (z, (tmm, tn), lambda i, j: (i, o_b + j)),
                (sv["ya"], (tmm, tn), lambda i, j: (i, j)), (sv["yb"], (tmm, tn), lambda i, j: (i, j))], chunk=256)
    d_w_out = blocks(_mm_simple("mm_dw_dd", sv["merged"], dx1b, "tn", bf16, 1024, 1024, 2048))
    dsg = _mm_nt_cols("mm_dsg", [(dyb, W["w_br_sg"])], bf16)
    d_w_bsg = _mm_tn_cols("mm_dw_bsg", sv["sg"], [dyb], D // N_DEV)[0]
    dattn = _mm_nt_cols("mm_dattn", [(dya, W["w_br_attn"])], bf16)
    d_w_battn = _mm_tn_cols("mm_dw_battn", sv["attn"], [dya], D // N_DEV)[0]
    sgwt = jnp.swapaxes(sm["sg_w"], 1, 2).astype(bf16)
    du, dv_sg, d_sgw, d_sgb, d_lg, d_lb = _sg_bwd(dsg, z, sv["sgw"], sgwt, sv["bb"], sm["sg_ln_g"], sm["sg_ln_b"])
    b3 = _bwd_groups(sv["attn"], dattn, sv["lse"])
    g_c = dict(w_out=d_w_out, w_br_sg=d_w_bsg, w_br_attn=d_w_battn)
    dqg, outs = _attn_bwd_dq(sv["qkv"], cos, sin, b3, comm=_both(to_owners(p4_b, ["w_ff_gate"]), to_sibling(g_c)))
    carry(p4_b, outs[:1], ["w_ff_gate"])
    p4_c = chip_sums(g_c, outs[1:])
    (dkg, dvg), outs = _attn_bwd_dkv(sv["qkv"], cos, sin, b3, comm=to_owners(p4_b, ["w_ff_up"]))
    carry(p4_b, outs, ["w_ff_up"])
    dz = jnp.concatenate([_to_natural(dqg), _to_natural(dkg), _to_natural(dvg), du, dv_sg, dga, dgb], axis=1)
    (d_w_in,), outs = _mm_tn_cols("mm_dw_in", sv["h1"], [dz], sv["IN"] // N_DEV, tr=True, comm=to_owners(p4_c))
    carry(p4_c, outs)
    g_d = dict(w_in=d_w_in)
    p4_d = chip_sums(g_d, _run_exchange("rs_sibling_w_in", to_sibling(g_d)))
    dh1, outs = _mm_nt_cols("mm_dh_in", [(dz, W["w_in"])], bf16, tr=True, comm=to_owners(p4_d))
    carry(p4_d, outs)
    dx0, _, dg_mix = _norm_bwd("norm_bwd", dh1, sv["x0"], sm["norm_mix"], dx1)
    small = dict(sg_w=d_sgw, sg_b=d_sgb[:, :, 0], sg_ln_g=d_lg[0], sg_ln_b=d_lb[0], norm_mix=dg_mix[0], norm_ffn=dg_ffn[0],
                 norm_ple=dg_ple[0])
    return dx0, reduced, small


_SMALL = ("sg_w", "sg_b", "sg_ln_g", "sg_ln_b", "norm_mix", "norm_ffn", "norm_ple", "norm_final")


SMALL_ROWS = 256


def _pack_small(parts, tail):
    rows = [parts[n].astype(f32).reshape(-1, LANES) for n in _SMALL] + [tail]
    n = sum(r.shape[0] for r in rows)
    return jnp.concatenate(rows + [jnp.zeros((-n % SMALL_ROWS, LANES), f32)], axis=0)


def kernel(x, p, w_in, w_br_attn, w_br_sg, w_out, sg_w, sg_b, sg_ln_g, sg_ln_b, norm_mix, norm_ffn, norm_ple, norm_final, w_ff_gate, w_ff_up, w_ff_down, w_ple_gate, w_ple, loss_target, m_w_in, m_w_br_attn, m_w_br_sg, m_w_out, m_sg_w, m_sg_b, m_sg_ln_g, m_sg_ln_b, m_norm_mix, m_norm_ffn, m_norm_ple, m_norm_final, m_w_ff_gate, m_w_ff_up, m_w_ff_down, m_w_ple_gate, m_w_ple, v_w_in, v_w_br_attn, v_w_br_sg, v_w_out, v_sg_w, v_sg_b, v_sg_ln_g, v_sg_ln_b, v_norm_mix, v_norm_ffn, v_norm_ple, v_norm_final, v_w_ff_gate, v_w_ff_up, v_w_ff_down, v_w_ple_gate, v_w_ple):
    wts = dict(w_in=w_in, w_br_attn=w_br_attn, w_br_sg=w_br_sg, w_out=w_out, w_ff_gate=w_ff_gate, w_ff_up=w_ff_up,
               w_ff_down=w_ff_down, w_ple_gate=w_ple_gate, w_ple=w_ple)
    mom_m = dict(w_in=m_w_in, w_br_attn=m_w_br_attn, w_br_sg=m_w_br_sg, w_out=m_w_out, w_ff_gate=m_w_ff_gate,
                 w_ff_up=m_w_ff_up, w_ff_down=m_w_ff_down, w_ple_gate=m_w_ple_gate, w_ple=m_w_ple)
    mom_v = dict(w_in=v_w_in, w_br_attn=v_w_br_attn, w_br_sg=v_w_br_sg, w_out=v_w_out, w_ff_gate=v_w_ff_gate,
                 w_ff_up=v_w_ff_up, w_ff_down=v_w_ff_down, w_ple_gate=v_w_ple_gate, w_ple=v_w_ple)
    small_w = dict(sg_w=sg_w, sg_b=sg_b, sg_ln_g=sg_ln_g, sg_ln_b=sg_ln_b, norm_mix=norm_mix, norm_ffn=norm_ffn,
                   norm_ple=norm_ple, norm_final=norm_final)
    small_m = dict(sg_w=m_sg_w, sg_b=m_sg_b, sg_ln_g=m_sg_ln_g, sg_ln_b=m_sg_ln_b, norm_mix=m_norm_mix, norm_ffn=m_norm_ffn,
                   norm_ple=m_norm_ple, norm_final=m_norm_final)
    small_v = dict(sg_w=v_sg_w, sg_b=v_sg_b, sg_ln_g=v_sg_ln_g, sg_ln_b=v_sg_ln_b, norm_mix=v_norm_mix, norm_ffn=v_norm_ffn,
                   norm_ple=v_norm_ple, norm_final=v_norm_final)
    depth = w_in.shape[0]
    S = x.shape[1]
    names = [n for n, _ in _BIG]
    c_idx = lax.axis_index("c").astype(jnp.int32).reshape(1)
    q_idx = (2 * lax.axis_index("x") + lax.axis_index("y")).astype(jnp.int32).reshape(1)
    tabs = _rope_tables(S)

    def turned(n, t):
        return jnp.swapaxes(t, -1, -2) if n in _TURNED else t

    arrived = {}

    def gather(keys):
        shards, parts, into = [], [], []
        for key in keys:
            s = turned(key[1], wts[key[1]][key[0]]).astype(bf16)
            shards.append(s)
            half = s.shape[0] // 2
            parts.append((key[2] * half, half) if len(key) == 3 else None)
            into.append(arrived.get(key[:2]) if len(key) == 3 else None)
        return _gather_exchange(shards, parts, into), [key[:2] for key in keys]

    ex, keys = gather([(0, "w_in")])
    arrived.update(zip(keys, _run_exchange("ag_w_in", ex)))

    xs = x[0]
    saved = []
    for i in range(depth):
        sm = {n: small_w[n][i] for n in _SMALL if n != "norm_final"}
        comm = {carrier: functools.partial(gather, keys) for carrier, keys in _gather_plan(i, depth).items()}
        xs, sv = _layer_fwd(xs, p[i, 0], i, arrived, sm, tabs, comm)
        saved.append(sv)
    dx, dg_final, loss_part = _loss_head(xs, norm_final, loss_target[0])

    reduced = [None] * depth
    small_parts = [None] * depth
    for i in reversed(range(depth)):
        sm = {n: small_w[n][i] for n in _SMALL if n != "norm_final"}
        dx, reduced[i], small_parts[i] = _layer_bwd(dx, p[i, 0], {n: arrived[(i, n)] for n in names}, sm, tabs, saved[i], c_idx)
    grad_x = dx[None]

    parts = {n: jnp.stack([small_parts[i][n] for i in range(depth)]) for n in _SMALL if n != "norm_final"}
    parts["norm_final"] = dg_final[0]
    gathered = _all_gather("ag_small", _pack_small(parts, loss_part), in_vmem=True)
    g_s, d_s, nm_s, nv_s = _small_sum_adamw(gathered, _pack_small(small_w, jnp.zeros((8, LANES), f32)),
                                            _pack_small(small_m, jnp.zeros((8, LANES), f32)),
                                            _pack_small(small_v, jnp.ones((8, LANES), f32)))
    loss = g_s[sum(small_w[n].size for n in _SMALL) // LANES, 0]

    def unpack_small(flat):
        out, off = {}, 0
        for n in _SMALL:
            k = small_w[n].size // LANES
            out[n] = flat[off:off + k].reshape(small_w[n].shape)
            off += k
        return out

    sm_g, sm_d, sm_nm, sm_nv = unpack_small(g_s), unpack_small(d_s), unpack_small(nm_s), unpack_small(nv_s)

    big_g, big_d, big_nm, big_nv = {}, {}, {}, {}
    for k, n in enumerate(names):
        outs = None
        for i in range(depth):
            p4, recv2 = reduced[i][n]
            outs = _adamw_layer(f"adamw_{n}_{i}", i, turned(n, wts[n]), turned(n, mom_m[n]), turned(n, mom_v[n]), p4, recv2,
                                q_idx, outs)
        big_g[n], big_d[n], big_nm[n], big_nv[n] = [turned(n, o) for o in outs]

    order = ["w_in", "w_br_attn", "w_br_sg", "w_out", "sg_w", "sg_b", "sg_ln_g", "sg_ln_b", "norm_mix", "norm_ffn", "norm_ple",
             "norm_final", "w_ff_gate", "w_ff_up", "w_ff_down", "w_ple_gate", "w_ple"]

    def pick(big, small):
        return [big[n] if n in big else small[n] for n in order]

    return (loss, grad_x, *pick(big_g, sm_g), *pick(big_d, sm_d), *pick(big_nm, sm_nm), *pick(big_nv, sm_nv))
```

```python
import functools
import math

import jax
import jax.numpy as jnp
from jax import lax
from jax.experimental import pallas as pl
from jax.experimental.pallas import tpu as pltpu

f32 = jnp.float32
bf16 = jnp.bfloat16

HEAD_DIM = 128
N_GROUPS = 3
HEADS = 4
DILATIONS = (1, 4, 16)
RADIUS = 64
BLK = 128
QKV_W = 3 * N_GROUPS * HEADS * HEAD_DIM
ATTN_W = HEADS * HEAD_DIM
SG_CHUNK = 128
SG_GROUPS = 8
SG_W = SG_GROUPS * 128
ROPE_THETA = 10000.0
EPS = 1e-6
NEG = -1e30
N_DEV = 8
LANES = 128

ADAM_LR = 0.001
ADAM_B1 = 0.9
ADAM_B2 = 0.999
ADAM_EPS = 1e-08
ADAM_WD = 0.01
ADAM_STEP = 10

VMEM_LIMIT = 56 * 1024 * 1024
MESH = pl.DeviceIdType.MESH

NN = (((1,), (0,)), ((), ()))
NT = (((1,), (1,)), ((), ()))
TN = (((0,), (0,)), ((), ()))
_DN = {"nn": NN, "nt": NT, "tn": TN}


def _cp(*sem):
    return pltpu.CompilerParams(dimension_semantics=sem, vmem_limit_bytes=VMEM_LIMIT)


def _tile(n, pref, unit=128):
    if n <= pref:
        return n
    t = (pref // unit) * unit
    while t >= unit:
        if n % t == 0:
            return t
        t -= unit
    return n


_ANY = pl.BlockSpec(memory_space=pl.ANY)


def _call(body, *, name, grid, in_specs, out_specs, out_shape, operands, scratch=(), comm=None):
    in_specs, out_specs, out_shape, scratch = list(in_specs), list(out_specs), list(out_shape), list(scratch)
    if comm is None:
        res = pl.pallas_call(
            body, name=name, out_shape=out_shape, grid=grid, in_specs=in_specs, out_specs=out_specs, scratch_shapes=scratch,
            compiler_params=_cp(*(("arbitrary",) * len(grid))))(*operands)
        return res, []
    n_in, n_out, n_scr = len(in_specs), len(out_specs), len(scratch)
    c_in, c_out = len(comm.ins), len(comm.outs)

    def hosted(*refs):
        own_in, refs = refs[:n_in], refs[n_in:]
        ex_in, refs = refs[:c_in], refs[c_in:]
        own_out, refs = refs[:n_out], refs[n_out:]
        ex_out, refs = refs[:c_out], refs[c_out:]
        own_scr, sems = refs[:n_scr], refs[n_scr:]
        ids = [pl.program_id(a) for a in range(len(grid))]
        first = functools.reduce(jnp.logical_and, [i == 0 for i in ids])
        last = functools.reduce(jnp.logical_and, [i == g - 1 for i, g in zip(ids, grid)])

        @pl.when(first)
        def _():
            comm.start(ex_in, ex_out, sems)

        body(*own_in, *own_out, *own_scr)

        @pl.when(last)
        def _():
            comm.finish(ex_in, ex_out, sems)

    res = pl.pallas_call(
        hosted, name=name, out_shape=out_shape + list(comm.outs), grid=grid,
        in_specs=in_specs + [_ANY] * c_in, out_specs=out_specs + [_ANY] * c_out, scratch_shapes=scratch + list(comm.sems),
        input_output_aliases={n_in + a: n_out + b for a, b in comm.aliases.items()},
        compiler_params=pltpu.CompilerParams(dimension_semantics=("arbitrary",) * len(grid), vmem_limit_bytes=VMEM_LIMIT,
                                             has_side_effects=True),
    )(*operands, *comm.ins)
    return res[:n_out], res[n_out:]


def _sigmoid(x):
    return 1.0 / (1.0 + jnp.exp(-x))


_GC = math.sqrt(2.0 / math.pi)
_GA = 0.044715


def _gelu(x):
    return 0.5 * x * (1.0 + jnp.tanh(_GC * (x + _GA * x * x * x)))


def _gelu_grad(x):
    t = jnp.tanh(_GC * (x + _GA * x * x * x))
    return 0.5 * (1.0 + t) + 0.5 * x * (1.0 - t * t) * _GC * (1.0 + 3.0 * _GA * x * x)


def _matmul(name, prods, M, N, tm, tn, nk, outs, epilogue, extras=(), n_acc=1, chunk=None, comm=None):
    in_specs, operands, metas = [], [], []
    for a, b, mode, acc in prods:
        if mode == "tn":
            tk = a.shape[0] // nk
            in_specs += [pl.BlockSpec((tk, tm), lambda i, j, k: (k, i)), pl.BlockSpec((tk, tn), lambda i, j, k: (k, j))]
        elif mode == "nt":
            tk = a.shape[1] // nk
            in_specs += [pl.BlockSpec((tm, tk), lambda i, j, k: (i, k)), pl.BlockSpec((tn, tk), lambda i, j, k: (j, k))]
        else:
            tk = a.shape[1] // nk
            in_specs += [pl.BlockSpec((tm, tk), lambda i, j, k: (i, k)), pl.BlockSpec((tk, tn), lambda i, j, k: (k, j))]
        operands += [a, b]
        metas.append((mode, acc))
    for arr, bshape, imap in extras:
        in_specs.append(pl.BlockSpec(bshape, functools.partial(lambda i, j, k, f: f(i, j), f=imap)))
        operands.append(arr)
    out_specs = [pl.BlockSpec(bs, functools.partial(lambda i, j, k, f: f(i, j), f=imap)) for _, _, bs, imap in outs]
    out_shape = [jax.ShapeDtypeStruct(s, d) for s, d, _, _ in outs]
    n_prod, n_ext, n_out = len(prods), len(extras), len(outs)

    def body(*refs):
        in_refs = refs[: 2 * n_prod]
        ex_refs = refs[2 * n_prod : 2 * n_prod + n_ext]
        out_refs = refs[2 * n_prod + n_ext : 2 * n_prod + n_ext + n_out]
        acc_refs = refs[2 * n_prod + n_ext + n_out :]

        def partials():
            res = [None] * n_acc
            for idx, (mode, acc) in enumerate(metas):
                a = in_refs[2 * idx][...].astype(bf16)
                b = in_refs[2 * idx + 1][...].astype(bf16)
                d = lax.dot_general(a, b, _DN[mode], preferred_element_type=f32)
                res[acc] = d if res[acc] is None else res[acc] + d
            return res

        def finish(accs):
            vals = epilogue(accs, [r[...] for r in ex_refs])
            for r, v in zip(out_refs, vals):
                r[...] = v.astype(r.dtype)

        if nk == 1:
            step = chunk or tn
            for c0 in range(0, tn, step):
                c1 = min(c0 + step, tn)
                res = [None] * n_acc
                for idx, (mode, acc) in enumerate(metas):
                    a = in_refs[2 * idx][...].astype(bf16)
                    b_ref = in_refs[2 * idx + 1]
                    b = (b_ref[c0:c1, :] if mode == "nt" else b_ref[:, c0:c1]).astype(bf16)
                    d = lax.dot_general(a, b, _DN[mode], preferred_element_type=f32)
                    res[acc] = d if res[acc] is None else res[acc] + d
                for r, v in zip(out_refs, epilogue(res, [r[:, c0:c1] for r in ex_refs])):
                    r[:, c0:c1] = v.astype(r.dtype)
        else:
            k = pl.program_id(2)
            parts = partials()

            @pl.when(k == 0)
            def _():
                for r, d in zip(acc_refs, parts):
                    r[...] = d

            @pl.when(k > 0)
            def _():
                for r, d in zip(acc_refs, parts):
                    r[...] += d

            @pl.when(k == nk - 1)
            def _():
                finish([r[...] for r in acc_refs])

    scratch = [pltpu.VMEM((tm, tn), f32) for _ in range(n_acc)] if nk > 1 else []
    res, ex = _call(body, name=name, grid=(M // tm, N // tn, nk), in_specs=in_specs, out_specs=out_specs,
                    out_shape=out_shape, operands=operands, scratch=scratch, comm=comm)
    return res if comm is None else (res, ex)


def _ident(accs, ex):
    return [accs[0]]


def _mm_simple(name, a, b, mode, out_dtype, tm_pref=1024, tn_pref=1024, tk_pref=1024, comm=None):
    if mode == "tn":
        K, M = a.shape
        N = b.shape[1]
    elif mode == "nt":
        M, K = a.shape
        N = b.shape[0]
    else:
        M, K = a.shape
        N = b.shape[1]
    tm, tn, tk = _tile(M, tm_pref), _tile(N, tn_pref), _tile(K, tk_pref)
    res = _matmul(name, [(a, b, mode, 0)], M, N, tm, tn, K // tk,
                  [((M, N), out_dtype, (tm, tn), lambda i, j: (i, j))], _ident, comm=comm)
    return res[0] if comm is None else (res[0][0], res[1])


def _group(c, width_pref=1024):
    g = LANES // math.gcd(c, LANES)
    while g < N_DEV and 2 * g * c <= width_pref:
        g *= 2
    return g


def _join(parts):
    return parts[0] if len(parts) == 1 else jnp.concatenate(parts, axis=1)


def _mm_nn_cols(name, a, gs_list, outs_dtypes, epilogue, tm_pref=512, width_pref=1024, tr=False, comm=None):
    M, K = a.shape
    c = gs_list[0].shape[1 if tr else 2]
    g = _group(c, width_pref)
    W = g * c
    tm = _tile(M, tm_pref, 8)
    n_g, n_out = len(gs_list), len(outs_dtypes)
    n_cols, n_steps = N_DEV * c, N_DEV // g
    if tr:
        g, c = 1, W
    blk = (g, c, K) if tr else (g, K, c)

    def body(*refs):
        a_ref = refs[0]
        g_refs = refs[1:1 + n_g]
        out_refs = refs[1 + n_g:]
        av = a_ref[...].astype(bf16)
        cols = [epilogue([lax.dot_general(av, gr[s], NT if tr else NN, preferred_element_type=f32) for gr in g_refs], [])
                for s in range(g)]
        for n, r in enumerate(out_refs):
            r[...] = _join([cols[s][n].astype(r.dtype) for s in range(g)])

    tile = pl.BlockSpec((tm, W), lambda j, i: (i, j))
    res, ex = _call(
        body, name=name, out_shape=[jax.ShapeDtypeStruct((M, n_cols), d) for d in outs_dtypes],
        grid=(n_steps, M // tm),
        in_specs=[pl.BlockSpec((tm, K), lambda j, i: (i, 0))] + [pl.BlockSpec((None,) + blk, lambda j, i: (j, 0, 0, 0))] * n_g,
        out_specs=[tile] * n_out, operands=[a, *[gm.reshape((n_steps,) + blk) for gm in gs_list]], comm=comm)
    return res if comm is None else (res, ex)


def _mm_nt_cols(name, pairs, out_dtype, tm_pref=1024, tn_pref=1024, width_pref=1024, tr=False, comm=None):
    M = pairs[0][0].shape[0]
    c, Kw = pairs[0][1].shape[1:][::1 if tr else -1]
    g = _group(c, width_pref)
    W = g * c
    tm, tn = _tile(M, tm_pref, 8), _tile(Kw, tn_pref)
    nk = N_DEV // g
    n_p = len(pairs)
    if tr:
        g, c = 1, W

    def body(*refs):
        o_ref, acc = refs[2 * n_p], refs[2 * n_p + 1]
        k = pl.program_id(2)
        tot = None
        for n in range(n_p):
            d_ref, g_ref = refs[2 * n], refs[2 * n + 1]
            for s in range(g):
                part = lax.dot_general(d_ref[:, s * c:(s + 1) * c], g_ref[s], NN if tr else NT, preferred_element_type=f32)
                tot = part if tot is None else tot + part

        @pl.when(k == 0)
        def _():
            acc[...] = tot

        @pl.when(k > 0)
        def _():
            acc[...] += tot

        @pl.when(k == nk - 1)
        def _():
            o_ref[...] = acc[...].astype(o_ref.dtype)

    in_specs, operands = [], []
    for d, gm in pairs:
        if tr:
            wspec, wview = pl.BlockSpec((None, g, c, tn), lambda i, j, k: (k, 0, 0, j)), gm.reshape(nk, g, c, Kw)
        else:
            wspec, wview = pl.BlockSpec((None, g, tn, c), lambda i, j, k: (k, 0, j, 0)), gm.reshape(nk, g, Kw, c)
        in_specs += [pl.BlockSpec((tm, W), lambda i, j, k: (i, k)), wspec]
        operands += [d, wview]
    res, ex = _call(
        body, name=name, out_shape=[jax.ShapeDtypeStruct((M, Kw), out_dtype)], grid=(M // tm, Kw // tn, nk),
        in_specs=in_specs, out_specs=[pl.BlockSpec((tm, tn), lambda i, j, k: (i, j))],
        scratch=[pltpu.VMEM((tm, tn), f32)], operands=operands, comm=comm)
    return res[0] if comm is None else (res[0], ex)


def _mm_tn_cols(name, x, ds, c, tm_pref=1024, tk_pref=1024, width_pref=1024, tr=False, comm=None):
    S, Kw = x.shape
    g = _group(c, width_pref)
    W = g * c
    tm, tk = _tile(Kw, tm_pref), _tile(S, tk_pref, 16)
    nk = S // tk
    n_d = len(ds)
    n_steps, c0 = N_DEV // g, c
    blk = (g, c, tm) if tr else (g, tm, c)
    full = (n_steps, g, c, Kw) if tr else (n_steps, g, Kw, c)

    def body(*refs):
        x_ref = refs[0]
        d_refs = refs[1:1 + n_d]
        o_refs = refs[1 + n_d:1 + 2 * n_d]
        accs = refs[1 + 2 * n_d:]
        k = pl.program_id(2)

        @pl.when(k == 0)
        def _():
            for acc in accs:
                acc[...] = jnp.zeros_like(acc)

        xv = x_ref[...].astype(bf16)
        for d_ref, acc in zip(d_refs, accs):
            for s in range(g):
                ds_ = d_ref[:, s * c:(s + 1) * c]
                acc[s] += lax.dot_general(ds_, xv, TN, preferred_element_type=f32) if tr else \
                    lax.dot_general(xv, ds_, TN, preferred_element_type=f32)

        @pl.when(k == nk - 1)
        def _():
            for o_ref, acc in zip(o_refs, accs):
                o_ref[...] = acc[...].astype(o_ref.dtype)

    out_map = (lambda i, j, k: (j, 0, 0, i)) if tr else (lambda i, j, k: (j, 0, i, 0))
    outs, ex = _call(
        body, name=name, out_shape=[jax.ShapeDtypeStruct(full, bf16)] * n_d, grid=(Kw // tm, n_steps, nk),
        in_specs=[pl.BlockSpec((tk, tm), lambda i, j, k: (k, i))] + [pl.BlockSpec((tk, W), lambda i, j, k: (k, j))] * n_d,
        out_specs=[pl.BlockSpec((None,) + blk, out_map)] * n_d,
        scratch=[pltpu.VMEM(blk, f32)] * n_d, operands=[x, *ds], comm=comm)
    outs = [o.reshape((N_DEV, c0, Kw) if tr else (N_DEV, Kw, c0)) for o in outs]
    return outs if comm is None else (outs, ex)


def _norm_fwd(name, x, g):
    S, D = x.shape
    tm = _tile(S, 512, 8)

    def body(x_ref, g_ref, h_ref):
        xv = x_ref[...]
        r = lax.rsqrt(jnp.mean(xv * xv, axis=-1, keepdims=True) + EPS)
        h_ref[...] = (xv * r * g_ref[...]).astype(bf16)

    return pl.pallas_call(
        body, name=name, out_shape=jax.ShapeDtypeStruct((S, D), bf16), grid=(S // tm,),
        in_specs=[pl.BlockSpec((tm, D), lambda i: (i, 0)), pl.BlockSpec((1, D), lambda i: (0, 0))],
        out_specs=pl.BlockSpec((tm, D), lambda i: (i, 0)), compiler_params=_cp("parallel"),
    )(x, g.reshape(1, D))


def _norm_bwd(name, dh, x, g, dx_in):
    S, D = x.shape
    tm = _tile(S, 256, 8)

    def body(dh_ref, x_ref, g_ref, dxi_ref, dx_ref, dxb_ref, dg_ref):
        i = pl.program_id(0)
        xv = x_ref[...]
        r = lax.rsqrt(jnp.mean(xv * xv, axis=-1, keepdims=True) + EPS)
        xh = xv * r
        dhv = dh_ref[...].astype(f32)
        dxh = dhv * g_ref[...]
        dx = dxi_ref[...] + r * (dxh - xh * jnp.mean(dxh * xh, axis=-1, keepdims=True))
        dx_ref[...] = dx
        dxb_ref[...] = dx.astype(bf16)

        @pl.when(i == 0)
        def _():
            dg_ref[...] = jnp.zeros_like(dg_ref)

        dg_ref[...] += jnp.sum(dhv * xh, axis=0, keepdims=True)

    row = pl.BlockSpec((tm, D), lambda i: (i, 0))
    vec = pl.BlockSpec((1, D), lambda i: (0, 0))
    return pl.pallas_call(
        body, name=name,
        out_shape=[jax.ShapeDtypeStruct((S, D), f32), jax.ShapeDtypeStruct((S, D), bf16), jax.ShapeDtypeStruct((1, D), f32)],
        grid=(S // tm,), in_specs=[row, row, vec, row], out_specs=[row, row, vec], compiler_params=_cp("arbitrary"),
    )(dh, x, g.reshape(1, D), dx_in)


def _loss_head(x, g, t):
    S, D = x.shape
    tm = _tile(S, 256, 8)

    def body(x_ref, g_ref, t_ref, dx_ref, dg_ref, loss_ref):
        i = pl.program_id(0)
        xv = x_ref[...]
        r = lax.rsqrt(jnp.mean(xv * xv, axis=-1, keepdims=True) + EPS)
        xh = xv * r
        gv = g_ref[...]
        err = xh * gv - t_ref[...]
        dy = err * (1.0 / D)
        dxh = dy * gv
        dx_ref[...] = r * (dxh - xh * jnp.mean(dxh * xh, axis=-1, keepdims=True))

        @pl.when(i == 0)
        def _():
            dg_ref[...] = jnp.zeros_like(dg_ref)
            loss_ref[...] = jnp.zeros_like(loss_ref)

        dg_ref[...] += jnp.sum(dy * xh, axis=0, keepdims=True)
        row = jnp.sum(err * err, axis=-1, keepdims=True) * (0.5 / D)
        loss_ref[...] += jnp.broadcast_to(jnp.sum(row, axis=0, keepdims=True), loss_ref.shape)

    return pl.pallas_call(
        body, name="loss_head",
        out_shape=[jax.ShapeDtypeStruct((S, D), f32), jax.ShapeDtypeStruct((1, D), f32), jax.ShapeDtypeStruct((8, LANES), f32)],
        grid=(S // tm,),
        in_specs=[pl.BlockSpec((tm, D), lambda i: (i, 0)), pl.BlockSpec((1, D), lambda i: (0, 0)), pl.BlockSpec((tm, D), lambda i: (i, 0))],
        out_specs=[pl.BlockSpec((tm, D), lambda i: (i, 0)), pl.BlockSpec((1, D), lambda i: (0, 0)), pl.BlockSpec((8, LANES), lambda i: (0, 0))],
        compiler_params=_cp("arbitrary"),
    )(x, g.reshape(1, D), t)


def _perm(t, d):
    if d == 1:
        return t
    S, C = t.shape
    return t.reshape(S // d, d, C).transpose(1, 0, 2).reshape(S, C)


def _rope_tables(S):
    half = HEAD_DIM // 2
    pos = jnp.arange(S, dtype=f32)
    inv_freq = ROPE_THETA ** (-jnp.arange(0, HEAD_DIM, 2, dtype=f32) / HEAD_DIM)
    ang = pos[:, None] * inv_freq[None, :]
    c, s = jnp.cos(ang), jnp.sin(ang)
    cos2 = jnp.concatenate([c, c], axis=-1)
    sin2 = jnp.concatenate([-s, s], axis=-1)
    assert cos2.shape == (S, 2 * half)
    return (jnp.stack([_perm(cos2, d) for d in DILATIONS]), jnp.stack([_perm(sin2, d) for d in DILATIONS]))


def _rope(t, c, s):
    return t * c + pltpu.roll(t, HEAD_DIM // 2, 1) * s


def _rope_bwd(dt, c, s):
    return dt * c - pltpu.roll(dt, HEAD_DIM // 2, 1) * s


def _band_bounds(i, nblk):
    g = pl.program_id(0)
    lb = jnp.right_shift(jnp.int32(nblk), 2 * g)
    pos = lax.rem(i, lb)
    lo = jnp.where(pos == 0, BLK, 0)
    hi = jnp.where(pos == lb - 1, 2 * BLK, 3 * BLK)
    return lo, hi


def _cur_spec(width, t=None):
    if t is None:
        return pl.BlockSpec((None, 2 * BLK, width), lambda g, i: (g, i, 0))
    return pl.BlockSpec((None, None, 2 * BLK, width), lambda g, i: (t, g, i, 0))


def _band_specs(width, nblk, t=None):
    lo, hi = (lambda i: jnp.maximum(2 * i - 1, 0)), (lambda i: jnp.minimum(2 * i + 2, nblk - 1))
    if t is None:
        return [pl.BlockSpec((None, BLK, width), lambda g, i: (g, lo(i), 0)), _cur_spec(width),
                pl.BlockSpec((None, BLK, width), lambda g, i: (g, hi(i), 0))]
    return [pl.BlockSpec((None, None, BLK, width), lambda g, i: (t, g, lo(i), 0)), _cur_spec(width, t),
            pl.BlockSpec((None, None, BLK, width), lambda g, i: (t, g, hi(i), 0))]


def _band(sub, prev, pair, nxt, cols=slice(None)):
    pieces = [(prev, slice(None)), (pair, slice(0, BLK)), (pair, slice(BLK, 2 * BLK)), (nxt, slice(None))][sub:sub + 3]
    return jnp.concatenate([ref[rows, cols] for ref, rows in pieces], axis=0)


_ROWS = 2048


def _to_scratch(scr, val):
    for h in range(HEADS):
        scr[h] = val[:, h * HEAD_DIM:(h + 1) * HEAD_DIM].astype(f32)


def _qkv_groups(z):
    S = z.shape[0]
    R = min(_ROWS, S)
    nb = S // R

    def body(x_ref, o_ref, scr):
        g, i = pl.program_id(1), pl.program_id(2)
        _to_scratch(scr, x_ref[...])
        for gi, d in enumerate(DILATIONS):
            @pl.when(g == gi)
            def _():
                n, L = R // d, S // d
                for r in range(d):
                    start = pl.multiple_of(r * L + i * n, 16)
                    for h in range(HEADS):
                        o_ref[pl.ds(start, n), h * HEAD_DIM:(h + 1) * HEAD_DIM] = scr[h, pl.ds(r, n, stride=d), :].astype(bf16)

    return pl.pallas_call(
        body, name="qkv_groups", out_shape=jax.ShapeDtypeStruct((3, N_GROUPS, S, ATTN_W), bf16), grid=(3, N_GROUPS, nb),
        in_specs=[pl.BlockSpec((R, ATTN_W), lambda t, g, i: (i, t * N_GROUPS + g))],
        out_specs=pl.BlockSpec((None, None, S, ATTN_W), lambda t, g, i: (t, g, 0, 0)),
        scratch_shapes=[pltpu.VMEM((HEADS, R, HEAD_DIM), f32)], compiler_params=_cp("arbitrary", "arbitrary", "arbitrary"),
    )(z)


def _bwd_groups(attn, dattn, lse):
    S = attn.shape[0]
    R = min(_ROWS, S)
    nb = S // R

    def body(a_ref, d_ref, l_ref, o_ref, scr):
        t, g, i = pl.program_id(0), pl.program_id(1), pl.program_id(2)

        @pl.when(t == 0)
        def _():
            _to_scratch(scr, d_ref[...])

        @pl.when(t == 1)
        def _():
            _to_scratch(scr, l_ref[...])

        @pl.when(t == 2)
        def _():
            prod = a_ref[...].astype(f32) * d_ref[...].astype(f32)
            for h in range(HEADS):
                part = jnp.sum(prod[:, h * HEAD_DIM:(h + 1) * HEAD_DIM], axis=-1, keepdims=True)
                scr[h] = jnp.broadcast_to(part, (R, HEAD_DIM))

        for gi, d in enumerate(DILATIONS):
            @pl.when(g == gi)
            def _():
                n, L = R // d, S // d
                for r in range(d):
                    start = pl.multiple_of(r * L + i * n, 8)
                    for h in range(HEADS):
                        o_ref[pl.ds(start, n), h * HEAD_DIM:(h + 1) * HEAD_DIM] = scr[h, pl.ds(r, n, stride=d), :]

    def nat(used):
        return pl.BlockSpec((R, ATTN_W), lambda t, g, i: (jnp.where(used(t), i, 0), 0))

    return pl.pallas_call(
        body, name="bwd_groups", out_shape=jax.ShapeDtypeStruct((3, N_GROUPS, S, ATTN_W), f32), grid=(3, N_GROUPS, nb),
        in_specs=[nat(lambda t: t == 2), nat(lambda t: t != 1), nat(lambda t: t == 1)],
        out_specs=pl.BlockSpec((None, None, S, ATTN_W), lambda t, g, i: (t, g, 0, 0)),
        scratch_shapes=[pltpu.VMEM((HEADS, R, HEAD_DIM), f32)], compiler_params=_cp("arbitrary", "arbitrary", "arbitrary"),
    )(attn, dattn, lse)


def _group_views(t3, R):
    S = t3.shape[1]
    views = [t3.reshape(N_GROUPS, d, S // d, ATTN_W) for d in DILATIONS]
    specs = [pl.BlockSpec((None, d, R // d, ATTN_W), functools.partial(lambda i, g: (g, 0, i, 0), g=g))
             for g, d in enumerate(DILATIONS)]
    return views, specs


def _from_groups(scr, ref, d):
    n = ref.shape[1]
    for r in range(d):
        blk = ref[r]
        for h in range(HEADS):
            scr[h, pl.ds(r, n, stride=d), :] = blk[:, h * HEAD_DIM:(h + 1) * HEAD_DIM].astype(f32)


def _to_natural(t3):
    S = t3.shape[1]
    R = min(_ROWS // 2, S)
    views, specs = _group_views(t3, R)

    def body(v0, v1, v2, o_ref, scr):
        for g, (ref, d) in enumerate(zip((v0, v1, v2), DILATIONS)):
            _from_groups(scr, ref, d)
            for h in range(HEADS):
                o_ref[:, g * ATTN_W + h * HEAD_DIM:g * ATTN_W + (h + 1) * HEAD_DIM] = scr[h].astype(bf16)

    return pl.pallas_call(
        body, name="to_natural", out_shape=jax.ShapeDtypeStruct((S, N_GROUPS * ATTN_W), bf16), grid=(S // R,),
        in_specs=specs, out_specs=pl.BlockSpec((R, N_GROUPS * ATTN_W), lambda i: (i, 0)),
        scratch_shapes=[pltpu.VMEM((HEADS, R, HEAD_DIM), f32)], compiler_params=_cp("arbitrary"),
    )(*views)


_SCALE = HEAD_DIM ** -0.5


def _attn_fwd(qkv, cos, sin, comm=None):
    _, _, S, W = qkv.shape
    nblk = S // BLK

    def body(q_ref, kp, kc, kn, vp, vc, vn, cq, sq, ckp, ckc, ckn, skp, skc, skn, o_ref, lse_ref):
        i = pl.program_id(1)
        a = lax.broadcasted_iota(jnp.int32, (BLK, 3 * BLK), 0)
        b = lax.broadcasted_iota(jnp.int32, (BLK, 3 * BLK), 1)
        for sub in range(2):
            rows = slice(sub * BLK, (sub + 1) * BLK)
            lo, hi = _band_bounds(2 * i + sub, nblk)
            mask = (jnp.abs(b - BLK - a) <= RADIUS) & (b >= lo) & (b < hi)
            ck, sk = _band(sub, ckp, ckc, ckn), _band(sub, skp, skc, skn)
            for hh in range(HEADS):
                sl = slice(hh * HEAD_DIM, (hh + 1) * HEAD_DIM)
                qh = _rope(q_ref[rows, sl].astype(f32), cq[rows, :], sq[rows, :]).astype(bf16)
                kh = _rope(_band(sub, kp, kc, kn, sl).astype(f32), ck, sk).astype(bf16)
                vh = _band(sub, vp, vc, vn, sl)
                s = lax.dot_general(qh, kh, NT, preferred_element_type=f32) * _SCALE
                s = jnp.where(mask, s, NEG)
                m = jnp.max(s, axis=-1, keepdims=True)
                e = jnp.exp(s - m)
                den = jnp.sum(e, axis=-1, keepdims=True)
                o = lax.dot_general(e.astype(bf16), vh, NN, preferred_element_type=f32) * (1.0 / den)
                o_ref[rows, sl] = o.astype(bf16)
                lse_ref[rows, sl] = jnp.broadcast_to(m + jnp.log(den), (BLK, HEAD_DIM))

    blk = _cur_spec(W)
    tab = _cur_spec(HEAD_DIM)
    res, ex = _call(
        body, name="attn_fwd",
        out_shape=[jax.ShapeDtypeStruct((N_GROUPS, S, W), bf16), jax.ShapeDtypeStruct((N_GROUPS, S, W), f32)],
        grid=(N_GROUPS, nblk // 2),
        in_specs=[_cur_spec(W, 0)] + _band_specs(W, nblk, 1) + _band_specs(W, nblk, 2) + [tab, tab]
        + _band_specs(HEAD_DIM, nblk) * 2,
        out_specs=[blk, blk], operands=[qkv] * 7 + [cos, sin, cos, cos, cos, sin, sin, sin], comm=comm)
    return res if comm is None else (res, ex)


def _attn_combine(o3, lse3):
    _, S, W = o3.shape
    R = min(_ROWS // 2, S)
    o_views, specs = _group_views(o3, R)
    l_views, _ = _group_views(lse3, R)

    def body(o0, o1, o2, l0, l1, l2, attn_ref, lse_ref, so0, so1, so2, sl0, sl1, sl2):
        for ref, scr, d in zip((o0, o1, o2, l0, l1, l2), (so0, so1, so2, sl0, sl1, sl2), DILATIONS * 2):
            _from_groups(scr, ref, d)
        for h in range(HEADS):
            a0, a1, a2 = sl0[h], sl1[h], sl2[h]
            m = jnp.maximum(jnp.maximum(a0, a1), a2)
            w0, w1, w2 = jnp.exp(a0 - m), jnp.exp(a1 - m), jnp.exp(a2 - m)
            den = w0 + w1 + w2
            acc = w0 * so0[h] + w1 * so1[h] + w2 * so2[h]
            attn_ref[:, h * HEAD_DIM:(h + 1) * HEAD_DIM] = (acc * (1.0 / den)).astype(bf16)
            lse_ref[:, h * HEAD_DIM:(h + 1) * HEAD_DIM] = m + jnp.log(den)

    nat = pl.BlockSpec((R, W), lambda i: (i, 0))
    return pl.pallas_call(
        body, name="attn_combine", out_shape=[jax.ShapeDtypeStruct((S, W), bf16), jax.ShapeDtypeStruct((S, W), f32)],
        grid=(S // R,), in_specs=specs * 2, out_specs=[nat, nat],
        scratch_shapes=[pltpu.VMEM((HEADS, R, HEAD_DIM), f32)] * 6, compiler_params=_cp("arbitrary"),
    )(*o_views, *l_views)


def _attn_bwd_dq(qkv, cos, sin, b3, comm=None):
    _, _, S, W = qkv.shape
    nblk = S // BLK

    def body(q_ref, kp, kc, kn, vp, vc, vn, cq, sq, ckp, ckc, ckn, skp, skc, skn, da_ref, l_ref, dl_ref, dq_ref):
        i = pl.program_id(1)
        a = lax.broadcasted_iota(jnp.int32, (BLK, 3 * BLK), 0)
        b = lax.broadcasted_iota(jnp.int32, (BLK, 3 * BLK), 1)
        for sub in range(2):
            rows = slice(sub * BLK, (sub + 1) * BLK)
            lo, hi = _band_bounds(2 * i + sub, nblk)
            mask = (jnp.abs(b - BLK - a) <= RADIUS) & (b >= lo) & (b < hi)
            ck, sk = _band(sub, ckp, ckc, ckn), _band(sub, skp, skc, skn)
            for hh in range(HEADS):
                sl = slice(hh * HEAD_DIM, (hh + 1) * HEAD_DIM)
                qh = _rope(q_ref[rows, sl].astype(f32), cq[rows, :], sq[rows, :]).astype(bf16)
                kh = _rope(_band(sub, kp, kc, kn, sl).astype(f32), ck, sk).astype(bf16)
                vh = _band(sub, vp, vc, vn, sl)
                s = lax.dot_general(qh, kh, NT, preferred_element_type=f32) * _SCALE
                lh = l_ref[rows, sl]
                l3 = jnp.concatenate([lh, lh, lh], axis=1)
                p = jnp.exp(jnp.where(mask, s - l3, NEG))
                dp = lax.dot_general(da_ref[rows, sl].astype(bf16), vh, NT, preferred_element_type=f32)
                dh = dl_ref[rows, sl]
                ds = p * (dp - jnp.concatenate([dh, dh, dh], axis=1))
                dqh = lax.dot_general(ds.astype(bf16), kh, NN, preferred_element_type=f32) * _SCALE
                dq_ref[rows, sl] = _rope_bwd(dqh, cq[rows, :], sq[rows, :]).astype(bf16)

    tab = _cur_spec(HEAD_DIM)
    res, ex = _call(
        body, name="attn_bwd_dq", out_shape=[jax.ShapeDtypeStruct((N_GROUPS, S, W), bf16)], grid=(N_GROUPS, nblk // 2),
        in_specs=[_cur_spec(W, 0)] + _band_specs(W, nblk, 1) + _band_specs(W, nblk, 2) + [tab, tab]
        + _band_specs(HEAD_DIM, nblk) * 2 + [_cur_spec(W, 0), _cur_spec(W, 1), _cur_spec(W, 2)],
        out_specs=[_cur_spec(W)], operands=[qkv] * 7 + [cos, sin, cos, cos, cos, sin, sin, sin, b3, b3, b3], comm=comm)
    return res[0] if comm is None else (res[0], ex)


def _attn_bwd_dkv(qkv, cos, sin, b3, comm=None):
    _, _, S, W = qkv.shape
    nblk = S // BLK

    def body(k_ref, v_ref, ck, sk, qp, qc, qn, cqp, cqc, cqn, sqp, sqc, sqn, dap, dac, dan, lp, lc, ln, dlp, dlc, dln,
             dk_ref, dv_ref):
        j = pl.program_id(1)
        a = lax.broadcasted_iota(jnp.int32, (3 * BLK, BLK), 0)
        b = lax.broadcasted_iota(jnp.int32, (3 * BLK, BLK), 1)
        for sub in range(2):
            rows = slice(sub * BLK, (sub + 1) * BLK)
            lo, hi = _band_bounds(2 * j + sub, nblk)
            mask = (jnp.abs(b - (a - BLK)) <= RADIUS) & (a >= lo) & (a < hi)
            cq, sq = _band(sub, cqp, cqc, cqn), _band(sub, sqp, sqc, sqn)
            for hh in range(HEADS):
                sl = slice(hh * HEAD_DIM, (hh + 1) * HEAD_DIM)
                kh = _rope(k_ref[rows, sl].astype(f32), ck[rows, :], sk[rows, :]).astype(bf16)
                vh = v_ref[rows, sl]
                qh = _rope(_band(sub, qp, qc, qn, sl).astype(f32), cq, sq).astype(bf16)
                dah = _band(sub, dap, dac, dan, sl).astype(bf16)
                lh = _band(sub, lp, lc, ln, sl)
                dlh = _band(sub, dlp, dlc, dln, sl)
                s = lax.dot_general(qh, kh, NT, preferred_element_type=f32) * _SCALE
                p = jnp.exp(jnp.where(mask, s - lh, NEG))
                dv_ref[rows, sl] = lax.dot_general(p.astype(bf16), dah, TN, preferred_element_type=f32).astype(bf16)
                dp = lax.dot_general(dah, vh, NT, preferred_element_type=f32)
                ds = p * (dp - dlh)
                dkh = lax.dot_general(ds.astype(bf16), qh, TN, preferred_element_type=f32) * _SCALE
                dk_ref[rows, sl] = _rope_bwd(dkh, ck[rows, :], sk[rows, :]).astype(bf16)

    blk, tab, bt = _cur_spec(W), _cur_spec(HEAD_DIM), _band_specs(HEAD_DIM, nblk)
    res, ex = _call(
        body, name="attn_bwd_dkv",
        out_shape=[jax.ShapeDtypeStruct((N_GROUPS, S, W), bf16), jax.ShapeDtypeStruct((N_GROUPS, S, W), bf16)],
        grid=(N_GROUPS, nblk // 2),
        in_specs=[_cur_spec(W, 1), _cur_spec(W, 2), tab, tab] + _band_specs(W, nblk, 0) + bt + bt
        + _band_specs(W, nblk, 0) + _band_specs(W, nblk, 1) + _band_specs(W, nblk, 2), out_specs=[blk, blk],
        operands=[qkv, qkv, cos, sin, qkv, qkv, qkv, cos, cos, cos, sin, sin, sin] + [b3] * 9, comm=comm)
    return res if comm is None else (res, ex)


_SG_ROWS = 512


def _sg_z_specs(tm, half):
    o = QKV_W // half
    return [pl.BlockSpec((tm, half), functools.partial(lambda i, c: (i, c), c=o + n)) for n in range(4)]


def _sg_norm(v, lg, lb):
    gv = _gelu(v)
    mu = jnp.mean(gv, axis=-1, keepdims=True)
    xc = gv - mu
    rstd = lax.rsqrt(jnp.mean(xc * xc, axis=-1, keepdims=True) + EPS)
    xh = xc * rstd
    return xh, rstd, xh * lg + lb


def _sg_fwd(z, w, bb, lg, lb):
    S = z.shape[0]
    tm = _tile(S, _SG_ROWS, SG_CHUNK)
    half = SG_W // 2

    def body(u0, u1, v0, v1, w_ref, bb_ref, lg_ref, lb_ref, o_ref):
        u = jnp.concatenate([u0[...], u1[...]], axis=1).astype(f32)
        v = jnp.concatenate([v0[...], v1[...]], axis=1).astype(f32)
        gu = _gelu(u)
        _, _, vn = _sg_norm(v, lg_ref[...], lb_ref[...])
        vnb = vn.astype(bf16)
        for c in range(tm // SG_CHUNK):
            rs = slice(c * SG_CHUNK, (c + 1) * SG_CHUNK)
            for g in range(SG_GROUPS):
                cs = slice(g * 128, (g + 1) * 128)
                mixed = lax.dot_general(w_ref[g], vnb[rs, cs], NN, preferred_element_type=f32) + bb_ref[g]
                o_ref[rs, cs] = (gu[rs, cs] * mixed).astype(bf16)

    full3 = pl.BlockSpec((SG_GROUPS, 128, 128), lambda i: (0, 0, 0))
    vec = pl.BlockSpec((1, SG_W), lambda i: (0, 0))
    return pl.pallas_call(
        body, name="sg_fwd", out_shape=jax.ShapeDtypeStruct((S, SG_W), bf16), grid=(S // tm,),
        in_specs=_sg_z_specs(tm, half) + [full3, full3, vec, vec],
        out_specs=pl.BlockSpec((tm, SG_W), lambda i: (i, 0)), compiler_params=_cp("parallel"),
    )(z, z, z, z, w, bb, lg.reshape(1, SG_W), lb.reshape(1, SG_W))


def _sg_bwd(dsg, z, w, wt, bb, lg, lb):
    S = z.shape[0]
    tm = _tile(S, _SG_ROWS, SG_CHUNK)
    half = SG_W // 2

    def body(d_ref, u0, u1, v0, v1, w_ref, wt_ref, bb_ref, lg_ref, lb_ref, du_ref, dv_ref, dw_ref, db_ref, dlg_ref, dlb_ref, dvn_scr):
        i = pl.program_id(0)

        @pl.when(i == 0)
        def _():
            dw_ref[...] = jnp.zeros_like(dw_ref)
            db_ref[...] = jnp.zeros_like(db_ref)
            dlg_ref[...] = jnp.zeros_like(dlg_ref)
            dlb_ref[...] = jnp.zeros_like(dlb_ref)

        u = jnp.concatenate([u0[...], u1[...]], axis=1).astype(f32)
        v = jnp.concatenate([v0[...], v1[...]], axis=1).astype(f32)
        gu = _gelu(u)
        dgu = _gelu_grad(u)
        xh, rstd, vn = _sg_norm(v, lg_ref[...], lb_ref[...])
        vnb = vn.astype(bf16)
        dsg_v = d_ref[...].astype(f32)
        for g in range(SG_GROUPS):
            cs = slice(g * 128, (g + 1) * 128)
            dw_g = jnp.zeros((128, 128), f32)
            db_g = jnp.zeros((128, 1), f32)
            for c in range(tm // SG_CHUNK):
                rs = slice(c * SG_CHUNK, (c + 1) * SG_CHUNK)
                ds = dsg_v[rs, cs]
                mixed = lax.dot_general(w_ref[g], vnb[rs, cs], NN, preferred_element_type=f32) + bb_ref[g]
                du_ref[rs, cs] = (ds * mixed * dgu[rs, cs]).astype(bf16)
                dmix = ds * gu[rs, cs]
                dmb = dmix.astype(bf16)
                dw_g = dw_g + lax.dot_general(dmb, vnb[rs, cs], NT, preferred_element_type=f32)
                db_g = db_g + jnp.sum(dmix, axis=-1, keepdims=True)
                dvn_scr[rs, cs] = lax.dot_general(wt_ref[g], dmb, NN, preferred_element_type=f32)
            dw_ref[g] += dw_g
            db_ref[g] += jnp.broadcast_to(db_g, (128, 128))
        dvn = dvn_scr[...]
        dlg_ref[...] += jnp.sum(dvn * xh, axis=0, keepdims=True)
        dlb_ref[...] += jnp.sum(dvn, axis=0, keepdims=True)
        dxh = dvn * lg_ref[...]
        dgv = rstd * (dxh - jnp.mean(dxh, axis=-1, keepdims=True) - xh * jnp.mean(dxh * xh, axis=-1, keepdims=True))
        dv_ref[...] = (dgv * _gelu_grad(v)).astype(bf16)

    full3 = pl.BlockSpec((SG_GROUPS, 128, 128), lambda i: (0, 0, 0))
    vec = pl.BlockSpec((1, SG_W), lambda i: (0, 0))
    row = pl.BlockSpec((tm, SG_W), lambda i: (i, 0))
    return pl.pallas_call(
        body, name="sg_bwd",
        out_shape=[jax.ShapeDtypeStruct((S, SG_W), bf16), jax.ShapeDtypeStruct((S, SG_W), bf16),
                   jax.ShapeDtypeStruct((SG_GROUPS, 128, 128), f32), jax.ShapeDtypeStruct((SG_GROUPS, 128, 128), f32),
                   jax.ShapeDtypeStruct((1, SG_W), f32), jax.ShapeDtypeStruct((1, SG_W), f32)],
        grid=(S // tm,),
        in_specs=[row] + _sg_z_specs(tm, half) + [full3, full3, full3, vec, vec],
        out_specs=[row, row, full3, full3, vec, vec],
        scratch_shapes=[pltpu.VMEM((tm, SG_W), f32)], compiler_params=_cp("arbitrary"),
    )(dsg, z, z, z, z, w, wt, bb, lg.reshape(1, SG_W), lb.reshape(1, SG_W))


def _merge_fwd(z, ya, yb, D):
    S = z.shape[0]
    tm, tc = _tile(S, 512, 8), _tile(D, 512)
    o_a, o_b = (QKV_W + 2 * SG_W) // tc, (QKV_W + 2 * SG_W + D) // tc

    def body(ga_ref, gb_ref, ya_ref, yb_ref, o_ref):
        ga = _sigmoid(ga_ref[...].astype(f32))
        gb = _sigmoid(gb_ref[...].astype(f32))
        o_ref[...] = (ga * ya_ref[...].astype(f32) + gb * yb_ref[...].astype(f32)).astype(bf16)

    blk = pl.BlockSpec((tm, tc), lambda i, j: (i, j))
    return pl.pallas_call(
        body, name="merge_fwd", out_shape=jax.ShapeDtypeStruct((S, D), bf16), grid=(S // tm, D // tc),
        in_specs=[pl.BlockSpec((tm, tc), lambda i, j: (i, o_a + j)), pl.BlockSpec((tm, tc), lambda i, j: (i, o_b + j)), blk, blk],
        out_specs=blk, compiler_params=_cp("parallel", "parallel"),
    )(z, z, ya, yb)


def _ple_bwd_ew(dx, gp, e):
    S, D = dx.shape
    tm, tc = _tile(S, 512, 8), _tile(D, 1024)

    def body(dx_ref, gp_ref, e_ref, dgp_ref, de_ref):
        dxv = dx_ref[...]
        sg = _sigmoid(gp_ref[...].astype(f32))
        dgp_ref[...] = (dxv * e_ref[...].astype(f32) * sg * (1.0 - sg)).astype(bf16)
        de_ref[...] = (dxv * sg).astype(bf16)

    blk = pl.BlockSpec((tm, tc), lambda i, j: (i, j))
    return pl.pallas_call(
        body, name="ple_bwd_ew", out_shape=[jax.ShapeDtypeStruct((S, D), bf16)] * 2, grid=(S // tm, D // tc),
        in_specs=[blk, blk, blk], out_specs=[blk, blk], compiler_params=_cp("parallel", "parallel"),
    )(dx, gp, e)


def _adam_math(w, g, m, v):
    m = ADAM_B1 * m + (1.0 - ADAM_B1) * g
    v = ADAM_B2 * v + (1.0 - ADAM_B2) * (g * g)
    m_hat = m / (1.0 - ADAM_B1 ** ADAM_STEP)
    v_hat = v / (1.0 - ADAM_B2 ** ADAM_STEP)
    delta = -ADAM_LR * (m_hat / (jnp.sqrt(v_hat) + ADAM_EPS) + ADAM_WD * w)
    return delta, m, v


def _small_sum_adamw(gathered, w, m, v):
    _, R, _ = gathered.shape
    tr = _tile(R, 1024, SMALL_ROWS)

    def body(p_ref, w_ref, m_ref, v_ref, g_ref, d_ref, nm_ref, nv_ref):
        g = p_ref[0]
        for n in range(1, N_DEV):
            g = g + p_ref[n]
        d, nm, nv = _adam_math(w_ref[...], g, m_ref[...], v_ref[...])
        g_ref[...] = g
        d_ref[...] = d
        nm_ref[...] = nm
        nv_ref[...] = nv

    blk = pl.BlockSpec((tr, LANES), lambda i: (i, 0))
    return pl.pallas_call(
        body, name="small_sum_adamw", out_shape=[jax.ShapeDtypeStruct((R, LANES), f32)] * 4, grid=(R // tr,),
        in_specs=[pl.BlockSpec((N_DEV, tr, LANES), lambda i: (0, i, 0)), blk, blk, blk], out_specs=[blk] * 4,
        compiler_params=_cp("parallel"),
    )(gathered, w, m, v)


def _all_gather(name, shard, in_vmem=False):
    R, C = shard.shape

    def body(x_ref, out_ref, send_sems, recv_sems, local_sem):
        x, y, c = lax.axis_index("x"), lax.axis_index("y"), lax.axis_index("c")
        me, sibling = (x, y, c), (x, y, 1 - c)
        chips = [(1 - x, y), (x, 1 - y), (1 - x, 1 - y)]

        def rows(px, py, pc):
            return out_ref.at[4 * px + 2 * py + pc]

        def copy(k, block, to, src=None):
            return pltpu.make_async_remote_copy(
                src_ref=rows(*block) if src is None else src, dst_ref=rows(*block),
                send_sem=send_sems.at[k], recv_sem=recv_sems.at[k], device_id=to, device_id_type=MESH)

        mine = pltpu.make_async_copy(x_ref, rows(*me), local_sem)
        mine.start()
        first = [copy(0, me, sibling, src=x_ref)]
        first += [copy(1 + j, me, (*chip, c), src=x_ref) for j, chip in enumerate(chips)]
        for cp in first:
            cp.start()
        passed = [copy(4 + j, (*chip, c), sibling) for j, chip in enumerate(chips)]
        for j, chip in enumerate(chips):
            copy(1 + j, (*chip, c), me).wait_recv()
            passed[j].start()
        copy(0, sibling, me).wait_recv()
        for j, chip in enumerate(chips):
            copy(4 + j, (*chip, 1 - c), me).wait_recv()
        for cp in first + passed:
            cp.wait_send()
        mine.wait()

    space = pl.BlockSpec(memory_space=pltpu.VMEM) if in_vmem else _ANY
    return pl.pallas_call(
        body, name=name, out_shape=jax.ShapeDtypeStruct((N_DEV, R, C), shard.dtype),
        in_specs=[space], out_specs=space,
        scratch_shapes=[pltpu.SemaphoreType.DMA((7,)), pltpu.SemaphoreType.DMA((7,)), pltpu.SemaphoreType.DMA],
        compiler_params=pltpu.CompilerParams(has_side_effects=True, vmem_limit_bytes=VMEM_LIMIT),
    )(shard)


class _Exchange:
    def __init__(self, ins, outs, sems, start, finish, aliases=None):
        self.ins, self.outs, self.sems, self.start, self.finish = list(ins), list(outs), list(sems), start, finish
        self.aliases = dict(aliases or {})


def _run_exchange(name, ex):
    c_in, c_out = len(ex.ins), len(ex.outs)

    def body(*refs):
        ins, outs, sems = refs[:c_in], refs[c_in:c_in + c_out], refs[c_in + c_out:]
        ex.start(ins, outs, sems)
        ex.finish(ins, outs, sems)

    return pl.pallas_call(
        body, name=name, out_shape=ex.outs, in_specs=[_ANY] * c_in, out_specs=[_ANY] * c_out, scratch_shapes=ex.sems,
        input_output_aliases=ex.aliases,
        compiler_params=pltpu.CompilerParams(has_side_effects=True, vmem_limit_bytes=VMEM_LIMIT),
    )(*ex.ins)


def _both(e1, e2):
    i1, o1, s1 = len(e1.ins), len(e1.outs), len(e1.sems)

    def start(ins, outs, sems):
        e1.start(ins[:i1], outs[:o1], sems[:s1])
        e2.start(ins[i1:], outs[o1:], sems[s1:])

    def finish(ins, outs, sems):
        e1.finish(ins[:i1], outs[:o1], sems[:s1])
        e2.finish(ins[i1:], outs[o1:], sems[s1:])

    aliases = dict(e1.aliases)
    aliases.update({i1 + a: o1 + b for a, b in e2.aliases.items()})
    return _Exchange(e1.ins + e2.ins, e1.outs + e2.outs, e1.sems + e2.sems, start, finish, aliases)


def _gather_exchange(shards, parts=None, into=None):
    n = len(shards)
    parts = parts or [None] * n
    into = into or [None] * n
    given = [w for w in range(n) if into[w] is not None]

    def plan(ins, outs, sems):
        send_sems, recv_sems, local_sems = sems
        x, y, c = lax.axis_index("x"), lax.axis_index("y"), lax.axis_index("c")
        me, sibling = (x, y, c), (x, y, 1 - c)
        chips = [(1 - x, y), (x, 1 - y), (1 - x, 1 - y)]

        def cut(ref, w):
            return ref if parts[w] is None else ref.at[pl.ds(parts[w][0], parts[w][1])]

        def rows(w, px, py, pc):
            return cut(outs[w].at[4 * px + 2 * py + pc], w)

        def copy(w, k, block, to, src=None):
            return pltpu.make_async_remote_copy(
                src_ref=rows(w, *block) if src is None else src, dst_ref=rows(w, *block),
                send_sem=send_sems.at[w, k], recv_sem=recv_sems.at[w, k], device_id=to, device_id_type=MESH)

        mine = [pltpu.make_async_copy(cut(ins[w], w), rows(w, *me), local_sems.at[w]) for w in range(n)]
        first = []
        for w in range(n):
            first.append(copy(w, 0, me, sibling, src=cut(ins[w], w)))
            first += [copy(w, 1 + j, me, (*chip, c), src=cut(ins[w], w)) for j, chip in enumerate(chips)]
        return c, me, sibling, chips, copy, mine, first

    def start(ins, outs, sems):
        _, _, _, _, _, mine, first = plan(ins, outs, sems)
        for cp in mine + first:
            cp.start()

    def finish(ins, outs, sems):
        c, me, sibling, chips, copy, mine, first = plan(ins, outs, sems)
        passed = []
        for w in range(n):
            for j, chip in enumerate(chips):
                copy(w, 1 + j, (*chip, c), me).wait_recv()
                passed.append(copy(w, 4 + j, (*chip, c), sibling))
                passed[-1].start()
        for w in range(n):
            copy(w, 0, sibling, me).wait_recv()
            for j, chip in enumerate(chips):
                copy(w, 4 + j, (*chip, 1 - c), me).wait_recv()
        for cp in first + passed:
            cp.wait_send()
        for cp in mine:
            cp.wait()

    return _Exchange(
        list(shards) + [into[w] for w in given], [jax.ShapeDtypeStruct((N_DEV,) + s.shape, s.dtype) for s in shards],
        [pltpu.SemaphoreType.DMA((n, 7)), pltpu.SemaphoreType.DMA((n, 7)), pltpu.SemaphoreType.DMA((n,))], start, finish,
        {n + k: w for k, w in enumerate(given)})


def _sibling_exchange(gs):
    n = len(gs)

    def copies(ins, outs, sems):
        send_sems, recv_sems = sems
        x, y, c = lax.axis_index("x"), lax.axis_index("y"), lax.axis_index("c")
        return [pltpu.make_async_remote_copy(
            src_ref=ins[w].at[2 * q + (1 - c)], dst_ref=outs[w].at[q], send_sem=send_sems.at[w, q],
            recv_sem=recv_sems.at[w, q], device_id=(x, y, 1 - c), device_id_type=MESH) for w in range(n) for q in range(4)]

    def start(ins, outs, sems):
        for cp in copies(ins, outs, sems):
            cp.start()

    def finish(ins, outs, sems):
        cps = copies(ins, outs, sems)
        for cp in cps:
            cp.wait_recv()
        for cp in cps:
            cp.wait_send()

    return _Exchange(gs, [jax.ShapeDtypeStruct((4,) + g.shape[1:], g.dtype) for g in gs],
                     [pltpu.SemaphoreType.DMA((n, 4)), pltpu.SemaphoreType.DMA((n, 4))], start, finish)


def _rs_chip_sum(name, g8, recv, c_idx):
    _, R, C = g8.shape
    tr = _tile(R, 512, 16)
    g42 = g8.reshape(4, 2, R, C)

    def body(c_ref, a_ref, b_ref, o_ref):
        o_ref[...] = (a_ref[...].astype(f32) + b_ref[...].astype(f32)).astype(o_ref.dtype)

    return pl.pallas_call(
        body, name=name, out_shape=jax.ShapeDtypeStruct((4, R, C), g8.dtype),
        grid_spec=pltpu.PrefetchScalarGridSpec(
            num_scalar_prefetch=1, grid=(4, R // tr),
            in_specs=[pl.BlockSpec((None, None, tr, C), lambda q, r, c_ref: (q, c_ref[0], r, 0)),
                      pl.BlockSpec((None, tr, C), lambda q, r, c_ref: (q, r, 0))],
            out_specs=pl.BlockSpec((None, tr, C), lambda q, r, c_ref: (q, r, 0))),
        compiler_params=_cp("parallel", "parallel"),
    )(c_idx, g42, recv)


def _chips_exchange(p4s):
    n = len(p4s)

    def copies(ins, outs, sems):
        send_sems, recv_sems = sems
        x, y, c = lax.axis_index("x"), lax.axis_index("y"), lax.axis_index("c")
        chips = [(1 - x, y), (x, 1 - y), (1 - x, 1 - y)]
        return [pltpu.make_async_remote_copy(
            src_ref=ins[w].at[2 * cx + cy], dst_ref=outs[w].at[k], send_sem=send_sems.at[w, k],
            recv_sem=recv_sems.at[w, k], device_id=(cx, cy, c), device_id_type=MESH)
            for w in range(n) for k, (cx, cy) in enumerate(chips)]

    def start(ins, outs, sems):
        for cp in copies(ins, outs, sems):
            cp.start()

    def finish(ins, outs, sems):
        cps = copies(ins, outs, sems)
        for cp in cps:
            cp.wait_recv()
        for cp in cps:
            cp.wait_send()

    return _Exchange(p4s, [jax.ShapeDtypeStruct((3,) + p.shape[1:], p.dtype) for p in p4s],
                     [pltpu.SemaphoreType.DMA((n, 3)), pltpu.SemaphoreType.DMA((n, 3))], start, finish)


def _adamw_layer(name, layer, w, m, v, p4, recv, q_idx, prev):
    depth, R, C = w.shape
    tr = _tile(R, 256, 8)

    def body(q_ref, w_ref, m_ref, v_ref, a_ref, b_ref, *rest):
        g_ref, d_ref, nm_ref, nv_ref = rest[-4:]
        g = ((a_ref[...].astype(f32) + b_ref[0].astype(f32)) + b_ref[1].astype(f32)) + b_ref[2].astype(f32)
        d, nm, nv = _adam_math(w_ref[...], g, m_ref[...], v_ref[...])
        g_ref[...] = g
        d_ref[...] = d
        nm_ref[...] = nm
        nv_ref[...] = nv

    lay = pl.BlockSpec((None, tr, C), lambda i, q_ref: (layer, i, 0))
    n_prev = 0 if prev is None else 4
    return pl.pallas_call(
        body, name=name, out_shape=[jax.ShapeDtypeStruct((depth, R, C), f32)] * 4,
        grid_spec=pltpu.PrefetchScalarGridSpec(
            num_scalar_prefetch=1, grid=(R // tr,),
            in_specs=[lay, lay, lay, pl.BlockSpec((None, tr, C), lambda i, q_ref: (q_ref[0], i, 0)),
                      pl.BlockSpec((3, tr, C), lambda i, q_ref: (0, i, 0))] + [_ANY] * n_prev,
            out_specs=[lay] * 4),
        input_output_aliases={6 + n: n for n in range(n_prev)},
        compiler_params=_cp("parallel"),
    )(q_idx, w, m, v, p4, recv, *(prev or ()))


_BIG = (("w_in", 1), ("w_br_attn", 1), ("w_br_sg", 1), ("w_out", 0), ("w_ff_gate", 1), ("w_ff_up", 1),
        ("w_ff_down", 0), ("w_ple_gate", 0), ("w_ple", 1))


_TURNED = ("w_in", "w_ff_gate", "w_ff_up")


def _gather_plan(i, depth):
    mixer = ["w_br_attn", "w_br_sg", "w_out"]
    plan = {
        "mm_in": [(i, "w_ff_gate")] + [(i, n) for n in mixer],
        "attn_fwd": [(i, "w_ff_up")],
        "mm_out": [(i, "w_ple_gate"), (i, "w_ple")],
        "mm_ffn_in": [(i, "w_ff_down")],
    }
    if i + 1 < depth:
        plan["mm_ffn_in"] = plan["mm_ffn_in"] + [(i + 1, "w_in", 0)]
        plan["mm_ffn_out"] = [(i + 1, "w_in", 1)]
    return plan


def _layer_fwd(x0, p_i, layer, arrived, sm, tabs, comm):
    S, D = x0.shape
    cos, sin = tabs
    tmm = _tile(S, 1024, 8)
    same = lambda accs, ex: accs

    def W(name):
        return arrived[(layer, name)]

    def hosted(key, fn):
        if key not in comm:
            return fn(None)
        ex, keys = comm[key]()
        res, outs = fn(ex)
        arrived.update(zip(keys, outs))
        return res

    h1 = _norm_fwd("norm_fwd", x0, sm["norm_mix"])
    z = hosted("mm_in", lambda ex: _mm_nn_cols("mm_in", h1, [W("w_in")], [bf16], same, tr=True, comm=ex))[0]
    IN = z.shape[1]

    qkv = _qkv_groups(z)
    o3, lse3 = hosted("attn_fwd", lambda ex: _attn_fwd(qkv, cos, sin, comm=ex))
    attn, lse = _attn_combine(o3, lse3)
    ya = _mm_nn_cols("mm_br_attn", attn, [W("w_br_attn")], [bf16], same, tm_pref=1024)[0]
    sgw = sm["sg_w"].astype(bf16)
    bb = jnp.broadcast_to(sm["sg_b"][:, :, None], (SG_GROUPS, SG_CHUNK, 128))
    sg = _sg_fwd(z, sgw, bb, sm["sg_ln_g"], sm["sg_ln_b"])
    yb = _mm_nn_cols("mm_br_sg", sg, [W("w_br_sg")], [bf16], same, tm_pref=1024)[0]
    merged = _merge_fwd(z, ya, yb, D)
    tn = _tile(D, 1024)
    x1 = hosted("mm_out", lambda ex: _matmul(
        "mm_out", [(merged, W("w_out").reshape(D, D), "nn", 0)], S, D, tmm, tn, 1,
        [((S, D), f32, (tmm, tn), lambda i, j: (i, j))], lambda accs, ex_tiles: [ex_tiles[0] + accs[0]],
        extras=[(x0, (tmm, tn), lambda i, j: (i, j))], chunk=512, comm=ex))[0]
    h2 = _norm_fwd("norm_fwd", x1, sm["norm_ffn"])

    def ffn_ep(accs, ex):
        a, b = accs
        sg = _sigmoid(a)
        silu = a * sg
        return [b * sg * (1.0 + a * (1.0 - sg)), silu, silu * b]

    dfa, dfb, f = hosted("mm_ffn_in", lambda ex: _mm_nn_cols("mm_ffn_in", h2, [W("w_ff_gate"), W("w_ff_up")], [bf16] * 3, ffn_ep,
                                                         tr=True, comm=ex))
    w_down = W("w_ff_down").reshape(-1, D)
    F = w_down.shape[0]
    thin = _tile(D, 512)
    x2 = hosted("mm_ffn_out", lambda ex: _matmul(
        "mm_ffn_out", [(f, w_down, "nn", 0)], S, D, tmm, thin, 1, [((S, D), f32, (tmm, thin), lambda i, j: (i, j))],
        lambda accs, ex_tiles: [ex_tiles[0] + accs[0]], extras=[(x1, (tmm, thin), lambda i, j: (i, j))], comm=ex))[0]
    h3 = _norm_fwd("norm_fwd", x2, sm["norm_ple"])

    e = _mm_nn_cols("mm_ple_emb", p_i, [W("w_ple")], [bf16], same, tm_pref=1024)[0]

    def ple_ep(accs, ex):
        gp = accs[0]
        return [ex[0] + _sigmoid(gp) * ex[1].astype(f32), gp]

    x3, gp = hosted("mm_ple", lambda ex: _matmul(
        "mm_ple", [(h3, W("w_ple_gate").reshape(D, D), "nn", 0)], S, D, tmm, tn, 1,
        [((S, D), f32, (tmm, tn), lambda i, j: (i, j)), ((S, D), bf16, (tmm, tn), lambda i, j: (i, j))],
        ple_ep, extras=[(x2, (tmm, tn), lambda i, j: (i, j)), (e, (tmm, tn), lambda i, j: (i, j))], chunk=512, comm=ex))
    saved = dict(x0=x0, h1=h1, z=z, qkv=qkv, attn=attn, lse=lse, ya=ya, yb=yb, sg=sg, merged=merged, x1=x1,
                 h2=h2, dfa=dfa, dfb=dfb, f=f, x2=x2, h3=h3, gp=gp, e=e, sgw=sgw, bb=bb, IN=IN)
    return x3, saved


def _layer_bwd(dx3, p_i, W, sm, tabs, sv, c_idx):
    S, D = dx3.shape
    w_out, w_down, w_pg = W["w_out"].reshape(D, D), W["w_ff_down"].reshape(-1, D), W["w_ple_gate"].reshape(D, D)
    F = w_down.shape[0]
    cos, sin = tabs
    tmm = _tile(S, 1024, 8)
    tn = _tile(D, 512)
    reduced = {}

    def to_sibling(grads):
        return _sibling_exchange(list(grads.values()))

    def chip_sums(grads, recv):
        return {n: _rs_chip_sum("rs_chip_sum_" + n, grads[n], r, c_idx) for n, r in zip(grads, recv)}

    def to_owners(p4, names=None):
        return _chips_exchange([p4[n] for n in (names or p4)])

    def carry(p4, outs, names=None):
        reduced.update({n: (p4[n], r) for n, r in zip(names or p4, outs)})

    def blocks(full):
        return full.reshape(N_DEV, full.shape[0] // N_DEV, full.shape[1])

    dgp, de = _ple_bwd_ew(dx3, sv["gp"], sv["e"])
    d_w_ple = _mm_tn_cols("mm_dw_ple", p_i, [de], D // N_DEV)[0]
    d_w_pg = blocks(_mm_simple("mm_dw_dd", sv["h3"], dgp, "tn", bf16, 1024, 1024, 2048))
    dh3 = _mm_simple("mm_dh_dd", dgp, w_pg, "nt", bf16, 1024, 1024, 2048)
    dx2, dx2b, dg_ple = _norm_bwd("norm_bwd", dh3, sv["x2"], sm["norm_ple"], dx3)
    tf = _tile(F, 1408)

    def ffn_bwd_ep(accs, ex):
        df = accs[0]
        return [df * ex[0].astype(f32), df * ex[1].astype(f32)]

    th = _tile(S, 512, 8)
    da, db = _matmul("mm_dffn", [(dx2b, w_down, "nt", 0)], S, F, th, tf, 1,
                     [((S, F), bf16, (th, tf), lambda i, j: (i, j))] * 2, ffn_bwd_ep,
                     extras=[(sv["dfa"], (th, tf), lambda i, j: (i, j)), (sv["dfb"], (th, tf), lambda i, j: (i, j))], chunk=512)
    d_w_down = blocks(_mm_simple("mm_dw_fd", sv["f"], dx2b, "tn", bf16, 1408, 1024, 2048))
    g_a = dict(w_ff_down=d_w_down, w_ple_gate=d_w_pg, w_ple=d_w_ple)
    (d_w_gate, d_w_up), recv = _mm_tn_cols("mm_dw_df", sv["h2"], [da, db], F // N_DEV, tr=True, comm=to_sibling(g_a))
    p4_a = chip_sums(g_a, recv)
    g_b = dict(w_ff_gate=d_w_gate, w_ff_up=d_w_up)
    dh2, outs = _mm_nt_cols("mm_dh_ffn", [(da, W["w_ff_gate"]), (db, W["w_ff_up"])], bf16, tr=True,
                            comm=_both(to_owners(p4_a), to_sibling(g_b)))
    carry(p4_a, outs[:len(p4_a)])
    p4_b = chip_sums(g_b, outs[len(p4_a):])
    dx1, dx1b, dg_ffn = _norm_bwd("norm_bwd", dh2, sv["x1"], sm["norm_ffn"], dx2)
    z = sv["z"]
    o_a, o_b = (QKV_W + 2 * SG_W) // tn, (QKV_W + 2 * SG_W + D) // tn

    def merge_bwd_ep(accs, ex):
        dm = accs[0]
        ga, gb = _sigmoid(ex[0].astype(f32)), _sigmoid(ex[1].astype(f32))
        ya, yb = ex[2].astype(f32), ex[3].astype(f32)
        return [dm * ya * ga * (1.0 - ga), dm * yb * gb * (1.0 - gb), dm * ga, dm * gb]

    dga, dgb, dya, dyb = _matmul(
        "mm_dmerge", [(dx1b, w_out, "nt", 0)], S, D, tmm, tn, 1,
        [((S, D), bf16, (tmm, tn), lambda i, j: (i, j))] * 4, merge_bwd_ep,
        extras=[(z, (tmm, tn), lambda i, j: (i, o_a + j)), (z, (tmm, tn), lambda i, j: (i, o_b + j)),
                (sv["ya"], (tmm, tn), lambda i, j: (i, j)), (sv["yb"], (tmm, tn), lambda i, j: (i, j))], chunk=256)
    d_w_out = blocks(_mm_simple("mm_dw_dd", sv["merged"], dx1b, "tn", bf16, 1024, 1024, 2048))
    dsg = _mm_nt_cols("mm_dsg", [(dyb, W["w_br_sg"])], bf16)
    d_w_bsg = _mm_tn_cols("mm_dw_bsg", sv["sg"], [dyb], D // N_DEV)[0]
    dattn = _mm_nt_cols("mm_dattn", [(dya, W["w_br_attn"])], bf16)
    d_w_battn = _mm_tn_cols("mm_dw_battn", sv["attn"], [dya], D // N_DEV)[0]
    sgwt = jnp.swapaxes(sm["sg_w"], 1, 2).astype(bf16)
    du, dv_sg, d_sgw, d_sgb, d_lg, d_lb = _sg_bwd(dsg, z, sv["sgw"], sgwt, sv["bb"], sm["sg_ln_g"], sm["sg_ln_b"])
    b3 = _bwd_groups(sv["attn"], dattn, sv["lse"])
    g_c = dict(w_out=d_w_out, w_br_sg=d_w_bsg, w_br_attn=d_w_battn)
    dqg, outs = _attn_bwd_dq(sv["qkv"], cos, sin, b3, comm=_both(to_owners(p4_b, ["w_ff_gate"]), to_sibling(g_c)))
    carry(p4_b, outs[:1], ["w_ff_gate"])
    p4_c = chip_sums(g_c, outs[1:])
    (dkg, dvg), outs = _attn_bwd_dkv(sv["qkv"], cos, sin, b3, comm=to_owners(p4_b, ["w_ff_up"]))
    carry(p4_b, outs, ["w_ff_up"])
    dz = jnp.concatenate([_to_natural(dqg), _to_natural(dkg), _to_natural(dvg), du, dv_sg, dga, dgb], axis=1)
    (d_w_in,), outs = _mm_tn_cols("mm_dw_in", sv["h1"], [dz], sv["IN"] // N_DEV, tr=True, comm=to_owners(p4_c))
    carry(p4_c, outs)
    g_d = dict(w_in=d_w_in)
    p4_d = chip_sums(g_d, _run_exchange("rs_sibling_w_in", to_sibling(g_d)))
    dh1, outs = _mm_nt_cols("mm_dh_in", [(dz, W["w_in"])], bf16, tr=True, comm=to_owners(p4_d))
    carry(p4_d, outs)
    dx0, _, dg_mix = _norm_bwd("norm_bwd", dh1, sv["x0"], sm["norm_mix"], dx1)
    small = dict(sg_w=d_sgw, sg_b=d_sgb[:, :, 0], sg_ln_g=d_lg[0], sg_ln_b=d_lb[0], norm_mix=dg_mix[0], norm_ffn=dg_ffn[0],
                 norm_ple=dg_ple[0])
    return dx0, reduced, small


_SMALL = ("sg_w", "sg_b", "sg_ln_g", "sg_ln_b", "norm_mix", "norm_ffn", "norm_ple", "norm_final")


SMALL_ROWS = 256


def _pack_small(parts, tail):
    rows = [parts[n].astype(f32).reshape(-1, LANES) for n in _SMALL] + [tail]
    n = sum(r.shape[0] for r in rows)
    return jnp.concatenate(rows + [jnp.zeros((-n % SMALL_ROWS, LANES), f32)], axis=0)


def kernel(x, p, w_in, w_br_attn, w_br_sg, w_out, sg_w, sg_b, sg_ln_g, sg_ln_b, norm_mix, norm_ffn, norm_ple, norm_final, w_ff_gate, w_ff_up, w_ff_down, w_ple_gate, w_ple, loss_target, m_w_in, m_w_br_attn, m_w_br_sg, m_w_out, m_sg_w, m_sg_b, m_sg_ln_g, m_sg_ln_b, m_norm_mix, m_norm_ffn, m_norm_ple, m_norm_final, m_w_ff_gate, m_w_ff_up, m_w_ff_down, m_w_ple_gate, m_w_ple, v_w_in, v_w_br_attn, v_w_br_sg, v_w_out, v_sg_w, v_sg_b, v_sg_ln_g, v_sg_ln_b, v_norm_mix, v_norm_ffn, v_norm_ple, v_norm_final, v_w_ff_gate, v_w_ff_up, v_w_ff_down, v_w_ple_gate, v_w_ple):
    wts = dict(w_in=w_in, w_br_attn=w_br_attn, w_br_sg=w_br_sg, w_out=w_out, w_ff_gate=w_ff_gate, w_ff_up=w_ff_up,
               w_ff_down=w_ff_down, w_ple_gate=w_ple_gate, w_ple=w_ple)
    mom_m = dict(w_in=m_w_in, w_br_attn=m_w_br_attn, w_br_sg=m_w_br_sg, w_out=m_w_out, w_ff_gate=m_w_ff_gate,
                 w_ff_up=m_w_ff_up, w_ff_down=m_w_ff_down, w_ple_gate=m_w_ple_gate, w_ple=m_w_ple)
    mom_v = dict(w_in=v_w_in, w_br_attn=v_w_br_attn, w_br_sg=v_w_br_sg, w_out=v_w_out, w_ff_gate=v_w_ff_gate,
                 w_ff_up=v_w_ff_up, w_ff_down=v_w_ff_down, w_ple_gate=v_w_ple_gate, w_ple=v_w_ple)
    small_w = dict(sg_w=sg_w, sg_b=sg_b, sg_ln_g=sg_ln_g, sg_ln_b=sg_ln_b, norm_mix=norm_mix, norm_ffn=norm_ffn,
                   norm_ple=norm_ple, norm_final=norm_final)
    small_m = dict(sg_w=m_sg_w, sg_b=m_sg_b, sg_ln_g=m_sg_ln_g, sg_ln_b=m_sg_ln_b, norm_mix=m_norm_mix, norm_ffn=m_norm_ffn,
                   norm_ple=m_norm_ple, norm_final=m_norm_final)
    small_v = dict(sg_w=v_sg_w, sg_b=v_sg_b, sg_ln_g=v_sg_ln_g, sg_ln_b=v_sg_ln_b, norm_mix=v_norm_mix, norm_ffn=v_norm_ffn,
                   norm_ple=v_norm_ple, norm_final=v_norm_final)
    depth = w_in.shape[0]
    S = x.shape[1]
    names = [n for n, _ in _BIG]
    c_idx = lax.axis_index("c").astype(jnp.int32).reshape(1)
    q_idx = (2 * lax.axis_index("x") + lax.axis_index("y")).astype(jnp.int32).reshape(1)
    tabs = _rope_tables(S)

    def turned(n, t):
        return jnp.swapaxes(t, -1, -2) if n in _TURNED else t

    arrived = {}

    def gather(keys):
        shards, parts, into = [], [], []
        for key in keys:
            s = turned(key[1], wts[key[1]][key[0]]).astype(bf16)
            shards.append(s)
            half = s.shape[0] // 2
            parts.append((key[2] * half, half) if len(key) == 3 else None)
            into.append(arrived.get(key[:2]) if len(key) == 3 else None)
        return _gather_exchange(shards, parts, into), [key[:2] for key in keys]

    ex, keys = gather([(0, "w_in")])
    arrived.update(zip(keys, _run_exchange("ag_w_in", ex)))

    xs = x[0]
    saved = []
    for i in range(depth):
        sm = {n: small_w[n][i] for n in _SMALL if n != "norm_final"}
        comm = {carrier: functools.partial(gather, keys) for carrier, keys in _gather_plan(i, depth).items()}
        xs, sv = _layer_fwd(xs, p[i, 0], i, arrived, sm, tabs, comm)
        saved.append(sv)
    dx, dg_final, loss_part = _loss_head(xs, norm_final, loss_target[0])

    reduced = [None] * depth
    small_parts = [None] * depth
    for i in reversed(range(depth)):
        sm = {n: small_w[n][i] for n in _SMALL if n != "norm_final"}
        dx, reduced[i], small_parts[i] = _layer_bwd(dx, p[i, 0], {n: arrived[(i, n)] for n in names}, sm, tabs, saved[i], c_idx)
    grad_x = dx[None]

    parts = {n: jnp.stack([small_parts[i][n] for i in range(depth)]) for n in _SMALL if n != "norm_final"}
    parts["norm_final"] = dg_final[0]
    gathered = _all_gather("ag_small", _pack_small(parts, loss_part), in_vmem=True)
    g_s, d_s, nm_s, nv_s = _small_sum_adamw(gathered, _pack_small(small_w, jnp.zeros((8, LANES), f32)),
                                            _pack_small(small_m, jnp.zeros((8, LANES), f32)),
                                            _pack_small(small_v, jnp.ones((8, LANES), f32)))
    loss = g_s[sum(small_w[n].size for n in _SMALL) // LANES, 0]

    def unpack_small(flat):
        out, off = {}, 0
        for n in _SMALL:
            k = small_w[n].size // LANES
            out[n] = flat[off:off + k].reshape(small_w[n].shape)
            off += k
        return out

    sm_g, sm_d, sm_nm, sm_nv = unpack_small(g_s), unpack_small(d_s), unpack_small(nm_s), unpack_small(nv_s)

    big_g, big_d, big_nm, big_nv = {}, {}, {}, {}
    for k, n in enumerate(names):
        outs = None
        for i in range(depth):
            p4, recv2 = reduced[i][n]
            outs = _adamw_layer(f"adamw_{n}_{i}", i, turned(n, wts[n]), turned(n, mom_m[n]), turned(n, mom_v[n]), p4, recv2,
                                q_idx, outs)
        big_g[n], big_d[n], big_nm[n], big_nv[n] = [turned(n, o) for o in outs]

    order = ["w_in", "w_br_attn", "w_br_sg", "w_out", "sg_w", "sg_b", "sg_ln_g", "sg_ln_b", "norm_mix", "norm_ffn", "norm_ple",
             "norm_final", "w_ff_gate", "w_ff_up", "w_ff_down", "w_ple_gate", "w_ple"]

    def pick(big, small):
        return [big[n] if n in big else small[n] for n in order]

    return (loss, grad_x, *pick(big_g, sm_g), *pick(big_d, sm_d), *pick(big_nm, sm_nm), *pick(big_nv, sm_nv))
```

```python
import functools
import math

import jax
import jax.numpy as jnp
from jax import lax
from jax.experimental import pallas as pl
from jax.experimental.pallas import tpu as pltpu

f32 = jnp.float32
bf16 = jnp.bfloat16

HEAD_DIM = 128
N_GROUPS = 3
HEADS = 4
DILATIONS = (1, 4, 16)
RADIUS = 64
BLK = 128
QKV_W = 3 * N_GROUPS * HEADS * HEAD_DIM
ATTN_W = HEADS * HEAD_DIM
SG_CHUNK = 128
SG_GROUPS = 8
SG_W = SG_GROUPS * 128
ROPE_THETA = 10000.0
EPS = 1e-6
NEG = -1e30
N_DEV = 8
LANES = 128

ADAM_LR = 0.001
ADAM_B1 = 0.9
ADAM_B2 = 0.999
ADAM_EPS = 1e-08
ADAM_WD = 0.01
ADAM_STEP = 10

VMEM_LIMIT = 56 * 1024 * 1024
MESH = pl.DeviceIdType.MESH

NN = (((1,), (0,)), ((), ()))
NT = (((1,), (1,)), ((), ()))
TN = (((0,), (0,)), ((), ()))
_DN = {"nn": NN, "nt": NT, "tn": TN}


def _cp(*sem):
    return pltpu.CompilerParams(dimension_semantics=sem, vmem_limit_bytes=VMEM_LIMIT)


def _tile(n, pref, unit=128):
    if n <= pref:
        return n
    t = (pref // unit) * unit
    while t >= unit:
        if n % t == 0:
            return t
        t -= unit
    return n


_ANY = pl.BlockSpec(memory_space=pl.ANY)


def _call(body, *, name, grid, in_specs, out_specs, out_shape, operands, scratch=(), comm=None):
    in_specs, out_specs, out_shape, scratch = list(in_specs), list(out_specs), list(out_shape), list(scratch)
    if comm is None:
        res = pl.pallas_call(
            body, name=name, out_shape=out_shape, grid=grid, in_specs=in_specs, out_specs=out_specs, scratch_shapes=scratch,
            compiler_params=_cp(*(("arbitrary",) * len(grid))))(*operands)
        return res, []
    n_in, n_out, n_scr = len(in_specs), len(out_specs), len(scratch)
    c_in, c_out = len(comm.ins), len(comm.outs)

    def hosted(*refs):
        own_in, refs = refs[:n_in], refs[n_in:]
        ex_in, refs = refs[:c_in], refs[c_in:]
        own_out, refs = refs[:n_out], refs[n_out:]
        ex_out, refs = refs[:c_out], refs[c_out:]
        own_scr, sems = refs[:n_scr], refs[n_scr:]
        ids = [pl.program_id(a) for a in range(len(grid))]
        first = functools.reduce(jnp.logical_and, [i == 0 for i in ids])
        last = functools.reduce(jnp.logical_and, [i == g - 1 for i, g in zip(ids, grid)])

        @pl.when(first)
        def _():
            comm.start(ex_in, ex_out, sems)

        body(*own_in, *own_out, *own_scr)

        @pl.when(last)
        def _():
            comm.finish(ex_in, ex_out, sems)

    res = pl.pallas_call(
        hosted, name=name, out_shape=out_shape + list(comm.outs), grid=grid,
        in_specs=in_specs + [_ANY] * c_in, out_specs=out_specs + [_ANY] * c_out, scratch_shapes=scratch + list(comm.sems),
        input_output_aliases={n_in + a: n_out + b for a, b in comm.aliases.items()},
        compiler_params=pltpu.CompilerParams(dimension_semantics=("arbitrary",) * len(grid), vmem_limit_bytes=VMEM_LIMIT,
                                             has_side_effects=True),
    )(*operands, *comm.ins)
    return res[:n_out], res[n_out:]


def _sigmoid(x):
    return 1.0 / (1.0 + jnp.exp(-x))


_GC = math.sqrt(2.0 / math.pi)
_GA = 0.044715


def _gelu(x):
    return 0.5 * x * (1.0 + jnp.tanh(_GC * (x + _GA * x * x * x)))


def _gelu_grad(x):
    t = jnp.tanh(_GC * (x + _GA * x * x * x))
    return 0.5 * (1.0 + t) + 0.5 * x * (1.0 - t * t) * _GC * (1.0 + 3.0 * _GA * x * x)


def _matmul(name, prods, M, N, tm, tn, nk, outs, epilogue, extras=(), n_acc=1, chunk=None, comm=None):
    in_specs, operands, metas = [], [], []
    for a, b, mode, acc in prods:
        if mode == "tn":
            tk = a.shape[0] // nk
            in_specs += [pl.BlockSpec((tk, tm), lambda i, j, k: (k, i)), pl.BlockSpec((tk, tn), lambda i, j, k: (k, j))]
        elif mode == "nt":
            tk = a.shape[1] // nk
            in_specs += [pl.BlockSpec((tm, tk), lambda i, j, k: (i, k)), pl.BlockSpec((tn, tk), lambda i, j, k: (j, k))]
        else:
            tk = a.shape[1] // nk
            in_specs += [pl.BlockSpec((tm, tk), lambda i, j, k: (i, k)), pl.BlockSpec((tk, tn), lambda i, j, k: (k, j))]
        operands += [a, b]
        metas.append((mode, acc))
    for arr, bshape, imap in extras:
        in_specs.append(pl.BlockSpec(bshape, functools.partial(lambda i, j, k, f: f(i, j), f=imap)))
        operands.append(arr)
    out_specs = [pl.BlockSpec(bs, functools.partial(lambda i, j, k, f: f(i, j), f=imap)) for _, _, bs, imap in outs]
    out_shape = [jax.ShapeDtypeStruct(s, d) for s, d, _, _ in outs]
    n_prod, n_ext, n_out = len(prods), len(extras), len(outs)

    def body(*refs):
        in_refs = refs[: 2 * n_prod]
        ex_refs = refs[2 * n_prod : 2 * n_prod + n_ext]
        out_refs = refs[2 * n_prod + n_ext : 2 * n_prod + n_ext + n_out]
        acc_refs = refs[2 * n_prod + n_ext + n_out :]

        def partials():
            res = [None] * n_acc
            for idx, (mode, acc) in enumerate(metas):
                a = in_refs[2 * idx][...].astype(bf16)
                b = in_refs[2 * idx + 1][...].astype(bf16)
                d = lax.dot_general(a, b, _DN[mode], preferred_element_type=f32)
                res[acc] = d if res[acc] is None else res[acc] + d
            return res

        def finish(accs):
            vals = epilogue(accs, [r[...] for r in ex_refs])
            for r, v in zip(out_refs, vals):
                r[...] = v.astype(r.dtype)

        if nk == 1:
            step = chunk or tn
            for c0 in range(0, tn, step):
                c1 = min(c0 + step, tn)
                res = [None] * n_acc
                for idx, (mode, acc) in enumerate(metas):
                    a = in_refs[2 * idx][...].astype(bf16)
                    b_ref = in_refs[2 * idx + 1]
                    b = (b_ref[c0:c1, :] if mode == "nt" else b_ref[:, c0:c1]).astype(bf16)
                    d = lax.dot_general(a, b, _DN[mode], preferred_element_type=f32)
                    res[acc] = d if res[acc] is None else res[acc] + d
                for r, v in zip(out_refs, epilogue(res, [r[:, c0:c1] for r in ex_refs])):
                    r[:, c0:c1] = v.astype(r.dtype)
        else:
            k = pl.program_id(2)
            parts = partials()

            @pl.when(k == 0)
            def _():
                for r, d in zip(acc_refs, parts):
                    r[...] = d

            @pl.when(k > 0)
            def _():
                for r, d in zip(acc_refs, parts):
                    r[...] += d

            @pl.when(k == nk - 1)
            def _():
                finish([r[...] for r in acc_refs])

    scratch = [pltpu.VMEM((tm, tn), f32) for _ in range(n_acc)] if nk > 1 else []
    res, ex = _call(body, name=name, grid=(M // tm, N // tn, nk), in_specs=in_specs, out_specs=out_specs,
                    out_shape=out_shape, operands=operands, scratch=scratch, comm=comm)
    return res if comm is None else (res, ex)


def _ident(accs, ex):
    return [accs[0]]


def _mm_simple(name, a, b, mode, out_dtype, tm_pref=1024, tn_pref=1024, tk_pref=1024, comm=None):
    if mode == "tn":
        K, M = a.shape
        N = b.shape[1]
    elif mode == "nt":
        M, K = a.shape
        N = b.shape[0]
    else:
        M, K = a.shape
        N = b.shape[1]
    tm, tn, tk = _tile(M, tm_pref), _tile(N, tn_pref), _tile(K, tk_pref)
    res = _matmul(name, [(a, b, mode, 0)], M, N, tm, tn, K // tk,
                  [((M, N), out_dtype, (tm, tn), lambda i, j: (i, j))], _ident, comm=comm)
    return res[0] if comm is None else (res[0][0], res[1])


def _group(c, width_pref=1024):
    g = LANES // math.gcd(c, LANES)
    while g < N_DEV and 2 * g * c <= width_pref:
        g *= 2
    return g


def _join(parts):
    return parts[0] if len(parts) == 1 else jnp.concatenate(parts, axis=1)


def _mm_nn_cols(name, a, gs_list, outs_dtypes, epilogue, extras=(), tm_pref=512, width_pref=1024, tr=False, comm=None):
    M, K = a.shape
    c = gs_list[0].shape[1 if tr else 2]
    g = _group(c, width_pref)
    W = g * c
    tm = _tile(M, tm_pref, 8)
    n_g, n_ex, n_out = len(gs_list), len(extras), len(outs_dtypes)
    n_cols, n_steps = N_DEV * c, N_DEV // g
    if tr:
        g, c = 1, W
    blk = (g, c, K) if tr else (g, K, c)

    def body(*refs):
        a_ref = refs[0]
        g_refs = refs[1:1 + n_g]
        ex_refs = refs[1 + n_g:1 + n_g + n_ex]
        out_refs = refs[1 + n_g + n_ex:]
        av = a_ref[...].astype(bf16)
        cols = [epilogue([lax.dot_general(av, gr[s], NT if tr else NN, preferred_element_type=f32) for gr in g_refs],
                         [r[:, s * c:(s + 1) * c] for r in ex_refs]) for s in range(g)]
        for n, r in enumerate(out_refs):
            r[...] = _join([cols[s][n].astype(r.dtype) for s in range(g)])

    tile = pl.BlockSpec((tm, W), lambda j, i: (i, j))
    res, ex = _call(
        body, name=name, out_shape=[jax.ShapeDtypeStruct((M, n_cols), d) for d in outs_dtypes],
        grid=(n_steps, M // tm),
        in_specs=[pl.BlockSpec((tm, K), lambda j, i: (i, 0))] + [pl.BlockSpec((None,) + blk, lambda j, i: (j, 0, 0, 0))] * n_g
        + [pl.BlockSpec((tm, W), functools.partial(lambda j, i, off: (i, off + j), off=off)) for _, off in extras],
        out_specs=[tile] * n_out,
        operands=[a, *[gm.reshape((n_steps,) + blk) for gm in gs_list], *[arr for arr, _ in extras]], comm=comm)
    return res if comm is None else (res, ex)


def _mm_nt_cols(name, pairs, out_dtype, tm_pref=1024, tn_pref=1024, width_pref=1024, tr=False, comm=None):
    M = pairs[0][0].shape[0]
    c, Kw = pairs[0][1].shape[1:][::1 if tr else -1]
    g = _group(c, width_pref)
    W = g * c
    tm, tn = _tile(M, tm_pref, 8), _tile(Kw, tn_pref)
    nk = N_DEV // g
    n_p = len(pairs)
    if tr:
        g, c = 1, W

    def body(*refs):
        o_ref, acc = refs[2 * n_p], refs[2 * n_p + 1]
        k = pl.program_id(2)
        tot = None
        for n in range(n_p):
            d_ref, g_ref = refs[2 * n], refs[2 * n + 1]
            for s in range(g):
                part = lax.dot_general(d_ref[:, s * c:(s + 1) * c], g_ref[s], NN if tr else NT, preferred_element_type=f32)
                tot = part if tot is None else tot + part

        @pl.when(k == 0)
        def _():
            acc[...] = tot

        @pl.when(k > 0)
        def _():
            acc[...] += tot

        @pl.when(k == nk - 1)
        def _():
            o_ref[...] = acc[...].astype(o_ref.dtype)

    in_specs, operands = [], []
    for d, gm in pairs:
        if tr:
            wspec, wview = pl.BlockSpec((None, g, c, tn), lambda i, j, k: (k, 0, 0, j)), gm.reshape(nk, g, c, Kw)
        else:
            wspec, wview = pl.BlockSpec((None, g, tn, c), lambda i, j, k: (k, 0, j, 0)), gm.reshape(nk, g, Kw, c)
        in_specs += [pl.BlockSpec((tm, W), lambda i, j, k: (i, k)), wspec]
        operands += [d, wview]
    res, ex = _call(
        body, name=name, out_shape=[jax.ShapeDtypeStruct((M, Kw), out_dtype)], grid=(M // tm, Kw // tn, nk),
        in_specs=in_specs, out_specs=[pl.BlockSpec((tm, tn), lambda i, j, k: (i, j))],
        scratch=[pltpu.VMEM((tm, tn), f32)], operands=operands, comm=comm)
    return res[0] if comm is None else (res[0], ex)


def _mm_tn_cols(name, x, ds, c, tm_pref=1024, tk_pref=1024, width_pref=1024, tr=False, comm=None):
    S, Kw = x.shape
    g = _group(c, width_pref)
    W = g * c
    tm, tk = _tile(Kw, tm_pref), _tile(S, tk_pref, 16)
    nk = S // tk
    n_d = len(ds)
    n_steps, c0 = N_DEV // g, c
    blk = (g, c, tm) if tr else (g, tm, c)
    full = (n_steps, g, c, Kw) if tr else (n_steps, g, Kw, c)

    def body(*refs):
        x_ref = refs[0]
        d_refs = refs[1:1 + n_d]
        o_refs = refs[1 + n_d:1 + 2 * n_d]
        accs = refs[1 + 2 * n_d:]
        k = pl.program_id(2)

        @pl.when(k == 0)
        def _():
            for acc in accs:
                acc[...] = jnp.zeros_like(acc)

        xv = x_ref[...].astype(bf16)
        for d_ref, acc in zip(d_refs, accs):
            for s in range(g):
                ds_ = d_ref[:, s * c:(s + 1) * c]
                acc[s] += lax.dot_general(ds_, xv, TN, preferred_element_type=f32) if tr else \
                    lax.dot_general(xv, ds_, TN, preferred_element_type=f32)

        @pl.when(k == nk - 1)
        def _():
            for o_ref, acc in zip(o_refs, accs):
                o_ref[...] = acc[...].astype(o_ref.dtype)

    out_map = (lambda i, j, k: (j, 0, 0, i)) if tr else (lambda i, j, k: (j, 0, i, 0))
    outs, ex = _call(
        body, name=name, out_shape=[jax.ShapeDtypeStruct(full, bf16)] * n_d, grid=(Kw // tm, n_steps, nk),
        in_specs=[pl.BlockSpec((tk, tm), lambda i, j, k: (k, i))] + [pl.BlockSpec((tk, W), lambda i, j, k: (k, j))] * n_d,
        out_specs=[pl.BlockSpec((None,) + blk, out_map)] * n_d,
        scratch=[pltpu.VMEM(blk, f32)] * n_d, operands=[x, *ds], comm=comm)
    outs = [o.reshape((N_DEV, c0, Kw) if tr else (N_DEV, Kw, c0)) for o in outs]
    return outs if comm is None else (outs, ex)


def _norm_fwd(name, x, g):
    S, D = x.shape
    tm = _tile(S, 512, 8)

    def body(x_ref, g_ref, h_ref):
        xv = x_ref[...]
        r = lax.rsqrt(jnp.mean(xv * xv, axis=-1, keepdims=True) + EPS)
        h_ref[...] = (xv * r * g_ref[...]).astype(bf16)

    return pl.pallas_call(
        body, name=name, out_shape=jax.ShapeDtypeStruct((S, D), bf16), grid=(S // tm,),
        in_specs=[pl.BlockSpec((tm, D), lambda i: (i, 0)), pl.BlockSpec((1, D), lambda i: (0, 0))],
        out_specs=pl.BlockSpec((tm, D), lambda i: (i, 0)), compiler_params=_cp("parallel"),
    )(x, g.reshape(1, D))


def _norm_bwd(name, dh, x, g, dx_in):
    S, D = x.shape
    tm = _tile(S, 256, 8)

    def body(dh_ref, x_ref, g_ref, dxi_ref, dx_ref, dxb_ref, dg_ref):
        i = pl.program_id(0)
        xv = x_ref[...]
        r = lax.rsqrt(jnp.mean(xv * xv, axis=-1, keepdims=True) + EPS)
        xh = xv * r
        dhv = dh_ref[...].astype(f32)
        dxh = dhv * g_ref[...]
        dx = dxi_ref[...] + r * (dxh - xh * jnp.mean(dxh * xh, axis=-1, keepdims=True))
        dx_ref[...] = dx
        dxb_ref[...] = dx.astype(bf16)

        @pl.when(i == 0)
        def _():
            dg_ref[...] = jnp.zeros_like(dg_ref)

        dg_ref[...] += jnp.sum(dhv * xh, axis=0, keepdims=True)

    row = pl.BlockSpec((tm, D), lambda i: (i, 0))
    vec = pl.BlockSpec((1, D), lambda i: (0, 0))
    return pl.pallas_call(
        body, name=name,
        out_shape=[jax.ShapeDtypeStruct((S, D), f32), jax.ShapeDtypeStruct((S, D), bf16), jax.ShapeDtypeStruct((1, D), f32)],
        grid=(S // tm,), in_specs=[row, row, vec, row], out_specs=[row, row, vec], compiler_params=_cp("arbitrary"),
    )(dh, x, g.reshape(1, D), dx_in)


def _loss_head(x, g, t):
    S, D = x.shape
    tm = _tile(S, 256, 8)

    def body(x_ref, g_ref, t_ref, dx_ref, dg_ref, loss_ref):
        i = pl.program_id(0)
        xv = x_ref[...]
        r = lax.rsqrt(jnp.mean(xv * xv, axis=-1, keepdims=True) + EPS)
        xh = xv * r
        gv = g_ref[...]
        err = xh * gv - t_ref[...]
        dy = err * (1.0 / D)
        dxh = dy * gv
        dx_ref[...] = r * (dxh - xh * jnp.mean(dxh * xh, axis=-1, keepdims=True))

        @pl.when(i == 0)
        def _():
            dg_ref[...] = jnp.zeros_like(dg_ref)
            loss_ref[...] = jnp.zeros_like(loss_ref)

        dg_ref[...] += jnp.sum(dy * xh, axis=0, keepdims=True)
        row = jnp.sum(err * err, axis=-1, keepdims=True) * (0.5 / D)
        loss_ref[...] += jnp.broadcast_to(jnp.sum(row, axis=0, keepdims=True), loss_ref.shape)

    return pl.pallas_call(
        body, name="loss_head",
        out_shape=[jax.ShapeDtypeStruct((S, D), f32), jax.ShapeDtypeStruct((1, D), f32), jax.ShapeDtypeStruct((8, LANES), f32)],
        grid=(S // tm,),
        in_specs=[pl.BlockSpec((tm, D), lambda i: (i, 0)), pl.BlockSpec((1, D), lambda i: (0, 0)), pl.BlockSpec((tm, D), lambda i: (i, 0))],
        out_specs=[pl.BlockSpec((tm, D), lambda i: (i, 0)), pl.BlockSpec((1, D), lambda i: (0, 0)), pl.BlockSpec((8, LANES), lambda i: (0, 0))],
        compiler_params=_cp("arbitrary"),
    )(x, g.reshape(1, D), t)


def _perm(t, d):
    if d == 1:
        return t
    S, C = t.shape
    return t.reshape(S // d, d, C).transpose(1, 0, 2).reshape(S, C)


def _rope_tables(S):
    half = HEAD_DIM // 2
    pos = jnp.arange(S, dtype=f32)
    inv_freq = ROPE_THETA ** (-jnp.arange(0, HEAD_DIM, 2, dtype=f32) / HEAD_DIM)
    ang = pos[:, None] * inv_freq[None, :]
    c, s = jnp.cos(ang), jnp.sin(ang)
    cos2 = jnp.concatenate([c, c], axis=-1)
    sin2 = jnp.concatenate([-s, s], axis=-1)
    assert cos2.shape == (S, 2 * half)
    return (jnp.stack([_perm(cos2, d) for d in DILATIONS]), jnp.stack([_perm(sin2, d) for d in DILATIONS]))


def _rope(t, c, s):
    return t * c + pltpu.roll(t, HEAD_DIM // 2, 1) * s


def _rope_bwd(dt, c, s):
    return dt * c - pltpu.roll(dt, HEAD_DIM // 2, 1) * s


def _band_bounds(i, nblk):
    g = pl.program_id(0)
    lb = jnp.right_shift(jnp.int32(nblk), 2 * g)
    pos = lax.rem(i, lb)
    lo = jnp.where(pos == 0, BLK, 0)
    hi = jnp.where(pos == lb - 1, 2 * BLK, 3 * BLK)
    return lo, hi


def _cur_spec(width, t=None):
    if t is None:
        return pl.BlockSpec((None, 2 * BLK, width), lambda g, i: (g, i, 0))
    return pl.BlockSpec((None, None, 2 * BLK, width), lambda g, i: (t, g, i, 0))


def _band_specs(width, nblk, t=None):
    lo, hi = (lambda i: jnp.maximum(2 * i - 1, 0)), (lambda i: jnp.minimum(2 * i + 2, nblk - 1))
    if t is None:
        return [pl.BlockSpec((None, BLK, width), lambda g, i: (g, lo(i), 0)), _cur_spec(width),
                pl.BlockSpec((None, BLK, width), lambda g, i: (g, hi(i), 0))]
    return [pl.BlockSpec((None, None, BLK, width), lambda g, i: (t, g, lo(i), 0)), _cur_spec(width, t),
            pl.BlockSpec((None, None, BLK, width), lambda g, i: (t, g, hi(i), 0))]


def _band(sub, prev, pair, nxt, cols=slice(None)):
    pieces = [(prev, slice(None)), (pair, slice(0, BLK)), (pair, slice(BLK, 2 * BLK)), (nxt, slice(None))][sub:sub + 3]
    return jnp.concatenate([ref[rows, cols] for ref, rows in pieces], axis=0)


_ROWS = 2048


def _to_scratch(scr, val):
    for h in range(HEADS):
        scr[h] = val[:, h * HEAD_DIM:(h + 1) * HEAD_DIM].astype(f32)


def _qkv_groups(z):
    S = z.shape[0]
    R = min(_ROWS, S)
    nb = S // R

    def body(x_ref, o_ref, scr):
        g, i = pl.program_id(1), pl.program_id(2)
        _to_scratch(scr, x_ref[...])
        for gi, d in enumerate(DILATIONS):
            @pl.when(g == gi)
            def _():
                n, L = R // d, S // d
                for r in range(d):
                    start = pl.multiple_of(r * L + i * n, 16)
                    for h in range(HEADS):
                        o_ref[pl.ds(start, n), h * HEAD_DIM:(h + 1) * HEAD_DIM] = scr[h, pl.ds(r, n, stride=d), :].astype(bf16)

    return pl.pallas_call(
        body, name="qkv_groups", out_shape=jax.ShapeDtypeStruct((3, N_GROUPS, S, ATTN_W), bf16), grid=(3, N_GROUPS, nb),
        in_specs=[pl.BlockSpec((R, ATTN_W), lambda t, g, i: (i, t * N_GROUPS + g))],
        out_specs=pl.BlockSpec((None, None, S, ATTN_W), lambda t, g, i: (t, g, 0, 0)),
        scratch_shapes=[pltpu.VMEM((HEADS, R, HEAD_DIM), f32)], compiler_params=_cp("arbitrary", "arbitrary", "arbitrary"),
    )(z)


def _bwd_groups(attn, dattn, lse):
    S = attn.shape[0]
    R = min(_ROWS, S)
    nb = S // R

    def body(a_ref, d_ref, l_ref, o_ref, scr):
        t, g, i = pl.program_id(0), pl.program_id(1), pl.program_id(2)

        @pl.when(t == 0)
        def _():
            _to_scratch(scr, d_ref[...])

        @pl.when(t == 1)
        def _():
            _to_scratch(scr, l_ref[...])

        @pl.when(t == 2)
        def _():
            prod = a_ref[...].astype(f32) * d_ref[...].astype(f32)
            for h in range(HEADS):
                part = jnp.sum(prod[:, h * HEAD_DIM:(h + 1) * HEAD_DIM], axis=-1, keepdims=True)
                scr[h] = jnp.broadcast_to(part, (R, HEAD_DIM))

        for gi, d in enumerate(DILATIONS):
            @pl.when(g == gi)
            def _():
                n, L = R // d, S // d
                for r in range(d):
                    start = pl.multiple_of(r * L + i * n, 8)
                    for h in range(HEADS):
                        o_ref[pl.ds(start, n), h * HEAD_DIM:(h + 1) * HEAD_DIM] = scr[h, pl.ds(r, n, stride=d), :]

    def nat(used):
        return pl.BlockSpec((R, ATTN_W), lambda t, g, i: (jnp.where(used(t), i, 0), 0))

    return pl.pallas_call(
        body, name="bwd_groups", out_shape=jax.ShapeDtypeStruct((3, N_GROUPS, S, ATTN_W), f32), grid=(3, N_GROUPS, nb),
        in_specs=[nat(lambda t: t == 2), nat(lambda t: t != 1), nat(lambda t: t == 1)],
        out_specs=pl.BlockSpec((None, None, S, ATTN_W), lambda t, g, i: (t, g, 0, 0)),
        scratch_shapes=[pltpu.VMEM((HEADS, R, HEAD_DIM), f32)], compiler_params=_cp("arbitrary", "arbitrary", "arbitrary"),
    )(attn, dattn, lse)


def _group_views(t3, R):
    S = t3.shape[1]
    views = [t3.reshape(N_GROUPS, d, S // d, ATTN_W) for d in DILATIONS]
    specs = [pl.BlockSpec((None, d, R // d, ATTN_W), functools.partial(lambda i, g: (g, 0, i, 0), g=g))
             for g, d in enumerate(DILATIONS)]
    return views, specs


def _from_groups(scr, ref, d):
    n = ref.shape[1]
    for r in range(d):
        blk = ref[r]
        for h in range(HEADS):
            scr[h, pl.ds(r, n, stride=d), :] = blk[:, h * HEAD_DIM:(h + 1) * HEAD_DIM].astype(f32)


def _to_natural(t3):
    S = t3.shape[1]
    R = min(_ROWS // 2, S)
    views, specs = _group_views(t3, R)

    def body(v0, v1, v2, o_ref, scr):
        for g, (ref, d) in enumerate(zip((v0, v1, v2), DILATIONS)):
            _from_groups(scr, ref, d)
            for h in range(HEADS):
                o_ref[:, g * ATTN_W + h * HEAD_DIM:g * ATTN_W + (h + 1) * HEAD_DIM] = scr[h].astype(bf16)

    return pl.pallas_call(
        body, name="to_natural", out_shape=jax.ShapeDtypeStruct((S, N_GROUPS * ATTN_W), bf16), grid=(S // R,),
        in_specs=specs, out_specs=pl.BlockSpec((R, N_GROUPS * ATTN_W), lambda i: (i, 0)),
        scratch_shapes=[pltpu.VMEM((HEADS, R, HEAD_DIM), f32)], compiler_params=_cp("arbitrary"),
    )(*views)


_SCALE = HEAD_DIM ** -0.5


def _attn_fwd(qkv, cos, sin, comm=None):
    _, _, S, W = qkv.shape
    nblk = S // BLK

    def body(q_ref, kp, kc, kn, vp, vc, vn, cq, sq, ckp, ckc, ckn, skp, skc, skn, o_ref, lse_ref):
        i = pl.program_id(1)
        a = lax.broadcasted_iota(jnp.int32, (BLK, 3 * BLK), 0)
        b = lax.broadcasted_iota(jnp.int32, (BLK, 3 * BLK), 1)
        for sub in range(2):
            rows = slice(sub * BLK, (sub + 1) * BLK)
            lo, hi = _band_bounds(2 * i + sub, nblk)
            mask = (jnp.abs(b - BLK - a) <= RADIUS) & (b >= lo) & (b < hi)
            ck, sk = _band(sub, ckp, ckc, ckn), _band(sub, skp, skc, skn)
            for hh in range(HEADS):
                sl = slice(hh * HEAD_DIM, (hh + 1) * HEAD_DIM)
                qh = _rope(q_ref[rows, sl].astype(f32), cq[rows, :], sq[rows, :]).astype(bf16)
                kh = _rope(_band(sub, kp, kc, kn, sl).astype(f32), ck, sk).astype(bf16)
                vh = _band(sub, vp, vc, vn, sl)
                s = lax.dot_general(qh, kh, NT, preferred_element_type=f32) * _SCALE
                s = jnp.where(mask, s, NEG)
                m = jnp.max(s, axis=-1, keepdims=True)
                e = jnp.exp(s - m)
                den = jnp.sum(e, axis=-1, keepdims=True)
                o = lax.dot_general(e.astype(bf16), vh, NN, preferred_element_type=f32) * (1.0 / den)
                o_ref[rows, sl] = o.astype(bf16)
                lse_ref[rows, sl] = jnp.broadcast_to(m + jnp.log(den), (BLK, HEAD_DIM))

    blk = _cur_spec(W)
    tab = _cur_spec(HEAD_DIM)
    res, ex = _call(
        body, name="attn_fwd",
        out_shape=[jax.ShapeDtypeStruct((N_GROUPS, S, W), bf16), jax.ShapeDtypeStruct((N_GROUPS, S, W), f32)],
        grid=(N_GROUPS, nblk // 2),
        in_specs=[_cur_spec(W, 0)] + _band_specs(W, nblk, 1) + _band_specs(W, nblk, 2) + [tab, tab]
        + _band_specs(HEAD_DIM, nblk) * 2,
        out_specs=[blk, blk], operands=[qkv] * 7 + [cos, sin, cos, cos, cos, sin, sin, sin], comm=comm)
    return res if comm is None else (res, ex)


def _attn_combine(o3, lse3):
    _, S, W = o3.shape
    R = min(_ROWS // 2, S)
    o_views, specs = _group_views(o3, R)
    l_views, _ = _group_views(lse3, R)

    def body(o0, o1, o2, l0, l1, l2, attn_ref, lse_ref, so0, so1, so2, sl0, sl1, sl2):
        for ref, scr, d in zip((o0, o1, o2, l0, l1, l2), (so0, so1, so2, sl0, sl1, sl2), DILATIONS * 2):
            _from_groups(scr, ref, d)
        for h in range(HEADS):
            a0, a1, a2 = sl0[h], sl1[h], sl2[h]
            m = jnp.maximum(jnp.maximum(a0, a1), a2)
            w0, w1, w2 = jnp.exp(a0 - m), jnp.exp(a1 - m), jnp.exp(a2 - m)
            den = w0 + w1 + w2
            acc = w0 * so0[h] + w1 * so1[h] + w2 * so2[h]
            attn_ref[:, h * HEAD_DIM:(h + 1) * HEAD_DIM] = (acc * (1.0 / den)).astype(bf16)
            lse_ref[:, h * HEAD_DIM:(h + 1) * HEAD_DIM] = m + jnp.log(den)

    nat = pl.BlockSpec((R, W), lambda i: (i, 0))
    return pl.pallas_call(
        body, name="attn_combine", out_shape=[jax.ShapeDtypeStruct((S, W), bf16), jax.ShapeDtypeStruct((S, W), f32)],
        grid=(S // R,), in_specs=specs * 2, out_specs=[nat, nat],
        scratch_shapes=[pltpu.VMEM((HEADS, R, HEAD_DIM), f32)] * 6, compiler_params=_cp("arbitrary"),
    )(*o_views, *l_views)


def _attn_bwd_dq(qkv, cos, sin, b3, comm=None):
    _, _, S, W = qkv.shape
    nblk = S // BLK

    def body(q_ref, kp, kc, kn, vp, vc, vn, cq, sq, ckp, ckc, ckn, skp, skc, skn, da_ref, l_ref, dl_ref, dq_ref):
        i = pl.program_id(1)
        a = lax.broadcasted_iota(jnp.int32, (BLK, 3 * BLK), 0)
        b = lax.broadcasted_iota(jnp.int32, (BLK, 3 * BLK), 1)
        for sub in range(2):
            rows = slice(sub * BLK, (sub + 1) * BLK)
            lo, hi = _band_bounds(2 * i + sub, nblk)
            mask = (jnp.abs(b - BLK - a) <= RADIUS) & (b >= lo) & (b < hi)
            ck, sk = _band(sub, ckp, ckc, ckn), _band(sub, skp, skc, skn)
            for hh in range(HEADS):
                sl = slice(hh * HEAD_DIM, (hh + 1) * HEAD_DIM)
                qh = _rope(q_ref[rows, sl].astype(f32), cq[rows, :], sq[rows, :]).astype(bf16)
                kh = _rope(_band(sub, kp, kc, kn, sl).astype(f32), ck, sk).astype(bf16)
                vh = _band(sub, vp, vc, vn, sl)
                s = lax.dot_general(qh, kh, NT, preferred_element_type=f32) * _SCALE
                lh = l_ref[rows, sl]
                l3 = jnp.concatenate([lh, lh, lh], axis=1)
                p = jnp.exp(jnp.where(mask, s - l3, NEG))
                dp = lax.dot_general(da_ref[rows, sl].astype(bf16), vh, NT, preferred_element_type=f32)
                dh = dl_ref[rows, sl]
                ds = p * (dp - jnp.concatenate([dh, dh, dh], axis=1))
                dqh = lax.dot_general(ds.astype(bf16), kh, NN, preferred_element_type=f32) * _SCALE
                dq_ref[rows, sl] = _rope_bwd(dqh, cq[rows, :], sq[rows, :]).astype(bf16)

    tab = _cur_spec(HEAD_DIM)
    res, ex = _call(
        body, name="attn_bwd_dq", out_shape=[jax.ShapeDtypeStruct((N_GROUPS, S, W), bf16)], grid=(N_GROUPS, nblk // 2),
        in_specs=[_cur_spec(W, 0)] + _band_specs(W, nblk, 1) + _band_specs(W, nblk, 2) + [tab, tab]
        + _band_specs(HEAD_DIM, nblk) * 2 + [_cur_spec(W, 0), _cur_spec(W, 1), _cur_spec(W, 2)],
        out_specs=[_cur_spec(W)], operands=[qkv] * 7 + [cos, sin, cos, cos, cos, sin, sin, sin, b3, b3, b3], comm=comm)
    return res[0] if comm is None else (res[0], ex)


def _attn_bwd_dkv(qkv, cos, sin, b3, comm=None):
    _, _, S, W = qkv.shape
    nblk = S // BLK

    def body(k_ref, v_ref, ck, sk, qp, qc, qn, cqp, cqc, cqn, sqp, sqc, sqn, dap, dac, dan, lp, lc, ln, dlp, dlc, dln,
             dk_ref, dv_ref):
        j = pl.program_id(1)
        a = lax.broadcasted_iota(jnp.int32, (3 * BLK, BLK), 0)
        b = lax.broadcasted_iota(jnp.int32, (3 * BLK, BLK), 1)
        for sub in range(2):
            rows = slice(sub * BLK, (sub + 1) * BLK)
            lo, hi = _band_bounds(2 * j + sub, nblk)
            mask = (jnp.abs(b - (a - BLK)) <= RADIUS) & (a >= lo) & (a < hi)
            cq, sq = _band(sub, cqp, cqc, cqn), _band(sub, sqp, sqc, sqn)
            for hh in range(HEADS):
                sl = slice(hh * HEAD_DIM, (hh + 1) * HEAD_DIM)
                kh = _rope(k_ref[rows, sl].astype(f32), ck[rows, :], sk[rows, :]).astype(bf16)
                vh = v_ref[rows, sl]
                qh = _rope(_band(sub, qp, qc, qn, sl).astype(f32), cq, sq).astype(bf16)
                dah = _band(sub, dap, dac, dan, sl).astype(bf16)
                lh = _band(sub, lp, lc, ln, sl)
                dlh = _band(sub, dlp, dlc, dln, sl)
                s = lax.dot_general(qh, kh, NT, preferred_element_type=f32) * _SCALE
                p = jnp.exp(jnp.where(mask, s - lh, NEG))
                dv_ref[rows, sl] = lax.dot_general(p.astype(bf16), dah, TN, preferred_element_type=f32).astype(bf16)
                dp = lax.dot_general(dah, vh, NT, preferred_element_type=f32)
                ds = p * (dp - dlh)
                dkh = lax.dot_general(ds.astype(bf16), qh, TN, preferred_element_type=f32) * _SCALE
                dk_ref[rows, sl] = _rope_bwd(dkh, ck[rows, :], sk[rows, :]).astype(bf16)

    blk, tab, bt = _cur_spec(W), _cur_spec(HEAD_DIM), _band_specs(HEAD_DIM, nblk)
    res, ex = _call(
        body, name="attn_bwd_dkv",
        out_shape=[jax.ShapeDtypeStruct((N_GROUPS, S, W), bf16), jax.ShapeDtypeStruct((N_GROUPS, S, W), bf16)],
        grid=(N_GROUPS, nblk // 2),
        in_specs=[_cur_spec(W, 1), _cur_spec(W, 2), tab, tab] + _band_specs(W, nblk, 0) + bt + bt
        + _band_specs(W, nblk, 0) + _band_specs(W, nblk, 1) + _band_specs(W, nblk, 2), out_specs=[blk, blk],
        operands=[qkv, qkv, cos, sin, qkv, qkv, qkv, cos, cos, cos, sin, sin, sin] + [b3] * 9, comm=comm)
    return res if comm is None else (res, ex)


_SG_ROWS = 512


def _sg_z_specs(tm, half):
    o = QKV_W // half
    return [pl.BlockSpec((tm, half), functools.partial(lambda i, c: (i, c), c=o + n)) for n in range(4)]


def _sg_norm(v, lg, lb):
    gv = _gelu(v)
    mu = jnp.mean(gv, axis=-1, keepdims=True)
    xc = gv - mu
    rstd = lax.rsqrt(jnp.mean(xc * xc, axis=-1, keepdims=True) + EPS)
    xh = xc * rstd
    return xh, rstd, xh * lg + lb


def _sg_fwd(z, w, bb, lg, lb):
    S = z.shape[0]
    tm = _tile(S, _SG_ROWS, SG_CHUNK)
    half = SG_W // 2

    def body(u0, u1, v0, v1, w_ref, bb_ref, lg_ref, lb_ref, o_ref):
        u = jnp.concatenate([u0[...], u1[...]], axis=1).astype(f32)
        v = jnp.concatenate([v0[...], v1[...]], axis=1).astype(f32)
        gu = _gelu(u)
        _, _, vn = _sg_norm(v, lg_ref[...], lb_ref[...])
        vnb = vn.astype(bf16)
        for c in range(tm // SG_CHUNK):
            rs = slice(c * SG_CHUNK, (c + 1) * SG_CHUNK)
            for g in range(SG_GROUPS):
                cs = slice(g * 128, (g + 1) * 128)
                mixed = lax.dot_general(w_ref[g], vnb[rs, cs], NN, preferred_element_type=f32) + bb_ref[g]
                o_ref[rs, cs] = (gu[rs, cs] * mixed).astype(bf16)

    full3 = pl.BlockSpec((SG_GROUPS, 128, 128), lambda i: (0, 0, 0))
    vec = pl.BlockSpec((1, SG_W), lambda i: (0, 0))
    return pl.pallas_call(
        body, name="sg_fwd", out_shape=jax.ShapeDtypeStruct((S, SG_W), bf16), grid=(S // tm,),
        in_specs=_sg_z_specs(tm, half) + [full3, full3, vec, vec],
        out_specs=pl.BlockSpec((tm, SG_W), lambda i: (i, 0)), compiler_params=_cp("parallel"),
    )(z, z, z, z, w, bb, lg.reshape(1, SG_W), lb.reshape(1, SG_W))


def _sg_bwd(dsg, z, w, wt, bb, lg, lb):
    S = z.shape[0]
    tm = _tile(S, _SG_ROWS, SG_CHUNK)
    half = SG_W // 2

    def body(d_ref, u0, u1, v0, v1, w_ref, wt_ref, bb_ref, lg_ref, lb_ref, du_ref, dv_ref, dw_ref, db_ref, dlg_ref, dlb_ref, dvn_scr):
        i = pl.program_id(0)

        @pl.when(i == 0)
        def _():
            dw_ref[...] = jnp.zeros_like(dw_ref)
            db_ref[...] = jnp.zeros_like(db_ref)
            dlg_ref[...] = jnp.zeros_like(dlg_ref)
            dlb_ref[...] = jnp.zeros_like(dlb_ref)

        u = jnp.concatenate([u0[...], u1[...]], axis=1).astype(f32)
        v = jnp.concatenate([v0[...], v1[...]], axis=1).astype(f32)
        gu = _gelu(u)
        dgu = _gelu_grad(u)
        xh, rstd, vn = _sg_norm(v, lg_ref[...], lb_ref[...])
        vnb = vn.astype(bf16)
        dsg_v = d_ref[...].astype(f32)
        for g in range(SG_GROUPS):
            cs = slice(g * 128, (g + 1) * 128)
            dw_g = jnp.zeros((128, 128), f32)
            db_g = jnp.zeros((128, 1), f32)
            for c in range(tm // SG_CHUNK):
                rs = slice(c * SG_CHUNK, (c + 1) * SG_CHUNK)
                ds = dsg_v[rs, cs]
                mixed = lax.dot_general(w_ref[g], vnb[rs, cs], NN, preferred_element_type=f32) + bb_ref[g]
                du_ref[rs, cs] = (ds * mixed * dgu[rs, cs]).astype(bf16)
                dmix = ds * gu[rs, cs]
                dmb = dmix.astype(bf16)
                dw_g = dw_g + lax.dot_general(dmb, vnb[rs, cs], NT, preferred_element_type=f32)
                db_g = db_g + jnp.sum(dmix, axis=-1, keepdims=True)
                dvn_scr[rs, cs] = lax.dot_general(wt_ref[g], dmb, NN, preferred_element_type=f32)
            dw_ref[g] += dw_g
            db_ref[g] += jnp.broadcast_to(db_g, (128, 128))
        dvn = dvn_scr[...]
        dlg_ref[...] += jnp.sum(dvn * xh, axis=0, keepdims=True)
        dlb_ref[...] += jnp.sum(dvn, axis=0, keepdims=True)
        dxh = dvn * lg_ref[...]
        dgv = rstd * (dxh - jnp.mean(dxh, axis=-1, keepdims=True) - xh * jnp.mean(dxh * xh, axis=-1, keepdims=True))
        dv_ref[...] = (dgv * _gelu_grad(v)).astype(bf16)

    full3 = pl.BlockSpec((SG_GROUPS, 128, 128), lambda i: (0, 0, 0))
    vec = pl.BlockSpec((1, SG_W), lambda i: (0, 0))
    row = pl.BlockSpec((tm, SG_W), lambda i: (i, 0))
    return pl.pallas_call(
        body, name="sg_bwd",
        out_shape=[jax.ShapeDtypeStruct((S, SG_W), bf16), jax.ShapeDtypeStruct((S, SG_W), bf16),
                   jax.ShapeDtypeStruct((SG_GROUPS, 128, 128), f32), jax.ShapeDtypeStruct((SG_GROUPS, 128, 128), f32),
                   jax.ShapeDtypeStruct((1, SG_W), f32), jax.ShapeDtypeStruct((1, SG_W), f32)],
        grid=(S // tm,),
        in_specs=[row] + _sg_z_specs(tm, half) + [full3, full3, full3, vec, vec],
        out_specs=[row, row, full3, full3, vec, vec],
        scratch_shapes=[pltpu.VMEM((tm, SG_W), f32)], compiler_params=_cp("arbitrary"),
    )(dsg, z, z, z, z, w, wt, bb, lg.reshape(1, SG_W), lb.reshape(1, SG_W))


def _ple_bwd_ew(dx, gp, e):
    S, D = dx.shape
    tm, tc = _tile(S, 512, 8), _tile(D, 1024)

    def body(dx_ref, gp_ref, e_ref, dgp_ref, de_ref):
        dxv = dx_ref[...]
        sg = _sigmoid(gp_ref[...].astype(f32))
        dgp_ref[...] = (dxv * e_ref[...].astype(f32) * sg * (1.0 - sg)).astype(bf16)
        de_ref[...] = (dxv * sg).astype(bf16)

    blk = pl.BlockSpec((tm, tc), lambda i, j: (i, j))
    return pl.pallas_call(
        body, name="ple_bwd_ew", out_shape=[jax.ShapeDtypeStruct((S, D), bf16)] * 2, grid=(S // tm, D // tc),
        in_specs=[blk, blk, blk], out_specs=[blk, blk], compiler_params=_cp("parallel", "parallel"),
    )(dx, gp, e)


def _adam_math(w, g, m, v):
    m = ADAM_B1 * m + (1.0 - ADAM_B1) * g
    v = ADAM_B2 * v + (1.0 - ADAM_B2) * (g * g)
    m_hat = m / (1.0 - ADAM_B1 ** ADAM_STEP)
    v_hat = v / (1.0 - ADAM_B2 ** ADAM_STEP)
    delta = -ADAM_LR * (m_hat / (jnp.sqrt(v_hat) + ADAM_EPS) + ADAM_WD * w)
    return delta, m, v


def _small_sum_adamw(gathered, w, m, v):
    _, R, _ = gathered.shape
    tr = _tile(R, 1024, SMALL_ROWS)

    def body(p_ref, w_ref, m_ref, v_ref, g_ref, d_ref, nm_ref, nv_ref):
        g = p_ref[0]
        for n in range(1, N_DEV):
            g = g + p_ref[n]
        d, nm, nv = _adam_math(w_ref[...], g, m_ref[...], v_ref[...])
        g_ref[...] = g
        d_ref[...] = d
        nm_ref[...] = nm
        nv_ref[...] = nv

    blk = pl.BlockSpec((tr, LANES), lambda i: (i, 0))
    return pl.pallas_call(
        body, name="small_sum_adamw", out_shape=[jax.ShapeDtypeStruct((R, LANES), f32)] * 4, grid=(R // tr,),
        in_specs=[pl.BlockSpec((N_DEV, tr, LANES), lambda i: (0, i, 0)), blk, blk, blk], out_specs=[blk] * 4,
        compiler_params=_cp("parallel"),
    )(gathered, w, m, v)


def _all_gather(name, shard, in_vmem=False):
    R, C = shard.shape

    def body(x_ref, out_ref, send_sems, recv_sems, local_sem):
        x, y, c = lax.axis_index("x"), lax.axis_index("y"), lax.axis_index("c")
        me, sibling = (x, y, c), (x, y, 1 - c)
        chips = [(1 - x, y), (x, 1 - y), (1 - x, 1 - y)]

        def rows(px, py, pc):
            return out_ref.at[4 * px + 2 * py + pc]

        def copy(k, block, to, src=None):
            return pltpu.make_async_remote_copy(
                src_ref=rows(*block) if src is None else src, dst_ref=rows(*block),
                send_sem=send_sems.at[k], recv_sem=recv_sems.at[k], device_id=to, device_id_type=MESH)

        mine = pltpu.make_async_copy(x_ref, rows(*me), local_sem)
        mine.start()
        first = [copy(0, me, sibling, src=x_ref)]
        first += [copy(1 + j, me, (*chip, c), src=x_ref) for j, chip in enumerate(chips)]
        for cp in first:
            cp.start()
        passed = [copy(4 + j, (*chip, c), sibling) for j, chip in enumerate(chips)]
        for j, chip in enumerate(chips):
            copy(1 + j, (*chip, c), me).wait_recv()
            passed[j].start()
        copy(0, sibling, me).wait_recv()
        for j, chip in enumerate(chips):
            copy(4 + j, (*chip, 1 - c), me).wait_recv()
        for cp in first + passed:
            cp.wait_send()
        mine.wait()

    space = pl.BlockSpec(memory_space=pltpu.VMEM) if in_vmem else _ANY
    return pl.pallas_call(
        body, name=name, out_shape=jax.ShapeDtypeStruct((N_DEV, R, C), shard.dtype),
        in_specs=[space], out_specs=space,
        scratch_shapes=[pltpu.SemaphoreType.DMA((7,)), pltpu.SemaphoreType.DMA((7,)), pltpu.SemaphoreType.DMA],
        compiler_params=pltpu.CompilerParams(has_side_effects=True, vmem_limit_bytes=VMEM_LIMIT),
    )(shard)


class _Exchange:
    def __init__(self, ins, outs, sems, start, finish, aliases=None):
        self.ins, self.outs, self.sems, self.start, self.finish = list(ins), list(outs), list(sems), start, finish
        self.aliases = dict(aliases or {})


def _run_exchange(name, ex):
    c_in, c_out = len(ex.ins), len(ex.outs)

    def body(*refs):
        ins, outs, sems = refs[:c_in], refs[c_in:c_in + c_out], refs[c_in + c_out:]
        ex.start(ins, outs, sems)
        ex.finish(ins, outs, sems)

    return pl.pallas_call(
        body, name=name, out_shape=ex.outs, in_specs=[_ANY] * c_in, out_specs=[_ANY] * c_out, scratch_shapes=ex.sems,
        input_output_aliases=ex.aliases,
        compiler_params=pltpu.CompilerParams(has_side_effects=True, vmem_limit_bytes=VMEM_LIMIT),
    )(*ex.ins)


def _both(e1, e2):
    i1, o1, s1 = len(e1.ins), len(e1.outs), len(e1.sems)

    def start(ins, outs, sems):
        e1.start(ins[:i1], outs[:o1], sems[:s1])
        e2.start(ins[i1:], outs[o1:], sems[s1:])

    def finish(ins, outs, sems):
        e1.finish(ins[:i1], outs[:o1], sems[:s1])
        e2.finish(ins[i1:], outs[o1:], sems[s1:])

    aliases = dict(e1.aliases)
    aliases.update({i1 + a: o1 + b for a, b in e2.aliases.items()})
    return _Exchange(e1.ins + e2.ins, e1.outs + e2.outs, e1.sems + e2.sems, start, finish, aliases)


def _gather_exchange(shards, parts=None, into=None):
    n = len(shards)
    parts = parts or [None] * n
    into = into or [None] * n
    given = [w for w in range(n) if into[w] is not None]

    def plan(ins, outs, sems):
        send_sems, recv_sems, local_sems = sems
        x, y, c = lax.axis_index("x"), lax.axis_index("y"), lax.axis_index("c")
        me, sibling = (x, y, c), (x, y, 1 - c)
        chips = [(1 - x, y), (x, 1 - y), (1 - x, 1 - y)]

        def cut(ref, w):
            return ref if parts[w] is None else ref.at[pl.ds(parts[w][0], parts[w][1])]

        def rows(w, px, py, pc):
            return cut(outs[w].at[4 * px + 2 * py + pc], w)

        def copy(w, k, block, to, src=None):
            return pltpu.make_async_remote_copy(
                src_ref=rows(w, *block) if src is None else src, dst_ref=rows(w, *block),
                send_sem=send_sems.at[w, k], recv_sem=recv_sems.at[w, k], device_id=to, device_id_type=MESH)

        mine = [pltpu.make_async_copy(cut(ins[w], w), rows(w, *me), local_sems.at[w]) for w in range(n)]
        first = []
        for w in range(n):
            first.append(copy(w, 0, me, sibling, src=cut(ins[w], w)))
            first += [copy(w, 1 + j, me, (*chip, c), src=cut(ins[w], w)) for j, chip in enumerate(chips)]
        return c, me, sibling, chips, copy, mine, first

    def start(ins, outs, sems):
        _, _, _, _, _, mine, first = plan(ins, outs, sems)
        for cp in mine + first:
            cp.start()

    def finish(ins, outs, sems):
        c, me, sibling, chips, copy, mine, first = plan(ins, outs, sems)
        passed = []
        for w in range(n):
            for j, chip in enumerate(chips):
                copy(w, 1 + j, (*chip, c), me).wait_recv()
                passed.append(copy(w, 4 + j, (*chip, c), sibling))
                passed[-1].start()
        for w in range(n):
            copy(w, 0, sibling, me).wait_recv()
            for j, chip in enumerate(chips):
                copy(w, 4 + j, (*chip, 1 - c), me).wait_recv()
        for cp in first + passed:
            cp.wait_send()
        for cp in mine:
            cp.wait()

    return _Exchange(
        list(shards) + [into[w] for w in given], [jax.ShapeDtypeStruct((N_DEV,) + s.shape, s.dtype) for s in shards],
        [pltpu.SemaphoreType.DMA((n, 7)), pltpu.SemaphoreType.DMA((n, 7)), pltpu.SemaphoreType.DMA((n,))], start, finish,
        {n + k: w for k, w in enumerate(given)})


def _sibling_exchange(gs):
    n = len(gs)

    def copies(ins, outs, sems):
        send_sems, recv_sems = sems
        x, y, c = lax.axis_index("x"), lax.axis_index("y"), lax.axis_index("c")
        return [pltpu.make_async_remote_copy(
            src_ref=ins[w].at[2 * q + (1 - c)], dst_ref=outs[w].at[q], send_sem=send_sems.at[w, q],
            recv_sem=recv_sems.at[w, q], device_id=(x, y, 1 - c), device_id_type=MESH) for w in range(n) for q in range(4)]

    def start(ins, outs, sems):
        for cp in copies(ins, outs, sems):
            cp.start()

    def finish(ins, outs, sems):
        cps = copies(ins, outs, sems)
        for cp in cps:
            cp.wait_recv()
        for cp in cps:
            cp.wait_send()

    return _Exchange(gs, [jax.ShapeDtypeStruct((4,) + g.shape[1:], g.dtype) for g in gs],
                     [pltpu.SemaphoreType.DMA((n, 4)), pltpu.SemaphoreType.DMA((n, 4))], start, finish)


def _rs_chip_sum(name, g8, recv, c_idx):
    _, R, C = g8.shape
    tr = _tile(R, 512, 16)
    g42 = g8.reshape(4, 2, R, C)

    def body(c_ref, a_ref, b_ref, o_ref):
        o_ref[...] = (a_ref[...].astype(f32) + b_ref[...].astype(f32)).astype(o_ref.dtype)

    return pl.pallas_call(
        body, name=name, out_shape=jax.ShapeDtypeStruct((4, R, C), g8.dtype),
        grid_spec=pltpu.PrefetchScalarGridSpec(
            num_scalar_prefetch=1, grid=(4, R // tr),
            in_specs=[pl.BlockSpec((None, None, tr, C), lambda q, r, c_ref: (q, c_ref[0], r, 0)),
                      pl.BlockSpec((None, tr, C), lambda q, r, c_ref: (q, r, 0))],
            out_specs=pl.BlockSpec((None, tr, C), lambda q, r, c_ref: (q, r, 0))),
        compiler_params=_cp("parallel", "parallel"),
    )(c_idx, g42, recv)


def _chips_exchange(p4s):
    n = len(p4s)

    def copies(ins, outs, sems):
        send_sems, recv_sems = sems
        x, y, c = lax.axis_index("x"), lax.axis_index("y"), lax.axis_index("c")
        chips = [(1 - x, y), (x, 1 - y), (1 - x, 1 - y)]
        return [pltpu.make_async_remote_copy(
            src_ref=ins[w].at[2 * cx + cy], dst_ref=outs[w].at[k], send_sem=send_sems.at[w, k],
            recv_sem=recv_sems.at[w, k], device_id=(cx, cy, c), device_id_type=MESH)
            for w in range(n) for k, (cx, cy) in enumerate(chips)]

    def start(ins, outs, sems):
        for cp in copies(ins, outs, sems):
            cp.start()

    def finish(ins, outs, sems):
        cps = copies(ins, outs, sems)
        for cp in cps:
            cp.wait_recv()
        for cp in cps:
            cp.wait_send()

    return _Exchange(p4s, [jax.ShapeDtypeStruct((3,) + p.shape[1:], p.dtype) for p in p4s],
                     [pltpu.SemaphoreType.DMA((n, 3)), pltpu.SemaphoreType.DMA((n, 3))], start, finish)


def _adamw_layer(name, layer, w, m, v, p4, recv, q_idx, prev):
    depth, R, C = w.shape
    tr = _tile(R, 256, 8)

    def body(q_ref, w_ref, m_ref, v_ref, a_ref, b_ref, *rest):
        g_ref, d_ref, nm_ref, nv_ref = rest[-4:]
        g = ((a_ref[...].astype(f32) + b_ref[0].astype(f32)) + b_ref[1].astype(f32)) + b_ref[2].astype(f32)
        d, nm, nv = _adam_math(w_ref[...], g, m_ref[...], v_ref[...])
        g_ref[...] = g
        d_ref[...] = d
        nm_ref[...] = nm
        nv_ref[...] = nv

    lay = pl.BlockSpec((None, tr, C), lambda i, q_ref: (layer, i, 0))
    n_prev = 0 if prev is None else 4
    return pl.pallas_call(
        body, name=name, out_shape=[jax.ShapeDtypeStruct((depth, R, C), f32)] * 4,
        grid_spec=pltpu.PrefetchScalarGridSpec(
            num_scalar_prefetch=1, grid=(R // tr,),
            in_specs=[lay, lay, lay, pl.BlockSpec((None, tr, C), lambda i, q_ref: (q_ref[0], i, 0)),
                      pl.BlockSpec((3, tr, C), lambda i, q_ref: (0, i, 0))] + [_ANY] * n_prev,
            out_specs=[lay] * 4),
        input_output_aliases={6 + n: n for n in range(n_prev)},
        compiler_params=_cp("parallel"),
    )(q_idx, w, m, v, p4, recv, *(prev or ()))


_BIG = (("w_in", 1), ("w_br_attn", 1), ("w_br_sg", 1), ("w_out", 0), ("w_ff_gate", 1), ("w_ff_up", 1),
        ("w_ff_down", 0), ("w_ple_gate", 0), ("w_ple", 1))


_TURNED = ("w_in", "w_ff_gate", "w_ff_up")


def _gather_plan(i, depth):
    mixer = ["w_br_attn", "w_br_sg", "w_out"]
    plan = {
        "mm_in": [(i, "w_ff_gate")] + [(i, n) for n in mixer],
        "attn_fwd": [(i, "w_ff_up")],
        "mm_ffn_in": [(i, "w_ff_down")],
    }
    last = [(i, "w_ple_gate"), (i, "w_ple")]
    if i + 1 < depth:
        plan["mm_out"] = last
        plan["mm_ffn_in"] = plan["mm_ffn_in"] + [(i + 1, "w_in", 0)]
        plan["mm_ffn_out"] = [(i + 1, "w_in", 1)]
    else:
        plan["mm_ffn_out"] = last
    return plan


def _layer_fwd(x0, p_i, layer, arrived, sm, tabs, comm):
    S, D = x0.shape
    cos, sin = tabs
    tmm = _tile(S, 1024, 8)
    same = lambda accs, ex: accs

    def W(name):
        return arrived[(layer, name)]

    def hosted(key, fn):
        if key not in comm:
            return fn(None)
        ex, keys = comm[key]()
        res, outs = fn(ex)
        arrived.update(zip(keys, outs))
        return res

    h1 = _norm_fwd("norm_fwd", x0, sm["norm_mix"])
    z = hosted("mm_in", lambda ex: _mm_nn_cols("mm_in", h1, [W("w_in")], [bf16], same, tr=True, comm=ex))[0]
    IN = z.shape[1]

    qkv = _qkv_groups(z)
    o3, lse3 = hosted("attn_fwd", lambda ex: _attn_fwd(qkv, cos, sin, comm=ex))
    attn, lse = _attn_combine(o3, lse3)
    ya = _mm_nn_cols("mm_br_attn", attn, [W("w_br_attn")], [bf16], same, tm_pref=1024)[0]
    sgw = sm["sg_w"].astype(bf16)
    bb = jnp.broadcast_to(sm["sg_b"][:, :, None], (SG_GROUPS, SG_CHUNK, 128))
    sg = _sg_fwd(z, sgw, bb, sm["sg_ln_g"], sm["sg_ln_b"])
    wide = _tile(D, 512)

    def merge_ep(accs, ex):
        yb_t = accs[0]
        ga, gb = _sigmoid(ex[0].astype(f32)), _sigmoid(ex[1].astype(f32))
        return [yb_t, ga * ex[2].astype(f32) + gb * yb_t]

    yb, merged = _mm_nn_cols("mm_br_sg", sg, [W("w_br_sg")], [bf16, bf16], merge_ep, tm_pref=1024, width_pref=wide,
                             extras=[(z, (QKV_W + 2 * SG_W) // wide), (z, (QKV_W + 2 * SG_W + D) // wide), (ya, 0)])
    tn = _tile(D, 1024)
    x1 = hosted("mm_out", lambda ex: _matmul(
        "mm_out", [(merged, W("w_out").reshape(D, D), "nn", 0)], S, D, tmm, tn, 1,
        [((S, D), f32, (tmm, tn), lambda i, j: (i, j))], lambda accs, ex_tiles: [ex_tiles[0] + accs[0]],
        extras=[(x0, (tmm, tn), lambda i, j: (i, j))], chunk=512, comm=ex))[0]
    h2 = _norm_fwd("norm_fwd", x1, sm["norm_ffn"])

    def ffn_ep(accs, ex):
        a, b = accs
        sg = _sigmoid(a)
        silu = a * sg
        return [b * sg * (1.0 + a * (1.0 - sg)), silu, silu * b]

    dfa, dfb, f = hosted("mm_ffn_in", lambda ex: _mm_nn_cols("mm_ffn_in", h2, [W("w_ff_gate"), W("w_ff_up")], [bf16] * 3, ffn_ep,
                                                         tr=True, comm=ex))
    w_down = W("w_ff_down").reshape(-1, D)
    F = w_down.shape[0]
    thin = _tile(D, 512)
    x2 = hosted("mm_ffn_out", lambda ex: _matmul(
        "mm_ffn_out", [(f, w_down, "nn", 0)], S, D, tmm, thin, 1, [((S, D), f32, (tmm, thin), lambda i, j: (i, j))],
        lambda accs, ex_tiles: [ex_tiles[0] + accs[0]], extras=[(x1, (tmm, thin), lambda i, j: (i, j))], comm=ex))[0]
    h3 = _norm_fwd("norm_fwd", x2, sm["norm_ple"])

    e = _mm_nn_cols("mm_ple_emb", p_i, [W("w_ple")], [bf16], same, tm_pref=1024)[0]

    def ple_ep(accs, ex):
        gp = accs[0]
        return [ex[0] + _sigmoid(gp) * ex[1].astype(f32), gp]

    x3, gp = hosted("mm_ple", lambda ex: _matmul(
        "mm_ple", [(h3, W("w_ple_gate").reshape(D, D), "nn", 0)], S, D, tmm, tn, 1,
        [((S, D), f32, (tmm, tn), lambda i, j: (i, j)), ((S, D), bf16, (tmm, tn), lambda i, j: (i, j))],
        ple_ep, extras=[(x2, (tmm, tn), lambda i, j: (i, j)), (e, (tmm, tn), lambda i, j: (i, j))], chunk=512, comm=ex))
    saved = dict(x0=x0, h1=h1, z=z, qkv=qkv, attn=attn, lse=lse, ya=ya, yb=yb, sg=sg, merged=merged, x1=x1,
                 h2=h2, dfa=dfa, dfb=dfb, f=f, x2=x2, h3=h3, gp=gp, e=e, sgw=sgw, bb=bb, IN=IN)
    return x3, saved


def _layer_bwd(dx3, p_i, W, sm, tabs, sv, c_idx):
    S, D = dx3.shape
    w_out, w_down, w_pg = W["w_out"].reshape(D, D), W["w_ff_down"].reshape(-1, D), W["w_ple_gate"].reshape(D, D)
    F = w_down.shape[0]
    cos, sin = tabs
    tmm = _tile(S, 1024, 8)
    tn = _tile(D, 512)
    reduced = {}

    def to_sibling(grads):
        return _sibling_exchange(list(grads.values()))

    def chip_sums(grads, recv):
        return {n: _rs_chip_sum("rs_chip_sum_" + n, grads[n], r, c_idx) for n, r in zip(grads, recv)}

    def to_owners(p4, names=None):
        return _chips_exchange([p4[n] for n in (names or p4)])

    def carry(p4, outs, names=None):
        reduced.update({n: (p4[n], r) for n, r in zip(names or p4, outs)})

    def blocks(full):
        return full.reshape(N_DEV, full.shape[0] // N_DEV, full.shape[1])

    dgp, de = _ple_bwd_ew(dx3, sv["gp"], sv["e"])
    d_w_ple = _mm_tn_cols("mm_dw_ple", p_i, [de], D // N_DEV)[0]
    d_w_pg = blocks(_mm_simple("mm_dw_dd", sv["h3"], dgp, "tn", bf16, 1024, 1024, 2048))
    dh3 = _mm_simple("mm_dh_dd", dgp, w_pg, "nt", bf16, 1024, 1024, 2048)
    dx2, dx2b, dg_ple = _norm_bwd("norm_bwd", dh3, sv["x2"], sm["norm_ple"], dx3)
    tf = _tile(F, 1408)

    def ffn_bwd_ep(accs, ex):
        df = accs[0]
        return [df * ex[0].astype(f32), df * ex[1].astype(f32)]

    th = _tile(S, 512, 8)
    da, db = _matmul("mm_dffn", [(dx2b, w_down, "nt", 0)], S, F, th, tf, 1,
                     [((S, F), bf16, (th, tf), lambda i, j: (i, j))] * 2, ffn_bwd_ep,
                     extras=[(sv["dfa"], (th, tf), lambda i, j: (i, j)), (sv["dfb"], (th, tf), lambda i, j: (i, j))], chunk=512)
    d_w_down = blocks(_mm_simple("mm_dw_fd", sv["f"], dx2b, "tn", bf16, 1408, 1024, 2048))
    g_a = dict(w_ff_down=d_w_down, w_ple_gate=d_w_pg, w_ple=d_w_ple)
    (d_w_gate, d_w_up), recv = _mm_tn_cols("mm_dw_df", sv["h2"], [da, db], F // N_DEV, tr=True, comm=to_sibling(g_a))
    p4_a = chip_sums(g_a, recv)
    g_b = dict(w_ff_gate=d_w_gate, w_ff_up=d_w_up)
    dh2, outs = _mm_nt_cols("mm_dh_ffn", [(da, W["w_ff_gate"]), (db, W["w_ff_up"])], bf16, tr=True,
                            comm=_both(to_owners(p4_a), to_sibling(g_b)))
    carry(p4_a, outs[:len(p4_a)])
    p4_b = chip_sums(g_b, outs[len(p4_a):])
    dx1, dx1b, dg_ffn = _norm_bwd("norm_bwd", dh2, sv["x1"], sm["norm_ffn"], dx2)
    z = sv["z"]
    o_a, o_b = (QKV_W + 2 * SG_W) // tn, (QKV_W + 2 * SG_W + D) // tn

    def merge_bwd_ep(accs, ex):
        dm = accs[0]
        ga, gb = _sigmoid(ex[0].astype(f32)), _sigmoid(ex[1].astype(f32))
        ya, yb = ex[2].astype(f32), ex[3].astype(f32)
        return [dm * ya * ga * (1.0 - ga), dm * yb * gb * (1.0 - gb), dm * ga, dm * gb]

    dga, dgb, dya, dyb = _matmul(
        "mm_dmerge", [(dx1b, w_out, "nt", 0)], S, D, tmm, tn, 1,
        [((S, D), bf16, (tmm, tn), lambda i, j: (i, j))] * 4, merge_bwd_ep,
        extras=[(z, (tmm, tn), lambda i, j: (i, o_a + j)), (z, (tmm, tn), lambda i, j: (i, o_b + j)),
                (sv["ya"], (tmm, tn), lambda i, j: (i, j)), (sv["yb"], (tmm, tn), lambda i, j: (i, j))], chunk=256)
    d_w_out = blocks(_mm_simple("mm_dw_dd", sv["merged"], dx1b, "tn", bf16, 1024, 1024, 2048))
    dsg = _mm_nt_cols("mm_dsg", [(dyb, W["w_br_sg"])], bf16)
    d_w_bsg = _mm_tn_cols("mm_dw_bsg", sv["sg"], [dyb], D // N_DEV)[0]
    dattn = _mm_nt_cols("mm_dattn", [(dya, W["w_br_attn"])], bf16)
    d_w_battn = _mm_tn_cols("mm_dw_battn", sv["attn"], [dya], D // N_DEV)[0]
    sgwt = jnp.swapaxes(sm["sg_w"], 1, 2).astype(bf16)
    du, dv_sg, d_sgw, d_sgb, d_lg, d_lb = _sg_bwd(dsg, z, sv["sgw"], sgwt, sv["bb"], sm["sg_ln_g"], sm["sg_ln_b"])
    b3 = _bwd_groups(sv["attn"], dattn, sv["lse"])
    g_c = dict(w_out=d_w_out, w_br_sg=d_w_bsg, w_br_attn=d_w_battn)
    dqg, outs = _attn_bwd_dq(sv["qkv"], cos, sin, b3, comm=_both(to_owners(p4_b, ["w_ff_gate"]), to_sibling(g_c)))
    carry(p4_b, outs[:1], ["w_ff_gate"])
    p4_c = chip_sums(g_c, outs[1:])
    (dkg, dvg), outs = _attn_bwd_dkv(sv["qkv"], cos, sin, b3, comm=to_owners(p4_b, ["w_ff_up"]))
    carry(p4_b, outs, ["w_ff_up"])
    dz = jnp.concatenate([_to_natural(dqg), _to_natural(dkg), _to_natural(dvg), du, dv_sg, dga, dgb], axis=1)
    (d_w_in,), outs = _mm_tn_cols("mm_dw_in", sv["h1"], [dz], sv["IN"] // N_DEV, tr=True, comm=to_owners(p4_c))
    carry(p4_c, outs)
    g_d = dict(w_in=d_w_in)
    p4_d = chip_sums(g_d, _run_exchange("rs_sibling_w_in", to_sibling(g_d)))
    dh1, outs = _mm_nt_cols("mm_dh_in", [(dz, W["w_in"])], bf16, tr=True, comm=to_owners(p4_d))
    carry(p4_d, outs)
    dx0, _, dg_mix = _norm_bwd("norm_bwd", dh1, sv["x0"], sm["norm_mix"], dx1)
    small = dict(sg_w=d_sgw, sg_b=d_sgb[:, :, 0], sg_ln_g=d_lg[0], sg_ln_b=d_lb[0], norm_mix=dg_mix[0], norm_ffn=dg_ffn[0],
                 norm_ple=dg_ple[0])
    return dx0, reduced, small


_SMALL = ("sg_w", "sg_b", "sg_ln_g", "sg_ln_b", "norm_mix", "norm_ffn", "norm_ple", "norm_final")


SMALL_ROWS = 256


def _pack_small(parts, tail):
    rows = [parts[n].astype(f32).reshape(-1, LANES) for n in _SMALL] + [tail]
    n = sum(r.shape[0] for r in rows)
    return jnp.concatenate(rows + [jnp.zeros((-n % SMALL_ROWS, LANES), f32)], axis=0)


def kernel(x, p, w_in, w_br_attn, w_br_sg, w_out, sg_w, sg_b, sg_ln_g, sg_ln_b, norm_mix, norm_ffn, norm_ple, norm_final, w_ff_gate, w_ff_up, w_ff_down, w_ple_gate, w_ple, loss_target, m_w_in, m_w_br_attn, m_w_br_sg, m_w_out, m_sg_w, m_sg_b, m_sg_ln_g, m_sg_ln_b, m_norm_mix, m_norm_ffn, m_norm_ple, m_norm_final, m_w_ff_gate, m_w_ff_up, m_w_ff_down, m_w_ple_gate, m_w_ple, v_w_in, v_w_br_attn, v_w_br_sg, v_w_out, v_sg_w, v_sg_b, v_sg_ln_g, v_sg_ln_b, v_norm_mix, v_norm_ffn, v_norm_ple, v_norm_final, v_w_ff_gate, v_w_ff_up, v_w_ff_down, v_w_ple_gate, v_w_ple):
    wts = dict(w_in=w_in, w_br_attn=w_br_attn, w_br_sg=w_br_sg, w_out=w_out, w_ff_gate=w_ff_gate, w_ff_up=w_ff_up,
               w_ff_down=w_ff_down, w_ple_gate=w_ple_gate, w_ple=w_ple)
    mom_m = dict(w_in=m_w_in, w_br_attn=m_w_br_attn, w_br_sg=m_w_br_sg, w_out=m_w_out, w_ff_gate=m_w_ff_gate,
                 w_ff_up=m_w_ff_up, w_ff_down=m_w_ff_down, w_ple_gate=m_w_ple_gate, w_ple=m_w_ple)
    mom_v = dict(w_in=v_w_in, w_br_attn=v_w_br_attn, w_br_sg=v_w_br_sg, w_out=v_w_out, w_ff_gate=v_w_ff_gate,
                 w_ff_up=v_w_ff_up, w_ff_down=v_w_ff_down, w_ple_gate=v_w_ple_gate, w_ple=v_w_ple)
    small_w = dict(sg_w=sg_w, sg_b=sg_b, sg_ln_g=sg_ln_g, sg_ln_b=sg_ln_b, norm_mix=norm_mix, norm_ffn=norm_ffn,
                   norm_ple=norm_ple, norm_final=norm_final)
    small_m = dict(sg_w=m_sg_w, sg_b=m_sg_b, sg_ln_g=m_sg_ln_g, sg_ln_b=m_sg_ln_b, norm_mix=m_norm_mix, norm_ffn=m_norm_ffn,
                   norm_ple=m_norm_ple, norm_final=m_norm_final)
    small_v = dict(sg_w=v_sg_w, sg_b=v_sg_b, sg_ln_g=v_sg_ln_g, sg_ln_b=v_sg_ln_b, norm_mix=v_norm_mix, norm_ffn=v_norm_ffn,
                   norm_ple=v_norm_ple, norm_final=v_norm_final)
    depth = w_in.shape[0]
    S = x.shape[1]
    names = [n for n, _ in _BIG]
    c_idx = lax.axis_index("c").astype(jnp.int32).reshape(1)
    q_idx = (2 * lax.axis_index("x") + lax.axis_index("y")).astype(jnp.int32).reshape(1)
    tabs = _rope_tables(S)

    def turned(n, t):
        return jnp.swapaxes(t, -1, -2) if n in _TURNED else t

    arrived = {}

    def gather(keys):
        shards, parts, into = [], [], []
        for key in keys:
            s = turned(key[1], wts[key[1]][key[0]]).astype(bf16)
            shards.append(s)
            half = s.shape[0] // 2
            parts.append((key[2] * half, half) if len(key) == 3 else None)
            into.append(arrived.get(key[:2]) if len(key) == 3 else None)
        return _gather_exchange(shards, parts, into), [key[:2] for key in keys]

    ex, keys = gather([(0, "w_in")])
    arrived.update(zip(keys, _run_exchange("ag_w_in", ex)))

    xs = x[0]
    saved = []
    for i in range(depth):
        sm = {n: small_w[n][i] for n in _SMALL if n != "norm_final"}
        comm = {carrier: functools.partial(gather, keys) for carrier, keys in _gather_plan(i, depth).items()}
        xs, sv = _layer_fwd(xs, p[i, 0], i, arrived, sm, tabs, comm)
        saved.append(sv)
    dx, dg_final, loss_part = _loss_head(xs, norm_final, loss_target[0])

    reduced = [None] * depth
    small_parts = [None] * depth
    for i in reversed(range(depth)):
        sm = {n: small_w[n][i] for n in _SMALL if n != "norm_final"}
        dx, reduced[i], small_parts[i] = _layer_bwd(dx, p[i, 0], {n: arrived[(i, n)] for n in names}, sm, tabs, saved[i], c_idx)
    grad_x = dx[None]

    parts = {n: jnp.stack([small_parts[i][n] for i in range(depth)]) for n in _SMALL if n != "norm_final"}
    parts["norm_final"] = dg_final[0]
    gathered = _all_gather("ag_small", _pack_small(parts, loss_part), in_vmem=True)
    g_s, d_s, nm_s, nv_s = _small_sum_adamw(gathered, _pack_small(small_w, jnp.zeros((8, LANES), f32)),
                                            _pack_small(small_m, jnp.zeros((8, LANES), f32)),
                                            _pack_small(small_v, jnp.ones((8, LANES), f32)))
    loss = g_s[sum(small_w[n].size for n in _SMALL) // LANES, 0]

    def unpack_small(flat):
        out, off = {}, 0
        for n in _SMALL:
            k = small_w[n].size // LANES
            out[n] = flat[off:off + k].reshape(small_w[n].shape)
            off += k
        return out

    sm_g, sm_d, sm_nm, sm_nv = unpack_small(g_s), unpack_small(d_s), unpack_small(nm_s), unpack_small(nv_s)

    big_g, big_d, big_nm, big_nv = {}, {}, {}, {}
    for k, n in enumerate(names):
        outs = None
        for i in range(depth):
            p4, recv2 = reduced[i][n]
            outs = _adamw_layer(f"adamw_{n}_{i}", i, turned(n, wts[n]), turned(n, mom_m[n]), turned(n, mom_v[n]), p4, recv2,
                                q_idx, outs)
        big_g[n], big_d[n], big_nm[n], big_nv[n] = [turned(n, o) for o in outs]

    order = ["w_in", "w_br_attn", "w_br_sg", "w_out", "sg_w", "sg_b", "sg_ln_g", "sg_ln_b", "norm_mix", "norm_ffn", "norm_ple",
             "norm_final", "w_ff_gate", "w_ff_up", "w_ff_down", "w_ple_gate", "w_ple"]

    def pick(big, small):
        return [big[n] if n in big else small[n] for n in order]

    return (loss, grad_x, *pick(big_g, sm_g), *pick(big_d, sm_d), *pick(big_nm, sm_nm), *pick(big_nv, sm_nv))
```

```python
import functools
import math

import jax
import jax.numpy as jnp
from jax import lax
from jax.experimental import pallas as pl
from jax.experimental.pallas import tpu as pltpu

f32 = jnp.float32
bf16 = jnp.bfloat16

HEAD_DIM = 128
N_GROUPS = 3
HEADS = 4
DILATIONS = (1, 4, 16)
RADIUS = 64
BLK = 128
QKV_W = 3 * N_GROUPS * HEADS * HEAD_DIM
ATTN_W = HEADS * HEAD_DIM
SG_CHUNK = 128
SG_GROUPS = 8
SG_W = SG_GROUPS * 128
ROPE_THETA = 10000.0
EPS = 1e-6
NEG = -1e30
N_DEV = 8
LANES = 128

ADAM_LR = 0.001
ADAM_B1 = 0.9
ADAM_B2 = 0.999
ADAM_EPS = 1e-08
ADAM_WD = 0.01
ADAM_STEP = 10

VMEM_LIMIT = 56 * 1024 * 1024
MESH = pl.DeviceIdType.MESH

NN = (((1,), (0,)), ((), ()))
NT = (((1,), (1,)), ((), ()))
TN = (((0,), (0,)), ((), ()))
_DN = {"nn": NN, "nt": NT, "tn": TN}


def _cp(*sem):
    return pltpu.CompilerParams(dimension_semantics=sem, vmem_limit_bytes=VMEM_LIMIT)


def _tile(n, pref, unit=128):
    if n <= pref:
        return n
    t = (pref // unit) * unit
    while t >= unit:
        if n % t == 0:
            return t
        t -= unit
    return n


_ANY = pl.BlockSpec(memory_space=pl.ANY)


def _call(body, *, name, grid, in_specs, out_specs, out_shape, operands, scratch=(), comm=None):
    in_specs, out_specs, out_shape, scratch = list(in_specs), list(out_specs), list(out_shape), list(scratch)
    if comm is None:
        res = pl.pallas_call(
            body, name=name, out_shape=out_shape, grid=grid, in_specs=in_specs, out_specs=out_specs, scratch_shapes=scratch,
            compiler_params=_cp(*(("arbitrary",) * len(grid))))(*operands)
        return res, []
    n_in, n_out, n_scr = len(in_specs), len(out_specs), len(scratch)
    c_in, c_out = len(comm.ins), len(comm.outs)

    def hosted(*refs):
        own_in, refs = refs[:n_in], refs[n_in:]
        ex_in, refs = refs[:c_in], refs[c_in:]
        own_out, refs = refs[:n_out], refs[n_out:]
        ex_out, refs = refs[:c_out], refs[c_out:]
        own_scr, sems = refs[:n_scr], refs[n_scr:]
        ids = [pl.program_id(a) for a in range(len(grid))]
        first = functools.reduce(jnp.logical_and, [i == 0 for i in ids])
        last = functools.reduce(jnp.logical_and, [i == g - 1 for i, g in zip(ids, grid)])

        @pl.when(first)
        def _():
            comm.start(ex_in, ex_out, sems)

        body(*own_in, *own_out, *own_scr)

        @pl.when(last)
        def _():
            comm.finish(ex_in, ex_out, sems)

    res = pl.pallas_call(
        hosted, name=name, out_shape=out_shape + list(comm.outs), grid=grid,
        in_specs=in_specs + [_ANY] * c_in, out_specs=out_specs + [_ANY] * c_out, scratch_shapes=scratch + list(comm.sems),
        input_output_aliases={n_in + a: n_out + b for a, b in comm.aliases.items()},
        compiler_params=pltpu.CompilerParams(dimension_semantics=("arbitrary",) * len(grid), vmem_limit_bytes=VMEM_LIMIT,
                                             has_side_effects=True),
    )(*operands, *comm.ins)
    return res[:n_out], res[n_out:]


def _sigmoid(x):
    return 1.0 / (1.0 + jnp.exp(-x))


_GC = math.sqrt(2.0 / math.pi)
_GA = 0.044715


def _gelu(x):
    return 0.5 * x * (1.0 + jnp.tanh(_GC * (x + _GA * x * x * x)))


def _gelu_grad(x):
    t = jnp.tanh(_GC * (x + _GA * x * x * x))
    return 0.5 * (1.0 + t) + 0.5 * x * (1.0 - t * t) * _GC * (1.0 + 3.0 * _GA * x * x)


def _matmul(name, prods, M, N, tm, tn, nk, outs, epilogue, extras=(), n_acc=1, chunk=None, comm=None):
    in_specs, operands, metas = [], [], []
    for a, b, mode, acc in prods:
        if mode == "tn":
            tk = a.shape[0] // nk
            in_specs += [pl.BlockSpec((tk, tm), lambda i, j, k: (k, i)), pl.BlockSpec((tk, tn), lambda i, j, k: (k, j))]
        elif mode == "nt":
            tk = a.shape[1] // nk
            in_specs += [pl.BlockSpec((tm, tk), lambda i, j, k: (i, k)), pl.BlockSpec((tn, tk), lambda i, j, k: (j, k))]
        else:
            tk = a.shape[1] // nk
            in_specs += [pl.BlockSpec((tm, tk), lambda i, j, k: (i, k)), pl.BlockSpec((tk, tn), lambda i, j, k: (k, j))]
        operands += [a, b]
        metas.append((mode, acc))
    for arr, bshape, imap in extras:
        in_specs.append(pl.BlockSpec(bshape, functools.partial(lambda i, j, k, f: f(i, j), f=imap)))
        operands.append(arr)
    out_specs = [pl.BlockSpec(bs, functools.partial(lambda i, j, k, f: f(i, j), f=imap)) for _, _, bs, imap in outs]
    out_shape = [jax.ShapeDtypeStruct(s, d) for s, d, _, _ in outs]
    n_prod, n_ext, n_out = len(prods), len(extras), len(outs)

    def body(*refs):
        in_refs = refs[: 2 * n_prod]
        ex_refs = refs[2 * n_prod : 2 * n_prod + n_ext]
        out_refs = refs[2 * n_prod + n_ext : 2 * n_prod + n_ext + n_out]
        acc_refs = refs[2 * n_prod + n_ext + n_out :]

        def partials():
            res = [None] * n_acc
            for idx, (mode, acc) in enumerate(metas):
                a = in_refs[2 * idx][...].astype(bf16)
                b = in_refs[2 * idx + 1][...].astype(bf16)
                d = lax.dot_general(a, b, _DN[mode], preferred_element_type=f32)
                res[acc] = d if res[acc] is None else res[acc] + d
            return res

        def finish(accs):
            vals = epilogue(accs, [r[...] for r in ex_refs])
            for r, v in zip(out_refs, vals):
                r[...] = v.astype(r.dtype)

        if nk == 1:
            step = chunk or tn
            for c0 in range(0, tn, step):
                c1 = min(c0 + step, tn)
                res = [None] * n_acc
                for idx, (mode, acc) in enumerate(metas):
                    a = in_refs[2 * idx][...].astype(bf16)
                    b_ref = in_refs[2 * idx + 1]
                    b = (b_ref[c0:c1, :] if mode == "nt" else b_ref[:, c0:c1]).astype(bf16)
                    d = lax.dot_general(a, b, _DN[mode], preferred_element_type=f32)
                    res[acc] = d if res[acc] is None else res[acc] + d
                for r, v in zip(out_refs, epilogue(res, [r[:, c0:c1] for r in ex_refs])):
                    r[:, c0:c1] = v.astype(r.dtype)
        else:
            k = pl.program_id(2)
            parts = partials()

            @pl.when(k == 0)
            def _():
                for r, d in zip(acc_refs, parts):
                    r[...] = d

            @pl.when(k > 0)
            def _():
                for r, d in zip(acc_refs, parts):
                    r[...] += d

            @pl.when(k == nk - 1)
            def _():
                finish([r[...] for r in acc_refs])

    scratch = [pltpu.VMEM((tm, tn), f32) for _ in range(n_acc)] if nk > 1 else []
    res, ex = _call(body, name=name, grid=(M // tm, N // tn, nk), in_specs=in_specs, out_specs=out_specs,
                    out_shape=out_shape, operands=operands, scratch=scratch, comm=comm)
    return res if comm is None else (res, ex)


def _ident(accs, ex):
    return [accs[0]]


def _mm_simple(name, a, b, mode, out_dtype, tm_pref=1024, tn_pref=1024, tk_pref=1024, comm=None):
    if mode == "tn":
        K, M = a.shape
        N = b.shape[1]
    elif mode == "nt":
        M, K = a.shape
        N = b.shape[0]
    else:
        M, K = a.shape
        N = b.shape[1]
    tm, tn, tk = _tile(M, tm_pref), _tile(N, tn_pref), _tile(K, tk_pref)
    res = _matmul(name, [(a, b, mode, 0)], M, N, tm, tn, K // tk,
                  [((M, N), out_dtype, (tm, tn), lambda i, j: (i, j))], _ident, comm=comm)
    return res[0] if comm is None else (res[0][0], res[1])


def _group(c, width_pref=1024):
    g = LANES // math.gcd(c, LANES)
    while g < N_DEV and 2 * g * c <= width_pref:
        g *= 2
    return g


def _join(parts):
    return parts[0] if len(parts) == 1 else jnp.concatenate(parts, axis=1)


def _mm_nn_cols(name, a, gs_list, outs_dtypes, epilogue, extras=(), tm_pref=512, width_pref=1024, tr=False, comm=None):
    M, K = a.shape
    c = gs_list[0].shape[1 if tr else 2]
    g = _group(c, width_pref)
    W = g * c
    tm = _tile(M, tm_pref, 8)
    n_g, n_ex, n_out = len(gs_list), len(extras), len(outs_dtypes)
    n_cols, n_steps = N_DEV * c, N_DEV // g
    if tr:
        g, c = 1, W
    blk = (g, c, K) if tr else (g, K, c)

    def body(*refs):
        a_ref = refs[0]
        g_refs = refs[1:1 + n_g]
        ex_refs = refs[1 + n_g:1 + n_g + n_ex]
        out_refs = refs[1 + n_g + n_ex:]
        av = a_ref[...].astype(bf16)
        cols = [epilogue([lax.dot_general(av, gr[s], NT if tr else NN, preferred_element_type=f32) for gr in g_refs],
                         [r[:, s * c:(s + 1) * c] for r in ex_refs]) for s in range(g)]
        for n, r in enumerate(out_refs):
            r[...] = _join([cols[s][n].astype(r.dtype) for s in range(g)])

    tile = pl.BlockSpec((tm, W), lambda j, i: (i, j))
    res, ex = _call(
        body, name=name, out_shape=[jax.ShapeDtypeStruct((M, n_cols), d) for d in outs_dtypes],
        grid=(n_steps, M // tm),
        in_specs=[pl.BlockSpec((tm, K), lambda j, i: (i, 0))] + [pl.BlockSpec((None,) + blk, lambda j, i: (j, 0, 0, 0))] * n_g
        + [pl.BlockSpec((tm, W), functools.partial(lambda j, i, off: (i, off + j), off=off)) for _, off in extras],
        out_specs=[tile] * n_out,
        operands=[a, *[gm.reshape((n_steps,) + blk) for gm in gs_list], *[arr for arr, _ in extras]], comm=comm)
    return res if comm is None else (res, ex)


def _mm_nt_cols(name, pairs, out_dtype, tm_pref=1024, tn_pref=1024, width_pref=1024, tr=False, comm=None):
    M = pairs[0][0].shape[0]
    c, Kw = pairs[0][1].shape[1:][::1 if tr else -1]
    g = _group(c, width_pref)
    W = g * c
    tm, tn = _tile(M, tm_pref, 8), _tile(Kw, tn_pref)
    nk = N_DEV // g
    n_p = len(pairs)
    if tr:
        g, c = 1, W

    def body(*refs):
        o_ref, acc = refs[2 * n_p], refs[2 * n_p + 1]
        k = pl.program_id(2)
        tot = None
        for n in range(n_p):
            d_ref, g_ref = refs[2 * n], refs[2 * n + 1]
            for s in range(g):
                part = lax.dot_general(d_ref[:, s * c:(s + 1) * c], g_ref[s], NN if tr else NT, preferred_element_type=f32)
                tot = part if tot is None else tot + part

        @pl.when(k == 0)
        def _():
            acc[...] = tot

        @pl.when(k > 0)
        def _():
            acc[...] += tot

        @pl.when(k == nk - 1)
        def _():
            o_ref[...] = acc[...].astype(o_ref.dtype)

    in_specs, operands = [], []
    for d, gm in pairs:
        if tr:
            wspec, wview = pl.BlockSpec((None, g, c, tn), lambda i, j, k: (k, 0, 0, j)), gm.reshape(nk, g, c, Kw)
        else:
            wspec, wview = pl.BlockSpec((None, g, tn, c), lambda i, j, k: (k, 0, j, 0)), gm.reshape(nk, g, Kw, c)
        in_specs += [pl.BlockSpec((tm, W), lambda i, j, k: (i, k)), wspec]
        operands += [d, wview]
    res, ex = _call(
        body, name=name, out_shape=[jax.ShapeDtypeStruct((M, Kw), out_dtype)], grid=(M // tm, Kw // tn, nk),
        in_specs=in_specs, out_specs=[pl.BlockSpec((tm, tn), lambda i, j, k: (i, j))],
        scratch=[pltpu.VMEM((tm, tn), f32)], operands=operands, comm=comm)
    return res[0] if comm is None else (res[0], ex)


def _mm_tn_cols(name, x, ds, c, tm_pref=1024, tk_pref=1024, width_pref=1024, tr=False, comm=None):
    S, Kw = x.shape
    g = _group(c, width_pref)
    W = g * c
    tm, tk = _tile(Kw, tm_pref), _tile(S, tk_pref, 16)
    nk = S // tk
    n_d = len(ds)
    n_steps, c0 = N_DEV // g, c
    blk = (g, c, tm) if tr else (g, tm, c)
    full = (n_steps, g, c, Kw) if tr else (n_steps, g, Kw, c)

    def body(*refs):
        x_ref = refs[0]
        d_refs = refs[1:1 + n_d]
        o_refs = refs[1 + n_d:1 + 2 * n_d]
        accs = refs[1 + 2 * n_d:]
        k = pl.program_id(2)

        @pl.when(k == 0)
        def _():
            for acc in accs:
                acc[...] = jnp.zeros_like(acc)

        xv = x_ref[...].astype(bf16)
        for d_ref, acc in zip(d_refs, accs):
            for s in range(g):
                ds_ = d_ref[:, s * c:(s + 1) * c]
                acc[s] += lax.dot_general(ds_, xv, TN, preferred_element_type=f32) if tr else \
                    lax.dot_general(xv, ds_, TN, preferred_element_type=f32)

        @pl.when(k == nk - 1)
        def _():
            for o_ref, acc in zip(o_refs, accs):
                o_ref[...] = acc[...].astype(o_ref.dtype)

    out_map = (lambda i, j, k: (j, 0, 0, i)) if tr else (lambda i, j, k: (j, 0, i, 0))
    outs, ex = _call(
        body, name=name, out_shape=[jax.ShapeDtypeStruct(full, bf16)] * n_d, grid=(Kw // tm, n_steps, nk),
        in_specs=[pl.BlockSpec((tk, tm), lambda i, j, k: (k, i))] + [pl.BlockSpec((tk, W), lambda i, j, k: (k, j))] * n_d,
        out_specs=[pl.BlockSpec((None,) + blk, out_map)] * n_d,
        scratch=[pltpu.VMEM(blk, f32)] * n_d, operands=[x, *ds], comm=comm)
    outs = [o.reshape((N_DEV, c0, Kw) if tr else (N_DEV, Kw, c0)) for o in outs]
    return outs if comm is None else (outs, ex)


def _norm_fwd(name, x, g):
    S, D = x.shape
    tm = _tile(S, 512, 8)

    def body(x_ref, g_ref, h_ref):
        xv = x_ref[...]
        r = lax.rsqrt(jnp.mean(xv * xv, axis=-1, keepdims=True) + EPS)
        h_ref[...] = (xv * r * g_ref[...]).astype(bf16)

    return pl.pallas_call(
        body, name=name, out_shape=jax.ShapeDtypeStruct((S, D), bf16), grid=(S // tm,),
        in_specs=[pl.BlockSpec((tm, D), lambda i: (i, 0)), pl.BlockSpec((1, D), lambda i: (0, 0))],
        out_specs=pl.BlockSpec((tm, D), lambda i: (i, 0)), compiler_params=_cp("parallel"),
    )(x, g.reshape(1, D))


def _norm_bwd(name, dh, x, g, dx_in):
    S, D = x.shape
    tm = _tile(S, 256, 8)

    def body(dh_ref, x_ref, g_ref, dxi_ref, dx_ref, dxb_ref, dg_ref):
        i = pl.program_id(0)
        xv = x_ref[...]
        r = lax.rsqrt(jnp.mean(xv * xv, axis=-1, keepdims=True) + EPS)
        xh = xv * r
        dhv = dh_ref[...].astype(f32)
        dxh = dhv * g_ref[...]
        dx = dxi_ref[...] + r * (dxh - xh * jnp.mean(dxh * xh, axis=-1, keepdims=True))
        dx_ref[...] = dx
        dxb_ref[...] = dx.astype(bf16)

        @pl.when(i == 0)
        def _():
            dg_ref[...] = jnp.zeros_like(dg_ref)

        dg_ref[...] += jnp.sum(dhv * xh, axis=0, keepdims=True)

    row = pl.BlockSpec((tm, D), lambda i: (i, 0))
    vec = pl.BlockSpec((1, D), lambda i: (0, 0))
    return pl.pallas_call(
        body, name=name,
        out_shape=[jax.ShapeDtypeStruct((S, D), f32), jax.ShapeDtypeStruct((S, D), bf16), jax.ShapeDtypeStruct((1, D), f32)],
        grid=(S // tm,), in_specs=[row, row, vec, row], out_specs=[row, row, vec], compiler_params=_cp("arbitrary"),
    )(dh, x, g.reshape(1, D), dx_in)


def _loss_head(x, g, t):
    S, D = x.shape
    tm = _tile(S, 256, 8)

    def body(x_ref, g_ref, t_ref, dx_ref, dg_ref, loss_ref):
        i = pl.program_id(0)
        xv = x_ref[...]
        r = lax.rsqrt(jnp.mean(xv * xv, axis=-1, keepdims=True) + EPS)
        xh = xv * r
        gv = g_ref[...]
        err = xh * gv - t_ref[...]
        dy = err * (1.0 / D)
        dxh = dy * gv
        dx_ref[...] = r * (dxh - xh * jnp.mean(dxh * xh, axis=-1, keepdims=True))

        @pl.when(i == 0)
        def _():
            dg_ref[...] = jnp.zeros_like(dg_ref)
            loss_ref[...] = jnp.zeros_like(loss_ref)

        dg_ref[...] += jnp.sum(dy * xh, axis=0, keepdims=True)
        row = jnp.sum(err * err, axis=-1, keepdims=True) * (0.5 / D)
        loss_ref[...] += jnp.broadcast_to(jnp.sum(row, axis=0, keepdims=True), loss_ref.shape)

    return pl.pallas_call(
        body, name="loss_head",
        out_shape=[jax.ShapeDtypeStruct((S, D), f32), jax.ShapeDtypeStruct((1, D), f32), jax.ShapeDtypeStruct((8, LANES), f32)],
        grid=(S // tm,),
        in_specs=[pl.BlockSpec((tm, D), lambda i: (i, 0)), pl.BlockSpec((1, D), lambda i: (0, 0)), pl.BlockSpec((tm, D), lambda i: (i, 0))],
        out_specs=[pl.BlockSpec((tm, D), lambda i: (i, 0)), pl.BlockSpec((1, D), lambda i: (0, 0)), pl.BlockSpec((8, LANES), lambda i: (0, 0))],
        compiler_params=_cp("arbitrary"),
    )(x, g.reshape(1, D), t)


def _perm(t, d):
    if d == 1:
        return t
    S, C = t.shape
    return t.reshape(S // d, d, C).transpose(1, 0, 2).reshape(S, C)


def _rope_tables(S):
    half = HEAD_DIM // 2
    pos = jnp.arange(S, dtype=f32)
    inv_freq = ROPE_THETA ** (-jnp.arange(0, HEAD_DIM, 2, dtype=f32) / HEAD_DIM)
    ang = pos[:, None] * inv_freq[None, :]
    c, s = jnp.cos(ang), jnp.sin(ang)
    cos2 = jnp.concatenate([c, c], axis=-1)
    sin2 = jnp.concatenate([-s, s], axis=-1)
    assert cos2.shape == (S, 2 * half)
    return (jnp.stack([_perm(cos2, d) for d in DILATIONS]), jnp.stack([_perm(sin2, d) for d in DILATIONS]))


def _rope(t, c, s):
    return t * c + pltpu.roll(t, HEAD_DIM // 2, 1) * s


def _rope_bwd(dt, c, s):
    return dt * c - pltpu.roll(dt, HEAD_DIM // 2, 1) * s


def _band_bounds(i, nblk):
    g = pl.program_id(0)
    lb = jnp.right_shift(jnp.int32(nblk), 2 * g)
    pos = lax.rem(i, lb)
    lo = jnp.where(pos == 0, BLK, 0)
    hi = jnp.where(pos == lb - 1, 2 * BLK, 3 * BLK)
    return lo, hi


_RUN = 4


def _cur_spec(width, t=None):
    if t is None:
        return pl.BlockSpec((None, _RUN * BLK, width), lambda g, i: (g, i, 0))
    return pl.BlockSpec((None, None, _RUN * BLK, width), lambda g, i: (t, g, i, 0))


def _band_specs(width, nblk, t=None):
    lo, hi = (lambda i: jnp.maximum(_RUN * i - 1, 0)), (lambda i: jnp.minimum(_RUN * i + _RUN, nblk - 1))
    if t is None:
        return [pl.BlockSpec((None, BLK, width), lambda g, i: (g, lo(i), 0)), _cur_spec(width),
                pl.BlockSpec((None, BLK, width), lambda g, i: (g, hi(i), 0))]
    return [pl.BlockSpec((None, None, BLK, width), lambda g, i: (t, g, lo(i), 0)), _cur_spec(width, t),
            pl.BlockSpec((None, None, BLK, width), lambda g, i: (t, g, hi(i), 0))]


def _band(sub, prev, run, nxt, cols=slice(None)):
    pieces = [(prev, slice(None))] + [(run, slice(n * BLK, (n + 1) * BLK)) for n in range(_RUN)] + [(nxt, slice(None))]
    return jnp.concatenate([ref[rows, cols] for ref, rows in pieces[sub:sub + 3]], axis=0)


_ROWS = 2048


def _to_scratch(scr, val):
    for h in range(HEADS):
        scr[h] = val[:, h * HEAD_DIM:(h + 1) * HEAD_DIM].astype(f32)


def _qkv_groups(z):
    S = z.shape[0]
    R = min(_ROWS, S)
    nb = S // R

    def body(x_ref, o_ref, scr):
        g, i = pl.program_id(1), pl.program_id(2)
        _to_scratch(scr, x_ref[...])
        for gi, d in enumerate(DILATIONS):
            @pl.when(g == gi)
            def _():
                n, L = R // d, S // d
                for r in range(d):
                    start = pl.multiple_of(r * L + i * n, 16)
                    for h in range(HEADS):
                        o_ref[pl.ds(start, n), h * HEAD_DIM:(h + 1) * HEAD_DIM] = scr[h, pl.ds(r, n, stride=d), :].astype(bf16)

    return pl.pallas_call(
        body, name="qkv_groups", out_shape=jax.ShapeDtypeStruct((3, N_GROUPS, S, ATTN_W), bf16), grid=(3, N_GROUPS, nb),
        in_specs=[pl.BlockSpec((R, ATTN_W), lambda t, g, i: (i, t * N_GROUPS + g))],
        out_specs=pl.BlockSpec((None, None, S, ATTN_W), lambda t, g, i: (t, g, 0, 0)),
        scratch_shapes=[pltpu.VMEM((HEADS, R, HEAD_DIM), f32)], compiler_params=_cp("arbitrary", "arbitrary", "arbitrary"),
    )(z)


def _bwd_groups(attn, dattn, lse):
    S = attn.shape[0]
    R = min(_ROWS, S)
    nb = S // R

    def body(a_ref, d_ref, l_ref, o_ref, scr):
        t, g, i = pl.program_id(0), pl.program_id(1), pl.program_id(2)

        @pl.when(t == 0)
        def _():
            _to_scratch(scr, d_ref[...])

        @pl.when(t == 1)
        def _():
            _to_scratch(scr, l_ref[...])

        @pl.when(t == 2)
        def _():
            prod = a_ref[...].astype(f32) * d_ref[...].astype(f32)
            for h in range(HEADS):
                part = jnp.sum(prod[:, h * HEAD_DIM:(h + 1) * HEAD_DIM], axis=-1, keepdims=True)
                scr[h] = jnp.broadcast_to(part, (R, HEAD_DIM))

        for gi, d in enumerate(DILATIONS):
            @pl.when(g == gi)
            def _():
                n, L = R // d, S // d
                for r in range(d):
                    start = pl.multiple_of(r * L + i * n, 8)
                    for h in range(HEADS):
                        o_ref[pl.ds(start, n), h * HEAD_DIM:(h + 1) * HEAD_DIM] = scr[h, pl.ds(r, n, stride=d), :]

    def nat(used):
        return pl.BlockSpec((R, ATTN_W), lambda t, g, i: (jnp.where(used(t), i, 0), 0))

    return pl.pallas_call(
        body, name="bwd_groups", out_shape=jax.ShapeDtypeStruct((3, N_GROUPS, S, ATTN_W), f32), grid=(3, N_GROUPS, nb),
        in_specs=[nat(lambda t: t == 2), nat(lambda t: t != 1), nat(lambda t: t == 1)],
        out_specs=pl.BlockSpec((None, None, S, ATTN_W), lambda t, g, i: (t, g, 0, 0)),
        scratch_shapes=[pltpu.VMEM((HEADS, R, HEAD_DIM), f32)], compiler_params=_cp("arbitrary", "arbitrary", "arbitrary"),
    )(attn, dattn, lse)


def _group_views(t3, R):
    S = t3.shape[1]
    views = [t3.reshape(N_GROUPS, d, S // d, ATTN_W) for d in DILATIONS]
    specs = [pl.BlockSpec((None, d, R // d, ATTN_W), functools.partial(lambda i, g: (g, 0, i, 0), g=g))
             for g, d in enumerate(DILATIONS)]
    return views, specs


def _from_groups(scr, ref, d):
    n = ref.shape[1]
    for r in range(d):
        blk = ref[r]
        for h in range(HEADS):
            scr[h, pl.ds(r, n, stride=d), :] = blk[:, h * HEAD_DIM:(h + 1) * HEAD_DIM].astype(f32)


def _to_natural(t3):
    S = t3.shape[1]
    R = min(_ROWS // 2, S)
    views, specs = _group_views(t3, R)

    def body(v0, v1, v2, o_ref, scr):
        for g, (ref, d) in enumerate(zip((v0, v1, v2), DILATIONS)):
            _from_groups(scr, ref, d)
            for h in range(HEADS):
                o_ref[:, g * ATTN_W + h * HEAD_DIM:g * ATTN_W + (h + 1) * HEAD_DIM] = scr[h].astype(bf16)

    return pl.pallas_call(
        body, name="to_natural", out_shape=jax.ShapeDtypeStruct((S, N_GROUPS * ATTN_W), bf16), grid=(S // R,),
        in_specs=specs, out_specs=pl.BlockSpec((R, N_GROUPS * ATTN_W), lambda i: (i, 0)),
        scratch_shapes=[pltpu.VMEM((HEADS, R, HEAD_DIM), f32)], compiler_params=_cp("arbitrary"),
    )(*views)


_SCALE = HEAD_DIM ** -0.5


def _attn_fwd(qkv, cos, sin, comm=None):
    _, _, S, W = qkv.shape
    nblk = S // BLK

    def body(q_ref, kp, kc, kn, vp, vc, vn, cq, sq, ckp, ckc, ckn, skp, skc, skn, o_ref, lse_ref):
        i = pl.program_id(1)
        a = lax.broadcasted_iota(jnp.int32, (BLK, 3 * BLK), 0)
        b = lax.broadcasted_iota(jnp.int32, (BLK, 3 * BLK), 1)
        for sub in range(_RUN):
            rows = slice(sub * BLK, (sub + 1) * BLK)
            lo, hi = _band_bounds(_RUN * i + sub, nblk)
            mask = (jnp.abs(b - BLK - a) <= RADIUS) & (b >= lo) & (b < hi)
            ck, sk = _band(sub, ckp, ckc, ckn), _band(sub, skp, skc, skn)
            for hh in range(HEADS):
                sl = slice(hh * HEAD_DIM, (hh + 1) * HEAD_DIM)
                qh = _rope(q_ref[rows, sl].astype(f32), cq[rows, :], sq[rows, :]).astype(bf16)
                kh = _rope(_band(sub, kp, kc, kn, sl).astype(f32), ck, sk).astype(bf16)
                vh = _band(sub, vp, vc, vn, sl)
                s = lax.dot_general(qh, kh, NT, preferred_element_type=f32) * _SCALE
                s = jnp.where(mask, s, NEG)
                m = jnp.max(s, axis=-1, keepdims=True)
                e = jnp.exp(s - m)
                den = jnp.sum(e, axis=-1, keepdims=True)
                o = lax.dot_general(e.astype(bf16), vh, NN, preferred_element_type=f32) * (1.0 / den)
                o_ref[rows, sl] = o.astype(bf16)
                lse_ref[rows, sl] = jnp.broadcast_to(m + jnp.log(den), (BLK, HEAD_DIM))

    blk = _cur_spec(W)
    tab = _cur_spec(HEAD_DIM)
    res, ex = _call(
        body, name="attn_fwd",
        out_shape=[jax.ShapeDtypeStruct((N_GROUPS, S, W), bf16), jax.ShapeDtypeStruct((N_GROUPS, S, W), f32)],
        grid=(N_GROUPS, nblk // _RUN),
        in_specs=[_cur_spec(W, 0)] + _band_specs(W, nblk, 1) + _band_specs(W, nblk, 2) + [tab, tab]
        + _band_specs(HEAD_DIM, nblk) * 2,
        out_specs=[blk, blk], operands=[qkv] * 7 + [cos, sin, cos, cos, cos, sin, sin, sin], comm=comm)
    return res if comm is None else (res, ex)


def _attn_combine(o3, lse3):
    _, S, W = o3.shape
    R = min(_ROWS // 2, S)
    o_views, specs = _group_views(o3, R)
    l_views, _ = _group_views(lse3, R)

    def body(o0, o1, o2, l0, l1, l2, attn_ref, lse_ref, so0, so1, so2, sl0, sl1, sl2):
        for ref, scr, d in zip((o0, o1, o2, l0, l1, l2), (so0, so1, so2, sl0, sl1, sl2), DILATIONS * 2):
            _from_groups(scr, ref, d)
        for h in range(HEADS):
            a0, a1, a2 = sl0[h], sl1[h], sl2[h]
            m = jnp.maximum(jnp.maximum(a0, a1), a2)
            w0, w1, w2 = jnp.exp(a0 - m), jnp.exp(a1 - m), jnp.exp(a2 - m)
            den = w0 + w1 + w2
            acc = w0 * so0[h] + w1 * so1[h] + w2 * so2[h]
            attn_ref[:, h * HEAD_DIM:(h + 1) * HEAD_DIM] = (acc * (1.0 / den)).astype(bf16)
            lse_ref[:, h * HEAD_DIM:(h + 1) * HEAD_DIM] = m + jnp.log(den)

    nat = pl.BlockSpec((R, W), lambda i: (i, 0))
    return pl.pallas_call(
        body, name="attn_combine", out_shape=[jax.ShapeDtypeStruct((S, W), bf16), jax.ShapeDtypeStruct((S, W), f32)],
        grid=(S // R,), in_specs=specs * 2, out_specs=[nat, nat],
        scratch_shapes=[pltpu.VMEM((HEADS, R, HEAD_DIM), f32)] * 6, compiler_params=_cp("arbitrary"),
    )(*o_views, *l_views)


def _attn_bwd_dq(qkv, cos, sin, b3, comm=None):
    _, _, S, W = qkv.shape
    nblk = S // BLK

    def body(q_ref, kp, kc, kn, vp, vc, vn, cq, sq, ckp, ckc, ckn, skp, skc, skn, da_ref, l_ref, dl_ref, dq_ref):
        i = pl.program_id(1)
        a = lax.broadcasted_iota(jnp.int32, (BLK, 3 * BLK), 0)
        b = lax.broadcasted_iota(jnp.int32, (BLK, 3 * BLK), 1)
        for sub in range(_RUN):
            rows = slice(sub * BLK, (sub + 1) * BLK)
            lo, hi = _band_bounds(_RUN * i + sub, nblk)
            mask = (jnp.abs(b - BLK - a) <= RADIUS) & (b >= lo) & (b < hi)
            ck, sk = _band(sub, ckp, ckc, ckn), _band(sub, skp, skc, skn)
            for hh in range(HEADS):
                sl = slice(hh * HEAD_DIM, (hh + 1) * HEAD_DIM)
                qh = _rope(q_ref[rows, sl].astype(f32), cq[rows, :], sq[rows, :]).astype(bf16)
                kh = _rope(_band(sub, kp, kc, kn, sl).astype(f32), ck, sk).astype(bf16)
                vh = _band(sub, vp, vc, vn, sl)
                s = lax.dot_general(qh, kh, NT, preferred_element_type=f32) * _SCALE
                lh = l_ref[rows, sl]
                l3 = jnp.concatenate([lh, lh, lh], axis=1)
                p = jnp.exp(jnp.where(mask, s - l3, NEG))
                dp = lax.dot_general(da_ref[rows, sl].astype(bf16), vh, NT, preferred_element_type=f32)
                dh = dl_ref[rows, sl]
                ds = p * (dp - jnp.concatenate([dh, dh, dh], axis=1))
                dqh = lax.dot_general(ds.astype(bf16), kh, NN, preferred_element_type=f32) * _SCALE
                dq_ref[rows, sl] = _rope_bwd(dqh, cq[rows, :], sq[rows, :]).astype(bf16)

    tab = _cur_spec(HEAD_DIM)
    res, ex = _call(
        body, name="attn_bwd_dq", out_shape=[jax.ShapeDtypeStruct((N_GROUPS, S, W), bf16)], grid=(N_GROUPS, nblk // _RUN),
        in_specs=[_cur_spec(W, 0)] + _band_specs(W, nblk, 1) + _band_specs(W, nblk, 2) + [tab, tab]
        + _band_specs(HEAD_DIM, nblk) * 2 + [_cur_spec(W, 0), _cur_spec(W, 1), _cur_spec(W, 2)],
        out_specs=[_cur_spec(W)], operands=[qkv] * 7 + [cos, sin, cos, cos, cos, sin, sin, sin, b3, b3, b3], comm=comm)
    return res[0] if comm is None else (res[0], ex)


def _attn_bwd_dkv(qkv, cos, sin, b3, comm=None):
    _, _, S, W = qkv.shape
    nblk = S // BLK

    def body(k_ref, v_ref, ck, sk, qp, qc, qn, cqp, cqc, cqn, sqp, sqc, sqn, dap, dac, dan, lp, lc, ln, dlp, dlc, dln,
             dk_ref, dv_ref):
        j = pl.program_id(1)
        a = lax.broadcasted_iota(jnp.int32, (3 * BLK, BLK), 0)
        b = lax.broadcasted_iota(jnp.int32, (3 * BLK, BLK), 1)
        for sub in range(_RUN):
            rows = slice(sub * BLK, (sub + 1) * BLK)
            lo, hi = _band_bounds(_RUN * j + sub, nblk)
            mask = (jnp.abs(b - (a - BLK)) <= RADIUS) & (a >= lo) & (a < hi)
            cq, sq = _band(sub, cqp, cqc, cqn), _band(sub, sqp, sqc, sqn)
            for hh in range(HEADS):
                sl = slice(hh * HEAD_DIM, (hh + 1) * HEAD_DIM)
                kh = _rope(k_ref[rows, sl].astype(f32), ck[rows, :], sk[rows, :]).astype(bf16)
                vh = v_ref[rows, sl]
                qh = _rope(_band(sub, qp, qc, qn, sl).astype(f32), cq, sq).astype(bf16)
                dah = _band(sub, dap, dac, dan, sl).astype(bf16)
                lh = _band(sub, lp, lc, ln, sl)
                dlh = _band(sub, dlp, dlc, dln, sl)
                s = lax.dot_general(qh, kh, NT, preferred_element_type=f32) * _SCALE
                p = jnp.exp(jnp.where(mask, s - lh, NEG))
                dv_ref[rows, sl] = lax.dot_general(p.astype(bf16), dah, TN, preferred_element_type=f32).astype(bf16)
                dp = lax.dot_general(dah, vh, NT, preferred_element_type=f32)
                ds = p * (dp - dlh)
                dkh = lax.dot_general(ds.astype(bf16), qh, TN, preferred_element_type=f32) * _SCALE
                dk_ref[rows, sl] = _rope_bwd(dkh, ck[rows, :], sk[rows, :]).astype(bf16)

    blk, tab, bt = _cur_spec(W), _cur_spec(HEAD_DIM), _band_specs(HEAD_DIM, nblk)
    res, ex = _call(
        body, name="attn_bwd_dkv",
        out_shape=[jax.ShapeDtypeStruct((N_GROUPS, S, W), bf16), jax.ShapeDtypeStruct((N_GROUPS, S, W), bf16)],
        grid=(N_GROUPS, nblk // _RUN),
        in_specs=[_cur_spec(W, 1), _cur_spec(W, 2), tab, tab] + _band_specs(W, nblk, 0) + bt + bt
        + _band_specs(W, nblk, 0) + _band_specs(W, nblk, 1) + _band_specs(W, nblk, 2), out_specs=[blk, blk],
        operands=[qkv, qkv, cos, sin, qkv, qkv, qkv, cos, cos, cos, sin, sin, sin] + [b3] * 9, comm=comm)
    return res if comm is None else (res, ex)


_SG_ROWS = 512


def _sg_z_specs(tm, half):
    o = QKV_W // half
    return [pl.BlockSpec((tm, half), functools.partial(lambda i, c: (i, c), c=o + n)) for n in range(4)]


def _sg_norm(v, lg, lb):
    gv = _gelu(v)
    mu = jnp.mean(gv, axis=-1, keepdims=True)
    xc = gv - mu
    rstd = lax.rsqrt(jnp.mean(xc * xc, axis=-1, keepdims=True) + EPS)
    xh = xc * rstd
    return xh, rstd, xh * lg + lb


def _sg_fwd(z, w, bb, lg, lb):
    S = z.shape[0]
    tm = _tile(S, _SG_ROWS, SG_CHUNK)
    half = SG_W // 2

    def body(u0, u1, v0, v1, w_ref, bb_ref, lg_ref, lb_ref, o_ref):
        u = jnp.concatenate([u0[...], u1[...]], axis=1).astype(f32)
        v = jnp.concatenate([v0[...], v1[...]], axis=1).astype(f32)
        gu = _gelu(u)
        _, _, vn = _sg_norm(v, lg_ref[...], lb_ref[...])
        vnb = vn.astype(bf16)
        for c in range(tm // SG_CHUNK):
            rs = slice(c * SG_CHUNK, (c + 1) * SG_CHUNK)
            for g in range(SG_GROUPS):
                cs = slice(g * 128, (g + 1) * 128)
                mixed = lax.dot_general(w_ref[g], vnb[rs, cs], NN, preferred_element_type=f32) + bb_ref[g]
                o_ref[rs, cs] = (gu[rs, cs] * mixed).astype(bf16)

    full3 = pl.BlockSpec((SG_GROUPS, 128, 128), lambda i: (0, 0, 0))
    vec = pl.BlockSpec((1, SG_W), lambda i: (0, 0))
    return pl.pallas_call(
        body, name="sg_fwd", out_shape=jax.ShapeDtypeStruct((S, SG_W), bf16), grid=(S // tm,),
        in_specs=_sg_z_specs(tm, half) + [full3, full3, vec, vec],
        out_specs=pl.BlockSpec((tm, SG_W), lambda i: (i, 0)), compiler_params=_cp("parallel"),
    )(z, z, z, z, w, bb, lg.reshape(1, SG_W), lb.reshape(1, SG_W))


def _sg_bwd(dsg, z, w, wt, bb, lg, lb):
    S = z.shape[0]
    tm = _tile(S, _SG_ROWS, SG_CHUNK)
    half = SG_W // 2

    def body(d_ref, u0, u1, v0, v1, w_ref, wt_ref, bb_ref, lg_ref, lb_ref, du_ref, dv_ref, dw_ref, db_ref, dlg_ref, dlb_ref, dvn_scr):
        i = pl.program_id(0)

        @pl.when(i == 0)
        def _():
            dw_ref[...] = jnp.zeros_like(dw_ref)
            db_ref[...] = jnp.zeros_like(db_ref)
            dlg_ref[...] = jnp.zeros_like(dlg_ref)
            dlb_ref[...] = jnp.zeros_like(dlb_ref)

        u = jnp.concatenate([u0[...], u1[...]], axis=1).astype(f32)
        v = jnp.concatenate([v0[...], v1[...]], axis=1).astype(f32)
        gu = _gelu(u)
        dgu = _gelu_grad(u)
        xh, rstd, vn = _sg_norm(v, lg_ref[...], lb_ref[...])
        vnb = vn.astype(bf16)
        dsg_v = d_ref[...].astype(f32)
        for g in range(SG_GROUPS):
            cs = slice(g * 128, (g + 1) * 128)
            dw_g = jnp.zeros((128, 128), f32)
            db_g = jnp.zeros((128, 1), f32)
            for c in range(tm // SG_CHUNK):
                rs = slice(c * SG_CHUNK, (c + 1) * SG_CHUNK)
                ds = dsg_v[rs, cs]
                mixed = lax.dot_general(w_ref[g], vnb[rs, cs], NN, preferred_element_type=f32) + bb_ref[g]
                du_ref[rs, cs] = (ds * mixed * dgu[rs, cs]).astype(bf16)
                dmix = ds * gu[rs, cs]
                dmb = dmix.astype(bf16)
                dw_g = dw_g + lax.dot_general(dmb, vnb[rs, cs], NT, preferred_element_type=f32)
                db_g = db_g + jnp.sum(dmix, axis=-1, keepdims=True)
                dvn_scr[rs, cs] = lax.dot_general(wt_ref[g], dmb, NN, preferred_element_type=f32)
            dw_ref[g] += dw_g
            db_ref[g] += jnp.broadcast_to(db_g, (128, 128))
        dvn = dvn_scr[...]
        dlg_ref[...] += jnp.sum(dvn * xh, axis=0, keepdims=True)
        dlb_ref[...] += jnp.sum(dvn, axis=0, keepdims=True)
        dxh = dvn * lg_ref[...]
        dgv = rstd * (dxh - jnp.mean(dxh, axis=-1, keepdims=True) - xh * jnp.mean(dxh * xh, axis=-1, keepdims=True))
        dv_ref[...] = (dgv * _gelu_grad(v)).astype(bf16)

    full3 = pl.BlockSpec((SG_GROUPS, 128, 128), lambda i: (0, 0, 0))
    vec = pl.BlockSpec((1, SG_W), lambda i: (0, 0))
    row = pl.BlockSpec((tm, SG_W), lambda i: (i, 0))
    return pl.pallas_call(
        body, name="sg_bwd",
        out_shape=[jax.ShapeDtypeStruct((S, SG_W), bf16), jax.ShapeDtypeStruct((S, SG_W), bf16),
                   jax.ShapeDtypeStruct((SG_GROUPS, 128, 128), f32), jax.ShapeDtypeStruct((SG_GROUPS, 128, 128), f32),
                   jax.ShapeDtypeStruct((1, SG_W), f32), jax.ShapeDtypeStruct((1, SG_W), f32)],
        grid=(S // tm,),
        in_specs=[row] + _sg_z_specs(tm, half) + [full3, full3, full3, vec, vec],
        out_specs=[row, row, full3, full3, vec, vec],
        scratch_shapes=[pltpu.VMEM((tm, SG_W), f32)], compiler_params=_cp("arbitrary"),
    )(dsg, z, z, z, z, w, wt, bb, lg.reshape(1, SG_W), lb.reshape(1, SG_W))


def _ple_bwd_ew(dx, gp, e):
    S, D = dx.shape
    tm, tc = _tile(S, 512, 8), _tile(D, 1024)

    def body(dx_ref, gp_ref, e_ref, dgp_ref, de_ref):
        dxv = dx_ref[...]
        sg = _sigmoid(gp_ref[...].astype(f32))
        dgp_ref[...] = (dxv * e_ref[...].astype(f32) * sg * (1.0 - sg)).astype(bf16)
        de_ref[...] = (dxv * sg).astype(bf16)

    blk = pl.BlockSpec((tm, tc), lambda i, j: (i, j))
    return pl.pallas_call(
        body, name="ple_bwd_ew", out_shape=[jax.ShapeDtypeStruct((S, D), bf16)] * 2, grid=(S // tm, D // tc),
        in_specs=[blk, blk, blk], out_specs=[blk, blk], compiler_params=_cp("parallel", "parallel"),
    )(dx, gp, e)


def _adam_math(w, g, m, v):
    m = ADAM_B1 * m + (1.0 - ADAM_B1) * g
    v = ADAM_B2 * v + (1.0 - ADAM_B2) * (g * g)
    m_hat = m / (1.0 - ADAM_B1 ** ADAM_STEP)
    v_hat = v / (1.0 - ADAM_B2 ** ADAM_STEP)
    delta = -ADAM_LR * (m_hat / (jnp.sqrt(v_hat) + ADAM_EPS) + ADAM_WD * w)
    return delta, m, v


def _small_sum_adamw(gathered, w, m, v):
    _, R, _ = gathered.shape
    tr = _tile(R, 1024, SMALL_ROWS)

    def body(p_ref, w_ref, m_ref, v_ref, g_ref, d_ref, nm_ref, nv_ref):
        g = p_ref[0]
        for n in range(1, N_DEV):
            g = g + p_ref[n]
        d, nm, nv = _adam_math(w_ref[...], g, m_ref[...], v_ref[...])
        g_ref[...] = g
        d_ref[...] = d
        nm_ref[...] = nm
        nv_ref[...] = nv

    blk = pl.BlockSpec((tr, LANES), lambda i: (i, 0))
    return pl.pallas_call(
        body, name="small_sum_adamw", out_shape=[jax.ShapeDtypeStruct((R, LANES), f32)] * 4, grid=(R // tr,),
        in_specs=[pl.BlockSpec((N_DEV, tr, LANES), lambda i: (0, i, 0)), blk, blk, blk], out_specs=[blk] * 4,
        compiler_params=_cp("parallel"),
    )(gathered, w, m, v)


def _all_gather(name, shard, in_vmem=False):
    R, C = shard.shape

    def body(x_ref, out_ref, send_sems, recv_sems, local_sem):
        x, y, c = lax.axis_index("x"), lax.axis_index("y"), lax.axis_index("c")
        me, sibling = (x, y, c), (x, y, 1 - c)
        chips = [(1 - x, y), (x, 1 - y), (1 - x, 1 - y)]

        def rows(px, py, pc):
            return out_ref.at[4 * px + 2 * py + pc]

        def copy(k, block, to, src=None):
            return pltpu.make_async_remote_copy(
                src_ref=rows(*block) if src is None else src, dst_ref=rows(*block),
                send_sem=send_sems.at[k], recv_sem=recv_sems.at[k], device_id=to, device_id_type=MESH)

        mine = pltpu.make_async_copy(x_ref, rows(*me), local_sem)
        mine.start()
        first = [copy(0, me, sibling, src=x_ref)]
        first += [copy(1 + j, me, (*chip, c), src=x_ref) for j, chip in enumerate(chips)]
        for cp in first:
            cp.start()
        passed = [copy(4 + j, (*chip, c), sibling) for j, chip in enumerate(chips)]
        for j, chip in enumerate(chips):
            copy(1 + j, (*chip, c), me).wait_recv()
            passed[j].start()
        copy(0, sibling, me).wait_recv()
        for j, chip in enumerate(chips):
            copy(4 + j, (*chip, 1 - c), me).wait_recv()
        for cp in first + passed:
            cp.wait_send()
        mine.wait()

    space = pl.BlockSpec(memory_space=pltpu.VMEM) if in_vmem else _ANY
    return pl.pallas_call(
        body, name=name, out_shape=jax.ShapeDtypeStruct((N_DEV, R, C), shard.dtype),
        in_specs=[space], out_specs=space,
        scratch_shapes=[pltpu.SemaphoreType.DMA((7,)), pltpu.SemaphoreType.DMA((7,)), pltpu.SemaphoreType.DMA],
        compiler_params=pltpu.CompilerParams(has_side_effects=True, vmem_limit_bytes=VMEM_LIMIT),
    )(shard)


class _Exchange:
    def __init__(self, ins, outs, sems, start, finish, aliases=None):
        self.ins, self.outs, self.sems, self.start, self.finish = list(ins), list(outs), list(sems), start, finish
        self.aliases = dict(aliases or {})


def _run_exchange(name, ex):
    c_in, c_out = len(ex.ins), len(ex.outs)

    def body(*refs):
        ins, outs, sems = refs[:c_in], refs[c_in:c_in + c_out], refs[c_in + c_out:]
        ex.start(ins, outs, sems)
        ex.finish(ins, outs, sems)

    return pl.pallas_call(
        body, name=name, out_shape=ex.outs, in_specs=[_ANY] * c_in, out_specs=[_ANY] * c_out, scratch_shapes=ex.sems,
        input_output_aliases=ex.aliases,
        compiler_params=pltpu.CompilerParams(has_side_effects=True, vmem_limit_bytes=VMEM_LIMIT),
    )(*ex.ins)


def _both(e1, e2):
    i1, o1, s1 = len(e1.ins), len(e1.outs), len(e1.sems)

    def start(ins, outs, sems):
        e1.start(ins[:i1], outs[:o1], sems[:s1])
        e2.start(ins[i1:], outs[o1:], sems[s1:])

    def finish(ins, outs, sems):
        e1.finish(ins[:i1], outs[:o1], sems[:s1])
        e2.finish(ins[i1:], outs[o1:], sems[s1:])

    aliases = dict(e1.aliases)
    aliases.update({i1 + a: o1 + b for a, b in e2.aliases.items()})
    return _Exchange(e1.ins + e2.ins, e1.outs + e2.outs, e1.sems + e2.sems, start, finish, aliases)


def _gather_exchange(shards, parts=None, into=None):
    n = len(shards)
    parts = parts or [None] * n
    into = into or [None] * n
    given = [w for w in range(n) if into[w] is not None]

    def plan(ins, outs, sems):
        send_sems, recv_sems, local_sems = sems
        x, y, c = lax.axis_index("x"), lax.axis_index("y"), lax.axis_index("c")
        me, sibling = (x, y, c), (x, y, 1 - c)
        chips = [(1 - x, y), (x, 1 - y), (1 - x, 1 - y)]

        def cut(ref, w):
            return ref if parts[w] is None else ref.at[pl.ds(parts[w][0], parts[w][1])]

        def rows(w, px, py, pc):
            return cut(outs[w].at[4 * px + 2 * py + pc], w)

        def copy(w, k, block, to, src=None):
            return pltpu.make_async_remote_copy(
                src_ref=rows(w, *block) if src is None else src, dst_ref=rows(w, *block),
                send_sem=send_sems.at[w, k], recv_sem=recv_sems.at[w, k], device_id=to, device_id_type=MESH)

        mine = [pltpu.make_async_copy(cut(ins[w], w), rows(w, *me), local_sems.at[w]) for w in range(n)]
        first = []
        for w in range(n):
            first.append(copy(w, 0, me, sibling, src=cut(ins[w], w)))
            first += [copy(w, 1 + j, me, (*chip, c), src=cut(ins[w], w)) for j, chip in enumerate(chips)]
        return c, me, sibling, chips, copy, mine, first

    def start(ins, outs, sems):
        _, _, _, _, _, mine, first = plan(ins, outs, sems)
        for cp in mine + first:
            cp.start()

    def finish(ins, outs, sems):
        c, me, sibling, chips, copy, mine, first = plan(ins, outs, sems)
        passed = []
        for w in range(n):
            for j, chip in enumerate(chips):
                copy(w, 1 + j, (*chip, c), me).wait_recv()
                passed.append(copy(w, 4 + j, (*chip, c), sibling))
                passed[-1].start()
        for w in range(n):
            copy(w, 0, sibling, me).wait_recv()
            for j, chip in enumerate(chips):
                copy(w, 4 + j, (*chip, 1 - c), me).wait_recv()
        for cp in first + passed:
            cp.wait_send()
        for cp in mine:
            cp.wait()

    return _Exchange(
        list(shards) + [into[w] for w in given], [jax.ShapeDtypeStruct((N_DEV,) + s.shape, s.dtype) for s in shards],
        [pltpu.SemaphoreType.DMA((n, 7)), pltpu.SemaphoreType.DMA((n, 7)), pltpu.SemaphoreType.DMA((n,))], start, finish,
        {n + k: w for k, w in enumerate(given)})


def _sibling_exchange(gs):
    n = len(gs)

    def copies(ins, outs, sems):
        send_sems, recv_sems = sems
        x, y, c = lax.axis_index("x"), lax.axis_index("y"), lax.axis_index("c")
        return [pltpu.make_async_remote_copy(
            src_ref=ins[w].at[2 * q + (1 - c)], dst_ref=outs[w].at[q], send_sem=send_sems.at[w, q],
            recv_sem=recv_sems.at[w, q], device_id=(x, y, 1 - c), device_id_type=MESH) for w in range(n) for q in range(4)]

    def start(ins, outs, sems):
        for cp in copies(ins, outs, sems):
            cp.start()

    def finish(ins, outs, sems):
        cps = copies(ins, outs, sems)
        for cp in cps:
            cp.wait_recv()
        for cp in cps:
            cp.wait_send()

    return _Exchange(gs, [jax.ShapeDtypeStruct((4,) + g.shape[1:], g.dtype) for g in gs],
                     [pltpu.SemaphoreType.DMA((n, 4)), pltpu.SemaphoreType.DMA((n, 4))], start, finish)


def _rs_chip_sum(name, g8, recv, c_idx):
    _, R, C = g8.shape
    tr = _tile(R, 512, 16)
    g42 = g8.reshape(4, 2, R, C)

    def body(c_ref, a_ref, b_ref, o_ref):
        o_ref[...] = (a_ref[...].astype(f32) + b_ref[...].astype(f32)).astype(o_ref.dtype)

    return pl.pallas_call(
        body, name=name, out_shape=jax.ShapeDtypeStruct((4, R, C), g8.dtype),
        grid_spec=pltpu.PrefetchScalarGridSpec(
            num_scalar_prefetch=1, grid=(4, R // tr),
            in_specs=[pl.BlockSpec((None, None, tr, C), lambda q, r, c_ref: (q, c_ref[0], r, 0)),
                      pl.BlockSpec((None, tr, C), lambda q, r, c_ref: (q, r, 0))],
            out_specs=pl.BlockSpec((None, tr, C), lambda q, r, c_ref: (q, r, 0))),
        compiler_params=_cp("parallel", "parallel"),
    )(c_idx, g42, recv)


def _chips_exchange(p4s):
    n = len(p4s)

    def copies(ins, outs, sems):
        send_sems, recv_sems = sems
        x, y, c = lax.axis_index("x"), lax.axis_index("y"), lax.axis_index("c")
        chips = [(1 - x, y), (x, 1 - y), (1 - x, 1 - y)]
        return [pltpu.make_async_remote_copy(
            src_ref=ins[w].at[2 * cx + cy], dst_ref=outs[w].at[k], send_sem=send_sems.at[w, k],
            recv_sem=recv_sems.at[w, k], device_id=(cx, cy, c), device_id_type=MESH)
            for w in range(n) for k, (cx, cy) in enumerate(chips)]

    def start(ins, outs, sems):
        for cp in copies(ins, outs, sems):
            cp.start()

    def finish(ins, outs, sems):
        cps = copies(ins, outs, sems)
        for cp in cps:
            cp.wait_recv()
        for cp in cps:
            cp.wait_send()

    return _Exchange(p4s, [jax.ShapeDtypeStruct((3,) + p.shape[1:], p.dtype) for p in p4s],
                     [pltpu.SemaphoreType.DMA((n, 3)), pltpu.SemaphoreType.DMA((n, 3))], start, finish)


def _adamw_layer(name, layer, w, m, v, p4, recv, q_idx, prev):
    depth, R, C = w.shape
    tr = _tile(R, 256, 8)

    def body(q_ref, w_ref, m_ref, v_ref, a_ref, b_ref, *rest):
        g_ref, d_ref, nm_ref, nv_ref = rest[-4:]
        g = ((a_ref[...].astype(f32) + b_ref[0].astype(f32)) + b_ref[1].astype(f32)) + b_ref[2].astype(f32)
        d, nm, nv = _adam_math(w_ref[...], g, m_ref[...], v_ref[...])
        g_ref[...] = g
        d_ref[...] = d
        nm_ref[...] = nm
        nv_ref[...] = nv

    lay = pl.BlockSpec((None, tr, C), lambda i, q_ref: (layer, i, 0))
    n_prev = 0 if prev is None else 4
    return pl.pallas_call(
        body, name=name, out_shape=[jax.ShapeDtypeStruct((depth, R, C), f32)] * 4,
        grid_spec=pltpu.PrefetchScalarGridSpec(
            num_scalar_prefetch=1, grid=(R // tr,),
            in_specs=[lay, lay, lay, pl.BlockSpec((None, tr, C), lambda i, q_ref: (q_ref[0], i, 0)),
                      pl.BlockSpec((3, tr, C), lambda i, q_ref: (0, i, 0))] + [_ANY] * n_prev,
            out_specs=[lay] * 4),
        input_output_aliases={6 + n: n for n in range(n_prev)},
        compiler_params=_cp("parallel"),
    )(q_idx, w, m, v, p4, recv, *(prev or ()))


_BIG = (("w_in", 1), ("w_br_attn", 1), ("w_br_sg", 1), ("w_out", 0), ("w_ff_gate", 1), ("w_ff_up", 1),
        ("w_ff_down", 0), ("w_ple_gate", 0), ("w_ple", 1))


_TURNED = ("w_in", "w_ff_gate", "w_ff_up")


def _gather_plan(i, depth):
    mixer = ["w_br_attn", "w_br_sg", "w_out"]
    plan = {
        "mm_in": [(i, "w_ff_gate")] + [(i, n) for n in mixer],
        "attn_fwd": [(i, "w_ff_up")],
        "mm_ffn_in": [(i, "w_ff_down")],
    }
    last = [(i, "w_ple_gate"), (i, "w_ple")]
    if i + 1 < depth:
        plan["mm_out"] = last
        plan["mm_ffn_in"] = plan["mm_ffn_in"] + [(i + 1, "w_in", 0)]
        plan["mm_ffn_out"] = [(i + 1, "w_in", 1)]
    else:
        plan["mm_ffn_out"] = last
    return plan


def _layer_fwd(x0, p_i, layer, arrived, sm, tabs, comm):
    S, D = x0.shape
    cos, sin = tabs
    tmm = _tile(S, 1024, 8)
    same = lambda accs, ex: accs

    def W(name):
        return arrived[(layer, name)]

    def hosted(key, fn):
        if key not in comm:
            return fn(None)
        ex, keys = comm[key]()
        res, outs = fn(ex)
        arrived.update(zip(keys, outs))
        return res

    h1 = _norm_fwd("norm_fwd", x0, sm["norm_mix"])
    z = hosted("mm_in", lambda ex: _mm_nn_cols("mm_in", h1, [W("w_in")], [bf16], same, tr=True, comm=ex))[0]
    IN = z.shape[1]

    qkv = _qkv_groups(z)
    o3, lse3 = hosted("attn_fwd", lambda ex: _attn_fwd(qkv, cos, sin, comm=ex))
    attn, lse = _attn_combine(o3, lse3)
    ya = _mm_nn_cols("mm_br_attn", attn, [W("w_br_attn")], [bf16], same, tm_pref=1024)[0]
    sgw = sm["sg_w"].astype(bf16)
    bb = jnp.broadcast_to(sm["sg_b"][:, :, None], (SG_GROUPS, SG_CHUNK, 128))
    sg = _sg_fwd(z, sgw, bb, sm["sg_ln_g"], sm["sg_ln_b"])
    wide = _tile(D, 512)

    def merge_ep(accs, ex):
        yb_t = accs[0]
        ga, gb = _sigmoid(ex[0].astype(f32)), _sigmoid(ex[1].astype(f32))
        return [yb_t, ga * ex[2].astype(f32) + gb * yb_t]

    yb, merged = _mm_nn_cols("mm_br_sg", sg, [W("w_br_sg")], [bf16, bf16], merge_ep, tm_pref=1024, width_pref=wide,
                             extras=[(z, (QKV_W + 2 * SG_W) // wide), (z, (QKV_W + 2 * SG_W + D) // wide), (ya, 0)])
    tn = _tile(D, 1024)
    x1 = hosted("mm_out", lambda ex: _matmul(
        "mm_out", [(merged, W("w_out").reshape(D, D), "nn", 0)], S, D, tmm, tn, 1,
        [((S, D), f32, (tmm, tn), lambda i, j: (i, j))], lambda accs, ex_tiles: [ex_tiles[0] + accs[0]],
        extras=[(x0, (tmm, tn), lambda i, j: (i, j))], chunk=512, comm=ex))[0]
    h2 = _norm_fwd("norm_fwd", x1, sm["norm_ffn"])

    def ffn_ep(accs, ex):
        a, b = accs
        sg = _sigmoid(a)
        silu = a * sg
        return [b * sg * (1.0 + a * (1.0 - sg)), silu, silu * b]

    dfa, dfb, f = hosted("mm_ffn_in", lambda ex: _mm_nn_cols("mm_ffn_in", h2, [W("w_ff_gate"), W("w_ff_up")], [bf16] * 3, ffn_ep,
                                                         tr=True, comm=ex))
    w_down = W("w_ff_down").reshape(-1, D)
    F = w_down.shape[0]
    thin = _tile(D, 512)
    x2 = hosted("mm_ffn_out", lambda ex: _matmul(
        "mm_ffn_out", [(f, w_down, "nn", 0)], S, D, tmm, thin, 1, [((S, D), f32, (tmm, thin), lambda i, j: (i, j))],
        lambda accs, ex_tiles: [ex_tiles[0] + accs[0]], extras=[(x1, (tmm, thin), lambda i, j: (i, j))], comm=ex))[0]
    h3 = _norm_fwd("norm_fwd", x2, sm["norm_ple"])

    e = _mm_nn_cols("mm_ple_emb", p_i, [W("w_ple")], [bf16], same, tm_pref=1024)[0]

    def ple_ep(accs, ex):
        gp = accs[0]
        return [ex[0] + _sigmoid(gp) * ex[1].astype(f32), gp]

    x3, gp = hosted("mm_ple", lambda ex: _matmul(
        "mm_ple", [(h3, W("w_ple_gate").reshape(D, D), "nn", 0)], S, D, tmm, tn, 1,
        [((S, D), f32, (tmm, tn), lambda i, j: (i, j)), ((S, D), bf16, (tmm, tn), lambda i, j: (i, j))],
        ple_ep, extras=[(x2, (tmm, tn), lambda i, j: (i, j)), (e, (tmm, tn), lambda i, j: (i, j))], chunk=512, comm=ex))
    saved = dict(x0=x0, h1=h1, z=z, qkv=qkv, attn=attn, lse=lse, ya=ya, yb=yb, sg=sg, merged=merged, x1=x1,
                 h2=h2, dfa=dfa, dfb=dfb, f=f, x2=x2, h3=h3, gp=gp, e=e, sgw=sgw, bb=bb, IN=IN)
    return x3, saved


def _layer_bwd(dx3, p_i, W, sm, tabs, sv, c_idx):
    S, D = dx3.shape
    w_out, w_down, w_pg = W["w_out"].reshape(D, D), W["w_ff_down"].reshape(-1, D), W["w_ple_gate"].reshape(D, D)
    F = w_down.shape[0]
    cos, sin = tabs
    tmm = _tile(S, 1024, 8)
    tn = _tile(D, 512)
    reduced = {}

    def to_sibling(grads):
        return _sibling_exchange(list(grads.values()))

    def chip_sums(grads, recv):
        return {n: _rs_chip_sum("rs_chip_sum_" + n, grads[n], r, c_idx) for n, r in zip(grads, recv)}

    def to_owners(p4, names=None):
        return _chips_exchange([p4[n] for n in (names or p4)])

    def carry(p4, outs, names=None):
        reduced.update({n: (p4[n], r) for n, r in zip(names or p4, outs)})

    def blocks(full):
        return full.reshape(N_DEV, full.shape[0] // N_DEV, full.shape[1])

    dgp, de = _ple_bwd_ew(dx3, sv["gp"], sv["e"])
    d_w_ple = _mm_tn_cols("mm_dw_ple", p_i, [de], D // N_DEV)[0]
    d_w_pg = blocks(_mm_simple("mm_dw_dd", sv["h3"], dgp, "tn", bf16, 1024, 1024, 2048))
    dh3 = _mm_simple("mm_dh_dd", dgp, w_pg, "nt", bf16, 1024, 1024, 2048)
    dx2, dx2b, dg_ple = _norm_bwd("norm_bwd", dh3, sv["x2"], sm["norm_ple"], dx3)
    tf = _tile(F, 1408)

    def ffn_bwd_ep(accs, ex):
        df = accs[0]
        return [df * ex[0].astype(f32), df * ex[1].astype(f32)]

    th = _tile(S, 512, 8)
    da, db = _matmul("mm_dffn", [(dx2b, w_down, "nt", 0)], S, F, th, tf, 1,
                     [((S, F), bf16, (th, tf), lambda i, j: (i, j))] * 2, ffn_bwd_ep,
                     extras=[(sv["dfa"], (th, tf), lambda i, j: (i, j)), (sv["dfb"], (th, tf), lambda i, j: (i, j))], chunk=512)
    d_w_down = blocks(_mm_simple("mm_dw_fd", sv["f"], dx2b, "tn", bf16, 1408, 1024, 2048))
    g_a = dict(w_ff_down=d_w_down, w_ple_gate=d_w_pg, w_ple=d_w_ple)
    (d_w_gate, d_w_up), recv = _mm_tn_cols("mm_dw_df", sv["h2"], [da, db], F // N_DEV, tr=True, comm=to_sibling(g_a))
    p4_a = chip_sums(g_a, recv)
    g_b = dict(w_ff_gate=d_w_gate, w_ff_up=d_w_up)
    dh2, outs = _mm_nt_cols("mm_dh_ffn", [(da, W["w_ff_gate"]), (db, W["w_ff_up"])], bf16, tr=True,
                            comm=_both(to_owners(p4_a), to_sibling(g_b)))
    carry(p4_a, outs[:len(p4_a)])
    p4_b = chip_sums(g_b, outs[len(p4_a):])
    dx1, dx1b, dg_ffn = _norm_bwd("norm_bwd", dh2, sv["x1"], sm["norm_ffn"], dx2)
    z = sv["z"]
    o_a, o_b = (QKV_W + 2 * SG_W) // tn, (QKV_W + 2 * SG_W + D) // tn

    def merge_bwd_ep(accs, ex):
        dm = accs[0]
        ga, gb = _sigmoid(ex[0].astype(f32)), _sigmoid(ex[1].astype(f32))
        ya, yb = ex[2].astype(f32), ex[3].astype(f32)
        return [dm * ya * ga * (1.0 - ga), dm * yb * gb * (1.0 - gb), dm * ga, dm * gb]

    dga, dgb, dya, dyb = _matmul(
        "mm_dmerge", [(dx1b, w_out, "nt", 0)], S, D, tmm, tn, 1,
        [((S, D), bf16, (tmm, tn), lambda i, j: (i, j))] * 4, merge_bwd_ep,
        extras=[(z, (tmm, tn), lambda i, j: (i, o_a + j)), (z, (tmm, tn), lambda i, j: (i, o_b + j)),
                (sv["ya"], (tmm, tn), lambda i, j: (i, j)), (sv["yb"], (tmm, tn), lambda i, j: (i, j))], chunk=256)
    d_w_out = blocks(_mm_simple("mm_dw_dd", sv["merged"], dx1b, "tn", bf16, 1024, 1024, 2048))
    dsg = _mm_nt_cols("mm_dsg", [(dyb, W["w_br_sg"])], bf16)
    d_w_bsg = _mm_tn_cols("mm_dw_bsg", sv["sg"], [dyb], D // N_DEV)[0]
    dattn = _mm_nt_cols("mm_dattn", [(dya, W["w_br_attn"])], bf16)
    d_w_battn = _mm_tn_cols("mm_dw_battn", sv["attn"], [dya], D // N_DEV)[0]
    sgwt = jnp.swapaxes(sm["sg_w"], 1, 2).astype(bf16)
    du, dv_sg, d_sgw, d_sgb, d_lg, d_lb = _sg_bwd(dsg, z, sv["sgw"], sgwt, sv["bb"], sm["sg_ln_g"], sm["sg_ln_b"])
    b3 = _bwd_groups(sv["attn"], dattn, sv["lse"])
    g_c = dict(w_out=d_w_out, w_br_sg=d_w_bsg, w_br_attn=d_w_battn)
    dqg, outs = _attn_bwd_dq(sv["qkv"], cos, sin, b3, comm=_both(to_owners(p4_b, ["w_ff_gate"]), to_sibling(g_c)))
    carry(p4_b, outs[:1], ["w_ff_gate"])
    p4_c = chip_sums(g_c, outs[1:])
    (dkg, dvg), outs = _attn_bwd_dkv(sv["qkv"], cos, sin, b3, comm=to_owners(p4_b, ["w_ff_up"]))
    carry(p4_b, outs, ["w_ff_up"])
    dz = jnp.concatenate([_to_natural(dqg), _to_natural(dkg), _to_natural(dvg), du, dv_sg, dga, dgb], axis=1)
    (d_w_in,), outs = _mm_tn_cols("mm_dw_in", sv["h1"], [dz], sv["IN"] // N_DEV, tr=True, comm=to_owners(p4_c))
    carry(p4_c, outs)
    g_d = dict(w_in=d_w_in)
    p4_d = chip_sums(g_d, _run_exchange("rs_sibling_w_in", to_sibling(g_d)))
    dh1, outs = _mm_nt_cols("mm_dh_in", [(dz, W["w_in"])], bf16, tr=True, comm=to_owners(p4_d))
    carry(p4_d, outs)
    dx0, _, dg_mix = _norm_bwd("norm_bwd", dh1, sv["x0"], sm["norm_mix"], dx1)
    small = dict(sg_w=d_sgw, sg_b=d_sgb[:, :, 0], sg_ln_g=d_lg[0], sg_ln_b=d_lb[0], norm_mix=dg_mix[0], norm_ffn=dg_ffn[0],
                 norm_ple=dg_ple[0])
    return dx0, reduced, small


_SMALL = ("sg_w", "sg_b", "sg_ln_g", "sg_ln_b", "norm_mix", "norm_ffn", "norm_ple", "norm_final")


SMALL_ROWS = 256


def _pack_small(parts, tail):
    rows = [parts[n].astype(f32).reshape(-1, LANES) for n in _SMALL] + [tail]
    n = sum(r.shape[0] for r in rows)
    return jnp.concatenate(rows + [jnp.zeros((-n % SMALL_ROWS, LANES), f32)], axis=0)


def kernel(x, p, w_in, w_br_attn, w_br_sg, w_out, sg_w, sg_b, sg_ln_g, sg_ln_b, norm_mix, norm_ffn, norm_ple, norm_final, w_ff_gate, w_ff_up, w_ff_down, w_ple_gate, w_ple, loss_target, m_w_in, m_w_br_attn, m_w_br_sg, m_w_out, m_sg_w, m_sg_b, m_sg_ln_g, m_sg_ln_b, m_norm_mix, m_norm_ffn, m_norm_ple, m_norm_final, m_w_ff_gate, m_w_ff_up, m_w_ff_down, m_w_ple_gate, m_w_ple, v_w_in, v_w_br_attn, v_w_br_sg, v_w_out, v_sg_w, v_sg_b, v_sg_ln_g, v_sg_ln_b, v_norm_mix, v_norm_ffn, v_norm_ple, v_norm_final, v_w_ff_gate, v_w_ff_up, v_w_ff_down, v_w_ple_gate, v_w_ple):
    wts = dict(w_in=w_in, w_br_attn=w_br_attn, w_br_sg=w_br_sg, w_out=w_out, w_ff_gate=w_ff_gate, w_ff_up=w_ff_up,
               w_ff_down=w_ff_down, w_ple_gate=w_ple_gate, w_ple=w_ple)
    mom_m = dict(w_in=m_w_in, w_br_attn=m_w_br_attn, w_br_sg=m_w_br_sg, w_out=m_w_out, w_ff_gate=m_w_ff_gate,
                 w_ff_up=m_w_ff_up, w_ff_down=m_w_ff_down, w_ple_gate=m_w_ple_gate, w_ple=m_w_ple)
    mom_v = dict(w_in=v_w_in, w_br_attn=v_w_br_attn, w_br_sg=v_w_br_sg, w_out=v_w_out, w_ff_gate=v_w_ff_gate,
                 w_ff_up=v_w_ff_up, w_ff_down=v_w_ff_down, w_ple_gate=v_w_ple_gate, w_ple=v_w_ple)
    small_w = dict(sg_w=sg_w, sg_b=sg_b, sg_ln_g=sg_ln_g, sg_ln_b=sg_ln_b, norm_mix=norm_mix, norm_ffn=norm_ffn,
                   norm_ple=norm_ple, norm_final=norm_final)
    small_m = dict(sg_w=m_sg_w, sg_b=m_sg_b, sg_ln_g=m_sg_ln_g, sg_ln_b=m_sg_ln_b, norm_mix=m_norm_mix, norm_ffn=m_norm_ffn,
                   norm_ple=m_norm_ple, norm_final=m_norm_final)
    small_v = dict(sg_w=v_sg_w, sg_b=v_sg_b, sg_ln_g=v_sg_ln_g, sg_ln_b=v_sg_ln_b, norm_mix=v_norm_mix, norm_ffn=v_norm_ffn,
                   norm_ple=v_norm_ple, norm_final=v_norm_final)
    depth = w_in.shape[0]
    S = x.shape[1]
    names = [n for n, _ in _BIG]
    c_idx = lax.axis_index("c").astype(jnp.int32).reshape(1)
    q_idx = (2 * lax.axis_index("x") + lax.axis_index("y")).astype(jnp.int32).reshape(1)
    tabs = _rope_tables(S)

    def turned(n, t):
        return jnp.swapaxes(t, -1, -2) if n in _TURNED else t

    arrived = {}

    def gather(keys):
        shards, parts, into = [], [], []
        for key in keys:
            s = turned(key[1], wts[key[1]][key[0]]).astype(bf16)
            shards.append(s)
            half = s.shape[0] // 2
            parts.append((key[2] * half, half) if len(key) == 3 else None)
            into.append(arrived.get(key[:2]) if len(key) == 3 else None)
        return _gather_exchange(shards, parts, into), [key[:2] for key in keys]

    ex, keys = gather([(0, "w_in")])
    arrived.update(zip(keys, _run_exchange("ag_w_in", ex)))

    xs = x[0]
    saved = []
    for i in range(depth):
        sm = {n: small_w[n][i] for n in _SMALL if n != "norm_final"}
        comm = {carrier: functools.partial(gather, keys) for carrier, keys in _gather_plan(i, depth).items()}
        xs, sv = _layer_fwd(xs, p[i, 0], i, arrived, sm, tabs, comm)
        saved.append(sv)
    dx, dg_final, loss_part = _loss_head(xs, norm_final, loss_target[0])

    reduced = [None] * depth
    small_parts = [None] * depth
    for i in reversed(range(depth)):
        sm = {n: small_w[n][i] for n in _SMALL if n != "norm_final"}
        dx, reduced[i], small_parts[i] = _layer_bwd(dx, p[i, 0], {n: arrived[(i, n)] for n in names}, sm, tabs, saved[i], c_idx)
    grad_x = dx[None]

    parts = {n: jnp.stack([small_parts[i][n] for i in range(depth)]) for n in _SMALL if n != "norm_final"}
    parts["norm_final"] = dg_final[0]
    gathered = _all_gather("ag_small", _pack_small(parts, loss_part), in_vmem=True)
    g_s, d_s, nm_s, nv_s = _small_sum_adamw(gathered, _pack_small(small_w, jnp.zeros((8, LANES), f32)),
                                            _pack_small(small_m, jnp.zeros((8, LANES), f32)),
                                            _pack_small(small_v, jnp.ones((8, LANES), f32)))
    loss = g_s[sum(small_w[n].size for n in _SMALL) // LANES, 0]

    def unpack_small(flat):
        out, off = {}, 0
        for n in _SMALL:
            k = small_w[n].size // LANES
            out[n] = flat[off:off + k].reshape(small_w[n].shape)
            off += k
        return out

    sm_g, sm_d, sm_nm, sm_nv = unpack_small(g_s), unpack_small(d_s), unpack_small(nm_s), unpack_small(nv_s)

    big_g, big_d, big_nm, big_nv = {}, {}, {}, {}
    for k, n in enumerate(names):
        outs = None
        for i in range(depth):
            p4, recv2 = reduced[i][n]
            outs = _adamw_layer(f"adamw_{n}_{i}", i, turned(n, wts[n]), turned(n, mom_m[n]), turned(n, mom_v[n]), p4, recv2,
                                q_idx, outs)
        big_g[n], big_d[n], big_nm[n], big_nv[n] = [turned(n, o) for o in outs]

    order = ["w_in", "w_br_attn", "w_br_sg", "w_out", "sg_w", "sg_b", "sg_ln_g", "sg_ln_b", "norm_mix", "norm_ffn", "norm_ple",
             "norm_final", "w_ff_gate", "w_ff_up", "w_ff_down", "w_ple_gate", "w_ple"]

    def pick(big, small):
        return [big[n] if n in big else small[n] for n in order]

    return (loss, grad_x, *pick(big_g, sm_g), *pick(big_d, sm_d), *pick(big_nm, sm_nm), *pick(big_nv, sm_nv))
```

```python
import functools
import math

import jax
import jax.numpy as jnp
from jax import lax
from jax.experimental import pallas as pl
from jax.experimental.pallas import tpu as pltpu

f32 = jnp.float32
bf16 = jnp.bfloat16

HEAD_DIM = 128
N_GROUPS = 3
HEADS = 4
DILATIONS = (1, 4, 16)
RADIUS = 64
BLK = 128
QKV_W = 3 * N_GROUPS * HEADS * HEAD_DIM
ATTN_W = HEADS * HEAD_DIM
SG_CHUNK = 128
SG_GROUPS = 8
SG_W = SG_GROUPS * 128
ROPE_THETA = 10000.0
EPS = 1e-6
NEG = -1e30
N_DEV = 8
LANES = 128

ADAM_LR = 0.001
ADAM_B1 = 0.9
ADAM_B2 = 0.999
ADAM_EPS = 1e-08
ADAM_WD = 0.01
ADAM_STEP = 10

VMEM_LIMIT = 56 * 1024 * 1024
MESH = pl.DeviceIdType.MESH

NN = (((1,), (0,)), ((), ()))
NT = (((1,), (1,)), ((), ()))
TN = (((0,), (0,)), ((), ()))
_DN = {"nn": NN, "nt": NT, "tn": TN}


def _cp(*sem):
    return pltpu.CompilerParams(dimension_semantics=sem, vmem_limit_bytes=VMEM_LIMIT)


def _tile(n, pref, unit=128):
    if n <= pref:
        return n
    t = (pref // unit) * unit
    while t >= unit:
        if n % t == 0:
            return t
        t -= unit
    return n


_ANY = pl.BlockSpec(memory_space=pl.ANY)


def _call(body, *, name, grid, in_specs, out_specs, out_shape, operands, scratch=(), comm=None):
    in_specs, out_specs, out_shape, scratch = list(in_specs), list(out_specs), list(out_shape), list(scratch)
    if comm is None:
        res = pl.pallas_call(
            body, name=name, out_shape=out_shape, grid=grid, in_specs=in_specs, out_specs=out_specs, scratch_shapes=scratch,
            compiler_params=_cp(*(("arbitrary",) * len(grid))))(*operands)
        return res, []
    n_in, n_out, n_scr = len(in_specs), len(out_specs), len(scratch)
    c_in, c_out = len(comm.ins), len(comm.outs)

    def hosted(*refs):
        own_in, refs = refs[:n_in], refs[n_in:]
        ex_in, refs = refs[:c_in], refs[c_in:]
        own_out, refs = refs[:n_out], refs[n_out:]
        ex_out, refs = refs[:c_out], refs[c_out:]
        own_scr, sems = refs[:n_scr], refs[n_scr:]
        ids = [pl.program_id(a) for a in range(len(grid))]
        first = functools.reduce(jnp.logical_and, [i == 0 for i in ids])
        last = functools.reduce(jnp.logical_and, [i == g - 1 for i, g in zip(ids, grid)])

        @pl.when(first)
        def _():
            comm.start(ex_in, ex_out, sems)

        body(*own_in, *own_out, *own_scr)

        @pl.when(last)
        def _():
            comm.finish(ex_in, ex_out, sems)

    res = pl.pallas_call(
        hosted, name=name, out_shape=out_shape + list(comm.outs), grid=grid,
        in_specs=in_specs + [_ANY] * c_in, out_specs=out_specs + [_ANY] * c_out, scratch_shapes=scratch + list(comm.sems),
        input_output_aliases={n_in + a: n_out + b for a, b in comm.aliases.items()},
        compiler_params=pltpu.CompilerParams(dimension_semantics=("arbitrary",) * len(grid), vmem_limit_bytes=VMEM_LIMIT,
                                             has_side_effects=True),
    )(*operands, *comm.ins)
    return res[:n_out], res[n_out:]


def _sigmoid(x):
    return 1.0 / (1.0 + jnp.exp(-x))


_GC = math.sqrt(2.0 / math.pi)
_GA = 0.044715


def _gelu(x):
    return 0.5 * x * (1.0 + jnp.tanh(_GC * (x + _GA * x * x * x)))


def _gelu_grad(x):
    t = jnp.tanh(_GC * (x + _GA * x * x * x))
    return 0.5 * (1.0 + t) + 0.5 * x * (1.0 - t * t) * _GC * (1.0 + 3.0 * _GA * x * x)


def _matmul(name, prods, M, N, tm, tn, nk, outs, epilogue, extras=(), n_acc=1, chunk=None, comm=None):
    in_specs, operands, metas = [], [], []
    for a, b, mode, acc in prods:
        if mode == "tn":
            tk = a.shape[0] // nk
            in_specs += [pl.BlockSpec((tk, tm), lambda i, j, k: (k, i)), pl.BlockSpec((tk, tn), lambda i, j, k: (k, j))]
        elif mode == "nt":
            tk = a.shape[1] // nk
            in_specs += [pl.BlockSpec((tm, tk), lambda i, j, k: (i, k)), pl.BlockSpec((tn, tk), lambda i, j, k: (j, k))]
        else:
            tk = a.shape[1] // nk
            in_specs += [pl.BlockSpec((tm, tk), lambda i, j, k: (i, k)), pl.BlockSpec((tk, tn), lambda i, j, k: (k, j))]
        operands += [a, b]
        metas.append((mode, acc))
    for arr, bshape, imap in extras:
        in_specs.append(pl.BlockSpec(bshape, functools.partial(lambda i, j, k, f: f(i, j), f=imap)))
        operands.append(arr)
    out_specs = [pl.BlockSpec(bs, functools.partial(lambda i, j, k, f: f(i, j), f=imap)) for _, _, bs, imap in outs]
    out_shape = [jax.ShapeDtypeStruct(s, d) for s, d, _, _ in outs]
    n_prod, n_ext, n_out = len(prods), len(extras), len(outs)

    def body(*refs):
        in_refs = refs[: 2 * n_prod]
        ex_refs = refs[2 * n_prod : 2 * n_prod + n_ext]
        out_refs = refs[2 * n_prod + n_ext : 2 * n_prod + n_ext + n_out]
        acc_refs = refs[2 * n_prod + n_ext + n_out :]

        def partials():
            res = [None] * n_acc
            for idx, (mode, acc) in enumerate(metas):
                a = in_refs[2 * idx][...].astype(bf16)
                b = in_refs[2 * idx + 1][...].astype(bf16)
                d = lax.dot_general(a, b, _DN[mode], preferred_element_type=f32)
                res[acc] = d if res[acc] is None else res[acc] + d
            return res

        def finish(accs):
            vals = epilogue(accs, [r[...] for r in ex_refs])
            for r, v in zip(out_refs, vals):
                r[...] = v.astype(r.dtype)

        if nk == 1:
            step = chunk or tn
            for c0 in range(0, tn, step):
                c1 = min(c0 + step, tn)
                res = [None] * n_acc
                for idx, (mode, acc) in enumerate(metas):
                    a = in_refs[2 * idx][...].astype(bf16)
                    b_ref = in_refs[2 * idx + 1]
                    b = (b_ref[c0:c1, :] if mode == "nt" else b_ref[:, c0:c1]).astype(bf16)
                    d = lax.dot_general(a, b, _DN[mode], preferred_element_type=f32)
                    res[acc] = d if res[acc] is None else res[acc] + d
                for r, v in zip(out_refs, epilogue(res, [r[:, c0:c1] for r in ex_refs])):
                    r[:, c0:c1] = v.astype(r.dtype)
        else:
            k = pl.program_id(2)
            parts = partials()

            @pl.when(k == 0)
            def _():
                for r, d in zip(acc_refs, parts):
                    r[...] = d

            @pl.when(k > 0)
            def _():
                for r, d in zip(acc_refs, parts):
                    r[...] += d

            @pl.when(k == nk - 1)
            def _():
                finish([r[...] for r in acc_refs])

    scratch = [pltpu.VMEM((tm, tn), f32) for _ in range(n_acc)] if nk > 1 else []
    res, ex = _call(body, name=name, grid=(M // tm, N // tn, nk), in_specs=in_specs, out_specs=out_specs,
                    out_shape=out_shape, operands=operands, scratch=scratch, comm=comm)
    return res if comm is None else (res, ex)


def _ident(accs, ex):
    return [accs[0]]


def _mm_simple(name, a, b, mode, out_dtype, tm_pref=1024, tn_pref=1024, tk_pref=1024, comm=None):
    if mode == "tn":
        K, M = a.shape
        N = b.shape[1]
    elif mode == "nt":
        M, K = a.shape
        N = b.shape[0]
    else:
        M, K = a.shape
        N = b.shape[1]
    tm, tn, tk = _tile(M, tm_pref), _tile(N, tn_pref), _tile(K, tk_pref)
    res = _matmul(name, [(a, b, mode, 0)], M, N, tm, tn, K // tk,
                  [((M, N), out_dtype, (tm, tn), lambda i, j: (i, j))], _ident, comm=comm)
    return res[0] if comm is None else (res[0][0], res[1])


def _group(c, width_pref=1024):
    g = LANES // math.gcd(c, LANES)
    while g < N_DEV and 2 * g * c <= width_pref:
        g *= 2
    return g


def _join(parts):
    return parts[0] if len(parts) == 1 else jnp.concatenate(parts, axis=1)


def _mm_nn_cols(name, a, gs_list, outs_dtypes, epilogue, extras=(), tm_pref=512, width_pref=1024, tr=False, comm=None):
    M, K = a.shape
    c = gs_list[0].shape[1 if tr else 2]
    g = _group(c, width_pref)
    W = g * c
    tm = _tile(M, tm_pref, 8)
    n_g, n_ex, n_out = len(gs_list), len(extras), len(outs_dtypes)
    n_cols, n_steps = N_DEV * c, N_DEV // g
    if tr:
        g, c = 1, W
    blk = (g, c, K) if tr else (g, K, c)

    def body(*refs):
        a_ref = refs[0]
        g_refs = refs[1:1 + n_g]
        ex_refs = refs[1 + n_g:1 + n_g + n_ex]
        out_refs = refs[1 + n_g + n_ex:]
        av = a_ref[...].astype(bf16)
        cols = [epilogue([lax.dot_general(av, gr[s], NT if tr else NN, preferred_element_type=f32) for gr in g_refs],
                         [r[:, s * c:(s + 1) * c] for r in ex_refs]) for s in range(g)]
        for n, r in enumerate(out_refs):
            r[...] = _join([cols[s][n].astype(r.dtype) for s in range(g)])

    tile = pl.BlockSpec((tm, W), lambda j, i: (i, j))
    res, ex = _call(
        body, name=name, out_shape=[jax.ShapeDtypeStruct((M, n_cols), d) for d in outs_dtypes],
        grid=(n_steps, M // tm),
        in_specs=[pl.BlockSpec((tm, K), lambda j, i: (i, 0))] + [pl.BlockSpec((None,) + blk, lambda j, i: (j, 0, 0, 0))] * n_g
        + [pl.BlockSpec((tm, W), functools.partial(lambda j, i, off: (i, off + j), off=off)) for _, off in extras],
        out_specs=[tile] * n_out,
        operands=[a, *[gm.reshape((n_steps,) + blk) for gm in gs_list], *[arr for arr, _ in extras]], comm=comm)
    return res if comm is None else (res, ex)


def _mm_nt_cols(name, pairs, out_dtype, tm_pref=1024, tn_pref=1024, width_pref=1024, tr=False, comm=None):
    M = pairs[0][0].shape[0]
    c, Kw = pairs[0][1].shape[1:][::1 if tr else -1]
    g = _group(c, width_pref)
    W = g * c
    tm, tn = _tile(M, tm_pref, 8), _tile(Kw, tn_pref)
    nk = N_DEV // g
    n_p = len(pairs)
    if tr:
        g, c = 1, W

    def body(*refs):
        o_ref, acc = refs[2 * n_p], refs[2 * n_p + 1]
        k = pl.program_id(2)
        tot = None
        for n in range(n_p):
            d_ref, g_ref = refs[2 * n], refs[2 * n + 1]
            for s in range(g):
                part = lax.dot_general(d_ref[:, s * c:(s + 1) * c], g_ref[s], NN if tr else NT, preferred_element_type=f32)
                tot = part if tot is None else tot + part

        @pl.when(k == 0)
        def _():
            acc[...] = tot

        @pl.when(k > 0)
        def _():
            acc[...] += tot

        @pl.when(k == nk - 1)
        def _():
            o_ref[...] = acc[...].astype(o_ref.dtype)

    in_specs, operands = [], []
    for d, gm in pairs:
        if tr:
            wspec, wview = pl.BlockSpec((None, g, c, tn), lambda i, j, k: (k, 0, 0, j)), gm.reshape(nk, g, c, Kw)
        else:
            wspec, wview = pl.BlockSpec((None, g, tn, c), lambda i, j, k: (k, 0, j, 0)), gm.reshape(nk, g, Kw, c)
        in_specs += [pl.BlockSpec((tm, W), lambda i, j, k: (i, k)), wspec]
        operands += [d, wview]
    res, ex = _call(
        body, name=name, out_shape=[jax.ShapeDtypeStruct((M, Kw), out_dtype)], grid=(M // tm, Kw // tn, nk),
        in_specs=in_specs, out_specs=[pl.BlockSpec((tm, tn), lambda i, j, k: (i, j))],
        scratch=[pltpu.VMEM((tm, tn), f32)], operands=operands, comm=comm)
    return res[0] if comm is None else (res[0], ex)


def _mm_tn_cols(name, x, ds, c, tm_pref=1024, tk_pref=1024, width_pref=1024, tr=False, comm=None):
    S, Kw = x.shape
    g = _group(c, width_pref)
    W = g * c
    tm, tk = _tile(Kw, tm_pref), _tile(S, tk_pref, 16)
    nk = S // tk
    n_d = len(ds)
    n_steps, c0 = N_DEV // g, c
    blk = (g, c, tm) if tr else (g, tm, c)
    full = (n_steps, g, c, Kw) if tr else (n_steps, g, Kw, c)

    def body(*refs):
        x_ref = refs[0]
        d_refs = refs[1:1 + n_d]
        o_refs = refs[1 + n_d:1 + 2 * n_d]
        accs = refs[1 + 2 * n_d:]
        k = pl.program_id(2)

        @pl.when(k == 0)
        def _():
            for acc in accs:
                acc[...] = jnp.zeros_like(acc)

        xv = x_ref[...].astype(bf16)
        for d_ref, acc in zip(d_refs, accs):
            for s in range(g):
                ds_ = d_ref[:, s * c:(s + 1) * c]
                acc[s] += lax.dot_general(ds_, xv, TN, preferred_element_type=f32) if tr else \
                    lax.dot_general(xv, ds_, TN, preferred_element_type=f32)

        @pl.when(k == nk - 1)
        def _():
            for o_ref, acc in zip(o_refs, accs):
                o_ref[...] = acc[...].astype(o_ref.dtype)

    out_map = (lambda i, j, k: (j, 0, 0, i)) if tr else (lambda i, j, k: (j, 0, i, 0))
    outs, ex = _call(
        body, name=name, out_shape=[jax.ShapeDtypeStruct(full, bf16)] * n_d, grid=(Kw // tm, n_steps, nk),
        in_specs=[pl.BlockSpec((tk, tm), lambda i, j, k: (k, i))] + [pl.BlockSpec((tk, W), lambda i, j, k: (k, j))] * n_d,
        out_specs=[pl.BlockSpec((None,) + blk, out_map)] * n_d,
        scratch=[pltpu.VMEM(blk, f32)] * n_d, operands=[x, *ds], comm=comm)
    outs = [o.reshape((N_DEV, c0, Kw) if tr else (N_DEV, Kw, c0)) for o in outs]
    return outs if comm is None else (outs, ex)


def _norm_fwd(name, x, g):
    S, D = x.shape
    tm = _tile(S, 512, 8)

    def body(x_ref, g_ref, h_ref):
        xv = x_ref[...]
        r = lax.rsqrt(jnp.mean(xv * xv, axis=-1, keepdims=True) + EPS)
        h_ref[...] = (xv * r * g_ref[...]).astype(bf16)

    return pl.pallas_call(
        body, name=name, out_shape=jax.ShapeDtypeStruct((S, D), bf16), grid=(S // tm,),
        in_specs=[pl.BlockSpec((tm, D), lambda i: (i, 0)), pl.BlockSpec((1, D), lambda i: (0, 0))],
        out_specs=pl.BlockSpec((tm, D), lambda i: (i, 0)), compiler_params=_cp("parallel"),
    )(x, g.reshape(1, D))


def _norm_bwd(name, dh, x, g, dx_in):
    S, D = x.shape
    tm = _tile(S, 256, 8)

    def body(dh_ref, x_ref, g_ref, dxi_ref, dx_ref, dxb_ref, dg_ref):
        i = pl.program_id(0)
        xv = x_ref[...]
        r = lax.rsqrt(jnp.mean(xv * xv, axis=-1, keepdims=True) + EPS)
        xh = xv * r
        dhv = dh_ref[...].astype(f32)
        dxh = dhv * g_ref[...]
        dx = dxi_ref[...] + r * (dxh - xh * jnp.mean(dxh * xh, axis=-1, keepdims=True))
        dx_ref[...] = dx
        dxb_ref[...] = dx.astype(bf16)

        @pl.when(i == 0)
        def _():
            dg_ref[...] = jnp.zeros_like(dg_ref)

        dg_ref[...] += jnp.sum(dhv * xh, axis=0, keepdims=True)

    row = pl.BlockSpec((tm, D), lambda i: (i, 0))
    vec = pl.BlockSpec((1, D), lambda i: (0, 0))
    return pl.pallas_call(
        body, name=name,
        out_shape=[jax.ShapeDtypeStruct((S, D), f32), jax.ShapeDtypeStruct((S, D), bf16), jax.ShapeDtypeStruct((1, D), f32)],
        grid=(S // tm,), in_specs=[row, row, vec, row], out_specs=[row, row, vec], compiler_params=_cp("arbitrary"),
    )(dh, x, g.reshape(1, D), dx_in)


def _loss_head(x, g, t):
    S, D = x.shape
    tm = _tile(S, 256, 8)

    def body(x_ref, g_ref, t_ref, dx_ref, dg_ref, loss_ref):
        i = pl.program_id(0)
        xv = x_ref[...]
        r = lax.rsqrt(jnp.mean(xv * xv, axis=-1, keepdims=True) + EPS)
        xh = xv * r
        gv = g_ref[...]
        err = xh * gv - t_ref[...]
        dy = err * (1.0 / D)
        dxh = dy * gv
        dx_ref[...] = r * (dxh - xh * jnp.mean(dxh * xh, axis=-1, keepdims=True))

        @pl.when(i == 0)
        def _():
            dg_ref[...] = jnp.zeros_like(dg_ref)
            loss_ref[...] = jnp.zeros_like(loss_ref)

        dg_ref[...] += jnp.sum(dy * xh, axis=0, keepdims=True)
        row = jnp.sum(err * err, axis=-1, keepdims=True) * (0.5 / D)
        loss_ref[...] += jnp.broadcast_to(jnp.sum(row, axis=0, keepdims=True), loss_ref.shape)

    return pl.pallas_call(
        body, name="loss_head",
        out_shape=[jax.ShapeDtypeStruct((S, D), f32), jax.ShapeDtypeStruct((1, D), f32), jax.ShapeDtypeStruct((8, LANES), f32)],
        grid=(S // tm,),
        in_specs=[pl.BlockSpec((tm, D), lambda i: (i, 0)), pl.BlockSpec((1, D), lambda i: (0, 0)), pl.BlockSpec((tm, D), lambda i: (i, 0))],
        out_specs=[pl.BlockSpec((tm, D), lambda i: (i, 0)), pl.BlockSpec((1, D), lambda i: (0, 0)), pl.BlockSpec((8, LANES), lambda i: (0, 0))],
        compiler_params=_cp("arbitrary"),
    )(x, g.reshape(1, D), t)


def _perm(t, d):
    if d == 1:
        return t
    S, C = t.shape
    return t.reshape(S // d, d, C).transpose(1, 0, 2).reshape(S, C)


def _rope_tables(S):
    half = HEAD_DIM // 2
    pos = jnp.arange(S, dtype=f32)
    inv_freq = ROPE_THETA ** (-jnp.arange(0, HEAD_DIM, 2, dtype=f32) / HEAD_DIM)
    ang = pos[:, None] * inv_freq[None, :]
    c, s = jnp.cos(ang), jnp.sin(ang)
    cos2 = jnp.concatenate([c, c], axis=-1)
    sin2 = jnp.concatenate([-s, s], axis=-1)
    assert cos2.shape == (S, 2 * half)
    return (jnp.stack([_perm(cos2, d) for d in DILATIONS]), jnp.stack([_perm(sin2, d) for d in DILATIONS]))


def _rope(t, c, s):
    return t * c + pltpu.roll(t, HEAD_DIM // 2, 1) * s


def _rope_bwd(dt, c, s):
    return dt * c - pltpu.roll(dt, HEAD_DIM // 2, 1) * s


def _band_bounds(i, nblk):
    g = pl.program_id(0)
    lb = jnp.right_shift(jnp.int32(nblk), 2 * g)
    pos = lax.rem(i, lb)
    lo = jnp.where(pos == 0, BLK, 0)
    hi = jnp.where(pos == lb - 1, 2 * BLK, 3 * BLK)
    return lo, hi


_RUN = 4


def _cur_spec(width, t=None):
    if t is None:
        return pl.BlockSpec((None, _RUN * BLK, width), lambda g, i: (g, i, 0))
    return pl.BlockSpec((None, None, _RUN * BLK, width), lambda g, i: (t, g, i, 0))


def _band_specs(width, nblk, t=None):
    lo, hi = (lambda i: jnp.maximum(_RUN * i - 1, 0)), (lambda i: jnp.minimum(_RUN * i + _RUN, nblk - 1))
    if t is None:
        return [pl.BlockSpec((None, BLK, width), lambda g, i: (g, lo(i), 0)), _cur_spec(width),
                pl.BlockSpec((None, BLK, width), lambda g, i: (g, hi(i), 0))]
    return [pl.BlockSpec((None, None, BLK, width), lambda g, i: (t, g, lo(i), 0)), _cur_spec(width, t),
            pl.BlockSpec((None, None, BLK, width), lambda g, i: (t, g, hi(i), 0))]


def _band(sub, prev, run, nxt, cols=slice(None)):
    pieces = [(prev, slice(None))] + [(run, slice(n * BLK, (n + 1) * BLK)) for n in range(_RUN)] + [(nxt, slice(None))]
    return jnp.concatenate([ref[rows, cols] for ref, rows in pieces[sub:sub + 3]], axis=0)


_ROWS = 2048


def _to_scratch(scr, val):
    for h in range(HEADS):
        scr[h] = val[:, h * HEAD_DIM:(h + 1) * HEAD_DIM].astype(f32)


def _qkv_groups(z):
    S = z.shape[0]
    R = min(_ROWS, S)
    nb = S // R

    def body(x_ref, o_ref, scr):
        g, i = pl.program_id(1), pl.program_id(2)
        _to_scratch(scr, x_ref[...])
        for gi, d in enumerate(DILATIONS):
            @pl.when(g == gi)
            def _():
                n, L = R // d, S // d
                for r in range(d):
                    start = pl.multiple_of(r * L + i * n, 16)
                    for h in range(HEADS):
                        o_ref[pl.ds(start, n), h * HEAD_DIM:(h + 1) * HEAD_DIM] = scr[h, pl.ds(r, n, stride=d), :].astype(bf16)

    return pl.pallas_call(
        body, name="qkv_groups", out_shape=jax.ShapeDtypeStruct((3, N_GROUPS, S, ATTN_W), bf16), grid=(3, N_GROUPS, nb),
        in_specs=[pl.BlockSpec((R, ATTN_W), lambda t, g, i: (i, t * N_GROUPS + g))],
        out_specs=pl.BlockSpec((None, None, S, ATTN_W), lambda t, g, i: (t, g, 0, 0)),
        scratch_shapes=[pltpu.VMEM((HEADS, R, HEAD_DIM), f32)], compiler_params=_cp("arbitrary", "arbitrary", "arbitrary"),
    )(z)


def _bwd_groups(attn, dattn, lse):
    S = attn.shape[0]
    R = min(_ROWS, S)
    nb = S // R

    def body(a_ref, d_ref, l_ref, o_ref, scr):
        t, g, i = pl.program_id(0), pl.program_id(1), pl.program_id(2)

        @pl.when(t == 0)
        def _():
            _to_scratch(scr, d_ref[...])

        @pl.when(t == 1)
        def _():
            _to_scratch(scr, l_ref[...])

        @pl.when(t == 2)
        def _():
            prod = a_ref[...].astype(f32) * d_ref[...].astype(f32)
            for h in range(HEADS):
                part = jnp.sum(prod[:, h * HEAD_DIM:(h + 1) * HEAD_DIM], axis=-1, keepdims=True)
                scr[h] = jnp.broadcast_to(part, (R, HEAD_DIM))

        for gi, d in enumerate(DILATIONS):
            @pl.when(g == gi)
            def _():
                n, L = R // d, S // d
                for r in range(d):
                    start = pl.multiple_of(r * L + i * n, 8)
                    for h in range(HEADS):
                        o_ref[pl.ds(start, n), h * HEAD_DIM:(h + 1) * HEAD_DIM] = scr[h, pl.ds(r, n, stride=d), :]

    def nat(used):
        return pl.BlockSpec((R, ATTN_W), lambda t, g, i: (jnp.where(used(t), i, 0), 0))

    return pl.pallas_call(
        body, name="bwd_groups", out_shape=jax.ShapeDtypeStruct((3, N_GROUPS, S, ATTN_W), f32), grid=(3, N_GROUPS, nb),
        in_specs=[nat(lambda t: t == 2), nat(lambda t: t != 1), nat(lambda t: t == 1)],
        out_specs=pl.BlockSpec((None, None, S, ATTN_W), lambda t, g, i: (t, g, 0, 0)),
        scratch_shapes=[pltpu.VMEM((HEADS, R, HEAD_DIM), f32)], compiler_params=_cp("arbitrary", "arbitrary", "arbitrary"),
    )(attn, dattn, lse)


def _group_views(t3, R):
    S = t3.shape[1]
    views = [t3.reshape(N_GROUPS, d, S // d, ATTN_W) for d in DILATIONS]
    specs = [pl.BlockSpec((None, d, R // d, ATTN_W), functools.partial(lambda i, g: (g, 0, i, 0), g=g))
             for g, d in enumerate(DILATIONS)]
    return views, specs


def _from_groups(scr, ref, d):
    n = ref.shape[1]
    for r in range(d):
        blk = ref[r]
        for h in range(HEADS):
            scr[h, pl.ds(r, n, stride=d), :] = blk[:, h * HEAD_DIM:(h + 1) * HEAD_DIM].astype(f32)


def _to_natural(ts):
    S = ts[0].shape[1]
    R = min(_ROWS // 2, S)
    views, specs = [], []
    for t3 in ts:
        v, s = _group_views(t3, R)
        views, specs = views + v, specs + s
    n_in, width = len(views), len(ts) * N_GROUPS * ATTN_W

    def body(*refs):
        ins, o_ref, scr = refs[:n_in], refs[n_in], refs[n_in + 1]
        for n, ref in enumerate(ins):
            _from_groups(scr, ref, DILATIONS[n % N_GROUPS])
            for h in range(HEADS):
                o_ref[:, n * ATTN_W + h * HEAD_DIM:n * ATTN_W + (h + 1) * HEAD_DIM] = scr[h].astype(bf16)

    return pl.pallas_call(
        body, name="to_natural", out_shape=jax.ShapeDtypeStruct((S, width), bf16), grid=(S // R,),
        in_specs=specs, out_specs=pl.BlockSpec((R, width), lambda i: (i, 0)),
        scratch_shapes=[pltpu.VMEM((HEADS, R, HEAD_DIM), f32)], compiler_params=_cp("arbitrary"),
    )(*views)


_SCALE = HEAD_DIM ** -0.5


def _attn_fwd(qkv, cos, sin, comm=None):
    _, _, S, W = qkv.shape
    nblk = S // BLK

    def body(q_ref, kp, kc, kn, vp, vc, vn, cq, sq, ckp, ckc, ckn, skp, skc, skn, o_ref, lse_ref):
        i = pl.program_id(1)
        a = lax.broadcasted_iota(jnp.int32, (BLK, 3 * BLK), 0)
        b = lax.broadcasted_iota(jnp.int32, (BLK, 3 * BLK), 1)
        for sub in range(_RUN):
            rows = slice(sub * BLK, (sub + 1) * BLK)
            lo, hi = _band_bounds(_RUN * i + sub, nblk)
            mask = (jnp.abs(b - BLK - a) <= RADIUS) & (b >= lo) & (b < hi)
            ck, sk = _band(sub, ckp, ckc, ckn), _band(sub, skp, skc, skn)
            for hh in range(HEADS):
                sl = slice(hh * HEAD_DIM, (hh + 1) * HEAD_DIM)
                qh = _rope(q_ref[rows, sl].astype(f32), cq[rows, :], sq[rows, :]).astype(bf16)
                kh = _rope(_band(sub, kp, kc, kn, sl).astype(f32), ck, sk).astype(bf16)
                vh = _band(sub, vp, vc, vn, sl)
                s = lax.dot_general(qh, kh, NT, preferred_element_type=f32) * _SCALE
                s = jnp.where(mask, s, NEG)
                m = jnp.max(s, axis=-1, keepdims=True)
                e = jnp.exp(s - m)
                den = jnp.sum(e, axis=-1, keepdims=True)
                o = lax.dot_general(e.astype(bf16), vh, NN, preferred_element_type=f32) * (1.0 / den)
                o_ref[rows, sl] = o.astype(bf16)
                lse_ref[rows, sl] = jnp.broadcast_to(m + jnp.log(den), (BLK, HEAD_DIM))

    blk = _cur_spec(W)
    tab = _cur_spec(HEAD_DIM)
    res, ex = _call(
        body, name="attn_fwd",
        out_shape=[jax.ShapeDtypeStruct((N_GROUPS, S, W), bf16), jax.ShapeDtypeStruct((N_GROUPS, S, W), f32)],
        grid=(N_GROUPS, nblk // _RUN),
        in_specs=[_cur_spec(W, 0)] + _band_specs(W, nblk, 1) + _band_specs(W, nblk, 2) + [tab, tab]
        + _band_specs(HEAD_DIM, nblk) * 2,
        out_specs=[blk, blk], operands=[qkv] * 7 + [cos, sin, cos, cos, cos, sin, sin, sin], comm=comm)
    return res if comm is None else (res, ex)


def _attn_combine(o3, lse3):
    _, S, W = o3.shape
    R = min(_ROWS // 2, S)
    o_views, specs = _group_views(o3, R)
    l_views, _ = _group_views(lse3, R)

    def body(o0, o1, o2, l0, l1, l2, attn_ref, lse_ref, so0, so1, so2, sl0, sl1, sl2):
        for ref, scr, d in zip((o0, o1, o2, l0, l1, l2), (so0, so1, so2, sl0, sl1, sl2), DILATIONS * 2):
            _from_groups(scr, ref, d)
        for h in range(HEADS):
            a0, a1, a2 = sl0[h], sl1[h], sl2[h]
            m = jnp.maximum(jnp.maximum(a0, a1), a2)
            w0, w1, w2 = jnp.exp(a0 - m), jnp.exp(a1 - m), jnp.exp(a2 - m)
            den = w0 + w1 + w2
            acc = w0 * so0[h] + w1 * so1[h] + w2 * so2[h]
            attn_ref[:, h * HEAD_DIM:(h + 1) * HEAD_DIM] = (acc * (1.0 / den)).astype(bf16)
            lse_ref[:, h * HEAD_DIM:(h + 1) * HEAD_DIM] = m + jnp.log(den)

    nat = pl.BlockSpec((R, W), lambda i: (i, 0))
    return pl.pallas_call(
        body, name="attn_combine", out_shape=[jax.ShapeDtypeStruct((S, W), bf16), jax.ShapeDtypeStruct((S, W), f32)],
        grid=(S // R,), in_specs=specs * 2, out_specs=[nat, nat],
        scratch_shapes=[pltpu.VMEM((HEADS, R, HEAD_DIM), f32)] * 6, compiler_params=_cp("arbitrary"),
    )(*o_views, *l_views)


def _attn_bwd_dq(qkv, cos, sin, b3, comm=None):
    _, _, S, W = qkv.shape
    nblk = S // BLK

    def body(q_ref, kp, kc, kn, vp, vc, vn, cq, sq, ckp, ckc, ckn, skp, skc, skn, da_ref, l_ref, dl_ref, dq_ref):
        i = pl.program_id(1)
        a = lax.broadcasted_iota(jnp.int32, (BLK, 3 * BLK), 0)
        b = lax.broadcasted_iota(jnp.int32, (BLK, 3 * BLK), 1)
        for sub in range(_RUN):
            rows = slice(sub * BLK, (sub + 1) * BLK)
            lo, hi = _band_bounds(_RUN * i + sub, nblk)
            mask = (jnp.abs(b - BLK - a) <= RADIUS) & (b >= lo) & (b < hi)
            ck, sk = _band(sub, ckp, ckc, ckn), _band(sub, skp, skc, skn)
            for hh in range(HEADS):
                sl = slice(hh * HEAD_DIM, (hh + 1) * HEAD_DIM)
                qh = _rope(q_ref[rows, sl].astype(f32), cq[rows, :], sq[rows, :]).astype(bf16)
                kh = _rope(_band(sub, kp, kc, kn, sl).astype(f32), ck, sk).astype(bf16)
                vh = _band(sub, vp, vc, vn, sl)
                s = lax.dot_general(qh, kh, NT, preferred_element_type=f32) * _SCALE
                lh = l_ref[rows, sl]
                l3 = jnp.concatenate([lh, lh, lh], axis=1)
                p = jnp.exp(jnp.where(mask, s - l3, NEG))
                dp = lax.dot_general(da_ref[rows, sl].astype(bf16), vh, NT, preferred_element_type=f32)
                dh = dl_ref[rows, sl]
                ds = p * (dp - jnp.concatenate([dh, dh, dh], axis=1))
                dqh = lax.dot_general(ds.astype(bf16), kh, NN, preferred_element_type=f32) * _SCALE
                dq_ref[rows, sl] = _rope_bwd(dqh, cq[rows, :], sq[rows, :]).astype(bf16)

    tab = _cur_spec(HEAD_DIM)
    res, ex = _call(
        body, name="attn_bwd_dq", out_shape=[jax.ShapeDtypeStruct((N_GROUPS, S, W), bf16)], grid=(N_GROUPS, nblk // _RUN),
        in_specs=[_cur_spec(W, 0)] + _band_specs(W, nblk, 1) + _band_specs(W, nblk, 2) + [tab, tab]
        + _band_specs(HEAD_DIM, nblk) * 2 + [_cur_spec(W, 0), _cur_spec(W, 1), _cur_spec(W, 2)],
        out_specs=[_cur_spec(W)], operands=[qkv] * 7 + [cos, sin, cos, cos, cos, sin, sin, sin, b3, b3, b3], comm=comm)
    return res[0] if comm is None else (res[0], ex)


def _attn_bwd_dkv(qkv, cos, sin, b3, comm=None):
    _, _, S, W = qkv.shape
    nblk = S // BLK

    def body(k_ref, v_ref, ck, sk, qp, qc, qn, cqp, cqc, cqn, sqp, sqc, sqn, dap, dac, dan, lp, lc, ln, dlp, dlc, dln,
             dk_ref, dv_ref):
        j = pl.program_id(1)
        a = lax.broadcasted_iota(jnp.int32, (3 * BLK, BLK), 0)
        b = lax.broadcasted_iota(jnp.int32, (3 * BLK, BLK), 1)
        for sub in range(_RUN):
            rows = slice(sub * BLK, (sub + 1) * BLK)
            lo, hi = _band_bounds(_RUN * j + sub, nblk)
            mask = (jnp.abs(b - (a - BLK)) <= RADIUS) & (a >= lo) & (a < hi)
            cq, sq = _band(sub, cqp, cqc, cqn), _band(sub, sqp, sqc, sqn)
            for hh in range(HEADS):
                sl = slice(hh * HEAD_DIM, (hh + 1) * HEAD_DIM)
                kh = _rope(k_ref[rows, sl].astype(f32), ck[rows, :], sk[rows, :]).astype(bf16)
                vh = v_ref[rows, sl]
                qh = _rope(_band(sub, qp, qc, qn, sl).astype(f32), cq, sq).astype(bf16)
                dah = _band(sub, dap, dac, dan, sl).astype(bf16)
                lh = _band(sub, lp, lc, ln, sl)
                dlh = _band(sub, dlp, dlc, dln, sl)
                s = lax.dot_general(qh, kh, NT, preferred_element_type=f32) * _SCALE
                p = jnp.exp(jnp.where(mask, s - lh, NEG))
                dv_ref[rows, sl] = lax.dot_general(p.astype(bf16), dah, TN, preferred_element_type=f32).astype(bf16)
                dp = lax.dot_general(dah, vh, NT, preferred_element_type=f32)
                ds = p * (dp - dlh)
                dkh = lax.dot_general(ds.astype(bf16), qh, TN, preferred_element_type=f32) * _SCALE
                dk_ref[rows, sl] = _rope_bwd(dkh, ck[rows, :], sk[rows, :]).astype(bf16)

    blk, tab, bt = _cur_spec(W), _cur_spec(HEAD_DIM), _band_specs(HEAD_DIM, nblk)
    res, ex = _call(
        body, name="attn_bwd_dkv",
        out_shape=[jax.ShapeDtypeStruct((N_GROUPS, S, W), bf16), jax.ShapeDtypeStruct((N_GROUPS, S, W), bf16)],
        grid=(N_GROUPS, nblk // _RUN),
        in_specs=[_cur_spec(W, 1), _cur_spec(W, 2), tab, tab] + _band_specs(W, nblk, 0) + bt + bt
        + _band_specs(W, nblk, 0) + _band_specs(W, nblk, 1) + _band_specs(W, nblk, 2), out_specs=[blk, blk],
        operands=[qkv, qkv, cos, sin, qkv, qkv, qkv, cos, cos, cos, sin, sin, sin] + [b3] * 9, comm=comm)
    return res if comm is None else (res, ex)


_SG_ROWS = 512


def _sg_z_specs(tm, half):
    o = QKV_W // half
    return [pl.BlockSpec((tm, half), functools.partial(lambda i, c: (i, c), c=o + n)) for n in range(4)]


def _sg_norm(v, lg, lb):
    gv = _gelu(v)
    mu = jnp.mean(gv, axis=-1, keepdims=True)
    xc = gv - mu
    rstd = lax.rsqrt(jnp.mean(xc * xc, axis=-1, keepdims=True) + EPS)
    xh = xc * rstd
    return xh, rstd, xh * lg + lb


def _sg_fwd(z, w, bb, lg, lb):
    S = z.shape[0]
    tm = _tile(S, _SG_ROWS, SG_CHUNK)
    half = SG_W // 2

    def body(u0, u1, v0, v1, w_ref, bb_ref, lg_ref, lb_ref, o_ref):
        u = jnp.concatenate([u0[...], u1[...]], axis=1).astype(f32)
        v = jnp.concatenate([v0[...], v1[...]], axis=1).astype(f32)
        gu = _gelu(u)
        _, _, vn = _sg_norm(v, lg_ref[...], lb_ref[...])
        vnb = vn.astype(bf16)
        for c in range(tm // SG_CHUNK):
            rs = slice(c * SG_CHUNK, (c + 1) * SG_CHUNK)
            for g in range(SG_GROUPS):
                cs = slice(g * 128, (g + 1) * 128)
                mixed = lax.dot_general(w_ref[g], vnb[rs, cs], NN, preferred_element_type=f32) + bb_ref[g]
                o_ref[rs, cs] = (gu[rs, cs] * mixed).astype(bf16)

    full3 = pl.BlockSpec((SG_GROUPS, 128, 128), lambda i: (0, 0, 0))
    vec = pl.BlockSpec((1, SG_W), lambda i: (0, 0))
    return pl.pallas_call(
        body, name="sg_fwd", out_shape=jax.ShapeDtypeStruct((S, SG_W), bf16), grid=(S // tm,),
        in_specs=_sg_z_specs(tm, half) + [full3, full3, vec, vec],
        out_specs=pl.BlockSpec((tm, SG_W), lambda i: (i, 0)), compiler_params=_cp("parallel"),
    )(z, z, z, z, w, bb, lg.reshape(1, SG_W), lb.reshape(1, SG_W))


def _sg_bwd(dsg, z, w, wt, bb, lg, lb):
    S = z.shape[0]
    tm = _tile(S, _SG_ROWS, SG_CHUNK)
    half = SG_W // 2

    def body(d_ref, u0, u1, v0, v1, w_ref, wt_ref, bb_ref, lg_ref, lb_ref, du_ref, dv_ref, dw_ref, db_ref, dlg_ref, dlb_ref, dvn_scr):
        i = pl.program_id(0)

        @pl.when(i == 0)
        def _():
            dw_ref[...] = jnp.zeros_like(dw_ref)
            db_ref[...] = jnp.zeros_like(db_ref)
            dlg_ref[...] = jnp.zeros_like(dlg_ref)
            dlb_ref[...] = jnp.zeros_like(dlb_ref)

        u = jnp.concatenate([u0[...], u1[...]], axis=1).astype(f32)
        v = jnp.concatenate([v0[...], v1[...]], axis=1).astype(f32)
        gu = _gelu(u)
        dgu = _gelu_grad(u)
        xh, rstd, vn = _sg_norm(v, lg_ref[...], lb_ref[...])
        vnb = vn.astype(bf16)
        dsg_v = d_ref[...].astype(f32)
        for g in range(SG_GROUPS):
            cs = slice(g * 128, (g + 1) * 128)
            dw_g = jnp.zeros((128, 128), f32)
            db_g = jnp.zeros((128, 1), f32)
            for c in range(tm // SG_CHUNK):
                rs = slice(c * SG_CHUNK, (c + 1) * SG_CHUNK)
                ds = dsg_v[rs, cs]
                mixed = lax.dot_general(w_ref[g], vnb[rs, cs], NN, preferred_element_type=f32) + bb_ref[g]
                du_ref[rs, cs] = (ds * mixed * dgu[rs, cs]).astype(bf16)
                dmix = ds * gu[rs, cs]
                dmb = dmix.astype(bf16)
                dw_g = dw_g + lax.dot_general(dmb, vnb[rs, cs], NT, preferred_element_type=f32)
                db_g = db_g + jnp.sum(dmix, axis=-1, keepdims=True)
                dvn_scr[rs, cs] = lax.dot_general(wt_ref[g], dmb, NN, preferred_element_type=f32)
            dw_ref[g] += dw_g
            db_ref[g] += jnp.broadcast_to(db_g, (128, 128))
        dvn = dvn_scr[...]
        dlg_ref[...] += jnp.sum(dvn * xh, axis=0, keepdims=True)
        dlb_ref[...] += jnp.sum(dvn, axis=0, keepdims=True)
        dxh = dvn * lg_ref[...]
        dgv = rstd * (dxh - jnp.mean(dxh, axis=-1, keepdims=True) - xh * jnp.mean(dxh * xh, axis=-1, keepdims=True))
        dv_ref[...] = (dgv * _gelu_grad(v)).astype(bf16)

    full3 = pl.BlockSpec((SG_GROUPS, 128, 128), lambda i: (0, 0, 0))
    vec = pl.BlockSpec((1, SG_W), lambda i: (0, 0))
    row = pl.BlockSpec((tm, SG_W), lambda i: (i, 0))
    return pl.pallas_call(
        body, name="sg_bwd",
        out_shape=[jax.ShapeDtypeStruct((S, SG_W), bf16), jax.ShapeDtypeStruct((S, SG_W), bf16),
                   jax.ShapeDtypeStruct((SG_GROUPS, 128, 128), f32), jax.ShapeDtypeStruct((SG_GROUPS, 128, 128), f32),
                   jax.ShapeDtypeStruct((1, SG_W), f32), jax.ShapeDtypeStruct((1, SG_W), f32)],
        grid=(S // tm,),
        in_specs=[row] + _sg_z_specs(tm, half) + [full3, full3, full3, vec, vec],
        out_specs=[row, row, full3, full3, vec, vec],
        scratch_shapes=[pltpu.VMEM((tm, SG_W), f32)], compiler_params=_cp("arbitrary"),
    )(dsg, z, z, z, z, w, wt, bb, lg.reshape(1, SG_W), lb.reshape(1, SG_W))


def _ple_bwd_ew(dx, gp, e):
    S, D = dx.shape
    tm, tc = _tile(S, 512, 8), _tile(D, 1024)

    def body(dx_ref, gp_ref, e_ref, dgp_ref, de_ref):
        dxv = dx_ref[...]
        sg = _sigmoid(gp_ref[...].astype(f32))
        dgp_ref[...] = (dxv * e_ref[...].astype(f32) * sg * (1.0 - sg)).astype(bf16)
        de_ref[...] = (dxv * sg).astype(bf16)

    blk = pl.BlockSpec((tm, tc), lambda i, j: (i, j))
    return pl.pallas_call(
        body, name="ple_bwd_ew", out_shape=[jax.ShapeDtypeStruct((S, D), bf16)] * 2, grid=(S // tm, D // tc),
        in_specs=[blk, blk, blk], out_specs=[blk, blk], compiler_params=_cp("parallel", "parallel"),
    )(dx, gp, e)


def _adam_math(w, g, m, v):
    m = ADAM_B1 * m + (1.0 - ADAM_B1) * g
    v = ADAM_B2 * v + (1.0 - ADAM_B2) * (g * g)
    m_hat = m / (1.0 - ADAM_B1 ** ADAM_STEP)
    v_hat = v / (1.0 - ADAM_B2 ** ADAM_STEP)
    delta = -ADAM_LR * (m_hat / (jnp.sqrt(v_hat) + ADAM_EPS) + ADAM_WD * w)
    return delta, m, v


def _small_sum_adamw(gathered, w, m, v):
    _, R, _ = gathered.shape
    tr = _tile(R, 1024, SMALL_ROWS)

    def body(p_ref, w_ref, m_ref, v_ref, g_ref, d_ref, nm_ref, nv_ref):
        g = p_ref[0]
        for n in range(1, N_DEV):
            g = g + p_ref[n]
        d, nm, nv = _adam_math(w_ref[...], g, m_ref[...], v_ref[...])
        g_ref[...] = g
        d_ref[...] = d
        nm_ref[...] = nm
        nv_ref[...] = nv

    blk = pl.BlockSpec((tr, LANES), lambda i: (i, 0))
    return pl.pallas_call(
        body, name="small_sum_adamw", out_shape=[jax.ShapeDtypeStruct((R, LANES), f32)] * 4, grid=(R // tr,),
        in_specs=[pl.BlockSpec((N_DEV, tr, LANES), lambda i: (0, i, 0)), blk, blk, blk], out_specs=[blk] * 4,
        compiler_params=_cp("parallel"),
    )(gathered, w, m, v)


def _all_gather(name, shard, in_vmem=False):
    R, C = shard.shape

    def body(x_ref, out_ref, send_sems, recv_sems, local_sem):
        x, y, c = lax.axis_index("x"), lax.axis_index("y"), lax.axis_index("c")
        me, sibling = (x, y, c), (x, y, 1 - c)
        chips = [(1 - x, y), (x, 1 - y), (1 - x, 1 - y)]

        def rows(px, py, pc):
            return out_ref.at[4 * px + 2 * py + pc]

        def copy(k, block, to, src=None):
            return pltpu.make_async_remote_copy(
                src_ref=rows(*block) if src is None else src, dst_ref=rows(*block),
                send_sem=send_sems.at[k], recv_sem=recv_sems.at[k], device_id=to, device_id_type=MESH)

        mine = pltpu.make_async_copy(x_ref, rows(*me), local_sem)
        mine.start()
        first = [copy(0, me, sibling, src=x_ref)]
        first += [copy(1 + j, me, (*chip, c), src=x_ref) for j, chip in enumerate(chips)]
        for cp in first:
            cp.start()
        passed = [copy(4 + j, (*chip, c), sibling) for j, chip in enumerate(chips)]
        for j, chip in enumerate(chips):
            copy(1 + j, (*chip, c), me).wait_recv()
            passed[j].start()
        copy(0, sibling, me).wait_recv()
        for j, chip in enumerate(chips):
            copy(4 + j, (*chip, 1 - c), me).wait_recv()
        for cp in first + passed:
            cp.wait_send()
        mine.wait()

    space = pl.BlockSpec(memory_space=pltpu.VMEM) if in_vmem else _ANY
    return pl.pallas_call(
        body, name=name, out_shape=jax.ShapeDtypeStruct((N_DEV, R, C), shard.dtype),
        in_specs=[space], out_specs=space,
        scratch_shapes=[pltpu.SemaphoreType.DMA((7,)), pltpu.SemaphoreType.DMA((7,)), pltpu.SemaphoreType.DMA],
        compiler_params=pltpu.CompilerParams(has_side_effects=True, vmem_limit_bytes=VMEM_LIMIT),
    )(shard)


class _Exchange:
    def __init__(self, ins, outs, sems, start, finish, aliases=None):
        self.ins, self.outs, self.sems, self.start, self.finish = list(ins), list(outs), list(sems), start, finish
        self.aliases = dict(aliases or {})


def _run_exchange(name, ex):
    c_in, c_out = len(ex.ins), len(ex.outs)

    def body(*refs):
        ins, outs, sems = refs[:c_in], refs[c_in:c_in + c_out], refs[c_in + c_out:]
        ex.start(ins, outs, sems)
        ex.finish(ins, outs, sems)

    return pl.pallas_call(
        body, name=name, out_shape=ex.outs, in_specs=[_ANY] * c_in, out_specs=[_ANY] * c_out, scratch_shapes=ex.sems,
        input_output_aliases=ex.aliases,
        compiler_params=pltpu.CompilerParams(has_side_effects=True, vmem_limit_bytes=VMEM_LIMIT),
    )(*ex.ins)


def _both(e1, e2):
    i1, o1, s1 = len(e1.ins), len(e1.outs), len(e1.sems)

    def start(ins, outs, sems):
        e1.start(ins[:i1], outs[:o1], sems[:s1])
        e2.start(ins[i1:], outs[o1:], sems[s1:])

    def finish(ins, outs, sems):
        e1.finish(ins[:i1], outs[:o1], sems[:s1])
        e2.finish(ins[i1:], outs[o1:], sems[s1:])

    aliases = dict(e1.aliases)
    aliases.update({i1 + a: o1 + b for a, b in e2.aliases.items()})
    return _Exchange(e1.ins + e2.ins, e1.outs + e2.outs, e1.sems + e2.sems, start, finish, aliases)


def _gather_exchange(shards, parts=None, into=None):
    n = len(shards)
    parts = parts or [None] * n
    into = into or [None] * n
    given = [w for w in range(n) if into[w] is not None]

    def plan(ins, outs, sems):
        send_sems, recv_sems, local_sems = sems
        x, y, c = lax.axis_index("x"), lax.axis_index("y"), lax.axis_index("c")
        me, sibling = (x, y, c), (x, y, 1 - c)
        chips = [(1 - x, y), (x, 1 - y), (1 - x, 1 - y)]

        def cut(ref, w):
            return ref if parts[w] is None else ref.at[pl.ds(parts[w][0], parts[w][1])]

        def rows(w, px, py, pc):
            return cut(outs[w].at[4 * px + 2 * py + pc], w)

        def copy(w, k, block, to, src=None):
            return pltpu.make_async_remote_copy(
                src_ref=rows(w, *block) if src is None else src, dst_ref=rows(w, *block),
                send_sem=send_sems.at[w, k], recv_sem=recv_sems.at[w, k], device_id=to, device_id_type=MESH)

        mine = [pltpu.make_async_copy(cut(ins[w], w), rows(w, *me), local_sems.at[w]) for w in range(n)]
        first = []
        for w in range(n):
            first.append(copy(w, 0, me, sibling, src=cut(ins[w], w)))
            first += [copy(w, 1 + j, me, (*chip, c), src=cut(ins[w], w)) for j, chip in enumerate(chips)]
        return c, me, sibling, chips, copy, mine, first

    def start(ins, outs, sems):
        _, _, _, _, _, mine, first = plan(ins, outs, sems)
        for cp in mine + first:
            cp.start()

    def finish(ins, outs, sems):
        c, me, sibling, chips, copy, mine, first = plan(ins, outs, sems)
        passed = []
        for w in range(n):
            for j, chip in enumerate(chips):
                copy(w, 1 + j, (*chip, c), me).wait_recv()
                passed.append(copy(w, 4 + j, (*chip, c), sibling))
                passed[-1].start()
        for w in range(n):
            copy(w, 0, sibling, me).wait_recv()
            for j, chip in enumerate(chips):
                copy(w, 4 + j, (*chip, 1 - c), me).wait_recv()
        for cp in first + passed:
            cp.wait_send()
        for cp in mine:
            cp.wait()

    return _Exchange(
        list(shards) + [into[w] for w in given], [jax.ShapeDtypeStruct((N_DEV,) + s.shape, s.dtype) for s in shards],
        [pltpu.SemaphoreType.DMA((n, 7)), pltpu.SemaphoreType.DMA((n, 7)), pltpu.SemaphoreType.DMA((n,))], start, finish,
        {n + k: w for k, w in enumerate(given)})


def _sibling_exchange(gs):
    n = len(gs)

    def copies(ins, outs, sems):
        send_sems, recv_sems = sems
        x, y, c = lax.axis_index("x"), lax.axis_index("y"), lax.axis_index("c")
        return [pltpu.make_async_remote_copy(
            src_ref=ins[w].at[2 * q + (1 - c)], dst_ref=outs[w].at[q], send_sem=send_sems.at[w, q],
            recv_sem=recv_sems.at[w, q], device_id=(x, y, 1 - c), device_id_type=MESH) for w in range(n) for q in range(4)]

    def start(ins, outs, sems):
        for cp in copies(ins, outs, sems):
            cp.start()

    def finish(ins, outs, sems):
        cps = copies(ins, outs, sems)
        for cp in cps:
            cp.wait_recv()
        for cp in cps:
            cp.wait_send()

    return _Exchange(gs, [jax.ShapeDtypeStruct((4,) + g.shape[1:], g.dtype) for g in gs],
                     [pltpu.SemaphoreType.DMA((n, 4)), pltpu.SemaphoreType.DMA((n, 4))], start, finish)


def _rs_chip_sum(name, g8, recv, c_idx):
    _, R, C = g8.shape
    tr = _tile(R, 512, 16)
    g42 = g8.reshape(4, 2, R, C)

    def body(c_ref, a_ref, b_ref, o_ref):
        o_ref[...] = (a_ref[...].astype(f32) + b_ref[...].astype(f32)).astype(o_ref.dtype)

    return pl.pallas_call(
        body, name=name, out_shape=jax.ShapeDtypeStruct((4, R, C), g8.dtype),
        grid_spec=pltpu.PrefetchScalarGridSpec(
            num_scalar_prefetch=1, grid=(4, R // tr),
            in_specs=[pl.BlockSpec((None, None, tr, C), lambda q, r, c_ref: (q, c_ref[0], r, 0)),
                      pl.BlockSpec((None, tr, C), lambda q, r, c_ref: (q, r, 0))],
            out_specs=pl.BlockSpec((None, tr, C), lambda q, r, c_ref: (q, r, 0))),
        compiler_params=_cp("parallel", "parallel"),
    )(c_idx, g42, recv)


def _chips_exchange(p4s):
    n = len(p4s)

    def copies(ins, outs, sems):
        send_sems, recv_sems = sems
        x, y, c = lax.axis_index("x"), lax.axis_index("y"), lax.axis_index("c")
        chips = [(1 - x, y), (x, 1 - y), (1 - x, 1 - y)]
        return [pltpu.make_async_remote_copy(
            src_ref=ins[w].at[2 * cx + cy], dst_ref=outs[w].at[k], send_sem=send_sems.at[w, k],
            recv_sem=recv_sems.at[w, k], device_id=(cx, cy, c), device_id_type=MESH)
            for w in range(n) for k, (cx, cy) in enumerate(chips)]

    def start(ins, outs, sems):
        for cp in copies(ins, outs, sems):
            cp.start()

    def finish(ins, outs, sems):
        cps = copies(ins, outs, sems)
        for cp in cps:
            cp.wait_recv()
        for cp in cps:
            cp.wait_send()

    return _Exchange(p4s, [jax.ShapeDtypeStruct((3,) + p.shape[1:], p.dtype) for p in p4s],
                     [pltpu.SemaphoreType.DMA((n, 3)), pltpu.SemaphoreType.DMA((n, 3))], start, finish)


def _adamw_layer(name, layer, w, m, v, p4, recv, q_idx, prev):
    depth, R, C = w.shape
    tr = _tile(R, 256, 8)

    def body(q_ref, w_ref, m_ref, v_ref, a_ref, b_ref, *rest):
        g_ref, d_ref, nm_ref, nv_ref = rest[-4:]
        g = ((a_ref[...].astype(f32) + b_ref[0].astype(f32)) + b_ref[1].astype(f32)) + b_ref[2].astype(f32)
        d, nm, nv = _adam_math(w_ref[...], g, m_ref[...], v_ref[...])
        g_ref[...] = g
        d_ref[...] = d
        nm_ref[...] = nm
        nv_ref[...] = nv

    lay = pl.BlockSpec((None, tr, C), lambda i, q_ref: (layer, i, 0))
    n_prev = 0 if prev is None else 4
    return pl.pallas_call(
        body, name=name, out_shape=[jax.ShapeDtypeStruct((depth, R, C), f32)] * 4,
        grid_spec=pltpu.PrefetchScalarGridSpec(
            num_scalar_prefetch=1, grid=(R // tr,),
            in_specs=[lay, lay, lay, pl.BlockSpec((None, tr, C), lambda i, q_ref: (q_ref[0], i, 0)),
                      pl.BlockSpec((3, tr, C), lambda i, q_ref: (0, i, 0))] + [_ANY] * n_prev,
            out_specs=[lay] * 4),
        input_output_aliases={6 + n: n for n in range(n_prev)},
        compiler_params=_cp("parallel"),
    )(q_idx, w, m, v, p4, recv, *(prev or ()))


_BIG = (("w_in", 1), ("w_br_attn", 1), ("w_br_sg", 1), ("w_out", 0), ("w_ff_gate", 1), ("w_ff_up", 1),
        ("w_ff_down", 0), ("w_ple_gate", 0), ("w_ple", 1))


_TURNED = ("w_in", "w_ff_gate", "w_ff_up")


def _gather_plan(i, depth):
    mixer = ["w_br_attn", "w_br_sg", "w_out"]
    plan = {
        "mm_in": [(i, "w_ff_gate")] + [(i, n) for n in mixer],
        "attn_fwd": [(i, "w_ff_up")],
        "mm_ffn_in": [(i, "w_ff_down")],
    }
    last = [(i, "w_ple_gate"), (i, "w_ple")]
    if i + 1 < depth:
        plan["mm_out"] = last
        plan["mm_ffn_in"] = plan["mm_ffn_in"] + [(i + 1, "w_in", 0)]
        plan["mm_ffn_out"] = [(i + 1, "w_in", 1)]
    else:
        plan["mm_ffn_out"] = last
    return plan


def _layer_fwd(x0, p_i, layer, arrived, sm, tabs, comm):
    S, D = x0.shape
    cos, sin = tabs
    tmm = _tile(S, 1024, 8)
    same = lambda accs, ex: accs

    def W(name):
        return arrived[(layer, name)]

    def hosted(key, fn):
        if key not in comm:
            return fn(None)
        ex, keys = comm[key]()
        res, outs = fn(ex)
        arrived.update(zip(keys, outs))
        return res

    h1 = _norm_fwd("norm_fwd", x0, sm["norm_mix"])
    z = hosted("mm_in", lambda ex: _mm_nn_cols("mm_in", h1, [W("w_in")], [bf16], same, tr=True, comm=ex))[0]
    IN = z.shape[1]

    qkv = _qkv_groups(z)
    o3, lse3 = hosted("attn_fwd", lambda ex: _attn_fwd(qkv, cos, sin, comm=ex))
    attn, lse = _attn_combine(o3, lse3)
    ya = _mm_nn_cols("mm_br_attn", attn, [W("w_br_attn")], [bf16], same, tm_pref=1024)[0]
    sgw = sm["sg_w"].astype(bf16)
    bb = jnp.broadcast_to(sm["sg_b"][:, :, None], (SG_GROUPS, SG_CHUNK, 128))
    sg = _sg_fwd(z, sgw, bb, sm["sg_ln_g"], sm["sg_ln_b"])
    wide = _tile(D, 512)

    def merge_ep(accs, ex):
        yb_t = accs[0]
        ga, gb = _sigmoid(ex[0].astype(f32)), _sigmoid(ex[1].astype(f32))
        return [yb_t, ga * ex[2].astype(f32) + gb * yb_t]

    yb, merged = _mm_nn_cols("mm_br_sg", sg, [W("w_br_sg")], [bf16, bf16], merge_ep, tm_pref=1024, width_pref=wide,
                             extras=[(z, (QKV_W + 2 * SG_W) // wide), (z, (QKV_W + 2 * SG_W + D) // wide), (ya, 0)])
    tn = _tile(D, 1024)
    x1 = hosted("mm_out", lambda ex: _matmul(
        "mm_out", [(merged, W("w_out").reshape(D, D), "nn", 0)], S, D, tmm, tn, 1,
        [((S, D), f32, (tmm, tn), lambda i, j: (i, j))], lambda accs, ex_tiles: [ex_tiles[0] + accs[0]],
        extras=[(x0, (tmm, tn), lambda i, j: (i, j))], chunk=512, comm=ex))[0]
    h2 = _norm_fwd("norm_fwd", x1, sm["norm_ffn"])

    def ffn_ep(accs, ex):
        a, b = accs
        sg = _sigmoid(a)
        silu = a * sg
        return [b * sg * (1.0 + a * (1.0 - sg)), silu, silu * b]

    dfa, dfb, f = hosted("mm_ffn_in", lambda ex: _mm_nn_cols("mm_ffn_in", h2, [W("w_ff_gate"), W("w_ff_up")], [bf16] * 3, ffn_ep,
                                                         tr=True, comm=ex))
    w_down = W("w_ff_down").reshape(-1, D)
    F = w_down.shape[0]
    thin = _tile(D, 512)
    x2 = hosted("mm_ffn_out", lambda ex: _matmul(
        "mm_ffn_out", [(f, w_down, "nn", 0)], S, D, tmm, thin, 1, [((S, D), f32, (tmm, thin), lambda i, j: (i, j))],
        lambda accs, ex_tiles: [ex_tiles[0] + accs[0]], extras=[(x1, (tmm, thin), lambda i, j: (i, j))], comm=ex))[0]
    h3 = _norm_fwd("norm_fwd", x2, sm["norm_ple"])

    e = _mm_nn_cols("mm_ple_emb", p_i, [W("w_ple")], [bf16], same, tm_pref=1024)[0]

    def ple_ep(accs, ex):
        gp = accs[0]
        return [ex[0] + _sigmoid(gp) * ex[1].astype(f32), gp]

    x3, gp = hosted("mm_ple", lambda ex: _matmul(
        "mm_ple", [(h3, W("w_ple_gate").reshape(D, D), "nn", 0)], S, D, tmm, tn, 1,
        [((S, D), f32, (tmm, tn), lambda i, j: (i, j)), ((S, D), bf16, (tmm, tn), lambda i, j: (i, j))],
        ple_ep, extras=[(x2, (tmm, tn), lambda i, j: (i, j)), (e, (tmm, tn), lambda i, j: (i, j))], chunk=512, comm=ex))
    saved = dict(x0=x0, h1=h1, z=z, qkv=qkv, attn=attn, lse=lse, ya=ya, yb=yb, sg=sg, merged=merged, x1=x1,
                 h2=h2, dfa=dfa, dfb=dfb, f=f, x2=x2, h3=h3, gp=gp, e=e, sgw=sgw, bb=bb, IN=IN)
    return x3, saved


def _layer_bwd(dx3, p_i, W, sm, tabs, sv, c_idx):
    S, D = dx3.shape
    w_out, w_down, w_pg = W["w_out"].reshape(D, D), W["w_ff_down"].reshape(-1, D), W["w_ple_gate"].reshape(D, D)
    F = w_down.shape[0]
    cos, sin = tabs
    tmm = _tile(S, 1024, 8)
    tn = _tile(D, 512)
    reduced = {}

    def to_sibling(grads):
        return _sibling_exchange(list(grads.values()))

    def chip_sums(grads, recv):
        return {n: _rs_chip_sum("rs_chip_sum_" + n, grads[n], r, c_idx) for n, r in zip(grads, recv)}

    def to_owners(p4, names=None):
        return _chips_exchange([p4[n] for n in (names or p4)])

    def carry(p4, outs, names=None):
        reduced.update({n: (p4[n], r) for n, r in zip(names or p4, outs)})

    def blocks(full):
        return full.reshape(N_DEV, full.shape[0] // N_DEV, full.shape[1])

    dgp, de = _ple_bwd_ew(dx3, sv["gp"], sv["e"])
    d_w_ple = _mm_tn_cols("mm_dw_ple", p_i, [de], D // N_DEV)[0]
    d_w_pg = blocks(_mm_simple("mm_dw_dd", sv["h3"], dgp, "tn", bf16, 1024, 1024, 2048))
    dh3 = _mm_simple("mm_dh_dd", dgp, w_pg, "nt", bf16, 1024, 1024, 2048)
    dx2, dx2b, dg_ple = _norm_bwd("norm_bwd", dh3, sv["x2"], sm["norm_ple"], dx3)
    tf = _tile(F, 1408)

    def ffn_bwd_ep(accs, ex):
        df = accs[0]
        return [df * ex[0].astype(f32), df * ex[1].astype(f32)]

    th = _tile(S, 512, 8)
    da, db = _matmul("mm_dffn", [(dx2b, w_down, "nt", 0)], S, F, th, tf, 1,
                     [((S, F), bf16, (th, tf), lambda i, j: (i, j))] * 2, ffn_bwd_ep,
                     extras=[(sv["dfa"], (th, tf), lambda i, j: (i, j)), (sv["dfb"], (th, tf), lambda i, j: (i, j))], chunk=512)
    d_w_down = blocks(_mm_simple("mm_dw_fd", sv["f"], dx2b, "tn", bf16, 1408, 1024, 2048))
    g_a = dict(w_ff_down=d_w_down, w_ple_gate=d_w_pg, w_ple=d_w_ple)
    (d_w_gate, d_w_up), recv = _mm_tn_cols("mm_dw_df", sv["h2"], [da, db], F // N_DEV, tr=True, comm=to_sibling(g_a))
    p4_a = chip_sums(g_a, recv)
    g_b = dict(w_ff_gate=d_w_gate, w_ff_up=d_w_up)
    dh2, outs = _mm_nt_cols("mm_dh_ffn", [(da, W["w_ff_gate"]), (db, W["w_ff_up"])], bf16, tr=True,
                            comm=_both(to_owners(p4_a), to_sibling(g_b)))
    carry(p4_a, outs[:len(p4_a)])
    p4_b = chip_sums(g_b, outs[len(p4_a):])
    dx1, dx1b, dg_ffn = _norm_bwd("norm_bwd", dh2, sv["x1"], sm["norm_ffn"], dx2)
    z = sv["z"]
    o_a, o_b = (QKV_W + 2 * SG_W) // tn, (QKV_W + 2 * SG_W + D) // tn

    def merge_bwd_ep(accs, ex):
        dm = accs[0]
        ga, gb = _sigmoid(ex[0].astype(f32)), _sigmoid(ex[1].astype(f32))
        ya, yb = ex[2].astype(f32), ex[3].astype(f32)
        return [dm * ya * ga * (1.0 - ga), dm * yb * gb * (1.0 - gb), dm * ga, dm * gb]

    dga, dgb, dya, dyb = _matmul(
        "mm_dmerge", [(dx1b, w_out, "nt", 0)], S, D, tmm, tn, 1,
        [((S, D), bf16, (tmm, tn), lambda i, j: (i, j))] * 4, merge_bwd_ep,
        extras=[(z, (tmm, tn), lambda i, j: (i, o_a + j)), (z, (tmm, tn), lambda i, j: (i, o_b + j)),
                (sv["ya"], (tmm, tn), lambda i, j: (i, j)), (sv["yb"], (tmm, tn), lambda i, j: (i, j))], chunk=256)
    d_w_out = blocks(_mm_simple("mm_dw_dd", sv["merged"], dx1b, "tn", bf16, 1024, 1024, 2048))
    dsg = _mm_nt_cols("mm_dsg", [(dyb, W["w_br_sg"])], bf16)
    d_w_bsg = _mm_tn_cols("mm_dw_bsg", sv["sg"], [dyb], D // N_DEV)[0]
    dattn = _mm_nt_cols("mm_dattn", [(dya, W["w_br_attn"])], bf16)
    d_w_battn = _mm_tn_cols("mm_dw_battn", sv["attn"], [dya], D // N_DEV)[0]
    sgwt = jnp.swapaxes(sm["sg_w"], 1, 2).astype(bf16)
    du, dv_sg, d_sgw, d_sgb, d_lg, d_lb = _sg_bwd(dsg, z, sv["sgw"], sgwt, sv["bb"], sm["sg_ln_g"], sm["sg_ln_b"])
    b3 = _bwd_groups(sv["attn"], dattn, sv["lse"])
    g_c = dict(w_out=d_w_out, w_br_sg=d_w_bsg, w_br_attn=d_w_battn)
    dqg, outs = _attn_bwd_dq(sv["qkv"], cos, sin, b3, comm=_both(to_owners(p4_b, ["w_ff_gate"]), to_sibling(g_c)))
    carry(p4_b, outs[:1], ["w_ff_gate"])
    p4_c = chip_sums(g_c, outs[1:])
    (dkg, dvg), outs = _attn_bwd_dkv(sv["qkv"], cos, sin, b3, comm=to_owners(p4_b, ["w_ff_up"]))
    carry(p4_b, outs, ["w_ff_up"])
    dz = jnp.concatenate([_to_natural([dqg, dkg, dvg]), du, dv_sg, dga, dgb], axis=1)
    (d_w_in,), outs = _mm_tn_cols("mm_dw_in", sv["h1"], [dz], sv["IN"] // N_DEV, tr=True, comm=to_owners(p4_c))
    carry(p4_c, outs)
    g_d = dict(w_in=d_w_in)
    p4_d = chip_sums(g_d, _run_exchange("rs_sibling_w_in", to_sibling(g_d)))
    dh1, outs = _mm_nt_cols("mm_dh_in", [(dz, W["w_in"])], bf16, tr=True, comm=to_owners(p4_d))
    carry(p4_d, outs)
    dx0, _, dg_mix = _norm_bwd("norm_bwd", dh1, sv["x0"], sm["norm_mix"], dx1)
    small = dict(sg_w=d_sgw, sg_b=d_sgb[:, :, 0], sg_ln_g=d_lg[0], sg_ln_b=d_lb[0], norm_mix=dg_mix[0], norm_ffn=dg_ffn[0],
                 norm_ple=dg_ple[0])
    return dx0, reduced, small


_SMALL = ("sg_w", "sg_b", "sg_ln_g", "sg_ln_b", "norm_mix", "norm_ffn", "norm_ple", "norm_final")


SMALL_ROWS = 256


def _pack_small(parts, tail):
    rows = [parts[n].astype(f32).reshape(-1, LANES) for n in _SMALL] + [tail]
    n = sum(r.shape[0] for r in rows)
    return jnp.concatenate(rows + [jnp.zeros((-n % SMALL_ROWS, LANES), f32)], axis=0)


def kernel(x, p, w_in, w_br_attn, w_br_sg, w_out, sg_w, sg_b, sg_ln_g, sg_ln_b, norm_mix, norm_ffn, norm_ple, norm_final, w_ff_gate, w_ff_up, w_ff_down, w_ple_gate, w_ple, loss_target, m_w_in, m_w_br_attn, m_w_br_sg, m_w_out, m_sg_w, m_sg_b, m_sg_ln_g, m_sg_ln_b, m_norm_mix, m_norm_ffn, m_norm_ple, m_norm_final, m_w_ff_gate, m_w_ff_up, m_w_ff_down, m_w_ple_gate, m_w_ple, v_w_in, v_w_br_attn, v_w_br_sg, v_w_out, v_sg_w, v_sg_b, v_sg_ln_g, v_sg_ln_b, v_norm_mix, v_norm_ffn, v_norm_ple, v_norm_final, v_w_ff_gate, v_w_ff_up, v_w_ff_down, v_w_ple_gate, v_w_ple):
    wts = dict(w_in=w_in, w_br_attn=w_br_attn, w_br_sg=w_br_sg, w_out=w_out, w_ff_gate=w_ff_gate, w_ff_up=w_ff_up,
               w_ff_down=w_ff_down, w_ple_gate=w_ple_gate, w_ple=w_ple)
    mom_m = dict(w_in=m_w_in, w_br_attn=m_w_br_attn, w_br_sg=m_w_br_sg, w_out=m_w_out, w_ff_gate=m_w_ff_gate,
                 w_ff_up=m_w_ff_up, w_ff_down=m_w_ff_down, w_ple_gate=m_w_ple_gate, w_ple=m_w_ple)
    mom_v = dict(w_in=v_w_in, w_br_attn=v_w_br_attn, w_br_sg=v_w_br_sg, w_out=v_w_out, w_ff_gate=v_w_ff_gate,
                 w_ff_up=v_w_ff_up, w_ff_down=v_w_ff_down, w_ple_gate=v_w_ple_gate, w_ple=v_w_ple)
    small_w = dict(sg_w=sg_w, sg_b=sg_b, sg_ln_g=sg_ln_g, sg_ln_b=sg_ln_b, norm_mix=norm_mix, norm_ffn=norm_ffn,
                   norm_ple=norm_ple, norm_final=norm_final)
    small_m = dict(sg_w=m_sg_w, sg_b=m_sg_b, sg_ln_g=m_sg_ln_g, sg_ln_b=m_sg_ln_b, norm_mix=m_norm_mix, norm_ffn=m_norm_ffn,
                   norm_ple=m_norm_ple, norm_final=m_norm_final)
    small_v = dict(sg_w=v_sg_w, sg_b=v_sg_b, sg_ln_g=v_sg_ln_g, sg_ln_b=v_sg_ln_b, norm_mix=v_norm_mix, norm_ffn=v_norm_ffn,
                   norm_ple=v_norm_ple, norm_final=v_norm_final)
    depth = w_in.shape[0]
    S = x.shape[1]
    names = [n for n, _ in _BIG]
    c_idx = lax.axis_index("c").astype(jnp.int32).reshape(1)
    q_idx = (2 * lax.axis_index("x") + lax.axis_index("y")).astype(jnp.int32).reshape(1)
    tabs = _rope_tables(S)

    def turned(n, t):
        return jnp.swapaxes(t, -1, -2) if n in _TURNED else t

    arrived = {}

    def gather(keys):
        shards, parts, into = [], [], []
        for key in keys:
            s = turned(key[1], wts[key[1]][key[0]]).astype(bf16)
            shards.append(s)
            half = s.shape[0] // 2
            parts.append((key[2] * half, half) if len(key) == 3 else None)
            into.append(arrived.get(key[:2]) if len(key) == 3 else None)
        return _gather_exchange(shards, parts, into), [key[:2] for key in keys]

    ex, keys = gather([(0, "w_in")])
    arrived.update(zip(keys, _run_exchange("ag_w_in", ex)))

    xs = x[0]
    saved = []
    for i in range(depth):
        sm = {n: small_w[n][i] for n in _SMALL if n != "norm_final"}
        comm = {carrier: functools.partial(gather, keys) for carrier, keys in _gather_plan(i, depth).items()}
        xs, sv = _layer_fwd(xs, p[i, 0], i, arrived, sm, tabs, comm)
        saved.append(sv)
    dx, dg_final, loss_part = _loss_head(xs, norm_final, loss_target[0])

    reduced = [None] * depth
    small_parts = [None] * depth
    for i in reversed(range(depth)):
        sm = {n: small_w[n][i] for n in _SMALL if n != "norm_final"}
        dx, reduced[i], small_parts[i] = _layer_bwd(dx, p[i, 0], {n: arrived[(i, n)] for n in names}, sm, tabs, saved[i], c_idx)
    grad_x = dx[None]

    parts = {n: jnp.stack([small_parts[i][n] for i in range(depth)]) for n in _SMALL if n != "norm_final"}
    parts["norm_final"] = dg_final[0]
    gathered = _all_gather("ag_small", _pack_small(parts, loss_part), in_vmem=True)
    g_s, d_s, nm_s, nv_s = _small_sum_adamw(gathered, _pack_small(small_w, jnp.zeros((8, LANES), f32)),
                                            _pack_small(small_m, jnp.zeros((8, LANES), f32)),
                                            _pack_small(small_v, jnp.ones((8, LANES), f32)))
    loss = g_s[sum(small_w[n].size for n in _SMALL) // LANES, 0]

    def unpack_small(flat):
        out, off = {}, 0
        for n in _SMALL:
            k = small_w[n].size // LANES
            out[n] = flat[off:off + k].reshape(small_w[n].shape)
            off += k
        return out

    sm_g, sm_d, sm_nm, sm_nv = unpack_small(g_s), unpack_small(d_s), unpack_small(nm_s), unpack_small(nv_s)

    big_g, big_d, big_nm, big_nv = {}, {}, {}, {}
    for k, n in enumerate(names):
        outs = None
        for i in range(depth):
            p4, recv2 = reduced[i][n]
            outs = _adamw_layer(f"adamw_{n}_{i}", i, turned(n, wts[n]), turned(n, mom_m[n]), turned(n, mom_v[n]), p4, recv2,
                                q_idx, outs)
        big_g[n], big_d[n], big_nm[n], big_nv[n] = [turned(n, o) for o in outs]

    order = ["w_in", "w_br_attn", "w_br_sg", "w_out", "sg_w", "sg_b", "sg_ln_g", "sg_ln_b", "norm_mix", "norm_ffn", "norm_ple",
             "norm_final", "w_ff_gate", "w_ff_up", "w_ff_down", "w_ple_gate", "w_ple"]

    def pick(big, small):
        return [big[n] if n in big else small[n] for n in order]

    return (loss, grad_x, *pick(big_g, sm_g), *pick(big_d, sm_d), *pick(big_nm, sm_nm), *pick(big_nv, sm_nv))
```

```python
import functools
import math

import jax
import jax.numpy as jnp
from jax import lax
from jax.experimental import pallas as pl
from jax.experimental.pallas import tpu as pltpu

f32 = jnp.float32
bf16 = jnp.bfloat16

HEAD_DIM = 128
N_GROUPS = 3
HEADS = 4
DILATIONS = (1, 4, 16)
RADIUS = 64
BLK = 128
QKV_W = 3 * N_GROUPS * HEADS * HEAD_DIM
ATTN_W = HEADS * HEAD_DIM
SG_CHUNK = 128
SG_GROUPS = 8
SG_W = SG_GROUPS * 128
ROPE_THETA = 10000.0
EPS = 1e-6
NEG = -1e30
N_DEV = 8
LANES = 128

ADAM_LR = 0.001
ADAM_B1 = 0.9
ADAM_B2 = 0.999
ADAM_EPS = 1e-08
ADAM_WD = 0.01
ADAM_STEP = 10

VMEM_LIMIT = 56 * 1024 * 1024
MESH = pl.DeviceIdType.MESH

NN = (((1,), (0,)), ((), ()))
NT = (((1,), (1,)), ((), ()))
TN = (((0,), (0,)), ((), ()))
_DN = {"nn": NN, "nt": NT, "tn": TN}


def _cp(*sem):
    return pltpu.CompilerParams(dimension_semantics=sem, vmem_limit_bytes=VMEM_LIMIT)


def _tile(n, pref, unit=128):
    if n <= pref:
        return n
    t = (pref // unit) * unit
    while t >= unit:
        if n % t == 0:
            return t
        t -= unit
    return n


_ANY = pl.BlockSpec(memory_space=pl.ANY)


def _call(body, *, name, grid, in_specs, out_specs, out_shape, operands, scratch=(), comm=None):
    in_specs, out_specs, out_shape, scratch = list(in_specs), list(out_specs), list(out_shape), list(scratch)
    if comm is None:
        res = pl.pallas_call(
            body, name=name, out_shape=out_shape, grid=grid, in_specs=in_specs, out_specs=out_specs, scratch_shapes=scratch,
            compiler_params=_cp(*(("arbitrary",) * len(grid))))(*operands)
        return res, []
    n_in, n_out, n_scr = len(in_specs), len(out_specs), len(scratch)
    c_in, c_out = len(comm.ins), len(comm.outs)

    def hosted(*refs):
        own_in, refs = refs[:n_in], refs[n_in:]
        ex_in, refs = refs[:c_in], refs[c_in:]
        own_out, refs = refs[:n_out], refs[n_out:]
        ex_out, refs = refs[:c_out], refs[c_out:]
        own_scr, sems = refs[:n_scr], refs[n_scr:]
        ids = [pl.program_id(a) for a in range(len(grid))]
        first = functools.reduce(jnp.logical_and, [i == 0 for i in ids])
        last = functools.reduce(jnp.logical_and, [i == g - 1 for i, g in zip(ids, grid)])

        @pl.when(first)
        def _():
            comm.start(ex_in, ex_out, sems)

        body(*own_in, *own_out, *own_scr)

        @pl.when(last)
        def _():
            comm.finish(ex_in, ex_out, sems)

    res = pl.pallas_call(
        hosted, name=name, out_shape=out_shape + list(comm.outs), grid=grid,
        in_specs=in_specs + [_ANY] * c_in, out_specs=out_specs + [_ANY] * c_out, scratch_shapes=scratch + list(comm.sems),
        input_output_aliases={n_in + a: n_out + b for a, b in comm.aliases.items()},
        compiler_params=pltpu.CompilerParams(dimension_semantics=("arbitrary",) * len(grid), vmem_limit_bytes=VMEM_LIMIT,
                                             has_side_effects=True),
    )(*operands, *comm.ins)
    return res[:n_out], res[n_out:]


def _sigmoid(x):
    return 1.0 / (1.0 + jnp.exp(-x))


_GC = math.sqrt(2.0 / math.pi)
_GA = 0.044715


def _gelu(x):
    return 0.5 * x * (1.0 + jnp.tanh(_GC * (x + _GA * x * x * x)))


def _gelu_grad(x):
    t = jnp.tanh(_GC * (x + _GA * x * x * x))
    return 0.5 * (1.0 + t) + 0.5 * x * (1.0 - t * t) * _GC * (1.0 + 3.0 * _GA * x * x)


def _matmul(name, prods, M, N, tm, tn, nk, outs, epilogue, extras=(), n_acc=1, chunk=None, comm=None):
    in_specs, operands, metas = [], [], []
    for a, b, mode, acc in prods:
        if mode == "tn":
            tk = a.shape[0] // nk
            in_specs += [pl.BlockSpec((tk, tm), lambda i, j, k: (k, i)), pl.BlockSpec((tk, tn), lambda i, j, k: (k, j))]
        elif mode == "nt":
            tk = a.shape[1] // nk
            in_specs += [pl.BlockSpec((tm, tk), lambda i, j, k: (i, k)), pl.BlockSpec((tn, tk), lambda i, j, k: (j, k))]
        else:
            tk = a.shape[1] // nk
            in_specs += [pl.BlockSpec((tm, tk), lambda i, j, k: (i, k)), pl.BlockSpec((tk, tn), lambda i, j, k: (k, j))]
        operands += [a, b]
        metas.append((mode, acc))
    for arr, bshape, imap in extras:
        in_specs.append(pl.BlockSpec(bshape, functools.partial(lambda i, j, k, f: f(i, j), f=imap)))
        operands.append(arr)
    out_specs = [pl.BlockSpec(bs, functools.partial(lambda i, j, k, f: f(i, j), f=imap)) for _, _, bs, imap in outs]
    out_shape = [jax.ShapeDtypeStruct(s, d) for s, d, _, _ in outs]
    n_prod, n_ext, n_out = len(prods), len(extras), len(outs)

    def body(*refs):
        in_refs = refs[: 2 * n_prod]
        ex_refs = refs[2 * n_prod : 2 * n_prod + n_ext]
        out_refs = refs[2 * n_prod + n_ext : 2 * n_prod + n_ext + n_out]
        acc_refs = refs[2 * n_prod + n_ext + n_out :]

        def partials():
            res = [None] * n_acc
            for idx, (mode, acc) in enumerate(metas):
                a = in_refs[2 * idx][...].astype(bf16)
                b = in_refs[2 * idx + 1][...].astype(bf16)
                d = lax.dot_general(a, b, _DN[mode], preferred_element_type=f32)
                res[acc] = d if res[acc] is None else res[acc] + d
            return res

        def finish(accs):
            vals = epilogue(accs, [r[...] for r in ex_refs])
            for r, v in zip(out_refs, vals):
                r[...] = v.astype(r.dtype)

        if nk == 1:
            step = chunk or tn
            for c0 in range(0, tn, step):
                c1 = min(c0 + step, tn)
                res = [None] * n_acc
                for idx, (mode, acc) in enumerate(metas):
                    a = in_refs[2 * idx][...].astype(bf16)
                    b_ref = in_refs[2 * idx + 1]
                    b = (b_ref[c0:c1, :] if mode == "nt" else b_ref[:, c0:c1]).astype(bf16)
                    d = lax.dot_general(a, b, _DN[mode], preferred_element_type=f32)
                    res[acc] = d if res[acc] is None else res[acc] + d
                for r, v in zip(out_refs, epilogue(res, [r[:, c0:c1] for r in ex_refs])):
                    r[:, c0:c1] = v.astype(r.dtype)
        else:
            k = pl.program_id(2)
            parts = partials()

            @pl.when(k == 0)
            def _():
                for r, d in zip(acc_refs, parts):
                    r[...] = d

            @pl.when(k > 0)
            def _():
                for r, d in zip(acc_refs, parts):
                    r[...] += d

            @pl.when(k == nk - 1)
            def _():
                finish([r[...] for r in acc_refs])

    scratch = [pltpu.VMEM((tm, tn), f32) for _ in range(n_acc)] if nk > 1 else []
    res, ex = _call(body, name=name, grid=(M // tm, N // tn, nk), in_specs=in_specs, out_specs=out_specs,
                    out_shape=out_shape, operands=operands, scratch=scratch, comm=comm)
    return res if comm is None else (res, ex)


def _ident(accs, ex):
    return [accs[0]]


def _mm_simple(name, a, b, mode, out_dtype, tm_pref=1024, tn_pref=1024, tk_pref=1024, comm=None):
    if mode == "tn":
        K, M = a.shape
        N = b.shape[1]
    elif mode == "nt":
        M, K = a.shape
        N = b.shape[0]
    else:
        M, K = a.shape
        N = b.shape[1]
    tm, tn, tk = _tile(M, tm_pref), _tile(N, tn_pref), _tile(K, tk_pref)
    res = _matmul(name, [(a, b, mode, 0)], M, N, tm, tn, K // tk,
                  [((M, N), out_dtype, (tm, tn), lambda i, j: (i, j))], _ident, comm=comm)
    return res[0] if comm is None else (res[0][0], res[1])


def _group(c, width_pref=1024):
    g = LANES // math.gcd(c, LANES)
    while g < N_DEV and 2 * g * c <= width_pref:
        g *= 2
    return g


def _join(parts):
    return parts[0] if len(parts) == 1 else jnp.concatenate(parts, axis=1)


def _mm_nn_cols(name, a, gs_list, outs_dtypes, epilogue, extras=(), tm_pref=512, width_pref=1024, tr=False, comm=None):
    M, K = a.shape
    c = gs_list[0].shape[1 if tr else 2]
    g = _group(c, width_pref)
    W = g * c
    tm = _tile(M, tm_pref, 8)
    n_g, n_ex, n_out = len(gs_list), len(extras), len(outs_dtypes)
    n_cols, n_steps = N_DEV * c, N_DEV // g
    if tr:
        g, c = 1, W
    blk = (g, c, K) if tr else (g, K, c)

    def body(*refs):
        a_ref = refs[0]
        g_refs = refs[1:1 + n_g]
        ex_refs = refs[1 + n_g:1 + n_g + n_ex]
        out_refs = refs[1 + n_g + n_ex:]
        av = a_ref[...].astype(bf16)
        cols = [epilogue([lax.dot_general(av, gr[s], NT if tr else NN, preferred_element_type=f32) for gr in g_refs],
                         [r[:, s * c:(s + 1) * c] for r in ex_refs]) for s in range(g)]
        for n, r in enumerate(out_refs):
            r[...] = _join([cols[s][n].astype(r.dtype) for s in range(g)])

    tile = pl.BlockSpec((tm, W), lambda j, i: (i, j))
    res, ex = _call(
        body, name=name, out_shape=[jax.ShapeDtypeStruct((M, n_cols), d) for d in outs_dtypes],
        grid=(n_steps, M // tm),
        in_specs=[pl.BlockSpec((tm, K), lambda j, i: (i, 0))] + [pl.BlockSpec((None,) + blk, lambda j, i: (j, 0, 0, 0))] * n_g
        + [pl.BlockSpec((tm, W), functools.partial(lambda j, i, off: (i, off + j), off=off)) for _, off in extras],
        out_specs=[tile] * n_out,
        operands=[a, *[gm.reshape((n_steps,) + blk) for gm in gs_list], *[arr for arr, _ in extras]], comm=comm)
    return res if comm is None else (res, ex)


def _mm_nt_cols(name, pairs, out_dtype, tm_pref=1024, tn_pref=1024, width_pref=1024, tr=False, comm=None):
    M = pairs[0][0].shape[0]
    c, Kw = pairs[0][1].shape[1:][::1 if tr else -1]
    g = _group(c, width_pref)
    W = g * c
    tm, tn = _tile(M, tm_pref, 8), _tile(Kw, tn_pref)
    nk = N_DEV // g
    n_p = len(pairs)
    if tr:
        g, c = 1, W

    def body(*refs):
        o_ref, acc = refs[2 * n_p], refs[2 * n_p + 1]
        k = pl.program_id(2)
        tot = None
        for n in range(n_p):
            d_ref, g_ref = refs[2 * n], refs[2 * n + 1]
            for s in range(g):
                part = lax.dot_general(d_ref[:, s * c:(s + 1) * c], g_ref[s], NN if tr else NT, preferred_element_type=f32)
                tot = part if tot is None else tot + part

        @pl.when(k == 0)
        def _():
            acc[...] = tot

        @pl.when(k > 0)
        def _():
            acc[...] += tot

        @pl.when(k == nk - 1)
        def _():
            o_ref[...] = acc[...].astype(o_ref.dtype)

    in_specs, operands = [], []
    for d, gm in pairs:
        if tr:
            wspec, wview = pl.BlockSpec((None, g, c, tn), lambda i, j, k: (k, 0, 0, j)), gm.reshape(nk, g, c, Kw)
        else:
            wspec, wview = pl.BlockSpec((None, g, tn, c), lambda i, j, k: (k, 0, j, 0)), gm.reshape(nk, g, Kw, c)
        in_specs += [pl.BlockSpec((tm, W), lambda i, j, k: (i, k)), wspec]
        operands += [d, wview]
    res, ex = _call(
        body, name=name, out_shape=[jax.ShapeDtypeStruct((M, Kw), out_dtype)], grid=(M // tm, Kw // tn, nk),
        in_specs=in_specs, out_specs=[pl.BlockSpec((tm, tn), lambda i, j, k: (i, j))],
        scratch=[pltpu.VMEM((tm, tn), f32)], operands=operands, comm=comm)
    return res[0] if comm is None else (res[0], ex)


def _mm_tn_cols(name, x, ds, c, tm_pref=1024, tk_pref=1024, width_pref=1024, tr=False, comm=None):
    S, Kw = x.shape
    g = _group(c, width_pref)
    W = g * c
    tm, tk = _tile(Kw, tm_pref), _tile(S, tk_pref, 16)
    nk = S // tk
    n_d = len(ds)
    n_steps, c0 = N_DEV // g, c
    blk = (g, c, tm) if tr else (g, tm, c)
    full = (n_steps, g, c, Kw) if tr else (n_steps, g, Kw, c)

    def body(*refs):
        x_ref = refs[0]
        d_refs = refs[1:1 + n_d]
        o_refs = refs[1 + n_d:1 + 2 * n_d]
        accs = refs[1 + 2 * n_d:]
        k = pl.program_id(2)

        @pl.when(k == 0)
        def _():
            for acc in accs:
                acc[...] = jnp.zeros_like(acc)

        xv = x_ref[...].astype(bf16)
        for d_ref, acc in zip(d_refs, accs):
            for s in range(g):
                ds_ = d_ref[:, s * c:(s + 1) * c]
                acc[s] += lax.dot_general(ds_, xv, TN, preferred_element_type=f32) if tr else \
                    lax.dot_general(xv, ds_, TN, preferred_element_type=f32)

        @pl.when(k == nk - 1)
        def _():
            for o_ref, acc in zip(o_refs, accs):
                o_ref[...] = acc[...].astype(o_ref.dtype)

    out_map = (lambda i, j, k: (j, 0, 0, i)) if tr else (lambda i, j, k: (j, 0, i, 0))
    outs, ex = _call(
        body, name=name, out_shape=[jax.ShapeDtypeStruct(full, bf16)] * n_d, grid=(Kw // tm, n_steps, nk),
        in_specs=[pl.BlockSpec((tk, tm), lambda i, j, k: (k, i))] + [pl.BlockSpec((tk, W), lambda i, j, k: (k, j))] * n_d,
        out_specs=[pl.BlockSpec((None,) + blk, out_map)] * n_d,
        scratch=[pltpu.VMEM(blk, f32)] * n_d, operands=[x, *ds], comm=comm)
    outs = [o.reshape((N_DEV, c0, Kw) if tr else (N_DEV, Kw, c0)) for o in outs]
    return outs if comm is None else (outs, ex)


def _norm_fwd(name, x, g):
    S, D = x.shape
    tm = _tile(S, 512, 8)

    def body(x_ref, g_ref, h_ref):
        xv = x_ref[...]
        r = lax.rsqrt(jnp.mean(xv * xv, axis=-1, keepdims=True) + EPS)
        h_ref[...] = (xv * r * g_ref[...]).astype(bf16)

    return pl.pallas_call(
        body, name=name, out_shape=jax.ShapeDtypeStruct((S, D), bf16), grid=(S // tm,),
        in_specs=[pl.BlockSpec((tm, D), lambda i: (i, 0)), pl.BlockSpec((1, D), lambda i: (0, 0))],
        out_specs=pl.BlockSpec((tm, D), lambda i: (i, 0)), compiler_params=_cp("parallel"),
    )(x, g.reshape(1, D))


def _norm_bwd(name, dh, x, g, dx_in):
    S, D = x.shape
    tm = _tile(S, 256, 8)

    def body(dh_ref, x_ref, g_ref, dxi_ref, dx_ref, dxb_ref, dg_ref):
        i = pl.program_id(0)
        xv = x_ref[...]
        r = lax.rsqrt(jnp.mean(xv * xv, axis=-1, keepdims=True) + EPS)
        xh = xv * r
        dhv = dh_ref[...].astype(f32)
        dxh = dhv * g_ref[...]
        dx = dxi_ref[...] + r * (dxh - xh * jnp.mean(dxh * xh, axis=-1, keepdims=True))
        dx_ref[...] = dx
        dxb_ref[...] = dx.astype(bf16)

        @pl.when(i == 0)
        def _():
            dg_ref[...] = jnp.zeros_like(dg_ref)

        dg_ref[...] += jnp.sum(dhv * xh, axis=0, keepdims=True)

    row = pl.BlockSpec((tm, D), lambda i: (i, 0))
    vec = pl.BlockSpec((1, D), lambda i: (0, 0))
    return pl.pallas_call(
        body, name=name,
        out_shape=[jax.ShapeDtypeStruct((S, D), f32), jax.ShapeDtypeStruct((S, D), bf16), jax.ShapeDtypeStruct((1, D), f32)],
        grid=(S // tm,), in_specs=[row, row, vec, row], out_specs=[row, row, vec], compiler_params=_cp("arbitrary"),
    )(dh, x, g.reshape(1, D), dx_in)


def _loss_head(x, g, t):
    S, D = x.shape
    tm = _tile(S, 256, 8)

    def body(x_ref, g_ref, t_ref, dx_ref, dg_ref, loss_ref):
        i = pl.program_id(0)
        xv = x_ref[...]
        r = lax.rsqrt(jnp.mean(xv * xv, axis=-1, keepdims=True) + EPS)
        xh = xv * r
        gv = g_ref[...]
        err = xh * gv - t_ref[...]
        dy = err * (1.0 / D)
        dxh = dy * gv
        dx_ref[...] = r * (dxh - xh * jnp.mean(dxh * xh, axis=-1, keepdims=True))

        @pl.when(i == 0)
        def _():
            dg_ref[...] = jnp.zeros_like(dg_ref)
            loss_ref[...] = jnp.zeros_like(loss_ref)

        dg_ref[...] += jnp.sum(dy * xh, axis=0, keepdims=True)
        row = jnp.sum(err * err, axis=-1, keepdims=True) * (0.5 / D)
        loss_ref[...] += jnp.broadcast_to(jnp.sum(row, axis=0, keepdims=True), loss_ref.shape)

    return pl.pallas_call(
        body, name="loss_head",
        out_shape=[jax.ShapeDtypeStruct((S, D), f32), jax.ShapeDtypeStruct((1, D), f32), jax.ShapeDtypeStruct((8, LANES), f32)],
        grid=(S // tm,),
        in_specs=[pl.BlockSpec((tm, D), lambda i: (i, 0)), pl.BlockSpec((1, D), lambda i: (0, 0)), pl.BlockSpec((tm, D), lambda i: (i, 0))],
        out_specs=[pl.BlockSpec((tm, D), lambda i: (i, 0)), pl.BlockSpec((1, D), lambda i: (0, 0)), pl.BlockSpec((8, LANES), lambda i: (0, 0))],
        compiler_params=_cp("arbitrary"),
    )(x, g.reshape(1, D), t)


def _perm(t, d):
    if d == 1:
        return t
    S, C = t.shape
    return t.reshape(S // d, d, C).transpose(1, 0, 2).reshape(S, C)


def _rope_tables(S):
    half = HEAD_DIM // 2
    pos = jnp.arange(S, dtype=f32)
    inv_freq = ROPE_THETA ** (-jnp.arange(0, HEAD_DIM, 2, dtype=f32) / HEAD_DIM)
    ang = pos[:, None] * inv_freq[None, :]
    c, s = jnp.cos(ang), jnp.sin(ang)
    cos2 = jnp.concatenate([c, c], axis=-1)
    sin2 = jnp.concatenate([-s, s], axis=-1)
    assert cos2.shape == (S, 2 * half)
    return (jnp.stack([_perm(cos2, d) for d in DILATIONS]), jnp.stack([_perm(sin2, d) for d in DILATIONS]))


def _rope(t, c, s):
    return t * c + pltpu.roll(t, HEAD_DIM // 2, 1) * s


def _rope_bwd(dt, c, s):
    return dt * c - pltpu.roll(dt, HEAD_DIM // 2, 1) * s


def _band_bounds(i, nblk):
    g = pl.program_id(0)
    lb = jnp.right_shift(jnp.int32(nblk), 2 * g)
    pos = lax.rem(i, lb)
    lo = jnp.where(pos == 0, BLK, 0)
    hi = jnp.where(pos == lb - 1, 2 * BLK, 3 * BLK)
    return lo, hi


_RUN = 4


def _cur_spec(width, t=None):
    if t is None:
        return pl.BlockSpec((None, _RUN * BLK, width), lambda g, i: (g, i, 0))
    return pl.BlockSpec((None, None, _RUN * BLK, width), lambda g, i: (t, g, i, 0))


def _band_specs(width, nblk, t=None):
    lo, hi = (lambda i: jnp.maximum(_RUN * i - 1, 0)), (lambda i: jnp.minimum(_RUN * i + _RUN, nblk - 1))
    if t is None:
        return [pl.BlockSpec((None, BLK, width), lambda g, i: (g, lo(i), 0)), _cur_spec(width),
                pl.BlockSpec((None, BLK, width), lambda g, i: (g, hi(i), 0))]
    return [pl.BlockSpec((None, None, BLK, width), lambda g, i: (t, g, lo(i), 0)), _cur_spec(width, t),
            pl.BlockSpec((None, None, BLK, width), lambda g, i: (t, g, hi(i), 0))]


def _band(sub, prev, run, nxt, cols=slice(None)):
    pieces = [(prev, slice(None))] + [(run, slice(n * BLK, (n + 1) * BLK)) for n in range(_RUN)] + [(nxt, slice(None))]
    return jnp.concatenate([ref[rows, cols] for ref, rows in pieces[sub:sub + 3]], axis=0)


_ROWS = 2048


def _to_scratch(scr, val):
    for h in range(HEADS):
        scr[h] = val[:, h * HEAD_DIM:(h + 1) * HEAD_DIM].astype(f32)


def _qkv_groups(z):
    S = z.shape[0]
    R = min(_ROWS, S)
    nb = S // R

    def body(x_ref, o_ref, scr):
        g, i = pl.program_id(1), pl.program_id(2)
        _to_scratch(scr, x_ref[...])
        for gi, d in enumerate(DILATIONS):
            @pl.when(g == gi)
            def _():
                n, L = R // d, S // d
                for r in range(d):
                    start = pl.multiple_of(r * L + i * n, 16)
                    for h in range(HEADS):
                        o_ref[pl.ds(start, n), h * HEAD_DIM:(h + 1) * HEAD_DIM] = scr[h, pl.ds(r, n, stride=d), :].astype(bf16)

    return pl.pallas_call(
        body, name="qkv_groups", out_shape=jax.ShapeDtypeStruct((3, N_GROUPS, S, ATTN_W), bf16), grid=(3, N_GROUPS, nb),
        in_specs=[pl.BlockSpec((R, ATTN_W), lambda t, g, i: (i, t * N_GROUPS + g))],
        out_specs=pl.BlockSpec((None, None, S, ATTN_W), lambda t, g, i: (t, g, 0, 0)),
        scratch_shapes=[pltpu.VMEM((HEADS, R, HEAD_DIM), f32)], compiler_params=_cp("arbitrary", "arbitrary", "arbitrary"),
    )(z)


def _bwd_groups(attn, dattn, lse):
    S = attn.shape[0]
    R = min(_ROWS, S)
    nb = S // R

    def body(a_ref, d_ref, l_ref, o_ref, scr):
        t, g, i = pl.program_id(0), pl.program_id(1), pl.program_id(2)

        @pl.when(t == 0)
        def _():
            _to_scratch(scr, d_ref[...])

        @pl.when(t == 1)
        def _():
            _to_scratch(scr, l_ref[...])

        @pl.when(t == 2)
        def _():
            prod = a_ref[...].astype(f32) * d_ref[...].astype(f32)
            for h in range(HEADS):
                part = jnp.sum(prod[:, h * HEAD_DIM:(h + 1) * HEAD_DIM], axis=-1, keepdims=True)
                scr[h] = jnp.broadcast_to(part, (R, HEAD_DIM))

        for gi, d in enumerate(DILATIONS):
            @pl.when(g == gi)
            def _():
                n, L = R // d, S // d
                for r in range(d):
                    start = pl.multiple_of(r * L + i * n, 8)
                    for h in range(HEADS):
                        o_ref[pl.ds(start, n), h * HEAD_DIM:(h + 1) * HEAD_DIM] = scr[h, pl.ds(r, n, stride=d), :]

    def nat(used):
        return pl.BlockSpec((R, ATTN_W), lambda t, g, i: (jnp.where(used(t), i, 0), 0))

    return pl.pallas_call(
        body, name="bwd_groups", out_shape=jax.ShapeDtypeStruct((3, N_GROUPS, S, ATTN_W), f32), grid=(3, N_GROUPS, nb),
        in_specs=[nat(lambda t: t == 2), nat(lambda t: t != 1), nat(lambda t: t == 1)],
        out_specs=pl.BlockSpec((None, None, S, ATTN_W), lambda t, g, i: (t, g, 0, 0)),
        scratch_shapes=[pltpu.VMEM((HEADS, R, HEAD_DIM), f32)], compiler_params=_cp("arbitrary", "arbitrary", "arbitrary"),
    )(attn, dattn, lse)


def _group_views(t3, R):
    S = t3.shape[1]
    views = [t3.reshape(N_GROUPS, d, S // d, ATTN_W) for d in DILATIONS]
    specs = [pl.BlockSpec((None, d, R // d, ATTN_W), functools.partial(lambda i, g: (g, 0, i, 0), g=g))
             for g, d in enumerate(DILATIONS)]
    return views, specs


def _from_groups(scr, ref, d):
    n = ref.shape[1]
    for r in range(d):
        blk = ref[r]
        for h in range(HEADS):
            scr[h, pl.ds(r, n, stride=d), :] = blk[:, h * HEAD_DIM:(h + 1) * HEAD_DIM].astype(f32)


def _to_natural(ts):
    S = ts[0].shape[1]
    R = min(_ROWS // 2, S)
    views, specs = [], []
    for t3 in ts:
        v, s = _group_views(t3, R)
        views, specs = views + v, specs + s
    n_in, width = len(views), len(ts) * N_GROUPS * ATTN_W

    def body(*refs):
        ins, o_ref, scr = refs[:n_in], refs[n_in], refs[n_in + 1]
        for n, ref in enumerate(ins):
            _from_groups(scr, ref, DILATIONS[n % N_GROUPS])
            for h in range(HEADS):
                o_ref[:, n * ATTN_W + h * HEAD_DIM:n * ATTN_W + (h + 1) * HEAD_DIM] = scr[h].astype(bf16)

    return pl.pallas_call(
        body, name="to_natural", out_shape=jax.ShapeDtypeStruct((S, width), bf16), grid=(S // R,),
        in_specs=specs, out_specs=pl.BlockSpec((R, width), lambda i: (i, 0)),
        scratch_shapes=[pltpu.VMEM((HEADS, R, HEAD_DIM), f32)], compiler_params=_cp("arbitrary"),
    )(*views)


_SCALE = HEAD_DIM ** -0.5


def _attn_fwd(qkv, cos, sin, comm=None):
    _, _, S, W = qkv.shape
    nblk = S // BLK

    def body(q_ref, kp, kc, kn, vp, vc, vn, cq, sq, ckp, ckc, ckn, skp, skc, skn, o_ref, lse_ref):
        i = pl.program_id(1)
        a = lax.broadcasted_iota(jnp.int32, (BLK, 3 * BLK), 0)
        b = lax.broadcasted_iota(jnp.int32, (BLK, 3 * BLK), 1)
        for sub in range(_RUN):
            rows = slice(sub * BLK, (sub + 1) * BLK)
            lo, hi = _band_bounds(_RUN * i + sub, nblk)
            mask = (jnp.abs(b - BLK - a) <= RADIUS) & (b >= lo) & (b < hi)
            ck, sk = _band(sub, ckp, ckc, ckn), _band(sub, skp, skc, skn)
            for hh in range(HEADS):
                sl = slice(hh * HEAD_DIM, (hh + 1) * HEAD_DIM)
                qh = _rope(q_ref[rows, sl].astype(f32), cq[rows, :], sq[rows, :]).astype(bf16)
                kh = _rope(_band(sub, kp, kc, kn, sl).astype(f32), ck, sk).astype(bf16)
                vh = _band(sub, vp, vc, vn, sl)
                s = lax.dot_general(qh, kh, NT, preferred_element_type=f32) * _SCALE
                s = jnp.where(mask, s, NEG)
                m = jnp.max(s, axis=-1, keepdims=True)
                e = jnp.exp(s - m)
                den = jnp.sum(e, axis=-1, keepdims=True)
                o = lax.dot_general(e.astype(bf16), vh, NN, preferred_element_type=f32) * (1.0 / den)
                o_ref[rows, sl] = o.astype(bf16)
                lse_ref[rows, sl] = jnp.broadcast_to(m + jnp.log(den), (BLK, HEAD_DIM))

    blk = _cur_spec(W)
    tab = _cur_spec(HEAD_DIM)
    res, ex = _call(
        body, name="attn_fwd",
        out_shape=[jax.ShapeDtypeStruct((N_GROUPS, S, W), bf16), jax.ShapeDtypeStruct((N_GROUPS, S, W), f32)],
        grid=(N_GROUPS, nblk // _RUN),
        in_specs=[_cur_spec(W, 0)] + _band_specs(W, nblk, 1) + _band_specs(W, nblk, 2) + [tab, tab]
        + _band_specs(HEAD_DIM, nblk) * 2,
        out_specs=[blk, blk], operands=[qkv] * 7 + [cos, sin, cos, cos, cos, sin, sin, sin], comm=comm)
    return res if comm is None else (res, ex)


def _attn_combine(o3, lse3):
    _, S, W = o3.shape
    R = min(_ROWS // 2, S)
    o_views, specs = _group_views(o3, R)
    l_views, _ = _group_views(lse3, R)

    def body(o0, o1, o2, l0, l1, l2, attn_ref, lse_ref, so0, so1, so2, sl0, sl1, sl2):
        for ref, scr, d in zip((o0, o1, o2, l0, l1, l2), (so0, so1, so2, sl0, sl1, sl2), DILATIONS * 2):
            _from_groups(scr, ref, d)
        for h in range(HEADS):
            a0, a1, a2 = sl0[h], sl1[h], sl2[h]
            m = jnp.maximum(jnp.maximum(a0, a1), a2)
            w0, w1, w2 = jnp.exp(a0 - m), jnp.exp(a1 - m), jnp.exp(a2 - m)
            den = w0 + w1 + w2
            acc = w0 * so0[h] + w1 * so1[h] + w2 * so2[h]
            attn_ref[:, h * HEAD_DIM:(h + 1) * HEAD_DIM] = (acc * (1.0 / den)).astype(bf16)
            lse_ref[:, h * HEAD_DIM:(h + 1) * HEAD_DIM] = m + jnp.log(den)

    nat = pl.BlockSpec((R, W), lambda i: (i, 0))
    return pl.pallas_call(
        body, name="attn_combine", out_shape=[jax.ShapeDtypeStruct((S, W), bf16), jax.ShapeDtypeStruct((S, W), f32)],
        grid=(S // R,), in_specs=specs * 2, out_specs=[nat, nat],
        scratch_shapes=[pltpu.VMEM((HEADS, R, HEAD_DIM), f32)] * 6, compiler_params=_cp("arbitrary"),
    )(*o_views, *l_views)


def _attn_bwd_dq(qkv, cos, sin, b3, comm=None):
    _, _, S, W = qkv.shape
    nblk = S // BLK

    def body(q_ref, kp, kc, kn, vp, vc, vn, cq, sq, ckp, ckc, ckn, skp, skc, skn, da_ref, l_ref, dl_ref, dq_ref):
        i = pl.program_id(1)
        a = lax.broadcasted_iota(jnp.int32, (BLK, 3 * BLK), 0)
        b = lax.broadcasted_iota(jnp.int32, (BLK, 3 * BLK), 1)
        for sub in range(_RUN):
            rows = slice(sub * BLK, (sub + 1) * BLK)
            lo, hi = _band_bounds(_RUN * i + sub, nblk)
            mask = (jnp.abs(b - BLK - a) <= RADIUS) & (b >= lo) & (b < hi)
            ck, sk = _band(sub, ckp, ckc, ckn), _band(sub, skp, skc, skn)
            for hh in range(HEADS):
                sl = slice(hh * HEAD_DIM, (hh + 1) * HEAD_DIM)
                qh = _rope(q_ref[rows, sl].astype(f32), cq[rows, :], sq[rows, :]).astype(bf16)
                kh = _rope(_band(sub, kp, kc, kn, sl).astype(f32), ck, sk).astype(bf16)
                vh = _band(sub, vp, vc, vn, sl)
                s = lax.dot_general(qh, kh, NT, preferred_element_type=f32) * _SCALE
                lh = l_ref[rows, sl]
                l3 = jnp.concatenate([lh, lh, lh], axis=1)
                p = jnp.exp(jnp.where(mask, s - l3, NEG))
                dp = lax.dot_general(da_ref[rows, sl].astype(bf16), vh, NT, preferred_element_type=f32)
                dh = dl_ref[rows, sl]
                ds = p * (dp - jnp.concatenate([dh, dh, dh], axis=1))
                dqh = lax.dot_general(ds.astype(bf16), kh, NN, preferred_element_type=f32) * _SCALE
                dq_ref[rows, sl] = _rope_bwd(dqh, cq[rows, :], sq[rows, :]).astype(bf16)

    tab = _cur_spec(HEAD_DIM)
    res, ex = _call(
        body, name="attn_bwd_dq", out_shape=[jax.ShapeDtypeStruct((N_GROUPS, S, W), bf16)], grid=(N_GROUPS, nblk // _RUN),
        in_specs=[_cur_spec(W, 0)] + _band_specs(W, nblk, 1) + _band_specs(W, nblk, 2) + [tab, tab]
        + _band_specs(HEAD_DIM, nblk) * 2 + [_cur_spec(W, 0), _cur_spec(W, 1), _cur_spec(W, 2)],
        out_specs=[_cur_spec(W)], operands=[qkv] * 7 + [cos, sin, cos, cos, cos, sin, sin, sin, b3, b3, b3], comm=comm)
    return res[0] if comm is None else (res[0], ex)


def _attn_bwd_dkv(qkv, cos, sin, b3, comm=None):
    _, _, S, W = qkv.shape
    nblk = S // BLK

    def body(k_ref, v_ref, ck, sk, qp, qc, qn, cqp, cqc, cqn, sqp, sqc, sqn, dap, dac, dan, lp, lc, ln, dlp, dlc, dln,
             dk_ref, dv_ref):
        j = pl.program_id(1)
        a = lax.broadcasted_iota(jnp.int32, (3 * BLK, BLK), 0)
        b = lax.broadcasted_iota(jnp.int32, (3 * BLK, BLK), 1)
        for sub in range(_RUN):
            rows = slice(sub * BLK, (sub + 1) * BLK)
            lo, hi = _band_bounds(_RUN * j + sub, nblk)
            mask = (jnp.abs(b - (a - BLK)) <= RADIUS) & (a >= lo) & (a < hi)
            cq, sq = _band(sub, cqp, cqc, cqn), _band(sub, sqp, sqc, sqn)
            for hh in range(HEADS):
                sl = slice(hh * HEAD_DIM, (hh + 1) * HEAD_DIM)
                kh = _rope(k_ref[rows, sl].astype(f32), ck[rows, :], sk[rows, :]).astype(bf16)
                vh = v_ref[rows, sl]
                qh = _rope(_band(sub, qp, qc, qn, sl).astype(f32), cq, sq).astype(bf16)
                dah = _band(sub, dap, dac, dan, sl).astype(bf16)
                lh = _band(sub, lp, lc, ln, sl)
                dlh = _band(sub, dlp, dlc, dln, sl)
                s = lax.dot_general(qh, kh, NT, preferred_element_type=f32) * _SCALE
                p = jnp.exp(jnp.where(mask, s - lh, NEG))
                dv_ref[rows, sl] = lax.dot_general(p.astype(bf16), dah, TN, preferred_element_type=f32).astype(bf16)
                dp = lax.dot_general(dah, vh, NT, preferred_element_type=f32)
                ds = p * (dp - dlh)
                dkh = lax.dot_general(ds.astype(bf16), qh, TN, preferred_element_type=f32) * _SCALE
                dk_ref[rows, sl] = _rope_bwd(dkh, ck[rows, :], sk[rows, :]).astype(bf16)

    blk, tab, bt = _cur_spec(W), _cur_spec(HEAD_DIM), _band_specs(HEAD_DIM, nblk)
    res, ex = _call(
        body, name="attn_bwd_dkv",
        out_shape=[jax.ShapeDtypeStruct((N_GROUPS, S, W), bf16), jax.ShapeDtypeStruct((N_GROUPS, S, W), bf16)],
        grid=(N_GROUPS, nblk // _RUN),
        in_specs=[_cur_spec(W, 1), _cur_spec(W, 2), tab, tab] + _band_specs(W, nblk, 0) + bt + bt
        + _band_specs(W, nblk, 0) + _band_specs(W, nblk, 1) + _band_specs(W, nblk, 2), out_specs=[blk, blk],
        operands=[qkv, qkv, cos, sin, qkv, qkv, qkv, cos, cos, cos, sin, sin, sin] + [b3] * 9, comm=comm)
    return res if comm is None else (res, ex)


_SG_ROWS = 512


def _sg_z_specs(tm, half):
    o = QKV_W // half
    return [pl.BlockSpec((tm, half), functools.partial(lambda i, c: (i, c), c=o + n)) for n in range(4)]


def _sg_norm(v, lg, lb):
    gv = _gelu(v)
    mu = jnp.mean(gv, axis=-1, keepdims=True)
    xc = gv - mu
    rstd = lax.rsqrt(jnp.mean(xc * xc, axis=-1, keepdims=True) + EPS)
    xh = xc * rstd
    return xh, rstd, xh * lg + lb


def _sg_fwd(z, w, bb, lg, lb):
    S = z.shape[0]
    tm = _tile(S, _SG_ROWS, SG_CHUNK)
    half = SG_W // 2

    def body(u0, u1, v0, v1, w_ref, bb_ref, lg_ref, lb_ref, o_ref):
        u = jnp.concatenate([u0[...], u1[...]], axis=1).astype(f32)
        v = jnp.concatenate([v0[...], v1[...]], axis=1).astype(f32)
        gu = _gelu(u)
        _, _, vn = _sg_norm(v, lg_ref[...], lb_ref[...])
        vnb = vn.astype(bf16)
        for c in range(tm // SG_CHUNK):
            rs = slice(c * SG_CHUNK, (c + 1) * SG_CHUNK)
            for g in range(SG_GROUPS):
                cs = slice(g * 128, (g + 1) * 128)
                mixed = lax.dot_general(w_ref[g], vnb[rs, cs], NN, preferred_element_type=f32) + bb_ref[g]
                o_ref[rs, cs] = (gu[rs, cs] * mixed).astype(bf16)

    full3 = pl.BlockSpec((SG_GROUPS, 128, 128), lambda i: (0, 0, 0))
    vec = pl.BlockSpec((1, SG_W), lambda i: (0, 0))
    return pl.pallas_call(
        body, name="sg_fwd", out_shape=jax.ShapeDtypeStruct((S, SG_W), bf16), grid=(S // tm,),
        in_specs=_sg_z_specs(tm, half) + [full3, full3, vec, vec],
        out_specs=pl.BlockSpec((tm, SG_W), lambda i: (i, 0)), compiler_params=_cp("parallel"),
    )(z, z, z, z, w, bb, lg.reshape(1, SG_W), lb.reshape(1, SG_W))


def _sg_bwd(dsg, z, w, wt, bb, lg, lb):
    S = z.shape[0]
    tm = _tile(S, _SG_ROWS, SG_CHUNK)
    half = SG_W // 2

    def body(d_ref, u0, u1, v0, v1, w_ref, wt_ref, bb_ref, lg_ref, lb_ref, du_ref, dv_ref, dw_ref, db_ref, dlg_ref, dlb_ref, dvn_scr):
        i = pl.program_id(0)

        @pl.when(i == 0)
        def _():
            dw_ref[...] = jnp.zeros_like(dw_ref)
            db_ref[...] = jnp.zeros_like(db_ref)
            dlg_ref[...] = jnp.zeros_like(dlg_ref)
            dlb_ref[...] = jnp.zeros_like(dlb_ref)

        u = jnp.concatenate([u0[...], u1[...]], axis=1).astype(f32)
        v = jnp.concatenate([v0[...], v1[...]], axis=1).astype(f32)
        gu = _gelu(u)
        dgu = _gelu_grad(u)
        xh, rstd, vn = _sg_norm(v, lg_ref[...], lb_ref[...])
        vnb = vn.astype(bf16)
        dsg_v = d_ref[...].astype(f32)
        for g in range(SG_GROUPS):
            cs = slice(g * 128, (g + 1) * 128)
            dw_g = jnp.zeros((128, 128), f32)
            db_g = jnp.zeros((128, 1), f32)
            for c in range(tm // SG_CHUNK):
                rs = slice(c * SG_CHUNK, (c + 1) * SG_CHUNK)
                ds = dsg_v[rs, cs]
                mixed = lax.dot_general(w_ref[g], vnb[rs, cs], NN, preferred_element_type=f32) + bb_ref[g]
                du_ref[rs, cs] = (ds * mixed * dgu[rs, cs]).astype(bf16)
                dmix = ds * gu[rs, cs]
                dmb = dmix.astype(bf16)
                dw_g = dw_g + lax.dot_general(dmb, vnb[rs, cs], NT, preferred_element_type=f32)
                db_g = db_g + jnp.sum(dmix, axis=-1, keepdims=True)
                dvn_scr[rs, cs] = lax.dot_general(wt_ref[g], dmb, NN, preferred_element_type=f32)
            dw_ref[g] += dw_g
            db_ref[g] += jnp.broadcast_to(db_g, (128, 128))
        dvn = dvn_scr[...]
        dlg_ref[...] += jnp.sum(dvn * xh, axis=0, keepdims=True)
        dlb_ref[...] += jnp.sum(dvn, axis=0, keepdims=True)
        dxh = dvn * lg_ref[...]
        dgv = rstd * (dxh - jnp.mean(dxh, axis=-1, keepdims=True) - xh * jnp.mean(dxh * xh, axis=-1, keepdims=True))
        dv_ref[...] = (dgv * _gelu_grad(v)).astype(bf16)

    full3 = pl.BlockSpec((SG_GROUPS, 128, 128), lambda i: (0, 0, 0))
    vec = pl.BlockSpec((1, SG_W), lambda i: (0, 0))
    row = pl.BlockSpec((tm, SG_W), lambda i: (i, 0))
    return pl.pallas_call(
        body, name="sg_bwd",
        out_shape=[jax.ShapeDtypeStruct((S, SG_W), bf16), jax.ShapeDtypeStruct((S, SG_W), bf16),
                   jax.ShapeDtypeStruct((SG_GROUPS, 128, 128), f32), jax.ShapeDtypeStruct((SG_GROUPS, 128, 128), f32),
                   jax.ShapeDtypeStruct((1, SG_W), f32), jax.ShapeDtypeStruct((1, SG_W), f32)],
        grid=(S // tm,),
        in_specs=[row] + _sg_z_specs(tm, half) + [full3, full3, full3, vec, vec],
        out_specs=[row, row, full3, full3, vec, vec],
        scratch_shapes=[pltpu.VMEM((tm, SG_W), f32)], compiler_params=_cp("arbitrary"),
    )(dsg, z, z, z, z, w, wt, bb, lg.reshape(1, SG_W), lb.reshape(1, SG_W))


def _ple_bwd_ew(dx, gp, e):
    S, D = dx.shape
    tm, tc = _tile(S, 512, 8), _tile(D, 1024)

    def body(dx_ref, gp_ref, e_ref, dgp_ref, de_ref):
        dxv = dx_ref[...]
        sg = _sigmoid(gp_ref[...].astype(f32))
        dgp_ref[...] = (dxv * e_ref[...].astype(f32) * sg * (1.0 - sg)).astype(bf16)
        de_ref[...] = (dxv * sg).astype(bf16)

    blk = pl.BlockSpec((tm, tc), lambda i, j: (i, j))
    return pl.pallas_call(
        body, name="ple_bwd_ew", out_shape=[jax.ShapeDtypeStruct((S, D), bf16)] * 2, grid=(S // tm, D // tc),
        in_specs=[blk, blk, blk], out_specs=[blk, blk], compiler_params=_cp("parallel", "parallel"),
    )(dx, gp, e)


def _adam_math(w, g, m, v):
    m = ADAM_B1 * m + (1.0 - ADAM_B1) * g
    v = ADAM_B2 * v + (1.0 - ADAM_B2) * (g * g)
    m_hat = m / (1.0 - ADAM_B1 ** ADAM_STEP)
    v_hat = v / (1.0 - ADAM_B2 ** ADAM_STEP)
    delta = -ADAM_LR * (m_hat / (jnp.sqrt(v_hat) + ADAM_EPS) + ADAM_WD * w)
    return delta, m, v


def _small_sum_adamw(gathered, w, m, v):
    _, R, _ = gathered.shape
    tr = _tile(R, 1024, SMALL_ROWS)

    def body(p_ref, w_ref, m_ref, v_ref, g_ref, d_ref, nm_ref, nv_ref):
        g = p_ref[0]
        for n in range(1, N_DEV):
            g = g + p_ref[n]
        d, nm, nv = _adam_math(w_ref[...], g, m_ref[...], v_ref[...])
        g_ref[...] = g
        d_ref[...] = d
        nm_ref[...] = nm
        nv_ref[...] = nv

    blk = pl.BlockSpec((tr, LANES), lambda i: (i, 0))
    return pl.pallas_call(
        body, name="small_sum_adamw", out_shape=[jax.ShapeDtypeStruct((R, LANES), f32)] * 4, grid=(R // tr,),
        in_specs=[pl.BlockSpec((N_DEV, tr, LANES), lambda i: (0, i, 0)), blk, blk, blk], out_specs=[blk] * 4,
        compiler_params=_cp("parallel"),
    )(gathered, w, m, v)


def _all_gather(name, shard, in_vmem=False):
    R, C = shard.shape

    def body(x_ref, out_ref, send_sems, recv_sems, local_sem):
        x, y, c = lax.axis_index("x"), lax.axis_index("y"), lax.axis_index("c")
        me, sibling = (x, y, c), (x, y, 1 - c)
        chips = [(1 - x, y), (x, 1 - y), (1 - x, 1 - y)]

        def rows(px, py, pc):
            return out_ref.at[4 * px + 2 * py + pc]

        def copy(k, block, to, src=None):
            return pltpu.make_async_remote_copy(
                src_ref=rows(*block) if src is None else src, dst_ref=rows(*block),
                send_sem=send_sems.at[k], recv_sem=recv_sems.at[k], device_id=to, device_id_type=MESH)

        mine = pltpu.make_async_copy(x_ref, rows(*me), local_sem)
        mine.start()
        first = [copy(0, me, sibling, src=x_ref)]
        first += [copy(1 + j, me, (*chip, c), src=x_ref) for j, chip in enumerate(chips)]
        for cp in first:
            cp.start()
        passed = [copy(4 + j, (*chip, c), sibling) for j, chip in enumerate(chips)]
        for j, chip in enumerate(chips):
            copy(1 + j, (*chip, c), me).wait_recv()
            passed[j].start()
        copy(0, sibling, me).wait_recv()
        for j, chip in enumerate(chips):
            copy(4 + j, (*chip, 1 - c), me).wait_recv()
        for cp in first + passed:
            cp.wait_send()
        mine.wait()

    space = pl.BlockSpec(memory_space=pltpu.VMEM) if in_vmem else _ANY
    return pl.pallas_call(
        body, name=name, out_shape=jax.ShapeDtypeStruct((N_DEV, R, C), shard.dtype),
        in_specs=[space], out_specs=space,
        scratch_shapes=[pltpu.SemaphoreType.DMA((7,)), pltpu.SemaphoreType.DMA((7,)), pltpu.SemaphoreType.DMA],
        compiler_params=pltpu.CompilerParams(has_side_effects=True, vmem_limit_bytes=VMEM_LIMIT),
    )(shard)


class _Exchange:
    def __init__(self, ins, outs, sems, start, finish, aliases=None):
        self.ins, self.outs, self.sems, self.start, self.finish = list(ins), list(outs), list(sems), start, finish
        self.aliases = dict(aliases or {})


def _run_exchange(name, ex):
    c_in, c_out = len(ex.ins), len(ex.outs)

    def body(*refs):
        ins, outs, sems = refs[:c_in], refs[c_in:c_in + c_out], refs[c_in + c_out:]
        ex.start(ins, outs, sems)
        ex.finish(ins, outs, sems)

    return pl.pallas_call(
        body, name=name, out_shape=ex.outs, in_specs=[_ANY] * c_in, out_specs=[_ANY] * c_out, scratch_shapes=ex.sems,
        input_output_aliases=ex.aliases,
        compiler_params=pltpu.CompilerParams(has_side_effects=True, vmem_limit_bytes=VMEM_LIMIT),
    )(*ex.ins)


def _both(e1, e2):
    i1, o1, s1 = len(e1.ins), len(e1.outs), len(e1.sems)

    def start(ins, outs, sems):
        e1.start(ins[:i1], outs[:o1], sems[:s1])
        e2.start(ins[i1:], outs[o1:], sems[s1:])

    def finish(ins, outs, sems):
        e1.finish(ins[:i1], outs[:o1], sems[:s1])
        e2.finish(ins[i1:], outs[o1:], sems[s1:])

    aliases = dict(e1.aliases)
    aliases.update({i1 + a: o1 + b for a, b in e2.aliases.items()})
    return _Exchange(e1.ins + e2.ins, e1.outs + e2.outs, e1.sems + e2.sems, start, finish, aliases)


def _gather_exchange(shards, parts=None, into=None):
    n = len(shards)
    parts = parts or [None] * n
    into = into or [None] * n
    given = [w for w in range(n) if into[w] is not None]

    def plan(ins, outs, sems):
        send_sems, recv_sems, local_sems = sems
        x, y, c = lax.axis_index("x"), lax.axis_index("y"), lax.axis_index("c")
        me, sibling = (x, y, c), (x, y, 1 - c)
        chips = [(1 - x, y), (x, 1 - y), (1 - x, 1 - y)]

        def cut(ref, w):
            return ref if parts[w] is None else ref.at[pl.ds(parts[w][0], parts[w][1])]

        def rows(w, px, py, pc):
            return cut(outs[w].at[4 * px + 2 * py + pc], w)

        def copy(w, k, block, to, src=None):
            return pltpu.make_async_remote_copy(
                src_ref=rows(w, *block) if src is None else src, dst_ref=rows(w, *block),
                send_sem=send_sems.at[w, k], recv_sem=recv_sems.at[w, k], device_id=to, device_id_type=MESH)

        mine = [pltpu.make_async_copy(cut(ins[w], w), rows(w, *me), local_sems.at[w]) for w in range(n)]
        first = []
        for w in range(n):
            first.append(copy(w, 0, me, sibling, src=cut(ins[w], w)))
            first += [copy(w, 1 + j, me, (*chip, c), src=cut(ins[w], w)) for j, chip in enumerate(chips)]
        return c, me, sibling, chips, copy, mine, first

    def start(ins, outs, sems):
        _, _, _, _, _, mine, first = plan(ins, outs, sems)
        for cp in mine + first:
            cp.start()

    def finish(ins, outs, sems):
        c, me, sibling, chips, copy, mine, first = plan(ins, outs, sems)
        passed = []
        for w in range(n):
            for j, chip in enumerate(chips):
                copy(w, 1 + j, (*chip, c), me).wait_recv()
                passed.append(copy(w, 4 + j, (*chip, c), sibling))
                passed[-1].start()
        for w in range(n):
            copy(w, 0, sibling, me).wait_recv()
            for j, chip in enumerate(chips):
                copy(w, 4 + j, (*chip, 1 - c), me).wait_recv()
        for cp in first + passed:
            cp.wait_send()
        for cp in mine:
            cp.wait()

    return _Exchange(
        list(shards) + [into[w] for w in given], [jax.ShapeDtypeStruct((N_DEV,) + s.shape, s.dtype) for s in shards],
        [pltpu.SemaphoreType.DMA((n, 7)), pltpu.SemaphoreType.DMA((n, 7)), pltpu.SemaphoreType.DMA((n,))], start, finish,
        {n + k: w for k, w in enumerate(given)})


def _sibling_exchange(gs):
    n = len(gs)

    def copies(ins, outs, sems):
        send_sems, recv_sems = sems
        x, y, c = lax.axis_index("x"), lax.axis_index("y"), lax.axis_index("c")
        return [pltpu.make_async_remote_copy(
            src_ref=ins[w].at[2 * q + (1 - c)], dst_ref=outs[w].at[q], send_sem=send_sems.at[w, q],
            recv_sem=recv_sems.at[w, q], device_id=(x, y, 1 - c), device_id_type=MESH) for w in range(n) for q in range(4)]

    def start(ins, outs, sems):
        for cp in copies(ins, outs, sems):
            cp.start()

    def finish(ins, outs, sems):
        cps = copies(ins, outs, sems)
        for cp in cps:
            cp.wait_recv()
        for cp in cps:
            cp.wait_send()

    return _Exchange(gs, [jax.ShapeDtypeStruct((4,) + g.shape[1:], g.dtype) for g in gs],
                     [pltpu.SemaphoreType.DMA((n, 4)), pltpu.SemaphoreType.DMA((n, 4))], start, finish)


def _rs_chip_sum(name, g8, recv, c_idx):
    _, R, C = g8.shape
    tr = _tile(R, 512, 16)
    g42 = g8.reshape(4, 2, R, C)

    def body(c_ref, a_ref, b_ref, o_ref):
        o_ref[...] = (a_ref[...].astype(f32) + b_ref[...].astype(f32)).astype(o_ref.dtype)

    return pl.pallas_call(
        body, name=name, out_shape=jax.ShapeDtypeStruct((4, R, C), g8.dtype),
        grid_spec=pltpu.PrefetchScalarGridSpec(
            num_scalar_prefetch=1, grid=(4, R // tr),
            in_specs=[pl.BlockSpec((None, None, tr, C), lambda q, r, c_ref: (q, c_ref[0], r, 0)),
                      pl.BlockSpec((None, tr, C), lambda q, r, c_ref: (q, r, 0))],
            out_specs=pl.BlockSpec((None, tr, C), lambda q, r, c_ref: (q, r, 0))),
        compiler_params=_cp("parallel", "parallel"),
    )(c_idx, g42, recv)


def _chips_exchange(p4s):
    n = len(p4s)

    def copies(ins, outs, sems):
        send_sems, recv_sems = sems
        x, y, c = lax.axis_index("x"), lax.axis_index("y"), lax.axis_index("c")
        chips = [(1 - x, y), (x, 1 - y), (1 - x, 1 - y)]
        return [pltpu.make_async_remote_copy(
            src_ref=ins[w].at[2 * cx + cy], dst_ref=outs[w].at[k], send_sem=send_sems.at[w, k],
            recv_sem=recv_sems.at[w, k], device_id=(cx, cy, c), device_id_type=MESH)
            for w in range(n) for k, (cx, cy) in enumerate(chips)]

    def start(ins, outs, sems):
        for cp in copies(ins, outs, sems):
            cp.start()

    def finish(ins, outs, sems):
        cps = copies(ins, outs, sems)
        for cp in cps:
            cp.wait_recv()
        for cp in cps:
            cp.wait_send()

    return _Exchange(p4s, [jax.ShapeDtypeStruct((3,) + p.shape[1:], p.dtype) for p in p4s],
                     [pltpu.SemaphoreType.DMA((n, 3)), pltpu.SemaphoreType.DMA((n, 3))], start, finish)


def _adamw_layer(name, layer, w, m, v, p4, recv, q_idx, prev):
    depth, R, C = w.shape
    tr = _tile(R, 256, 8)

    def body(q_ref, w_ref, m_ref, v_ref, a_ref, b_ref, *rest):
        g_ref, d_ref, nm_ref, nv_ref = rest[-4:]
        g = ((a_ref[...].astype(f32) + b_ref[0].astype(f32)) + b_ref[1].astype(f32)) + b_ref[2].astype(f32)
        d, nm, nv = _adam_math(w_ref[...], g, m_ref[...], v_ref[...])
        g_ref[...] = g
        d_ref[...] = d
        nm_ref[...] = nm
        nv_ref[...] = nv

    lay = pl.BlockSpec((None, tr, C), lambda i, q_ref: (layer, i, 0))
    n_prev = 0 if prev is None else 4
    return pl.pallas_call(
        body, name=name, out_shape=[jax.ShapeDtypeStruct((depth, R, C), f32)] * 4,
        grid_spec=pltpu.PrefetchScalarGridSpec(
            num_scalar_prefetch=1, grid=(R // tr,),
            in_specs=[lay, lay, lay, pl.BlockSpec((None, tr, C), lambda i, q_ref: (q_ref[0], i, 0)),
                      pl.BlockSpec((3, tr, C), lambda i, q_ref: (0, i, 0))] + [_ANY] * n_prev,
            out_specs=[lay] * 4),
        input_output_aliases={6 + n: n for n in range(n_prev)},
        compiler_params=_cp("parallel"),
    )(q_idx, w, m, v, p4, recv, *(prev or ()))


_BIG = (("w_in", 1), ("w_br_attn", 1), ("w_br_sg", 1), ("w_out", 0), ("w_ff_gate", 1), ("w_ff_up", 1),
        ("w_ff_down", 0), ("w_ple_gate", 0), ("w_ple", 1))


_TURNED = ("w_in", "w_ff_gate", "w_ff_up")


def _gather_plan(i, depth):
    mixer = ["w_br_attn", "w_br_sg", "w_out"]
    plan = {
        "mm_in": [(i, "w_ff_gate")] + [(i, n) for n in mixer],
        "attn_fwd": [(i, "w_ff_up")],
        "mm_ffn_in": [(i, "w_ff_down")],
    }
    last = [(i, "w_ple_gate"), (i, "w_ple")]
    if i + 1 < depth:
        plan["mm_out"] = last
        plan["mm_ffn_in"] = plan["mm_ffn_in"] + [(i + 1, "w_in", 0)]
        plan["mm_ffn_out"] = [(i + 1, "w_in", 1)]
    else:
        plan["mm_ffn_out"] = last
    return plan


def _layer_fwd(x0, p_i, layer, arrived, sm, tabs, comm):
    S, D = x0.shape
    cos, sin = tabs
    tmm = _tile(S, 1024, 8)
    same = lambda accs, ex: accs

    def W(name):
        return arrived[(layer, name)]

    def hosted(key, fn):
        if key not in comm:
            return fn(None)
        ex, keys = comm[key]()
        res, outs = fn(ex)
        arrived.update(zip(keys, outs))
        return res

    h1 = _norm_fwd("norm_fwd", x0, sm["norm_mix"])
    z = hosted("mm_in", lambda ex: _mm_nn_cols("mm_in", h1, [W("w_in")], [bf16], same, tr=True, comm=ex))[0]
    IN = z.shape[1]

    qkv = _qkv_groups(z)
    o3, lse3 = hosted("attn_fwd", lambda ex: _attn_fwd(qkv, cos, sin, comm=ex))
    attn, lse = _attn_combine(o3, lse3)
    ya = _mm_nn_cols("mm_br_attn", attn, [W("w_br_attn")], [bf16], same, tm_pref=1024)[0]
    sgw = sm["sg_w"].astype(bf16)
    bb = jnp.broadcast_to(sm["sg_b"][:, :, None], (SG_GROUPS, SG_CHUNK, 128))
    sg = _sg_fwd(z, sgw, bb, sm["sg_ln_g"], sm["sg_ln_b"])
    wide = _tile(D, 512)

    def merge_ep(accs, ex):
        yb_t = accs[0]
        ga, gb = _sigmoid(ex[0].astype(f32)), _sigmoid(ex[1].astype(f32))
        return [yb_t, ga * ex[2].astype(f32) + gb * yb_t]

    yb, merged = _mm_nn_cols("mm_br_sg", sg, [W("w_br_sg")], [bf16, bf16], merge_ep, tm_pref=1024, width_pref=wide,
                             extras=[(z, (QKV_W + 2 * SG_W) // wide), (z, (QKV_W + 2 * SG_W + D) // wide), (ya, 0)])
    tn = _tile(D, 1024)
    def out_ep(accs, ex_tiles):
        x1_t = ex_tiles[0] + accs[0]
        r = lax.rsqrt(jnp.mean(x1_t * x1_t, axis=-1, keepdims=True) + EPS)
        return [x1_t, x1_t * r * ex_tiles[1]]

    rows = _tile(S, 512, 8)
    x1, h2 = hosted("mm_out", lambda ex: _matmul(
        "mm_out", [(merged, W("w_out").reshape(D, D), "nn", 0)], S, D, rows, D, 1,
        [((S, D), f32, (rows, D), lambda i, j: (i, j)), ((S, D), bf16, (rows, D), lambda i, j: (i, j))], out_ep,
        extras=[(x0, (rows, D), lambda i, j: (i, j)), (sm["norm_ffn"].reshape(1, D), (1, D), lambda i, j: (0, j))], comm=ex))

    def ffn_ep(accs, ex):
        a, b = accs
        sg = _sigmoid(a)
        silu = a * sg
        return [b * sg * (1.0 + a * (1.0 - sg)), silu, silu * b]

    dfa, dfb, f = hosted("mm_ffn_in", lambda ex: _mm_nn_cols("mm_ffn_in", h2, [W("w_ff_gate"), W("w_ff_up")], [bf16] * 3, ffn_ep,
                                                         tr=True, comm=ex))
    w_down = W("w_ff_down").reshape(-1, D)
    F = w_down.shape[0]
    thin = _tile(D, 512)
    x2 = hosted("mm_ffn_out", lambda ex: _matmul(
        "mm_ffn_out", [(f, w_down, "nn", 0)], S, D, tmm, thin, 1, [((S, D), f32, (tmm, thin), lambda i, j: (i, j))],
        lambda accs, ex_tiles: [ex_tiles[0] + accs[0]], extras=[(x1, (tmm, thin), lambda i, j: (i, j))], comm=ex))[0]
    h3 = _norm_fwd("norm_fwd", x2, sm["norm_ple"])

    e = _mm_nn_cols("mm_ple_emb", p_i, [W("w_ple")], [bf16], same, tm_pref=1024)[0]

    def ple_ep(accs, ex):
        gp = accs[0]
        return [ex[0] + _sigmoid(gp) * ex[1].astype(f32), gp]

    x3, gp = hosted("mm_ple", lambda ex: _matmul(
        "mm_ple", [(h3, W("w_ple_gate").reshape(D, D), "nn", 0)], S, D, tmm, tn, 1,
        [((S, D), f32, (tmm, tn), lambda i, j: (i, j)), ((S, D), bf16, (tmm, tn), lambda i, j: (i, j))],
        ple_ep, extras=[(x2, (tmm, tn), lambda i, j: (i, j)), (e, (tmm, tn), lambda i, j: (i, j))], chunk=512, comm=ex))
    saved = dict(x0=x0, h1=h1, z=z, qkv=qkv, attn=attn, lse=lse, ya=ya, yb=yb, sg=sg, merged=merged, x1=x1,
                 h2=h2, dfa=dfa, dfb=dfb, f=f, x2=x2, h3=h3, gp=gp, e=e, sgw=sgw, bb=bb, IN=IN)
    return x3, saved


def _layer_bwd(dx3, p_i, W, sm, tabs, sv, c_idx):
    S, D = dx3.shape
    w_out, w_down, w_pg = W["w_out"].reshape(D, D), W["w_ff_down"].reshape(-1, D), W["w_ple_gate"].reshape(D, D)
    F = w_down.shape[0]
    cos, sin = tabs
    tmm = _tile(S, 1024, 8)
    tn = _tile(D, 512)
    reduced = {}

    def to_sibling(grads):
        return _sibling_exchange(list(grads.values()))

    def chip_sums(grads, recv):
        return {n: _rs_chip_sum("rs_chip_sum_" + n, grads[n], r, c_idx) for n, r in zip(grads, recv)}

    def to_owners(p4, names=None):
        return _chips_exchange([p4[n] for n in (names or p4)])

    def carry(p4, outs, names=None):
        reduced.update({n: (p4[n], r) for n, r in zip(names or p4, outs)})

    def blocks(full):
        return full.reshape(N_DEV, full.shape[0] // N_DEV, full.shape[1])

    dgp, de = _ple_bwd_ew(dx3, sv["gp"], sv["e"])
    d_w_ple = _mm_tn_cols("mm_dw_ple", p_i, [de], D // N_DEV)[0]
    d_w_pg = blocks(_mm_simple("mm_dw_dd", sv["h3"], dgp, "tn", bf16, 1024, 1024, 2048))
    dh3 = _mm_simple("mm_dh_dd", dgp, w_pg, "nt", bf16, 1024, 1024, 2048)
    dx2, dx2b, dg_ple = _norm_bwd("norm_bwd", dh3, sv["x2"], sm["norm_ple"], dx3)
    tf = _tile(F, 1408)

    def ffn_bwd_ep(accs, ex):
        df = accs[0]
        return [df * ex[0].astype(f32), df * ex[1].astype(f32)]

    th = _tile(S, 512, 8)
    da, db = _matmul("mm_dffn", [(dx2b, w_down, "nt", 0)], S, F, th, tf, 1,
                     [((S, F), bf16, (th, tf), lambda i, j: (i, j))] * 2, ffn_bwd_ep,
                     extras=[(sv["dfa"], (th, tf), lambda i, j: (i, j)), (sv["dfb"], (th, tf), lambda i, j: (i, j))], chunk=512)
    d_w_down = blocks(_mm_simple("mm_dw_fd", sv["f"], dx2b, "tn", bf16, 1408, 1024, 2048))
    g_a = dict(w_ff_down=d_w_down, w_ple_gate=d_w_pg, w_ple=d_w_ple)
    (d_w_gate, d_w_up), recv = _mm_tn_cols("mm_dw_df", sv["h2"], [da, db], F // N_DEV, tr=True, comm=to_sibling(g_a))
    p4_a = chip_sums(g_a, recv)
    g_b = dict(w_ff_gate=d_w_gate, w_ff_up=d_w_up)
    dh2, outs = _mm_nt_cols("mm_dh_ffn", [(da, W["w_ff_gate"]), (db, W["w_ff_up"])], bf16, tr=True,
                            comm=_both(to_owners(p4_a), to_sibling(g_b)))
    carry(p4_a, outs[:len(p4_a)])
    p4_b = chip_sums(g_b, outs[len(p4_a):])
    dx1, dx1b, dg_ffn = _norm_bwd("norm_bwd", dh2, sv["x1"], sm["norm_ffn"], dx2)
    z = sv["z"]
    o_a, o_b = (QKV_W + 2 * SG_W) // tn, (QKV_W + 2 * SG_W + D) // tn

    def merge_bwd_ep(accs, ex):
        dm = accs[0]
        ga, gb = _sigmoid(ex[0].astype(f32)), _sigmoid(ex[1].astype(f32))
        ya, yb = ex[2].astype(f32), ex[3].astype(f32)
        return [dm * ya * ga * (1.0 - ga), dm * yb * gb * (1.0 - gb), dm * ga, dm * gb]

    dga, dgb, dya, dyb = _matmul(
        "mm_dmerge", [(dx1b, w_out, "nt", 0)], S, D, tmm, tn, 1,
        [((S, D), bf16, (tmm, tn), lambda i, j: (i, j))] * 4, merge_bwd_ep,
        extras=[(z, (tmm, tn), lambda i, j: (i, o_a + j)), (z, (tmm, tn), lambda i, j: (i, o_b + j)),
                (sv["ya"], (tmm, tn), lambda i, j: (i, j)), (sv["yb"], (tmm, tn), lambda i, j: (i, j))], chunk=256)
    d_w_out = blocks(_mm_simple("mm_dw_dd", sv["merged"], dx1b, "tn", bf16, 1024, 1024, 2048))
    dsg = _mm_nt_cols("mm_dsg", [(dyb, W["w_br_sg"])], bf16)
    d_w_bsg = _mm_tn_cols("mm_dw_bsg", sv["sg"], [dyb], D // N_DEV)[0]
    dattn = _mm_nt_cols("mm_dattn", [(dya, W["w_br_attn"])], bf16)
    d_w_battn = _mm_tn_cols("mm_dw_battn", sv["attn"], [dya], D // N_DEV)[0]
    sgwt = jnp.swapaxes(sm["sg_w"], 1, 2).astype(bf16)
    du, dv_sg, d_sgw, d_sgb, d_lg, d_lb = _sg_bwd(dsg, z, sv["sgw"], sgwt, sv["bb"], sm["sg_ln_g"], sm["sg_ln_b"])
    b3 = _bwd_groups(sv["attn"], dattn, sv["lse"])
    g_c = dict(w_out=d_w_out, w_br_sg=d_w_bsg, w_br_attn=d_w_battn)
    dqg, outs = _attn_bwd_dq(sv["qkv"], cos, sin, b3, comm=_both(to_owners(p4_b, ["w_ff_gate"]), to_sibling(g_c)))
    carry(p4_b, outs[:1], ["w_ff_gate"])
    p4_c = chip_sums(g_c, outs[1:])
    (dkg, dvg), outs = _attn_bwd_dkv(sv["qkv"], cos, sin, b3, comm=to_owners(p4_b, ["w_ff_up"]))
    carry(p4_b, outs, ["w_ff_up"])
    dz = jnp.concatenate([_to_natural([dqg, dkg, dvg]), du, dv_sg, dga, dgb], axis=1)
    (d_w_in,), outs = _mm_tn_cols("mm_dw_in", sv["h1"], [dz], sv["IN"] // N_DEV, tr=True, comm=to_owners(p4_c))
    carry(p4_c, outs)
    g_d = dict(w_in=d_w_in)
    p4_d = chip_sums(g_d, _run_exchange("rs_sibling_w_in", to_sibling(g_d)))
    dh1, outs = _mm_nt_cols("mm_dh_in", [(dz, W["w_in"])], bf16, tr=True, comm=to_owners(p4_d))
    carry(p4_d, outs)
    dx0, _, dg_mix = _norm_bwd("norm_bwd", dh1, sv["x0"], sm["norm_mix"], dx1)
    small = dict(sg_w=d_sgw, sg_b=d_sgb[:, :, 0], sg_ln_g=d_lg[0], sg_ln_b=d_lb[0], norm_mix=dg_mix[0], norm_ffn=dg_ffn[0],
                 norm_ple=dg_ple[0])
    return dx0, reduced, small


_SMALL = ("sg_w", "sg_b", "sg_ln_g", "sg_ln_b", "norm_mix", "norm_ffn", "norm_ple", "norm_final")


SMALL_ROWS = 256


def _pack_small(parts, tail):
    rows = [parts[n].astype(f32).reshape(-1, LANES) for n in _SMALL] + [tail]
    n = sum(r.shape[0] for r in rows)
    return jnp.concatenate(rows + [jnp.zeros((-n % SMALL_ROWS, LANES), f32)], axis=0)


def kernel(x, p, w_in, w_br_attn, w_br_sg, w_out, sg_w, sg_b, sg_ln_g, sg_ln_b, norm_mix, norm_ffn, norm_ple, norm_final, w_ff_gate, w_ff_up, w_ff_down, w_ple_gate, w_ple, loss_target, m_w_in, m_w_br_attn, m_w_br_sg, m_w_out, m_sg_w, m_sg_b, m_sg_ln_g, m_sg_ln_b, m_norm_mix, m_norm_ffn, m_norm_ple, m_norm_final, m_w_ff_gate, m_w_ff_up, m_w_ff_down, m_w_ple_gate, m_w_ple, v_w_in, v_w_br_attn, v_w_br_sg, v_w_out, v_sg_w, v_sg_b, v_sg_ln_g, v_sg_ln_b, v_norm_mix, v_norm_ffn, v_norm_ple, v_norm_final, v_w_ff_gate, v_w_ff_up, v_w_ff_down, v_w_ple_gate, v_w_ple):
    wts = dict(w_in=w_in, w_br_attn=w_br_attn, w_br_sg=w_br_sg, w_out=w_out, w_ff_gate=w_ff_gate, w_ff_up=w_ff_up,
               w_ff_down=w_ff_down, w_ple_gate=w_ple_gate, w_ple=w_ple)
    mom_m = dict(w_in=m_w_in, w_br_attn=m_w_br_attn, w_br_sg=m_w_br_sg, w_out=m_w_out, w_ff_gate=m_w_ff_gate,
                 w_ff_up=m_w_ff_up, w_ff_down=m_w_ff_down, w_ple_gate=m_w_ple_gate, w_ple=m_w_ple)
    mom_v = dict(w_in=v_w_in, w_br_attn=v_w_br_attn, w_br_sg=v_w_br_sg, w_out=v_w_out, w_ff_gate=v_w_ff_gate,
                 w_ff_up=v_w_ff_up, w_ff_down=v_w_ff_down, w_ple_gate=v_w_ple_gate, w_ple=v_w_ple)
    small_w = dict(sg_w=sg_w, sg_b=sg_b, sg_ln_g=sg_ln_g, sg_ln_b=sg_ln_b, norm_mix=norm_mix, norm_ffn=norm_ffn,
                   norm_ple=norm_ple, norm_final=norm_final)
    small_m = dict(sg_w=m_sg_w, sg_b=m_sg_b, sg_ln_g=m_sg_ln_g, sg_ln_b=m_sg_ln_b, norm_mix=m_norm_mix, norm_ffn=m_norm_ffn,
                   norm_ple=m_norm_ple, norm_final=m_norm_final)
    small_v = dict(sg_w=v_sg_w, sg_b=v_sg_b, sg_ln_g=v_sg_ln_g, sg_ln_b=v_sg_ln_b, norm_mix=v_norm_mix, norm_ffn=v_norm_ffn,
                   norm_ple=v_norm_ple, norm_final=v_norm_final)
    depth = w_in.shape[0]
    S = x.shape[1]
    names = [n for n, _ in _BIG]
    c_idx = lax.axis_index("c").astype(jnp.int32).reshape(1)
    q_idx = (2 * lax.axis_index("x") + lax.axis_index("y")).astype(jnp.int32).reshape(1)
    tabs = _rope_tables(S)

    def turned(n, t):
        return jnp.swapaxes(t, -1, -2) if n in _TURNED else t

    arrived = {}

    def gather(keys):
        shards, parts, into = [], [], []
        for key in keys:
            s = turned(key[1], wts[key[1]][key[0]]).astype(bf16)
            shards.append(s)
            half = s.shape[0] // 2
            parts.append((key[2] * half, half) if len(key) == 3 else None)
            into.append(arrived.get(key[:2]) if len(key) == 3 else None)
        return _gather_exchange(shards, parts, into), [key[:2] for key in keys]

    ex, keys = gather([(0, "w_in")])
    arrived.update(zip(keys, _run_exchange("ag_w_in", ex)))

    xs = x[0]
    saved = []
    for i in range(depth):
        sm = {n: small_w[n][i] for n in _SMALL if n != "norm_final"}
        comm = {carrier: functools.partial(gather, keys) for carrier, keys in _gather_plan(i, depth).items()}
        xs, sv = _layer_fwd(xs, p[i, 0], i, arrived, sm, tabs, comm)
        saved.append(sv)
    dx, dg_final, loss_part = _loss_head(xs, norm_final, loss_target[0])

    reduced = [None] * depth
    small_parts = [None] * depth
    for i in reversed(range(depth)):
        sm = {n: small_w[n][i] for n in _SMALL if n != "norm_final"}
        dx, reduced[i], small_parts[i] = _layer_bwd(dx, p[i, 0], {n: arrived[(i, n)] for n in names}, sm, tabs, saved[i], c_idx)
    grad_x = dx[None]

    parts = {n: jnp.stack([small_parts[i][n] for i in range(depth)]) for n in _SMALL if n != "norm_final"}
    parts["norm_final"] = dg_final[0]
    gathered = _all_gather("ag_small", _pack_small(parts, loss_part), in_vmem=True)
    g_s, d_s, nm_s, nv_s = _small_sum_adamw(gathered, _pack_small(small_w, jnp.zeros((8, LANES), f32)),
                                            _pack_small(small_m, jnp.zeros((8, LANES), f32)),
                                            _pack_small(small_v, jnp.ones((8, LANES), f32)))
    loss = g_s[sum(small_w[n].size for n in _SMALL) // LANES, 0]

    def unpack_small(flat):
        out, off = {}, 0
        for n in _SMALL:
            k = small_w[n].size // LANES
            out[n] = flat[off:off + k].reshape(small_w[n].shape)
            off += k
        return out

    sm_g, sm_d, sm_nm, sm_nv = unpack_small(g_s), unpack_small(d_s), unpack_small(nm_s), unpack_small(nv_s)

    big_g, big_d, big_nm, big_nv = {}, {}, {}, {}
    for k, n in enumerate(names):
        outs = None
        for i in range(depth):
            p4, recv2 = reduced[i][n]
            outs = _adamw_layer(f"adamw_{n}_{i}", i, turned(n, wts[n]), turned(n, mom_m[n]), turned(n, mom_v[n]), p4, recv2,
                                q_idx, outs)
        big_g[n], big_d[n], big_nm[n], big_nv[n] = [turned(n, o) for o in outs]

    order = ["w_in", "w_br_attn", "w_br_sg", "w_out", "sg_w", "sg_b", "sg_ln_g", "sg_ln_b", "norm_mix", "norm_ffn", "norm_ple",
             "norm_final", "w_ff_gate", "w_ff_up", "w_ff_down", "w_ple_gate", "w_ple"]

    def pick(big, small):
        return [big[n] if n in big else small[n] for n in order]

    return (loss, grad_x, *pick(big_g, sm_g), *pick(big_d, sm_d), *pick(big_nm, sm_nm), *pick(big_nv, sm_nv))
```
